```python
import jax, jax.numpy as jnp
from jax import lax
import numpy as np

D_MODEL = 1024
BATCH = 8
SEQ = 4096
DEPTH = 1

ROPE_THETA = 500000.0
BLOCK = 128
NEG = -1e30
RMS_EPS = 1e-6
LN_EPS = 1e-5

MLA_HEADS = 8
MLA_NOPE_DIM = 64
MLA_ROPE_DIM = 32
MLA_V_DIM = 64
Q_LORA_RANK = 384
KV_LORA_RANK = 256
MLA_WIDTH = MLA_HEADS * MLA_V_DIM

DIL_HEADS = 8
DIL_HEAD_DIM = 64
DIL_ROT_DIM = DIL_HEAD_DIM // 4
DIL_WIDTH = DIL_HEADS * DIL_HEAD_DIM
DIL_CONFIGS = ((128, 1), (512, 4), (2048, 16))

MIX_WIDTH = MLA_WIDTH + DIL_WIDTH
IN_SPLITS = (Q_LORA_RANK, KV_LORA_RANK, MLA_ROPE_DIM, MLA_WIDTH, DIL_WIDTH, DIL_WIDTH, DIL_WIDTH, DIL_WIDTH)
IN_WIDTH = sum(IN_SPLITS)

DEEPNORM_ALPHA = (2.0 * DEPTH) ** 0.25
DEEPNORM_BETA = (8.0 * DEPTH) ** -0.25

kernel_name = "hybrid_mla_dilated_deepnorm"


def rmsnorm(t, g):
    tf = t.astype(jnp.float32)
    tf = tf * lax.rsqrt(jnp.mean(tf * tf, axis=-1, keepdims=True) + RMS_EPS)
    return (tf * g.astype(jnp.float32)).astype(t.dtype)


def layernorm(t, g, b):
    tf = t.astype(jnp.float32)
    mu = jnp.mean(tf, axis=-1, keepdims=True)
    var = jnp.mean(jnp.square(tf - mu), axis=-1, keepdims=True)
    return ((tf - mu) * lax.rsqrt(var + LN_EPS) * g.astype(jnp.float32) + b.astype(jnp.float32)).astype(t.dtype)


def rope_tables(seq_len, dim):
    inv_freq = ROPE_THETA ** (-jnp.arange(0, dim, 2, dtype=jnp.float32) / dim)
    ang = jnp.arange(seq_len, dtype=jnp.float32)[:, None] * inv_freq[None, :]
    return jnp.cos(ang), jnp.sin(ang)


def apply_rope(t, cos, sin):
    t1, t2 = jnp.split(t.astype(jnp.float32), 2, axis=-1)
    c, s = cos[:, None, :], sin[:, None, :]
    return jnp.concatenate([t1 * c - t2 * s, t1 * s + t2 * c], axis=-1).astype(t.dtype)


def mla_attention(c_q, c_kv, k_rope, q_norm_g, kv_norm_g, w_uq, w_ukv):
    B, S, _ = c_q.shape
    H, DN, DR, DV = MLA_HEADS, MLA_NOPE_DIM, MLA_ROPE_DIM, MLA_V_DIM
    cos, sin = rope_tables(S, DR)
    q = (rmsnorm(c_q, q_norm_g) @ w_uq).reshape(B, S, H, DN + DR)
    q = jnp.concatenate([q[..., :DN], apply_rope(q[..., DN:], cos, sin)], axis=-1)
    kv = (rmsnorm(c_kv, kv_norm_g) @ w_ukv).reshape(B, S, H, DN + DV)
    k_nope, v = kv[..., :DN], kv[..., DN:]
    k_pe = apply_rope(k_rope[:, :, None, :], cos, sin)
    k = jnp.concatenate([k_nope, jnp.broadcast_to(k_pe, (B, S, H, DR))], axis=-1)
    scale = (DN + DR) ** -0.5
    nblk = S // BLOCK
    qb = q.reshape(B, nblk, BLOCK, H, DN + DR).transpose(1, 0, 3, 2, 4)
    kpos = jnp.arange(S)

    def one_block(args):
        q_blk, i = args
        s = jnp.einsum('bhqd,bkhd->bhqk', q_blk, k).astype(jnp.float32) * scale
        qpos = i * BLOCK + jnp.arange(BLOCK)
        s = jnp.where(kpos[None, :] <= qpos[:, None], s, NEG)
        p = jax.nn.softmax(s, axis=-1)
        return jnp.einsum('bhqk,bkhd->bqhd', p.astype(v.dtype), v)

    out = lax.map(one_block, (qb, jnp.arange(nblk)))
    return out.transpose(1, 0, 2, 3, 4).reshape(B, S, H * DV)


def dilated_branch(q, k, v, window, dilation):
    B, S, H, D = q.shape
    n_back = window // dilation
    seg = dilation * BLOCK
    S_pad = -(-S // seg) * seg
    pad = ((0, 0), (0, S_pad - S), (0, 0), (0, 0))
    L = S_pad // dilation
    nb = L // BLOCK

    def to_sub(t):
        t = jnp.pad(t, pad).reshape(B, L, dilation, H, D).transpose(0, 2, 3, 1, 4)
        return t.reshape(B, dilation, H, nb, BLOCK, D)

    def with_prev(t):
        prev = jnp.pad(t, ((0, 0), (0, 0), (0, 0), (1, 0), (0, 0), (0, 0)))[:, :, :, :-1]
        return jnp.concatenate([prev, t], axis=4)

    qs = to_sub(q)
    ks = with_prev(to_sub(k))
    vs = with_prev(to_sub(v))
    s = jnp.einsum('bdhnqe,bdhnke->bdhnqk', qs, ks).astype(jnp.float32)
    q_loc = jnp.arange(BLOCK)
    k_loc = jnp.arange(2 * BLOCK) - BLOCK
    dist = q_loc[:, None] - k_loc[None, :]
    valid = (jnp.arange(nb)[:, None, None] * BLOCK + k_loc[None, None, :]) >= 0
    mask = (dist >= 0) & (dist <= n_back) & valid
    s = jnp.where(mask, s, NEG)
    m = jnp.max(s, axis=-1, keepdims=True)
    p = jnp.exp(s - m)
    l = jnp.sum(p, axis=-1, keepdims=True)
    num = jnp.einsum('bdhnqk,bdhnke->bdhnqe', p, vs.astype(jnp.float32))

    def to_seq(t):
        c = t.shape[-1]
        t = t.reshape(B, dilation, H, L, c).transpose(0, 3, 1, 2, 4).reshape(B, S_pad, H, c)
        return t[:, :S]

    return to_seq(num), to_seq(m), to_seq(l)


def dilated_attention(q, k, v):
    B, S, _ = q.shape
    H, D = DIL_HEADS, DIL_HEAD_DIM
    cos, sin = rope_tables(S, DIL_ROT_DIM)

    def heads_rope(t):
        t = t.reshape(B, S, H, D)
        return jnp.concatenate([apply_rope(t[..., :DIL_ROT_DIM], cos, sin), t[..., DIL_ROT_DIM:]], axis=-1)

    qh = heads_rope(q) * (D ** -0.5)
    kh = heads_rope(k)
    vh = v.reshape(B, S, H, D)
    parts = [dilated_branch(qh, kh, vh, w, d) for (w, d) in DIL_CONFIGS]
    m_all = jnp.max(jnp.stack([pm for (_, pm, _) in parts], axis=0), axis=0)
    num = jnp.zeros((B, S, H, D), jnp.float32)
    den = jnp.zeros((B, S, H, 1), jnp.float32)
    for (pn, pm, pl) in parts:
        w = jnp.exp(pm - m_all)
        num = num + w * pn
        den = den + w * pl
    return (num / den).astype(q.dtype).reshape(B, S, H * D)


def _fwd_setup_inputs(seed: int = 0) -> dict:
    key = jax.random.key(seed)
    ks = jax.random.split(key, 9)
    f32 = jnp.float32
    x = jax.random.normal(ks[0], (BATCH, SEQ, D_MODEL), f32)
    w_in = jax.random.normal(ks[1], (D_MODEL, IN_WIDTH), f32) * D_MODEL ** -0.5
    q_norm_g = 1.0 + 0.02 * jax.random.normal(ks[2], (Q_LORA_RANK,), f32)
    kv_norm_g = 1.0 + 0.02 * jax.random.normal(ks[3], (KV_LORA_RANK,), f32)
    w_uq = jax.random.normal(ks[4], (Q_LORA_RANK, MLA_HEADS * (MLA_NOPE_DIM + MLA_ROPE_DIM)), f32) * Q_LORA_RANK ** -0.5
    w_ukv = jax.random.normal(ks[5], (KV_LORA_RANK, MLA_HEADS * (MLA_NOPE_DIM + MLA_V_DIM)), f32) * KV_LORA_RANK ** -0.5
    w_out = jax.random.normal(ks[6], (MIX_WIDTH, D_MODEL), f32) * (MIX_WIDTH ** -0.5) * DEEPNORM_BETA
    ln_g = 1.0 + 0.02 * jax.random.normal(ks[7], (D_MODEL,), f32)
    ln_b = 0.02 * jax.random.normal(ks[8], (D_MODEL,), f32)
    return {"x": x, "w_in": w_in, "q_norm_g": q_norm_g, "kv_norm_g": kv_norm_g, "w_uq": w_uq,
            "w_ukv": w_ukv, "w_out": w_out, "ln_g": ln_g, "ln_b": ln_b}


def _fwd_reference(x, w_in, q_norm_g, kv_norm_g, w_uq, w_ukv, w_out, ln_g, ln_b):
    offs = np.cumsum(IN_SPLITS)[:-1].tolist()
    for _ in range(DEPTH):
        h = x @ w_in
        c_q, c_kv, k_rope, g_a, q_b, k_b, v_b, g_b = jnp.split(h, offs, axis=-1)
        y_a = mla_attention(c_q, c_kv, k_rope, q_norm_g, kv_norm_g, w_uq, w_ukv) * jax.nn.silu(g_a)
        y_b = dilated_attention(q_b, k_b, v_b) * jax.nn.silu(g_b)
        mix = jnp.concatenate([y_a, y_b], axis=-1)
        x = layernorm(DEEPNORM_ALPHA * x + mix @ w_out, ln_g, ln_b)
    return x


import jax as _jax
import jax.numpy as _jnp

TWIN_FORMAT = 'train_step'
FWD_PARAMS = ['x', 'w_in', 'q_norm_g', 'kv_norm_g', 'w_uq', 'w_ukv', 'w_out', 'ln_g', 'ln_b']
TWIN_WEIGHTS = ['w_in', 'q_norm_g', 'kv_norm_g', 'w_uq', 'w_ukv', 'w_out', 'ln_g', 'ln_b']
TWIN_DIFF_INPUT = 'x'
TWIN_INPUTS = ['x', 'w_in', 'q_norm_g', 'kv_norm_g', 'w_uq', 'w_ukv', 'w_out', 'ln_g', 'ln_b', 'loss_target', 'm_w_in', 'm_q_norm_g', 'm_kv_norm_g', 'm_w_uq', 'm_w_ukv', 'm_w_out', 'm_ln_g', 'm_ln_b', 'v_w_in', 'v_q_norm_g', 'v_kv_norm_g', 'v_w_uq', 'v_w_ukv', 'v_w_out', 'v_ln_g', 'v_ln_b']
TWIN_OUTPUTS = ['loss', 'grad_x', 'grad_w_in', 'grad_q_norm_g', 'grad_kv_norm_g', 'grad_w_uq', 'grad_w_ukv', 'grad_w_out', 'grad_ln_g', 'grad_ln_b', 'delta_w_in', 'delta_q_norm_g', 'delta_kv_norm_g', 'delta_w_uq', 'delta_w_ukv', 'delta_w_out', 'delta_ln_g', 'delta_ln_b', 'new_m_w_in', 'new_m_q_norm_g', 'new_m_kv_norm_g', 'new_m_w_uq', 'new_m_w_ukv', 'new_m_w_out', 'new_m_ln_g', 'new_m_ln_b', 'new_v_w_in', 'new_v_q_norm_g', 'new_v_kv_norm_g', 'new_v_w_uq', 'new_v_w_ukv', 'new_v_w_out', 'new_v_ln_g', 'new_v_ln_b']
TWIN_LEAF_KINDS = {'loss': 'loss', 'grad_x': 'grad_x', 'grad_w_in': 'grad_w', 'grad_q_norm_g': 'grad_w', 'grad_kv_norm_g': 'grad_w', 'grad_w_uq': 'grad_w', 'grad_w_ukv': 'grad_w', 'grad_w_out': 'grad_w', 'grad_ln_g': 'grad_w', 'grad_ln_b': 'grad_w', 'delta_w_in': 'delta_w', 'delta_q_norm_g': 'delta_w', 'delta_kv_norm_g': 'delta_w', 'delta_w_uq': 'delta_w', 'delta_w_ukv': 'delta_w', 'delta_w_out': 'delta_w', 'delta_ln_g': 'delta_w', 'delta_ln_b': 'delta_w', 'new_m_w_in': 'new_m', 'new_m_q_norm_g': 'new_m', 'new_m_kv_norm_g': 'new_m', 'new_m_w_uq': 'new_m', 'new_m_w_ukv': 'new_m', 'new_m_w_out': 'new_m', 'new_m_ln_g': 'new_m', 'new_m_ln_b': 'new_m', 'new_v_w_in': 'new_v', 'new_v_q_norm_g': 'new_v', 'new_v_kv_norm_g': 'new_v', 'new_v_w_uq': 'new_v', 'new_v_w_ukv': 'new_v', 'new_v_w_out': 'new_v', 'new_v_ln_g': 'new_v', 'new_v_ln_b': 'new_v'}


def _forward(args):
    return _fwd_reference(*[args[k] for k in FWD_PARAMS])


def _output_shape():
    def fwd():
        inp = _fwd_setup_inputs(0)
        return _fwd_reference(*[inp[k] for k in FWD_PARAMS])
    out = _jax.eval_shape(fwd)
    return out.shape, out.dtype

N_MICROBATCH = 1
ADAM_LR = 0.001
ADAM_B1 = 0.9
ADAM_B2 = 0.999
ADAM_EPS = 1e-08
ADAM_WD = 0.01
ADAM_STEP = 10
PER_EXAMPLE_BATCH_AXIS = {'x': 0, 'loss_target': 0}
SHARED_INPUTS = []
_WEIGHT_DTYPES = {'w_in': _jnp.float32, 'q_norm_g': _jnp.float32, 'kv_norm_g': _jnp.float32, 'w_uq': _jnp.float32, 'w_ukv': _jnp.float32, 'w_out': _jnp.float32, 'ln_g': _jnp.float32, 'ln_b': _jnp.float32}
MOMENT_SCALE = {'w_in': 1.223814e-02, 'q_norm_g': 1.040658e-02, 'kv_norm_g': 2.003934e-02, 'w_uq': 7.761262e-03, 'w_ukv': 9.832748e-03, 'w_out': 2.004059e-02, 'ln_g': 3.197395e+01, 'ln_b': 5.235488e-01}


def _to_microbatches(a, axis):
    t = _jnp.moveaxis(a, axis, 0)
    t = t.reshape((N_MICROBATCH, t.shape[0] // N_MICROBATCH) + t.shape[1:])
    return _jnp.moveaxis(t, 1, axis + 1)


def setup_inputs(seed: int = 0) -> dict:
    inp = _fwd_setup_inputs(seed)
    key = _jax.random.fold_in(_jax.random.key(seed), 7919)
    shape, _ = _output_shape()
    out = dict(inp)
    out["loss_target"] = _jax.random.normal(_jax.random.fold_in(key, 0), shape, _jnp.float32)
    for i, name in enumerate(TWIN_WEIGHTS):
        w = inp[name].astype(_jnp.float32)
        if MOMENT_SCALE is None:
            s = _jnp.sqrt(_jnp.mean(_jnp.square(w)) + 1e-30)
        else:
            s = MOMENT_SCALE[name]
        km, kv = _jax.random.split(_jax.random.fold_in(key, i + 1))
        out[name] = w
        out["m_" + name] = s * _jax.random.normal(km, w.shape, _jnp.float32)
        out["v_" + name] = (s * s) * _jax.random.uniform(kv, w.shape, _jnp.float32, 0.5, 1.5)
    if N_MICROBATCH > 1:
        for name, axis in PER_EXAMPLE_BATCH_AXIS.items():
            out[name] = _to_microbatches(out[name], axis)
    return {'x': out['x'], 'w_in': out['w_in'], 'q_norm_g': out['q_norm_g'], 'kv_norm_g': out['kv_norm_g'], 'w_uq': out['w_uq'], 'w_ukv': out['w_ukv'], 'w_out': out['w_out'], 'ln_g': out['ln_g'], 'ln_b': out['ln_b'], 'loss_target': out['loss_target'], 'm_w_in': out['m_w_in'], 'm_q_norm_g': out['m_q_norm_g'], 'm_kv_norm_g': out['m_kv_norm_g'], 'm_w_uq': out['m_w_uq'], 'm_w_ukv': out['m_w_ukv'], 'm_w_out': out['m_w_out'], 'm_ln_g': out['m_ln_g'], 'm_ln_b': out['m_ln_b'], 'v_w_in': out['v_w_in'], 'v_q_norm_g': out['v_q_norm_g'], 'v_kv_norm_g': out['v_kv_norm_g'], 'v_w_uq': out['v_w_uq'], 'v_w_ukv': out['v_w_ukv'], 'v_w_out': out['v_w_out'], 'v_ln_g': out['v_ln_g'], 'v_ln_b': out['v_ln_b']}


def _loss(weights, diff, rest, loss_target):
    with _jax.named_scope("forward"):
        args = {**rest, TWIN_DIFF_INPUT: diff, **{k: w.astype(_WEIGHT_DTYPES[k]) for k, w in weights.items()}}
        y = _forward(args)
    with _jax.named_scope("loss_head"):
        err = _jnp.square(y.astype(_jnp.float32) - loss_target)
        return 0.5 * _jnp.sum(_jnp.mean(err, axis=-1)) if err.ndim else 0.5 * err


def _adamw(w, g, m, v):
    m = ADAM_B1 * m + (1.0 - ADAM_B1) * g
    v = ADAM_B2 * v + (1.0 - ADAM_B2) * _jnp.square(g)
    m_hat = m / (1.0 - ADAM_B1 ** ADAM_STEP)
    v_hat = v / (1.0 - ADAM_B2 ** ADAM_STEP)
    delta = -ADAM_LR * (m_hat / (_jnp.sqrt(v_hat) + ADAM_EPS) + ADAM_WD * w)
    return delta, m, v


def reference(x, w_in, q_norm_g, kv_norm_g, w_uq, w_ukv, w_out, ln_g, ln_b, loss_target, m_w_in, m_q_norm_g, m_kv_norm_g, m_w_uq, m_w_ukv, m_w_out, m_ln_g, m_ln_b, v_w_in, v_q_norm_g, v_kv_norm_g, v_w_uq, v_w_ukv, v_w_out, v_ln_g, v_ln_b):
    given = dict(x=x, w_in=w_in, q_norm_g=q_norm_g, kv_norm_g=kv_norm_g, w_uq=w_uq, w_ukv=w_ukv, w_out=w_out, ln_g=ln_g, ln_b=ln_b, loss_target=loss_target, m_w_in=m_w_in, m_q_norm_g=m_q_norm_g, m_kv_norm_g=m_kv_norm_g, m_w_uq=m_w_uq, m_w_ukv=m_w_ukv, m_w_out=m_w_out, m_ln_g=m_ln_g, m_ln_b=m_ln_b, v_w_in=v_w_in, v_q_norm_g=v_q_norm_g, v_kv_norm_g=v_kv_norm_g, v_w_uq=v_w_uq, v_w_ukv=v_w_ukv, v_w_out=v_w_out, v_ln_g=v_ln_g, v_ln_b=v_ln_b)
    weights = {n: given[n] for n in TWIN_WEIGHTS}
    shared = {n: given[n] for n in SHARED_INPUTS}
    per_example = {n: given[n] for n in ['x']}
    grad_fn = _jax.value_and_grad(_loss, argnums=(0, 1))

    def one_microbatch(ex, loss_target):
        ex = dict(ex)
        diff = ex.pop(TWIN_DIFF_INPUT)
        return grad_fn(weights, diff, {**shared, **ex}, loss_target)

    if N_MICROBATCH == 1:
        loss, (grad_w, grad_x) = one_microbatch(per_example, given["loss_target"])
    else:
        def body(carry, xs):
            loss_sum, grad_sum = carry
            l_k, (gw_k, gx_k) = one_microbatch(xs[0], xs[1])
            with _jax.named_scope("update"):
                return (loss_sum + l_k, _jax.tree.map(_jnp.add, grad_sum, gw_k)), gx_k

        init = (_jnp.zeros((), _jnp.float32), _jax.tree.map(_jnp.zeros_like, weights))
        (loss, grad_w), grad_x = _jax.lax.scan(body, init, (per_example, given["loss_target"]))
    with _jax.named_scope("update"):
        delta_w, new_m, new_v = {}, {}, {}
        for n in TWIN_WEIGHTS:
            delta_w[n], new_m[n], new_v[n] = _adamw(weights[n], grad_w[n], given["m_" + n], given["v_" + n])
    return (loss, grad_x, *[grad_w[n] for n in TWIN_WEIGHTS], *[delta_w[n] for n in TWIN_WEIGHTS],
            *[new_m[n] for n in TWIN_WEIGHTS], *[new_v[n] for n in TWIN_WEIGHTS])
```

```python
import functools

import jax
import jax.numpy as jnp
from jax import lax
from jax.experimental import pallas as pl
from jax.experimental.pallas import tpu as pltpu

F32 = jnp.float32
BF16 = jnp.bfloat16

D_MODEL = 1024
ROPE_THETA = 500000.0
BLOCK = 128
NEG = -1e30
RMS_EPS = 1e-6
LN_EPS = 1e-5

MLA_HEADS = 8
MLA_NOPE = 64
MLA_ROPE = 32
Q_LORA = 384
KV_LORA = 256
DIL_HEADS = 8
DIL_HEAD_DIM = 64
DIL_ROT = 16
DIL_DILATIONS = (1, 4, 16)
IN_WIDTH = 3232
IN_WIDTH_PAD = 3328
MLA_SCALE = (MLA_NOPE + MLA_ROPE) ** -0.5
DIL_SCALE = DIL_HEAD_DIM ** -0.5
ALPHA = 2.0 ** 0.25

ADAM_LR = 0.001
ADAM_B1 = 0.9
ADAM_B2 = 0.999
ADAM_EPS = 1e-08
ADAM_WD = 0.01
ADAM_STEP = 10

N_SHARD = 4
SHARD_SIZES = (1024 * 808, 384 * 192, 256 * 256, 256 * 1024)
PACK_ROWS = 1216
HALF_ROWS = PACK_ROWS // 2
LANES = 128
VMEM_LIMIT = 56 * 1024 * 1024
MESH = pl.DeviceIdType.MESH

NT = (((1,), (1,)), ((), ()))
TN = (((0,), (0,)), ((), ()))


def _cp(sem=None, vmem=None):
    return pltpu.CompilerParams(dimension_semantics=sem, vmem_limit_bytes=vmem)


def _dot(a, b, dims=None):
    if dims is None:
        return jnp.dot(a, b, preferred_element_type=F32)
    return lax.dot_general(a, b, dims, preferred_element_type=F32)


def _rope_tables(seq):
    pos = jnp.arange(seq, dtype=F32)[:, None]
    inv = ROPE_THETA ** (-jnp.arange(0, MLA_ROPE, 2, dtype=F32) / MLA_ROPE)
    ang = pos * inv[None, :]
    cos, sin = jnp.cos(ang), jnp.sin(ang)
    one, zero = jnp.ones((seq, 64), F32), jnp.zeros((seq, 64), F32)
    ct = jnp.concatenate([one, cos, cos, zero[:, :32]], axis=1)
    st = jnp.concatenate([zero, -sin, sin, zero[:, :32]], axis=1)
    inv = ROPE_THETA ** (-jnp.arange(0, DIL_ROT, 2, dtype=F32) / DIL_ROT)
    ang = pos * inv[None, :]
    cos, sin = jnp.cos(ang), jnp.sin(ang)
    cd = jnp.concatenate([cos, cos, one[:, :48]], axis=1)
    sd = jnp.concatenate([-sin, sin, zero[:, :48]], axis=1)
    return ct, st, jnp.tile(cd, (1, 2)), jnp.tile(sd, (1, 2))


def _pack_shard(w_in, w_uq, w_ukv, w_out, dtype):
    flat = jnp.concatenate([w_in.reshape(-1), w_uq.reshape(-1), w_ukv.reshape(-1), w_out.reshape(-1)])
    flat = jnp.pad(flat.astype(dtype), (0, PACK_ROWS * D_MODEL - flat.shape[0]))
    return flat.reshape(PACK_ROWS, D_MODEL)


def _unpack_shard(packed):
    flat = packed.reshape(-1)
    o1, o2, o3, o4 = (sum(SHARD_SIZES[:i + 1]) for i in range(4))
    return (flat[:o1].reshape(1024, 808), flat[o1:o2].reshape(384, 192),
            flat[o2:o3].reshape(256, 256), flat[o3:o4].reshape(256, 1024))


def _permute_w_in(w):
    z = jnp.zeros((w.shape[0], 64), w.dtype)
    return jnp.concatenate([w[:, 0:640], w[:, 672:1184], w[:, 2720:3232], w[:, 1184:2720],
                            z, w[:, 640:672], z[:, :32]], axis=1)


def _unpermute_dw_in(dw):
    return jnp.concatenate([dw[:, 0:640], dw[:, 3264:3296], dw[:, 640:1152], dw[:, 1664:3200],
                            dw[:, 1152:1664]], axis=1)


def _position():
    return lax.axis_index("x"), lax.axis_index("y"), lax.axis_index("c")


def _all_gather_weights(packed):
    def body(src, out, send_sems, recv_sems, local_sem):
        x, y, c = _position()
        chips = [(1 - x, y), (x, 1 - y), (1 - x, 1 - y)]
        half = pl.ds(c * HALF_ROWS, HALF_ROWS)
        other = pl.ds((1 - c) * HALF_ROWS, HALF_ROWS)

        def slot(px, py):
            return 2 * px + py

        mine = pltpu.make_async_copy(src, out.at[slot(x, y)], local_sem)
        mine.start()

        def copy(k, src_ref, px, py, rows, to):
            return pltpu.make_async_remote_copy(
                src_ref=src_ref, dst_ref=out.at[slot(px, py), rows], send_sem=send_sems.at[k],
                recv_sem=recv_sems.at[k], device_id=to, device_id_type=MESH)

        first = [copy(k, src.at[half], x, y, half, (px, py, c)) for k, (px, py) in enumerate(chips)]
        for cp in first:
            cp.start()
        passed = [copy(3 + k, out.at[slot(px, py), half], px, py, half, (x, y, 1 - c))
                  for k, (px, py) in enumerate(chips)]
        for k, (px, py) in enumerate(chips):
            copy(k, src.at[half], px, py, half, (x, y, c)).wait_recv()
            passed[k].start()
        for k, (px, py) in enumerate(chips):
            copy(3 + k, src.at[half], px, py, other, (x, y, c)).wait_recv()
        for cp in first + passed:
            cp.wait_send()
        mine.wait()

    return pl.pallas_call(
        body, name="all_gather_weights",
        out_shape=jax.ShapeDtypeStruct((N_SHARD,) + packed.shape, packed.dtype),
        in_specs=[pl.BlockSpec(memory_space=pl.ANY)], out_specs=pl.BlockSpec(memory_space=pl.ANY),
        scratch_shapes=[pltpu.SemaphoreType.DMA((6,)), pltpu.SemaphoreType.DMA((6,)), pltpu.SemaphoreType.DMA],
    )(packed)


def _exchange_with_sibling(grads, small):
    def body(g, sm, got, own, smalls, send_sems, recv_sems, local_sems):
        x, y, c = _position()
        half = pl.ds(c * HALF_ROWS, HALF_ROWS)
        other = pl.ds((1 - c) * HALF_ROWS, HALF_ROWS)
        me = 4 * x + 2 * y + c
        keep = pltpu.make_async_copy(g.at[:, half], own, local_sems.at[0])
        keep.start()
        mine = pltpu.make_async_copy(sm, smalls.at[me], local_sems.at[1])
        mine.start()
        swap = pltpu.make_async_remote_copy(
            src_ref=g.at[:, other], dst_ref=got, send_sem=send_sems.at[0], recv_sem=recv_sems.at[0],
            device_id=(x, y, 1 - c), device_id_type=MESH)
        swap.start()
        sends = []
        for rel in range(1, 8):
            dx, dy, dc = rel // 4, (rel // 2) % 2, rel % 2
            px = 1 - x if dx else x
            py = 1 - y if dy else y
            pc = 1 - c if dc else c
            cp = pltpu.make_async_remote_copy(
                src_ref=sm, dst_ref=smalls.at[me], send_sem=send_sems.at[rel], recv_sem=recv_sems.at[rel],
                device_id=(px, py, pc), device_id_type=MESH)
            cp.start()
            sends.append((cp, 4 * px + 2 * py + pc))
        swap.wait_recv()
        for rel, (cp, peer) in enumerate(sends, start=1):
            pltpu.make_async_remote_copy(
                src_ref=sm, dst_ref=smalls.at[peer], send_sem=send_sems.at[rel], recv_sem=recv_sems.at[rel],
                device_id=(x, y, c), device_id_type=MESH).wait_recv()
        swap.wait_send()
        for cp, _ in sends:
            cp.wait_send()
        keep.wait()
        mine.wait()

    half_shape = jax.ShapeDtypeStruct((N_SHARD, HALF_ROWS, D_MODEL), F32)
    return pl.pallas_call(
        body, name="exchange_with_sibling",
        out_shape=(half_shape, half_shape, jax.ShapeDtypeStruct((8,) + small.shape, F32)),
        in_specs=[pl.BlockSpec(memory_space=pl.ANY)] * 2, out_specs=[pl.BlockSpec(memory_space=pl.ANY)] * 3,
        scratch_shapes=[pltpu.SemaphoreType.DMA((8,)), pltpu.SemaphoreType.DMA((8,)), pltpu.SemaphoreType.DMA((2,))],
    )(grads, small)


def _exchange_between_chips(chip_sums):
    def body(h, got, own, send_sems, recv_sems, local_sem):
        x, y, c = _position()
        keep = pltpu.make_async_copy(h.at[2 * x + y], own, local_sem)
        keep.start()
        chips = [(1 - x, y), (x, 1 - y), (1 - x, 1 - y)]
        sends = []
        for k, (px, py) in enumerate(chips):
            cp = pltpu.make_async_remote_copy(
                src_ref=h.at[2 * px + py], dst_ref=got.at[k], send_sem=send_sems.at[k], recv_sem=recv_sems.at[k],
                device_id=(px, py, c), device_id_type=MESH)
            cp.start()
            sends.append(cp)
        for cp in sends:
            cp.wait_recv()
        for cp in sends:
            cp.wait_send()
        keep.wait()

    return pl.pallas_call(
        body, name="exchange_between_chips",
        out_shape=(jax.ShapeDtypeStruct((3, HALF_ROWS, D_MODEL), BF16), jax.ShapeDtypeStruct((HALF_ROWS, D_MODEL), BF16)),
        in_specs=[pl.BlockSpec(memory_space=pl.ANY)], out_specs=[pl.BlockSpec(memory_space=pl.ANY)] * 2,
        scratch_shapes=[pltpu.SemaphoreType.DMA((3,)), pltpu.SemaphoreType.DMA((3,)), pltpu.SemaphoreType.DMA],
    )(chip_sums)


def _join_halves(total_half):
    def body(t, out, send_sem, recv_sem, local_sem):
        x, y, c = _position()
        half = pl.ds(c * HALF_ROWS, HALF_ROWS)
        other = pl.ds((1 - c) * HALF_ROWS, HALF_ROWS)
        keep = pltpu.make_async_copy(t, out.at[half], local_sem)
        keep.start()
        cp = pltpu.make_async_remote_copy(
            src_ref=t, dst_ref=out.at[half], send_sem=send_sem, recv_sem=recv_sem,
            device_id=(x, y, 1 - c), device_id_type=MESH)
        cp.start()
        pltpu.make_async_remote_copy(
            src_ref=t, dst_ref=out.at[other], send_sem=send_sem, recv_sem=recv_sem,
            device_id=(x, y, c), device_id_type=MESH).wait_recv()
        cp.wait_send()
        keep.wait()

    return pl.pallas_call(
        body, name="join_halves",
        out_shape=jax.ShapeDtypeStruct((PACK_ROWS, D_MODEL), F32),
        in_specs=[pl.BlockSpec(memory_space=pl.ANY)], out_specs=pl.BlockSpec(memory_space=pl.ANY),
        scratch_shapes=[pltpu.SemaphoreType.DMA, pltpu.SemaphoreType.DMA, pltpu.SemaphoreType.DMA],
    )(total_half)


def _add_pairs(own, got):
    rows = HALF_ROWS // 2

    def body(a, b, o):
        o[...] = (a[...] + b[...]).astype(BF16)

    spec = pl.BlockSpec((1, rows, D_MODEL), lambda k, i: (k, i, 0))
    return pl.pallas_call(
        body, name="add_pairs", grid=(N_SHARD, 2), in_specs=[spec, spec], out_specs=spec,
        out_shape=jax.ShapeDtypeStruct(own.shape, BF16), compiler_params=_cp(("arbitrary", "arbitrary")),
    )(own, got)


def _add_chips(own, got):
    rows = HALF_ROWS // 2

    def body(a, b, o):
        o[...] = ((a[...].astype(F32) + b[0].astype(F32)) + b[1].astype(F32)) + b[2].astype(F32)

    return pl.pallas_call(
        body, name="add_chips", grid=(2,),
        in_specs=[pl.BlockSpec((rows, D_MODEL), lambda i: (i, 0)), pl.BlockSpec((3, rows, D_MODEL), lambda i: (0, i, 0))],
        out_specs=pl.BlockSpec((rows, D_MODEL), lambda i: (i, 0)),
        out_shape=jax.ShapeDtypeStruct(own.shape, F32), compiler_params=_cp(("arbitrary",)),
    )(own, got)


def _sum_smalls(smalls):
    def body(s, o):
        acc = s[0]
        for d in range(1, 8):
            acc = acc + s[d]
        o[...] = acc

    return pl.pallas_call(body, name="sum_smalls", out_shape=jax.ShapeDtypeStruct(smalls.shape[1:], F32))(smalls)


def _proj(x, w_in_p):
    seq = x.shape[0]
    tr = 512
    splits = ((0, 384), (384, 640), (640, 1664), (1664, 3200), (3200, 3328))

    def body(x_ref, w_ref, *outs):
        xb = x_ref[...].astype(BF16)
        for (lo, hi), o in zip(splits, outs):
            o[...] = _dot(xb, w_ref[:, lo:hi])

    return pl.pallas_call(
        body, name="proj", grid=(seq // tr,),
        in_specs=[pl.BlockSpec((tr, D_MODEL), lambda i: (i, 0)), pl.BlockSpec((D_MODEL, IN_WIDTH_PAD), lambda i: (0, 0))],
        out_specs=[pl.BlockSpec((tr, hi - lo), lambda i: (i, 0)) for lo, hi in splits],
        out_shape=[jax.ShapeDtypeStruct((seq, hi - lo), F32) for lo, hi in splits],
        compiler_params=_cp(("arbitrary",), VMEM_LIMIT),
    )(x, w_in_p)


def _mla_rot(t, lane):
    return jnp.where(lane < 80, pltpu.roll(t, 112, 1), pltpu.roll(t, 16, 1))


def _dil_rot(t, lane):
    return jnp.where(lane % 64 < 8, pltpu.roll(t, 120, 1), pltpu.roll(t, 8, 1))


def _rms(c, g):
    r = lax.rsqrt(jnp.mean(c * c, axis=-1, keepdims=True) + RMS_EPS)
    return r, c * r * g


def _mla_pre(cq, ckv, kr, gq, gkv, wuq_e, wukv, ct, st):
    seq = cq.shape[0]
    tr = 512

    def body(cq_ref, ckv_ref, kr_ref, gq_ref, gkv_ref, wuq_ref, wukv_ref, ct_ref, st_ref, q_out, k_out, v_out):
        lane = lax.broadcasted_iota(jnp.int32, (tr, LANES), 1)
        ct_, st_ = ct_ref[...], st_ref[...]

        def rope(t):
            return t * ct_ + _mla_rot(t, lane) * st_

        _, qn = _rms(cq_ref[...], gq_ref[...])
        q_all = _dot(qn.astype(BF16), wuq_ref[...])
        for h in range(MLA_HEADS):
            q_out[h] = (rope(q_all[:, LANES * h:LANES * (h + 1)]) * MLA_SCALE).astype(BF16)
        _, kvn = _rms(ckv_ref[...], gkv_ref[...])
        kv_all = _dot(kvn.astype(BF16), wukv_ref[...])
        kpe = rope(kr_ref[...])
        for h in range(MLA_HEADS):
            kv_h = kv_all[:, LANES * h:LANES * (h + 1)]
            k_out[h] = jnp.where(lane < 64, kv_h, kpe).astype(BF16)
            if h % 2:
                v = jnp.where(lane >= 64, kv_h, 0.0)
            else:
                v = jnp.where(lane < 64, pltpu.roll(kv_h, 64, 1), 0.0)
            v_out[h] = v.astype(BF16)

    row = lambda w: pl.BlockSpec((tr, w), lambda i: (i, 0))
    full = lambda a: pl.BlockSpec(a.shape, lambda i: (0,) * a.ndim)
    head = pl.BlockSpec((MLA_HEADS, tr, LANES), lambda i: (0, i, 0))
    return pl.pallas_call(
        body, name="mla_pre", grid=(seq // tr,),
        in_specs=[row(Q_LORA), row(KV_LORA), row(LANES), full(gq), full(gkv), full(wuq_e), full(wukv), row(LANES), row(LANES)],
        out_specs=[head] * 3,
        out_shape=[jax.ShapeDtypeStruct((MLA_HEADS, seq, LANES), BF16)] * 3,
        compiler_params=_cp(("arbitrary",), VMEM_LIMIT),
    )(cq, ckv, kr, gq, gkv, wuq_e, wukv, ct, st)


def _mla_fwd(q, k, v):
    seq = q.shape[1]
    tq = 512
    nq = seq // tq

    def body(q_ref, k_ref, v_ref, o_ref, lse_ref, m_s, l_s, acc_s):
        i = pl.program_id(1)
        row = lax.broadcasted_iota(jnp.int32, (tq, tq), 0)
        col = lax.broadcasted_iota(jnp.int32, (tq, tq), 1)
        total = jnp.zeros((tq, LANES), F32)
        for hh in range(2):
            qh = q_ref[hh]
            m_s[...] = jnp.full((tq, LANES), NEG, F32)
            l_s[...] = jnp.zeros((tq, LANES), F32)
            acc_s[...] = jnp.zeros((tq, LANES), F32)

            def step(j, masked):
                rows = pl.ds(pl.multiple_of(j * tq, tq), tq)
                s = _dot(qh, k_ref[hh, rows, :], NT)
                if masked:
                    s = jnp.where(col <= row, s, NEG)
                m_prev = m_s[...]
                m_new = jnp.maximum(m_prev, jnp.max(s, axis=1, keepdims=True))
                p = jnp.exp(s - m_new[:, :1])
                alpha = jnp.exp(m_prev - m_new)
                l_s[...] = alpha * l_s[...] + jnp.sum(p, axis=1, keepdims=True)
                acc_s[...] = alpha * acc_s[...] + _dot(p.astype(BF16), v_ref[hh, rows, :])
                m_s[...] = m_new

            def full_step(j, carry):
                step(j, False)
                return carry

            lax.fori_loop(0, i, full_step, 0)
            step(i, True)
            total = total + acc_s[...] / l_s[...]
            lse_ref[hh] = m_s[...] + jnp.log(l_s[...])
        o_ref[...] = total

    kv_spec = pl.BlockSpec((2, seq, LANES), lambda p, i: (p, 0, 0))
    return pl.pallas_call(
        body, name="mla_fwd", grid=(MLA_HEADS // 2, nq),
        in_specs=[pl.BlockSpec((2, tq, LANES), lambda p, i: (p, i, 0)), kv_spec, kv_spec],
        out_specs=[pl.BlockSpec((tq, LANES), lambda p, i: (i, p)), pl.BlockSpec((2, tq, LANES), lambda p, i: (p, i, 0))],
        out_shape=[jax.ShapeDtypeStruct((seq, 4 * LANES), F32), jax.ShapeDtypeStruct((MLA_HEADS, seq, LANES), F32)],
        scratch_shapes=[pltpu.VMEM((tq, LANES), F32)] * 3,
        compiler_params=_cp(("arbitrary", "arbitrary"), VMEM_LIMIT),
    )(q, k, v)


def _dil_pre(qkv, cd, sd):
    seq = qkv.shape[0]
    tr = 512

    def body(q_ref, k_ref, cd_ref, sd_ref, qr_ref, kr_ref):
        lane = lax.broadcasted_iota(jnp.int32, (tr, LANES), 1)
        cd_, sd_ = cd_ref[...], sd_ref[...]
        for p in range(4):
            cols = slice(LANES * p, LANES * (p + 1))
            t = q_ref[:, cols]
            qr_ref[:, cols] = (t * cd_ + _dil_rot(t, lane) * sd_) * DIL_SCALE
            t = k_ref[:, cols]
            kr_ref[:, cols] = t * cd_ + _dil_rot(t, lane) * sd_

    blk = lambda j: pl.BlockSpec((tr, 4 * LANES), lambda i: (i, j))
    tab = pl.BlockSpec((tr, LANES), lambda i: (i, 0))
    return pl.pallas_call(
        body, name="dil_pre", grid=(seq // tr,),
        in_specs=[blk(0), blk(1), tab, tab], out_specs=[blk(0), blk(0)],
        out_shape=[jax.ShapeDtypeStruct((seq, 4 * LANES), F32)] * 2,
        compiler_params=_cp(("arbitrary",)),
    )(qkv, qkv, cd, sd)


def _dil_tile_index(t, d, seq):
    per_class = seq // (BLOCK * d)
    shift = per_class.bit_length() - 1
    r = t >> shift
    n = t & (per_class - 1)
    start = r + (BLOCK * d) * n
    prev = r + (BLOCK * d) * jnp.maximum(n - 1, 0)
    if d == 1:
        start = pl.multiple_of(start, BLOCK)
        prev = pl.multiple_of(prev, BLOCK)
    return n, start, prev


def _dil_rows(start, d):
    return pl.ds(start, BLOCK) if d == 1 else pl.ds(start, BLOCK, stride=d)


def _dil_valid(n):
    i = lax.broadcasted_iota(jnp.int32, (BLOCK, 2 * BLOCK), 0)
    j = lax.broadcasted_iota(jnp.int32, (BLOCK, 2 * BLOCK), 1)
    in_prev = (j < BLOCK) & (j >= i) & (n > 0)
    in_cur = (j >= BLOCK) & (j - BLOCK <= i)
    return in_prev | in_cur


def _dil_fwd(qr, kr, qkv):
    seq = qr.shape[0]
    n_tiles = seq // BLOCK

    def body(q_ref, k_ref, v_ref, o_ref, lse_ref, m_s, l_s, n_s):
        lane = lax.broadcasted_iota(jnp.int32, (BLOCK, LANES), 1)
        lane2 = lax.broadcasted_iota(jnp.int32, (2 * BLOCK, LANES), 1)
        for bi, d in enumerate(DIL_DILATIONS):

            def tile(t, carry, d=d, bi=bi):
                n, start, prev = _dil_tile_index(t, d, seq)
                rows, prows = _dil_rows(start, d), _dil_rows(prev, d)
                q_t = q_ref[rows, :]
                kcat = jnp.concatenate([k_ref[prows, :], k_ref[rows, :]], axis=0).astype(BF16)
                vcat = jnp.concatenate([v_ref[prows, :], v_ref[rows, :]], axis=0)
                valid = _dil_valid(n)
                m2 = l2 = num2 = None
                for hh in range(2):
                    mine = (lane >= 64) if hh else (lane < 64)
                    mine2 = (lane2 >= 64) if hh else (lane2 < 64)
                    s = _dot(jnp.where(mine, q_t, 0.0).astype(BF16), kcat, NT)
                    s = jnp.where(valid, s, NEG)
                    m = jnp.max(s, axis=1, keepdims=True)
                    p = jnp.exp(s - m)
                    l = jnp.sum(p, axis=1, keepdims=True)
                    num = _dot(p.astype(BF16), jnp.where(mine2, vcat, 0.0).astype(BF16))
                    if hh == 0:
                        m2, l2, num2 = m, l, num
                    else:
                        m2 = jnp.where(mine, m, m2)
                        l2 = jnp.where(mine, l, l2)
                        num2 = num2 + num
                if bi == 0:
                    m_s[rows, :] = m2 + jnp.zeros((BLOCK, LANES), F32)
                    l_s[rows, :] = l2 + jnp.zeros((BLOCK, LANES), F32)
                    n_s[rows, :] = num2
                else:
                    m_old = m_s[rows, :]
                    m_new = jnp.maximum(m_old, m2)
                    a = jnp.exp(m_old - m_new)
                    b = jnp.exp(m2 - m_new)
                    m_s[rows, :] = m_new
                    l_s[rows, :] = a * l_s[rows, :] + b * l2
                    n_s[rows, :] = a * n_s[rows, :] + b * num2
                return carry

            lax.fori_loop(0, n_tiles, tile, 0)
        o_ref[...] = n_s[...] / l_s[...]
        lse_ref[...] = m_s[...] + jnp.log(l_s[...])

    col = lambda off: pl.BlockSpec((seq, LANES), lambda p: (0, p + off))
    return pl.pallas_call(
        body, name="dil_fwd", grid=(4,),
        in_specs=[col(0), col(0), col(8)],
        out_specs=[col(0), pl.BlockSpec((None, seq, LANES), lambda p: (p, 0, 0))],
        out_shape=[jax.ShapeDtypeStruct((seq, 4 * LANES), F32), jax.ShapeDtypeStruct((4, seq, LANES), F32)],
        scratch_shapes=[pltpu.VMEM((seq, LANES), F32)] * 3,
        compiler_params=_cp(("arbitrary",), VMEM_LIMIT),
    )(qr, kr, qkv)


def _post(x, o_a, o_b, gates, w_out, ln_g, ln_b, target):
    seq = x.shape[0]
    tr = 512

    def body(x_ref, oa_ref, ob_ref, g_ref, w_ref, lg_ref, lb_ref, t_ref,
             dz_ref, do_ref, dg_ref, dw_ref, dlg_ref, dlb_ref, loss_ref):
        @pl.when(pl.program_id(0) == 0)
        def _():
            dw_ref[...] = jnp.zeros_like(dw_ref)
            dlg_ref[...] = jnp.zeros_like(dlg_ref)
            dlb_ref[...] = jnp.zeros_like(dlb_ref)
            loss_ref[...] = jnp.zeros_like(loss_ref)

        g = g_ref[...]
        sg = 1.0 / (1.0 + jnp.exp(-g))
        silu = g * sg
        o = jnp.concatenate([oa_ref[...], ob_ref[...]], axis=1)
        mixb = (o * silu).astype(BF16)
        w = w_ref[...]
        z = ALPHA * x_ref[...] + _dot(mixb, w)
        mu = jnp.mean(z, axis=-1, keepdims=True)
        zc = z - mu
        rstd = lax.rsqrt(jnp.mean(zc * zc, axis=-1, keepdims=True) + LN_EPS)
        xhat = zc * rstd
        lg = lg_ref[...]
        err = xhat * lg + lb_ref[...] - t_ref[...]
        loss_ref[...] += jnp.sum(err * err) * (0.5 / D_MODEL)
        dy = err * (1.0 / D_MODEL)
        dlg_ref[...] += jnp.sum(dy * xhat, axis=0, keepdims=True)
        dlb_ref[...] += jnp.sum(dy, axis=0, keepdims=True)
        dxh = dy * lg
        dz = rstd * (dxh - jnp.mean(dxh, axis=-1, keepdims=True) - xhat * jnp.mean(dxh * xhat, axis=-1, keepdims=True))
        dz_ref[...] = dz
        dzb = dz.astype(BF16)
        dmix = _dot(dzb, w, NT)
        do_ref[...] = dmix * silu
        dg_ref[...] = (dmix * o * (sg * (1.0 + g * (1.0 - sg)))).astype(BF16)
        dw_ref[...] += _dot(mixb, dzb, TN)

    row = lambda w: pl.BlockSpec((tr, w), lambda i: (i, 0))
    full = lambda s: pl.BlockSpec(s, lambda i: (0, 0))
    return pl.pallas_call(
        body, name="post", grid=(seq // tr,),
        in_specs=[row(D_MODEL), row(512), row(512), row(D_MODEL), full((D_MODEL, D_MODEL)), full((1, D_MODEL)),
                  full((1, D_MODEL)), row(D_MODEL)],
        out_specs=[row(D_MODEL), row(D_MODEL), row(D_MODEL), full((D_MODEL, D_MODEL)), full((1, D_MODEL)),
                   full((1, D_MODEL)), full((1, LANES))],
        out_shape=[jax.ShapeDtypeStruct((seq, D_MODEL), F32), jax.ShapeDtypeStruct((seq, D_MODEL), F32),
                   jax.ShapeDtypeStruct((seq, D_MODEL), BF16), jax.ShapeDtypeStruct((D_MODEL, D_MODEL), F32),
                   jax.ShapeDtypeStruct((1, D_MODEL), F32), jax.ShapeDtypeStruct((1, D_MODEL), F32),
                   jax.ShapeDtypeStruct((1, LANES), F32)],
        compiler_params=_cp(("arbitrary",), VMEM_LIMIT),
    )(x, o_a, o_b, gates, w_out, ln_g, ln_b, target)


def _mla_bwd(q, k, v, d_o, o, lse):
    seq = q.shape[1]
    tq = 512
    nq = seq // tq

    def body(q_ref, k_ref, v_ref, do_ref, o_ref, lse_ref, dq_ref, dk_ref, dv_ref, d_s, dk_s, dv_s):
        h = pl.program_id(0)
        j = pl.program_id(1)
        lane = lax.broadcasted_iota(jnp.int32, (tq, LANES), 1)
        first_lane = (h % 2) * 64
        mine = (lane >= first_lane) & (lane < first_lane + 64)
        row = lax.broadcasted_iota(jnp.int32, (tq, tq), 0)
        col = lax.broadcasted_iota(jnp.int32, (tq, tq), 1)

        @pl.when(j == 0)
        def _():
            dq_ref[...] = jnp.zeros_like(dq_ref)

            def rowsum(i, carry):
                rows = pl.ds(pl.multiple_of(i * tq, tq), tq)
                prod = jnp.where(mine, do_ref[rows, :] * o_ref[rows, :], 0.0)
                d_s[rows, :] = jnp.sum(prod, axis=1, keepdims=True) + jnp.zeros((tq, LANES), F32)
                return carry

            lax.fori_loop(0, nq, rowsum, 0)

        kb, vb = k_ref[...], v_ref[...]
        dk_s[...] = jnp.zeros_like(dk_s)
        dv_s[...] = jnp.zeros_like(dv_s)

        def step(i, masked):
            rows = pl.ds(pl.multiple_of(i * tq, tq), tq)
            qb = q_ref[rows, :]
            dob = do_ref[rows, :].astype(BF16)
            s = _dot(qb, kb, NT)
            p = jnp.exp(s - lse_ref[rows, :][:, :1])
            if masked:
                p = jnp.where(col <= row, p, 0.0)
            dv_s[...] += _dot(p.astype(BF16), dob, TN)
            dp = _dot(dob, vb, NT)
            ds = (p * (dp - d_s[rows, :][:, :1])).astype(BF16)
            dk_s[...] += _dot(ds, qb, TN)
            dq_ref[rows, :] += _dot(ds, kb)

        def full_step(i, carry):
            step(i, False)
            return carry

        step(j, True)
        lax.fori_loop(j + 1, nq, full_step, 0)
        dk_ref[...] = dk_s[...]
        dv_ref[...] = dv_s[...]

    whole = pl.BlockSpec((None, seq, LANES), lambda h, j: (h, 0, 0))
    blk = pl.BlockSpec((None, tq, LANES), lambda h, j: (h, j, 0))
    pair = pl.BlockSpec((seq, LANES), lambda h, j: (0, h // 2))
    shape = jax.ShapeDtypeStruct((MLA_HEADS, seq, LANES), F32)
    return pl.pallas_call(
        body, name="mla_bwd", grid=(MLA_HEADS, nq),
        in_specs=[whole, blk, blk, pair, pair, whole],
        out_specs=[whole, blk, blk], out_shape=[shape] * 3,
        scratch_shapes=[pltpu.VMEM((seq, LANES), F32), pltpu.VMEM((tq, LANES), F32), pltpu.VMEM((tq, LANES), F32)],
        compiler_params=_cp(("arbitrary", "arbitrary"), VMEM_LIMIT),
    )(q, k, v, d_o, o, lse)


def _dil_bwd(qr, kr, qkv, d_o, o, lse):
    seq = qr.shape[0]
    n_tiles = seq // BLOCK
    chunk = 512

    def body(q_ref, k_ref, v_ref, do_ref, o_ref, lse_ref, dq_ref, dk_ref, dv_ref, d_s, dq_s, dk_s, dv_s):
        lane = lax.broadcasted_iota(jnp.int32, (BLOCK, LANES), 1)
        lanec = lax.broadcasted_iota(jnp.int32, (chunk, LANES), 1)

        def rowsum(i, carry):
            rows = pl.ds(pl.multiple_of(i * chunk, chunk), chunk)
            prod = do_ref[rows, :] * o_ref[rows, :]
            lo = jnp.sum(jnp.where(lanec < 64, prod, 0.0), axis=1, keepdims=True)
            hi = jnp.sum(jnp.where(lanec >= 64, prod, 0.0), axis=1, keepdims=True)
            d_s[rows, :] = jnp.where(lanec < 64, lo, hi)
            return carry

        lax.fori_loop(0, seq // chunk, rowsum, 0)
        dq_s[...] = jnp.zeros_like(dq_s)
        dk_s[...] = jnp.zeros_like(dk_s)
        dv_s[...] = jnp.zeros_like(dv_s)
        for d in DIL_DILATIONS:

            def tile(t, carry, d=d):
                n, start, prev = _dil_tile_index(t, d, seq)
                rows, prows = _dil_rows(start, d), _dil_rows(prev, d)
                q_t = q_ref[rows, :]
                do_t = do_ref[rows, :]
                lse_t = lse_ref[rows, :]
                d_t = d_s[rows, :]
                kcat = jnp.concatenate([k_ref[prows, :], k_ref[rows, :]], axis=0).astype(BF16)
                vcat = jnp.concatenate([v_ref[prows, :], v_ref[rows, :]], axis=0).astype(BF16)
                valid = _dil_valid(n)
                dq_t = jnp.zeros((BLOCK, LANES), F32)
                dkcat = jnp.zeros((2 * BLOCK, LANES), F32)
                dvcat = jnp.zeros((2 * BLOCK, LANES), F32)
                for hh in range(2):
                    mine = (lane >= 64) if hh else (lane < 64)
                    c0 = 64 * hh
                    qh = jnp.where(mine, q_t, 0.0).astype(BF16)
                    doh = jnp.where(mine, do_t, 0.0).astype(BF16)
                    s = _dot(qh, kcat, NT)
                    p = jnp.where(valid, jnp.exp(s - lse_t[:, c0:c0 + 1]), 0.0)
                    dvcat = dvcat + _dot(p.astype(BF16), doh, TN)
                    dp = _dot(doh, vcat, NT)
                    ds = (p * (dp - d_t[:, c0:c0 + 1])).astype(BF16)
                    dq_t = dq_t + jnp.where(mine, _dot(ds, kcat), 0.0)
                    dkcat = dkcat + _dot(ds, qh, TN)
                dq_s[rows, :] += dq_t
                dk_s[prows, :] += dkcat[:BLOCK]
                dk_s[rows, :] += dkcat[BLOCK:]
                dv_s[prows, :] += dvcat[:BLOCK]
                dv_s[rows, :] += dvcat[BLOCK:]
                return carry

            lax.fori_loop(0, n_tiles, tile, 0)
        dq_ref[...] = dq_s[...].astype(BF16)
        dk_ref[...] = dk_s[...].astype(BF16)
        dv_ref[...] = dv_s[...].astype(BF16)

    col = lambda off: pl.BlockSpec((seq, LANES), lambda p: (0, p + off))
    shape = jax.ShapeDtypeStruct((seq, 4 * LANES), BF16)
    return pl.pallas_call(
        body, name="dil_bwd", grid=(4,),
        in_specs=[col(0), col(0), col(8), col(4), col(0), pl.BlockSpec((None, seq, LANES), lambda p: (p, 0, 0))],
        out_specs=[col(0)] * 3, out_shape=[shape] * 3,
        scratch_shapes=[pltpu.VMEM((seq, LANES), F32)] * 4,
        compiler_params=_cp(("arbitrary",), VMEM_LIMIT),
    )(qr, kr, qkv, d_o, o, lse)


def _mla_pre_bwd(cq, ckv, gq, gkv, wuq_e, wukv, ct, st, dq, dk, dv):
    seq = cq.shape[0]
    tr = 512

    def body(cq_ref, ckv_ref, gq_ref, gkv_ref, wuq_ref, wukv_ref, ct_ref, st_ref, dq_ref, dk_ref, dv_ref,
             dcq_ref, dckv_ref, dkr_ref, dwuq_ref, dwukv_ref, dgq_ref, dgkv_ref):
        @pl.when(pl.program_id(0) == 0)
        def _():
            dwuq_ref[...] = jnp.zeros_like(dwuq_ref)
            dwukv_ref[...] = jnp.zeros_like(dwukv_ref)
            dgq_ref[...] = jnp.zeros_like(dgq_ref)
            dgkv_ref[...] = jnp.zeros_like(dgkv_ref)

        lane = lax.broadcasted_iota(jnp.int32, (tr, LANES), 1)
        rope_lanes = jnp.logical_and(lane >= 64, lane < 96)
        ct_, st_ = ct_ref[...], st_ref[...]

        def rope_t(g):
            return ct_ * g + jnp.where(rope_lanes, _mla_rot(st_ * g, lane), 0.0)

        def norm_bwd(c, g, dn, dg_ref):
            r, _ = _rms(c, g)
            u = dn * g
            dg_ref[...] += jnp.sum(dn * c * r, axis=0, keepdims=True)
            return r * u - c * (r * r * r) * jnp.mean(u * c, axis=-1, keepdims=True)

        c, g = cq_ref[...], gq_ref[...]
        _, qn = _rms(c, g)
        dq_all = jnp.concatenate([rope_t(dq_ref[h] * MLA_SCALE) for h in range(MLA_HEADS)], axis=1).astype(BF16)
        dwuq_ref[...] += _dot(qn.astype(BF16), dq_all, TN)
        dcq_ref[...] = norm_bwd(c, g, _dot(dq_all, wuq_ref[...], NT), dgq_ref).astype(BF16)

        c, g = ckv_ref[...], gkv_ref[...]
        _, kvn = _rms(c, g)
        dkpe = jnp.zeros((tr, LANES), F32)
        parts = []
        for h in range(MLA_HEADS):
            dk_h, dv_h = dk_ref[h], dv_ref[h]
            if h % 2 == 0:
                dv_h = pltpu.roll(dv_h, 64, 1)
            parts.append(jnp.where(lane < 64, dk_h, dv_h))
            dkpe = dkpe + jnp.where(rope_lanes, dk_h, 0.0)
        dkv_all = jnp.concatenate(parts, axis=1).astype(BF16)
        dwukv_ref[...] += _dot(kvn.astype(BF16), dkv_all, TN)
        dckv_ref[...] = norm_bwd(c, g, _dot(dkv_all, wukv_ref[...], NT), dgkv_ref).astype(BF16)
        dkr_ref[...] = rope_t(dkpe).astype(BF16)

    row = lambda w: pl.BlockSpec((tr, w), lambda i: (i, 0))
    full = lambda a: pl.BlockSpec(a.shape, lambda i: (0,) * a.ndim)
    head = pl.BlockSpec((MLA_HEADS, tr, LANES), lambda i: (0, i, 0))
    return pl.pallas_call(
        body, name="mla_pre_bwd", grid=(seq // tr,),
        in_specs=[row(Q_LORA), row(KV_LORA), full(gq), full(gkv), full(wuq_e), full(wukv), row(LANES), row(LANES),
                  head, head, head],
        out_specs=[row(Q_LORA), row(KV_LORA), row(LANES), full(wuq_e), full(wukv), full(gq), full(gkv)],
        out_shape=[jax.ShapeDtypeStruct((seq, Q_LORA), BF16), jax.ShapeDtypeStruct((seq, KV_LORA), BF16),
                   jax.ShapeDtypeStruct((seq, LANES), BF16), jax.ShapeDtypeStruct(wuq_e.shape, F32),
                   jax.ShapeDtypeStruct(wukv.shape, F32), jax.ShapeDtypeStruct(gq.shape, F32),
                   jax.ShapeDtypeStruct(gkv.shape, F32)],
        compiler_params=_cp(("arbitrary",), VMEM_LIMIT),
    )(cq, ckv, gq, gkv, wuq_e, wukv, ct, st, dq, dk, dv)


def _in_bwd(dz, dcq, dckv, dgates, dqr, dkr, dvb, dkrope, cd, sd, w_in_p):
    seq = dz.shape[0]
    tr = 512

    def body(dz_ref, dcq_ref, dckv_ref, dg_ref, dqr_ref, dkr_ref, dvb_ref, dkp_ref, cd_ref, sd_ref, w_ref, gx_ref, dh_ref):
        lane = lax.broadcasted_iota(jnp.int32, (tr, LANES), 1)
        rot_lanes = lane % 64 < DIL_ROT
        cd_, sd_ = cd_ref[...], sd_ref[...]

        def rope_t(g):
            return cd_ * g + jnp.where(rot_lanes, _dil_rot(sd_ * g, lane), 0.0)

        dq = [rope_t(dqr_ref[:, LANES * p:LANES * (p + 1)].astype(F32) * DIL_SCALE).astype(BF16) for p in range(4)]
        dk = [rope_t(dkr_ref[:, LANES * p:LANES * (p + 1)].astype(F32)).astype(BF16) for p in range(4)]
        dh = jnp.concatenate([dcq_ref[...], dckv_ref[...], dg_ref[...]] + dq + dk + [dvb_ref[...], dkp_ref[...]], axis=1)
        dh_ref[...] = dh
        gx_ref[...] = ALPHA * dz_ref[...] + _dot(dh, w_ref[...], NT)

    row = lambda w: pl.BlockSpec((tr, w), lambda i: (i, 0))
    return pl.pallas_call(
        body, name="in_bwd", grid=(seq // tr,),
        in_specs=[row(D_MODEL), row(Q_LORA), row(KV_LORA), row(D_MODEL), row(512), row(512), row(512), row(LANES),
                  row(LANES), row(LANES), pl.BlockSpec((D_MODEL, IN_WIDTH_PAD), lambda i: (0, 0))],
        out_specs=[row(D_MODEL), row(IN_WIDTH_PAD)],
        out_shape=[jax.ShapeDtypeStruct((seq, D_MODEL), F32), jax.ShapeDtypeStruct((seq, IN_WIDTH_PAD), BF16)],
        compiler_params=_cp(("arbitrary",), VMEM_LIMIT),
    )(dz, dcq, dckv, dgates, dqr, dkr, dvb, dkrope, cd, sd, w_in_p)


def _dw_in(x_t, dh):
    seq = dh.shape[0]
    tk = 512
    tn = IN_WIDTH_PAD // 2

    def body(x_ref, dh_ref, o_ref):
        @pl.when(pl.program_id(1) == 0)
        def _():
            o_ref[...] = jnp.zeros_like(o_ref)

        o_ref[...] += _dot(x_ref[...], dh_ref[...])

    return pl.pallas_call(
        body, name="dw_in", grid=(2, seq // tk),
        in_specs=[pl.BlockSpec((D_MODEL, tk), lambda n, k: (0, k)), pl.BlockSpec((tk, tn), lambda n, k: (k, n))],
        out_specs=pl.BlockSpec((D_MODEL, tn), lambda n, k: (0, n)),
        out_shape=jax.ShapeDtypeStruct((D_MODEL, IN_WIDTH_PAD), F32),
        compiler_params=_cp(("arbitrary", "arbitrary"), VMEM_LIMIT),
    )(x_t, dh)


def _adamw(w, g, m, v, name):
    rows, cols = w.shape
    tr = 256 if rows % 256 == 0 else rows

    def body(w_ref, g_ref, m_ref, v_ref, d_ref, nm_ref, nv_ref):
        g_ = g_ref[...]
        nm = ADAM_B1 * m_ref[...] + (1.0 - ADAM_B1) * g_
        nv = ADAM_B2 * v_ref[...] + (1.0 - ADAM_B2) * jnp.square(g_)
        m_hat = nm / (1.0 - ADAM_B1 ** ADAM_STEP)
        v_hat = nv / (1.0 - ADAM_B2 ** ADAM_STEP)
        d_ref[...] = -ADAM_LR * (m_hat / (jnp.sqrt(v_hat) + ADAM_EPS) + ADAM_WD * w_ref[...])
        nm_ref[...] = nm
        nv_ref[...] = nv

    spec = pl.BlockSpec((tr, cols), lambda i: (i, 0))
    return pl.pallas_call(
        body, name=name, grid=(rows // tr,), in_specs=[spec] * 4, out_specs=[spec] * 3,
        out_shape=[jax.ShapeDtypeStruct(w.shape, F32)] * 3, compiler_params=_cp(("arbitrary",)),
    )(w, g, m, v)


def _pad_row(v):
    return jnp.pad(v.reshape(1, -1), ((0, 0), (0, D_MODEL - v.shape[-1])))


def _local_step(x2, target, w_in_f, w_uq_f, wukv_f, w_out_f, q_norm_g, kv_norm_g, ln_g, ln_b):
    seq = x2.shape[0]
    w_in_p = _permute_w_in(w_in_f)
    wuq_e = jnp.pad(w_uq_f.reshape(Q_LORA, MLA_HEADS, 96), ((0, 0), (0, 0), (0, 32))).reshape(Q_LORA, MLA_HEADS * LANES)
    ct, st, cd, sd = _rope_tables(seq)
    gq = q_norm_g.reshape(1, Q_LORA)
    gkv = kv_norm_g.reshape(1, KV_LORA)

    cq, ckv, gates, qkv, kr = _proj(x2, w_in_p)
    q_e, k_e, v_e = _mla_pre(cq, ckv, kr, gq, gkv, wuq_e, wukv_f, ct, st)
    o_a, lse_a = _mla_fwd(q_e, k_e, v_e)
    qr, krot = _dil_pre(qkv, cd, sd)
    o_b, lse_b = _dil_fwd(qr, krot, qkv)

    dz, d_o, d_gates, dw_out, dln_g, dln_b, loss_part = _post(
        x2, o_a, o_b, gates, w_out_f, ln_g.reshape(1, D_MODEL), ln_b.reshape(1, D_MODEL), target)
    dq_e, dk_e, dv_e = _mla_bwd(q_e, k_e, v_e, d_o, o_a, lse_a)
    dqr, dkr, dvb = _dil_bwd(qr, krot, qkv, d_o, o_b, lse_b)
    dcq, dckv, dkrope, dwuq_e, dwukv, dgq, dgkv = _mla_pre_bwd(cq, ckv, gq, gkv, wuq_e, wukv_f, ct, st, dq_e, dk_e, dv_e)
    grad_x, dh = _in_bwd(dz, dcq, dckv, d_gates, dqr, dkr, dvb, dkrope, cd, sd, w_in_p)
    dw_in = _unpermute_dw_in(_dw_in(x2.T.astype(BF16), dh))
    dw_uq = dwuq_e.reshape(Q_LORA, MLA_HEADS, LANES)[:, :, :96].reshape(Q_LORA, MLA_HEADS * 96)
    return loss_part, grad_x, dw_in, dw_uq, dwukv, dw_out, dgq, dgkv, dln_g, dln_b


def kernel(x, w_in, q_norm_g, kv_norm_g, w_uq, w_ukv, w_out, ln_g, ln_b, loss_target, m_w_in, m_q_norm_g, m_kv_norm_g, m_w_uq, m_w_ukv, m_w_out, m_ln_g, m_ln_b, v_w_in, v_q_norm_g, v_kv_norm_g, v_w_uq, v_w_ukv, v_w_out, v_ln_g, v_ln_b):
    seq = x.shape[1]
    x2 = x.reshape(seq, D_MODEL)
    target = loss_target.reshape(seq, D_MODEL)

    gathered = _all_gather_weights(_pack_shard(w_in, w_uq, w_ukv, w_out, BF16))
    parts = [_unpack_shard(gathered[j]) for j in range(N_SHARD)]
    w_in_f = jnp.concatenate([p[0] for p in parts], axis=1)
    w_uq_f = jnp.concatenate([p[1] for p in parts], axis=1)
    wukv_f = jnp.concatenate([p[2] for p in parts], axis=1)
    w_out_f = jnp.concatenate([p[3] for p in parts], axis=0)

    loss_part, grad_x, dw_in, dw_uq, dwukv, dw_out, dgq, dgkv, dln_g, dln_b = _local_step(
        x2, target, w_in_f, w_uq_f, wukv_f, w_out_f, q_norm_g, kv_norm_g, ln_g, ln_b)

    grads = jnp.stack([
        _pack_shard(dw_in[:, 808 * j:808 * (j + 1)], dw_uq[:, 192 * j:192 * (j + 1)],
                    dwukv[:, 256 * j:256 * (j + 1)], dw_out[256 * j:256 * (j + 1)], F32)
        for j in range(N_SHARD)])
    small = jnp.concatenate([_pad_row(dgq), _pad_row(dgkv), dln_g, dln_b, _pad_row(loss_part),
                             jnp.zeros((3, D_MODEL), F32)], axis=0)
    got, own, smalls = _exchange_with_sibling(grads, small)
    got2, own2 = _exchange_between_chips(_add_pairs(own, got))
    g_in, g_uq, g_ukv, g_out = _unpack_shard(_join_halves(_add_chips(own2, got2)))
    small_sum = _sum_smalls(smalls)
    loss = small_sum[4, 0]

    big = [_adamw(w, g, m, v, name) for w, g, m, v, name in (
        (w_in, g_in, m_w_in, v_w_in, "adamw_w_in"), (w_uq, g_uq, m_w_uq, v_w_uq, "adamw_w_uq"),
        (w_ukv, g_ukv, m_w_ukv, v_w_ukv, "adamw_w_ukv"), (w_out, g_out, m_w_out, v_w_out, "adamw_w_out"))]
    vec = lambda a, b, c_, d: jnp.concatenate([_pad_row(a), _pad_row(b), _pad_row(c_), _pad_row(d),
                                               jnp.zeros((4, D_MODEL), F32)], axis=0)
    sw = vec(q_norm_g, kv_norm_g, ln_g, ln_b)
    sm = vec(m_q_norm_g, m_kv_norm_g, m_ln_g, m_ln_b)
    sv = vec(v_q_norm_g, v_kv_norm_g, v_ln_g, v_ln_b)
    sg = jnp.concatenate([small_sum[:4], jnp.zeros((4, D_MODEL), F32)], axis=0)
    s_delta, s_m, s_v = _adamw(sw, sg, sm, sv, "adamw_vectors")

    def vectors(a):
        return [a[0, :Q_LORA], a[1, :KV_LORA], a[2], a[3]]

    def ordered(bigs, smalls_):
        return [bigs[0], smalls_[0], smalls_[1], bigs[1], bigs[2], bigs[3], smalls_[2], smalls_[3]]

    grads_out = ordered([g_in, g_uq, g_ukv, g_out], vectors(small_sum))
    deltas = ordered([b[0] for b in big], vectors(s_delta))
    new_m = ordered([b[1] for b in big], vectors(s_m))
    new_v = ordered([b[2] for b in big], vectors(s_v))
    return (loss, grad_x.reshape(x.shape), *grads_out, *deltas, *new_m, *new_v)
```

```python
import functools

import jax
import jax.numpy as jnp
from jax import lax
from jax.experimental import pallas as pl
from jax.experimental.pallas import tpu as pltpu

F32 = jnp.float32
BF16 = jnp.bfloat16

D_MODEL = 1024
ROPE_THETA = 500000.0
BLOCK = 128
NEG = -1e30
RMS_EPS = 1e-6
LN_EPS = 1e-5

MLA_HEADS = 8
MLA_NOPE = 64
MLA_ROPE = 32
Q_LORA = 384
KV_LORA = 256
DIL_HEADS = 8
DIL_HEAD_DIM = 64
DIL_ROT = 16
DIL_DILATIONS = (1, 4, 16)
IN_WIDTH = 3232
IN_WIDTH_PAD = 3328
MLA_SCALE = (MLA_NOPE + MLA_ROPE) ** -0.5
DIL_SCALE = DIL_HEAD_DIM ** -0.5
ALPHA = 2.0 ** 0.25

ADAM_LR = 0.001
ADAM_B1 = 0.9
ADAM_B2 = 0.999
ADAM_EPS = 1e-08
ADAM_WD = 0.01
ADAM_STEP = 10

N_SHARD = 4
SHARD_SHAPES = ((1024, 808), (384, 192), (256, 256), (256, 1024))
ROW_CHUNK = 64
LANES = 128
VMEM_LIMIT = 56 * 1024 * 1024
MESH = pl.DeviceIdType.MESH

NT = (((1,), (1,)), ((), ()))
TN = (((0,), (0,)), ((), ()))


def _cp(sem=None, vmem=None):
    return pltpu.CompilerParams(dimension_semantics=sem, vmem_limit_bytes=vmem)


def _dot(a, b, dims=None):
    if dims is None:
        return jnp.dot(a, b, preferred_element_type=F32)
    return lax.dot_general(a, b, dims, preferred_element_type=F32)


def _rope_tables(seq):
    pos = jnp.arange(seq, dtype=F32)[:, None]
    inv = ROPE_THETA ** (-jnp.arange(0, MLA_ROPE, 2, dtype=F32) / MLA_ROPE)
    ang = pos * inv[None, :]
    cos, sin = jnp.cos(ang), jnp.sin(ang)
    one, zero = jnp.ones((seq, 64), F32), jnp.zeros((seq, 64), F32)
    ct = jnp.concatenate([one, cos, cos, zero[:, :32]], axis=1)
    st = jnp.concatenate([zero, -sin, sin, zero[:, :32]], axis=1)
    inv = ROPE_THETA ** (-jnp.arange(0, DIL_ROT, 2, dtype=F32) / DIL_ROT)
    ang = pos * inv[None, :]
    cos, sin = jnp.cos(ang), jnp.sin(ang)
    cd = jnp.concatenate([cos, cos, one[:, :48]], axis=1)
    sd = jnp.concatenate([-sin, sin, zero[:, :48]], axis=1)
    return ct, st, jnp.tile(cd, (1, 2)), jnp.tile(sd, (1, 2))


def _permute_w_in(w):
    z = jnp.zeros((w.shape[0], 64), w.dtype)
    return jnp.concatenate([w[:, 0:640], w[:, 672:1184], w[:, 2720:3232], w[:, 1184:2720],
                            z, w[:, 640:672], z[:, :32]], axis=1)


def _unpermute_dw_in(dw):
    return jnp.concatenate([dw[:, 0:640], dw[:, 3264:3296], dw[:, 640:1152], dw[:, 1664:3200],
                            dw[:, 1152:1664]], axis=1)


def _position():
    return lax.axis_index("x"), lax.axis_index("y"), lax.axis_index("c")


def _halves(c, rows):
    hr = rows // 2
    return pl.ds(pl.multiple_of(c * hr, 8), hr), pl.ds(pl.multiple_of((1 - c) * hr, 8), hr)


def _for_row_chunks(rows, fn):
    def step(i, carry):
        fn(pl.multiple_of(i * ROW_CHUNK, ROW_CHUNK))
        return carry

    lax.fori_loop(0, rows // ROW_CHUNK, step, 0)


def _all_gather_weights(shards):
    n = len(shards)

    def body(*refs):
        ins, outs = refs[:n], refs[n:2 * n]
        send_sems, recv_sems = refs[2 * n:]
        x, y, c = _position()
        me = 2 * x + y
        chips = [(1 - x, y), (x, 1 - y), (1 - x, 1 - y)]
        for a in range(n):
            def cast(r, a=a):
                outs[a][me, pl.ds(r, ROW_CHUNK), :] = ins[a][pl.ds(r, ROW_CHUNK), :].astype(BF16)

            _for_row_chunks(SHARD_SHAPES[a][0], cast)

        def copy(k, a, slot, rows, to):
            ref = outs[a].at[slot, rows]
            return pltpu.make_async_remote_copy(
                src_ref=ref, dst_ref=ref, send_sem=send_sems.at[k * n + a], recv_sem=recv_sems.at[k * n + a],
                device_id=to, device_id_type=MESH)

        half = [_halves(c, SHARD_SHAPES[a][0])[0] for a in range(n)]
        other = [_halves(c, SHARD_SHAPES[a][0])[1] for a in range(n)]
        first = [copy(k, a, me, half[a], (px, py, c)) for k, (px, py) in enumerate(chips) for a in range(n)]
        for cp in first:
            cp.start()
        passed = []
        for k, (px, py) in enumerate(chips):
            for a in range(n):
                copy(k, a, 2 * px + py, half[a], (x, y, c)).wait_recv()
                cp = copy(3 + k, a, 2 * px + py, half[a], (x, y, 1 - c))
                cp.start()
                passed.append(cp)
        for k, (px, py) in enumerate(chips):
            for a in range(n):
                copy(3 + k, a, 2 * px + py, other[a], (x, y, c)).wait_recv()
        for cp in first + passed:
            cp.wait_send()

    vmem = pl.BlockSpec(memory_space=pltpu.VMEM)
    return pl.pallas_call(
        body, name="all_gather_weights",
        out_shape=[jax.ShapeDtypeStruct((N_SHARD,) + s, BF16) for s in SHARD_SHAPES],
        in_specs=[vmem] * n, out_specs=[vmem] * n,
        scratch_shapes=[pltpu.SemaphoreType.DMA((6 * n,)), pltpu.SemaphoreType.DMA((6 * n,))],
        compiler_params=_cp(None, VMEM_LIMIT),
    )(*shards)


def _reduce_over_sibling(grads, small):
    n = len(grads)

    def body(*refs):
        g_hbm, sm = refs[:n], refs[n]
        sums, smalls = refs[n + 1:2 * n + 1], refs[2 * n + 1]
        stage, got = refs[2 * n + 2:3 * n + 2], refs[3 * n + 2:4 * n + 2]
        send_sems, recv_sems, local_sems = refs[4 * n + 2:]
        x, y, c = _position()
        me = 4 * x + 2 * y + c
        loads = [pltpu.make_async_copy(g_hbm[a], stage[a], local_sems.at[a]) for a in range(n)]
        for ld in loads:
            ld.start()
        smalls[me] = sm[...]
        sends = []
        for rel in range(1, 8):
            px = 1 - x if rel // 4 else x
            py = 1 - y if (rel // 2) % 2 else y
            pc = 1 - c if rel % 2 else c
            cp = pltpu.make_async_remote_copy(
                src_ref=sm, dst_ref=smalls.at[me], send_sem=send_sems.at[n + rel], recv_sem=recv_sems.at[n + rel],
                device_id=(px, py, pc), device_id_type=MESH)
            cp.start()
            sends.append((cp, 4 * px + 2 * py + pc))
        swaps = []
        for a in range(n):
            loads[a].wait()
            _, other = _halves(c, SHARD_SHAPES[a][0])
            cp = pltpu.make_async_remote_copy(
                src_ref=stage[a].at[:, other], dst_ref=got[a], send_sem=send_sems.at[a], recv_sem=recv_sems.at[a],
                device_id=(x, y, 1 - c), device_id_type=MESH)
            cp.start()
            swaps.append(cp)
        for a in range(n):
            swaps[a].wait_recv()
            hr = SHARD_SHAPES[a][0] // 2
            for k in range(N_SHARD):
                def add(r, a=a, k=k, hr=hr):
                    mine = stage[a][k, pl.ds(pl.multiple_of(c * hr + r, ROW_CHUNK), ROW_CHUNK), :]
                    sums[a][k, pl.ds(r, ROW_CHUNK), :] = (mine + got[a][k, pl.ds(r, ROW_CHUNK), :]).astype(BF16)

                _for_row_chunks(hr, add)
        for rel, (cp, peer) in enumerate(sends, start=1):
            pltpu.make_async_remote_copy(
                src_ref=sm, dst_ref=smalls.at[peer], send_sem=send_sems.at[n + rel], recv_sem=recv_sems.at[n + rel],
                device_id=(x, y, c), device_id_type=MESH).wait_recv()
        for cp in swaps:
            cp.wait_send()
        for cp, _ in sends:
            cp.wait_send()

    vmem = pl.BlockSpec(memory_space=pltpu.VMEM)
    half = [(N_SHARD, r // 2, cols) for r, cols in SHARD_SHAPES]
    return pl.pallas_call(
        body, name="reduce_over_sibling",
        out_shape=[jax.ShapeDtypeStruct(s, BF16) for s in half] + [jax.ShapeDtypeStruct((8,) + small.shape, F32)],
        in_specs=[pl.BlockSpec(memory_space=pl.ANY)] * n + [vmem], out_specs=[vmem] * (n + 1),
        scratch_shapes=[pltpu.VMEM((N_SHARD,) + s, F32) for s in SHARD_SHAPES] + [pltpu.VMEM(s, F32) for s in half]
        + [pltpu.SemaphoreType.DMA((n + 8,)), pltpu.SemaphoreType.DMA((n + 8,)), pltpu.SemaphoreType.DMA((n,))],
        compiler_params=_cp(None, VMEM_LIMIT),
    )(*grads, small)


def _reduce_over_chips(sums):
    n = len(sums)

    def body(*refs):
        h, outs, got = refs[:n], refs[n:2 * n], refs[2 * n:3 * n]
        send_sems, recv_sems = refs[3 * n:]
        x, y, c = _position()
        me = 2 * x + y
        chips = [(1 - x, y), (x, 1 - y), (1 - x, 1 - y)]
        sends = []
        for k, (px, py) in enumerate(chips):
            for a in range(n):
                cp = pltpu.make_async_remote_copy(
                    src_ref=h[a].at[2 * px + py], dst_ref=got[a].at[k], send_sem=send_sems.at[k * n + a],
                    recv_sem=recv_sems.at[k * n + a], device_id=(px, py, c), device_id_type=MESH)
                cp.start()
                sends.append(cp)
        for cp in sends:
            cp.wait_recv()
        joins = []
        for a in range(n):
            hr = SHARD_SHAPES[a][0] // 2
            half, other = _halves(c, SHARD_SHAPES[a][0])

            def add(r, a=a, hr=hr):
                rows = pl.ds(r, ROW_CHUNK)
                total = h[a][me, rows, :].astype(F32)
                for k in range(3):
                    total = total + got[a][k, rows, :].astype(F32)
                outs[a][pl.ds(pl.multiple_of(c * hr + r, ROW_CHUNK), ROW_CHUNK), :] = total

            _for_row_chunks(hr, add)
            cp = pltpu.make_async_remote_copy(
                src_ref=outs[a].at[half], dst_ref=outs[a].at[half], send_sem=send_sems.at[3 * n + a],
                recv_sem=recv_sems.at[3 * n + a], device_id=(x, y, 1 - c), device_id_type=MESH)
            cp.start()
            joins.append(cp)
        for a in range(n):
            other = _halves(c, SHARD_SHAPES[a][0])[1]
            pltpu.make_async_remote_copy(
                src_ref=outs[a].at[other], dst_ref=outs[a].at[other], send_sem=send_sems.at[3 * n + a],
                recv_sem=recv_sems.at[3 * n + a], device_id=(x, y, c), device_id_type=MESH).wait_recv()
        for cp in sends + joins:
            cp.wait_send()

    vmem = pl.BlockSpec(memory_space=pltpu.VMEM)
    return pl.pallas_call(
        body, name="reduce_over_chips",
        out_shape=[jax.ShapeDtypeStruct(s, F32) for s in SHARD_SHAPES],
        in_specs=[vmem] * n, out_specs=[vmem] * n,
        scratch_shapes=[pltpu.VMEM((3, r // 2, cols), BF16) for r, cols in SHARD_SHAPES]
        + [pltpu.SemaphoreType.DMA((4 * n,)), pltpu.SemaphoreType.DMA((4 * n,))],
        compiler_params=_cp(None, VMEM_LIMIT),
    )(*sums)


def _sum_smalls(smalls):
    def body(s, o):
        acc = s[0]
        for d in range(1, 8):
            acc = acc + s[d]
        o[...] = acc

    return pl.pallas_call(body, name="sum_smalls", out_shape=jax.ShapeDtypeStruct(smalls.shape[1:], F32))(smalls)


def _proj(x, w_in_p):
    seq = x.shape[0]
    tr = 512
    splits = ((0, 384), (384, 640), (640, 1664), (1664, 3200), (3200, 3328))

    def body(x_ref, w_ref, *outs):
        xb = x_ref[...].astype(BF16)
        for (lo, hi), o in zip(splits, outs):
            o[...] = _dot(xb, w_ref[:, lo:hi])

    return pl.pallas_call(
        body, name="proj", grid=(seq // tr,),
        in_specs=[pl.BlockSpec((tr, D_MODEL), lambda i: (i, 0)), pl.BlockSpec((D_MODEL, IN_WIDTH_PAD), lambda i: (0, 0))],
        out_specs=[pl.BlockSpec((tr, hi - lo), lambda i: (i, 0)) for lo, hi in splits],
        out_shape=[jax.ShapeDtypeStruct((seq, hi - lo), F32) for lo, hi in splits],
        compiler_params=_cp(("arbitrary",), VMEM_LIMIT),
    )(x, w_in_p)


def _mla_rot(t, lane):
    return jnp.where(lane < 80, pltpu.roll(t, 112, 1), pltpu.roll(t, 16, 1))


def _dil_rot(t, lane):
    return jnp.where(lane % 64 < 8, pltpu.roll(t, 120, 1), pltpu.roll(t, 8, 1))


def _rms(c, g):
    r = lax.rsqrt(jnp.mean(c * c, axis=-1, keepdims=True) + RMS_EPS)
    return r, c * r * g


def _mla_pre(cq, ckv, kr, gq, gkv, wuq_e, wukv, ct, st):
    seq = cq.shape[0]
    tr = 512

    def body(cq_ref, ckv_ref, kr_ref, gq_ref, gkv_ref, wuq_ref, wukv_ref, ct_ref, st_ref, q_out, k_out, v_out):
        lane = lax.broadcasted_iota(jnp.int32, (tr, LANES), 1)
        ct_, st_ = ct_ref[...], st_ref[...]

        def rope(t):
            return t * ct_ + _mla_rot(t, lane) * st_

        _, qn = _rms(cq_ref[...], gq_ref[...])
        q_all = _dot(qn.astype(BF16), wuq_ref[...])
        for h in range(MLA_HEADS):
            q_out[h] = (rope(q_all[:, LANES * h:LANES * (h + 1)]) * MLA_SCALE).astype(BF16)
        _, kvn = _rms(ckv_ref[...], gkv_ref[...])
        kv_all = _dot(kvn.astype(BF16), wukv_ref[...])
        kpe = rope(kr_ref[...])
        for h in range(MLA_HEADS):
            kv_h = kv_all[:, LANES * h:LANES * (h + 1)]
            k_out[h] = jnp.where(lane < 64, kv_h, kpe).astype(BF16)
            if h % 2:
                v = jnp.where(lane >= 64, kv_h, 0.0)
            else:
                v = jnp.where(lane < 64, pltpu.roll(kv_h, 64, 1), 0.0)
            v_out[h] = v.astype(BF16)

    row = lambda w: pl.BlockSpec((tr, w), lambda i: (i, 0))
    full = lambda a: pl.BlockSpec(a.shape, lambda i: (0,) * a.ndim)
    head = pl.BlockSpec((MLA_HEADS, tr, LANES), lambda i: (0, i, 0))
    return pl.pallas_call(
        body, name="mla_pre", grid=(seq // tr,),
        in_specs=[row(Q_LORA), row(KV_LORA), row(LANES), full(gq), full(gkv), full(wuq_e), full(wukv), row(LANES), row(LANES)],
        out_specs=[head] * 3,
        out_shape=[jax.ShapeDtypeStruct((MLA_HEADS, seq, LANES), BF16)] * 3,
        compiler_params=_cp(("arbitrary",), VMEM_LIMIT),
    )(cq, ckv, kr, gq, gkv, wuq_e, wukv, ct, st)


def _mla_fwd(q, k, v):
    seq = q.shape[1]
    tq = 512
    nq = seq // tq

    def body(q_ref, k_ref, v_ref, o_ref, lse_ref, m_s, l_s, acc_s):
        i = pl.program_id(1)
        row = lax.broadcasted_iota(jnp.int32, (tq, tq), 0)
        col = lax.broadcasted_iota(jnp.int32, (tq, tq), 1)
        total = jnp.zeros((tq, LANES), F32)
        for hh in range(2):
            qh = q_ref[hh]
            m_s[...] = jnp.full((tq, LANES), NEG, F32)
            l_s[...] = jnp.zeros((tq, LANES), F32)
            acc_s[...] = jnp.zeros((tq, LANES), F32)

            def step(j, masked):
                rows = pl.ds(pl.multiple_of(j * tq, tq), tq)
                s = _dot(qh, k_ref[hh, rows, :], NT)
                if masked:
                    s = jnp.where(col <= row, s, NEG)
                m_prev = m_s[...]
                m_new = jnp.maximum(m_prev, jnp.max(s, axis=1, keepdims=True))
                p = jnp.exp(s - m_new[:, :1])
                alpha = jnp.exp(m_prev - m_new)
                l_s[...] = alpha * l_s[...] + jnp.sum(p, axis=1, keepdims=True)
                acc_s[...] = alpha * acc_s[...] + _dot(p.astype(BF16), v_ref[hh, rows, :])
                m_s[...] = m_new

            def full_step(j, carry):
                step(j, False)
                return carry

            lax.fori_loop(0, i, full_step, 0)
            step(i, True)
            total = total + acc_s[...] / l_s[...]
            lse_ref[hh] = m_s[...] + jnp.log(l_s[...])
        o_ref[...] = total

    kv_spec = pl.BlockSpec((2, seq, LANES), lambda p, i: (p, 0, 0))
    return pl.pallas_call(
        body, name="mla_fwd", grid=(MLA_HEADS // 2, nq),
        in_specs=[pl.BlockSpec((2, tq, LANES), lambda p, i: (p, i, 0)), kv_spec, kv_spec],
        out_specs=[pl.BlockSpec((tq, LANES), lambda p, i: (i, p)), pl.BlockSpec((2, tq, LANES), lambda p, i: (p, i, 0))],
        out_shape=[jax.ShapeDtypeStruct((seq, 4 * LANES), F32), jax.ShapeDtypeStruct((MLA_HEADS, seq, LANES), F32)],
        scratch_shapes=[pltpu.VMEM((tq, LANES), F32)] * 3,
        compiler_params=_cp(("arbitrary", "arbitrary"), VMEM_LIMIT),
    )(q, k, v)


def _dil_pre(qkv, cd, sd):
    seq = qkv.shape[0]
    tr = 512

    def body(q_ref, k_ref, cd_ref, sd_ref, qr_ref, kr_ref):
        lane = lax.broadcasted_iota(jnp.int32, (tr, LANES), 1)
        cd_, sd_ = cd_ref[...], sd_ref[...]
        for p in range(4):
            cols = slice(LANES * p, LANES * (p + 1))
            t = q_ref[:, cols]
            qr_ref[:, cols] = (t * cd_ + _dil_rot(t, lane) * sd_) * DIL_SCALE
            t = k_ref[:, cols]
            kr_ref[:, cols] = t * cd_ + _dil_rot(t, lane) * sd_

    blk = lambda j: pl.BlockSpec((tr, 4 * LANES), lambda i: (i, j))
    tab = pl.BlockSpec((tr, LANES), lambda i: (i, 0))
    return pl.pallas_call(
        body, name="dil_pre", grid=(seq // tr,),
        in_specs=[blk(0), blk(1), tab, tab], out_specs=[blk(0), blk(0)],
        out_shape=[jax.ShapeDtypeStruct((seq, 4 * LANES), F32)] * 2,
        compiler_params=_cp(("arbitrary",)),
    )(qkv, qkv, cd, sd)


def _dil_tile_index(t, d, seq):
    per_class = seq // (BLOCK * d)
    shift = per_class.bit_length() - 1
    r = t >> shift
    n = t & (per_class - 1)
    start = r + (BLOCK * d) * n
    prev = r + (BLOCK * d) * jnp.maximum(n - 1, 0)
    if d == 1:
        start = pl.multiple_of(start, BLOCK)
        prev = pl.multiple_of(prev, BLOCK)
    return n, start, prev


def _dil_rows(start, d):
    return pl.ds(start, BLOCK) if d == 1 else pl.ds(start, BLOCK, stride=d)


def _dil_valid(n):
    i = lax.broadcasted_iota(jnp.int32, (BLOCK, 2 * BLOCK), 0)
    j = lax.broadcasted_iota(jnp.int32, (BLOCK, 2 * BLOCK), 1)
    in_prev = (j < BLOCK) & (j >= i) & (n > 0)
    in_cur = (j >= BLOCK) & (j - BLOCK <= i)
    return in_prev | in_cur


def _dil_fwd(qr, kr, qkv):
    seq = qr.shape[0]
    n_tiles = seq // BLOCK

    def body(q_ref, k_ref, v_ref, o_ref, lse_ref, m_s, l_s, n_s):
        lane = lax.broadcasted_iota(jnp.int32, (BLOCK, LANES), 1)
        lane2 = lax.broadcasted_iota(jnp.int32, (2 * BLOCK, LANES), 1)
        for bi, d in enumerate(DIL_DILATIONS):

            def tile(t, carry, d=d, bi=bi):
                n, start, prev = _dil_tile_index(t, d, seq)
                rows, prows = _dil_rows(start, d), _dil_rows(prev, d)
                q_t = q_ref[rows, :]
                kcat = jnp.concatenate([k_ref[prows, :], k_ref[rows, :]], axis=0).astype(BF16)
                vcat = jnp.concatenate([v_ref[prows, :], v_ref[rows, :]], axis=0)
                valid = _dil_valid(n)
                m2 = l2 = num2 = None
                for hh in range(2):
                    mine = (lane >= 64) if hh else (lane < 64)
                    mine2 = (lane2 >= 64) if hh else (lane2 < 64)
                    s = _dot(jnp.where(mine, q_t, 0.0).astype(BF16), kcat, NT)
                    s = jnp.where(valid, s, NEG)
                    m = jnp.max(s, axis=1, keepdims=True)
                    p = jnp.exp(s - m)
                    l = jnp.sum(p, axis=1, keepdims=True)
                    num = _dot(p.astype(BF16), jnp.where(mine2, vcat, 0.0).astype(BF16))
                    if hh == 0:
                        m2, l2, num2 = m, l, num
                    else:
                        m2 = jnp.where(mine, m, m2)
                        l2 = jnp.where(mine, l, l2)
                        num2 = num2 + num
                if bi == 0:
                    m_s[rows, :] = m2 + jnp.zeros((BLOCK, LANES), F32)
                    l_s[rows, :] = l2 + jnp.zeros((BLOCK, LANES), F32)
                    n_s[rows, :] = num2
                else:
                    m_old = m_s[rows, :]
                    m_new = jnp.maximum(m_old, m2)
                    a = jnp.exp(m_old - m_new)
                    b = jnp.exp(m2 - m_new)
                    m_s[rows, :] = m_new
                    l_s[rows, :] = a * l_s[rows, :] + b * l2
                    n_s[rows, :] = a * n_s[rows, :] + b * num2
                return carry

            lax.fori_loop(0, n_tiles, tile, 0)
        o_ref[...] = n_s[...] / l_s[...]
        lse_ref[...] = m_s[...] + jnp.log(l_s[...])

    col = lambda off: pl.BlockSpec((seq, LANES), lambda p: (0, p + off))
    return pl.pallas_call(
        body, name="dil_fwd", grid=(4,),
        in_specs=[col(0), col(0), col(8)],
        out_specs=[col(0), pl.BlockSpec((None, seq, LANES), lambda p: (p, 0, 0))],
        out_shape=[jax.ShapeDtypeStruct((seq, 4 * LANES), F32), jax.ShapeDtypeStruct((4, seq, LANES), F32)],
        scratch_shapes=[pltpu.VMEM((seq, LANES), F32)] * 3,
        compiler_params=_cp(("arbitrary",), VMEM_LIMIT),
    )(qr, kr, qkv)


def _post(x, o_a, o_b, gates, w_out, ln_g, ln_b, target):
    seq = x.shape[0]
    tr = 512

    def body(x_ref, oa_ref, ob_ref, g_ref, w_ref, lg_ref, lb_ref, t_ref,
             dz_ref, do_ref, dg_ref, dw_ref, dlg_ref, dlb_ref, loss_ref):
        @pl.when(pl.program_id(0) == 0)
        def _():
            dw_ref[...] = jnp.zeros_like(dw_ref)
            dlg_ref[...] = jnp.zeros_like(dlg_ref)
            dlb_ref[...] = jnp.zeros_like(dlb_ref)
            loss_ref[...] = jnp.zeros_like(loss_ref)

        g = g_ref[...]
        sg = 1.0 / (1.0 + jnp.exp(-g))
        silu = g * sg
        o = jnp.concatenate([oa_ref[...], ob_ref[...]], axis=1)
        mixb = (o * silu).astype(BF16)
        w = w_ref[...]
        z = ALPHA * x_ref[...] + _dot(mixb, w)
        mu = jnp.mean(z, axis=-1, keepdims=True)
        zc = z - mu
        rstd = lax.rsqrt(jnp.mean(zc * zc, axis=-1, keepdims=True) + LN_EPS)
        xhat = zc * rstd
        lg = lg_ref[...]
        err = xhat * lg + lb_ref[...] - t_ref[...]
        loss_ref[...] += jnp.sum(err * err) * (0.5 / D_MODEL)
        dy = err * (1.0 / D_MODEL)
        dlg_ref[...] += jnp.sum(dy * xhat, axis=0, keepdims=True)
        dlb_ref[...] += jnp.sum(dy, axis=0, keepdims=True)
        dxh = dy * lg
        dz = rstd * (dxh - jnp.mean(dxh, axis=-1, keepdims=True) - xhat * jnp.mean(dxh * xhat, axis=-1, keepdims=True))
        dz_ref[...] = dz
        dzb = dz.astype(BF16)
        dmix = _dot(dzb, w, NT)
        do_ref[...] = dmix * silu
        dg_ref[...] = (dmix * o * (sg * (1.0 + g * (1.0 - sg)))).astype(BF16)
        dw_ref[...] += _dot(mixb, dzb, TN)

    row = lambda w: pl.BlockSpec((tr, w), lambda i: (i, 0))
    full = lambda s: pl.BlockSpec(s, lambda i: (0, 0))
    return pl.pallas_call(
        body, name="post", grid=(seq // tr,),
        in_specs=[row(D_MODEL), row(512), row(512), row(D_MODEL), full((D_MODEL, D_MODEL)), full((1, D_MODEL)),
                  full((1, D_MODEL)), row(D_MODEL)],
        out_specs=[row(D_MODEL), row(D_MODEL), row(D_MODEL), full((D_MODEL, D_MODEL)), full((1, D_MODEL)),
                   full((1, D_MODEL)), full((1, LANES))],
        out_shape=[jax.ShapeDtypeStruct((seq, D_MODEL), F32), jax.ShapeDtypeStruct((seq, D_MODEL), F32),
                   jax.ShapeDtypeStruct((seq, D_MODEL), BF16), jax.ShapeDtypeStruct((D_MODEL, D_MODEL), F32),
                   jax.ShapeDtypeStruct((1, D_MODEL), F32), jax.ShapeDtypeStruct((1, D_MODEL), F32),
                   jax.ShapeDtypeStruct((1, LANES), F32)],
        compiler_params=_cp(("arbitrary",), VMEM_LIMIT),
    )(x, o_a, o_b, gates, w_out, ln_g, ln_b, target)


def _mla_bwd(q, k, v, d_o, o, lse):
    seq = q.shape[1]
    tq = 512
    nq = seq // tq

    def body(q_ref, k_ref, v_ref, do_ref, o_ref, lse_ref, dq_ref, dk_ref, dv_ref, d_s, dk_s, dv_s):
        h = pl.program_id(0)
        j = pl.program_id(1)
        lane = lax.broadcasted_iota(jnp.int32, (tq, LANES), 1)
        first_lane = (h % 2) * 64
        mine = (lane >= first_lane) & (lane < first_lane + 64)
        row = lax.broadcasted_iota(jnp.int32, (tq, tq), 0)
        col = lax.broadcasted_iota(jnp.int32, (tq, tq), 1)

        @pl.when(j == 0)
        def _():
            dq_ref[...] = jnp.zeros_like(dq_ref)

            def rowsum(i, carry):
                rows = pl.ds(pl.multiple_of(i * tq, tq), tq)
                prod = jnp.where(mine, do_ref[rows, :] * o_ref[rows, :], 0.0)
                d_s[rows, :] = jnp.sum(prod, axis=1, keepdims=True) + jnp.zeros((tq, LANES), F32)
                return carry

            lax.fori_loop(0, nq, rowsum, 0)

        kb, vb = k_ref[...], v_ref[...]
        dk_s[...] = jnp.zeros_like(dk_s)
        dv_s[...] = jnp.zeros_like(dv_s)

        def step(i, masked):
            rows = pl.ds(pl.multiple_of(i * tq, tq), tq)
            qb = q_ref[rows, :]
            dob = do_ref[rows, :].astype(BF16)
            s = _dot(qb, kb, NT)
            p = jnp.exp(s - lse_ref[rows, :][:, :1])
            if masked:
                p = jnp.where(col <= row, p, 0.0)
            dv_s[...] += _dot(p.astype(BF16), dob, TN)
            dp = _dot(dob, vb, NT)
            ds = (p * (dp - d_s[rows, :][:, :1])).astype(BF16)
            dk_s[...] += _dot(ds, qb, TN)
            dq_ref[rows, :] += _dot(ds, kb)

        def full_step(i, carry):
            step(i, False)
            return carry

        step(j, True)
        lax.fori_loop(j + 1, nq, full_step, 0)
        dk_ref[...] = dk_s[...]
        dv_ref[...] = dv_s[...]

    whole = pl.BlockSpec((None, seq, LANES), lambda h, j: (h, 0, 0))
    blk = pl.BlockSpec((None, tq, LANES), lambda h, j: (h, j, 0))
    pair = pl.BlockSpec((seq, LANES), lambda h, j: (0, h // 2))
    shape = jax.ShapeDtypeStruct((MLA_HEADS, seq, LANES), F32)
    return pl.pallas_call(
        body, name="mla_bwd", grid=(MLA_HEADS, nq),
        in_specs=[whole, blk, blk, pair, pair, whole],
        out_specs=[whole, blk, blk], out_shape=[shape] * 3,
        scratch_shapes=[pltpu.VMEM((seq, LANES), F32), pltpu.VMEM((tq, LANES), F32), pltpu.VMEM((tq, LANES), F32)],
        compiler_params=_cp(("arbitrary", "arbitrary"), VMEM_LIMIT),
    )(q, k, v, d_o, o, lse)


def _dil_bwd(qr, kr, qkv, d_o, o, lse):
    seq = qr.shape[0]
    n_tiles = seq // BLOCK
    chunk = 512

    def body(q_ref, k_ref, v_ref, do_ref, o_ref, lse_ref, dq_ref, dk_ref, dv_ref, d_s, dq_s, dk_s, dv_s):
        lane = lax.broadcasted_iota(jnp.int32, (BLOCK, LANES), 1)
        lanec = lax.broadcasted_iota(jnp.int32, (chunk, LANES), 1)

        def rowsum(i, carry):
            rows = pl.ds(pl.multiple_of(i * chunk, chunk), chunk)
            prod = do_ref[rows, :] * o_ref[rows, :]
            lo = jnp.sum(jnp.where(lanec < 64, prod, 0.0), axis=1, keepdims=True)
            hi = jnp.sum(jnp.where(lanec >= 64, prod, 0.0), axis=1, keepdims=True)
            d_s[rows, :] = jnp.where(lanec < 64, lo, hi)
            return carry

        lax.fori_loop(0, seq // chunk, rowsum, 0)
        dq_s[...] = jnp.zeros_like(dq_s)
        dk_s[...] = jnp.zeros_like(dk_s)
        dv_s[...] = jnp.zeros_like(dv_s)
        for d in DIL_DILATIONS:

            def tile(t, carry, d=d):
                n, start, prev = _dil_tile_index(t, d, seq)
                rows, prows = _dil_rows(start, d), _dil_rows(prev, d)
                q_t = q_ref[rows, :]
                do_t = do_ref[rows, :]
                lse_t = lse_ref[rows, :]
                d_t = d_s[rows, :]
                kcat = jnp.concatenate([k_ref[prows, :], k_ref[rows, :]], axis=0).astype(BF16)
                vcat = jnp.concatenate([v_ref[prows, :], v_ref[rows, :]], axis=0).astype(BF16)
                valid = _dil_valid(n)
                dq_t = jnp.zeros((BLOCK, LANES), F32)
                dkcat = jnp.zeros((2 * BLOCK, LANES), F32)
                dvcat = jnp.zeros((2 * BLOCK, LANES), F32)
                for hh in range(2):
                    mine = (lane >= 64) if hh else (lane < 64)
                    c0 = 64 * hh
                    qh = jnp.where(mine, q_t, 0.0).astype(BF16)
                    doh = jnp.where(mine, do_t, 0.0).astype(BF16)
                    s = _dot(qh, kcat, NT)
                    p = jnp.where(valid, jnp.exp(s - lse_t[:, c0:c0 + 1]), 0.0)
                    dvcat = dvcat + _dot(p.astype(BF16), doh, TN)
                    dp = _dot(doh, vcat, NT)
                    ds = (p * (dp - d_t[:, c0:c0 + 1])).astype(BF16)
                    dq_t = dq_t + jnp.where(mine, _dot(ds, kcat), 0.0)
                    dkcat = dkcat + _dot(ds, qh, TN)
                dq_s[rows, :] += dq_t
                dk_s[prows, :] += dkcat[:BLOCK]
                dk_s[rows, :] += dkcat[BLOCK:]
                dv_s[prows, :] += dvcat[:BLOCK]
                dv_s[rows, :] += dvcat[BLOCK:]
                return carry

            lax.fori_loop(0, n_tiles, tile, 0)
        dq_ref[...] = dq_s[...].astype(BF16)
        dk_ref[...] = dk_s[...].astype(BF16)
        dv_ref[...] = dv_s[...].astype(BF16)

    col = lambda off: pl.BlockSpec((seq, LANES), lambda p: (0, p + off))
    shape = jax.ShapeDtypeStruct((seq, 4 * LANES), BF16)
    return pl.pallas_call(
        body, name="dil_bwd", grid=(4,),
        in_specs=[col(0), col(0), col(8), col(4), col(0), pl.BlockSpec((None, seq, LANES), lambda p: (p, 0, 0))],
        out_specs=[col(0)] * 3, out_shape=[shape] * 3,
        scratch_shapes=[pltpu.VMEM((seq, LANES), F32)] * 4,
        compiler_params=_cp(("arbitrary",), VMEM_LIMIT),
    )(qr, kr, qkv, d_o, o, lse)


def _mla_pre_bwd(cq, ckv, gq, gkv, wuq_e, wukv, ct, st, dq, dk, dv):
    seq = cq.shape[0]
    tr = 512

    def body(cq_ref, ckv_ref, gq_ref, gkv_ref, wuq_ref, wukv_ref, ct_ref, st_ref, dq_ref, dk_ref, dv_ref,
             dcq_ref, dckv_ref, dkr_ref, dwuq_ref, dwukv_ref, dgq_ref, dgkv_ref):
        @pl.when(pl.program_id(0) == 0)
        def _():
            dwuq_ref[...] = jnp.zeros_like(dwuq_ref)
            dwukv_ref[...] = jnp.zeros_like(dwukv_ref)
            dgq_ref[...] = jnp.zeros_like(dgq_ref)
            dgkv_ref[...] = jnp.zeros_like(dgkv_ref)

        lane = lax.broadcasted_iota(jnp.int32, (tr, LANES), 1)
        rope_lanes = jnp.logical_and(lane >= 64, lane < 96)
        ct_, st_ = ct_ref[...], st_ref[...]

        def rope_t(g):
            return ct_ * g + jnp.where(rope_lanes, _mla_rot(st_ * g, lane), 0.0)

        def norm_bwd(c, g, dn, dg_ref):
            r, _ = _rms(c, g)
            u = dn * g
            dg_ref[...] += jnp.sum(dn * c * r, axis=0, keepdims=True)
            return r * u - c * (r * r * r) * jnp.mean(u * c, axis=-1, keepdims=True)

        c, g = cq_ref[...], gq_ref[...]
        _, qn = _rms(c, g)
        dq_all = jnp.concatenate([rope_t(dq_ref[h] * MLA_SCALE) for h in range(MLA_HEADS)], axis=1).astype(BF16)
        dwuq_ref[...] += _dot(qn.astype(BF16), dq_all, TN)
        dcq_ref[...] = norm_bwd(c, g, _dot(dq_all, wuq_ref[...], NT), dgq_ref).astype(BF16)

        c, g = ckv_ref[...], gkv_ref[...]
        _, kvn = _rms(c, g)
        dkpe = jnp.zeros((tr, LANES), F32)
        parts = []
        for h in range(MLA_HEADS):
            dk_h, dv_h = dk_ref[h], dv_ref[h]
            if h % 2 == 0:
                dv_h = pltpu.roll(dv_h, 64, 1)
            parts.append(jnp.where(lane < 64, dk_h, dv_h))
            dkpe = dkpe + jnp.where(rope_lanes, dk_h, 0.0)
        dkv_all = jnp.concatenate(parts, axis=1).astype(BF16)
        dwukv_ref[...] += _dot(kvn.astype(BF16), dkv_all, TN)
        dckv_ref[...] = norm_bwd(c, g, _dot(dkv_all, wukv_ref[...], NT), dgkv_ref).astype(BF16)
        dkr_ref[...] = rope_t(dkpe).astype(BF16)

    row = lambda w: pl.BlockSpec((tr, w), lambda i: (i, 0))
    full = lambda a: pl.BlockSpec(a.shape, lambda i: (0,) * a.ndim)
    head = pl.BlockSpec((MLA_HEADS, tr, LANES), lambda i: (0, i, 0))
    return pl.pallas_call(
        body, name="mla_pre_bwd", grid=(seq // tr,),
        in_specs=[row(Q_LORA), row(KV_LORA), full(gq), full(gkv), full(wuq_e), full(wukv), row(LANES), row(LANES),
                  head, head, head],
        out_specs=[row(Q_LORA), row(KV_LORA), row(LANES), full(wuq_e), full(wukv), full(gq), full(gkv)],
        out_shape=[jax.ShapeDtypeStruct((seq, Q_LORA), BF16), jax.ShapeDtypeStruct((seq, KV_LORA), BF16),
                   jax.ShapeDtypeStruct((seq, LANES), BF16), jax.ShapeDtypeStruct(wuq_e.shape, F32),
                   jax.ShapeDtypeStruct(wukv.shape, F32), jax.ShapeDtypeStruct(gq.shape, F32),
                   jax.ShapeDtypeStruct(gkv.shape, F32)],
        compiler_params=_cp(("arbitrary",), VMEM_LIMIT),
    )(cq, ckv, gq, gkv, wuq_e, wukv, ct, st, dq, dk, dv)


def _in_bwd(dz, dcq, dckv, dgates, dqr, dkr, dvb, dkrope, cd, sd, w_in_p):
    seq = dz.shape[0]
    tr = 512

    def body(dz_ref, dcq_ref, dckv_ref, dg_ref, dqr_ref, dkr_ref, dvb_ref, dkp_ref, cd_ref, sd_ref, w_ref, gx_ref, dh_ref):
        lane = lax.broadcasted_iota(jnp.int32, (tr, LANES), 1)
        rot_lanes = lane % 64 < DIL_ROT
        cd_, sd_ = cd_ref[...], sd_ref[...]

        def rope_t(g):
            return cd_ * g + jnp.where(rot_lanes, _dil_rot(sd_ * g, lane), 0.0)

        dq = [rope_t(dqr_ref[:, LANES * p:LANES * (p + 1)].astype(F32) * DIL_SCALE).astype(BF16) for p in range(4)]
        dk = [rope_t(dkr_ref[:, LANES * p:LANES * (p + 1)].astype(F32)).astype(BF16) for p in range(4)]
        dh = jnp.concatenate([dcq_ref[...], dckv_ref[...], dg_ref[...]] + dq + dk + [dvb_ref[...], dkp_ref[...]], axis=1)
        dh_ref[...] = dh
        gx_ref[...] = ALPHA * dz_ref[...] + _dot(dh, w_ref[...], NT)

    row = lambda w: pl.BlockSpec((tr, w), lambda i: (i, 0))
    return pl.pallas_call(
        body, name="in_bwd", grid=(seq // tr,),
        in_specs=[row(D_MODEL), row(Q_LORA), row(KV_LORA), row(D_MODEL), row(512), row(512), row(512), row(LANES),
                  row(LANES), row(LANES), pl.BlockSpec((D_MODEL, IN_WIDTH_PAD), lambda i: (0, 0))],
        out_specs=[row(D_MODEL), row(IN_WIDTH_PAD)],
        out_shape=[jax.ShapeDtypeStruct((seq, D_MODEL), F32), jax.ShapeDtypeStruct((seq, IN_WIDTH_PAD), BF16)],
        compiler_params=_cp(("arbitrary",), VMEM_LIMIT),
    )(dz, dcq, dckv, dgates, dqr, dkr, dvb, dkrope, cd, sd, w_in_p)


def _dw_in(x_t, dh):
    seq = dh.shape[0]
    tk = 512
    tn = IN_WIDTH_PAD // 2

    def body(x_ref, dh_ref, o_ref):
        @pl.when(pl.program_id(1) == 0)
        def _():
            o_ref[...] = jnp.zeros_like(o_ref)

        o_ref[...] += _dot(x_ref[...], dh_ref[...])

    return pl.pallas_call(
        body, name="dw_in", grid=(2, seq // tk),
        in_specs=[pl.BlockSpec((D_MODEL, tk), lambda n, k: (0, k)), pl.BlockSpec((tk, tn), lambda n, k: (k, n))],
        out_specs=pl.BlockSpec((D_MODEL, tn), lambda n, k: (0, n)),
        out_shape=jax.ShapeDtypeStruct((D_MODEL, IN_WIDTH_PAD), F32),
        compiler_params=_cp(("arbitrary", "arbitrary"), VMEM_LIMIT),
    )(x_t, dh)


def _adamw(w, g, m, v, name):
    rows, cols = w.shape
    tr = 256 if rows % 256 == 0 else rows

    def body(w_ref, g_ref, m_ref, v_ref, d_ref, nm_ref, nv_ref):
        g_ = g_ref[...]
        nm = ADAM_B1 * m_ref[...] + (1.0 - ADAM_B1) * g_
        nv = ADAM_B2 * v_ref[...] + (1.0 - ADAM_B2) * jnp.square(g_)
        m_hat = nm / (1.0 - ADAM_B1 ** ADAM_STEP)
        v_hat = nv / (1.0 - ADAM_B2 ** ADAM_STEP)
        d_ref[...] = -ADAM_LR * (m_hat / (jnp.sqrt(v_hat) + ADAM_EPS) + ADAM_WD * w_ref[...])
        nm_ref[...] = nm
        nv_ref[...] = nv

    spec = pl.BlockSpec((tr, cols), lambda i: (i, 0))
    return pl.pallas_call(
        body, name=name, grid=(rows // tr,), in_specs=[spec] * 4, out_specs=[spec] * 3,
        out_shape=[jax.ShapeDtypeStruct(w.shape, F32)] * 3, compiler_params=_cp(("arbitrary",)),
    )(w, g, m, v)


def _pad_row(v):
    return jnp.pad(v.reshape(1, -1), ((0, 0), (0, D_MODEL - v.shape[-1])))


def _local_step(x2, target, w_in_f, w_uq_f, wukv_f, w_out_f, q_norm_g, kv_norm_g, ln_g, ln_b):
    seq = x2.shape[0]
    w_in_p = _permute_w_in(w_in_f)
    wuq_e = jnp.pad(w_uq_f.reshape(Q_LORA, MLA_HEADS, 96), ((0, 0), (0, 0), (0, 32))).reshape(Q_LORA, MLA_HEADS * LANES)
    ct, st, cd, sd = _rope_tables(seq)
    gq = q_norm_g.reshape(1, Q_LORA)
    gkv = kv_norm_g.reshape(1, KV_LORA)

    cq, ckv, gates, qkv, kr = _proj(x2, w_in_p)
    q_e, k_e, v_e = _mla_pre(cq, ckv, kr, gq, gkv, wuq_e, wukv_f, ct, st)
    o_a, lse_a = _mla_fwd(q_e, k_e, v_e)
    qr, krot = _dil_pre(qkv, cd, sd)
    o_b, lse_b = _dil_fwd(qr, krot, qkv)

    dz, d_o, d_gates, dw_out, dln_g, dln_b, loss_part = _post(
        x2, o_a, o_b, gates, w_out_f, ln_g.reshape(1, D_MODEL), ln_b.reshape(1, D_MODEL), target)
    dq_e, dk_e, dv_e = _mla_bwd(q_e, k_e, v_e, d_o, o_a, lse_a)
    dqr, dkr, dvb = _dil_bwd(qr, krot, qkv, d_o, o_b, lse_b)
    dcq, dckv, dkrope, dwuq_e, dwukv, dgq, dgkv = _mla_pre_bwd(cq, ckv, gq, gkv, wuq_e, wukv_f, ct, st, dq_e, dk_e, dv_e)
    grad_x, dh = _in_bwd(dz, dcq, dckv, d_gates, dqr, dkr, dvb, dkrope, cd, sd, w_in_p)
    dw_in = _unpermute_dw_in(_dw_in(x2.T.astype(BF16), dh))
    dw_uq = dwuq_e.reshape(Q_LORA, MLA_HEADS, LANES)[:, :, :96].reshape(Q_LORA, MLA_HEADS * 96)
    return loss_part, grad_x, dw_in, dw_uq, dwukv, dw_out, dgq, dgkv, dln_g, dln_b


def kernel(x, w_in, q_norm_g, kv_norm_g, w_uq, w_ukv, w_out, ln_g, ln_b, loss_target, m_w_in, m_q_norm_g, m_kv_norm_g, m_w_uq, m_w_ukv, m_w_out, m_ln_g, m_ln_b, v_w_in, v_q_norm_g, v_kv_norm_g, v_w_uq, v_w_ukv, v_w_out, v_ln_g, v_ln_b):
    seq = x.shape[1]
    x2 = x.reshape(seq, D_MODEL)
    target = loss_target.reshape(seq, D_MODEL)

    g_w_in, g_w_uq, g_w_ukv, g_w_out = _all_gather_weights([w_in, w_uq, w_ukv, w_out])
    by_cols = lambda g: jnp.concatenate([g[j] for j in range(N_SHARD)], axis=1)
    loss_part, grad_x, dw_in, dw_uq, dwukv, dw_out, dgq, dgkv, dln_g, dln_b = _local_step(
        x2, target, by_cols(g_w_in), by_cols(g_w_uq), by_cols(g_w_ukv), g_w_out.reshape(D_MODEL, D_MODEL),
        q_norm_g, kv_norm_g, ln_g, ln_b)

    to_shards = lambda d: d.reshape(d.shape[0], N_SHARD, d.shape[1] // N_SHARD).transpose(1, 0, 2)
    grads = [to_shards(dw_in), to_shards(dw_uq), to_shards(dwukv), dw_out.reshape(N_SHARD, 256, D_MODEL)]
    small = jnp.concatenate([_pad_row(dgq), _pad_row(dgkv), dln_g, dln_b, _pad_row(loss_part),
                             jnp.zeros((3, D_MODEL), F32)], axis=0)
    *chip_sums, smalls = _reduce_over_sibling(grads, small)
    g_in, g_uq, g_ukv, g_out = _reduce_over_chips(chip_sums)
    small_sum = _sum_smalls(smalls)
    loss = small_sum[4, 0]

    big = [_adamw(w, g, m, v, name) for w, g, m, v, name in (
        (w_in, g_in, m_w_in, v_w_in, "adamw_w_in"), (w_uq, g_uq, m_w_uq, v_w_uq, "adamw_w_uq"),
        (w_ukv, g_ukv, m_w_ukv, v_w_ukv, "adamw_w_ukv"), (w_out, g_out, m_w_out, v_w_out, "adamw_w_out"))]
    vec = lambda a, b, c_, d: jnp.concatenate([_pad_row(a), _pad_row(b), _pad_row(c_), _pad_row(d),
                                               jnp.zeros((4, D_MODEL), F32)], axis=0)
    sw = vec(q_norm_g, kv_norm_g, ln_g, ln_b)
    sm = vec(m_q_norm_g, m_kv_norm_g, m_ln_g, m_ln_b)
    sv = vec(v_q_norm_g, v_kv_norm_g, v_ln_g, v_ln_b)
    sg = jnp.concatenate([small_sum[:4], jnp.zeros((4, D_MODEL), F32)], axis=0)
    s_delta, s_m, s_v = _adamw(sw, sg, sm, sv, "adamw_vectors")

    def vectors(a):
        return [a[0, :Q_LORA], a[1, :KV_LORA], a[2], a[3]]

    def ordered(bigs, smalls_):
        return [bigs[0], smalls_[0], smalls_[1], bigs[1], bigs[2], bigs[3], smalls_[2], smalls_[3]]

    grads_out = ordered([g_in, g_uq, g_ukv, g_out], vectors(small_sum))
    deltas = ordered([b[0] for b in big], vectors(s_delta))
    new_m = ordered([b[1] for b in big], vectors(s_m))
    new_v = ordered([b[2] for b in big], vectors(s_v))
    return (loss, grad_x.reshape(x.shape), *grads_out, *deltas, *new_m, *new_v)
```

```python
import functools

import jax
import jax.numpy as jnp
from jax import lax
from jax.experimental import pallas as pl
from jax.experimental.pallas import tpu as pltpu

F32 = jnp.float32
BF16 = jnp.bfloat16

D_MODEL = 1024
ROPE_THETA = 500000.0
BLOCK = 128
NEG = -1e30
RMS_EPS = 1e-6
LN_EPS = 1e-5

MLA_HEADS = 8
MLA_NOPE = 64
MLA_ROPE = 32
Q_LORA = 384
KV_LORA = 256
DIL_HEADS = 8
DIL_HEAD_DIM = 64
DIL_ROT = 16
DIL_DILATIONS = (1, 4, 16)
IN_WIDTH = 3232
IN_WIDTH_PAD = 3328
MLA_SCALE = (MLA_NOPE + MLA_ROPE) ** -0.5
DIL_SCALE = DIL_HEAD_DIM ** -0.5
ALPHA = 2.0 ** 0.25

ADAM_LR = 0.001
ADAM_B1 = 0.9
ADAM_B2 = 0.999
ADAM_EPS = 1e-08
ADAM_WD = 0.01
ADAM_STEP = 10

N_SHARD = 4
SHARD_SHAPES = ((1024, 808), (384, 192), (256, 256), (256, 1024))
ROW_CHUNK = 64
LANES = 128
VMEM_LIMIT = 56 * 1024 * 1024
MESH = pl.DeviceIdType.MESH

NT = (((1,), (1,)), ((), ()))
TN = (((0,), (0,)), ((), ()))


def _cp(sem=None, vmem=None):
    return pltpu.CompilerParams(dimension_semantics=sem, vmem_limit_bytes=vmem)


def _dot(a, b, dims=None):
    if dims is None:
        return jnp.dot(a, b, preferred_element_type=F32)
    return lax.dot_general(a, b, dims, preferred_element_type=F32)


def _rope_tables(seq):
    pos = jnp.arange(seq, dtype=F32)[:, None]
    inv = ROPE_THETA ** (-jnp.arange(0, MLA_ROPE, 2, dtype=F32) / MLA_ROPE)
    ang = pos * inv[None, :]
    cos, sin = jnp.cos(ang), jnp.sin(ang)
    one, zero = jnp.ones((seq, 64), F32), jnp.zeros((seq, 64), F32)
    ct = jnp.concatenate([one, cos, cos, zero[:, :32]], axis=1)
    st = jnp.concatenate([zero, -sin, sin, zero[:, :32]], axis=1)
    inv = ROPE_THETA ** (-jnp.arange(0, DIL_ROT, 2, dtype=F32) / DIL_ROT)
    ang = pos * inv[None, :]
    cos, sin = jnp.cos(ang), jnp.sin(ang)
    cd = jnp.concatenate([cos, cos, one[:, :48]], axis=1)
    sd = jnp.concatenate([-sin, sin, zero[:, :48]], axis=1)
    return ct, st, jnp.tile(cd, (1, 2)), jnp.tile(sd, (1, 2))


def _permute_w_in(w):
    z = jnp.zeros((w.shape[0], 64), w.dtype)
    return jnp.concatenate([w[:, 0:640], w[:, 672:1184], w[:, 2720:3232], w[:, 1184:2720],
                            z, w[:, 640:672], z[:, :32]], axis=1)


def _unpermute_dw_in(dw):
    return jnp.concatenate([dw[:, 0:640], dw[:, 3264:3296], dw[:, 640:1152], dw[:, 1664:3200],
                            dw[:, 1152:1664]], axis=1)


def _position():
    return lax.axis_index("x"), lax.axis_index("y"), lax.axis_index("c")


def _halves(c, rows):
    hr = rows // 2
    return pl.ds(pl.multiple_of(c * hr, 8), hr), pl.ds(pl.multiple_of((1 - c) * hr, 8), hr)


def _for_row_chunks(rows, fn):
    def step(i, carry):
        fn(pl.multiple_of(i * ROW_CHUNK, ROW_CHUNK))
        return carry

    lax.fori_loop(0, rows // ROW_CHUNK, step, 0)


def _all_gather_weights(shards):
    n = len(shards)

    def body(*refs):
        ins, outs = refs[:n], refs[n:2 * n]
        send_sems, recv_sems = refs[2 * n:]
        x, y, c = _position()
        me = 2 * x + y
        chips = [(1 - x, y), (x, 1 - y), (1 - x, 1 - y)]
        for a in range(n):
            def cast(r, a=a):
                outs[a][me, pl.ds(r, ROW_CHUNK), :] = ins[a][pl.ds(r, ROW_CHUNK), :].astype(BF16)

            _for_row_chunks(SHARD_SHAPES[a][0], cast)

        def copy(k, a, slot, rows, to):
            ref = outs[a].at[slot, rows]
            return pltpu.make_async_remote_copy(
                src_ref=ref, dst_ref=ref, send_sem=send_sems.at[k * n + a], recv_sem=recv_sems.at[k * n + a],
                device_id=to, device_id_type=MESH)

        half = [_halves(c, SHARD_SHAPES[a][0])[0] for a in range(n)]
        other = [_halves(c, SHARD_SHAPES[a][0])[1] for a in range(n)]
        first = [copy(k, a, me, half[a], (px, py, c)) for k, (px, py) in enumerate(chips) for a in range(n)]
        for cp in first:
            cp.start()
        passed = []
        for k, (px, py) in enumerate(chips):
            for a in range(n):
                copy(k, a, 2 * px + py, half[a], (x, y, c)).wait_recv()
                cp = copy(3 + k, a, 2 * px + py, half[a], (x, y, 1 - c))
                cp.start()
                passed.append(cp)
        for k, (px, py) in enumerate(chips):
            for a in range(n):
                copy(3 + k, a, 2 * px + py, other[a], (x, y, c)).wait_recv()
        for cp in first + passed:
            cp.wait_send()

    vmem = pl.BlockSpec(memory_space=pltpu.VMEM)
    return pl.pallas_call(
        body, name="all_gather_weights",
        out_shape=[jax.ShapeDtypeStruct((N_SHARD,) + s, BF16) for s in SHARD_SHAPES],
        in_specs=[vmem] * n, out_specs=[vmem] * n,
        scratch_shapes=[pltpu.SemaphoreType.DMA((6 * n,)), pltpu.SemaphoreType.DMA((6 * n,))],
        compiler_params=_cp(None, VMEM_LIMIT),
    )(*shards)


def _reduce_over_sibling(grads, small):
    n = len(grads)

    def body(*refs):
        g_hbm, sm = refs[:n], refs[n]
        sums, smalls = refs[n + 1:2 * n + 1], refs[2 * n + 1]
        stage, got = refs[2 * n + 2:3 * n + 2], refs[3 * n + 2:4 * n + 2]
        send_sems, recv_sems, local_sems = refs[4 * n + 2:]
        x, y, c = _position()
        me = 4 * x + 2 * y + c
        loads = [pltpu.make_async_copy(g_hbm[a], stage[a], local_sems.at[a]) for a in range(n)]
        for ld in loads:
            ld.start()
        smalls[me] = sm[...]
        sends = []
        for rel in range(1, 8):
            px = 1 - x if rel // 4 else x
            py = 1 - y if (rel // 2) % 2 else y
            pc = 1 - c if rel % 2 else c
            cp = pltpu.make_async_remote_copy(
                src_ref=sm, dst_ref=smalls.at[me], send_sem=send_sems.at[n + rel], recv_sem=recv_sems.at[n + rel],
                device_id=(px, py, pc), device_id_type=MESH)
            cp.start()
            sends.append((cp, 4 * px + 2 * py + pc))
        swaps = []
        for a in range(n):
            loads[a].wait()
            _, other = _halves(c, SHARD_SHAPES[a][0])
            cp = pltpu.make_async_remote_copy(
                src_ref=stage[a].at[:, other], dst_ref=got[a], send_sem=send_sems.at[a], recv_sem=recv_sems.at[a],
                device_id=(x, y, 1 - c), device_id_type=MESH)
            cp.start()
            swaps.append(cp)
        for a in range(n):
            swaps[a].wait_recv()
            hr = SHARD_SHAPES[a][0] // 2
            for k in range(N_SHARD):
                def add(r, a=a, k=k, hr=hr):
                    mine = stage[a][k, pl.ds(pl.multiple_of(c * hr + r, ROW_CHUNK), ROW_CHUNK), :]
                    sums[a][k, pl.ds(r, ROW_CHUNK), :] = (mine + got[a][k, pl.ds(r, ROW_CHUNK), :]).astype(BF16)

                _for_row_chunks(hr, add)
        for rel, (cp, peer) in enumerate(sends, start=1):
            pltpu.make_async_remote_copy(
                src_ref=sm, dst_ref=smalls.at[peer], send_sem=send_sems.at[n + rel], recv_sem=recv_sems.at[n + rel],
                device_id=(x, y, c), device_id_type=MESH).wait_recv()
        for cp in swaps:
            cp.wait_send()
        for cp, _ in sends:
            cp.wait_send()

    vmem = pl.BlockSpec(memory_space=pltpu.VMEM)
    half = [(N_SHARD, r // 2, cols) for r, cols in SHARD_SHAPES]
    return pl.pallas_call(
        body, name="reduce_over_sibling",
        out_shape=[jax.ShapeDtypeStruct(s, BF16) for s in half] + [jax.ShapeDtypeStruct((8,) + small.shape, F32)],
        in_specs=[pl.BlockSpec(memory_space=pl.ANY)] * n + [vmem], out_specs=[vmem] * (n + 1),
        scratch_shapes=[pltpu.VMEM((N_SHARD,) + s, F32) for s in SHARD_SHAPES] + [pltpu.VMEM(s, F32) for s in half]
        + [pltpu.SemaphoreType.DMA((n + 8,)), pltpu.SemaphoreType.DMA((n + 8,)), pltpu.SemaphoreType.DMA((n,))],
        compiler_params=_cp(None, VMEM_LIMIT),
    )(*grads, small)


def _reduce_over_chips(sums):
    n = len(sums)

    def body(*refs):
        h, outs, got = refs[:n], refs[n:2 * n], refs[2 * n:3 * n]
        send_sems, recv_sems = refs[3 * n:]
        x, y, c = _position()
        me = 2 * x + y
        chips = [(1 - x, y), (x, 1 - y), (1 - x, 1 - y)]
        sends = []
        for k, (px, py) in enumerate(chips):
            for a in range(n):
                cp = pltpu.make_async_remote_copy(
                    src_ref=h[a].at[2 * px + py], dst_ref=got[a].at[k], send_sem=send_sems.at[k * n + a],
                    recv_sem=recv_sems.at[k * n + a], device_id=(px, py, c), device_id_type=MESH)
                cp.start()
                sends.append(cp)
        for cp in sends:
            cp.wait_recv()
        joins = []
        for a in range(n):
            hr = SHARD_SHAPES[a][0] // 2
            half, other = _halves(c, SHARD_SHAPES[a][0])

            def add(r, a=a, hr=hr):
                rows = pl.ds(r, ROW_CHUNK)
                total = h[a][me, rows, :].astype(F32)
                for k in range(3):
                    total = total + got[a][k, rows, :].astype(F32)
                outs[a][pl.ds(pl.multiple_of(c * hr + r, ROW_CHUNK), ROW_CHUNK), :] = total

            _for_row_chunks(hr, add)
            cp = pltpu.make_async_remote_copy(
                src_ref=outs[a].at[half], dst_ref=outs[a].at[half], send_sem=send_sems.at[3 * n + a],
                recv_sem=recv_sems.at[3 * n + a], device_id=(x, y, 1 - c), device_id_type=MESH)
            cp.start()
            joins.append(cp)
        for a in range(n):
            other = _halves(c, SHARD_SHAPES[a][0])[1]
            pltpu.make_async_remote_copy(
                src_ref=outs[a].at[other], dst_ref=outs[a].at[other], send_sem=send_sems.at[3 * n + a],
                recv_sem=recv_sems.at[3 * n + a], device_id=(x, y, c), device_id_type=MESH).wait_recv()
        for cp in sends + joins:
            cp.wait_send()

    vmem = pl.BlockSpec(memory_space=pltpu.VMEM)
    return pl.pallas_call(
        body, name="reduce_over_chips",
        out_shape=[jax.ShapeDtypeStruct(s, F32) for s in SHARD_SHAPES],
        in_specs=[vmem] * n, out_specs=[vmem] * n,
        scratch_shapes=[pltpu.VMEM((3, r // 2, cols), BF16) for r, cols in SHARD_SHAPES]
        + [pltpu.SemaphoreType.DMA((4 * n,)), pltpu.SemaphoreType.DMA((4 * n,))],
        compiler_params=_cp(None, VMEM_LIMIT),
    )(*sums)


def _sum_smalls(smalls):
    def body(s, o):
        acc = s[0]
        for d in range(1, 8):
            acc = acc + s[d]
        o[...] = acc

    return pl.pallas_call(body, name="sum_smalls", out_shape=jax.ShapeDtypeStruct(smalls.shape[1:], F32))(smalls)


def _proj(x, w_in_p):
    seq = x.shape[0]
    tr = 512
    splits = ((0, 384), (384, 640), (640, 1664), (1664, 3200), (3200, 3328))

    def body(x_ref, w_ref, *outs):
        xb = x_ref[...].astype(BF16)
        for (lo, hi), o in zip(splits, outs):
            o[...] = _dot(xb, w_ref[:, lo:hi])

    return pl.pallas_call(
        body, name="proj", grid=(seq // tr,),
        in_specs=[pl.BlockSpec((tr, D_MODEL), lambda i: (i, 0)), pl.BlockSpec((D_MODEL, IN_WIDTH_PAD), lambda i: (0, 0))],
        out_specs=[pl.BlockSpec((tr, hi - lo), lambda i: (i, 0)) for lo, hi in splits],
        out_shape=[jax.ShapeDtypeStruct((seq, hi - lo), F32) for lo, hi in splits],
        compiler_params=_cp(("arbitrary",), VMEM_LIMIT),
    )(x, w_in_p)


def _mla_rot(t, lane):
    return jnp.where(lane < 80, pltpu.roll(t, 112, 1), pltpu.roll(t, 16, 1))


def _dil_rot(t, lane):
    return jnp.where(lane % 64 < 8, pltpu.roll(t, 120, 1), pltpu.roll(t, 8, 1))


def _rms(c, g):
    r = lax.rsqrt(jnp.mean(c * c, axis=-1, keepdims=True) + RMS_EPS)
    return r, c * r * g


def _mla_pre(cq, ckv, kr, gq, gkv, wuq_e, wukv, ct, st):
    seq = cq.shape[0]
    tr = 512

    def body(cq_ref, ckv_ref, kr_ref, gq_ref, gkv_ref, wuq_ref, wukv_ref, ct_ref, st_ref, q_out, k_out, v_out):
        lane = lax.broadcasted_iota(jnp.int32, (tr, LANES), 1)
        ct_, st_ = ct_ref[...], st_ref[...]

        def rope(t):
            return t * ct_ + _mla_rot(t, lane) * st_

        _, qn = _rms(cq_ref[...], gq_ref[...])
        q_all = _dot(qn.astype(BF16), wuq_ref[...])
        for h in range(MLA_HEADS):
            q_out[h] = (rope(q_all[:, LANES * h:LANES * (h + 1)]) * MLA_SCALE).astype(BF16)
        _, kvn = _rms(ckv_ref[...], gkv_ref[...])
        kv_all = _dot(kvn.astype(BF16), wukv_ref[...])
        kpe = rope(kr_ref[...])
        for h in range(MLA_HEADS):
            kv_h = kv_all[:, LANES * h:LANES * (h + 1)]
            k_out[h] = jnp.where(lane < 64, kv_h, kpe).astype(BF16)
            if h % 2:
                v = jnp.where(lane >= 64, kv_h, 0.0)
            else:
                v = jnp.where(lane < 64, pltpu.roll(kv_h, 64, 1), 0.0)
            v_out[h] = v.astype(BF16)

    row = lambda w: pl.BlockSpec((tr, w), lambda i: (i, 0))
    full = lambda a: pl.BlockSpec(a.shape, lambda i: (0,) * a.ndim)
    head = pl.BlockSpec((MLA_HEADS, tr, LANES), lambda i: (0, i, 0))
    return pl.pallas_call(
        body, name="mla_pre", grid=(seq // tr,),
        in_specs=[row(Q_LORA), row(KV_LORA), row(LANES), full(gq), full(gkv), full(wuq_e), full(wukv), row(LANES), row(LANES)],
        out_specs=[head] * 3,
        out_shape=[jax.ShapeDtypeStruct((MLA_HEADS, seq, LANES), BF16)] * 3,
        compiler_params=_cp(("arbitrary",), VMEM_LIMIT),
    )(cq, ckv, kr, gq, gkv, wuq_e, wukv, ct, st)


def _mla_fwd(q, k, v):
    seq = q.shape[1]
    tq = 512
    tk = 512
    nq = seq // tq

    def body(q_ref, k_ref, v_ref, o_ref, lse_ref, m_s, l_s, acc_s):
        i = pl.program_id(1)
        row = lax.broadcasted_iota(jnp.int32, (tq, tk), 0)
        col = lax.broadcasted_iota(jnp.int32, (tq, tk), 1)
        m_s[...] = jnp.full((2, tq, LANES), NEG, F32)
        l_s[...] = jnp.zeros((2, tq, LANES), F32)
        acc_s[...] = jnp.zeros((2, tq, LANES), F32)

        def step(j, masked):
            halves = [pl.ds(pl.multiple_of(j * tq + tk * b, tk), tk) for b in range(tq // tk)]
            for hh in range(2):
                qh = q_ref[hh]
                s = [_dot(qh, k_ref[hh, rows, :], NT) for rows in halves]
                if masked:
                    s = [jnp.where(col + tk * b <= row, sb, NEG) for b, sb in enumerate(s)]
                m_prev = m_s[hh]
                m_new = m_prev
                for sb in s:
                    m_new = jnp.maximum(m_new, jnp.max(sb, axis=1, keepdims=True))
                p = [jnp.exp(sb - m_new[:, :1]) for sb in s]
                alpha = jnp.exp(m_prev - m_new)
                l_new = alpha * l_s[hh]
                acc = alpha * acc_s[hh]
                for pb, rows in zip(p, halves):
                    l_new = l_new + jnp.sum(pb, axis=1, keepdims=True)
                    acc = acc + _dot(pb.astype(BF16), v_ref[hh, rows, :])
                l_s[hh] = l_new
                acc_s[hh] = acc
                m_s[hh] = m_new

        def full_step(j, carry):
            step(j, False)
            return carry

        lax.fori_loop(0, i, full_step, 0)
        step(i, True)
        o_ref[...] = acc_s[0] / l_s[0] + acc_s[1] / l_s[1]
        for hh in range(2):
            lse_ref[hh] = m_s[hh] + jnp.log(l_s[hh])

    kv_spec = pl.BlockSpec((2, seq, LANES), lambda p, i: (p, 0, 0))
    return pl.pallas_call(
        body, name="mla_fwd", grid=(MLA_HEADS // 2, nq),
        in_specs=[pl.BlockSpec((2, tq, LANES), lambda p, i: (p, i, 0)), kv_spec, kv_spec],
        out_specs=[pl.BlockSpec((tq, LANES), lambda p, i: (i, p)), pl.BlockSpec((2, tq, LANES), lambda p, i: (p, i, 0))],
        out_shape=[jax.ShapeDtypeStruct((seq, 4 * LANES), F32), jax.ShapeDtypeStruct((MLA_HEADS, seq, LANES), F32)],
        scratch_shapes=[pltpu.VMEM((2, tq, LANES), F32)] * 3,
        compiler_params=_cp(("arbitrary", "arbitrary"), VMEM_LIMIT),
    )(q, k, v)


def _dil_pre(qkv, cd, sd):
    seq = qkv.shape[0]
    tr = 512

    def body(q_ref, k_ref, cd_ref, sd_ref, qr_ref, kr_ref):
        lane = lax.broadcasted_iota(jnp.int32, (tr, LANES), 1)
        cd_, sd_ = cd_ref[...], sd_ref[...]
        for p in range(4):
            cols = slice(LANES * p, LANES * (p + 1))
            t = q_ref[:, cols]
            qr_ref[:, cols] = (t * cd_ + _dil_rot(t, lane) * sd_) * DIL_SCALE
            t = k_ref[:, cols]
            kr_ref[:, cols] = t * cd_ + _dil_rot(t, lane) * sd_

    blk = lambda j: pl.BlockSpec((tr, 4 * LANES), lambda i: (i, j))
    tab = pl.BlockSpec((tr, LANES), lambda i: (i, 0))
    return pl.pallas_call(
        body, name="dil_pre", grid=(seq // tr,),
        in_specs=[blk(0), blk(1), tab, tab], out_specs=[blk(0), blk(0)],
        out_shape=[jax.ShapeDtypeStruct((seq, 4 * LANES), F32)] * 2,
        compiler_params=_cp(("arbitrary",)),
    )(qkv, qkv, cd, sd)


def _dil_tile_index(t, d, seq):
    per_class = seq // (BLOCK * d)
    shift = per_class.bit_length() - 1
    r = t >> shift
    n = t & (per_class - 1)
    start = r + (BLOCK * d) * n
    prev = r + (BLOCK * d) * jnp.maximum(n - 1, 0)
    if d == 1:
        start = pl.multiple_of(start, BLOCK)
        prev = pl.multiple_of(prev, BLOCK)
    return n, start, prev


def _dil_rows(start, d):
    return pl.ds(start, BLOCK) if d == 1 else pl.ds(start, BLOCK, stride=d)


def _dil_valid(n):
    i = lax.broadcasted_iota(jnp.int32, (BLOCK, 2 * BLOCK), 0)
    j = lax.broadcasted_iota(jnp.int32, (BLOCK, 2 * BLOCK), 1)
    in_prev = (j < BLOCK) & (j >= i) & (n > 0)
    in_cur = (j >= BLOCK) & (j - BLOCK <= i)
    return in_prev | in_cur


def _dil_fwd(qr, kr, qkv):
    seq = qr.shape[0]
    n_tiles = seq // BLOCK

    def body(q_ref, k_ref, v_ref, o_ref, lse_ref, m_s, l_s, n_s):
        lane = lax.broadcasted_iota(jnp.int32, (BLOCK, LANES), 1)
        lane2 = lax.broadcasted_iota(jnp.int32, (2 * BLOCK, LANES), 1)
        for bi, d in enumerate(DIL_DILATIONS):

            def tile(t, carry, d=d, bi=bi):
                n, start, prev = _dil_tile_index(t, d, seq)
                rows, prows = _dil_rows(start, d), _dil_rows(prev, d)
                q_t = q_ref[rows, :]
                kcat = jnp.concatenate([k_ref[prows, :], k_ref[rows, :]], axis=0).astype(BF16)
                vcat = jnp.concatenate([v_ref[prows, :], v_ref[rows, :]], axis=0)
                valid = _dil_valid(n)
                m2 = l2 = num2 = None
                for hh in range(2):
                    mine = (lane >= 64) if hh else (lane < 64)
                    mine2 = (lane2 >= 64) if hh else (lane2 < 64)
                    s = _dot(jnp.where(mine, q_t, 0.0).astype(BF16), kcat, NT)
                    s = jnp.where(valid, s, NEG)
                    m = jnp.max(s, axis=1, keepdims=True)
                    p = jnp.exp(s - m)
                    l = jnp.sum(p, axis=1, keepdims=True)
                    num = _dot(p.astype(BF16), jnp.where(mine2, vcat, 0.0).astype(BF16))
                    if hh == 0:
                        m2, l2, num2 = m, l, num
                    else:
                        m2 = jnp.where(mine, m, m2)
                        l2 = jnp.where(mine, l, l2)
                        num2 = num2 + num
                if bi == 0:
                    m_s[rows, :] = m2 + jnp.zeros((BLOCK, LANES), F32)
                    l_s[rows, :] = l2 + jnp.zeros((BLOCK, LANES), F32)
                    n_s[rows, :] = num2
                else:
                    m_old = m_s[rows, :]
                    m_new = jnp.maximum(m_old, m2)
                    a = jnp.exp(m_old - m_new)
                    b = jnp.exp(m2 - m_new)
                    m_s[rows, :] = m_new
                    l_s[rows, :] = a * l_s[rows, :] + b * l2
                    n_s[rows, :] = a * n_s[rows, :] + b * num2
                return carry

            lax.fori_loop(0, n_tiles, tile, 0, unroll=4)
        o_ref[...] = n_s[...] / l_s[...]
        lse_ref[...] = m_s[...] + jnp.log(l_s[...])

    col = lambda off: pl.BlockSpec((seq, LANES), lambda p: (0, p + off))
    return pl.pallas_call(
        body, name="dil_fwd", grid=(4,),
        in_specs=[col(0), col(0), col(8)],
        out_specs=[col(0), pl.BlockSpec((None, seq, LANES), lambda p: (p, 0, 0))],
        out_shape=[jax.ShapeDtypeStruct((seq, 4 * LANES), F32), jax.ShapeDtypeStruct((4, seq, LANES), F32)],
        scratch_shapes=[pltpu.VMEM((seq, LANES), F32)] * 3,
        compiler_params=_cp(("arbitrary",), VMEM_LIMIT),
    )(qr, kr, qkv)


def _post(x, o_a, o_b, gates, w_out, ln_g, ln_b, target):
    seq = x.shape[0]
    tr = 512

    def body(x_ref, oa_ref, ob_ref, g_ref, w_ref, lg_ref, lb_ref, t_ref,
             dz_ref, do_ref, dg_ref, dw_ref, dlg_ref, dlb_ref, loss_ref):
        @pl.when(pl.program_id(0) == 0)
        def _():
            dw_ref[...] = jnp.zeros_like(dw_ref)
            dlg_ref[...] = jnp.zeros_like(dlg_ref)
            dlb_ref[...] = jnp.zeros_like(dlb_ref)
            loss_ref[...] = jnp.zeros_like(loss_ref)

        g = g_ref[...]
        sg = 1.0 / (1.0 + jnp.exp(-g))
        silu = g * sg
        o = jnp.concatenate([oa_ref[...], ob_ref[...]], axis=1)
        mixb = (o * silu).astype(BF16)
        w = w_ref[...]
        z = ALPHA * x_ref[...] + _dot(mixb, w)
        mu = jnp.mean(z, axis=-1, keepdims=True)
        zc = z - mu
        rstd = lax.rsqrt(jnp.mean(zc * zc, axis=-1, keepdims=True) + LN_EPS)
        xhat = zc * rstd
        lg = lg_ref[...]
        err = xhat * lg + lb_ref[...] - t_ref[...]
        loss_ref[...] += jnp.sum(err * err) * (0.5 / D_MODEL)
        dy = err * (1.0 / D_MODEL)
        dlg_ref[...] += jnp.sum(dy * xhat, axis=0, keepdims=True)
        dlb_ref[...] += jnp.sum(dy, axis=0, keepdims=True)
        dxh = dy * lg
        dz = rstd * (dxh - jnp.mean(dxh, axis=-1, keepdims=True) - xhat * jnp.mean(dxh * xhat, axis=-1, keepdims=True))
        dz_ref[...] = dz
        dzb = dz.astype(BF16)
        dmix = _dot(dzb, w, NT)
        do_ref[...] = dmix * silu
        dg_ref[...] = (dmix * o * (sg * (1.0 + g * (1.0 - sg)))).astype(BF16)
        dw_ref[...] += _dot(mixb, dzb, TN)

    row = lambda w: pl.BlockSpec((tr, w), lambda i: (i, 0))
    full = lambda s: pl.BlockSpec(s, lambda i: (0, 0))
    return pl.pallas_call(
        body, name="post", grid=(seq // tr,),
        in_specs=[row(D_MODEL), row(512), row(512), row(D_MODEL), full((D_MODEL, D_MODEL)), full((1, D_MODEL)),
                  full((1, D_MODEL)), row(D_MODEL)],
        out_specs=[row(D_MODEL), row(D_MODEL), row(D_MODEL), full((D_MODEL, D_MODEL)), full((1, D_MODEL)),
                   full((1, D_MODEL)), full((1, LANES))],
        out_shape=[jax.ShapeDtypeStruct((seq, D_MODEL), F32), jax.ShapeDtypeStruct((seq, D_MODEL), F32),
                   jax.ShapeDtypeStruct((seq, D_MODEL), BF16), jax.ShapeDtypeStruct((D_MODEL, D_MODEL), F32),
                   jax.ShapeDtypeStruct((1, D_MODEL), F32), jax.ShapeDtypeStruct((1, D_MODEL), F32),
                   jax.ShapeDtypeStruct((1, LANES), F32)],
        compiler_params=_cp(("arbitrary",), VMEM_LIMIT),
    )(x, o_a, o_b, gates, w_out, ln_g, ln_b, target)


def _mla_bwd(q, k, v, d_o, o, lse):
    seq = q.shape[1]
    tq = 512
    nq = seq // tq

    def body(q_ref, k_ref, v_ref, do_ref, o_ref, lse_ref, dq_ref, dk_ref, dv_ref, d_s, dk_s, dv_s):
        j = pl.program_id(1)
        lane = lax.broadcasted_iota(jnp.int32, (tq, LANES), 1)
        row = lax.broadcasted_iota(jnp.int32, (tq, tq), 0)
        col = lax.broadcasted_iota(jnp.int32, (tq, tq), 1)

        @pl.when(j == 0)
        def _():
            dq_ref[...] = jnp.zeros_like(dq_ref)

            def rowsum(i, carry):
                rows = pl.ds(pl.multiple_of(i * tq, tq), tq)
                prod = do_ref[rows, :] * o_ref[rows, :]
                for hh in range(2):
                    mine = (lane >= 64) if hh else (lane < 64)
                    total = jnp.sum(jnp.where(mine, prod, 0.0), axis=1, keepdims=True)
                    d_s[hh, rows, :] = total + jnp.zeros((tq, LANES), F32)
                return carry

            lax.fori_loop(0, nq, rowsum, 0)

        dk_s[...] = jnp.zeros_like(dk_s)
        dv_s[...] = jnp.zeros_like(dv_s)

        def step(i, masked):
            rows = pl.ds(pl.multiple_of(i * tq, tq), tq)
            dob = do_ref[rows, :].astype(BF16)
            for hh in range(2):
                qb, kb, vb = q_ref[hh, rows, :], k_ref[hh], v_ref[hh]
                s = _dot(qb, kb, NT)
                p = jnp.exp(s - lse_ref[hh, rows, :][:, :1])
                if masked:
                    p = jnp.where(col <= row, p, 0.0)
                dv_s[hh] += _dot(p.astype(BF16), dob, TN)
                dp = _dot(dob, vb, NT)
                ds = (p * (dp - d_s[hh, rows, :][:, :1])).astype(BF16)
                dk_s[hh] += _dot(ds, qb, TN)
                dq_ref[hh, rows, :] += _dot(ds, kb)

        def full_step(i, carry):
            step(i, False)
            return carry

        step(j, True)
        lax.fori_loop(j + 1, nq, full_step, 0)
        dk_ref[...] = dk_s[...]
        dv_ref[...] = dv_s[...]

    whole = pl.BlockSpec((2, seq, LANES), lambda p, j: (p, 0, 0))
    blk = pl.BlockSpec((2, tq, LANES), lambda p, j: (p, j, 0))
    pair = pl.BlockSpec((seq, LANES), lambda p, j: (0, p))
    shape = jax.ShapeDtypeStruct((MLA_HEADS, seq, LANES), F32)
    return pl.pallas_call(
        body, name="mla_bwd", grid=(MLA_HEADS // 2, nq),
        in_specs=[whole, blk, blk, pair, pair, whole],
        out_specs=[whole, blk, blk], out_shape=[shape] * 3,
        scratch_shapes=[pltpu.VMEM((2, seq, LANES), F32), pltpu.VMEM((2, tq, LANES), F32),
                        pltpu.VMEM((2, tq, LANES), F32)],
        compiler_params=_cp(("arbitrary", "arbitrary"), VMEM_LIMIT),
    )(q, k, v, d_o, o, lse)


def _dil_bwd(qr, kr, qkv, d_o, o, lse):
    seq = qr.shape[0]
    n_tiles = seq // BLOCK
    chunk = 512

    def body(q_ref, k_ref, v_ref, do_ref, o_ref, lse_ref, dq_ref, dk_ref, dv_ref, d_s, dq_s, dk_s, dv_s):
        lane = lax.broadcasted_iota(jnp.int32, (BLOCK, LANES), 1)
        lanec = lax.broadcasted_iota(jnp.int32, (chunk, LANES), 1)

        def rowsum(i, carry):
            rows = pl.ds(pl.multiple_of(i * chunk, chunk), chunk)
            prod = do_ref[rows, :] * o_ref[rows, :]
            lo = jnp.sum(jnp.where(lanec < 64, prod, 0.0), axis=1, keepdims=True)
            hi = jnp.sum(jnp.where(lanec >= 64, prod, 0.0), axis=1, keepdims=True)
            d_s[rows, :] = jnp.where(lanec < 64, lo, hi)
            return carry

        lax.fori_loop(0, seq // chunk, rowsum, 0)
        dq_s[...] = jnp.zeros_like(dq_s)
        dk_s[...] = jnp.zeros_like(dk_s)
        dv_s[...] = jnp.zeros_like(dv_s)
        for d in DIL_DILATIONS:

            def tile(t, carry, d=d):
                n, start, prev = _dil_tile_index(t, d, seq)
                rows, prows = _dil_rows(start, d), _dil_rows(prev, d)
                q_t = q_ref[rows, :]
                do_t = do_ref[rows, :]
                lse_t = lse_ref[rows, :]
                d_t = d_s[rows, :]
                kcat = jnp.concatenate([k_ref[prows, :], k_ref[rows, :]], axis=0).astype(BF16)
                vcat = jnp.concatenate([v_ref[prows, :], v_ref[rows, :]], axis=0).astype(BF16)
                valid = _dil_valid(n)
                dq_t = jnp.zeros((BLOCK, LANES), F32)
                dkcat = jnp.zeros((2 * BLOCK, LANES), F32)
                dvcat = jnp.zeros((2 * BLOCK, LANES), F32)
                for hh in range(2):
                    mine = (lane >= 64) if hh else (lane < 64)
                    c0 = 64 * hh
                    qh = jnp.where(mine, q_t, 0.0).astype(BF16)
                    doh = jnp.where(mine, do_t, 0.0).astype(BF16)
                    s = _dot(qh, kcat, NT)
                    p = jnp.where(valid, jnp.exp(s - lse_t[:, c0:c0 + 1]), 0.0)
                    dvcat = dvcat + _dot(p.astype(BF16), doh, TN)
                    dp = _dot(doh, vcat, NT)
                    ds = (p * (dp - d_t[:, c0:c0 + 1])).astype(BF16)
                    dq_t = dq_t + jnp.where(mine, _dot(ds, kcat), 0.0)
                    dkcat = dkcat + _dot(ds, qh, TN)
                dq_s[rows, :] += dq_t
                dk_s[prows, :] += dkcat[:BLOCK]
                dk_s[rows, :] += dkcat[BLOCK:]
                dv_s[prows, :] += dvcat[:BLOCK]
                dv_s[rows, :] += dvcat[BLOCK:]
                return carry

            lax.fori_loop(0, n_tiles, tile, 0, unroll=4)
        dq_ref[...] = dq_s[...].astype(BF16)
        dk_ref[...] = dk_s[...].astype(BF16)
        dv_ref[...] = dv_s[...].astype(BF16)

    col = lambda off: pl.BlockSpec((seq, LANES), lambda p: (0, p + off))
    shape = jax.ShapeDtypeStruct((seq, 4 * LANES), BF16)
    return pl.pallas_call(
        body, name="dil_bwd", grid=(4,),
        in_specs=[col(0), col(0), col(8), col(4), col(0), pl.BlockSpec((None, seq, LANES), lambda p: (p, 0, 0))],
        out_specs=[col(0)] * 3, out_shape=[shape] * 3,
        scratch_shapes=[pltpu.VMEM((seq, LANES), F32)] * 4,
        compiler_params=_cp(("arbitrary",), VMEM_LIMIT),
    )(qr, kr, qkv, d_o, o, lse)


def _mla_pre_bwd(cq, ckv, gq, gkv, wuq_e, wukv, ct, st, dq, dk, dv):
    seq = cq.shape[0]
    tr = 512

    def body(cq_ref, ckv_ref, gq_ref, gkv_ref, wuq_ref, wukv_ref, ct_ref, st_ref, dq_ref, dk_ref, dv_ref,
             dcq_ref, dckv_ref, dkr_ref, dwuq_ref, dwukv_ref, dgq_ref, dgkv_ref):
        @pl.when(pl.program_id(0) == 0)
        def _():
            dwuq_ref[...] = jnp.zeros_like(dwuq_ref)
            dwukv_ref[...] = jnp.zeros_like(dwukv_ref)
            dgq_ref[...] = jnp.zeros_like(dgq_ref)
            dgkv_ref[...] = jnp.zeros_like(dgkv_ref)

        lane = lax.broadcasted_iota(jnp.int32, (tr, LANES), 1)
        rope_lanes = jnp.logical_and(lane >= 64, lane < 96)
        ct_, st_ = ct_ref[...], st_ref[...]

        def rope_t(g):
            return ct_ * g + jnp.where(rope_lanes, _mla_rot(st_ * g, lane), 0.0)

        def norm_bwd(c, g, dn, dg_ref):
            r, _ = _rms(c, g)
            u = dn * g
            dg_ref[...] += jnp.sum(dn * c * r, axis=0, keepdims=True)
            return r * u - c * (r * r * r) * jnp.mean(u * c, axis=-1, keepdims=True)

        c, g = cq_ref[...], gq_ref[...]
        _, qn = _rms(c, g)
        dq_all = jnp.concatenate([rope_t(dq_ref[h] * MLA_SCALE) for h in range(MLA_HEADS)], axis=1).astype(BF16)
        dwuq_ref[...] += _dot(qn.astype(BF16), dq_all, TN)
        dcq_ref[...] = norm_bwd(c, g, _dot(dq_all, wuq_ref[...], NT), dgq_ref).astype(BF16)

        c, g = ckv_ref[...], gkv_ref[...]
        _, kvn = _rms(c, g)
        dkpe = jnp.zeros((tr, LANES), F32)
        parts = []
        for h in range(MLA_HEADS):
            dk_h, dv_h = dk_ref[h], dv_ref[h]
            if h % 2 == 0:
                dv_h = pltpu.roll(dv_h, 64, 1)
            parts.append(jnp.where(lane < 64, dk_h, dv_h))
            dkpe = dkpe + jnp.where(rope_lanes, dk_h, 0.0)
        dkv_all = jnp.concatenate(parts, axis=1).astype(BF16)
        dwukv_ref[...] += _dot(kvn.astype(BF16), dkv_all, TN)
        dckv_ref[...] = norm_bwd(c, g, _dot(dkv_all, wukv_ref[...], NT), dgkv_ref).astype(BF16)
        dkr_ref[...] = rope_t(dkpe).astype(BF16)

    row = lambda w: pl.BlockSpec((tr, w), lambda i: (i, 0))
    full = lambda a: pl.BlockSpec(a.shape, lambda i: (0,) * a.ndim)
    head = pl.BlockSpec((MLA_HEADS, tr, LANES), lambda i: (0, i, 0))
    return pl.pallas_call(
        body, name="mla_pre_bwd", grid=(seq // tr,),
        in_specs=[row(Q_LORA), row(KV_LORA), full(gq), full(gkv), full(wuq_e), full(wukv), row(LANES), row(LANES),
                  head, head, head],
        out_specs=[row(Q_LORA), row(KV_LORA), row(LANES), full(wuq_e), full(wukv), full(gq), full(gkv)],
        out_shape=[jax.ShapeDtypeStruct((seq, Q_LORA), BF16), jax.ShapeDtypeStruct((seq, KV_LORA), BF16),
                   jax.ShapeDtypeStruct((seq, LANES), BF16), jax.ShapeDtypeStruct(wuq_e.shape, F32),
                   jax.ShapeDtypeStruct(wukv.shape, F32), jax.ShapeDtypeStruct(gq.shape, F32),
                   jax.ShapeDtypeStruct(gkv.shape, F32)],
        compiler_params=_cp(("arbitrary",), VMEM_LIMIT),
    )(cq, ckv, gq, gkv, wuq_e, wukv, ct, st, dq, dk, dv)


def _in_bwd(dz, dcq, dckv, dgates, dqr, dkr, dvb, dkrope, cd, sd, w_in_p):
    seq = dz.shape[0]
    tr = 512

    def body(dz_ref, dcq_ref, dckv_ref, dg_ref, dqr_ref, dkr_ref, dvb_ref, dkp_ref, cd_ref, sd_ref, w_ref, gx_ref, dh_ref):
        lane = lax.broadcasted_iota(jnp.int32, (tr, LANES), 1)
        rot_lanes = lane % 64 < DIL_ROT
        cd_, sd_ = cd_ref[...], sd_ref[...]

        def rope_t(g):
            return cd_ * g + jnp.where(rot_lanes, _dil_rot(sd_ * g, lane), 0.0)

        dq = [rope_t(dqr_ref[:, LANES * p:LANES * (p + 1)].astype(F32) * DIL_SCALE).astype(BF16) for p in range(4)]
        dk = [rope_t(dkr_ref[:, LANES * p:LANES * (p + 1)].astype(F32)).astype(BF16) for p in range(4)]
        dh = jnp.concatenate([dcq_ref[...], dckv_ref[...], dg_ref[...]] + dq + dk + [dvb_ref[...], dkp_ref[...]], axis=1)
        dh_ref[...] = dh
        gx_ref[...] = ALPHA * dz_ref[...] + _dot(dh, w_ref[...], NT)

    row = lambda w: pl.BlockSpec((tr, w), lambda i: (i, 0))
    return pl.pallas_call(
        body, name="in_bwd", grid=(seq // tr,),
        in_specs=[row(D_MODEL), row(Q_LORA), row(KV_LORA), row(D_MODEL), row(512), row(512), row(512), row(LANES),
                  row(LANES), row(LANES), pl.BlockSpec((D_MODEL, IN_WIDTH_PAD), lambda i: (0, 0))],
        out_specs=[row(D_MODEL), row(IN_WIDTH_PAD)],
        out_shape=[jax.ShapeDtypeStruct((seq, D_MODEL), F32), jax.ShapeDtypeStruct((seq, IN_WIDTH_PAD), BF16)],
        compiler_params=_cp(("arbitrary",), VMEM_LIMIT),
    )(dz, dcq, dckv, dgates, dqr, dkr, dvb, dkrope, cd, sd, w_in_p)


def _dw_in(x_t, dh):
    seq = dh.shape[0]
    tk = 512
    tn = IN_WIDTH_PAD // 2

    def body(x_ref, dh_ref, o_ref):
        @pl.when(pl.program_id(1) == 0)
        def _():
            o_ref[...] = jnp.zeros_like(o_ref)

        o_ref[...] += _dot(x_ref[...], dh_ref[...])

    return pl.pallas_call(
        body, name="dw_in", grid=(2, seq // tk),
        in_specs=[pl.BlockSpec((D_MODEL, tk), lambda n, k: (0, k)), pl.BlockSpec((tk, tn), lambda n, k: (k, n))],
        out_specs=pl.BlockSpec((D_MODEL, tn), lambda n, k: (0, n)),
        out_shape=jax.ShapeDtypeStruct((D_MODEL, IN_WIDTH_PAD), F32),
        compiler_params=_cp(("arbitrary", "arbitrary"), VMEM_LIMIT),
    )(x_t, dh)


def _adamw(w, g, m, v, name):
    rows, cols = w.shape
    tr = 256 if rows % 256 == 0 else rows

    def body(w_ref, g_ref, m_ref, v_ref, d_ref, nm_ref, nv_ref):
        g_ = g_ref[...]
        nm = ADAM_B1 * m_ref[...] + (1.0 - ADAM_B1) * g_
        nv = ADAM_B2 * v_ref[...] + (1.0 - ADAM_B2) * jnp.square(g_)
        m_hat = nm / (1.0 - ADAM_B1 ** ADAM_STEP)
        v_hat = nv / (1.0 - ADAM_B2 ** ADAM_STEP)
        d_ref[...] = -ADAM_LR * (m_hat / (jnp.sqrt(v_hat) + ADAM_EPS) + ADAM_WD * w_ref[...])
        nm_ref[...] = nm
        nv_ref[...] = nv

    spec = pl.BlockSpec((tr, cols), lambda i: (i, 0))
    return pl.pallas_call(
        body, name=name, grid=(rows // tr,), in_specs=[spec] * 4, out_specs=[spec] * 3,
        out_shape=[jax.ShapeDtypeStruct(w.shape, F32)] * 3, compiler_params=_cp(("arbitrary",)),
    )(w, g, m, v)


def _pad_row(v):
    return jnp.pad(v.reshape(1, -1), ((0, 0), (0, D_MODEL - v.shape[-1])))


def _local_step(x2, target, w_in_f, w_uq_f, wukv_f, w_out_f, q_norm_g, kv_norm_g, ln_g, ln_b):
    seq = x2.shape[0]
    w_in_p = _permute_w_in(w_in_f)
    wuq_e = jnp.pad(w_uq_f.reshape(Q_LORA, MLA_HEADS, 96), ((0, 0), (0, 0), (0, 32))).reshape(Q_LORA, MLA_HEADS * LANES)
    ct, st, cd, sd = _rope_tables(seq)
    gq = q_norm_g.reshape(1, Q_LORA)
    gkv = kv_norm_g.reshape(1, KV_LORA)

    cq, ckv, gates, qkv, kr = _proj(x2, w_in_p)
    q_e, k_e, v_e = _mla_pre(cq, ckv, kr, gq, gkv, wuq_e, wukv_f, ct, st)
    o_a, lse_a = _mla_fwd(q_e, k_e, v_e)
    qr, krot = _dil_pre(qkv, cd, sd)
    o_b, lse_b = _dil_fwd(qr, krot, qkv)

    dz, d_o, d_gates, dw_out, dln_g, dln_b, loss_part = _post(
        x2, o_a, o_b, gates, w_out_f, ln_g.reshape(1, D_MODEL), ln_b.reshape(1, D_MODEL), target)
    dq_e, dk_e, dv_e = _mla_bwd(q_e, k_e, v_e, d_o, o_a, lse_a)
    dqr, dkr, dvb = _dil_bwd(qr, krot, qkv, d_o, o_b, lse_b)
    dcq, dckv, dkrope, dwuq_e, dwukv, dgq, dgkv = _mla_pre_bwd(cq, ckv, gq, gkv, wuq_e, wukv_f, ct, st, dq_e, dk_e, dv_e)
    grad_x, dh = _in_bwd(dz, dcq, dckv, d_gates, dqr, dkr, dvb, dkrope, cd, sd, w_in_p)
    dw_in = _unpermute_dw_in(_dw_in(x2.T.astype(BF16), dh))
    dw_uq = dwuq_e.reshape(Q_LORA, MLA_HEADS, LANES)[:, :, :96].reshape(Q_LORA, MLA_HEADS * 96)
    return loss_part, grad_x, dw_in, dw_uq, dwukv, dw_out, dgq, dgkv, dln_g, dln_b


def kernel(x, w_in, q_norm_g, kv_norm_g, w_uq, w_ukv, w_out, ln_g, ln_b, loss_target, m_w_in, m_q_norm_g, m_kv_norm_g, m_w_uq, m_w_ukv, m_w_out, m_ln_g, m_ln_b, v_w_in, v_q_norm_g, v_kv_norm_g, v_w_uq, v_w_ukv, v_w_out, v_ln_g, v_ln_b):
    seq = x.shape[1]
    x2 = x.reshape(seq, D_MODEL)
    target = loss_target.reshape(seq, D_MODEL)

    g_w_in, g_w_uq, g_w_ukv, g_w_out = _all_gather_weights([w_in, w_uq, w_ukv, w_out])
    by_cols = lambda g: jnp.concatenate([g[j] for j in range(N_SHARD)], axis=1)
    loss_part, grad_x, dw_in, dw_uq, dwukv, dw_out, dgq, dgkv, dln_g, dln_b = _local_step(
        x2, target, by_cols(g_w_in), by_cols(g_w_uq), by_cols(g_w_ukv), g_w_out.reshape(D_MODEL, D_MODEL),
        q_norm_g, kv_norm_g, ln_g, ln_b)

    to_shards = lambda d: d.reshape(d.shape[0], N_SHARD, d.shape[1] // N_SHARD).transpose(1, 0, 2)
    grads = [to_shards(dw_in), to_shards(dw_uq), to_shards(dwukv), dw_out.reshape(N_SHARD, 256, D_MODEL)]
    small = jnp.concatenate([_pad_row(dgq), _pad_row(dgkv), dln_g, dln_b, _pad_row(loss_part),
                             jnp.zeros((3, D_MODEL), F32)], axis=0)
    *chip_sums, smalls = _reduce_over_sibling(grads, small)
    g_in, g_uq, g_ukv, g_out = _reduce_over_chips(chip_sums)
    small_sum = _sum_smalls(smalls)
    loss = small_sum[4, 0]

    big = [_adamw(w, g, m, v, name) for w, g, m, v, name in (
        (w_in, g_in, m_w_in, v_w_in, "adamw_w_in"), (w_uq, g_uq, m_w_uq, v_w_uq, "adamw_w_uq"),
        (w_ukv, g_ukv, m_w_ukv, v_w_ukv, "adamw_w_ukv"), (w_out, g_out, m_w_out, v_w_out, "adamw_w_out"))]
    vec = lambda a, b, c_, d: jnp.concatenate([_pad_row(a), _pad_row(b), _pad_row(c_), _pad_row(d),
                                               jnp.zeros((4, D_MODEL), F32)], axis=0)
    sw = vec(q_norm_g, kv_norm_g, ln_g, ln_b)
    sm = vec(m_q_norm_g, m_kv_norm_g, m_ln_g, m_ln_b)
    sv = vec(v_q_norm_g, v_kv_norm_g, v_ln_g, v_ln_b)
    sg = jnp.concatenate([small_sum[:4], jnp.zeros((4, D_MODEL), F32)], axis=0)
    s_delta, s_m, s_v = _adamw(sw, sg, sm, sv, "adamw_vectors")

    def vectors(a):
        return [a[0, :Q_LORA], a[1, :KV_LORA], a[2], a[3]]

    def ordered(bigs, smalls_):
        return [bigs[0], smalls_[0], smalls_[1], bigs[1], bigs[2], bigs[3], smalls_[2], smalls_[3]]

    grads_out = ordered([g_in, g_uq, g_ukv, g_out], vectors(small_sum))
    deltas = ordered([b[0] for b in big], vectors(s_delta))
    new_m = ordered([b[1] for b in big], vectors(s_m))
    new_v = ordered([b[2] for b in big], vectors(s_v))
    return (loss, grad_x.reshape(x.shape), *grads_out, *deltas, *new_m, *new_v)
```

```python
import functools

import jax
import jax.numpy as jnp
from jax import lax
from jax.experimental import pallas as pl
from jax.experimental.pallas import tpu as pltpu

F32 = jnp.float32
BF16 = jnp.bfloat16

D_MODEL = 1024
ROPE_THETA = 500000.0
BLOCK = 128
NEG = -1e30
RMS_EPS = 1e-6
LN_EPS = 1e-5

MLA_HEADS = 8
MLA_NOPE = 64
MLA_ROPE = 32
Q_LORA = 384
KV_LORA = 256
DIL_HEADS = 8
DIL_HEAD_DIM = 64
DIL_ROT = 16
DIL_DILATIONS = (1, 4, 16)
IN_WIDTH = 3232
IN_WIDTH_PAD = 3328
ONES_LANE = (64, 0)
MLA_SCALE = (MLA_NOPE + MLA_ROPE) ** -0.5
DIL_SCALE = DIL_HEAD_DIM ** -0.5
ALPHA = 2.0 ** 0.25

ADAM_LR = 0.001
ADAM_B1 = 0.9
ADAM_B2 = 0.999
ADAM_EPS = 1e-08
ADAM_WD = 0.01
ADAM_STEP = 10

N_SHARD = 4
SHARD_SHAPES = ((1024, 808), (384, 192), (256, 256), (256, 1024))
ROW_CHUNK = 64
LANES = 128
VMEM_LIMIT = 56 * 1024 * 1024
MESH = pl.DeviceIdType.MESH

NT = (((1,), (1,)), ((), ()))
TN = (((0,), (0,)), ((), ()))


def _cp(sem=None, vmem=None):
    return pltpu.CompilerParams(dimension_semantics=sem, vmem_limit_bytes=vmem)


def _dot(a, b, dims=None):
    if dims is None:
        return jnp.dot(a, b, preferred_element_type=F32)
    return lax.dot_general(a, b, dims, preferred_element_type=F32)


def _rope_tables(seq):
    pos = jnp.arange(seq, dtype=F32)[:, None]
    inv = ROPE_THETA ** (-jnp.arange(0, MLA_ROPE, 2, dtype=F32) / MLA_ROPE)
    ang = pos * inv[None, :]
    cos, sin = jnp.cos(ang), jnp.sin(ang)
    one, zero = jnp.ones((seq, 64), F32), jnp.zeros((seq, 64), F32)
    ct = jnp.concatenate([one, cos, cos, zero[:, :32]], axis=1)
    st = jnp.concatenate([zero, -sin, sin, zero[:, :32]], axis=1)
    inv = ROPE_THETA ** (-jnp.arange(0, DIL_ROT, 2, dtype=F32) / DIL_ROT)
    ang = pos * inv[None, :]
    cos, sin = jnp.cos(ang), jnp.sin(ang)
    cd = jnp.concatenate([cos, cos, one[:, :48]], axis=1)
    sd = jnp.concatenate([-sin, sin, zero[:, :48]], axis=1)
    return ct, st, jnp.tile(cd, (1, 2)), jnp.tile(sd, (1, 2))


W_IN_ORDER = ((0, 640), (672, 1184), (2720, 3232), (1184, 2720), None, (640, 672))


def _permute_w_in(w):
    z = jnp.zeros((w.shape[0], 64), w.dtype)
    parts = [z if r is None else w[:, r[0]:r[1]] for r in W_IN_ORDER]
    return jnp.concatenate(parts + [z[:, :32]], axis=1)


def _permute_w_in_shards(g):
    width = g.shape[2]
    z = jnp.zeros((g.shape[1], 64), g.dtype)
    parts = []
    for r in W_IN_ORDER:
        if r is None:
            parts.append(z)
            continue
        for j in range(N_SHARD):
            lo, hi = max(r[0], width * j), min(r[1], width * (j + 1))
            if lo < hi:
                parts.append(g[j, :, lo - width * j:hi - width * j])
    return jnp.concatenate(parts + [z[:, :32]], axis=1)


def _unpermute_dw_in(dw):
    return jnp.concatenate([dw[:, 0:640], dw[:, 3264:3296], dw[:, 640:1152], dw[:, 1664:3200],
                            dw[:, 1152:1664]], axis=1)


def _position():
    return lax.axis_index("x"), lax.axis_index("y"), lax.axis_index("c")


def _halves(c, rows):
    hr = rows // 2
    return pl.ds(pl.multiple_of(c * hr, 8), hr), pl.ds(pl.multiple_of((1 - c) * hr, 8), hr)


def _for_row_chunks(rows, fn):
    def step(i, carry):
        fn(pl.multiple_of(i * ROW_CHUNK, ROW_CHUNK))
        return carry

    lax.fori_loop(0, rows // ROW_CHUNK, step, 0)


def _all_gather_weights(shards):
    n = len(shards)

    def body(*refs):
        ins, outs = refs[:n], refs[n:2 * n]
        send_sems, recv_sems = refs[2 * n:]
        x, y, c = _position()
        me = 2 * x + y
        chips = [(1 - x, y), (x, 1 - y), (1 - x, 1 - y)]
        for a in range(n):
            def cast(r, a=a):
                outs[a][me, pl.ds(r, ROW_CHUNK), :] = ins[a][pl.ds(r, ROW_CHUNK), :].astype(BF16)

            _for_row_chunks(SHARD_SHAPES[a][0], cast)

        def copy(k, a, slot, rows, to):
            ref = outs[a].at[slot, rows]
            return pltpu.make_async_remote_copy(
                src_ref=ref, dst_ref=ref, send_sem=send_sems.at[k * n + a], recv_sem=recv_sems.at[k * n + a],
                device_id=to, device_id_type=MESH)

        half = [_halves(c, SHARD_SHAPES[a][0])[0] for a in range(n)]
        other = [_halves(c, SHARD_SHAPES[a][0])[1] for a in range(n)]
        first = [copy(k, a, me, half[a], (px, py, c)) for k, (px, py) in enumerate(chips) for a in range(n)]
        for cp in first:
            cp.start()
        passed = []
        for k, (px, py) in enumerate(chips):
            for a in range(n):
                copy(k, a, 2 * px + py, half[a], (x, y, c)).wait_recv()
                cp = copy(3 + k, a, 2 * px + py, half[a], (x, y, 1 - c))
                cp.start()
                passed.append(cp)
        for k, (px, py) in enumerate(chips):
            for a in range(n):
                copy(3 + k, a, 2 * px + py, other[a], (x, y, c)).wait_recv()
        for cp in first + passed:
            cp.wait_send()

    vmem = pl.BlockSpec(memory_space=pltpu.VMEM)
    return pl.pallas_call(
        body, name="all_gather_weights",
        out_shape=[jax.ShapeDtypeStruct((N_SHARD,) + s, BF16) for s in SHARD_SHAPES],
        in_specs=[vmem] * n, out_specs=[vmem] * n,
        scratch_shapes=[pltpu.SemaphoreType.DMA((6 * n,)), pltpu.SemaphoreType.DMA((6 * n,))],
        compiler_params=_cp(None, VMEM_LIMIT),
    )(*shards)


def _reduce_over_sibling(grads, small):
    n = len(grads)

    def body(*refs):
        g_hbm, sm = refs[:n], refs[n]
        sums, smalls = refs[n + 1:2 * n + 1], refs[2 * n + 1]
        stage, got = refs[2 * n + 2:3 * n + 2], refs[3 * n + 2:4 * n + 2]
        send_sems, recv_sems, local_sems = refs[4 * n + 2:]
        x, y, c = _position()
        me = 4 * x + 2 * y + c
        loads = [pltpu.make_async_copy(g_hbm[a], stage[a], local_sems.at[a]) for a in range(n)]
        for ld in loads:
            ld.start()
        smalls[me] = sm[...]
        sends = []
        for rel in range(1, 8):
            px = 1 - x if rel // 4 else x
            py = 1 - y if (rel // 2) % 2 else y
            pc = 1 - c if rel % 2 else c
            cp = pltpu.make_async_remote_copy(
                src_ref=sm, dst_ref=smalls.at[me], send_sem=send_sems.at[n + rel], recv_sem=recv_sems.at[n + rel],
                device_id=(px, py, pc), device_id_type=MESH)
            cp.start()
            sends.append((cp, 4 * px + 2 * py + pc))
        swaps = []
        for a in range(n):
            loads[a].wait()
            _, other = _halves(c, SHARD_SHAPES[a][0])
            cp = pltpu.make_async_remote_copy(
                src_ref=stage[a].at[:, other], dst_ref=got[a], send_sem=send_sems.at[a], recv_sem=recv_sems.at[a],
                device_id=(x, y, 1 - c), device_id_type=MESH)
            cp.start()
            swaps.append(cp)
        for a in range(n):
            swaps[a].wait_recv()
            hr = SHARD_SHAPES[a][0] // 2
            for k in range(N_SHARD):
                def add(r, a=a, k=k, hr=hr):
                    mine = stage[a][k, pl.ds(pl.multiple_of(c * hr + r, ROW_CHUNK), ROW_CHUNK), :]
                    sums[a][k, pl.ds(r, ROW_CHUNK), :] = (mine + got[a][k, pl.ds(r, ROW_CHUNK), :]).astype(BF16)

                _for_row_chunks(hr, add)
        for rel, (cp, peer) in enumerate(sends, start=1):
            pltpu.make_async_remote_copy(
                src_ref=sm, dst_ref=smalls.at[peer], send_sem=send_sems.at[n + rel], recv_sem=recv_sems.at[n + rel],
                device_id=(x, y, c), device_id_type=MESH).wait_recv()
        for cp in swaps:
            cp.wait_send()
        for cp, _ in sends:
            cp.wait_send()

    vmem = pl.BlockSpec(memory_space=pltpu.VMEM)
    half = [(N_SHARD, r // 2, cols) for r, cols in SHARD_SHAPES]
    return pl.pallas_call(
        body, name="reduce_over_sibling",
        out_shape=[jax.ShapeDtypeStruct(s, BF16) for s in half] + [jax.ShapeDtypeStruct((8,) + small.shape, F32)],
        in_specs=[pl.BlockSpec(memory_space=pl.ANY)] * n + [vmem], out_specs=[vmem] * (n + 1),
        scratch_shapes=[pltpu.VMEM((N_SHARD,) + s, F32) for s in SHARD_SHAPES] + [pltpu.VMEM(s, F32) for s in half]
        + [pltpu.SemaphoreType.DMA((n + 8,)), pltpu.SemaphoreType.DMA((n + 8,)), pltpu.SemaphoreType.DMA((n,))],
        compiler_params=_cp(None, VMEM_LIMIT),
    )(*grads, small)


def _reduce_over_chips(sums):
    n = len(sums)

    def body(*refs):
        h, outs, got = refs[:n], refs[n:2 * n], refs[2 * n:3 * n]
        send_sems, recv_sems = refs[3 * n:]
        x, y, c = _position()
        me = 2 * x + y
        chips = [(1 - x, y), (x, 1 - y), (1 - x, 1 - y)]
        sends = []
        for k, (px, py) in enumerate(chips):
            for a in range(n):
                cp = pltpu.make_async_remote_copy(
                    src_ref=h[a].at[2 * px + py], dst_ref=got[a].at[k], send_sem=send_sems.at[k * n + a],
                    recv_sem=recv_sems.at[k * n + a], device_id=(px, py, c), device_id_type=MESH)
                cp.start()
                sends.append(cp)
        for cp in sends:
            cp.wait_recv()
        joins = []
        for a in range(n):
            hr = SHARD_SHAPES[a][0] // 2
            half, other = _halves(c, SHARD_SHAPES[a][0])

            def add(r, a=a, hr=hr):
                rows = pl.ds(r, ROW_CHUNK)
                total = h[a][me, rows, :].astype(F32)
                for k in range(3):
                    total = total + got[a][k, rows, :].astype(F32)
                outs[a][pl.ds(pl.multiple_of(c * hr + r, ROW_CHUNK), ROW_CHUNK), :] = total

            _for_row_chunks(hr, add)
            cp = pltpu.make_async_remote_copy(
                src_ref=outs[a].at[half], dst_ref=outs[a].at[half], send_sem=send_sems.at[3 * n + a],
                recv_sem=recv_sems.at[3 * n + a], device_id=(x, y, 1 - c), device_id_type=MESH)
            cp.start()
            joins.append(cp)
        for a in range(n):
            other = _halves(c, SHARD_SHAPES[a][0])[1]
            pltpu.make_async_remote_copy(
                src_ref=outs[a].at[other], dst_ref=outs[a].at[other], send_sem=send_sems.at[3 * n + a],
                recv_sem=recv_sems.at[3 * n + a], device_id=(x, y, c), device_id_type=MESH).wait_recv()
        for cp in sends + joins:
            cp.wait_send()

    vmem = pl.BlockSpec(memory_space=pltpu.VMEM)
    return pl.pallas_call(
        body, name="reduce_over_chips",
        out_shape=[jax.ShapeDtypeStruct(s, F32) for s in SHARD_SHAPES],
        in_specs=[vmem] * n, out_specs=[vmem] * n,
        scratch_shapes=[pltpu.VMEM((3, r // 2, cols), BF16) for r, cols in SHARD_SHAPES]
        + [pltpu.SemaphoreType.DMA((4 * n,)), pltpu.SemaphoreType.DMA((4 * n,))],
        compiler_params=_cp(None, VMEM_LIMIT),
    )(*sums)


def _sum_smalls(smalls):
    def body(s, o):
        acc = s[0]
        for d in range(1, 8):
            acc = acc + s[d]
        o[...] = acc

    return pl.pallas_call(body, name="sum_smalls", out_shape=jax.ShapeDtypeStruct(smalls.shape[1:], F32))(smalls)


def _proj(x, w_in_p):
    seq = x.shape[0]
    tr = 512
    splits = ((0, 384), (384, 640), (640, 1664), (1664, 3200), (3200, 3328))

    def body(x_ref, w_ref, *outs):
        xb = x_ref[...].astype(BF16)
        for (lo, hi), o in zip(splits, outs):
            o[...] = _dot(xb, w_ref[:, lo:hi])

    return pl.pallas_call(
        body, name="proj", grid=(seq // tr,),
        in_specs=[pl.BlockSpec((tr, D_MODEL), lambda i: (i, 0)), pl.BlockSpec((D_MODEL, IN_WIDTH_PAD), lambda i: (0, 0))],
        out_specs=[pl.BlockSpec((tr, hi - lo), lambda i: (i, 0)) for lo, hi in splits],
        out_shape=[jax.ShapeDtypeStruct((seq, hi - lo), F32) for lo, hi in splits],
        compiler_params=_cp(("arbitrary",), VMEM_LIMIT),
    )(x, w_in_p)


def _mla_rot(t, lane):
    return jnp.where(lane < 80, pltpu.roll(t, 112, 1), pltpu.roll(t, 16, 1))


def _dil_rot(t, lane):
    return jnp.where(lane % 64 < 8, pltpu.roll(t, 120, 1), pltpu.roll(t, 8, 1))


def _rms(c, g):
    r = lax.rsqrt(jnp.mean(c * c, axis=-1, keepdims=True) + RMS_EPS)
    return r, c * r * g


def _mla_pre(cq, ckv, kr, gq, gkv, wuq_e, wukv, ct, st):
    seq = cq.shape[0]
    tr = 512

    def body(cq_ref, ckv_ref, kr_ref, gq_ref, gkv_ref, wuq_ref, wukv_ref, ct_ref, st_ref, q_out, k_out, v_out):
        lane = lax.broadcasted_iota(jnp.int32, (tr, LANES), 1)
        ct_, st_ = ct_ref[...], st_ref[...]

        def rope(t):
            return t * ct_ + _mla_rot(t, lane) * st_

        _, qn = _rms(cq_ref[...], gq_ref[...])
        q_all = _dot(qn.astype(BF16), wuq_ref[...])
        for h in range(MLA_HEADS):
            q_out[h] = (rope(q_all[:, LANES * h:LANES * (h + 1)]) * MLA_SCALE).astype(BF16)
        _, kvn = _rms(ckv_ref[...], gkv_ref[...])
        kv_all = _dot(kvn.astype(BF16), wukv_ref[...])
        kpe = rope(kr_ref[...])
        for h in range(MLA_HEADS):
            kv_h = kv_all[:, LANES * h:LANES * (h + 1)]
            k_out[h] = jnp.where(lane < 64, kv_h, kpe).astype(BF16)
            if h % 2:
                v = jnp.where(lane >= 64, kv_h, 0.0)
            else:
                v = jnp.where(lane < 64, pltpu.roll(kv_h, 64, 1), 0.0)
            v_out[h] = jnp.where(lane == ONES_LANE[h % 2], 1.0, v).astype(BF16)

    row = lambda w: pl.BlockSpec((tr, w), lambda i: (i, 0))
    full = lambda a: pl.BlockSpec(a.shape, lambda i: (0,) * a.ndim)
    head = pl.BlockSpec((MLA_HEADS, tr, LANES), lambda i: (0, i, 0))
    return pl.pallas_call(
        body, name="mla_pre", grid=(seq // tr,),
        in_specs=[row(Q_LORA), row(KV_LORA), row(LANES), full(gq), full(gkv), full(wuq_e), full(wukv), row(LANES), row(LANES)],
        out_specs=[head] * 3,
        out_shape=[jax.ShapeDtypeStruct((MLA_HEADS, seq, LANES), BF16)] * 3,
        compiler_params=_cp(("arbitrary",), VMEM_LIMIT),
    )(cq, ckv, kr, gq, gkv, wuq_e, wukv, ct, st)


def _mla_fwd(q, k, v):
    seq = q.shape[1]
    tq = 512
    nq = seq // tq

    def body(q_ref, k_ref, v_ref, o_ref, lse_ref, m_s, acc_s, s_buf):
        i = pl.program_id(1)
        row = lax.broadcasted_iota(jnp.int32, (tq, tq), 0)
        col = lax.broadcasted_iota(jnp.int32, (tq, tq), 1)
        lane = lax.broadcasted_iota(jnp.int32, (tq, LANES), 1)
        m_s[...] = jnp.full((2, tq, LANES), NEG, F32)
        acc_s[...] = jnp.zeros((2, tq, LANES), F32)

        def block(j):
            return pl.ds(pl.multiple_of(j * tq, tq), tq)

        def scores(hh, j):
            return _dot(q_ref[hh], k_ref[hh, block(j), :], NT)

        def consume(hh, j, s):
            m_prev = m_s[hh]
            m_new = jnp.maximum(m_prev, jnp.max(s, axis=1, keepdims=True))
            p = jnp.exp(s - m_new[:, :1])
            acc_s[hh] = jnp.exp(m_prev - m_new) * acc_s[hh] + _dot(p.astype(BF16), v_ref[hh, block(j), :])
            m_s[hh] = m_new

        for hh in range(2):
            s_buf[0, hh] = scores(hh, 0)

        def full_step(j, carry):
            slot = j & 1
            for hh in range(2):
                s = s_buf[slot, hh]
                s_buf[1 - slot, hh] = scores(hh, j + 1)
                consume(hh, j, s)
            return carry

        lax.fori_loop(0, i, full_step, 0)
        total = jnp.zeros((tq, LANES), F32)
        for hh in range(2):
            consume(hh, i, jnp.where(col <= row, s_buf[i & 1, hh], NEG))
            acc = acc_s[hh]
            l = acc[:, ONES_LANE[hh]:ONES_LANE[hh] + 1]
            mine = (lane >= 64) if hh else (lane < 64)
            total = total + jnp.where(mine, acc / l, 0.0)
            lse_ref[hh] = m_s[hh] + jnp.log(l)
        o_ref[...] = total

    kv_spec = pl.BlockSpec((2, seq, LANES), lambda p, i: (p, 0, 0))
    return pl.pallas_call(
        body, name="mla_fwd", grid=(MLA_HEADS // 2, nq),
        in_specs=[pl.BlockSpec((2, tq, LANES), lambda p, i: (p, i, 0)), kv_spec, kv_spec],
        out_specs=[pl.BlockSpec((tq, LANES), lambda p, i: (i, p)), pl.BlockSpec((2, tq, LANES), lambda p, i: (p, i, 0))],
        out_shape=[jax.ShapeDtypeStruct((seq, 4 * LANES), F32), jax.ShapeDtypeStruct((MLA_HEADS, seq, LANES), F32)],
        scratch_shapes=[pltpu.VMEM((2, tq, LANES), F32), pltpu.VMEM((2, tq, LANES), F32),
                        pltpu.VMEM((2, 2, tq, tq), F32)],
        compiler_params=_cp(("arbitrary", "arbitrary"), VMEM_LIMIT),
    )(q, k, v)


def _dil_pre(qkv, cd, sd):
    seq = qkv.shape[0]
    tr = 512

    def body(q_ref, k_ref, cd_ref, sd_ref, qr_ref, kr_ref):
        lane = lax.broadcasted_iota(jnp.int32, (tr, LANES), 1)
        cd_, sd_ = cd_ref[...], sd_ref[...]
        for p in range(4):
            cols = slice(LANES * p, LANES * (p + 1))
            t = q_ref[:, cols]
            qr_ref[:, cols] = (t * cd_ + _dil_rot(t, lane) * sd_) * DIL_SCALE
            t = k_ref[:, cols]
            kr_ref[:, cols] = t * cd_ + _dil_rot(t, lane) * sd_

    blk = lambda j: pl.BlockSpec((tr, 4 * LANES), lambda i: (i, j))
    tab = pl.BlockSpec((tr, LANES), lambda i: (i, 0))
    return pl.pallas_call(
        body, name="dil_pre", grid=(seq // tr,),
        in_specs=[blk(0), blk(1), tab, tab], out_specs=[blk(0), blk(0)],
        out_shape=[jax.ShapeDtypeStruct((seq, 4 * LANES), F32)] * 2,
        compiler_params=_cp(("arbitrary",)),
    )(qkv, qkv, cd, sd)


def _dil_tile_index(t, d, seq):
    per_class = seq // (BLOCK * d)
    shift = per_class.bit_length() - 1
    r = t >> shift
    n = t & (per_class - 1)
    start = r + (BLOCK * d) * n
    prev = r + (BLOCK * d) * jnp.maximum(n - 1, 0)
    if d == 1:
        start = pl.multiple_of(start, BLOCK)
        prev = pl.multiple_of(prev, BLOCK)
    return n, start, prev


def _dil_rows(start, d):
    return pl.ds(start, BLOCK) if d == 1 else pl.ds(start, BLOCK, stride=d)


def _dil_valid(n):
    i = lax.broadcasted_iota(jnp.int32, (BLOCK, 2 * BLOCK), 0)
    j = lax.broadcasted_iota(jnp.int32, (BLOCK, 2 * BLOCK), 1)
    in_prev = (j < BLOCK) & (j >= i) & (n > 0)
    in_cur = (j >= BLOCK) & (j - BLOCK <= i)
    return in_prev | in_cur


def _dil_fwd(qr, kr, qkv):
    seq = qr.shape[0]
    n_tiles = seq // BLOCK

    def body(q_ref, k_ref, v_ref, o_ref, lse_ref, m_s, l_s, n_s):
        lane = lax.broadcasted_iota(jnp.int32, (BLOCK, LANES), 1)
        lane2 = lax.broadcasted_iota(jnp.int32, (2 * BLOCK, LANES), 1)
        for bi, d in enumerate(DIL_DILATIONS):

            def tile(t, carry, d=d, bi=bi):
                n, start, prev = _dil_tile_index(t, d, seq)
                rows, prows = _dil_rows(start, d), _dil_rows(prev, d)
                q_t = q_ref[rows, :]
                kcat = jnp.concatenate([k_ref[prows, :], k_ref[rows, :]], axis=0).astype(BF16)
                vcat = jnp.concatenate([v_ref[prows, :], v_ref[rows, :]], axis=0)
                valid = _dil_valid(n)
                m2 = l2 = num2 = None
                for hh in range(2):
                    mine = (lane >= 64) if hh else (lane < 64)
                    mine2 = (lane2 >= 64) if hh else (lane2 < 64)
                    s = _dot(jnp.where(mine, q_t, 0.0).astype(BF16), kcat, NT)
                    s = jnp.where(valid, s, NEG)
                    m = jnp.max(s, axis=1, keepdims=True)
                    p = jnp.exp(s - m)
                    l = jnp.sum(p, axis=1, keepdims=True)
                    num = _dot(p.astype(BF16), jnp.where(mine2, vcat, 0.0).astype(BF16))
                    if hh == 0:
                        m2, l2, num2 = m, l, num
                    else:
                        m2 = jnp.where(mine, m, m2)
                        l2 = jnp.where(mine, l, l2)
                        num2 = num2 + num
                if bi == 0:
                    m_s[rows, :] = m2 + jnp.zeros((BLOCK, LANES), F32)
                    l_s[rows, :] = l2 + jnp.zeros((BLOCK, LANES), F32)
                    n_s[rows, :] = num2
                else:
                    m_old = m_s[rows, :]
                    m_new = jnp.maximum(m_old, m2)
                    a = jnp.exp(m_old - m_new)
                    b = jnp.exp(m2 - m_new)
                    m_s[rows, :] = m_new
                    l_s[rows, :] = a * l_s[rows, :] + b * l2
                    n_s[rows, :] = a * n_s[rows, :] + b * num2
                return carry

            lax.fori_loop(0, n_tiles, tile, 0, unroll=4)
        o_ref[...] = n_s[...] / l_s[...]
        lse_ref[...] = m_s[...] + jnp.log(l_s[...])

    col = lambda off: pl.BlockSpec((seq, LANES), lambda p: (0, p + off))
    return pl.pallas_call(
        body, name="dil_fwd", grid=(4,),
        in_specs=[col(0), col(0), col(8)],
        out_specs=[col(0), pl.BlockSpec((None, seq, LANES), lambda p: (p, 0, 0))],
        out_shape=[jax.ShapeDtypeStruct((seq, 4 * LANES), F32), jax.ShapeDtypeStruct((4, seq, LANES), F32)],
        scratch_shapes=[pltpu.VMEM((seq, LANES), F32)] * 3,
        compiler_params=_cp(("arbitrary",), VMEM_LIMIT),
    )(qr, kr, qkv)


def _post(x, o_a, o_b, gates, w_out, ln_g, ln_b, target):
    seq = x.shape[0]
    tr = 512

    def body(x_ref, oa_ref, ob_ref, g_ref, w_ref, lg_ref, lb_ref, t_ref,
             dz_ref, do_ref, dg_ref, dw_ref, dlg_ref, dlb_ref, loss_ref):
        @pl.when(pl.program_id(0) == 0)
        def _():
            dw_ref[...] = jnp.zeros_like(dw_ref)
            dlg_ref[...] = jnp.zeros_like(dlg_ref)
            dlb_ref[...] = jnp.zeros_like(dlb_ref)
            loss_ref[...] = jnp.zeros_like(loss_ref)

        g = g_ref[...]
        sg = 1.0 / (1.0 + jnp.exp(-g))
        silu = g * sg
        o = jnp.concatenate([oa_ref[...], ob_ref[...]], axis=1)
        mixb = (o * silu).astype(BF16)
        w = w_ref[...]
        z = ALPHA * x_ref[...] + _dot(mixb, w)
        mu = jnp.mean(z, axis=-1, keepdims=True)
        zc = z - mu
        rstd = lax.rsqrt(jnp.mean(zc * zc, axis=-1, keepdims=True) + LN_EPS)
        xhat = zc * rstd
        lg = lg_ref[...]
        err = xhat * lg + lb_ref[...] - t_ref[...]
        loss_ref[...] += jnp.sum(err * err) * (0.5 / D_MODEL)
        dy = err * (1.0 / D_MODEL)
        dlg_ref[...] += jnp.sum(dy * xhat, axis=0, keepdims=True)
        dlb_ref[...] += jnp.sum(dy, axis=0, keepdims=True)
        dxh = dy * lg
        dz = rstd * (dxh - jnp.mean(dxh, axis=-1, keepdims=True) - xhat * jnp.mean(dxh * xhat, axis=-1, keepdims=True))
        dz_ref[...] = dz
        dzb = dz.astype(BF16)
        dmix = _dot(dzb, w, NT)
        do_ref[...] = dmix * silu
        dg_ref[...] = (dmix * o * (sg * (1.0 + g * (1.0 - sg)))).astype(BF16)
        dw_ref[...] += _dot(mixb, dzb, TN)

    row = lambda w: pl.BlockSpec((tr, w), lambda i: (i, 0))
    full = lambda s: pl.BlockSpec(s, lambda i: (0, 0))
    return pl.pallas_call(
        body, name="post", grid=(seq // tr,),
        in_specs=[row(D_MODEL), row(512), row(512), row(D_MODEL), full((D_MODEL, D_MODEL)), full((1, D_MODEL)),
                  full((1, D_MODEL)), row(D_MODEL)],
        out_specs=[row(D_MODEL), row(D_MODEL), row(D_MODEL), full((D_MODEL, D_MODEL)), full((1, D_MODEL)),
                   full((1, D_MODEL)), full((1, LANES))],
        out_shape=[jax.ShapeDtypeStruct((seq, D_MODEL), F32), jax.ShapeDtypeStruct((seq, D_MODEL), F32),
                   jax.ShapeDtypeStruct((seq, D_MODEL), BF16), jax.ShapeDtypeStruct((D_MODEL, D_MODEL), F32),
                   jax.ShapeDtypeStruct((1, D_MODEL), F32), jax.ShapeDtypeStruct((1, D_MODEL), F32),
                   jax.ShapeDtypeStruct((1, LANES), F32)],
        compiler_params=_cp(("arbitrary",), VMEM_LIMIT),
    )(x, o_a, o_b, gates, w_out, ln_g, ln_b, target)


def _mla_bwd(q, k, v, d_o, o, lse):
    seq = q.shape[1]
    tq = 512
    nq = seq // tq

    def body(q_ref, k_ref, v_ref, do_ref, o_ref, lse_ref, dq_ref, dk_ref, dv_ref, d_s, dk_s, dv_s, v_s):
        j = pl.program_id(1)
        lane = lax.broadcasted_iota(jnp.int32, (tq, LANES), 1)
        row = lax.broadcasted_iota(jnp.int32, (tq, tq), 0)
        col = lax.broadcasted_iota(jnp.int32, (tq, tq), 1)

        @pl.when(j == 0)
        def _():
            dq_ref[...] = jnp.zeros_like(dq_ref)

            def rowsum(i, carry):
                rows = pl.ds(pl.multiple_of(i * tq, tq), tq)
                prod = do_ref[rows, :] * o_ref[rows, :]
                for hh in range(2):
                    mine = (lane >= 64) if hh else (lane < 64)
                    total = jnp.sum(jnp.where(mine, prod, 0.0), axis=1, keepdims=True)
                    d_s[hh, rows, :] = total + jnp.zeros((tq, LANES), F32)
                return carry

            lax.fori_loop(0, nq, rowsum, 0)

        dk_s[...] = jnp.zeros_like(dk_s)
        dv_s[...] = jnp.zeros_like(dv_s)
        for hh in range(2):
            v_s[hh] = jnp.where(lane == ONES_LANE[hh], 0.0, v_ref[hh].astype(F32)).astype(BF16)

        def step(i, masked):
            rows = pl.ds(pl.multiple_of(i * tq, tq), tq)
            dob = do_ref[rows, :].astype(BF16)
            for hh in range(2):
                qb, kb, vb = q_ref[hh, rows, :], k_ref[hh], v_s[hh]
                s = _dot(qb, kb, NT)
                p = jnp.exp(s - lse_ref[hh, rows, :][:, :1])
                if masked:
                    p = jnp.where(col <= row, p, 0.0)
                dv_s[hh] += _dot(p.astype(BF16), dob, TN)
                dp = _dot(dob, vb, NT)
                ds = (p * (dp - d_s[hh, rows, :][:, :1])).astype(BF16)
                dk_s[hh] += _dot(ds, qb, TN)
                dq_ref[hh, rows, :] += _dot(ds, kb)

        def full_step(i, carry):
            step(i, False)
            return carry

        step(j, True)
        lax.fori_loop(j + 1, nq, full_step, 0)
        dk_ref[...] = dk_s[...]
        dv_ref[...] = dv_s[...]

    whole = pl.BlockSpec((2, seq, LANES), lambda p, j: (p, 0, 0))
    blk = pl.BlockSpec((2, tq, LANES), lambda p, j: (p, j, 0))
    pair = pl.BlockSpec((seq, LANES), lambda p, j: (0, p))
    shape = jax.ShapeDtypeStruct((MLA_HEADS, seq, LANES), F32)
    return pl.pallas_call(
        body, name="mla_bwd", grid=(MLA_HEADS // 2, nq),
        in_specs=[whole, blk, blk, pair, pair, whole],
        out_specs=[whole, blk, blk], out_shape=[shape] * 3,
        scratch_shapes=[pltpu.VMEM((2, seq, LANES), F32), pltpu.VMEM((2, tq, LANES), F32),
                        pltpu.VMEM((2, tq, LANES), F32), pltpu.VMEM((2, tq, LANES), BF16)],
        compiler_params=_cp(("arbitrary", "arbitrary"), VMEM_LIMIT),
    )(q, k, v, d_o, o, lse)


def _dil_bwd(qr, kr, qkv, d_o, o, lse):
    seq = qr.shape[0]
    n_tiles = seq // BLOCK
    chunk = 512

    def body(q_ref, k_ref, v_ref, do_ref, o_ref, lse_ref, dq_ref, dk_ref, dv_ref, d_s, dq_s, dk_s, dv_s):
        lane = lax.broadcasted_iota(jnp.int32, (BLOCK, LANES), 1)
        lanec = lax.broadcasted_iota(jnp.int32, (chunk, LANES), 1)

        def rowsum(i, carry):
            rows = pl.ds(pl.multiple_of(i * chunk, chunk), chunk)
            prod = do_ref[rows, :] * o_ref[rows, :]
            lo = jnp.sum(jnp.where(lanec < 64, prod, 0.0), axis=1, keepdims=True)
            hi = jnp.sum(jnp.where(lanec >= 64, prod, 0.0), axis=1, keepdims=True)
            d_s[rows, :] = jnp.where(lanec < 64, lo, hi)
            return carry

        lax.fori_loop(0, seq // chunk, rowsum, 0)
        dq_s[...] = jnp.zeros_like(dq_s)
        dk_s[...] = jnp.zeros_like(dk_s)
        dv_s[...] = jnp.zeros_like(dv_s)
        for d in DIL_DILATIONS:

            def tile(t, carry, d=d):
                n, start, prev = _dil_tile_index(t, d, seq)
                rows, prows = _dil_rows(start, d), _dil_rows(prev, d)
                q_t = q_ref[rows, :]
                do_t = do_ref[rows, :]
                lse_t = lse_ref[rows, :]
                d_t = d_s[rows, :]
                kcat = jnp.concatenate([k_ref[prows, :], k_ref[rows, :]], axis=0).astype(BF16)
                vcat = jnp.concatenate([v_ref[prows, :], v_ref[rows, :]], axis=0).astype(BF16)
                valid = _dil_valid(n)
                dq_t = jnp.zeros((BLOCK, LANES), F32)
                dkcat = jnp.zeros((2 * BLOCK, LANES), F32)
                dvcat = jnp.zeros((2 * BLOCK, LANES), F32)
                for hh in range(2):
                    mine = (lane >= 64) if hh else (lane < 64)
                    c0 = 64 * hh
                    qh = jnp.where(mine, q_t, 0.0).astype(BF16)
                    doh = jnp.where(mine, do_t, 0.0).astype(BF16)
                    s = _dot(qh, kcat, NT)
                    p = jnp.where(valid, jnp.exp(s - lse_t[:, c0:c0 + 1]), 0.0)
                    dvcat = dvcat + _dot(p.astype(BF16), doh, TN)
                    dp = _dot(doh, vcat, NT)
                    ds = (p * (dp - d_t[:, c0:c0 + 1])).astype(BF16)
                    dq_t = dq_t + jnp.where(mine, _dot(ds, kcat), 0.0)
                    dkcat = dkcat + _dot(ds, qh, TN)
                dq_s[rows, :] += dq_t
                dk_s[prows, :] += dkcat[:BLOCK]
                dk_s[rows, :] += dkcat[BLOCK:]
                dv_s[prows, :] += dvcat[:BLOCK]
                dv_s[rows, :] += dvcat[BLOCK:]
                return carry

            lax.fori_loop(0, n_tiles, tile, 0, unroll=4)
        dq_ref[...] = dq_s[...].astype(BF16)
        dk_ref[...] = dk_s[...].astype(BF16)
        dv_ref[...] = dv_s[...].astype(BF16)

    col = lambda off: pl.BlockSpec((seq, LANES), lambda p: (0, p + off))
    shape = jax.ShapeDtypeStruct((seq, 4 * LANES), BF16)
    return pl.pallas_call(
        body, name="dil_bwd", grid=(4,),
        in_specs=[col(0), col(0), col(8), col(4), col(0), pl.BlockSpec((None, seq, LANES), lambda p: (p, 0, 0))],
        out_specs=[col(0)] * 3, out_shape=[shape] * 3,
        scratch_shapes=[pltpu.VMEM((seq, LANES), F32)] * 4,
        compiler_params=_cp(("arbitrary",), VMEM_LIMIT),
    )(qr, kr, qkv, d_o, o, lse)


def _mla_pre_bwd(cq, ckv, gq, gkv, wuq_e, wukv, ct, st, dq, dk, dv):
    seq = cq.shape[0]
    tr = 512

    def body(cq_ref, ckv_ref, gq_ref, gkv_ref, wuq_ref, wukv_ref, ct_ref, st_ref, dq_ref, dk_ref, dv_ref,
             dcq_ref, dckv_ref, dkr_ref, dwuq_ref, dwukv_ref, dgq_ref, dgkv_ref):
        @pl.when(pl.program_id(0) == 0)
        def _():
            dwuq_ref[...] = jnp.zeros_like(dwuq_ref)
            dwukv_ref[...] = jnp.zeros_like(dwukv_ref)
            dgq_ref[...] = jnp.zeros_like(dgq_ref)
            dgkv_ref[...] = jnp.zeros_like(dgkv_ref)

        lane = lax.broadcasted_iota(jnp.int32, (tr, LANES), 1)
        rope_lanes = jnp.logical_and(lane >= 64, lane < 96)
        ct_, st_ = ct_ref[...], st_ref[...]

        def rope_t(g):
            return ct_ * g + jnp.where(rope_lanes, _mla_rot(st_ * g, lane), 0.0)

        def norm_bwd(c, g, dn, dg_ref):
            r, _ = _rms(c, g)
            u = dn * g
            dg_ref[...] += jnp.sum(dn * c * r, axis=0, keepdims=True)
            return r * u - c * (r * r * r) * jnp.mean(u * c, axis=-1, keepdims=True)

        c, g = cq_ref[...], gq_ref[...]
        _, qn = _rms(c, g)
        dq_all = jnp.concatenate([rope_t(dq_ref[h] * MLA_SCALE) for h in range(MLA_HEADS)], axis=1).astype(BF16)
        dwuq_ref[...] += _dot(qn.astype(BF16), dq_all, TN)
        dcq_ref[...] = norm_bwd(c, g, _dot(dq_all, wuq_ref[...], NT), dgq_ref).astype(BF16)

        c, g = ckv_ref[...], gkv_ref[...]
        _, kvn = _rms(c, g)
        dkpe = jnp.zeros((tr, LANES), F32)
        parts = []
        for h in range(MLA_HEADS):
            dk_h, dv_h = dk_ref[h], dv_ref[h]
            if h % 2 == 0:
                dv_h = pltpu.roll(dv_h, 64, 1)
            parts.append(jnp.where(lane < 64, dk_h, dv_h))
            dkpe = dkpe + jnp.where(rope_lanes, dk_h, 0.0)
        dkv_all = jnp.concatenate(parts, axis=1).astype(BF16)
        dwukv_ref[...] += _dot(kvn.astype(BF16), dkv_all, TN)
        dckv_ref[...] = norm_bwd(c, g, _dot(dkv_all, wukv_ref[...], NT), dgkv_ref).astype(BF16)
        dkr_ref[...] = rope_t(dkpe).astype(BF16)

    row = lambda w: pl.BlockSpec((tr, w), lambda i: (i, 0))
    full = lambda a: pl.BlockSpec(a.shape, lambda i: (0,) * a.ndim)
    head = pl.BlockSpec((MLA_HEADS, tr, LANES), lambda i: (0, i, 0))
    return pl.pallas_call(
        body, name="mla_pre_bwd", grid=(seq // tr,),
        in_specs=[row(Q_LORA), row(KV_LORA), full(gq), full(gkv), full(wuq_e), full(wukv), row(LANES), row(LANES),
                  head, head, head],
        out_specs=[row(Q_LORA), row(KV_LORA), row(LANES), full(wuq_e), full(wukv), full(gq), full(gkv)],
        out_shape=[jax.ShapeDtypeStruct((seq, Q_LORA), BF16), jax.ShapeDtypeStruct((seq, KV_LORA), BF16),
                   jax.ShapeDtypeStruct((seq, LANES), BF16), jax.ShapeDtypeStruct(wuq_e.shape, F32),
                   jax.ShapeDtypeStruct(wukv.shape, F32), jax.ShapeDtypeStruct(gq.shape, F32),
                   jax.ShapeDtypeStruct(gkv.shape, F32)],
        compiler_params=_cp(("arbitrary",), VMEM_LIMIT),
    )(cq, ckv, gq, gkv, wuq_e, wukv, ct, st, dq, dk, dv)


def _in_bwd(dz, dcq, dckv, dgates, dqr, dkr, dvb, dkrope, cd, sd, w_in_p):
    seq = dz.shape[0]
    tr = 512

    def body(dz_ref, dcq_ref, dckv_ref, dg_ref, dqr_ref, dkr_ref, dvb_ref, dkp_ref, cd_ref, sd_ref, w_ref, gx_ref, dh_ref):
        lane = lax.broadcasted_iota(jnp.int32, (tr, LANES), 1)
        rot_lanes = lane % 64 < DIL_ROT
        cd_, sd_ = cd_ref[...], sd_ref[...]

        def rope_t(g):
            return cd_ * g + jnp.where(rot_lanes, _dil_rot(sd_ * g, lane), 0.0)

        dq = [rope_t(dqr_ref[:, LANES * p:LANES * (p + 1)].astype(F32) * DIL_SCALE).astype(BF16) for p in range(4)]
        dk = [rope_t(dkr_ref[:, LANES * p:LANES * (p + 1)].astype(F32)).astype(BF16) for p in range(4)]
        dh = jnp.concatenate([dcq_ref[...], dckv_ref[...], dg_ref[...]] + dq + dk + [dvb_ref[...], dkp_ref[...]], axis=1)
        dh_ref[...] = dh
        gx_ref[...] = ALPHA * dz_ref[...] + _dot(dh, w_ref[...], NT)

    row = lambda w: pl.BlockSpec((tr, w), lambda i: (i, 0))
    return pl.pallas_call(
        body, name="in_bwd", grid=(seq // tr,),
        in_specs=[row(D_MODEL), row(Q_LORA), row(KV_LORA), row(D_MODEL), row(512), row(512), row(512), row(LANES),
                  row(LANES), row(LANES), pl.BlockSpec((D_MODEL, IN_WIDTH_PAD), lambda i: (0, 0))],
        out_specs=[row(D_MODEL), row(IN_WIDTH_PAD)],
        out_shape=[jax.ShapeDtypeStruct((seq, D_MODEL), F32), jax.ShapeDtypeStruct((seq, IN_WIDTH_PAD), BF16)],
        compiler_params=_cp(("arbitrary",), VMEM_LIMIT),
    )(dz, dcq, dckv, dgates, dqr, dkr, dvb, dkrope, cd, sd, w_in_p)


def _dw_in(x, dh):
    seq = dh.shape[0]
    tk = 512
    tn = IN_WIDTH_PAD // 2

    def body(x_ref, dh_ref, o_ref):
        @pl.when(pl.program_id(1) == 0)
        def _():
            o_ref[...] = jnp.zeros_like(o_ref)

        o_ref[...] += _dot(x_ref[...].astype(BF16), dh_ref[...], TN)

    return pl.pallas_call(
        body, name="dw_in", grid=(2, seq // tk),
        in_specs=[pl.BlockSpec((tk, D_MODEL), lambda n, k: (k, 0)), pl.BlockSpec((tk, tn), lambda n, k: (k, n))],
        out_specs=pl.BlockSpec((D_MODEL, tn), lambda n, k: (0, n)),
        out_shape=jax.ShapeDtypeStruct((D_MODEL, IN_WIDTH_PAD), F32),
        compiler_params=_cp(("arbitrary", "arbitrary"), VMEM_LIMIT),
    )(x, dh)


def _adamw(w, g, m, v, name):
    rows, cols = w.shape
    tc = 256 if cols % 256 == 0 and rows * cols > 2 ** 18 else cols

    def body(w_ref, g_ref, m_ref, v_ref, d_ref, nm_ref, nv_ref):
        g_ = g_ref[...]
        nm = ADAM_B1 * m_ref[...] + (1.0 - ADAM_B1) * g_
        nv = ADAM_B2 * v_ref[...] + (1.0 - ADAM_B2) * jnp.square(g_)
        m_hat = nm / (1.0 - ADAM_B1 ** ADAM_STEP)
        v_hat = nv / (1.0 - ADAM_B2 ** ADAM_STEP)
        d_ref[...] = -ADAM_LR * (m_hat / (jnp.sqrt(v_hat) + ADAM_EPS) + ADAM_WD * w_ref[...])
        nm_ref[...] = nm
        nv_ref[...] = nv

    spec = pl.BlockSpec((rows, tc), lambda i: (0, i))
    return pl.pallas_call(
        body, name=name, grid=(cols // tc,), in_specs=[spec] * 4, out_specs=[spec] * 3,
        out_shape=[jax.ShapeDtypeStruct(w.shape, F32)] * 3, compiler_params=_cp(("arbitrary",)),
    )(w, g, m, v)


def _pad_row(v):
    return jnp.pad(v.reshape(1, -1), ((0, 0), (0, D_MODEL - v.shape[-1])))


def _local_step(x2, target, w_in_p, w_uq_f, wukv_f, w_out_f, q_norm_g, kv_norm_g, ln_g, ln_b):
    seq = x2.shape[0]
    wuq_e = jnp.pad(w_uq_f.reshape(Q_LORA, MLA_HEADS, 96), ((0, 0), (0, 0), (0, 32))).reshape(Q_LORA, MLA_HEADS * LANES)
    ct, st, cd, sd = _rope_tables(seq)
    gq = q_norm_g.reshape(1, Q_LORA)
    gkv = kv_norm_g.reshape(1, KV_LORA)

    cq, ckv, gates, qkv, kr = _proj(x2, w_in_p)
    q_e, k_e, v_e = _mla_pre(cq, ckv, kr, gq, gkv, wuq_e, wukv_f, ct, st)
    o_a, lse_a = _mla_fwd(q_e, k_e, v_e)
    qr, krot = _dil_pre(qkv, cd, sd)
    o_b, lse_b = _dil_fwd(qr, krot, qkv)

    dz, d_o, d_gates, dw_out, dln_g, dln_b, loss_part = _post(
        x2, o_a, o_b, gates, w_out_f, ln_g.reshape(1, D_MODEL), ln_b.reshape(1, D_MODEL), target)
    dq_e, dk_e, dv_e = _mla_bwd(q_e, k_e, v_e, d_o, o_a, lse_a)
    dqr, dkr, dvb = _dil_bwd(qr, krot, qkv, d_o, o_b, lse_b)
    dcq, dckv, dkrope, dwuq_e, dwukv, dgq, dgkv = _mla_pre_bwd(cq, ckv, gq, gkv, wuq_e, wukv_f, ct, st, dq_e, dk_e, dv_e)
    grad_x, dh = _in_bwd(dz, dcq, dckv, d_gates, dqr, dkr, dvb, dkrope, cd, sd, w_in_p)
    dw_in = _unpermute_dw_in(_dw_in(x2, dh))
    dw_uq = dwuq_e.reshape(Q_LORA, MLA_HEADS, LANES)[:, :, :96].reshape(Q_LORA, MLA_HEADS * 96)
    return loss_part, grad_x, dw_in, dw_uq, dwukv, dw_out, dgq, dgkv, dln_g, dln_b


def kernel(x, w_in, q_norm_g, kv_norm_g, w_uq, w_ukv, w_out, ln_g, ln_b, loss_target, m_w_in, m_q_norm_g, m_kv_norm_g, m_w_uq, m_w_ukv, m_w_out, m_ln_g, m_ln_b, v_w_in, v_q_norm_g, v_kv_norm_g, v_w_uq, v_w_ukv, v_w_out, v_ln_g, v_ln_b):
    seq = x.shape[1]
    x2 = x.reshape(seq, D_MODEL)
    target = loss_target.reshape(seq, D_MODEL)

    g_w_in, g_w_uq, g_w_ukv, g_w_out = _all_gather_weights([w_in, w_uq, w_ukv, w_out])
    by_cols = lambda g: jnp.concatenate([g[j] for j in range(N_SHARD)], axis=1)
    loss_part, grad_x, dw_in, dw_uq, dwukv, dw_out, dgq, dgkv, dln_g, dln_b = _local_step(
        x2, target, _permute_w_in_shards(g_w_in), by_cols(g_w_uq), by_cols(g_w_ukv), g_w_out.reshape(D_MODEL, D_MODEL),
        q_norm_g, kv_norm_g, ln_g, ln_b)

    to_shards = lambda d: d.reshape(d.shape[0], N_SHARD, d.shape[1] // N_SHARD).transpose(1, 0, 2)
    grads = [to_shards(dw_in), to_shards(dw_uq), to_shards(dwukv), dw_out.reshape(N_SHARD, 256, D_MODEL)]
    small = jnp.concatenate([_pad_row(dgq), _pad_row(dgkv), dln_g, dln_b, _pad_row(loss_part),
                             jnp.zeros((3, D_MODEL), F32)], axis=0)
    *chip_sums, smalls = _reduce_over_sibling(grads, small)
    g_in, g_uq, g_ukv, g_out = _reduce_over_chips(chip_sums)
    small_sum = _sum_smalls(smalls)
    loss = small_sum[4, 0]

    big = [[o.T for o in _adamw(w.T, g.T, m.T, v.T, name)] for w, g, m, v, name in (
        (w_in, g_in, m_w_in, v_w_in, "adamw_w_in"), (w_uq, g_uq, m_w_uq, v_w_uq, "adamw_w_uq"))]
    big += [_adamw(w, g, m, v, name) for w, g, m, v, name in (
        (w_ukv, g_ukv, m_w_ukv, v_w_ukv, "adamw_w_ukv"), (w_out, g_out, m_w_out, v_w_out, "adamw_w_out"))]
    vec = lambda a, b, c_, d: jnp.concatenate([_pad_row(a), _pad_row(b), _pad_row(c_), _pad_row(d),
                                               jnp.zeros((4, D_MODEL), F32)], axis=0)
    sw = vec(q_norm_g, kv_norm_g, ln_g, ln_b)
    sm = vec(m_q_norm_g, m_kv_norm_g, m_ln_g, m_ln_b)
    sv = vec(v_q_norm_g, v_kv_norm_g, v_ln_g, v_ln_b)
    sg = jnp.concatenate([small_sum[:4], jnp.zeros((4, D_MODEL), F32)], axis=0)
    s_delta, s_m, s_v = _adamw(sw, sg, sm, sv, "adamw_vectors")

    def vectors(a):
        return [a[0, :Q_LORA], a[1, :KV_LORA], a[2], a[3]]

    def ordered(bigs, smalls_):
        return [bigs[0], smalls_[0], smalls_[1], bigs[1], bigs[2], bigs[3], smalls_[2], smalls_[3]]

    grads_out = ordered([g_in, g_uq, g_ukv, g_out], vectors(small_sum))
    deltas = ordered([b[0] for b in big], vectors(s_delta))
    new_m = ordered([b[1] for b in big], vectors(s_m))
    new_v = ordered([b[2] for b in big], vectors(s_v))
    return (loss, grad_x.reshape(x.shape), *grads_out, *deltas, *new_m, *new_v)
```

```python
import functools

import jax
import jax.numpy as jnp
import numpy as np
from jax import lax
from jax.experimental import pallas as pl
from jax.experimental.pallas import tpu as pltpu

F32 = jnp.float32
BF16 = jnp.bfloat16

D_MODEL = 1024
ROPE_THETA = 500000.0
BLOCK = 128
NEG = -1e30
RMS_EPS = 1e-6
LN_EPS = 1e-5

MLA_HEADS = 8
MLA_NOPE = 64
MLA_ROPE = 32
Q_LORA = 384
KV_LORA = 256
DIL_HEADS = 8
DIL_HEAD_DIM = 64
DIL_ROT = 16
DIL_DILATIONS = (1, 4, 16)
IN_WIDTH = 3232
IN_WIDTH_PAD = 3328
ONES_LANE = (64, 0)
MLA_SCALE = (MLA_NOPE + MLA_ROPE) ** -0.5
DIL_SCALE = DIL_HEAD_DIM ** -0.5
ALPHA = 2.0 ** 0.25

ADAM_LR = 0.001
ADAM_B1 = 0.9
ADAM_B2 = 0.999
ADAM_EPS = 1e-08
ADAM_WD = 0.01
ADAM_STEP = 10

N_SHARD = 4
SHARD_SHAPES = ((1024, 808), (384, 192), (256, 256), (256, 1024))
ROW_CHUNK = 64
LANES = 128
VMEM_LIMIT = 56 * 1024 * 1024
MESH = pl.DeviceIdType.MESH

NT = (((1,), (1,)), ((), ()))
TN = (((0,), (0,)), ((), ()))


def _cp(sem=None, vmem=None):
    return pltpu.CompilerParams(dimension_semantics=sem, vmem_limit_bytes=vmem)


def _dot(a, b, dims=None):
    if dims is None:
        return jnp.dot(a, b, preferred_element_type=F32)
    return lax.dot_general(a, b, dims, preferred_element_type=F32)


def _rope_tables(seq):
    f32 = np.float32
    pos = np.arange(seq, dtype=f32)[:, None]
    one, zero = np.ones((seq, 64), f32), np.zeros((seq, 64), f32)

    def cos_sin(dim):
        inv = np.power(f32(ROPE_THETA), -np.arange(0, dim, 2, dtype=f32) / f32(dim)).astype(f32)
        ang = (pos * inv[None, :]).astype(f32)
        return np.cos(ang).astype(f32), np.sin(ang).astype(f32)

    cos, sin = cos_sin(MLA_ROPE)
    ct = np.concatenate([one, cos, cos, zero[:, :32]], axis=1)
    st = np.concatenate([zero, -sin, sin, zero[:, :32]], axis=1)
    cos, sin = cos_sin(DIL_ROT)
    cd = np.concatenate([cos, cos, one[:, :48]], axis=1)
    sd = np.concatenate([-sin, sin, zero[:, :48]], axis=1)
    return tuple(jnp.asarray(t) for t in (ct, st, np.tile(cd, (1, 2)), np.tile(sd, (1, 2))))


W_IN_ORDER = ((0, 640), (672, 1184), (2720, 3232), (1184, 2720), None, (640, 672))


def _permute_w_in(w):
    z = jnp.zeros((w.shape[0], 64), w.dtype)
    parts = [z if r is None else w[:, r[0]:r[1]] for r in W_IN_ORDER]
    return jnp.concatenate(parts + [z[:, :32]], axis=1)


def _permute_w_in_shards(g):
    width = g.shape[2]
    z = jnp.zeros((g.shape[1], 64), g.dtype)
    parts = []
    for r in W_IN_ORDER:
        if r is None:
            parts.append(z)
            continue
        for j in range(N_SHARD):
            lo, hi = max(r[0], width * j), min(r[1], width * (j + 1))
            if lo < hi:
                parts.append(g[j, :, lo - width * j:hi - width * j])
    return jnp.concatenate(parts + [z[:, :32]], axis=1)


def _unpermute_dw_in(dw):
    return jnp.concatenate([dw[:, 0:640], dw[:, 3264:3296], dw[:, 640:1152], dw[:, 1664:3200],
                            dw[:, 1152:1664]], axis=1)


def _position():
    return lax.axis_index("x"), lax.axis_index("y"), lax.axis_index("c")


def _halves(c, rows):
    hr = rows // 2
    return pl.ds(pl.multiple_of(c * hr, 8), hr), pl.ds(pl.multiple_of((1 - c) * hr, 8), hr)


def _for_row_chunks(rows, fn):
    def step(i, carry):
        fn(pl.multiple_of(i * ROW_CHUNK, ROW_CHUNK))
        return carry

    lax.fori_loop(0, rows // ROW_CHUNK, step, 0)


def _all_gather_weights(shards):
    n = len(shards)

    def body(*refs):
        ins, outs = refs[:n], refs[n:2 * n]
        send_sems, recv_sems = refs[2 * n:]
        x, y, c = _position()
        me = 2 * x + y
        chips = [(1 - x, y), (x, 1 - y), (1 - x, 1 - y)]
        for a in range(n):
            def cast(r, a=a):
                outs[a][me, pl.ds(r, ROW_CHUNK), :] = ins[a][pl.ds(r, ROW_CHUNK), :].astype(BF16)

            _for_row_chunks(SHARD_SHAPES[a][0], cast)

        def copy(k, a, slot, rows, to):
            ref = outs[a].at[slot, rows]
            return pltpu.make_async_remote_copy(
                src_ref=ref, dst_ref=ref, send_sem=send_sems.at[k * n + a], recv_sem=recv_sems.at[k * n + a],
                device_id=to, device_id_type=MESH)

        half = [_halves(c, SHARD_SHAPES[a][0])[0] for a in range(n)]
        other = [_halves(c, SHARD_SHAPES[a][0])[1] for a in range(n)]
        first = [copy(k, a, me, half[a], (px, py, c)) for k, (px, py) in enumerate(chips) for a in range(n)]
        for cp in first:
            cp.start()
        passed = []
        for k, (px, py) in enumerate(chips):
            for a in range(n):
                copy(k, a, 2 * px + py, half[a], (x, y, c)).wait_recv()
                cp = copy(3 + k, a, 2 * px + py, half[a], (x, y, 1 - c))
                cp.start()
                passed.append(cp)
        for k, (px, py) in enumerate(chips):
            for a in range(n):
                copy(3 + k, a, 2 * px + py, other[a], (x, y, c)).wait_recv()
        for cp in first + passed:
            cp.wait_send()

    vmem = pl.BlockSpec(memory_space=pltpu.VMEM)
    return pl.pallas_call(
        body, name="all_gather_weights",
        out_shape=[jax.ShapeDtypeStruct((N_SHARD,) + s, BF16) for s in SHARD_SHAPES],
        in_specs=[vmem] * n, out_specs=[vmem] * n,
        scratch_shapes=[pltpu.SemaphoreType.DMA((6 * n,)), pltpu.SemaphoreType.DMA((6 * n,))],
        compiler_params=_cp(None, VMEM_LIMIT),
    )(*shards)


def _reduce_over_sibling(grads, small):
    n = len(grads)

    def body(*refs):
        g_hbm, sm = refs[:n], refs[n]
        sums, smalls = refs[n + 1:2 * n + 1], refs[2 * n + 1]
        stage, got = refs[2 * n + 2:3 * n + 2], refs[3 * n + 2:4 * n + 2]
        send_sems, recv_sems, local_sems = refs[4 * n + 2:]
        x, y, c = _position()
        me = 4 * x + 2 * y + c
        loads = [pltpu.make_async_copy(g_hbm[a], stage[a], local_sems.at[a]) for a in range(n)]
        for ld in loads:
            ld.start()
        smalls[me] = sm[...]
        sends = []
        for rel in range(1, 8):
            px = 1 - x if rel // 4 else x
            py = 1 - y if (rel // 2) % 2 else y
            pc = 1 - c if rel % 2 else c
            cp = pltpu.make_async_remote_copy(
                src_ref=sm, dst_ref=smalls.at[me], send_sem=send_sems.at[n + rel], recv_sem=recv_sems.at[n + rel],
                device_id=(px, py, pc), device_id_type=MESH)
            cp.start()
            sends.append((cp, 4 * px + 2 * py + pc))
        swaps = []
        for a in range(n):
            loads[a].wait()
            _, other = _halves(c, SHARD_SHAPES[a][0])
            cp = pltpu.make_async_remote_copy(
                src_ref=stage[a].at[:, other], dst_ref=got[a], send_sem=send_sems.at[a], recv_sem=recv_sems.at[a],
                device_id=(x, y, 1 - c), device_id_type=MESH)
            cp.start()
            swaps.append(cp)
        for a in range(n):
            swaps[a].wait_recv()
            hr = SHARD_SHAPES[a][0] // 2
            for k in range(N_SHARD):
                def add(r, a=a, k=k, hr=hr):
                    mine = stage[a][k, pl.ds(pl.multiple_of(c * hr + r, ROW_CHUNK), ROW_CHUNK), :]
                    sums[a][k, pl.ds(r, ROW_CHUNK), :] = (mine + got[a][k, pl.ds(r, ROW_CHUNK), :]).astype(BF16)

                _for_row_chunks(hr, add)
        for rel, (cp, peer) in enumerate(sends, start=1):
            pltpu.make_async_remote_copy(
                src_ref=sm, dst_ref=smalls.at[peer], send_sem=send_sems.at[n + rel], recv_sem=recv_sems.at[n + rel],
                device_id=(x, y, c), device_id_type=MESH).wait_recv()
        for cp in swaps:
            cp.wait_send()
        for cp, _ in sends:
            cp.wait_send()

    vmem = pl.BlockSpec(memory_space=pltpu.VMEM)
    half = [(N_SHARD, r // 2, cols) for r, cols in SHARD_SHAPES]
    return pl.pallas_call(
        body, name="reduce_over_sibling",
        out_shape=[jax.ShapeDtypeStruct(s, BF16) for s in half] + [jax.ShapeDtypeStruct((8,) + small.shape, F32)],
        in_specs=[pl.BlockSpec(memory_space=pl.ANY)] * n + [vmem], out_specs=[vmem] * (n + 1),
        scratch_shapes=[pltpu.VMEM((N_SHARD,) + s, F32) for s in SHARD_SHAPES] + [pltpu.VMEM(s, F32) for s in half]
        + [pltpu.SemaphoreType.DMA((n + 8,)), pltpu.SemaphoreType.DMA((n + 8,)), pltpu.SemaphoreType.DMA((n,))],
        compiler_params=_cp(None, VMEM_LIMIT),
    )(*grads, small)


def _reduce_over_chips(sums):
    n = len(sums)

    def body(*refs):
        h, outs, got = refs[:n], refs[n:2 * n], refs[2 * n:3 * n]
        send_sems, recv_sems = refs[3 * n:]
        x, y, c = _position()
        me = 2 * x + y
        chips = [(1 - x, y), (x, 1 - y), (1 - x, 1 - y)]
        sends = []
        for k, (px, py) in enumerate(chips):
            for a in range(n):
                cp = pltpu.make_async_remote_copy(
                    src_ref=h[a].at[2 * px + py], dst_ref=got[a].at[k], send_sem=send_sems.at[k * n + a],
                    recv_sem=recv_sems.at[k * n + a], device_id=(px, py, c), device_id_type=MESH)
                cp.start()
                sends.append(cp)
        for cp in sends:
            cp.wait_recv()
        joins = []
        for a in range(n):
            hr = SHARD_SHAPES[a][0] // 2
            half, other = _halves(c, SHARD_SHAPES[a][0])

            def add(r, a=a, hr=hr):
                rows = pl.ds(r, ROW_CHUNK)
                total = h[a][me, rows, :].astype(F32)
                for k in range(3):
                    total = total + got[a][k, rows, :].astype(F32)
                outs[a][pl.ds(pl.multiple_of(c * hr + r, ROW_CHUNK), ROW_CHUNK), :] = total

            _for_row_chunks(hr, add)
            cp = pltpu.make_async_remote_copy(
                src_ref=outs[a].at[half], dst_ref=outs[a].at[half], send_sem=send_sems.at[3 * n + a],
                recv_sem=recv_sems.at[3 * n + a], device_id=(x, y, 1 - c), device_id_type=MESH)
            cp.start()
            joins.append(cp)
        for a in range(n):
            other = _halves(c, SHARD_SHAPES[a][0])[1]
            pltpu.make_async_remote_copy(
                src_ref=outs[a].at[other], dst_ref=outs[a].at[other], send_sem=send_sems.at[3 * n + a],
                recv_sem=recv_sems.at[3 * n + a], device_id=(x, y, c), device_id_type=MESH).wait_recv()
        for cp in sends + joins:
            cp.wait_send()

    vmem = pl.BlockSpec(memory_space=pltpu.VMEM)
    return pl.pallas_call(
        body, name="reduce_over_chips",
        out_shape=[jax.ShapeDtypeStruct(s, F32) for s in SHARD_SHAPES],
        in_specs=[vmem] * n, out_specs=[vmem] * n,
        scratch_shapes=[pltpu.VMEM((3, r // 2, cols), BF16) for r, cols in SHARD_SHAPES]
        + [pltpu.SemaphoreType.DMA((4 * n,)), pltpu.SemaphoreType.DMA((4 * n,))],
        compiler_params=_cp(None, VMEM_LIMIT),
    )(*sums)


def _sum_smalls(smalls):
    def body(s, o):
        acc = s[0]
        for d in range(1, 8):
            acc = acc + s[d]
        o[...] = acc

    return pl.pallas_call(body, name="sum_smalls", out_shape=jax.ShapeDtypeStruct(smalls.shape[1:], F32))(smalls)


def _proj(x, w_in_p):
    seq = x.shape[0]
    tr = 512
    splits = ((0, 384), (384, 640), (640, 1664), (1664, 3200), (3200, 3328))

    def body(x_ref, w_ref, *outs):
        xb = x_ref[...].astype(BF16)
        for (lo, hi), o in zip(splits, outs):
            o[...] = _dot(xb, w_ref[:, lo:hi])

    return pl.pallas_call(
        body, name="proj", grid=(seq // tr,),
        in_specs=[pl.BlockSpec((tr, D_MODEL), lambda i: (i, 0)), pl.BlockSpec((D_MODEL, IN_WIDTH_PAD), lambda i: (0, 0))],
        out_specs=[pl.BlockSpec((tr, hi - lo), lambda i: (i, 0)) for lo, hi in splits],
        out_shape=[jax.ShapeDtypeStruct((seq, hi - lo), F32) for lo, hi in splits],
        compiler_params=_cp(("arbitrary",), VMEM_LIMIT),
    )(x, w_in_p)


def _mla_rot(t, lane):
    return jnp.where(lane < 80, pltpu.roll(t, 112, 1), pltpu.roll(t, 16, 1))


def _dil_rot(t, lane):
    return jnp.where(lane % 64 < 8, pltpu.roll(t, 120, 1), pltpu.roll(t, 8, 1))


def _rms(c, g):
    r = lax.rsqrt(jnp.mean(c * c, axis=-1, keepdims=True) + RMS_EPS)
    return r, c * r * g


def _mla_pre(cq, ckv, kr, gq, gkv, wuq_e, wukv, ct, st):
    seq = cq.shape[0]
    tr = 512

    def body(cq_ref, ckv_ref, kr_ref, gq_ref, gkv_ref, wuq_ref, wukv_ref, ct_ref, st_ref, q_out, k_out, v_out):
        lane = lax.broadcasted_iota(jnp.int32, (tr, LANES), 1)
        ct_, st_ = ct_ref[...], st_ref[...]

        def rope(t):
            return t * ct_ + _mla_rot(t, lane) * st_

        _, qn = _rms(cq_ref[...], gq_ref[...])
        q_all = _dot(qn.astype(BF16), wuq_ref[...])
        for h in range(MLA_HEADS):
            q_out[h] = (rope(q_all[:, LANES * h:LANES * (h + 1)]) * MLA_SCALE).astype(BF16)
        _, kvn = _rms(ckv_ref[...], gkv_ref[...])
        kv_all = _dot(kvn.astype(BF16), wukv_ref[...])
        kpe = rope(kr_ref[...])
        for h in range(MLA_HEADS):
            kv_h = kv_all[:, LANES * h:LANES * (h + 1)]
            k_out[h] = jnp.where(lane < 64, kv_h, kpe).astype(BF16)
            if h % 2:
                v = jnp.where(lane >= 64, kv_h, 0.0)
            else:
                v = jnp.where(lane < 64, pltpu.roll(kv_h, 64, 1), 0.0)
            v_out[h] = jnp.where(lane == ONES_LANE[h % 2], 1.0, v).astype(BF16)

    row = lambda w: pl.BlockSpec((tr, w), lambda i: (i, 0))
    full = lambda a: pl.BlockSpec(a.shape, lambda i: (0,) * a.ndim)
    head = pl.BlockSpec((MLA_HEADS, tr, LANES), lambda i: (0, i, 0))
    return pl.pallas_call(
        body, name="mla_pre", grid=(seq // tr,),
        in_specs=[row(Q_LORA), row(KV_LORA), row(LANES), full(gq), full(gkv), full(wuq_e), full(wukv), row(LANES), row(LANES)],
        out_specs=[head] * 3,
        out_shape=[jax.ShapeDtypeStruct((MLA_HEADS, seq, LANES), BF16)] * 3,
        compiler_params=_cp(("arbitrary",), VMEM_LIMIT),
    )(cq, ckv, kr, gq, gkv, wuq_e, wukv, ct, st)


def _mla_fwd(q, k, v):
    seq = q.shape[1]
    tq = 512
    nq = seq // tq

    def body(q_ref, k_ref, v_ref, o_ref, lse_ref, m_s, acc_s, s_buf):
        i = pl.program_id(1)
        row = lax.broadcasted_iota(jnp.int32, (tq, tq), 0)
        col = lax.broadcasted_iota(jnp.int32, (tq, tq), 1)
        lane = lax.broadcasted_iota(jnp.int32, (tq, LANES), 1)
        m_s[...] = jnp.full((2, tq, LANES), NEG, F32)
        acc_s[...] = jnp.zeros((2, tq, LANES), F32)

        def block(j):
            return pl.ds(pl.multiple_of(j * tq, tq), tq)

        def scores(hh, j):
            return _dot(q_ref[hh], k_ref[hh, block(j), :], NT)

        def consume(hh, j, s):
            m_prev = m_s[hh]
            m_new = jnp.maximum(m_prev, jnp.max(s, axis=1, keepdims=True))
            p = jnp.exp(s - m_new[:, :1])
            acc_s[hh] = jnp.exp(m_prev - m_new) * acc_s[hh] + _dot(p.astype(BF16), v_ref[hh, block(j), :])
            m_s[hh] = m_new

        for hh in range(2):
            s_buf[0, hh] = scores(hh, 0)

        def full_step(j, carry):
            slot = j & 1
            for hh in range(2):
                s = s_buf[slot, hh]
                s_buf[1 - slot, hh] = scores(hh, j + 1)
                consume(hh, j, s)
            return carry

        lax.fori_loop(0, i, full_step, 0)
        total = jnp.zeros((tq, LANES), F32)
        for hh in range(2):
            consume(hh, i, jnp.where(col <= row, s_buf[i & 1, hh], NEG))
            acc = acc_s[hh]
            l = acc[:, ONES_LANE[hh]:ONES_LANE[hh] + 1]
            mine = (lane >= 64) if hh else (lane < 64)
            total = total + jnp.where(mine, acc / l, 0.0)
            lse_ref[hh] = m_s[hh] + jnp.log(l)
        o_ref[...] = total

    kv_spec = pl.BlockSpec((2, seq, LANES), lambda p, i: (p, 0, 0))
    return pl.pallas_call(
        body, name="mla_fwd", grid=(MLA_HEADS // 2, nq),
        in_specs=[pl.BlockSpec((2, tq, LANES), lambda p, i: (p, i, 0)), kv_spec, kv_spec],
        out_specs=[pl.BlockSpec((tq, LANES), lambda p, i: (i, p)), pl.BlockSpec((2, tq, LANES), lambda p, i: (p, i, 0))],
        out_shape=[jax.ShapeDtypeStruct((seq, 4 * LANES), F32), jax.ShapeDtypeStruct((MLA_HEADS, seq, LANES), F32)],
        scratch_shapes=[pltpu.VMEM((2, tq, LANES), F32), pltpu.VMEM((2, tq, LANES), F32),
                        pltpu.VMEM((2, 2, tq, tq), F32)],
        compiler_params=_cp(("arbitrary", "arbitrary"), VMEM_LIMIT),
    )(q, k, v)


def _dil_pre(qkv, cd, sd):
    seq = qkv.shape[0]
    tr = 512

    def body(q_ref, k_ref, cd_ref, sd_ref, qr_ref, kr_ref):
        lane = lax.broadcasted_iota(jnp.int32, (tr, LANES), 1)
        cd_, sd_ = cd_ref[...], sd_ref[...]
        for p in range(4):
            cols = slice(LANES * p, LANES * (p + 1))
            t = q_ref[:, cols]
            qr_ref[:, cols] = (t * cd_ + _dil_rot(t, lane) * sd_) * DIL_SCALE
            t = k_ref[:, cols]
            kr_ref[:, cols] = t * cd_ + _dil_rot(t, lane) * sd_

    blk = lambda j: pl.BlockSpec((tr, 4 * LANES), lambda i: (i, j))
    tab = pl.BlockSpec((tr, LANES), lambda i: (i, 0))
    return pl.pallas_call(
        body, name="dil_pre", grid=(seq // tr,),
        in_specs=[blk(0), blk(1), tab, tab], out_specs=[blk(0), blk(0)],
        out_shape=[jax.ShapeDtypeStruct((seq, 4 * LANES), F32)] * 2,
        compiler_params=_cp(("arbitrary",)),
    )(qkv, qkv, cd, sd)


DIL_Q_FWD = 2 * BLOCK
DIL_Q_BWD = BLOCK


def _dil_tile_index(t, d, seq, nq):
    per_class = seq // (nq * d)
    shift = per_class.bit_length() - 1
    r = t >> shift
    n = t & (per_class - 1)
    start = r + (nq * d) * n
    prev = jnp.maximum(start - BLOCK * d, r)
    if d == 1:
        start = pl.multiple_of(start, nq)
        prev = pl.multiple_of(prev, BLOCK)
    return (n == 0).astype(jnp.int32), start, prev


def _dil_rows(start, d, size):
    return pl.ds(start, size) if d == 1 else pl.ds(start, size, stride=d)


def _dil_bias(nq):
    i = lax.broadcasted_iota(jnp.int32, (2 * nq, BLOCK + nq), 0) % nq
    j = lax.broadcasted_iota(jnp.int32, (2 * nq, BLOCK + nq), 1)
    band = (j >= i) & (j <= i + BLOCK)
    return jnp.where(band, 0.0, NEG), jnp.where(band & (j >= BLOCK), 0.0, NEG)


def _stack_heads(t, lane):
    return jnp.concatenate([jnp.where(lane < 64, t, 0.0), jnp.where(lane >= 64, t, 0.0)], axis=0)


def _unstack_heads(t, lane):
    nq = t.shape[0] // 2
    return jnp.where(lane < 64, t[:nq], t[nq:])


def _dil_fwd(qr, kr, qkv):
    seq = qr.shape[0]
    nq = DIL_Q_FWD
    n_tiles = seq // nq
    assert seq % (nq * max(DIL_DILATIONS)) == 0

    def body(q_ref, k_ref, v_ref, o_ref, lse_ref, m_s, l_s, n_s, bias_s):
        lane = lax.broadcasted_iota(jnp.int32, (nq, LANES), 1)
        bias_s[0], bias_s[1] = _dil_bias(nq)
        for bi, d in enumerate(DIL_DILATIONS):

            def tile(t, carry, d=d, bi=bi):
                first, start, prev = _dil_tile_index(t, d, seq, nq)
                rows, prows = _dil_rows(start, d, nq), _dil_rows(prev, d, BLOCK)
                qst = _stack_heads(q_ref[rows, :], lane).astype(BF16)
                kcat = jnp.concatenate([k_ref[prows, :], k_ref[rows, :]], axis=0).astype(BF16)
                vcat = jnp.concatenate([v_ref[prows, :], v_ref[rows, :]], axis=0).astype(BF16)
                s = _dot(qst, kcat, NT) + bias_s[first]
                m = jnp.max(s, axis=1, keepdims=True)
                p = jnp.exp(s - m)
                l2 = _unstack_heads(jnp.sum(p, axis=1, keepdims=True) + jnp.zeros((2 * nq, LANES), F32), lane)
                m2 = _unstack_heads(m + jnp.zeros((2 * nq, LANES), F32), lane)
                num2 = _unstack_heads(_dot(p.astype(BF16), vcat), lane)
                if bi == 0:
                    m_s[rows, :] = m2
                    l_s[rows, :] = l2
                    n_s[rows, :] = num2
                else:
                    m_old = m_s[rows, :]
                    m_new = jnp.maximum(m_old, m2)
                    a = jnp.exp(m_old - m_new)
                    b = jnp.exp(m2 - m_new)
                    m_s[rows, :] = m_new
                    l_s[rows, :] = a * l_s[rows, :] + b * l2
                    n_s[rows, :] = a * n_s[rows, :] + b * num2
                return carry

            lax.fori_loop(0, n_tiles, tile, 0, unroll=2)
        o_ref[...] = n_s[...] / l_s[...]
        lse_ref[...] = m_s[...] + jnp.log(l_s[...])

    col = lambda off: pl.BlockSpec((seq, LANES), lambda p: (0, p + off))
    return pl.pallas_call(
        body, name="dil_fwd", grid=(4,),
        in_specs=[col(0), col(0), col(8)],
        out_specs=[col(0), pl.BlockSpec((None, seq, LANES), lambda p: (p, 0, 0))],
        out_shape=[jax.ShapeDtypeStruct((seq, 4 * LANES), F32), jax.ShapeDtypeStruct((4, seq, LANES), F32)],
        scratch_shapes=[pltpu.VMEM((seq, LANES), F32)] * 3 + [pltpu.VMEM((2, 2 * nq, BLOCK + nq), F32)],
        compiler_params=_cp(("arbitrary",), VMEM_LIMIT),
    )(qr, kr, qkv)


def _post(x, o_a, o_b, gates, w_out, ln_g, ln_b, target):
    seq = x.shape[0]
    tr = 512

    def body(x_ref, oa_ref, ob_ref, g_ref, w_ref, lg_ref, lb_ref, t_ref,
             dz_ref, do_ref, dg_ref, dw_ref, dlg_ref, dlb_ref, loss_ref):
        @pl.when(pl.program_id(0) == 0)
        def _():
            dw_ref[...] = jnp.zeros_like(dw_ref)
            dlg_ref[...] = jnp.zeros_like(dlg_ref)
            dlb_ref[...] = jnp.zeros_like(dlb_ref)
            loss_ref[...] = jnp.zeros_like(loss_ref)

        g = g_ref[...]
        sg = 1.0 / (1.0 + jnp.exp(-g))
        silu = g * sg
        o = jnp.concatenate([oa_ref[...], ob_ref[...]], axis=1)
        mixb = (o * silu).astype(BF16)
        w = w_ref[...]
        z = ALPHA * x_ref[...] + _dot(mixb, w)
        mu = jnp.mean(z, axis=-1, keepdims=True)
        zc = z - mu
        rstd = lax.rsqrt(jnp.mean(zc * zc, axis=-1, keepdims=True) + LN_EPS)
        xhat = zc * rstd
        lg = lg_ref[...]
        err = xhat * lg + lb_ref[...] - t_ref[...]
        loss_ref[...] += jnp.sum(err * err) * (0.5 / D_MODEL)
        dy = err * (1.0 / D_MODEL)
        dlg_ref[...] += jnp.sum(dy * xhat, axis=0, keepdims=True)
        dlb_ref[...] += jnp.sum(dy, axis=0, keepdims=True)
        dxh = dy * lg
        dz = rstd * (dxh - jnp.mean(dxh, axis=-1, keepdims=True) - xhat * jnp.mean(dxh * xhat, axis=-1, keepdims=True))
        dz_ref[...] = dz
        dzb = dz.astype(BF16)
        dmix = _dot(dzb, w, NT)
        do_ref[...] = dmix * silu
        dg_ref[...] = (dmix * o * (sg * (1.0 + g * (1.0 - sg)))).astype(BF16)
        dw_ref[...] += _dot(mixb, dzb, TN)

    row = lambda w: pl.BlockSpec((tr, w), lambda i: (i, 0))
    full = lambda s: pl.BlockSpec(s, lambda i: (0, 0))
    return pl.pallas_call(
        body, name="post", grid=(seq // tr,),
        in_specs=[row(D_MODEL), row(512), row(512), row(D_MODEL), full((D_MODEL, D_MODEL)), full((1, D_MODEL)),
                  full((1, D_MODEL)), row(D_MODEL)],
        out_specs=[row(D_MODEL), row(D_MODEL), row(D_MODEL), full((D_MODEL, D_MODEL)), full((1, D_MODEL)),
                   full((1, D_MODEL)), full((1, LANES))],
        out_shape=[jax.ShapeDtypeStruct((seq, D_MODEL), F32), jax.ShapeDtypeStruct((seq, D_MODEL), F32),
                   jax.ShapeDtypeStruct((seq, D_MODEL), BF16), jax.ShapeDtypeStruct((D_MODEL, D_MODEL), F32),
                   jax.ShapeDtypeStruct((1, D_MODEL), F32), jax.ShapeDtypeStruct((1, D_MODEL), F32),
                   jax.ShapeDtypeStruct((1, LANES), F32)],
        compiler_params=_cp(("arbitrary",), VMEM_LIMIT),
    )(x, o_a, o_b, gates, w_out, ln_g, ln_b, target)


def _mla_bwd(q, k, v, d_o, o, lse):
    seq = q.shape[1]
    tq = 512
    nq = seq // tq

    def body(q_ref, k_ref, v_ref, do_ref, o_ref, lse_ref, dq_ref, dk_ref, dv_ref, d_s, dk_s, dv_s, v_s):
        j = pl.program_id(1)
        lane = lax.broadcasted_iota(jnp.int32, (tq, LANES), 1)
        row = lax.broadcasted_iota(jnp.int32, (tq, tq), 0)
        col = lax.broadcasted_iota(jnp.int32, (tq, tq), 1)

        @pl.when(j == 0)
        def _():
            dq_ref[...] = jnp.zeros_like(dq_ref)

            def rowsum(i, carry):
                rows = pl.ds(pl.multiple_of(i * tq, tq), tq)
                prod = do_ref[rows, :] * o_ref[rows, :]
                for hh in range(2):
                    mine = (lane >= 64) if hh else (lane < 64)
                    total = jnp.sum(jnp.where(mine, prod, 0.0), axis=1, keepdims=True)
                    d_s[hh, rows, :] = total + jnp.zeros((tq, LANES), F32)
                return carry

            lax.fori_loop(0, nq, rowsum, 0)

        dk_s[...] = jnp.zeros_like(dk_s)
        dv_s[...] = jnp.zeros_like(dv_s)
        for hh in range(2):
            v_s[hh] = jnp.where(lane == ONES_LANE[hh], 0.0, v_ref[hh].astype(F32)).astype(BF16)

        def step(i, masked):
            rows = pl.ds(pl.multiple_of(i * tq, tq), tq)
            dob = do_ref[rows, :].astype(BF16)
            for hh in range(2):
                qb, kb, vb = q_ref[hh, rows, :], k_ref[hh], v_s[hh]
                s = _dot(qb, kb, NT)
                p = jnp.exp(s - lse_ref[hh, rows, :][:, :1])
                if masked:
                    p = jnp.where(col <= row, p, 0.0)
                dv_s[hh] += _dot(p.astype(BF16), dob, TN)
                dp = _dot(dob, vb, NT)
                ds = (p * (dp - d_s[hh, rows, :][:, :1])).astype(BF16)
                dk_s[hh] += _dot(ds, qb, TN)
                dq_ref[hh, rows, :] += _dot(ds, kb)

        def full_step(i, carry):
            step(i, False)
            return carry

        step(j, True)
        lax.fori_loop(j + 1, nq, full_step, 0)
        dk_ref[...] = dk_s[...]
        dv_ref[...] = dv_s[...]

    whole = pl.BlockSpec((2, seq, LANES), lambda p, j: (p, 0, 0))
    blk = pl.BlockSpec((2, tq, LANES), lambda p, j: (p, j, 0))
    pair = pl.BlockSpec((seq, LANES), lambda p, j: (0, p))
    shape = jax.ShapeDtypeStruct((MLA_HEADS, seq, LANES), F32)
    return pl.pallas_call(
        body, name="mla_bwd", grid=(MLA_HEADS // 2, nq),
        in_specs=[whole, blk, blk, pair, pair, whole],
        out_specs=[whole, blk, blk], out_shape=[shape] * 3,
        scratch_shapes=[pltpu.VMEM((2, seq, LANES), F32), pltpu.VMEM((2, tq, LANES), F32),
                        pltpu.VMEM((2, tq, LANES), F32), pltpu.VMEM((2, tq, LANES), BF16)],
        compiler_params=_cp(("arbitrary", "arbitrary"), VMEM_LIMIT),
    )(q, k, v, d_o, o, lse)


def _dil_bwd(qr, kr, qkv, d_o, o, lse):
    seq = qr.shape[0]
    nq = DIL_Q_BWD
    n_tiles = seq // nq
    chunk = 512

    def body(q_ref, k_ref, v_ref, do_ref, o_ref, lse_ref, dq_ref, dk_ref, dv_ref, d_s, dq_s, dk_s, dv_s, bias_s):
        lane = lax.broadcasted_iota(jnp.int32, (nq, LANES), 1)
        lanec = lax.broadcasted_iota(jnp.int32, (chunk, LANES), 1)
        bias_s[0], bias_s[1] = [b[:nq] for b in _dil_bias(nq)]

        def rowsum(i, carry):
            rows = pl.ds(pl.multiple_of(i * chunk, chunk), chunk)
            prod = do_ref[rows, :] * o_ref[rows, :]
            lo = jnp.sum(jnp.where(lanec < 64, prod, 0.0), axis=1, keepdims=True)
            hi = jnp.sum(jnp.where(lanec >= 64, prod, 0.0), axis=1, keepdims=True)
            d_s[rows, :] = jnp.where(lanec < 64, lo, hi)
            return carry

        lax.fori_loop(0, seq // chunk, rowsum, 0)
        dq_s[...] = jnp.zeros_like(dq_s)
        dk_s[...] = jnp.zeros_like(dk_s)
        dv_s[...] = jnp.zeros_like(dv_s)
        for d in DIL_DILATIONS:

            def tile(t, carry, d=d):
                first, start, prev = _dil_tile_index(t, d, seq, nq)
                rows, prows = _dil_rows(start, d, nq), _dil_rows(prev, d, BLOCK)
                q_t, do_t = q_ref[rows, :], do_ref[rows, :]
                lse_t, d_t = lse_ref[rows, :], d_s[rows, :]
                kcat = jnp.concatenate([k_ref[prows, :], k_ref[rows, :]], axis=0).astype(BF16)
                vcat = jnp.concatenate([v_ref[prows, :], v_ref[rows, :]], axis=0).astype(BF16)
                bias = bias_s[first]
                dq_t = jnp.zeros((nq, LANES), F32)
                dkcat = jnp.zeros((BLOCK + nq, LANES), F32)
                dvcat = jnp.zeros((BLOCK + nq, LANES), F32)
                for hh in range(2):
                    mine = (lane >= 64) if hh else (lane < 64)
                    c0 = 64 * hh
                    qh = jnp.where(mine, q_t, 0.0).astype(BF16)
                    doh = jnp.where(mine, do_t, 0.0).astype(BF16)
                    p = jnp.exp(_dot(qh, kcat, NT) + bias - lse_t[:, c0:c0 + 1])
                    dvcat = dvcat + _dot(p.astype(BF16), doh, TN)
                    dp = _dot(doh, vcat, NT)
                    ds = (p * (dp - d_t[:, c0:c0 + 1])).astype(BF16)
                    dq_t = dq_t + jnp.where(mine, _dot(ds, kcat), 0.0)
                    dkcat = dkcat + _dot(ds, qh, TN)
                dq_s[rows, :] += dq_t
                dk_s[prows, :] += dkcat[:BLOCK]
                dk_s[rows, :] += dkcat[BLOCK:]
                dv_s[prows, :] += dvcat[:BLOCK]
                dv_s[rows, :] += dvcat[BLOCK:]
                return carry

            lax.fori_loop(0, n_tiles, tile, 0, unroll=4)
        dq_ref[...] = dq_s[...].astype(BF16)
        dk_ref[...] = dk_s[...].astype(BF16)
        dv_ref[...] = dv_s[...].astype(BF16)

    col = lambda off: pl.BlockSpec((seq, LANES), lambda p: (0, p + off))
    shape = jax.ShapeDtypeStruct((seq, 4 * LANES), BF16)
    return pl.pallas_call(
        body, name="dil_bwd", grid=(4,),
        in_specs=[col(0), col(0), col(8), col(4), col(0), pl.BlockSpec((None, seq, LANES), lambda p: (p, 0, 0))],
        out_specs=[col(0)] * 3, out_shape=[shape] * 3,
        scratch_shapes=[pltpu.VMEM((seq, LANES), F32)] * 4 + [pltpu.VMEM((2, nq, BLOCK + nq), F32)],
        compiler_params=_cp(("arbitrary",), VMEM_LIMIT),
    )(qr, kr, qkv, d_o, o, lse)


def _mla_pre_bwd(cq, ckv, gq, gkv, wuq_e, wukv, ct, st, dq, dk, dv):
    seq = cq.shape[0]
    tr = 512

    def body(cq_ref, ckv_ref, gq_ref, gkv_ref, wuq_ref, wukv_ref, ct_ref, st_ref, dq_ref, dk_ref, dv_ref,
             dcq_ref, dckv_ref, dkr_ref, dwuq_ref, dwukv_ref, dgq_ref, dgkv_ref):
        @pl.when(pl.program_id(0) == 0)
        def _():
            dwuq_ref[...] = jnp.zeros_like(dwuq_ref)
            dwukv_ref[...] = jnp.zeros_like(dwukv_ref)
            dgq_ref[...] = jnp.zeros_like(dgq_ref)
            dgkv_ref[...] = jnp.zeros_like(dgkv_ref)

        lane = lax.broadcasted_iota(jnp.int32, (tr, LANES), 1)
        rope_lanes = jnp.logical_and(lane >= 64, lane < 96)
        ct_, st_ = ct_ref[...], st_ref[...]

        def rope_t(g):
            return ct_ * g + jnp.where(rope_lanes, _mla_rot(st_ * g, lane), 0.0)

        def norm_bwd(c, g, dn, dg_ref):
            r, _ = _rms(c, g)
            u = dn * g
            dg_ref[...] += jnp.sum(dn * c * r, axis=0, keepdims=True)
            return r * u - c * (r * r * r) * jnp.mean(u * c, axis=-1, keepdims=True)

        c, g = cq_ref[...], gq_ref[...]
        _, qn = _rms(c, g)
        dq_all = jnp.concatenate([rope_t(dq_ref[h] * MLA_SCALE) for h in range(MLA_HEADS)], axis=1).astype(BF16)
        dwuq_ref[...] += _dot(qn.astype(BF16), dq_all, TN)
        dcq_ref[...] = norm_bwd(c, g, _dot(dq_all, wuq_ref[...], NT), dgq_ref).astype(BF16)

        c, g = ckv_ref[...], gkv_ref[...]
        _, kvn = _rms(c, g)
        dkpe = jnp.zeros((tr, LANES), F32)
        parts = []
        for h in range(MLA_HEADS):
            dk_h, dv_h = dk_ref[h], dv_ref[h]
            if h % 2 == 0:
                dv_h = pltpu.roll(dv_h, 64, 1)
            parts.append(jnp.where(lane < 64, dk_h, dv_h))
            dkpe = dkpe + jnp.where(rope_lanes, dk_h, 0.0)
        dkv_all = jnp.concatenate(parts, axis=1).astype(BF16)
        dwukv_ref[...] += _dot(kvn.astype(BF16), dkv_all, TN)
        dckv_ref[...] = norm_bwd(c, g, _dot(dkv_all, wukv_ref[...], NT), dgkv_ref).astype(BF16)
        dkr_ref[...] = rope_t(dkpe).astype(BF16)

    row = lambda w: pl.BlockSpec((tr, w), lambda i: (i, 0))
    full = lambda a: pl.BlockSpec(a.shape, lambda i: (0,) * a.ndim)
    head = pl.BlockSpec((MLA_HEADS, tr, LANES), lambda i: (0, i, 0))
    return pl.pallas_call(
        body, name="mla_pre_bwd", grid=(seq // tr,),
        in_specs=[row(Q_LORA), row(KV_LORA), full(gq), full(gkv), full(wuq_e), full(wukv), row(LANES), row(LANES),
                  head, head, head],
        out_specs=[row(Q_LORA), row(KV_LORA), row(LANES), full(wuq_e), full(wukv), full(gq), full(gkv)],
        out_shape=[jax.ShapeDtypeStruct((seq, Q_LORA), BF16), jax.ShapeDtypeStruct((seq, KV_LORA), BF16),
                   jax.ShapeDtypeStruct((seq, LANES), BF16), jax.ShapeDtypeStruct(wuq_e.shape, F32),
                   jax.ShapeDtypeStruct(wukv.shape, F32), jax.ShapeDtypeStruct(gq.shape, F32),
                   jax.ShapeDtypeStruct(gkv.shape, F32)],
        compiler_params=_cp(("arbitrary",), VMEM_LIMIT),
    )(cq, ckv, gq, gkv, wuq_e, wukv, ct, st, dq, dk, dv)


def _in_bwd(dz, dcq, dckv, dgates, dqr, dkr, dvb, dkrope, cd, sd, w_in_p):
    seq = dz.shape[0]
    tr = 512

    def body(dz_ref, dcq_ref, dckv_ref, dg_ref, dqr_ref, dkr_ref, dvb_ref, dkp_ref, cd_ref, sd_ref, w_ref, gx_ref, dh_ref):
        lane = lax.broadcasted_iota(jnp.int32, (tr, LANES), 1)
        rot_lanes = lane % 64 < DIL_ROT
        cd_, sd_ = cd_ref[...], sd_ref[...]

        def rope_t(g):
            return cd_ * g + jnp.where(rot_lanes, _dil_rot(sd_ * g, lane), 0.0)

        dq = [rope_t(dqr_ref[:, LANES * p:LANES * (p + 1)].astype(F32) * DIL_SCALE).astype(BF16) for p in range(4)]
        dk = [rope_t(dkr_ref[:, LANES * p:LANES * (p + 1)].astype(F32)).astype(BF16) for p in range(4)]
        dh = jnp.concatenate([dcq_ref[...], dckv_ref[...], dg_ref[...]] + dq + dk + [dvb_ref[...], dkp_ref[...]], axis=1)
        dh_ref[...] = dh
        gx_ref[...] = ALPHA * dz_ref[...] + _dot(dh, w_ref[...], NT)

    row = lambda w: pl.BlockSpec((tr, w), lambda i: (i, 0))
    return pl.pallas_call(
        body, name="in_bwd", grid=(seq // tr,),
        in_specs=[row(D_MODEL), row(Q_LORA), row(KV_LORA), row(D_MODEL), row(512), row(512), row(512), row(LANES),
                  row(LANES), row(LANES), pl.BlockSpec((D_MODEL, IN_WIDTH_PAD), lambda i: (0, 0))],
        out_specs=[row(D_MODEL), row(IN_WIDTH_PAD)],
        out_shape=[jax.ShapeDtypeStruct((seq, D_MODEL), F32), jax.ShapeDtypeStruct((seq, IN_WIDTH_PAD), BF16)],
        compiler_params=_cp(("arbitrary",), VMEM_LIMIT),
    )(dz, dcq, dckv, dgates, dqr, dkr, dvb, dkrope, cd, sd, w_in_p)


def _dw_in(x, dh):
    seq = dh.shape[0]
    tk = 512
    tn = IN_WIDTH_PAD // 2

    def body(x_ref, dh_ref, o_ref):
        @pl.when(pl.program_id(1) == 0)
        def _():
            o_ref[...] = jnp.zeros_like(o_ref)

        o_ref[...] += _dot(x_ref[...].astype(BF16), dh_ref[...], TN)

    return pl.pallas_call(
        body, name="dw_in", grid=(2, seq // tk),
        in_specs=[pl.BlockSpec((tk, D_MODEL), lambda n, k: (k, 0)), pl.BlockSpec((tk, tn), lambda n, k: (k, n))],
        out_specs=pl.BlockSpec((D_MODEL, tn), lambda n, k: (0, n)),
        out_shape=jax.ShapeDtypeStruct((D_MODEL, IN_WIDTH_PAD), F32),
        compiler_params=_cp(("arbitrary", "arbitrary"), VMEM_LIMIT),
    )(x, dh)


def _adamw(w, g, m, v, name):
    rows, cols = w.shape
    tc = 256 if cols % 256 == 0 and rows * cols > 2 ** 18 else cols

    def body(w_ref, g_ref, m_ref, v_ref, d_ref, nm_ref, nv_ref):
        g_ = g_ref[...]
        nm = ADAM_B1 * m_ref[...] + (1.0 - ADAM_B1) * g_
        nv = ADAM_B2 * v_ref[...] + (1.0 - ADAM_B2) * jnp.square(g_)
        m_hat = nm / (1.0 - ADAM_B1 ** ADAM_STEP)
        v_hat = nv / (1.0 - ADAM_B2 ** ADAM_STEP)
        d_ref[...] = -ADAM_LR * (m_hat / (jnp.sqrt(v_hat) + ADAM_EPS) + ADAM_WD * w_ref[...])
        nm_ref[...] = nm
        nv_ref[...] = nv

    spec = pl.BlockSpec((rows, tc), lambda i: (0, i))
    return pl.pallas_call(
        body, name=name, grid=(cols // tc,), in_specs=[spec] * 4, out_specs=[spec] * 3,
        out_shape=[jax.ShapeDtypeStruct(w.shape, F32)] * 3, compiler_params=_cp(("arbitrary",)),
    )(w, g, m, v)


def _pad_row(v):
    return jnp.pad(v.reshape(1, -1), ((0, 0), (0, D_MODEL - v.shape[-1])))


def _local_step(x2, target, w_in_p, w_uq_f, wukv_f, w_out_f, q_norm_g, kv_norm_g, ln_g, ln_b):
    seq = x2.shape[0]
    wuq_e = jnp.pad(w_uq_f.reshape(Q_LORA, MLA_HEADS, 96), ((0, 0), (0, 0), (0, 32))).reshape(Q_LORA, MLA_HEADS * LANES)
    ct, st, cd, sd = _rope_tables(seq)
    gq = q_norm_g.reshape(1, Q_LORA)
    gkv = kv_norm_g.reshape(1, KV_LORA)

    cq, ckv, gates, qkv, kr = _proj(x2, w_in_p)
    q_e, k_e, v_e = _mla_pre(cq, ckv, kr, gq, gkv, wuq_e, wukv_f, ct, st)
    o_a, lse_a = _mla_fwd(q_e, k_e, v_e)
    qr, krot = _dil_pre(qkv, cd, sd)
    o_b, lse_b = _dil_fwd(qr, krot, qkv)

    dz, d_o, d_gates, dw_out, dln_g, dln_b, loss_part = _post(
        x2, o_a, o_b, gates, w_out_f, ln_g.reshape(1, D_MODEL), ln_b.reshape(1, D_MODEL), target)
    dq_e, dk_e, dv_e = _mla_bwd(q_e, k_e, v_e, d_o, o_a, lse_a)
    dqr, dkr, dvb = _dil_bwd(qr, krot, qkv, d_o, o_b, lse_b)
    dcq, dckv, dkrope, dwuq_e, dwukv, dgq, dgkv = _mla_pre_bwd(cq, ckv, gq, gkv, wuq_e, wukv_f, ct, st, dq_e, dk_e, dv_e)
    grad_x, dh = _in_bwd(dz, dcq, dckv, d_gates, dqr, dkr, dvb, dkrope, cd, sd, w_in_p)
    dw_in = _unpermute_dw_in(_dw_in(x2, dh))
    dw_uq = dwuq_e.reshape(Q_LORA, MLA_HEADS, LANES)[:, :, :96].reshape(Q_LORA, MLA_HEADS * 96)
    return loss_part, grad_x, dw_in, dw_uq, dwukv, dw_out, dgq, dgkv, dln_g, dln_b


def kernel(x, w_in, q_norm_g, kv_norm_g, w_uq, w_ukv, w_out, ln_g, ln_b, loss_target, m_w_in, m_q_norm_g, m_kv_norm_g, m_w_uq, m_w_ukv, m_w_out, m_ln_g, m_ln_b, v_w_in, v_q_norm_g, v_kv_norm_g, v_w_uq, v_w_ukv, v_w_out, v_ln_g, v_ln_b):
    seq = x.shape[1]
    x2 = x.reshape(seq, D_MODEL)
    target = loss_target.reshape(seq, D_MODEL)

    g_w_in, g_w_uq, g_w_ukv, g_w_out = _all_gather_weights([w_in, w_uq, w_ukv, w_out])
    by_cols = lambda g: jnp.concatenate([g[j] for j in range(N_SHARD)], axis=1)
    loss_part, grad_x, dw_in, dw_uq, dwukv, dw_out, dgq, dgkv, dln_g, dln_b = _local_step(
        x2, target, _permute_w_in_shards(g_w_in), by_cols(g_w_uq), by_cols(g_w_ukv), g_w_out.reshape(D_MODEL, D_MODEL),
        q_norm_g, kv_norm_g, ln_g, ln_b)

    to_shards = lambda d: d.reshape(d.shape[0], N_SHARD, d.shape[1] // N_SHARD).transpose(1, 0, 2)
    grads = [to_shards(dw_in), to_shards(dw_uq), to_shards(dwukv), dw_out.reshape(N_SHARD, 256, D_MODEL)]
    small = jnp.concatenate([_pad_row(dgq), _pad_row(dgkv), dln_g, dln_b, _pad_row(loss_part),
                             jnp.zeros((3, D_MODEL), F32)], axis=0)
    *chip_sums, smalls = _reduce_over_sibling(grads, small)
    g_in, g_uq, g_ukv, g_out = _reduce_over_chips(chip_sums)
    small_sum = _sum_smalls(smalls)
    loss = small_sum[4, 0]

    big = [[o.T for o in _adamw(w.T, g.T, m.T, v.T, name)] for w, g, m, v, name in (
        (w_in, g_in, m_w_in, v_w_in, "adamw_w_in"), (w_uq, g_uq, m_w_uq, v_w_uq, "adamw_w_uq"))]
    big += [_adamw(w, g, m, v, name) for w, g, m, v, name in (
        (w_ukv, g_ukv, m_w_ukv, v_w_ukv, "adamw_w_ukv"), (w_out, g_out, m_w_out, v_w_out, "adamw_w_out"))]
    vec = lambda a, b, c_, d: jnp.concatenate([_pad_row(a), _pad_row(b), _pad_row(c_), _pad_row(d),
                                               jnp.zeros((4, D_MODEL), F32)], axis=0)
    sw = vec(q_norm_g, kv_norm_g, ln_g, ln_b)
    sm = vec(m_q_norm_g, m_kv_norm_g, m_ln_g, m_ln_b)
    sv = vec(v_q_norm_g, v_kv_norm_g, v_ln_g, v_ln_b)
    sg = jnp.concatenate([small_sum[:4], jnp.zeros((4, D_MODEL), F32)], axis=0)
    s_delta, s_m, s_v = _adamw(sw, sg, sm, sv, "adamw_vectors")

    def vectors(a):
        return [a[0, :Q_LORA], a[1, :KV_LORA], a[2], a[3]]

    def ordered(bigs, smalls_):
        return [bigs[0], smalls_[0], smalls_[1], bigs[1], bigs[2], bigs[3], smalls_[2], smalls_[3]]

    grads_out = ordered([g_in, g_uq, g_ukv, g_out], vectors(small_sum))
    deltas = ordered([b[0] for b in big], vectors(s_delta))
    new_m = ordered([b[1] for b in big], vectors(s_m))
    new_v = ordered([b[2] for b in big], vectors(s_v))
    return (loss, grad_x.reshape(x.shape), *grads_out, *deltas, *new_m, *new_v)
```

```python
import functools

import jax
import jax.numpy as jnp
import numpy as np
from jax import lax
from jax.experimental import pallas as pl
from jax.experimental.pallas import tpu as pltpu

F32 = jnp.float32
BF16 = jnp.bfloat16

D_MODEL = 1024
ROPE_THETA = 500000.0
BLOCK = 128
NEG = -1e30
RMS_EPS = 1e-6
LN_EPS = 1e-5

MLA_HEADS = 8
MLA_NOPE = 64
MLA_ROPE = 32
Q_LORA = 384
KV_LORA = 256
DIL_HEADS = 8
DIL_HEAD_DIM = 64
DIL_ROT = 16
DIL_DILATIONS = (1, 4, 16)
IN_WIDTH = 3232
IN_WIDTH_PAD = 3328
ONES_LANE = (64, 0)
MLA_SCALE = (MLA_NOPE + MLA_ROPE) ** -0.5
DIL_SCALE = DIL_HEAD_DIM ** -0.5
ALPHA = 2.0 ** 0.25

ADAM_LR = 0.001
ADAM_B1 = 0.9
ADAM_B2 = 0.999
ADAM_EPS = 1e-08
ADAM_WD = 0.01
ADAM_STEP = 10

N_SHARD = 4
SHARD_SHAPES = ((1024, 808), (384, 192), (256, 256), (256, 1024))
GRAD_SHAPES = ((808, 1024), (384, 192), (256, 256), (256, 1024))
GRAD_SPLIT_COLS = (True, False, False, False)
ROW_CHUNK = 64
LANES = 128
VMEM_LIMIT = 56 * 1024 * 1024
MESH = pl.DeviceIdType.MESH

NT = (((1,), (1,)), ((), ()))
TN = (((0,), (0,)), ((), ()))


def _cp(sem=None, vmem=None):
    return pltpu.CompilerParams(dimension_semantics=sem, vmem_limit_bytes=vmem)


def _dot(a, b, dims=None):
    if dims is None:
        return jnp.dot(a, b, preferred_element_type=F32)
    return lax.dot_general(a, b, dims, preferred_element_type=F32)


def _rope_tables(seq):
    f32 = np.float32
    pos = np.arange(seq, dtype=f32)[:, None]
    one, zero = np.ones((seq, 64), f32), np.zeros((seq, 64), f32)

    def cos_sin(dim):
        inv = np.power(f32(ROPE_THETA), -np.arange(0, dim, 2, dtype=f32) / f32(dim)).astype(f32)
        ang = (pos * inv[None, :]).astype(f32)
        return np.cos(ang).astype(f32), np.sin(ang).astype(f32)

    cos, sin = cos_sin(MLA_ROPE)
    ct = np.concatenate([one, cos, cos, zero[:, :32]], axis=1)
    st = np.concatenate([zero, -sin, sin, zero[:, :32]], axis=1)
    cos, sin = cos_sin(DIL_ROT)
    cd = np.concatenate([cos, cos, one[:, :48]], axis=1)
    sd = np.concatenate([-sin, sin, zero[:, :48]], axis=1)
    return tuple(jnp.asarray(t) for t in (ct, st, np.tile(cd, (1, 2)), np.tile(sd, (1, 2))))


W_IN_ORDER = ((0, 640), (672, 1184), (2720, 3232), (1184, 2720), None, (640, 672))


def _permute_w_in(w):
    z = jnp.zeros((w.shape[0], 64), w.dtype)
    parts = [z if r is None else w[:, r[0]:r[1]] for r in W_IN_ORDER]
    return jnp.concatenate(parts + [z[:, :32]], axis=1)


def _permute_w_in_shards(g):
    width = g.shape[2]
    z = jnp.zeros((g.shape[1], 64), g.dtype)
    parts = []
    for r in W_IN_ORDER:
        if r is None:
            parts.append(z)
            continue
        for j in range(N_SHARD):
            lo, hi = max(r[0], width * j), min(r[1], width * (j + 1))
            if lo < hi:
                parts.append(g[j, :, lo - width * j:hi - width * j])
    return jnp.concatenate(parts + [z[:, :32]], axis=1)


def _unpermute_dw_in_t(dw_t):
    return jnp.concatenate([dw_t[0:640], dw_t[3264:3296], dw_t[640:1152], dw_t[1664:3200], dw_t[1152:1664]], axis=0)


def _position():
    return lax.axis_index("x"), lax.axis_index("y"), lax.axis_index("c")


def _halves(c, rows):
    hr = rows // 2
    return pl.ds(pl.multiple_of(c * hr, 8), hr), pl.ds(pl.multiple_of((1 - c) * hr, 8), hr)


def _for_row_chunks(rows, fn):
    def step(i, carry):
        fn(pl.multiple_of(i * ROW_CHUNK, ROW_CHUNK))
        return carry

    lax.fori_loop(0, rows // ROW_CHUNK, step, 0)


def _all_gather_weights(shards):
    n = len(shards)

    def body(*refs):
        ins, outs = refs[:n], refs[n:2 * n]
        send_sems, recv_sems = refs[2 * n:]
        x, y, c = _position()
        me = 2 * x + y
        chips = [(1 - x, y), (x, 1 - y), (1 - x, 1 - y)]
        for a in range(n):
            def cast(r, a=a):
                outs[a][me, pl.ds(r, ROW_CHUNK), :] = ins[a][pl.ds(r, ROW_CHUNK), :].astype(BF16)

            _for_row_chunks(SHARD_SHAPES[a][0], cast)

        def copy(k, a, slot, rows, to):
            ref = outs[a].at[slot, rows]
            return pltpu.make_async_remote_copy(
                src_ref=ref, dst_ref=ref, send_sem=send_sems.at[k * n + a], recv_sem=recv_sems.at[k * n + a],
                device_id=to, device_id_type=MESH)

        half = [_halves(c, SHARD_SHAPES[a][0])[0] for a in range(n)]
        other = [_halves(c, SHARD_SHAPES[a][0])[1] for a in range(n)]
        first = [copy(k, a, me, half[a], (px, py, c)) for k, (px, py) in enumerate(chips) for a in range(n)]
        for cp in first:
            cp.start()
        passed = []
        for k, (px, py) in enumerate(chips):
            for a in range(n):
                copy(k, a, 2 * px + py, half[a], (x, y, c)).wait_recv()
                cp = copy(3 + k, a, 2 * px + py, half[a], (x, y, 1 - c))
                cp.start()
                passed.append(cp)
        for k, (px, py) in enumerate(chips):
            for a in range(n):
                copy(3 + k, a, 2 * px + py, other[a], (x, y, c)).wait_recv()
        for cp in first + passed:
            cp.wait_send()

    vmem = pl.BlockSpec(memory_space=pltpu.VMEM)
    return pl.pallas_call(
        body, name="all_gather_weights",
        out_shape=[jax.ShapeDtypeStruct((N_SHARD,) + s, BF16) for s in SHARD_SHAPES],
        in_specs=[vmem] * n, out_specs=[vmem] * n,
        scratch_shapes=[pltpu.SemaphoreType.DMA((6 * n,)), pltpu.SemaphoreType.DMA((6 * n,))],
        compiler_params=_cp(None, VMEM_LIMIT),
    )(*shards)


def _grad_half_shape(a):
    rows, cols = GRAD_SHAPES[a]
    return (rows, cols // 2) if GRAD_SPLIT_COLS[a] else (rows // 2, cols)


def _grad_half(a, c):
    rows, cols = GRAD_SHAPES[a]
    if GRAD_SPLIT_COLS[a]:
        return slice(None), pl.ds(pl.multiple_of(c * (cols // 2), LANES), cols // 2)
    return pl.ds(pl.multiple_of(c * (rows // 2), ROW_CHUNK), rows // 2), slice(None)


def _grad_chunks(a, c):
    rows, cols = GRAD_SHAPES[a]
    if GRAD_SPLIT_COLS[a]:
        return [((slice(None), pl.ds(c0, LANES)),
                 (slice(None), pl.ds(pl.multiple_of(c * (cols // 2) + c0, LANES), LANES)))
                for c0 in range(0, cols // 2, LANES)]
    return [((pl.ds(r0, ROW_CHUNK), slice(None)),
             (pl.ds(pl.multiple_of(c * (rows // 2) + r0, ROW_CHUNK), ROW_CHUNK), slice(None)))
            for r0 in range(0, rows // 2, ROW_CHUNK)]


def _reduce_over_sibling(grads, small):
    n = len(grads)

    def body(*refs):
        g_hbm, sm = refs[:n], refs[n]
        sums, smalls = refs[n + 1:2 * n + 1], refs[2 * n + 1]
        stage, got = refs[2 * n + 2:3 * n + 2], refs[3 * n + 2:4 * n + 2]
        send_sems, recv_sems, local_sems = refs[4 * n + 2:]
        x, y, c = _position()
        me = 4 * x + 2 * y + c
        loads = [pltpu.make_async_copy(g_hbm[a], stage[a], local_sems.at[a]) for a in range(n)]
        for ld in loads:
            ld.start()
        smalls[me] = sm[...]
        sends = []
        for rel in range(1, 8):
            px = 1 - x if rel // 4 else x
            py = 1 - y if (rel // 2) % 2 else y
            pc = 1 - c if rel % 2 else c
            cp = pltpu.make_async_remote_copy(
                src_ref=sm, dst_ref=smalls.at[me], send_sem=send_sems.at[n + rel], recv_sem=recv_sems.at[n + rel],
                device_id=(px, py, pc), device_id_type=MESH)
            cp.start()
            sends.append((cp, 4 * px + 2 * py + pc))
        swaps = []
        for a in range(n):
            loads[a].wait()
            cp = pltpu.make_async_remote_copy(
                src_ref=stage[a].at[(slice(None),) + _grad_half(a, 1 - c)], dst_ref=got[a], send_sem=send_sems.at[a], recv_sem=recv_sems.at[a],
                device_id=(x, y, 1 - c), device_id_type=MESH)
            cp.start()
            swaps.append(cp)
        for a in range(n):
            swaps[a].wait_recv()
            for k in range(N_SHARD):
                for in_half, in_whole in _grad_chunks(a, c):
                    pair = stage[a][(k,) + in_whole] + got[a][(k,) + in_half]
                    sums[a][(k,) + in_half] = pair.astype(BF16)
        for rel, (cp, peer) in enumerate(sends, start=1):
            pltpu.make_async_remote_copy(
                src_ref=sm, dst_ref=smalls.at[peer], send_sem=send_sems.at[n + rel], recv_sem=recv_sems.at[n + rel],
                device_id=(x, y, c), device_id_type=MESH).wait_recv()
        for cp in swaps:
            cp.wait_send()
        for cp, _ in sends:
            cp.wait_send()

    vmem = pl.BlockSpec(memory_space=pltpu.VMEM)
    half = [(N_SHARD,) + _grad_half_shape(a) for a in range(n)]
    return pl.pallas_call(
        body, name="reduce_over_sibling",
        out_shape=[jax.ShapeDtypeStruct(s, BF16) for s in half] + [jax.ShapeDtypeStruct((8,) + small.shape, F32)],
        in_specs=[pl.BlockSpec(memory_space=pl.ANY)] * n + [vmem], out_specs=[vmem] * (n + 1),
        scratch_shapes=[pltpu.VMEM((N_SHARD,) + s, F32) for s in GRAD_SHAPES] + [pltpu.VMEM(s, F32) for s in half]
        + [pltpu.SemaphoreType.DMA((n + 8,)), pltpu.SemaphoreType.DMA((n + 8,)), pltpu.SemaphoreType.DMA((n,))],
        compiler_params=_cp(None, VMEM_LIMIT),
    )(*grads, small)


def _reduce_over_chips(sums):
    n = len(sums)

    def body(*refs):
        h, outs, got = refs[:n], refs[n:2 * n], refs[2 * n:3 * n]
        send_sems, recv_sems = refs[3 * n:]
        x, y, c = _position()
        me = 2 * x + y
        chips = [(1 - x, y), (x, 1 - y), (1 - x, 1 - y)]
        sends = []
        for k, (px, py) in enumerate(chips):
            for a in range(n):
                cp = pltpu.make_async_remote_copy(
                    src_ref=h[a].at[2 * px + py], dst_ref=got[a].at[k], send_sem=send_sems.at[k * n + a],
                    recv_sem=recv_sems.at[k * n + a], device_id=(px, py, c), device_id_type=MESH)
                cp.start()
                sends.append(cp)
        for cp in sends:
            cp.wait_recv()
        joins = []
        for a in range(n):
            for in_half, in_whole in _grad_chunks(a, c):
                total = h[a][(me,) + in_half].astype(F32)
                for k in range(3):
                    total = total + got[a][(k,) + in_half].astype(F32)
                outs[a][in_whole] = total
            half = outs[a].at[_grad_half(a, c)]
            cp = pltpu.make_async_remote_copy(
                src_ref=half, dst_ref=half, send_sem=send_sems.at[3 * n + a],
                recv_sem=recv_sems.at[3 * n + a], device_id=(x, y, 1 - c), device_id_type=MESH)
            cp.start()
            joins.append(cp)
        for a in range(n):
            other = outs[a].at[_grad_half(a, 1 - c)]
            pltpu.make_async_remote_copy(
                src_ref=other, dst_ref=other, send_sem=send_sems.at[3 * n + a],
                recv_sem=recv_sems.at[3 * n + a], device_id=(x, y, c), device_id_type=MESH).wait_recv()
        for cp in sends + joins:
            cp.wait_send()

    vmem = pl.BlockSpec(memory_space=pltpu.VMEM)
    return pl.pallas_call(
        body, name="reduce_over_chips",
        out_shape=[jax.ShapeDtypeStruct(s, F32) for s in GRAD_SHAPES],
        in_specs=[vmem] * n, out_specs=[vmem] * n,
        scratch_shapes=[pltpu.VMEM((3,) + _grad_half_shape(a), BF16) for a in range(n)]
        + [pltpu.SemaphoreType.DMA((4 * n,)), pltpu.SemaphoreType.DMA((4 * n,))],
        compiler_params=_cp(None, VMEM_LIMIT),
    )(*sums)


def _sum_smalls(smalls):
    def body(s, o):
        acc = s[0]
        for d in range(1, 8):
            acc = acc + s[d]
        o[...] = acc

    return pl.pallas_call(body, name="sum_smalls", out_shape=jax.ShapeDtypeStruct(smalls.shape[1:], F32))(smalls)


def _proj(x, w_in_p):
    seq = x.shape[0]
    tr = 512
    splits = ((0, 384), (384, 640), (640, 1664), (1664, 3200), (3200, 3328))

    def body(x_ref, w_ref, *outs):
        xb = x_ref[...].astype(BF16)
        for (lo, hi), o in zip(splits, outs):
            o[...] = _dot(xb, w_ref[:, lo:hi])

    return pl.pallas_call(
        body, name="proj", grid=(seq // tr,),
        in_specs=[pl.BlockSpec((tr, D_MODEL), lambda i: (i, 0)), pl.BlockSpec((D_MODEL, IN_WIDTH_PAD), lambda i: (0, 0))],
        out_specs=[pl.BlockSpec((tr, hi - lo), lambda i: (i, 0)) for lo, hi in splits],
        out_shape=[jax.ShapeDtypeStruct((seq, hi - lo), F32) for lo, hi in splits],
        compiler_params=_cp(("arbitrary",), VMEM_LIMIT),
    )(x, w_in_p)


def _mla_rot(t, lane):
    return jnp.where(lane < 80, pltpu.roll(t, 112, 1), pltpu.roll(t, 16, 1))


def _dil_rot(t, lane):
    return jnp.where(lane % 64 < 8, pltpu.roll(t, 120, 1), pltpu.roll(t, 8, 1))


def _rms(c, g):
    r = lax.rsqrt(jnp.mean(c * c, axis=-1, keepdims=True) + RMS_EPS)
    return r, c * r * g


def _mla_pre(cq, ckv, kr, gq, gkv, wuq_e, wukv, ct, st):
    seq = cq.shape[0]
    tr = 512

    def body(cq_ref, ckv_ref, kr_ref, gq_ref, gkv_ref, wuq_ref, wukv_ref, ct_ref, st_ref, q_out, k_out, v_out):
        lane = lax.broadcasted_iota(jnp.int32, (tr, LANES), 1)
        ct_, st_ = ct_ref[...], st_ref[...]

        def rope(t):
            return t * ct_ + _mla_rot(t, lane) * st_

        _, qn = _rms(cq_ref[...], gq_ref[...])
        q_all = _dot(qn.astype(BF16), wuq_ref[...])
        for h in range(MLA_HEADS):
            q_out[h] = (rope(q_all[:, LANES * h:LANES * (h + 1)]) * MLA_SCALE).astype(BF16)
        _, kvn = _rms(ckv_ref[...], gkv_ref[...])
        kv_all = _dot(kvn.astype(BF16), wukv_ref[...])
        kpe = rope(kr_ref[...])
        for h in range(MLA_HEADS):
            kv_h = kv_all[:, LANES * h:LANES * (h + 1)]
            k_out[h] = jnp.where(lane < 64, kv_h, kpe).astype(BF16)
            if h % 2:
                v = jnp.where(lane >= 64, kv_h, 0.0)
            else:
                v = jnp.where(lane < 64, pltpu.roll(kv_h, 64, 1), 0.0)
            v_out[h] = jnp.where(lane == ONES_LANE[h % 2], 1.0, v).astype(BF16)

    row = lambda w: pl.BlockSpec((tr, w), lambda i: (i, 0))
    full = lambda a: pl.BlockSpec(a.shape, lambda i: (0,) * a.ndim)
    head = pl.BlockSpec((MLA_HEADS, tr, LANES), lambda i: (0, i, 0))
    return pl.pallas_call(
        body, name="mla_pre", grid=(seq // tr,),
        in_specs=[row(Q_LORA), row(KV_LORA), row(LANES), full(gq), full(gkv), full(wuq_e), full(wukv), row(LANES), row(LANES)],
        out_specs=[head] * 3,
        out_shape=[jax.ShapeDtypeStruct((MLA_HEADS, seq, LANES), BF16)] * 3,
        compiler_params=_cp(("arbitrary",), VMEM_LIMIT),
    )(cq, ckv, kr, gq, gkv, wuq_e, wukv, ct, st)


def _mla_fwd(q, k, v):
    seq = q.shape[1]
    tq = 512
    nq = seq // tq

    def body(q_ref, k_ref, v_ref, o_ref, lse_ref, m_s, acc_s, s_buf):
        i = pl.program_id(1)
        row = lax.broadcasted_iota(jnp.int32, (tq, tq), 0)
        col = lax.broadcasted_iota(jnp.int32, (tq, tq), 1)
        lane = lax.broadcasted_iota(jnp.int32, (tq, LANES), 1)
        m_s[...] = jnp.full((2, tq, LANES), NEG, F32)
        acc_s[...] = jnp.zeros((2, tq, LANES), F32)

        def block(j):
            return pl.ds(pl.multiple_of(j * tq, tq), tq)

        def scores(hh, j):
            return _dot(q_ref[hh], k_ref[hh, block(j), :], NT)

        def consume(hh, j, s):
            m_prev = m_s[hh]
            m_new = jnp.maximum(m_prev, jnp.max(s, axis=1, keepdims=True))
            p = jnp.exp(s - m_new[:, :1])
            acc_s[hh] = jnp.exp(m_prev - m_new) * acc_s[hh] + _dot(p.astype(BF16), v_ref[hh, block(j), :])
            m_s[hh] = m_new

        for hh in range(2):
            s_buf[0, hh] = scores(hh, 0)

        def full_step(j, carry):
            slot = j & 1
            for hh in range(2):
                s = s_buf[slot, hh]
                s_buf[1 - slot, hh] = scores(hh, j + 1)
                consume(hh, j, s)
            return carry

        lax.fori_loop(0, i, full_step, 0)
        total = jnp.zeros((tq, LANES), F32)
        for hh in range(2):
            consume(hh, i, jnp.where(col <= row, s_buf[i & 1, hh], NEG))
            acc = acc_s[hh]
            l = acc[:, ONES_LANE[hh]:ONES_LANE[hh] + 1]
            mine = (lane >= 64) if hh else (lane < 64)
            total = total + jnp.where(mine, acc / l, 0.0)
            lse_ref[hh] = m_s[hh] + jnp.log(l)
        o_ref[...] = total

    kv_spec = pl.BlockSpec((2, seq, LANES), lambda p, i: (p, 0, 0))
    return pl.pallas_call(
        body, name="mla_fwd", grid=(MLA_HEADS // 2, nq),
        in_specs=[pl.BlockSpec((2, tq, LANES), lambda p, i: (p, i, 0)), kv_spec, kv_spec],
        out_specs=[pl.BlockSpec((tq, LANES), lambda p, i: (i, p)), pl.BlockSpec((2, tq, LANES), lambda p, i: (p, i, 0))],
        out_shape=[jax.ShapeDtypeStruct((seq, 4 * LANES), F32), jax.ShapeDtypeStruct((MLA_HEADS, seq, LANES), F32)],
        scratch_shapes=[pltpu.VMEM((2, tq, LANES), F32), pltpu.VMEM((2, tq, LANES), F32),
                        pltpu.VMEM((2, 2, tq, tq), F32)],
        compiler_params=_cp(("arbitrary", "arbitrary"), VMEM_LIMIT),
    )(q, k, v)


def _dil_pre(qkv, cd, sd):
    seq = qkv.shape[0]
    tr = 512

    def body(q_ref, k_ref, cd_ref, sd_ref, qr_ref, kr_ref):
        lane = lax.broadcasted_iota(jnp.int32, (tr, LANES), 1)
        cd_, sd_ = cd_ref[...], sd_ref[...]
        for p in range(4):
            cols = slice(LANES * p, LANES * (p + 1))
            t = q_ref[:, cols]
            qr_ref[:, cols] = (t * cd_ + _dil_rot(t, lane) * sd_) * DIL_SCALE
            t = k_ref[:, cols]
            kr_ref[:, cols] = t * cd_ + _dil_rot(t, lane) * sd_

    blk = lambda j: pl.BlockSpec((tr, 4 * LANES), lambda i: (i, j))
    tab = pl.BlockSpec((tr, LANES), lambda i: (i, 0))
    return pl.pallas_call(
        body, name="dil_pre", grid=(seq // tr,),
        in_specs=[blk(0), blk(1), tab, tab], out_specs=[blk(0), blk(0)],
        out_shape=[jax.ShapeDtypeStruct((seq, 4 * LANES), F32)] * 2,
        compiler_params=_cp(("arbitrary",)),
    )(qkv, qkv, cd, sd)


DIL_Q_FWD = 2 * BLOCK
DIL_Q_BWD = BLOCK


def _dil_tile_index(t, d, seq, nq):
    per_class = seq // (nq * d)
    shift = per_class.bit_length() - 1
    r = t >> shift
    n = t & (per_class - 1)
    start = r + (nq * d) * n
    prev = jnp.maximum(start - BLOCK * d, r)
    if d == 1:
        start = pl.multiple_of(start, nq)
        prev = pl.multiple_of(prev, BLOCK)
    return (n == 0).astype(jnp.int32), start, prev


def _dil_rows(start, d, size):
    return pl.ds(start, size) if d == 1 else pl.ds(start, size, stride=d)


def _dil_bias(nq):
    i = lax.broadcasted_iota(jnp.int32, (2 * nq, BLOCK + nq), 0) % nq
    j = lax.broadcasted_iota(jnp.int32, (2 * nq, BLOCK + nq), 1)
    band = (j >= i) & (j <= i + BLOCK)
    return jnp.where(band, 0.0, NEG), jnp.where(band & (j >= BLOCK), 0.0, NEG)


def _stack_heads(t, lane):
    return jnp.concatenate([jnp.where(lane < 64, t, 0.0), jnp.where(lane >= 64, t, 0.0)], axis=0)


def _unstack_heads(t, lane):
    nq = t.shape[0] // 2
    return jnp.where(lane < 64, t[:nq], t[nq:])


def _dil_fwd(qr, kr, qkv):
    seq = qr.shape[0]
    nq = DIL_Q_FWD
    n_tiles = seq // nq
    assert seq % (nq * max(DIL_DILATIONS)) == 0

    def body(q_ref, k_ref, v_ref, o_ref, lse_ref, m_s, l_s, n_s, bias_s):
        lane = lax.broadcasted_iota(jnp.int32, (nq, LANES), 1)
        bias_s[0], bias_s[1] = _dil_bias(nq)
        for bi, d in enumerate(DIL_DILATIONS):

            def tile(t, carry, d=d, bi=bi):
                first, start, prev = _dil_tile_index(t, d, seq, nq)
                rows, prows = _dil_rows(start, d, nq), _dil_rows(prev, d, BLOCK)
                qst = _stack_heads(q_ref[rows, :], lane).astype(BF16)
                kcat = jnp.concatenate([k_ref[prows, :], k_ref[rows, :]], axis=0).astype(BF16)
                vcat = jnp.concatenate([v_ref[prows, :], v_ref[rows, :]], axis=0).astype(BF16)
                s = _dot(qst, kcat, NT) + bias_s[first]
                m = jnp.max(s, axis=1, keepdims=True)
                p = jnp.exp(s - m)
                l2 = _unstack_heads(jnp.sum(p, axis=1, keepdims=True) + jnp.zeros((2 * nq, LANES), F32), lane)
                m2 = _unstack_heads(m + jnp.zeros((2 * nq, LANES), F32), lane)
                num2 = _unstack_heads(_dot(p.astype(BF16), vcat), lane)
                if bi == 0:
                    m_s[rows, :] = m2
                    l_s[rows, :] = l2
                    n_s[rows, :] = num2
                else:
                    m_old = m_s[rows, :]
                    m_new = jnp.maximum(m_old, m2)
                    a = jnp.exp(m_old - m_new)
                    b = jnp.exp(m2 - m_new)
                    m_s[rows, :] = m_new
                    l_s[rows, :] = a * l_s[rows, :] + b * l2
                    n_s[rows, :] = a * n_s[rows, :] + b * num2
                return carry

            lax.fori_loop(0, n_tiles, tile, 0, unroll=2)
        o_ref[...] = n_s[...] / l_s[...]
        lse_ref[...] = m_s[...] + jnp.log(l_s[...])

    col = lambda off: pl.BlockSpec((seq, LANES), lambda p: (0, p + off))
    return pl.pallas_call(
        body, name="dil_fwd", grid=(4,),
        in_specs=[col(0), col(0), col(8)],
        out_specs=[col(0), pl.BlockSpec((None, seq, LANES), lambda p: (p, 0, 0))],
        out_shape=[jax.ShapeDtypeStruct((seq, 4 * LANES), F32), jax.ShapeDtypeStruct((4, seq, LANES), F32)],
        scratch_shapes=[pltpu.VMEM((seq, LANES), F32)] * 3 + [pltpu.VMEM((2, 2 * nq, BLOCK + nq), F32)],
        compiler_params=_cp(("arbitrary",), VMEM_LIMIT),
    )(qr, kr, qkv)


def _post(x, o_a, o_b, gates, w_out, ln_g, ln_b, target):
    seq = x.shape[0]
    tr = 512

    def body(x_ref, oa_ref, ob_ref, g_ref, w_ref, lg_ref, lb_ref, t_ref,
             dz_ref, do_ref, dg_ref, dw_ref, dlg_ref, dlb_ref, loss_ref):
        @pl.when(pl.program_id(0) == 0)
        def _():
            dw_ref[...] = jnp.zeros_like(dw_ref)
            dlg_ref[...] = jnp.zeros_like(dlg_ref)
            dlb_ref[...] = jnp.zeros_like(dlb_ref)
            loss_ref[...] = jnp.zeros_like(loss_ref)

        g = g_ref[...]
        sg = 1.0 / (1.0 + jnp.exp(-g))
        silu = g * sg
        o = jnp.concatenate([oa_ref[...], ob_ref[...]], axis=1)
        mixb = (o * silu).astype(BF16)
        w = w_ref[...]
        z = ALPHA * x_ref[...] + _dot(mixb, w)
        mu = jnp.mean(z, axis=-1, keepdims=True)
        zc = z - mu
        rstd = lax.rsqrt(jnp.mean(zc * zc, axis=-1, keepdims=True) + LN_EPS)
        xhat = zc * rstd
        lg = lg_ref[...]
        err = xhat * lg + lb_ref[...] - t_ref[...]
        loss_ref[...] += jnp.sum(err * err) * (0.5 / D_MODEL)
        dy = err * (1.0 / D_MODEL)
        dlg_ref[...] += jnp.sum(dy * xhat, axis=0, keepdims=True)
        dlb_ref[...] += jnp.sum(dy, axis=0, keepdims=True)
        dxh = dy * lg
        dz = rstd * (dxh - jnp.mean(dxh, axis=-1, keepdims=True) - xhat * jnp.mean(dxh * xhat, axis=-1, keepdims=True))
        dz_ref[...] = dz
        dzb = dz.astype(BF16)
        dmix = _dot(dzb, w, NT)
        do_ref[...] = dmix * silu
        dg_ref[...] = (dmix * o * (sg * (1.0 + g * (1.0 - sg)))).astype(BF16)
        dw_ref[...] += _dot(mixb, dzb, TN)

    row = lambda w: pl.BlockSpec((tr, w), lambda i: (i, 0))
    full = lambda s: pl.BlockSpec(s, lambda i: (0, 0))
    return pl.pallas_call(
        body, name="post", grid=(seq // tr,),
        in_specs=[row(D_MODEL), row(512), row(512), row(D_MODEL), full((D_MODEL, D_MODEL)), full((1, D_MODEL)),
                  full((1, D_MODEL)), row(D_MODEL)],
        out_specs=[row(D_MODEL), row(D_MODEL), row(D_MODEL), full((D_MODEL, D_MODEL)), full((1, D_MODEL)),
                   full((1, D_MODEL)), full((1, LANES))],
        out_shape=[jax.ShapeDtypeStruct((seq, D_MODEL), F32), jax.ShapeDtypeStruct((seq, D_MODEL), F32),
                   jax.ShapeDtypeStruct((seq, D_MODEL), BF16), jax.ShapeDtypeStruct((D_MODEL, D_MODEL), F32),
                   jax.ShapeDtypeStruct((1, D_MODEL), F32), jax.ShapeDtypeStruct((1, D_MODEL), F32),
                   jax.ShapeDtypeStruct((1, LANES), F32)],
        compiler_params=_cp(("arbitrary",), VMEM_LIMIT),
    )(x, o_a, o_b, gates, w_out, ln_g, ln_b, target)


def _mla_bwd(q, k, v, d_o, o, lse):
    seq = q.shape[1]
    tq = 512
    nq = seq // tq

    def body(q_ref, k_ref, v_ref, do_ref, o_ref, lse_ref, dq_ref, dk_ref, dv_ref, d_s, dk_s, dv_s, v_s):
        j = pl.program_id(1)
        lane = lax.broadcasted_iota(jnp.int32, (tq, LANES), 1)
        row = lax.broadcasted_iota(jnp.int32, (tq, tq), 0)
        col = lax.broadcasted_iota(jnp.int32, (tq, tq), 1)

        @pl.when(j == 0)
        def _():
            dq_ref[...] = jnp.zeros_like(dq_ref)

            def rowsum(i, carry):
                rows = pl.ds(pl.multiple_of(i * tq, tq), tq)
                prod = do_ref[rows, :] * o_ref[rows, :]
                for hh in range(2):
                    mine = (lane >= 64) if hh else (lane < 64)
                    total = jnp.sum(jnp.where(mine, prod, 0.0), axis=1, keepdims=True)
                    d_s[hh, rows, :] = total + jnp.zeros((tq, LANES), F32)
                return carry

            lax.fori_loop(0, nq, rowsum, 0)

        dk_s[...] = jnp.zeros_like(dk_s)
        dv_s[...] = jnp.zeros_like(dv_s)
        for hh in range(2):
            v_s[hh] = jnp.where(lane == ONES_LANE[hh], 0.0, v_ref[hh].astype(F32)).astype(BF16)

        def step(i, masked):
            rows = pl.ds(pl.multiple_of(i * tq, tq), tq)
            dob = do_ref[rows, :].astype(BF16)
            for hh in range(2):
                qb, kb, vb = q_ref[hh, rows, :], k_ref[hh], v_s[hh]
                s = _dot(qb, kb, NT)
                p = jnp.exp(s - lse_ref[hh, rows, :][:, :1])
                if masked:
                    p = jnp.where(col <= row, p, 0.0)
                dv_s[hh] += _dot(p.astype(BF16), dob, TN)
                dp = _dot(dob, vb, NT)
                ds = (p * (dp - d_s[hh, rows, :][:, :1])).astype(BF16)
                dk_s[hh] += _dot(ds, qb, TN)
                dq_ref[hh, rows, :] += _dot(ds, kb)

        def full_step(i, carry):
            step(i, False)
            return carry

        step(j, True)
        lax.fori_loop(j + 1, nq, full_step, 0)
        dk_ref[...] = dk_s[...]
        dv_ref[...] = dv_s[...]

    whole = pl.BlockSpec((2, seq, LANES), lambda p, j: (p, 0, 0))
    blk = pl.BlockSpec((2, tq, LANES), lambda p, j: (p, j, 0))
    pair = pl.BlockSpec((seq, LANES), lambda p, j: (0, p))
    shape = jax.ShapeDtypeStruct((MLA_HEADS, seq, LANES), F32)
    return pl.pallas_call(
        body, name="mla_bwd", grid=(MLA_HEADS // 2, nq),
        in_specs=[whole, blk, blk, pair, pair, whole],
        out_specs=[whole, blk, blk], out_shape=[shape] * 3,
        scratch_shapes=[pltpu.VMEM((2, seq, LANES), F32), pltpu.VMEM((2, tq, LANES), F32),
                        pltpu.VMEM((2, tq, LANES), F32), pltpu.VMEM((2, tq, LANES), BF16)],
        compiler_params=_cp(("arbitrary", "arbitrary"), VMEM_LIMIT),
    )(q, k, v, d_o, o, lse)


def _dil_bwd(qr, kr, qkv, d_o, o, lse):
    seq = qr.shape[0]
    nq = DIL_Q_BWD
    n_tiles = seq // nq
    chunk = 512

    def body(q_ref, k_ref, v_ref, do_ref, o_ref, lse_ref, dq_ref, dk_ref, dv_ref, d_s, dq_s, dk_s, dv_s, bias_s):
        lane = lax.broadcasted_iota(jnp.int32, (nq, LANES), 1)
        lanec = lax.broadcasted_iota(jnp.int32, (chunk, LANES), 1)
        bias_s[0], bias_s[1] = [b[:nq] for b in _dil_bias(nq)]

        def rowsum(i, carry):
            rows = pl.ds(pl.multiple_of(i * chunk, chunk), chunk)
            prod = do_ref[rows, :] * o_ref[rows, :]
            lo = jnp.sum(jnp.where(lanec < 64, prod, 0.0), axis=1, keepdims=True)
            hi = jnp.sum(jnp.where(lanec >= 64, prod, 0.0), axis=1, keepdims=True)
            d_s[rows, :] = jnp.where(lanec < 64, lo, hi)
            return carry

        lax.fori_loop(0, seq // chunk, rowsum, 0)
        dq_s[...] = jnp.zeros_like(dq_s)
        dk_s[...] = jnp.zeros_like(dk_s)
        dv_s[...] = jnp.zeros_like(dv_s)
        for d in DIL_DILATIONS:

            def tile(t, carry, d=d):
                first, start, prev = _dil_tile_index(t, d, seq, nq)
                rows, prows = _dil_rows(start, d, nq), _dil_rows(prev, d, BLOCK)
                q_t, do_t = q_ref[rows, :], do_ref[rows, :]
                lse_t, d_t = lse_ref[rows, :], d_s[rows, :]
                kcat = jnp.concatenate([k_ref[prows, :], k_ref[rows, :]], axis=0).astype(BF16)
                vcat = jnp.concatenate([v_ref[prows, :], v_ref[rows, :]], axis=0).astype(BF16)
                bias = bias_s[first]
                dq_t = jnp.zeros((nq, LANES), F32)
                dkcat = jnp.zeros((BLOCK + nq, LANES), F32)
                dvcat = jnp.zeros((BLOCK + nq, LANES), F32)
                for hh in range(2):
                    mine = (lane >= 64) if hh else (lane < 64)
                    c0 = 64 * hh
                    qh = jnp.where(mine, q_t, 0.0).astype(BF16)
                    doh = jnp.where(mine, do_t, 0.0).astype(BF16)
                    p = jnp.exp(_dot(qh, kcat, NT) + bias - lse_t[:, c0:c0 + 1])
                    dvcat = dvcat + _dot(p.astype(BF16), doh, TN)
                    dp = _dot(doh, vcat, NT)
                    ds = (p * (dp - d_t[:, c0:c0 + 1])).astype(BF16)
                    dq_t = dq_t + jnp.where(mine, _dot(ds, kcat), 0.0)
                    dkcat = dkcat + _dot(ds, qh, TN)
                dq_s[rows, :] += dq_t
                dk_s[prows, :] += dkcat[:BLOCK]
                dk_s[rows, :] += dkcat[BLOCK:]
                dv_s[prows, :] += dvcat[:BLOCK]
                dv_s[rows, :] += dvcat[BLOCK:]
                return carry

            lax.fori_loop(0, n_tiles, tile, 0, unroll=4)
        dq_ref[...] = dq_s[...].astype(BF16)
        dk_ref[...] = dk_s[...].astype(BF16)
        dv_ref[...] = dv_s[...].astype(BF16)

    col = lambda off: pl.BlockSpec((seq, LANES), lambda p: (0, p + off))
    shape = jax.ShapeDtypeStruct((seq, 4 * LANES), BF16)
    return pl.pallas_call(
        body, name="dil_bwd", grid=(4,),
        in_specs=[col(0), col(0), col(8), col(4), col(0), pl.BlockSpec((None, seq, LANES), lambda p: (p, 0, 0))],
        out_specs=[col(0)] * 3, out_shape=[shape] * 3,
        scratch_shapes=[pltpu.VMEM((seq, LANES), F32)] * 4 + [pltpu.VMEM((2, nq, BLOCK + nq), F32)],
        compiler_params=_cp(("arbitrary",), VMEM_LIMIT),
    )(qr, kr, qkv, d_o, o, lse)


def _mla_pre_bwd(cq, ckv, gq, gkv, wuq_e, wukv, ct, st, dq, dk, dv):
    seq = cq.shape[0]
    tr = 512

    def body(cq_ref, ckv_ref, gq_ref, gkv_ref, wuq_ref, wukv_ref, ct_ref, st_ref, dq_ref, dk_ref, dv_ref,
             dcq_ref, dckv_ref, dkr_ref, dwuq_ref, dwukv_ref, dgq_ref, dgkv_ref):
        @pl.when(pl.program_id(0) == 0)
        def _():
            dwuq_ref[...] = jnp.zeros_like(dwuq_ref)
            dwukv_ref[...] = jnp.zeros_like(dwukv_ref)
            dgq_ref[...] = jnp.zeros_like(dgq_ref)
            dgkv_ref[...] = jnp.zeros_like(dgkv_ref)

        lane = lax.broadcasted_iota(jnp.int32, (tr, LANES), 1)
        rope_lanes = jnp.logical_and(lane >= 64, lane < 96)
        ct_, st_ = ct_ref[...], st_ref[...]

        def rope_t(g):
            return ct_ * g + jnp.where(rope_lanes, _mla_rot(st_ * g, lane), 0.0)

        def norm_bwd(c, g, dn, dg_ref):
            r, _ = _rms(c, g)
            u = dn * g
            dg_ref[...] += jnp.sum(dn * c * r, axis=0, keepdims=True)
            return r * u - c * (r * r * r) * jnp.mean(u * c, axis=-1, keepdims=True)

        c, g = cq_ref[...], gq_ref[...]
        _, qn = _rms(c, g)
        dq_all = jnp.concatenate([rope_t(dq_ref[h] * MLA_SCALE) for h in range(MLA_HEADS)], axis=1).astype(BF16)
        dwuq_ref[...] += _dot(qn.astype(BF16), dq_all, TN)
        dcq_ref[...] = norm_bwd(c, g, _dot(dq_all, wuq_ref[...], NT), dgq_ref).astype(BF16)

        c, g = ckv_ref[...], gkv_ref[...]
        _, kvn = _rms(c, g)
        dkpe = jnp.zeros((tr, LANES), F32)
        parts = []
        for h in range(MLA_HEADS):
            dk_h, dv_h = dk_ref[h], dv_ref[h]
            if h % 2 == 0:
                dv_h = pltpu.roll(dv_h, 64, 1)
            parts.append(jnp.where(lane < 64, dk_h, dv_h))
            dkpe = dkpe + jnp.where(rope_lanes, dk_h, 0.0)
        dkv_all = jnp.concatenate(parts, axis=1).astype(BF16)
        dwukv_ref[...] += _dot(kvn.astype(BF16), dkv_all, TN)
        dckv_ref[...] = norm_bwd(c, g, _dot(dkv_all, wukv_ref[...], NT), dgkv_ref).astype(BF16)
        dkr_ref[...] = rope_t(dkpe).astype(BF16)

    row = lambda w: pl.BlockSpec((tr, w), lambda i: (i, 0))
    full = lambda a: pl.BlockSpec(a.shape, lambda i: (0,) * a.ndim)
    head = pl.BlockSpec((MLA_HEADS, tr, LANES), lambda i: (0, i, 0))
    return pl.pallas_call(
        body, name="mla_pre_bwd", grid=(seq // tr,),
        in_specs=[row(Q_LORA), row(KV_LORA), full(gq), full(gkv), full(wuq_e), full(wukv), row(LANES), row(LANES),
                  head, head, head],
        out_specs=[row(Q_LORA), row(KV_LORA), row(LANES), full(wuq_e), full(wukv), full(gq), full(gkv)],
        out_shape=[jax.ShapeDtypeStruct((seq, Q_LORA), BF16), jax.ShapeDtypeStruct((seq, KV_LORA), BF16),
                   jax.ShapeDtypeStruct((seq, LANES), BF16), jax.ShapeDtypeStruct(wuq_e.shape, F32),
                   jax.ShapeDtypeStruct(wukv.shape, F32), jax.ShapeDtypeStruct(gq.shape, F32),
                   jax.ShapeDtypeStruct(gkv.shape, F32)],
        compiler_params=_cp(("arbitrary",), VMEM_LIMIT),
    )(cq, ckv, gq, gkv, wuq_e, wukv, ct, st, dq, dk, dv)


def _in_bwd(dz, dcq, dckv, dgates, dqr, dkr, dvb, dkrope, cd, sd, w_in_p):
    seq = dz.shape[0]
    tr = 512

    def body(dz_ref, dcq_ref, dckv_ref, dg_ref, dqr_ref, dkr_ref, dvb_ref, dkp_ref, cd_ref, sd_ref, w_ref, gx_ref, dh_ref):
        lane = lax.broadcasted_iota(jnp.int32, (tr, LANES), 1)
        rot_lanes = lane % 64 < DIL_ROT
        cd_, sd_ = cd_ref[...], sd_ref[...]

        def rope_t(g):
            return cd_ * g + jnp.where(rot_lanes, _dil_rot(sd_ * g, lane), 0.0)

        dq = [rope_t(dqr_ref[:, LANES * p:LANES * (p + 1)].astype(F32) * DIL_SCALE).astype(BF16) for p in range(4)]
        dk = [rope_t(dkr_ref[:, LANES * p:LANES * (p + 1)].astype(F32)).astype(BF16) for p in range(4)]
        dh = jnp.concatenate([dcq_ref[...], dckv_ref[...], dg_ref[...]] + dq + dk + [dvb_ref[...], dkp_ref[...]], axis=1)
        dh_ref[...] = dh
        gx_ref[...] = ALPHA * dz_ref[...] + _dot(dh, w_ref[...], NT)

    row = lambda w: pl.BlockSpec((tr, w), lambda i: (i, 0))
    return pl.pallas_call(
        body, name="in_bwd", grid=(seq // tr,),
        in_specs=[row(D_MODEL), row(Q_LORA), row(KV_LORA), row(D_MODEL), row(512), row(512), row(512), row(LANES),
                  row(LANES), row(LANES), pl.BlockSpec((D_MODEL, IN_WIDTH_PAD), lambda i: (0, 0))],
        out_specs=[row(D_MODEL), row(IN_WIDTH_PAD)],
        out_shape=[jax.ShapeDtypeStruct((seq, D_MODEL), F32), jax.ShapeDtypeStruct((seq, IN_WIDTH_PAD), BF16)],
        compiler_params=_cp(("arbitrary",), VMEM_LIMIT),
    )(dz, dcq, dckv, dgates, dqr, dkr, dvb, dkrope, cd, sd, w_in_p)


def _dw_in(x, dh):
    seq = dh.shape[0]
    tk = 512
    tn = IN_WIDTH_PAD // 2

    def body(x_ref, dh_ref, o_ref):
        @pl.when(pl.program_id(1) == 0)
        def _():
            o_ref[...] = jnp.zeros_like(o_ref)

        o_ref[...] += _dot(dh_ref[...], x_ref[...].astype(BF16), TN)

    return pl.pallas_call(
        body, name="dw_in", grid=(2, seq // tk),
        in_specs=[pl.BlockSpec((tk, D_MODEL), lambda n, k: (k, 0)), pl.BlockSpec((tk, tn), lambda n, k: (k, n))],
        out_specs=pl.BlockSpec((tn, D_MODEL), lambda n, k: (n, 0)),
        out_shape=jax.ShapeDtypeStruct((IN_WIDTH_PAD, D_MODEL), F32),
        compiler_params=_cp(("arbitrary", "arbitrary"), VMEM_LIMIT),
    )(x, dh)


def _adamw(w, g, m, v, name):
    rows, cols = w.shape
    tc = 256 if cols % 256 == 0 and rows * cols > 2 ** 18 else cols

    def body(w_ref, g_ref, m_ref, v_ref, d_ref, nm_ref, nv_ref):
        g_ = g_ref[...]
        nm = ADAM_B1 * m_ref[...] + (1.0 - ADAM_B1) * g_
        nv = ADAM_B2 * v_ref[...] + (1.0 - ADAM_B2) * jnp.square(g_)
        m_hat = nm / (1.0 - ADAM_B1 ** ADAM_STEP)
        v_hat = nv / (1.0 - ADAM_B2 ** ADAM_STEP)
        d_ref[...] = -ADAM_LR * (m_hat / (jnp.sqrt(v_hat) + ADAM_EPS) + ADAM_WD * w_ref[...])
        nm_ref[...] = nm
        nv_ref[...] = nv

    spec = pl.BlockSpec((rows, tc), lambda i: (0, i))
    return pl.pallas_call(
        body, name=name, grid=(cols // tc,), in_specs=[spec] * 4, out_specs=[spec] * 3,
        out_shape=[jax.ShapeDtypeStruct(w.shape, F32)] * 3, compiler_params=_cp(("arbitrary",)),
    )(w, g, m, v)


def _pad_row(v):
    return jnp.pad(v.reshape(1, -1), ((0, 0), (0, D_MODEL - v.shape[-1])))


def _local_step(x2, target, w_in_p, w_uq_f, wukv_f, w_out_f, q_norm_g, kv_norm_g, ln_g, ln_b):
    seq = x2.shape[0]
    wuq_e = jnp.pad(w_uq_f.reshape(Q_LORA, MLA_HEADS, 96), ((0, 0), (0, 0), (0, 32))).reshape(Q_LORA, MLA_HEADS * LANES)
    ct, st, cd, sd = _rope_tables(seq)
    gq = q_norm_g.reshape(1, Q_LORA)
    gkv = kv_norm_g.reshape(1, KV_LORA)

    cq, ckv, gates, qkv, kr = _proj(x2, w_in_p)
    q_e, k_e, v_e = _mla_pre(cq, ckv, kr, gq, gkv, wuq_e, wukv_f, ct, st)
    o_a, lse_a = _mla_fwd(q_e, k_e, v_e)
    qr, krot = _dil_pre(qkv, cd, sd)
    o_b, lse_b = _dil_fwd(qr, krot, qkv)

    dz, d_o, d_gates, dw_out, dln_g, dln_b, loss_part = _post(
        x2, o_a, o_b, gates, w_out_f, ln_g.reshape(1, D_MODEL), ln_b.reshape(1, D_MODEL), target)
    dq_e, dk_e, dv_e = _mla_bwd(q_e, k_e, v_e, d_o, o_a, lse_a)
    dqr, dkr, dvb = _dil_bwd(qr, krot, qkv, d_o, o_b, lse_b)
    dcq, dckv, dkrope, dwuq_e, dwukv, dgq, dgkv = _mla_pre_bwd(cq, ckv, gq, gkv, wuq_e, wukv_f, ct, st, dq_e, dk_e, dv_e)
    grad_x, dh = _in_bwd(dz, dcq, dckv, d_gates, dqr, dkr, dvb, dkrope, cd, sd, w_in_p)
    dw_in = _unpermute_dw_in_t(_dw_in(x2, dh))
    dw_uq = dwuq_e.reshape(Q_LORA, MLA_HEADS, LANES)[:, :, :96].reshape(Q_LORA, MLA_HEADS * 96)
    return loss_part, grad_x, dw_in, dw_uq, dwukv, dw_out, dgq, dgkv, dln_g, dln_b


def kernel(x, w_in, q_norm_g, kv_norm_g, w_uq, w_ukv, w_out, ln_g, ln_b, loss_target, m_w_in, m_q_norm_g, m_kv_norm_g, m_w_uq, m_w_ukv, m_w_out, m_ln_g, m_ln_b, v_w_in, v_q_norm_g, v_kv_norm_g, v_w_uq, v_w_ukv, v_w_out, v_ln_g, v_ln_b):
    seq = x.shape[1]
    x2 = x.reshape(seq, D_MODEL)
    target = loss_target.reshape(seq, D_MODEL)

    g_w_in, g_w_uq, g_w_ukv, g_w_out = _all_gather_weights([w_in, w_uq, w_ukv, w_out])
    by_cols = lambda g: jnp.concatenate([g[j] for j in range(N_SHARD)], axis=1)
    loss_part, grad_x, dw_in, dw_uq, dwukv, dw_out, dgq, dgkv, dln_g, dln_b = _local_step(
        x2, target, _permute_w_in_shards(g_w_in), by_cols(g_w_uq), by_cols(g_w_ukv), g_w_out.reshape(D_MODEL, D_MODEL),
        q_norm_g, kv_norm_g, ln_g, ln_b)

    to_shards = lambda d: d.reshape(d.shape[0], N_SHARD, d.shape[1] // N_SHARD).transpose(1, 0, 2)
    grads = [dw_in.reshape(N_SHARD, 808, D_MODEL), to_shards(dw_uq), to_shards(dwukv),
             dw_out.reshape(N_SHARD, 256, D_MODEL)]
    small = jnp.concatenate([_pad_row(dgq), _pad_row(dgkv), dln_g, dln_b, _pad_row(loss_part),
                             jnp.zeros((3, D_MODEL), F32)], axis=0)
    *chip_sums, smalls = _reduce_over_sibling(grads, small)
    g_in_t, g_uq, g_ukv, g_out = _reduce_over_chips(chip_sums)
    g_in = g_in_t.T
    small_sum = _sum_smalls(smalls)
    loss = small_sum[4, 0]

    big = [[o.T for o in _adamw(w.T, g.T, m.T, v.T, name)] for w, g, m, v, name in (
        (w_in, g_in, m_w_in, v_w_in, "adamw_w_in"), (w_uq, g_uq, m_w_uq, v_w_uq, "adamw_w_uq"))]
    big += [_adamw(w, g, m, v, name) for w, g, m, v, name in (
        (w_ukv, g_ukv, m_w_ukv, v_w_ukv, "adamw_w_ukv"), (w_out, g_out, m_w_out, v_w_out, "adamw_w_out"))]
    vec = lambda a, b, c_, d: jnp.concatenate([_pad_row(a), _pad_row(b), _pad_row(c_), _pad_row(d),
                                               jnp.zeros((4, D_MODEL), F32)], axis=0)
    sw = vec(q_norm_g, kv_norm_g, ln_g, ln_b)
    sm = vec(m_q_norm_g, m_kv_norm_g, m_ln_g, m_ln_b)
    sv = vec(v_q_norm_g, v_kv_norm_g, v_ln_g, v_ln_b)
    sg = jnp.concatenate([small_sum[:4], jnp.zeros((4, D_MODEL), F32)], axis=0)
    s_delta, s_m, s_v = _adamw(sw, sg, sm, sv, "adamw_vectors")

    def vectors(a):
        return [a[0, :Q_LORA], a[1, :KV_LORA], a[2], a[3]]

    def ordered(bigs, smalls_):
        return [bigs[0], smalls_[0], smalls_[1], bigs[1], bigs[2], bigs[3], smalls_[2], smalls_[3]]

    grads_out = ordered([g_in, g_uq, g_ukv, g_out], vectors(small_sum))
    deltas = ordered([b[0] for b in big], vectors(s_delta))
    new_m = ordered([b[1] for b in big], vectors(s_m))
    new_v = ordered([b[2] for b in big], vectors(s_v))
    return (loss, grad_x.reshape(x.shape), *grads_out, *deltas, *new_m, *new_v)
```

```python
import functools

import jax
import jax.numpy as jnp
import numpy as np
from jax import lax
from jax.experimental import pallas as pl
from jax.experimental.pallas import tpu as pltpu

F32 = jnp.float32
BF16 = jnp.bfloat16

D_MODEL = 1024
ROPE_THETA = 500000.0
BLOCK = 128
NEG = -1e30
RMS_EPS = 1e-6
LN_EPS = 1e-5

MLA_HEADS = 8
MLA_NOPE = 64
MLA_ROPE = 32
Q_LORA = 384
KV_LORA = 256
DIL_HEADS = 8
DIL_HEAD_DIM = 64
DIL_ROT = 16
DIL_DILATIONS = (1, 4, 16)
IN_WIDTH = 3232
IN_WIDTH_PAD = 3328
ONES_LANE = (64, 0)
MLA_SCALE = (MLA_NOPE + MLA_ROPE) ** -0.5
DIL_SCALE = DIL_HEAD_DIM ** -0.5
ALPHA = 2.0 ** 0.25

ADAM_LR = 0.001
ADAM_B1 = 0.9
ADAM_B2 = 0.999
ADAM_EPS = 1e-08
ADAM_WD = 0.01
ADAM_STEP = 10

N_SHARD = 4
SHARD_SHAPES = ((1024, 808), (384, 192), (256, 256), (256, 1024))
GRAD_SHAPES = ((808, 1024), (384, 192), (256, 256), (256, 1024))
GRAD_SPLIT_COLS = (True, False, False, False)
ROW_CHUNK = 64
LANES = 128
VMEM_LIMIT = 56 * 1024 * 1024
MESH = pl.DeviceIdType.MESH

NT = (((1,), (1,)), ((), ()))
TN = (((0,), (0,)), ((), ()))


def _cp(sem=None, vmem=None):
    return pltpu.CompilerParams(dimension_semantics=sem, vmem_limit_bytes=vmem)


def _dot(a, b, dims=None):
    if dims is None:
        return jnp.dot(a, b, preferred_element_type=F32)
    return lax.dot_general(a, b, dims, preferred_element_type=F32)


def _rope_tables(seq):
    f32 = np.float32
    pos = np.arange(seq, dtype=f32)[:, None]
    one, zero = np.ones((seq, 64), f32), np.zeros((seq, 64), f32)

    def cos_sin(dim):
        inv = np.power(f32(ROPE_THETA), -np.arange(0, dim, 2, dtype=f32) / f32(dim)).astype(f32)
        ang = (pos * inv[None, :]).astype(f32)
        return np.cos(ang).astype(f32), np.sin(ang).astype(f32)

    cos, sin = cos_sin(MLA_ROPE)
    ct = np.concatenate([one, cos, cos, zero[:, :32]], axis=1)
    st = np.concatenate([zero, -sin, sin, zero[:, :32]], axis=1)
    cos, sin = cos_sin(DIL_ROT)
    cd = np.concatenate([cos, cos, one[:, :48]], axis=1)
    sd = np.concatenate([-sin, sin, zero[:, :48]], axis=1)
    return tuple(jnp.asarray(t) for t in (ct, st, np.tile(cd, (1, 2)), np.tile(sd, (1, 2))))


W_IN_ORDER = ((0, 640), (672, 1184), (2720, 3232), (1184, 2720), None, (640, 672))


def _permute_w_in(w):
    z = jnp.zeros((w.shape[0], 64), w.dtype)
    parts = [z if r is None else w[:, r[0]:r[1]] for r in W_IN_ORDER]
    return jnp.concatenate(parts + [z[:, :32]], axis=1)


def _permute_w_in_shards(g):
    width = g.shape[2]
    z = jnp.zeros((g.shape[1], 64), g.dtype)
    parts = []
    for r in W_IN_ORDER:
        if r is None:
            parts.append(z)
            continue
        for j in range(N_SHARD):
            lo, hi = max(r[0], width * j), min(r[1], width * (j + 1))
            if lo < hi:
                parts.append(g[j, :, lo - width * j:hi - width * j])
    return jnp.concatenate(parts + [z[:, :32]], axis=1)


def _unpermute_dw_in_t(dw_t):
    return jnp.concatenate([dw_t[0:640], dw_t[3264:3296], dw_t[640:1152], dw_t[1664:3200], dw_t[1152:1664]], axis=0)


def _position():
    return lax.axis_index("x"), lax.axis_index("y"), lax.axis_index("c")


def _halves(c, rows):
    hr = rows // 2
    return pl.ds(pl.multiple_of(c * hr, 8), hr), pl.ds(pl.multiple_of((1 - c) * hr, 8), hr)


def _for_row_chunks(rows, fn):
    def step(i, carry):
        fn(pl.multiple_of(i * ROW_CHUNK, ROW_CHUNK))
        return carry

    lax.fori_loop(0, rows // ROW_CHUNK, step, 0)


def _all_gather_weights(shards):
    n = len(shards)

    def body(*refs):
        ins, outs = refs[:n], refs[n:2 * n]
        send_sems, recv_sems = refs[2 * n:]
        x, y, c = _position()
        me = 2 * x + y
        chips = [(1 - x, y), (x, 1 - y), (1 - x, 1 - y)]
        for a in range(n):
            def cast(r, a=a):
                outs[a][me, pl.ds(r, ROW_CHUNK), :] = ins[a][pl.ds(r, ROW_CHUNK), :].astype(BF16)

            _for_row_chunks(SHARD_SHAPES[a][0], cast)

        def copy(k, a, slot, rows, to):
            ref = outs[a].at[slot, rows]
            return pltpu.make_async_remote_copy(
                src_ref=ref, dst_ref=ref, send_sem=send_sems.at[k * n + a], recv_sem=recv_sems.at[k * n + a],
                device_id=to, device_id_type=MESH)

        half = [_halves(c, SHARD_SHAPES[a][0])[0] for a in range(n)]
        other = [_halves(c, SHARD_SHAPES[a][0])[1] for a in range(n)]
        first = [copy(k, a, me, half[a], (px, py, c)) for k, (px, py) in enumerate(chips) for a in range(n)]
        for cp in first:
            cp.start()
        passed = []
        for k, (px, py) in enumerate(chips):
            for a in range(n):
                copy(k, a, 2 * px + py, half[a], (x, y, c)).wait_recv()
                cp = copy(3 + k, a, 2 * px + py, half[a], (x, y, 1 - c))
                cp.start()
                passed.append(cp)
        for k, (px, py) in enumerate(chips):
            for a in range(n):
                copy(3 + k, a, 2 * px + py, other[a], (x, y, c)).wait_recv()
        for cp in first + passed:
            cp.wait_send()

    vmem = pl.BlockSpec(memory_space=pltpu.VMEM)
    return pl.pallas_call(
        body, name="all_gather_weights",
        out_shape=[jax.ShapeDtypeStruct((N_SHARD,) + s, BF16) for s in SHARD_SHAPES],
        in_specs=[vmem] * n, out_specs=[vmem] * n,
        scratch_shapes=[pltpu.SemaphoreType.DMA((6 * n,)), pltpu.SemaphoreType.DMA((6 * n,))],
        compiler_params=_cp(None, VMEM_LIMIT),
    )(*shards)


def _grad_half_shape(a):
    rows, cols = GRAD_SHAPES[a]
    return (rows, cols // 2) if GRAD_SPLIT_COLS[a] else (rows // 2, cols)


def _grad_half(a, c):
    rows, cols = GRAD_SHAPES[a]
    if GRAD_SPLIT_COLS[a]:
        return slice(None), pl.ds(pl.multiple_of(c * (cols // 2), LANES), cols // 2)
    return pl.ds(pl.multiple_of(c * (rows // 2), ROW_CHUNK), rows // 2), slice(None)


def _grad_chunks(a, c):
    rows, cols = GRAD_SHAPES[a]
    if GRAD_SPLIT_COLS[a]:
        return [((slice(None), pl.ds(c0, LANES)),
                 (slice(None), pl.ds(pl.multiple_of(c * (cols // 2) + c0, LANES), LANES)))
                for c0 in range(0, cols // 2, LANES)]
    return [((pl.ds(r0, ROW_CHUNK), slice(None)),
             (pl.ds(pl.multiple_of(c * (rows // 2) + r0, ROW_CHUNK), ROW_CHUNK), slice(None)))
            for r0 in range(0, rows // 2, ROW_CHUNK)]


def _reduce_over_sibling(grads, small):
    n = len(grads)

    def body(*refs):
        g_hbm, sm = refs[:n], refs[n]
        sums, smalls = refs[n + 1:2 * n + 1], refs[2 * n + 1]
        stage, got = refs[2 * n + 2:3 * n + 2], refs[3 * n + 2:4 * n + 2]
        send_sems, recv_sems, local_sems = refs[4 * n + 2:]
        x, y, c = _position()
        me = 4 * x + 2 * y + c
        loads = [pltpu.make_async_copy(g_hbm[a], stage[a], local_sems.at[a]) for a in range(n)]
        for ld in loads:
            ld.start()
        smalls[me] = sm[...]
        sends = []
        for rel in range(1, 8):
            px = 1 - x if rel // 4 else x
            py = 1 - y if (rel // 2) % 2 else y
            pc = 1 - c if rel % 2 else c
            cp = pltpu.make_async_remote_copy(
                src_ref=sm, dst_ref=smalls.at[me], send_sem=send_sems.at[n + rel], recv_sem=recv_sems.at[n + rel],
                device_id=(px, py, pc), device_id_type=MESH)
            cp.start()
            sends.append((cp, 4 * px + 2 * py + pc))
        swaps = []
        for a in range(n):
            loads[a].wait()
            cp = pltpu.make_async_remote_copy(
                src_ref=stage[a].at[(slice(None),) + _grad_half(a, 1 - c)], dst_ref=got[a], send_sem=send_sems.at[a], recv_sem=recv_sems.at[a],
                device_id=(x, y, 1 - c), device_id_type=MESH)
            cp.start()
            swaps.append(cp)
        for a in range(n):
            swaps[a].wait_recv()
            for k in range(N_SHARD):
                for in_half, in_whole in _grad_chunks(a, c):
                    pair = stage[a][(k,) + in_whole] + got[a][(k,) + in_half]
                    sums[a][(k,) + in_half] = pair.astype(BF16)
        for rel, (cp, peer) in enumerate(sends, start=1):
            pltpu.make_async_remote_copy(
                src_ref=sm, dst_ref=smalls.at[peer], send_sem=send_sems.at[n + rel], recv_sem=recv_sems.at[n + rel],
                device_id=(x, y, c), device_id_type=MESH).wait_recv()
        for cp in swaps:
            cp.wait_send()
        for cp, _ in sends:
            cp.wait_send()

    vmem = pl.BlockSpec(memory_space=pltpu.VMEM)
    half = [(N_SHARD,) + _grad_half_shape(a) for a in range(n)]
    return pl.pallas_call(
        body, name="reduce_over_sibling",
        out_shape=[jax.ShapeDtypeStruct(s, BF16) for s in half] + [jax.ShapeDtypeStruct((8,) + small.shape, F32)],
        in_specs=[pl.BlockSpec(memory_space=pl.ANY)] * n + [vmem], out_specs=[vmem] * (n + 1),
        scratch_shapes=[pltpu.VMEM((N_SHARD,) + s, F32) for s in GRAD_SHAPES] + [pltpu.VMEM(s, F32) for s in half]
        + [pltpu.SemaphoreType.DMA((n + 8,)), pltpu.SemaphoreType.DMA((n + 8,)), pltpu.SemaphoreType.DMA((n,))],
        compiler_params=_cp(None, VMEM_LIMIT),
    )(*grads, small)


def _reduce_over_chips(sums):
    n = len(sums)

    def body(*refs):
        h, outs, got = refs[:n], refs[n:2 * n], refs[2 * n:3 * n]
        send_sems, recv_sems = refs[3 * n:]
        x, y, c = _position()
        me = 2 * x + y
        chips = [(1 - x, y), (x, 1 - y), (1 - x, 1 - y)]
        sends = []
        for k, (px, py) in enumerate(chips):
            for a in range(n):
                cp = pltpu.make_async_remote_copy(
                    src_ref=h[a].at[2 * px + py], dst_ref=got[a].at[k], send_sem=send_sems.at[k * n + a],
                    recv_sem=recv_sems.at[k * n + a], device_id=(px, py, c), device_id_type=MESH)
                cp.start()
                sends.append(cp)
        for cp in sends:
            cp.wait_recv()
        joins = []
        for a in range(n):
            for in_half, in_whole in _grad_chunks(a, c):
                total = h[a][(me,) + in_half].astype(F32)
                for k in range(3):
                    total = total + got[a][(k,) + in_half].astype(F32)
                outs[a][in_whole] = total
            half = outs[a].at[_grad_half(a, c)]
            cp = pltpu.make_async_remote_copy(
                src_ref=half, dst_ref=half, send_sem=send_sems.at[3 * n + a],
                recv_sem=recv_sems.at[3 * n + a], device_id=(x, y, 1 - c), device_id_type=MESH)
            cp.start()
            joins.append(cp)
        for a in range(n):
            other = outs[a].at[_grad_half(a, 1 - c)]
            pltpu.make_async_remote_copy(
                src_ref=other, dst_ref=other, send_sem=send_sems.at[3 * n + a],
                recv_sem=recv_sems.at[3 * n + a], device_id=(x, y, c), device_id_type=MESH).wait_recv()
        for cp in sends + joins:
            cp.wait_send()

    vmem = pl.BlockSpec(memory_space=pltpu.VMEM)
    return pl.pallas_call(
        body, name="reduce_over_chips",
        out_shape=[jax.ShapeDtypeStruct(s, F32) for s in GRAD_SHAPES],
        in_specs=[vmem] * n, out_specs=[vmem] * n,
        scratch_shapes=[pltpu.VMEM((3,) + _grad_half_shape(a), BF16) for a in range(n)]
        + [pltpu.SemaphoreType.DMA((4 * n,)), pltpu.SemaphoreType.DMA((4 * n,))],
        compiler_params=_cp(None, VMEM_LIMIT),
    )(*sums)


def _sum_smalls(smalls):
    def body(s, o):
        acc = s[0]
        for d in range(1, 8):
            acc = acc + s[d]
        o[...] = acc

    return pl.pallas_call(body, name="sum_smalls", out_shape=jax.ShapeDtypeStruct(smalls.shape[1:], F32))(smalls)


def _proj(x, w_in_p, gq, gkv, wuq_e, wukv, ct, st, cd, sd):
    seq = x.shape[0]
    tr = 512

    def body(x_ref, w_ref, gq_ref, gkv_ref, wuq_ref, wukv_ref, ct_ref, st_ref, cd_ref, sd_ref,
             cq_ref, ckv_ref, g_ref, qr_ref, kr_ref, vb_ref, q_out, k_out, v_out):
        lane = lax.broadcasted_iota(jnp.int32, (tr, LANES), 1)
        xb = x_ref[...].astype(BF16)
        cq = _dot(xb, w_ref[:, 0:384])
        ckv = _dot(xb, w_ref[:, 384:640])
        cq_ref[...] = cq
        ckv_ref[...] = ckv
        g_ref[...] = _dot(xb, w_ref[:, 640:1664])

        cd_, sd_ = cd_ref[...], sd_ref[...]
        qb = _dot(xb, w_ref[:, 1664:2176])
        kb = _dot(xb, w_ref[:, 2176:2688])
        for p in range(4):
            cols = slice(LANES * p, LANES * (p + 1))
            t = qb[:, cols]
            qr_ref[:, cols] = (t * cd_ + _dil_rot(t, lane) * sd_) * DIL_SCALE
            t = kb[:, cols]
            kr_ref[:, cols] = t * cd_ + _dil_rot(t, lane) * sd_
        vb_ref[...] = _dot(xb, w_ref[:, 2688:3200])

        ct_, st_ = ct_ref[...], st_ref[...]

        def rope(t):
            return t * ct_ + _mla_rot(t, lane) * st_

        _, qn = _rms(cq, gq_ref[...])
        q_all = _dot(qn.astype(BF16), wuq_ref[...])
        for h in range(MLA_HEADS):
            q_out[h] = (rope(q_all[:, LANES * h:LANES * (h + 1)]) * MLA_SCALE).astype(BF16)
        _, kvn = _rms(ckv, gkv_ref[...])
        kv_all = _dot(kvn.astype(BF16), wukv_ref[...])
        kpe = rope(_dot(xb, w_ref[:, 3200:3328]))
        for h in range(MLA_HEADS):
            kv_h = kv_all[:, LANES * h:LANES * (h + 1)]
            k_out[h] = jnp.where(lane < 64, kv_h, kpe).astype(BF16)
            if h % 2:
                v = jnp.where(lane >= 64, kv_h, 0.0)
            else:
                v = jnp.where(lane < 64, pltpu.roll(kv_h, 64, 1), 0.0)
            v_out[h] = jnp.where(lane == ONES_LANE[h % 2], 1.0, v).astype(BF16)

    row = lambda w: pl.BlockSpec((tr, w), lambda i: (i, 0))
    full = lambda a: pl.BlockSpec(a.shape, lambda i: (0,) * a.ndim)
    head = pl.BlockSpec((MLA_HEADS, tr, LANES), lambda i: (0, i, 0))
    widths = (Q_LORA, KV_LORA, D_MODEL, 512, 512, 512)
    return pl.pallas_call(
        body, name="proj", grid=(seq // tr,),
        in_specs=[row(D_MODEL), full(w_in_p), full(gq), full(gkv), full(wuq_e), full(wukv)] + [row(LANES)] * 4,
        out_specs=[row(w) for w in widths] + [head] * 3,
        out_shape=[jax.ShapeDtypeStruct((seq, w), F32) for w in widths]
        + [jax.ShapeDtypeStruct((MLA_HEADS, seq, LANES), BF16)] * 3,
        compiler_params=_cp(("arbitrary",), VMEM_LIMIT),
    )(x, w_in_p, gq, gkv, wuq_e, wukv, ct, st, cd, sd)


def _mla_rot(t, lane):
    return jnp.where(lane < 80, pltpu.roll(t, 112, 1), pltpu.roll(t, 16, 1))


def _dil_rot(t, lane):
    return jnp.where(lane % 64 < 8, pltpu.roll(t, 120, 1), pltpu.roll(t, 8, 1))


def _rms(c, g):
    r = lax.rsqrt(jnp.mean(c * c, axis=-1, keepdims=True) + RMS_EPS)
    return r, c * r * g


def _mla_fwd(q, k, v):
    seq = q.shape[1]
    tq = 512
    nq = seq // tq

    def body(q_ref, k_ref, v_ref, o_ref, lse_ref, m_s, acc_s, s_buf):
        i = pl.program_id(1)
        row = lax.broadcasted_iota(jnp.int32, (tq, tq), 0)
        col = lax.broadcasted_iota(jnp.int32, (tq, tq), 1)
        lane = lax.broadcasted_iota(jnp.int32, (tq, LANES), 1)
        m_s[...] = jnp.full((2, tq, LANES), NEG, F32)
        acc_s[...] = jnp.zeros((2, tq, LANES), F32)

        def block(j):
            return pl.ds(pl.multiple_of(j * tq, tq), tq)

        def scores(hh, j):
            return _dot(q_ref[hh], k_ref[hh, block(j), :], NT)

        def consume(hh, j, s):
            m_prev = m_s[hh]
            m_new = jnp.maximum(m_prev, jnp.max(s, axis=1, keepdims=True))
            p = jnp.exp(s - m_new[:, :1])
            acc_s[hh] = jnp.exp(m_prev - m_new) * acc_s[hh] + _dot(p.astype(BF16), v_ref[hh, block(j), :])
            m_s[hh] = m_new

        for hh in range(2):
            s_buf[0, hh] = scores(hh, 0)

        def full_step(j, carry):
            slot = j & 1
            for hh in range(2):
                s = s_buf[slot, hh]
                s_buf[1 - slot, hh] = scores(hh, j + 1)
                consume(hh, j, s)
            return carry

        lax.fori_loop(0, i, full_step, 0)
        total = jnp.zeros((tq, LANES), F32)
        for hh in range(2):
            consume(hh, i, jnp.where(col <= row, s_buf[i & 1, hh], NEG))
            acc = acc_s[hh]
            l = acc[:, ONES_LANE[hh]:ONES_LANE[hh] + 1]
            mine = (lane >= 64) if hh else (lane < 64)
            total = total + jnp.where(mine, acc / l, 0.0)
            lse_ref[hh] = m_s[hh] + jnp.log(l)
        o_ref[...] = total

    kv_spec = pl.BlockSpec((2, seq, LANES), lambda p, i: (p, 0, 0))
    return pl.pallas_call(
        body, name="mla_fwd", grid=(MLA_HEADS // 2, nq),
        in_specs=[pl.BlockSpec((2, tq, LANES), lambda p, i: (p, i, 0)), kv_spec, kv_spec],
        out_specs=[pl.BlockSpec((tq, LANES), lambda p, i: (i, p)), pl.BlockSpec((2, tq, LANES), lambda p, i: (p, i, 0))],
        out_shape=[jax.ShapeDtypeStruct((seq, 4 * LANES), F32), jax.ShapeDtypeStruct((MLA_HEADS, seq, LANES), F32)],
        scratch_shapes=[pltpu.VMEM((2, tq, LANES), F32), pltpu.VMEM((2, tq, LANES), F32),
                        pltpu.VMEM((2, 2, tq, tq), F32)],
        compiler_params=_cp(("arbitrary", "arbitrary"), VMEM_LIMIT),
    )(q, k, v)


DIL_Q_FWD = 2 * BLOCK
DIL_Q_BWD = BLOCK


def _dil_tile_index(t, d, seq, nq):
    per_class = seq // (nq * d)
    shift = per_class.bit_length() - 1
    r = t >> shift
    n = t & (per_class - 1)
    start = r + (nq * d) * n
    prev = jnp.maximum(start - BLOCK * d, r)
    if d == 1:
        start = pl.multiple_of(start, nq)
        prev = pl.multiple_of(prev, BLOCK)
    return (n == 0).astype(jnp.int32), start, prev


def _dil_rows(start, d, size):
    return pl.ds(start, size) if d == 1 else pl.ds(start, size, stride=d)


def _dil_bias(nq):
    i = lax.broadcasted_iota(jnp.int32, (2 * nq, BLOCK + nq), 0) % nq
    j = lax.broadcasted_iota(jnp.int32, (2 * nq, BLOCK + nq), 1)
    band = (j >= i) & (j <= i + BLOCK)
    return jnp.where(band, 0.0, NEG), jnp.where(band & (j >= BLOCK), 0.0, NEG)


def _stack_heads(t, lane):
    return jnp.concatenate([jnp.where(lane < 64, t, 0.0), jnp.where(lane >= 64, t, 0.0)], axis=0)


def _unstack_heads(t, lane):
    nq = t.shape[0] // 2
    return jnp.where(lane < 64, t[:nq], t[nq:])


def _dil_fwd(qr, kr, vb):
    seq = qr.shape[0]
    nq = DIL_Q_FWD
    n_tiles = seq // nq
    assert seq % (nq * max(DIL_DILATIONS)) == 0

    def body(q_ref, k_ref, v_ref, o_ref, lse_ref, m_s, l_s, n_s, bias_s):
        lane = lax.broadcasted_iota(jnp.int32, (nq, LANES), 1)
        bias_s[0], bias_s[1] = _dil_bias(nq)
        for bi, d in enumerate(DIL_DILATIONS):

            def tile(t, carry, d=d, bi=bi):
                first, start, prev = _dil_tile_index(t, d, seq, nq)
                rows, prows = _dil_rows(start, d, nq), _dil_rows(prev, d, BLOCK)
                qst = _stack_heads(q_ref[rows, :], lane).astype(BF16)
                if seq == nq * d:
                    kcat, vcat = k_ref[rows, :].astype(BF16), v_ref[rows, :].astype(BF16)
                    s = _dot(qst, kcat, NT) + bias_s[1, :, BLOCK:]
                else:
                    kcat = jnp.concatenate([k_ref[prows, :], k_ref[rows, :]], axis=0).astype(BF16)
                    vcat = jnp.concatenate([v_ref[prows, :], v_ref[rows, :]], axis=0).astype(BF16)
                    s = _dot(qst, kcat, NT) + bias_s[first]
                m = jnp.max(s, axis=1, keepdims=True)
                p = jnp.exp(s - m)
                l2 = _unstack_heads(jnp.sum(p, axis=1, keepdims=True) + jnp.zeros((2 * nq, LANES), F32), lane)
                m2 = _unstack_heads(m + jnp.zeros((2 * nq, LANES), F32), lane)
                num2 = _unstack_heads(_dot(p.astype(BF16), vcat), lane)
                if bi == 0:
                    m_s[rows, :] = m2
                    l_s[rows, :] = l2
                    n_s[rows, :] = num2
                else:
                    m_old = m_s[rows, :]
                    m_new = jnp.maximum(m_old, m2)
                    a = jnp.exp(m_old - m_new)
                    b = jnp.exp(m2 - m_new)
                    m_s[rows, :] = m_new
                    l_s[rows, :] = a * l_s[rows, :] + b * l2
                    n_s[rows, :] = a * n_s[rows, :] + b * num2
                return carry

            lax.fori_loop(0, n_tiles, tile, 0, unroll=2)
        o_ref[...] = n_s[...] / l_s[...]
        lse_ref[...] = m_s[...] + jnp.log(l_s[...])

    col = lambda off: pl.BlockSpec((seq, LANES), lambda p: (0, p + off))
    return pl.pallas_call(
        body, name="dil_fwd", grid=(4,),
        in_specs=[col(0), col(0), col(0)],
        out_specs=[col(0), pl.BlockSpec((None, seq, LANES), lambda p: (p, 0, 0))],
        out_shape=[jax.ShapeDtypeStruct((seq, 4 * LANES), F32), jax.ShapeDtypeStruct((4, seq, LANES), F32)],
        scratch_shapes=[pltpu.VMEM((seq, LANES), F32)] * 3 + [pltpu.VMEM((2, 2 * nq, BLOCK + nq), F32)],
        compiler_params=_cp(("arbitrary",), VMEM_LIMIT),
    )(qr, kr, vb)


def _post(x, o_a, o_b, gates, w_out, ln_g, ln_b, target):
    seq = x.shape[0]
    tr = 512

    def body(x_ref, oa_ref, ob_ref, g_ref, w_ref, lg_ref, lb_ref, t_ref,
             dz_ref, do_ref, dg_ref, dw_ref, dlg_ref, dlb_ref, loss_ref):
        @pl.when(pl.program_id(0) == 0)
        def _():
            dw_ref[...] = jnp.zeros_like(dw_ref)
            dlg_ref[...] = jnp.zeros_like(dlg_ref)
            dlb_ref[...] = jnp.zeros_like(dlb_ref)
            loss_ref[...] = jnp.zeros_like(loss_ref)

        g = g_ref[...]
        sg = 1.0 / (1.0 + jnp.exp(-g))
        silu = g * sg
        o = jnp.concatenate([oa_ref[...], ob_ref[...]], axis=1)
        mixb = (o * silu).astype(BF16)
        w = w_ref[...]
        z = ALPHA * x_ref[...] + _dot(mixb, w)
        mu = jnp.mean(z, axis=-1, keepdims=True)
        zc = z - mu
        rstd = lax.rsqrt(jnp.mean(zc * zc, axis=-1, keepdims=True) + LN_EPS)
        xhat = zc * rstd
        lg = lg_ref[...]
        err = xhat * lg + lb_ref[...] - t_ref[...]
        loss_ref[...] += jnp.sum(err * err) * (0.5 / D_MODEL)
        dy = err * (1.0 / D_MODEL)
        dlg_ref[...] += jnp.sum(dy * xhat, axis=0, keepdims=True)
        dlb_ref[...] += jnp.sum(dy, axis=0, keepdims=True)
        dxh = dy * lg
        dz = rstd * (dxh - jnp.mean(dxh, axis=-1, keepdims=True) - xhat * jnp.mean(dxh * xhat, axis=-1, keepdims=True))
        dz_ref[...] = dz
        dzb = dz.astype(BF16)
        dmix = _dot(dzb, w, NT)
        do_ref[...] = dmix * silu
        dg_ref[...] = (dmix * o * (sg * (1.0 + g * (1.0 - sg)))).astype(BF16)
        dw_ref[...] += _dot(mixb, dzb, TN)

    row = lambda w: pl.BlockSpec((tr, w), lambda i: (i, 0))
    full = lambda s: pl.BlockSpec(s, lambda i: (0, 0))
    return pl.pallas_call(
        body, name="post", grid=(seq // tr,),
        in_specs=[row(D_MODEL), row(512), row(512), row(D_MODEL), full((D_MODEL, D_MODEL)), full((1, D_MODEL)),
                  full((1, D_MODEL)), row(D_MODEL)],
        out_specs=[row(D_MODEL), row(D_MODEL), row(D_MODEL), full((D_MODEL, D_MODEL)), full((1, D_MODEL)),
                   full((1, D_MODEL)), full((1, LANES))],
        out_shape=[jax.ShapeDtypeStruct((seq, D_MODEL), F32), jax.ShapeDtypeStruct((seq, D_MODEL), F32),
                   jax.ShapeDtypeStruct((seq, D_MODEL), BF16), jax.ShapeDtypeStruct((D_MODEL, D_MODEL), F32),
                   jax.ShapeDtypeStruct((1, D_MODEL), F32), jax.ShapeDtypeStruct((1, D_MODEL), F32),
                   jax.ShapeDtypeStruct((1, LANES), F32)],
        compiler_params=_cp(("arbitrary",), VMEM_LIMIT),
    )(x, o_a, o_b, gates, w_out, ln_g, ln_b, target)


def _mla_bwd(q, k, v, d_o, o, lse):
    seq = q.shape[1]
    tq = 512
    nq = seq // tq

    def body(q_ref, k_ref, v_ref, do_ref, o_ref, lse_ref, dq_ref, dk_ref, dv_ref, d_s, dk_s, dv_s, v_s):
        j = pl.program_id(1)
        lane = lax.broadcasted_iota(jnp.int32, (tq, LANES), 1)
        row = lax.broadcasted_iota(jnp.int32, (tq, tq), 0)
        col = lax.broadcasted_iota(jnp.int32, (tq, tq), 1)

        @pl.when(j == 0)
        def _():
            dq_ref[...] = jnp.zeros_like(dq_ref)

            def rowsum(i, carry):
                rows = pl.ds(pl.multiple_of(i * tq, tq), tq)
                prod = do_ref[rows, :] * o_ref[rows, :]
                for hh in range(2):
                    mine = (lane >= 64) if hh else (lane < 64)
                    total = jnp.sum(jnp.where(mine, prod, 0.0), axis=1, keepdims=True)
                    d_s[hh, rows, :] = total + jnp.zeros((tq, LANES), F32)
                return carry

            lax.fori_loop(0, nq, rowsum, 0)

        dk_s[...] = jnp.zeros_like(dk_s)
        dv_s[...] = jnp.zeros_like(dv_s)
        for hh in range(2):
            v_s[hh] = jnp.where(lane == ONES_LANE[hh], 0.0, v_ref[hh].astype(F32)).astype(BF16)

        def step(i, masked):
            rows = pl.ds(pl.multiple_of(i * tq, tq), tq)
            dob = do_ref[rows, :].astype(BF16)
            for hh in range(2):
                qb, kb, vb = q_ref[hh, rows, :], k_ref[hh], v_s[hh]
                s = _dot(qb, kb, NT)
                p = jnp.exp(s - lse_ref[hh, rows, :][:, :1])
                if masked:
                    p = jnp.where(col <= row, p, 0.0)
                dv_s[hh] += _dot(p.astype(BF16), dob, TN)
                dp = _dot(dob, vb, NT)
                ds = (p * (dp - d_s[hh, rows, :][:, :1])).astype(BF16)
                dk_s[hh] += _dot(ds, qb, TN)
                dq_ref[hh, rows, :] += _dot(ds, kb)

        def full_step(i, carry):
            step(i, False)
            return carry

        step(j, True)
        lax.fori_loop(j + 1, nq, full_step, 0)
        dk_ref[...] = dk_s[...]
        dv_ref[...] = dv_s[...]

    whole = pl.BlockSpec((2, seq, LANES), lambda p, j: (p, 0, 0))
    blk = pl.BlockSpec((2, tq, LANES), lambda p, j: (p, j, 0))
    pair = pl.BlockSpec((seq, LANES), lambda p, j: (0, p))
    shape = jax.ShapeDtypeStruct((MLA_HEADS, seq, LANES), F32)
    return pl.pallas_call(
        body, name="mla_bwd", grid=(MLA_HEADS // 2, nq),
        in_specs=[whole, blk, blk, pair, pair, whole],
        out_specs=[whole, blk, blk], out_shape=[shape] * 3,
        scratch_shapes=[pltpu.VMEM((2, seq, LANES), F32), pltpu.VMEM((2, tq, LANES), F32),
                        pltpu.VMEM((2, tq, LANES), F32), pltpu.VMEM((2, tq, LANES), BF16)],
        compiler_params=_cp(("arbitrary", "arbitrary"), VMEM_LIMIT),
    )(q, k, v, d_o, o, lse)


def _dil_bwd(qr, kr, vb, d_o, o, lse):
    seq = qr.shape[0]
    nq = DIL_Q_BWD
    n_tiles = seq // nq
    chunk = 512

    def body(q_ref, k_ref, v_ref, do_ref, o_ref, lse_ref, dq_ref, dk_ref, dv_ref, d_s, dq_s, dk_s, dv_s, bias_s):
        lane = lax.broadcasted_iota(jnp.int32, (nq, LANES), 1)
        lanec = lax.broadcasted_iota(jnp.int32, (chunk, LANES), 1)
        bias_s[0], bias_s[1] = [b[:nq] for b in _dil_bias(nq)]

        def rowsum(i, carry):
            rows = pl.ds(pl.multiple_of(i * chunk, chunk), chunk)
            prod = do_ref[rows, :] * o_ref[rows, :]
            lo = jnp.sum(jnp.where(lanec < 64, prod, 0.0), axis=1, keepdims=True)
            hi = jnp.sum(jnp.where(lanec >= 64, prod, 0.0), axis=1, keepdims=True)
            d_s[rows, :] = jnp.where(lanec < 64, lo, hi)
            return carry

        lax.fori_loop(0, seq // chunk, rowsum, 0)
        dq_s[...] = jnp.zeros_like(dq_s)
        dk_s[...] = jnp.zeros_like(dk_s)
        dv_s[...] = jnp.zeros_like(dv_s)
        for d in DIL_DILATIONS:

            def tile(t, carry, d=d):
                first, start, prev = _dil_tile_index(t, d, seq, nq)
                rows, prows = _dil_rows(start, d, nq), _dil_rows(prev, d, BLOCK)
                q_t, do_t = q_ref[rows, :], do_ref[rows, :]
                lse_t, d_t = lse_ref[rows, :], d_s[rows, :]
                kcat = jnp.concatenate([k_ref[prows, :], k_ref[rows, :]], axis=0).astype(BF16)
                vcat = jnp.concatenate([v_ref[prows, :], v_ref[rows, :]], axis=0).astype(BF16)
                bias = bias_s[first]
                dq_t = jnp.zeros((nq, LANES), F32)
                dkcat = jnp.zeros((BLOCK + nq, LANES), F32)
                dvcat = jnp.zeros((BLOCK + nq, LANES), F32)
                for hh in range(2):
                    mine = (lane >= 64) if hh else (lane < 64)
                    c0 = 64 * hh
                    qh = jnp.where(mine, q_t, 0.0).astype(BF16)
                    doh = jnp.where(mine, do_t, 0.0).astype(BF16)
                    p = jnp.exp(_dot(qh, kcat, NT) + bias - lse_t[:, c0:c0 + 1])
                    dvcat = dvcat + _dot(p.astype(BF16), doh, TN)
                    dp = _dot(doh, vcat, NT)
                    ds = (p * (dp - d_t[:, c0:c0 + 1])).astype(BF16)
                    dq_t = dq_t + jnp.where(mine, _dot(ds, kcat), 0.0)
                    dkcat = dkcat + _dot(ds, qh, TN)
                dq_s[rows, :] += dq_t
                dk_s[prows, :] += dkcat[:BLOCK]
                dk_s[rows, :] += dkcat[BLOCK:]
                dv_s[prows, :] += dvcat[:BLOCK]
                dv_s[rows, :] += dvcat[BLOCK:]
                return carry

            lax.fori_loop(0, n_tiles, tile, 0, unroll=4)
        dq_ref[...] = dq_s[...].astype(BF16)
        dk_ref[...] = dk_s[...].astype(BF16)
        dv_ref[...] = dv_s[...].astype(BF16)

    col = lambda off: pl.BlockSpec((seq, LANES), lambda p: (0, p + off))
    shape = jax.ShapeDtypeStruct((seq, 4 * LANES), BF16)
    return pl.pallas_call(
        body, name="dil_bwd", grid=(4,),
        in_specs=[col(0), col(0), col(0), col(4), col(0), pl.BlockSpec((None, seq, LANES), lambda p: (p, 0, 0))],
        out_specs=[col(0)] * 3, out_shape=[shape] * 3,
        scratch_shapes=[pltpu.VMEM((seq, LANES), F32)] * 4 + [pltpu.VMEM((2, nq, BLOCK + nq), F32)],
        compiler_params=_cp(("arbitrary",), VMEM_LIMIT),
    )(qr, kr, vb, d_o, o, lse)


def _mla_pre_bwd(cq, ckv, gq, gkv, wuq_e, wukv, ct, st, dq, dk, dv):
    seq = cq.shape[0]
    tr = 512

    def body(cq_ref, ckv_ref, gq_ref, gkv_ref, wuq_ref, wukv_ref, ct_ref, st_ref, dq_ref, dk_ref, dv_ref,
             dcq_ref, dckv_ref, dkr_ref, dwuq_ref, dwukv_ref, dgq_ref, dgkv_ref):
        @pl.when(pl.program_id(0) == 0)
        def _():
            dwuq_ref[...] = jnp.zeros_like(dwuq_ref)
            dwukv_ref[...] = jnp.zeros_like(dwukv_ref)
            dgq_ref[...] = jnp.zeros_like(dgq_ref)
            dgkv_ref[...] = jnp.zeros_like(dgkv_ref)

        lane = lax.broadcasted_iota(jnp.int32, (tr, LANES), 1)
        rope_lanes = jnp.logical_and(lane >= 64, lane < 96)
        ct_, st_ = ct_ref[...], st_ref[...]

        def rope_t(g):
            return ct_ * g + jnp.where(rope_lanes, _mla_rot(st_ * g, lane), 0.0)

        def norm_bwd(c, g, dn, dg_ref):
            r, _ = _rms(c, g)
            u = dn * g
            dg_ref[...] += jnp.sum(dn * c * r, axis=0, keepdims=True)
            return r * u - c * (r * r * r) * jnp.mean(u * c, axis=-1, keepdims=True)

        c, g = cq_ref[...], gq_ref[...]
        _, qn = _rms(c, g)
        dq_all = jnp.concatenate([rope_t(dq_ref[h] * MLA_SCALE) for h in range(MLA_HEADS)], axis=1).astype(BF16)
        dwuq_ref[...] += _dot(qn.astype(BF16), dq_all, TN)
        dcq_ref[...] = norm_bwd(c, g, _dot(dq_all, wuq_ref[...], NT), dgq_ref).astype(BF16)

        c, g = ckv_ref[...], gkv_ref[...]
        _, kvn = _rms(c, g)
        dkpe = jnp.zeros((tr, LANES), F32)
        parts = []
        for h in range(MLA_HEADS):
            dk_h, dv_h = dk_ref[h], dv_ref[h]
            if h % 2 == 0:
                dv_h = pltpu.roll(dv_h, 64, 1)
            parts.append(jnp.where(lane < 64, dk_h, dv_h))
            dkpe = dkpe + jnp.where(rope_lanes, dk_h, 0.0)
        dkv_all = jnp.concatenate(parts, axis=1).astype(BF16)
        dwukv_ref[...] += _dot(kvn.astype(BF16), dkv_all, TN)
        dckv_ref[...] = norm_bwd(c, g, _dot(dkv_all, wukv_ref[...], NT), dgkv_ref).astype(BF16)
        dkr_ref[...] = rope_t(dkpe).astype(BF16)

    row = lambda w: pl.BlockSpec((tr, w), lambda i: (i, 0))
    full = lambda a: pl.BlockSpec(a.shape, lambda i: (0,) * a.ndim)
    head = pl.BlockSpec((MLA_HEADS, tr, LANES), lambda i: (0, i, 0))
    return pl.pallas_call(
        body, name="mla_pre_bwd", grid=(seq // tr,),
        in_specs=[row(Q_LORA), row(KV_LORA), full(gq), full(gkv), full(wuq_e), full(wukv), row(LANES), row(LANES),
                  head, head, head],
        out_specs=[row(Q_LORA), row(KV_LORA), row(LANES), full(wuq_e), full(wukv), full(gq), full(gkv)],
        out_shape=[jax.ShapeDtypeStruct((seq, Q_LORA), BF16), jax.ShapeDtypeStruct((seq, KV_LORA), BF16),
                   jax.ShapeDtypeStruct((seq, LANES), BF16), jax.ShapeDtypeStruct(wuq_e.shape, F32),
                   jax.ShapeDtypeStruct(wukv.shape, F32), jax.ShapeDtypeStruct(gq.shape, F32),
                   jax.ShapeDtypeStruct(gkv.shape, F32)],
        compiler_params=_cp(("arbitrary",), VMEM_LIMIT),
    )(cq, ckv, gq, gkv, wuq_e, wukv, ct, st, dq, dk, dv)


def _in_bwd(dz, dcq, dckv, dgates, dqr, dkr, dvb, dkrope, cd, sd, w_in_p):
    seq = dz.shape[0]
    tr = 512

    def body(dz_ref, dcq_ref, dckv_ref, dg_ref, dqr_ref, dkr_ref, dvb_ref, dkp_ref, cd_ref, sd_ref, w_ref, gx_ref, dh_ref):
        lane = lax.broadcasted_iota(jnp.int32, (tr, LANES), 1)
        rot_lanes = lane % 64 < DIL_ROT
        cd_, sd_ = cd_ref[...], sd_ref[...]

        def rope_t(g):
            return cd_ * g + jnp.where(rot_lanes, _dil_rot(sd_ * g, lane), 0.0)

        dq = [rope_t(dqr_ref[:, LANES * p:LANES * (p + 1)].astype(F32) * DIL_SCALE).astype(BF16) for p in range(4)]
        dk = [rope_t(dkr_ref[:, LANES * p:LANES * (p + 1)].astype(F32)).astype(BF16) for p in range(4)]
        dh = jnp.concatenate([dcq_ref[...], dckv_ref[...], dg_ref[...]] + dq + dk + [dvb_ref[...], dkp_ref[...]], axis=1)
        dh_ref[...] = dh
        gx_ref[...] = ALPHA * dz_ref[...] + _dot(dh, w_ref[...], NT)

    row = lambda w: pl.BlockSpec((tr, w), lambda i: (i, 0))
    return pl.pallas_call(
        body, name="in_bwd", grid=(seq // tr,),
        in_specs=[row(D_MODEL), row(Q_LORA), row(KV_LORA), row(D_MODEL), row(512), row(512), row(512), row(LANES),
                  row(LANES), row(LANES), pl.BlockSpec((D_MODEL, IN_WIDTH_PAD), lambda i: (0, 0))],
        out_specs=[row(D_MODEL), row(IN_WIDTH_PAD)],
        out_shape=[jax.ShapeDtypeStruct((seq, D_MODEL), F32), jax.ShapeDtypeStruct((seq, IN_WIDTH_PAD), BF16)],
        compiler_params=_cp(("arbitrary",), VMEM_LIMIT),
    )(dz, dcq, dckv, dgates, dqr, dkr, dvb, dkrope, cd, sd, w_in_p)


def _dw_in(x, dh):
    seq = dh.shape[0]
    tk = 512
    tn = IN_WIDTH_PAD // 2

    def body(x_ref, dh_ref, o_ref):
        @pl.when(pl.program_id(1) == 0)
        def _():
            o_ref[...] = jnp.zeros_like(o_ref)

        o_ref[...] += _dot(dh_ref[...], x_ref[...].astype(BF16), TN)

    return pl.pallas_call(
        body, name="dw_in", grid=(2, seq // tk),
        in_specs=[pl.BlockSpec((tk, D_MODEL), lambda n, k: (k, 0)), pl.BlockSpec((tk, tn), lambda n, k: (k, n))],
        out_specs=pl.BlockSpec((tn, D_MODEL), lambda n, k: (n, 0)),
        out_shape=jax.ShapeDtypeStruct((IN_WIDTH_PAD, D_MODEL), F32),
        compiler_params=_cp(("arbitrary", "arbitrary"), VMEM_LIMIT),
    )(x, dh)


def _adamw(w, g, m, v, name):
    rows, cols = w.shape
    tc = 256 if cols % 256 == 0 and rows * cols > 2 ** 18 else cols

    def body(w_ref, g_ref, m_ref, v_ref, d_ref, nm_ref, nv_ref):
        g_ = g_ref[...]
        nm = ADAM_B1 * m_ref[...] + (1.0 - ADAM_B1) * g_
        nv = ADAM_B2 * v_ref[...] + (1.0 - ADAM_B2) * jnp.square(g_)
        m_hat = nm / (1.0 - ADAM_B1 ** ADAM_STEP)
        v_hat = nv / (1.0 - ADAM_B2 ** ADAM_STEP)
        d_ref[...] = -ADAM_LR * (m_hat / (jnp.sqrt(v_hat) + ADAM_EPS) + ADAM_WD * w_ref[...])
        nm_ref[...] = nm
        nv_ref[...] = nv

    spec = pl.BlockSpec((rows, tc), lambda i: (0, i))
    return pl.pallas_call(
        body, name=name, grid=(cols // tc,), in_specs=[spec] * 4, out_specs=[spec] * 3,
        out_shape=[jax.ShapeDtypeStruct(w.shape, F32)] * 3, compiler_params=_cp(("arbitrary",)),
    )(w, g, m, v)


def _pad_row(v):
    return jnp.pad(v.reshape(1, -1), ((0, 0), (0, D_MODEL - v.shape[-1])))


def _local_step(x2, target, w_in_p, w_uq_f, wukv_f, w_out_f, q_norm_g, kv_norm_g, ln_g, ln_b):
    seq = x2.shape[0]
    wuq_e = jnp.pad(w_uq_f.reshape(Q_LORA, MLA_HEADS, 96), ((0, 0), (0, 0), (0, 32))).reshape(Q_LORA, MLA_HEADS * LANES)
    ct, st, cd, sd = _rope_tables(seq)
    gq = q_norm_g.reshape(1, Q_LORA)
    gkv = kv_norm_g.reshape(1, KV_LORA)

    cq, ckv, gates, qr, krot, vb, q_e, k_e, v_e = _proj(x2, w_in_p, gq, gkv, wuq_e, wukv_f, ct, st, cd, sd)
    o_a, lse_a = _mla_fwd(q_e, k_e, v_e)
    o_b, lse_b = _dil_fwd(qr, krot, vb)

    dz, d_o, d_gates, dw_out, dln_g, dln_b, loss_part = _post(
        x2, o_a, o_b, gates, w_out_f, ln_g.reshape(1, D_MODEL), ln_b.reshape(1, D_MODEL), target)
    dq_e, dk_e, dv_e = _mla_bwd(q_e, k_e, v_e, d_o, o_a, lse_a)
    dqr, dkr, dvb = _dil_bwd(qr, krot, vb, d_o, o_b, lse_b)
    dcq, dckv, dkrope, dwuq_e, dwukv, dgq, dgkv = _mla_pre_bwd(cq, ckv, gq, gkv, wuq_e, wukv_f, ct, st, dq_e, dk_e, dv_e)
    grad_x, dh = _in_bwd(dz, dcq, dckv, d_gates, dqr, dkr, dvb, dkrope, cd, sd, w_in_p)
    dw_in = _unpermute_dw_in_t(_dw_in(x2, dh))
    dw_uq = dwuq_e.reshape(Q_LORA, MLA_HEADS, LANES)[:, :, :96].reshape(Q_LORA, MLA_HEADS * 96)
    return loss_part, grad_x, dw_in, dw_uq, dwukv, dw_out, dgq, dgkv, dln_g, dln_b


def kernel(x, w_in, q_norm_g, kv_norm_g, w_uq, w_ukv, w_out, ln_g, ln_b, loss_target, m_w_in, m_q_norm_g, m_kv_norm_g, m_w_uq, m_w_ukv, m_w_out, m_ln_g, m_ln_b, v_w_in, v_q_norm_g, v_kv_norm_g, v_w_uq, v_w_ukv, v_w_out, v_ln_g, v_ln_b):
    seq = x.shape[1]
    x2 = x.reshape(seq, D_MODEL)
    target = loss_target.reshape(seq, D_MODEL)

    g_w_in, g_w_uq, g_w_ukv, g_w_out = _all_gather_weights([w_in, w_uq, w_ukv, w_out])
    by_cols = lambda g: jnp.concatenate([g[j] for j in range(N_SHARD)], axis=1)
    loss_part, grad_x, dw_in, dw_uq, dwukv, dw_out, dgq, dgkv, dln_g, dln_b = _local_step(
        x2, target, _permute_w_in_shards(g_w_in), by_cols(g_w_uq), by_cols(g_w_ukv), g_w_out.reshape(D_MODEL, D_MODEL),
        q_norm_g, kv_norm_g, ln_g, ln_b)

    to_shards = lambda d: d.reshape(d.shape[0], N_SHARD, d.shape[1] // N_SHARD).transpose(1, 0, 2)
    grads = [dw_in.reshape(N_SHARD, 808, D_MODEL), to_shards(dw_uq), to_shards(dwukv),
             dw_out.reshape(N_SHARD, 256, D_MODEL)]
    small = jnp.concatenate([_pad_row(dgq), _pad_row(dgkv), dln_g, dln_b, _pad_row(loss_part),
                             jnp.zeros((3, D_MODEL), F32)], axis=0)
    *chip_sums, smalls = _reduce_over_sibling(grads, small)
    g_in_t, g_uq, g_ukv, g_out = _reduce_over_chips(chip_sums)
    g_in = g_in_t.T
    small_sum = _sum_smalls(smalls)
    loss = small_sum[4, 0]

    big = [[o.T for o in _adamw(w.T, g.T, m.T, v.T, name)] for w, g, m, v, name in (
        (w_in, g_in, m_w_in, v_w_in, "adamw_w_in"), (w_uq, g_uq, m_w_uq, v_w_uq, "adamw_w_uq"))]
    big += [_adamw(w, g, m, v, name) for w, g, m, v, name in (
        (w_ukv, g_ukv, m_w_ukv, v_w_ukv, "adamw_w_ukv"), (w_out, g_out, m_w_out, v_w_out, "adamw_w_out"))]
    vec = lambda a, b, c_, d: jnp.concatenate([_pad_row(a), _pad_row(b), _pad_row(c_), _pad_row(d),
                                               jnp.zeros((4, D_MODEL), F32)], axis=0)
    sw = vec(q_norm_g, kv_norm_g, ln_g, ln_b)
    sm = vec(m_q_norm_g, m_kv_norm_g, m_ln_g, m_ln_b)
    sv = vec(v_q_norm_g, v_kv_norm_g, v_ln_g, v_ln_b)
    sg = jnp.concatenate([small_sum[:4], jnp.zeros((4, D_MODEL), F32)], axis=0)
    s_delta, s_m, s_v = _adamw(sw, sg, sm, sv, "adamw_vectors")

    def vectors(a):
        return [a[0, :Q_LORA], a[1, :KV_LORA], a[2], a[3]]

    def ordered(bigs, smalls_):
        return [bigs[0], smalls_[0], smalls_[1], bigs[1], bigs[2], bigs[3], smalls_[2], smalls_[3]]

    grads_out = ordered([g_in, g_uq, g_ukv, g_out], vectors(small_sum))
    deltas = ordered([b[0] for b in big], vectors(s_delta))
    new_m = ordered([b[1] for b in big], vectors(s_m))
    new_v = ordered([b[2] for b in big], vectors(s_v))
    return (loss, grad_x.reshape(x.shape), *grads_out, *deltas, *new_m, *new_v)
```

```python
import functools

import jax
import jax.numpy as jnp
import numpy as np
from jax import lax
from jax.experimental import pallas as pl
from jax.experimental.pallas import tpu as pltpu

F32 = jnp.float32
BF16 = jnp.bfloat16

D_MODEL = 1024
ROPE_THETA = 500000.0
BLOCK = 128
NEG = -1e30
RMS_EPS = 1e-6
LN_EPS = 1e-5

MLA_HEADS = 8
MLA_NOPE = 64
MLA_ROPE = 32
Q_LORA = 384
KV_LORA = 256
DIL_HEADS = 8
DIL_HEAD_DIM = 64
DIL_ROT = 16
DIL_DILATIONS = (1, 4, 16)
IN_WIDTH = 3232
IN_WIDTH_PAD = 3328
ONES_LANE = (64, 0)
MLA_SCALE = (MLA_NOPE + MLA_ROPE) ** -0.5
DIL_SCALE = DIL_HEAD_DIM ** -0.5
ALPHA = 2.0 ** 0.25

ADAM_LR = 0.001
ADAM_B1 = 0.9
ADAM_B2 = 0.999
ADAM_EPS = 1e-08
ADAM_WD = 0.01
ADAM_STEP = 10

N_SHARD = 4
SHARD_SHAPES = ((1024, 808), (384, 192), (256, 256), (256, 1024))
GRAD_SHAPES = ((808, 1024), (384, 192), (256, 256), (256, 1024))
GRAD_SPLIT_COLS = (True, False, False, False)
ROW_CHUNK = 64
LANES = 128
VMEM_LIMIT = 56 * 1024 * 1024
MESH = pl.DeviceIdType.MESH

NT = (((1,), (1,)), ((), ()))
TN = (((0,), (0,)), ((), ()))


def _cp(sem=None, vmem=None):
    return pltpu.CompilerParams(dimension_semantics=sem, vmem_limit_bytes=vmem)


def _dot(a, b, dims=None):
    if dims is None:
        return jnp.dot(a, b, preferred_element_type=F32)
    return lax.dot_general(a, b, dims, preferred_element_type=F32)


def _rope_tables(seq):
    f32 = np.float32
    pos = np.arange(seq, dtype=f32)[:, None]
    one, zero = np.ones((seq, 64), f32), np.zeros((seq, 64), f32)

    def cos_sin(dim):
        inv = np.power(f32(ROPE_THETA), -np.arange(0, dim, 2, dtype=f32) / f32(dim)).astype(f32)
        ang = (pos * inv[None, :]).astype(f32)
        return np.cos(ang).astype(f32), np.sin(ang).astype(f32)

    cos, sin = cos_sin(MLA_ROPE)
    ct = np.concatenate([one, cos, cos, zero[:, :32]], axis=1)
    st = np.concatenate([zero, -sin, sin, zero[:, :32]], axis=1)
    cos, sin = cos_sin(DIL_ROT)
    cd = np.concatenate([cos, cos, one[:, :48]], axis=1)
    sd = np.concatenate([-sin, sin, zero[:, :48]], axis=1)
    return tuple(jnp.asarray(t) for t in (ct, st, np.tile(cd, (1, 2)), np.tile(sd, (1, 2))))


W_IN_ORDER = ((0, 640), (672, 1184), (2720, 3232), (1184, 2720), None, (640, 672))


def _permute_w_in(w):
    z = jnp.zeros((w.shape[0], 64), w.dtype)
    parts = [z if r is None else w[:, r[0]:r[1]] for r in W_IN_ORDER]
    return jnp.concatenate(parts + [z[:, :32]], axis=1)


def _permute_w_in_shards(g):
    width = g.shape[2]
    z = jnp.zeros((g.shape[1], 64), g.dtype)
    parts = []
    for r in W_IN_ORDER:
        if r is None:
            parts.append(z)
            continue
        for j in range(N_SHARD):
            lo, hi = max(r[0], width * j), min(r[1], width * (j + 1))
            if lo < hi:
                parts.append(g[j, :, lo - width * j:hi - width * j])
    return jnp.concatenate(parts + [z[:, :32]], axis=1)


def _unpermute_dw_in_t(dw_t):
    return jnp.concatenate([dw_t[0:640], dw_t[3264:3296], dw_t[640:1152], dw_t[1664:3200], dw_t[1152:1664]], axis=0)


def _position():
    return lax.axis_index("x"), lax.axis_index("y"), lax.axis_index("c")


def _halves(c, rows):
    hr = rows // 2
    return pl.ds(pl.multiple_of(c * hr, 8), hr), pl.ds(pl.multiple_of((1 - c) * hr, 8), hr)


def _for_row_chunks(rows, fn):
    def step(i, carry):
        fn(pl.multiple_of(i * ROW_CHUNK, ROW_CHUNK))
        return carry

    lax.fori_loop(0, rows // ROW_CHUNK, step, 0)


def _all_gather_weights(shards):
    n = len(shards)

    def body(*refs):
        ins, outs = refs[:n], refs[n:2 * n]
        send_sems, recv_sems = refs[2 * n:]
        x, y, c = _position()
        me = 2 * x + y
        chips = [(1 - x, y), (x, 1 - y), (1 - x, 1 - y)]
        for a in range(n):
            def cast(r, a=a):
                outs[a][me, pl.ds(r, ROW_CHUNK), :] = ins[a][pl.ds(r, ROW_CHUNK), :].astype(BF16)

            _for_row_chunks(SHARD_SHAPES[a][0], cast)

        def copy(k, a, slot, rows, to):
            ref = outs[a].at[slot, rows]
            return pltpu.make_async_remote_copy(
                src_ref=ref, dst_ref=ref, send_sem=send_sems.at[k * n + a], recv_sem=recv_sems.at[k * n + a],
                device_id=to, device_id_type=MESH)

        half = [_halves(c, SHARD_SHAPES[a][0])[0] for a in range(n)]
        other = [_halves(c, SHARD_SHAPES[a][0])[1] for a in range(n)]
        first = [copy(k, a, me, half[a], (px, py, c)) for k, (px, py) in enumerate(chips) for a in range(n)]
        for cp in first:
            cp.start()
        passed = []
        for k, (px, py) in enumerate(chips):
            for a in range(n):
                copy(k, a, 2 * px + py, half[a], (x, y, c)).wait_recv()
                cp = copy(3 + k, a, 2 * px + py, half[a], (x, y, 1 - c))
                cp.start()
                passed.append(cp)
        for k, (px, py) in enumerate(chips):
            for a in range(n):
                copy(3 + k, a, 2 * px + py, other[a], (x, y, c)).wait_recv()
        for cp in first + passed:
            cp.wait_send()

    vmem = pl.BlockSpec(memory_space=pltpu.VMEM)
    return pl.pallas_call(
        body, name="all_gather_weights",
        out_shape=[jax.ShapeDtypeStruct((N_SHARD,) + s, BF16) for s in SHARD_SHAPES],
        in_specs=[vmem] * n, out_specs=[vmem] * n,
        scratch_shapes=[pltpu.SemaphoreType.DMA((6 * n,)), pltpu.SemaphoreType.DMA((6 * n,))],
        compiler_params=_cp(None, VMEM_LIMIT),
    )(*shards)


def _grad_half_shape(a):
    rows, cols = GRAD_SHAPES[a]
    return (rows, cols // 2) if GRAD_SPLIT_COLS[a] else (rows // 2, cols)


def _grad_half(a, c):
    rows, cols = GRAD_SHAPES[a]
    if GRAD_SPLIT_COLS[a]:
        return slice(None), pl.ds(pl.multiple_of(c * (cols // 2), LANES), cols // 2)
    return pl.ds(pl.multiple_of(c * (rows // 2), ROW_CHUNK), rows // 2), slice(None)


def _grad_chunks(a, c):
    rows, cols = GRAD_SHAPES[a]
    if GRAD_SPLIT_COLS[a]:
        return [((slice(None), pl.ds(c0, LANES)),
                 (slice(None), pl.ds(pl.multiple_of(c * (cols // 2) + c0, LANES), LANES)))
                for c0 in range(0, cols // 2, LANES)]
    return [((pl.ds(r0, ROW_CHUNK), slice(None)),
             (pl.ds(pl.multiple_of(c * (rows // 2) + r0, ROW_CHUNK), ROW_CHUNK), slice(None)))
            for r0 in range(0, rows // 2, ROW_CHUNK)]


def _reduce_over_sibling(grads, small):
    n = len(grads)

    def body(*refs):
        g_hbm, sm = refs[:n], refs[n]
        sums, smalls = refs[n + 1:2 * n + 1], refs[2 * n + 1]
        stage, got = refs[2 * n + 2:3 * n + 2], refs[3 * n + 2:4 * n + 2]
        send_sems, recv_sems, local_sems = refs[4 * n + 2:]
        x, y, c = _position()
        me = 4 * x + 2 * y + c
        loads = [pltpu.make_async_copy(g_hbm[a], stage[a], local_sems.at[a]) for a in range(n)]
        for ld in loads:
            ld.start()
        smalls[me] = sm[...]
        sends = []
        for rel in range(1, 8):
            px = 1 - x if rel // 4 else x
            py = 1 - y if (rel // 2) % 2 else y
            pc = 1 - c if rel % 2 else c
            cp = pltpu.make_async_remote_copy(
                src_ref=sm, dst_ref=smalls.at[me], send_sem=send_sems.at[n + rel], recv_sem=recv_sems.at[n + rel],
                device_id=(px, py, pc), device_id_type=MESH)
            cp.start()
            sends.append((cp, 4 * px + 2 * py + pc))
        swaps = []
        for a in range(n):
            loads[a].wait()
            cp = pltpu.make_async_remote_copy(
                src_ref=stage[a].at[(slice(None),) + _grad_half(a, 1 - c)], dst_ref=got[a], send_sem=send_sems.at[a], recv_sem=recv_sems.at[a],
                device_id=(x, y, 1 - c), device_id_type=MESH)
            cp.start()
            swaps.append(cp)
        for a in range(n):
            swaps[a].wait_recv()
            for k in range(N_SHARD):
                for in_half, in_whole in _grad_chunks(a, c):
                    pair = stage[a][(k,) + in_whole] + got[a][(k,) + in_half]
                    sums[a][(k,) + in_half] = pair.astype(BF16)
        for rel, (cp, peer) in enumerate(sends, start=1):
            pltpu.make_async_remote_copy(
                src_ref=sm, dst_ref=smalls.at[peer], send_sem=send_sems.at[n + rel], recv_sem=recv_sems.at[n + rel],
                device_id=(x, y, c), device_id_type=MESH).wait_recv()
        for cp in swaps:
            cp.wait_send()
        for cp, _ in sends:
            cp.wait_send()

    vmem = pl.BlockSpec(memory_space=pltpu.VMEM)
    half = [(N_SHARD,) + _grad_half_shape(a) for a in range(n)]
    return pl.pallas_call(
        body, name="reduce_over_sibling",
        out_shape=[jax.ShapeDtypeStruct(s, BF16) for s in half] + [jax.ShapeDtypeStruct((8,) + small.shape, F32)],
        in_specs=[pl.BlockSpec(memory_space=pl.ANY)] * n + [vmem], out_specs=[vmem] * (n + 1),
        scratch_shapes=[pltpu.VMEM((N_SHARD,) + s, F32) for s in GRAD_SHAPES] + [pltpu.VMEM(s, F32) for s in half]
        + [pltpu.SemaphoreType.DMA((n + 8,)), pltpu.SemaphoreType.DMA((n + 8,)), pltpu.SemaphoreType.DMA((n,))],
        compiler_params=_cp(None, VMEM_LIMIT),
    )(*grads, small)


def _reduce_over_chips(sums):
    n = len(sums)

    def body(*refs):
        h, outs, got = refs[:n], refs[n:2 * n], refs[2 * n:3 * n]
        send_sems, recv_sems = refs[3 * n:]
        x, y, c = _position()
        me = 2 * x + y
        chips = [(1 - x, y), (x, 1 - y), (1 - x, 1 - y)]
        sends = []
        for k, (px, py) in enumerate(chips):
            for a in range(n):
                cp = pltpu.make_async_remote_copy(
                    src_ref=h[a].at[2 * px + py], dst_ref=got[a].at[k], send_sem=send_sems.at[k * n + a],
                    recv_sem=recv_sems.at[k * n + a], device_id=(px, py, c), device_id_type=MESH)
                cp.start()
                sends.append(cp)
        for cp in sends:
            cp.wait_recv()
        joins = []
        for a in range(n):
            for in_half, in_whole in _grad_chunks(a, c):
                total = h[a][(me,) + in_half].astype(F32)
                for k in range(3):
                    total = total + got[a][(k,) + in_half].astype(F32)
                outs[a][in_whole] = total
            half = outs[a].at[_grad_half(a, c)]
            cp = pltpu.make_async_remote_copy(
                src_ref=half, dst_ref=half, send_sem=send_sems.at[3 * n + a],
                recv_sem=recv_sems.at[3 * n + a], device_id=(x, y, 1 - c), device_id_type=MESH)
            cp.start()
            joins.append(cp)
        for a in range(n):
            other = outs[a].at[_grad_half(a, 1 - c)]
            pltpu.make_async_remote_copy(
                src_ref=other, dst_ref=other, send_sem=send_sems.at[3 * n + a],
                recv_sem=recv_sems.at[3 * n + a], device_id=(x, y, c), device_id_type=MESH).wait_recv()
        for cp in sends + joins:
            cp.wait_send()

    vmem = pl.BlockSpec(memory_space=pltpu.VMEM)
    return pl.pallas_call(
        body, name="reduce_over_chips",
        out_shape=[jax.ShapeDtypeStruct(s, F32) for s in GRAD_SHAPES],
        in_specs=[vmem] * n, out_specs=[vmem] * n,
        scratch_shapes=[pltpu.VMEM((3,) + _grad_half_shape(a), BF16) for a in range(n)]
        + [pltpu.SemaphoreType.DMA((4 * n,)), pltpu.SemaphoreType.DMA((4 * n,))],
        compiler_params=_cp(None, VMEM_LIMIT),
    )(*sums)


def _sum_smalls(smalls):
    def body(s, o):
        acc = s[0]
        for d in range(1, 8):
            acc = acc + s[d]
        o[...] = acc

    return pl.pallas_call(body, name="sum_smalls", out_shape=jax.ShapeDtypeStruct(smalls.shape[1:], F32))(smalls)


def _proj(x, w_in_p, gq, gkv, wuq_e, wukv, ct, st, cd, sd):
    seq = x.shape[0]
    tr = 512

    def body(x_ref, w_ref, gq_ref, gkv_ref, wuq_ref, wukv_ref, ct_ref, st_ref, cd_ref, sd_ref,
             cq_ref, ckv_ref, g_ref, qr_ref, kr_ref, vb_ref, q_out, k_out, v_out):
        lane = lax.broadcasted_iota(jnp.int32, (tr, LANES), 1)
        xb = x_ref[...].astype(BF16)
        cq = _dot(xb, w_ref[:, 0:384])
        ckv = _dot(xb, w_ref[:, 384:640])
        cq_ref[...] = cq
        ckv_ref[...] = ckv
        g_ref[...] = _dot(xb, w_ref[:, 640:1664])

        cd_, sd_ = cd_ref[...], sd_ref[...]
        qb = _dot(xb, w_ref[:, 1664:2176])
        kb = _dot(xb, w_ref[:, 2176:2688])
        for p in range(4):
            cols = slice(LANES * p, LANES * (p + 1))
            t = qb[:, cols]
            qr_ref[:, cols] = (t * cd_ + _dil_rot(t, lane) * sd_) * DIL_SCALE
            t = kb[:, cols]
            kr_ref[:, cols] = t * cd_ + _dil_rot(t, lane) * sd_
        vb_ref[...] = _dot(xb, w_ref[:, 2688:3200])

        ct_, st_ = ct_ref[...], st_ref[...]

        def rope(t):
            return t * ct_ + _mla_rot(t, lane) * st_

        _, qn = _rms(cq, gq_ref[...])
        q_all = _dot(qn.astype(BF16), wuq_ref[...])
        for h in range(MLA_HEADS):
            q_out[h] = (rope(q_all[:, LANES * h:LANES * (h + 1)]) * MLA_SCALE).astype(BF16)
        _, kvn = _rms(ckv, gkv_ref[...])
        kv_all = _dot(kvn.astype(BF16), wukv_ref[...])
        kpe = rope(_dot(xb, w_ref[:, 3200:3328]))
        for h in range(MLA_HEADS):
            kv_h = kv_all[:, LANES * h:LANES * (h + 1)]
            k_out[h] = jnp.where(lane < 64, kv_h, kpe).astype(BF16)
            if h % 2:
                v = jnp.where(lane >= 64, kv_h, 0.0)
            else:
                v = jnp.where(lane < 64, pltpu.roll(kv_h, 64, 1), 0.0)
            v_out[h] = jnp.where(lane == ONES_LANE[h % 2], 1.0, v).astype(BF16)

    row = lambda w: pl.BlockSpec((tr, w), lambda i: (i, 0))
    full = lambda a: pl.BlockSpec(a.shape, lambda i: (0,) * a.ndim)
    head = pl.BlockSpec((MLA_HEADS, tr, LANES), lambda i: (0, i, 0))
    widths = (Q_LORA, KV_LORA, D_MODEL, 512, 512, 512)
    return pl.pallas_call(
        body, name="proj", grid=(seq // tr,),
        in_specs=[row(D_MODEL), full(w_in_p), full(gq), full(gkv), full(wuq_e), full(wukv)] + [row(LANES)] * 4,
        out_specs=[row(w) for w in widths] + [head] * 3,
        out_shape=[jax.ShapeDtypeStruct((seq, w), F32) for w in widths]
        + [jax.ShapeDtypeStruct((MLA_HEADS, seq, LANES), BF16)] * 3,
        compiler_params=_cp(("arbitrary",), VMEM_LIMIT),
    )(x, w_in_p, gq, gkv, wuq_e, wukv, ct, st, cd, sd)


def _mla_rot(t, lane):
    return jnp.where(lane < 80, pltpu.roll(t, 112, 1), pltpu.roll(t, 16, 1))


def _dil_rot(t, lane):
    return jnp.where(lane % 64 < 8, pltpu.roll(t, 120, 1), pltpu.roll(t, 8, 1))


def _rms(c, g):
    r = lax.rsqrt(jnp.mean(c * c, axis=-1, keepdims=True) + RMS_EPS)
    return r, c * r * g


def _mla_fwd(q, k, v):
    seq = q.shape[1]
    tq = 512
    nq = seq // tq

    def body(q_ref, k_ref, v_ref, o_ref, lse_ref, m_s, acc_s, s_buf):
        i = pl.program_id(1)
        row = lax.broadcasted_iota(jnp.int32, (tq, tq), 0)
        col = lax.broadcasted_iota(jnp.int32, (tq, tq), 1)
        lane = lax.broadcasted_iota(jnp.int32, (tq, LANES), 1)
        m_s[...] = jnp.full((2, tq, LANES), NEG, F32)
        acc_s[...] = jnp.zeros((2, tq, LANES), F32)

        def block(j):
            return pl.ds(pl.multiple_of(j * tq, tq), tq)

        def scores(hh, j):
            return _dot(q_ref[hh], k_ref[hh, block(j), :], NT)

        def consume(hh, j, s):
            m_prev = m_s[hh]
            m_new = jnp.maximum(m_prev, jnp.max(s, axis=1, keepdims=True))
            p = jnp.exp(s - m_new[:, :1])
            acc_s[hh] = jnp.exp(m_prev - m_new) * acc_s[hh] + _dot(p.astype(BF16), v_ref[hh, block(j), :])
            m_s[hh] = m_new

        for hh in range(2):
            s_buf[0, hh] = scores(hh, 0)

        def full_step(j, carry):
            slot = j & 1
            for hh in range(2):
                s = s_buf[slot, hh]
                s_buf[1 - slot, hh] = scores(hh, j + 1)
                consume(hh, j, s)
            return carry

        lax.fori_loop(0, i, full_step, 0)
        total = jnp.zeros((tq, LANES), F32)
        for hh in range(2):
            consume(hh, i, jnp.where(col <= row, s_buf[i & 1, hh], NEG))
            acc = acc_s[hh]
            l = acc[:, ONES_LANE[hh]:ONES_LANE[hh] + 1]
            mine = (lane >= 64) if hh else (lane < 64)
            total = total + jnp.where(mine, acc / l, 0.0)
            lse_ref[hh] = m_s[hh] + jnp.log(l)
        o_ref[...] = total

    kv_spec = pl.BlockSpec((2, seq, LANES), lambda p, i: (p, 0, 0))
    return pl.pallas_call(
        body, name="mla_fwd", grid=(MLA_HEADS // 2, nq),
        in_specs=[pl.BlockSpec((2, tq, LANES), lambda p, i: (p, i, 0)), kv_spec, kv_spec],
        out_specs=[pl.BlockSpec((tq, LANES), lambda p, i: (i, p)), pl.BlockSpec((2, tq, LANES), lambda p, i: (p, i, 0))],
        out_shape=[jax.ShapeDtypeStruct((seq, 4 * LANES), F32), jax.ShapeDtypeStruct((MLA_HEADS, seq, LANES), F32)],
        scratch_shapes=[pltpu.VMEM((2, tq, LANES), F32), pltpu.VMEM((2, tq, LANES), F32),
                        pltpu.VMEM((2, 2, tq, tq), F32)],
        compiler_params=_cp(("arbitrary", "arbitrary"), VMEM_LIMIT),
    )(q, k, v)


DIL_Q_FWD = 2 * BLOCK
DIL_Q_BWD = BLOCK


def _dil_tile_index(t, d, seq, nq):
    per_class = seq // (nq * d)
    shift = per_class.bit_length() - 1
    r = t >> shift
    n = t & (per_class - 1)
    start = r + (nq * d) * n
    prev = jnp.maximum(start - BLOCK * d, r)
    if d == 1:
        start = pl.multiple_of(start, nq)
        prev = pl.multiple_of(prev, BLOCK)
    return (n == 0).astype(jnp.int32), start, prev


def _dil_rows(start, d, size):
    return pl.ds(start, size) if d == 1 else pl.ds(start, size, stride=d)


def _dil_bias(nq):
    i = lax.broadcasted_iota(jnp.int32, (2 * nq, BLOCK + nq), 0) % nq
    j = lax.broadcasted_iota(jnp.int32, (2 * nq, BLOCK + nq), 1)
    band = (j >= i) & (j <= i + BLOCK)
    return jnp.where(band, 0.0, NEG), jnp.where(band & (j >= BLOCK), 0.0, NEG)


def _stack_heads(t, lane):
    return jnp.concatenate([jnp.where(lane < 64, t, 0.0), jnp.where(lane >= 64, t, 0.0)], axis=0)


def _unstack_heads(t, lane):
    nq = t.shape[0] // 2
    return jnp.where(lane < 64, t[:nq], t[nq:])


def _dil_fwd(qr, kr, vb):
    seq = qr.shape[0]
    nq = DIL_Q_FWD
    n_tiles = seq // nq
    assert seq % (nq * max(DIL_DILATIONS)) == 0

    def body(q_ref, k_ref, v_ref, o_ref, lse_ref, m_s, l_s, n_s, bias_s):
        lane = lax.broadcasted_iota(jnp.int32, (nq, LANES), 1)
        bias_s[0], bias_s[1] = _dil_bias(nq)
        for bi, d in enumerate(DIL_DILATIONS):

            def tile(t, carry, d=d, bi=bi):
                first, start, prev = _dil_tile_index(t, d, seq, nq)
                rows, prows = _dil_rows(start, d, nq), _dil_rows(prev, d, BLOCK)
                qst = _stack_heads(q_ref[rows, :], lane).astype(BF16)
                if seq == nq * d:
                    kcat, vcat = k_ref[rows, :].astype(BF16), v_ref[rows, :].astype(BF16)
                    s = _dot(qst, kcat, NT) + bias_s[1, :, BLOCK:]
                else:
                    kcat = jnp.concatenate([k_ref[prows, :], k_ref[rows, :]], axis=0).astype(BF16)
                    vcat = jnp.concatenate([v_ref[prows, :], v_ref[rows, :]], axis=0).astype(BF16)
                    s = _dot(qst, kcat, NT) + bias_s[first]
                m = jnp.max(s, axis=1, keepdims=True)
                p = jnp.exp(s - m)
                l2 = _unstack_heads(jnp.sum(p, axis=1, keepdims=True) + jnp.zeros((2 * nq, LANES), F32), lane)
                m2 = _unstack_heads(m + jnp.zeros((2 * nq, LANES), F32), lane)
                num2 = _unstack_heads(_dot(p.astype(BF16), vcat), lane)
                if bi == 0:
                    m_s[rows, :] = m2
                    l_s[rows, :] = l2
                    n_s[rows, :] = num2
                else:
                    m_old = m_s[rows, :]
                    m_new = jnp.maximum(m_old, m2)
                    a = jnp.exp(m_old - m_new)
                    b = jnp.exp(m2 - m_new)
                    m_s[rows, :] = m_new
                    l_s[rows, :] = a * l_s[rows, :] + b * l2
                    n_s[rows, :] = a * n_s[rows, :] + b * num2
                return carry

            lax.fori_loop(0, n_tiles, tile, 0, unroll=2)
        o_ref[...] = n_s[...] / l_s[...]
        lse_ref[...] = m_s[...] + jnp.log(l_s[...])

    col = lambda off: pl.BlockSpec((seq, LANES), lambda p: (0, p + off))
    return pl.pallas_call(
        body, name="dil_fwd", grid=(4,),
        in_specs=[col(0), col(0), col(0)],
        out_specs=[col(0), pl.BlockSpec((None, seq, LANES), lambda p: (p, 0, 0))],
        out_shape=[jax.ShapeDtypeStruct((seq, 4 * LANES), F32), jax.ShapeDtypeStruct((4, seq, LANES), F32)],
        scratch_shapes=[pltpu.VMEM((seq, LANES), F32)] * 3 + [pltpu.VMEM((2, 2 * nq, BLOCK + nq), F32)],
        compiler_params=_cp(("arbitrary",), VMEM_LIMIT),
    )(qr, kr, vb)


def _post(x, o_a, o_b, gates, w_out, ln_g, ln_b, target):
    seq = x.shape[0]
    tr = 512

    def body(x_ref, oa_ref, ob_ref, g_ref, w_ref, lg_ref, lb_ref, t_ref,
             dz_ref, do_ref, dg_ref, dw_ref, dlg_ref, dlb_ref, loss_ref):
        @pl.when(pl.program_id(0) == 0)
        def _():
            dw_ref[...] = jnp.zeros_like(dw_ref)
            dlg_ref[...] = jnp.zeros_like(dlg_ref)
            dlb_ref[...] = jnp.zeros_like(dlb_ref)
            loss_ref[...] = jnp.zeros_like(loss_ref)

        g = g_ref[...]
        sg = 1.0 / (1.0 + jnp.exp(-g))
        silu = g * sg
        o = jnp.concatenate([oa_ref[...], ob_ref[...]], axis=1)
        mixb = (o * silu).astype(BF16)
        w = w_ref[...]
        z = ALPHA * x_ref[...] + _dot(mixb, w)
        mu = jnp.mean(z, axis=-1, keepdims=True)
        zc = z - mu
        rstd = lax.rsqrt(jnp.mean(zc * zc, axis=-1, keepdims=True) + LN_EPS)
        xhat = zc * rstd
        lg = lg_ref[...]
        err = xhat * lg + lb_ref[...] - t_ref[...]
        loss_ref[...] += jnp.sum(err * err) * (0.5 / D_MODEL)
        dy = err * (1.0 / D_MODEL)
        dlg_ref[...] += jnp.sum(dy * xhat, axis=0, keepdims=True)
        dlb_ref[...] += jnp.sum(dy, axis=0, keepdims=True)
        dxh = dy * lg
        dz = rstd * (dxh - jnp.mean(dxh, axis=-1, keepdims=True) - xhat * jnp.mean(dxh * xhat, axis=-1, keepdims=True))
        dz_ref[...] = dz
        dzb = dz.astype(BF16)
        dmix = _dot(dzb, w, NT)
        do_ref[...] = dmix * silu
        dg_ref[...] = (dmix * o * (sg * (1.0 + g * (1.0 - sg)))).astype(BF16)
        dw_ref[...] += _dot(mixb, dzb, TN)

    row = lambda w: pl.BlockSpec((tr, w), lambda i: (i, 0))
    full = lambda s: pl.BlockSpec(s, lambda i: (0, 0))
    return pl.pallas_call(
        body, name="post", grid=(seq // tr,),
        in_specs=[row(D_MODEL), row(512), row(512), row(D_MODEL), full((D_MODEL, D_MODEL)), full((1, D_MODEL)),
                  full((1, D_MODEL)), row(D_MODEL)],
        out_specs=[row(D_MODEL), row(D_MODEL), row(D_MODEL), full((D_MODEL, D_MODEL)), full((1, D_MODEL)),
                   full((1, D_MODEL)), full((1, LANES))],
        out_shape=[jax.ShapeDtypeStruct((seq, D_MODEL), F32), jax.ShapeDtypeStruct((seq, D_MODEL), F32),
                   jax.ShapeDtypeStruct((seq, D_MODEL), BF16), jax.ShapeDtypeStruct((D_MODEL, D_MODEL), F32),
                   jax.ShapeDtypeStruct((1, D_MODEL), F32), jax.ShapeDtypeStruct((1, D_MODEL), F32),
                   jax.ShapeDtypeStruct((1, LANES), F32)],
        compiler_params=_cp(("arbitrary",), VMEM_LIMIT),
    )(x, o_a, o_b, gates, w_out, ln_g, ln_b, target)


def _mla_bwd(q, k, v, d_o, o, lse):
    seq = q.shape[1]
    tq = 512
    nq = seq // tq

    def body(q_ref, k_ref, v_ref, do_ref, o_ref, lse_ref, dq_ref, dk_ref, dv_ref, d_s, lse_s, dk_s, dv_s, v_s):
        j = pl.program_id(1)
        lane = lax.broadcasted_iota(jnp.int32, (tq, LANES), 1)
        row = lax.broadcasted_iota(jnp.int32, (tq, tq), 0)
        col = lax.broadcasted_iota(jnp.int32, (tq, tq), 1)

        @pl.when(j == 0)
        def _():
            dq_ref[...] = jnp.zeros_like(dq_ref)

            def rowsum(i, carry):
                rows = pl.ds(pl.multiple_of(i * tq, tq), tq)
                prod = do_ref[rows, :] * o_ref[rows, :]
                for hh in range(2):
                    mine = (lane >= 64) if hh else (lane < 64)
                    total = jnp.sum(jnp.where(mine, prod, 0.0), axis=1, keepdims=True)
                    d_s[hh, i] = jnp.transpose(total + jnp.zeros((tq, LANES), F32))[:8]
                    lse_s[hh, i] = jnp.transpose(lse_ref[hh, rows, :])[:8]
                return carry

            lax.fori_loop(0, nq, rowsum, 0)

        dk_s[...] = jnp.zeros_like(dk_s)
        dv_s[...] = jnp.zeros_like(dv_s)
        for hh in range(2):
            v_s[hh] = jnp.where(lane == ONES_LANE[hh], 0.0, v_ref[hh].astype(F32)).astype(BF16)

        def step(i, masked):
            rows = pl.ds(pl.multiple_of(i * tq, tq), tq)
            dob = do_ref[rows, :].astype(BF16)
            for hh in range(2):
                qb, kb, vb = q_ref[hh, rows, :], k_ref[hh], v_s[hh]
                p = jnp.exp(_dot(kb, qb, NT) - lse_s[hh, i][:1])
                if masked:
                    p = jnp.where(row <= col, p, 0.0)
                dv_s[hh] += _dot(p.astype(BF16), dob)
                ds = (p * (_dot(vb, dob, NT) - d_s[hh, i][:1])).astype(BF16)
                dk_s[hh] += _dot(ds, qb)
                dq_ref[hh, rows, :] += _dot(ds, kb, TN)

        def full_step(i, carry):
            step(i, False)
            return carry

        step(j, True)
        lax.fori_loop(j + 1, nq, full_step, 0)
        dk_ref[...] = dk_s[...]
        dv_ref[...] = dv_s[...]

    whole = pl.BlockSpec((2, seq, LANES), lambda p, j: (p, 0, 0))
    blk = pl.BlockSpec((2, tq, LANES), lambda p, j: (p, j, 0))
    pair = pl.BlockSpec((seq, LANES), lambda p, j: (0, p))
    shape = jax.ShapeDtypeStruct((MLA_HEADS, seq, LANES), F32)
    return pl.pallas_call(
        body, name="mla_bwd", grid=(MLA_HEADS // 2, nq),
        in_specs=[whole, blk, blk, pair, pair, whole],
        out_specs=[whole, blk, blk], out_shape=[shape] * 3,
        scratch_shapes=[pltpu.VMEM((2, nq, 8, tq), F32), pltpu.VMEM((2, nq, 8, tq), F32),
                        pltpu.VMEM((2, tq, LANES), F32), pltpu.VMEM((2, tq, LANES), F32),
                        pltpu.VMEM((2, tq, LANES), BF16)],
        compiler_params=_cp(("arbitrary", "arbitrary"), VMEM_LIMIT),
    )(q, k, v, d_o, o, lse)


def _dil_bwd(qr, kr, vb, d_o, o, lse):
    seq = qr.shape[0]
    nq = DIL_Q_BWD
    n_tiles = seq // nq
    chunk = 512

    def body(q_ref, k_ref, v_ref, do_ref, o_ref, lse_ref, dq_ref, dk_ref, dv_ref, d_s, dq_s, dk_s, dv_s, bias_s):
        lane = lax.broadcasted_iota(jnp.int32, (nq, LANES), 1)
        lanec = lax.broadcasted_iota(jnp.int32, (chunk, LANES), 1)
        bias_s[0], bias_s[1] = [b[:nq] for b in _dil_bias(nq)]

        def rowsum(i, carry):
            rows = pl.ds(pl.multiple_of(i * chunk, chunk), chunk)
            prod = do_ref[rows, :] * o_ref[rows, :]
            lo = jnp.sum(jnp.where(lanec < 64, prod, 0.0), axis=1, keepdims=True)
            hi = jnp.sum(jnp.where(lanec >= 64, prod, 0.0), axis=1, keepdims=True)
            d_s[rows, :] = jnp.where(lanec < 64, lo, hi)
            return carry

        lax.fori_loop(0, seq // chunk, rowsum, 0)
        dq_s[...] = jnp.zeros_like(dq_s)
        dk_s[...] = jnp.zeros_like(dk_s)
        dv_s[...] = jnp.zeros_like(dv_s)
        for d in DIL_DILATIONS:

            def tile(t, carry, d=d):
                first, start, prev = _dil_tile_index(t, d, seq, nq)
                rows, prows = _dil_rows(start, d, nq), _dil_rows(prev, d, BLOCK)
                q_t, do_t = q_ref[rows, :], do_ref[rows, :]
                lse_t, d_t = lse_ref[rows, :], d_s[rows, :]
                kcat = jnp.concatenate([k_ref[prows, :], k_ref[rows, :]], axis=0).astype(BF16)
                vcat = jnp.concatenate([v_ref[prows, :], v_ref[rows, :]], axis=0).astype(BF16)
                bias = bias_s[first]
                dq_t = jnp.zeros((nq, LANES), F32)
                dkcat = jnp.zeros((BLOCK + nq, LANES), F32)
                dvcat = jnp.zeros((BLOCK + nq, LANES), F32)
                for hh in range(2):
                    mine = (lane >= 64) if hh else (lane < 64)
                    c0 = 64 * hh
                    qh = jnp.where(mine, q_t, 0.0).astype(BF16)
                    doh = jnp.where(mine, do_t, 0.0).astype(BF16)
                    p = jnp.exp(_dot(qh, kcat, NT) + bias - lse_t[:, c0:c0 + 1])
                    dvcat = dvcat + _dot(p.astype(BF16), doh, TN)
                    dp = _dot(doh, vcat, NT)
                    ds = (p * (dp - d_t[:, c0:c0 + 1])).astype(BF16)
                    dq_t = dq_t + jnp.where(mine, _dot(ds, kcat), 0.0)
                    dkcat = dkcat + _dot(ds, qh, TN)
                dq_s[rows, :] += dq_t
                dk_s[prows, :] += dkcat[:BLOCK]
                dk_s[rows, :] += dkcat[BLOCK:]
                dv_s[prows, :] += dvcat[:BLOCK]
                dv_s[rows, :] += dvcat[BLOCK:]
                return carry

            lax.fori_loop(0, n_tiles, tile, 0, unroll=4)
        dq_ref[...] = dq_s[...].astype(BF16)
        dk_ref[...] = dk_s[...].astype(BF16)
        dv_ref[...] = dv_s[...].astype(BF16)

    col = lambda off: pl.BlockSpec((seq, LANES), lambda p: (0, p + off))
    shape = jax.ShapeDtypeStruct((seq, 4 * LANES), BF16)
    return pl.pallas_call(
        body, name="dil_bwd", grid=(4,),
        in_specs=[col(0), col(0), col(0), col(4), col(0), pl.BlockSpec((None, seq, LANES), lambda p: (p, 0, 0))],
        out_specs=[col(0)] * 3, out_shape=[shape] * 3,
        scratch_shapes=[pltpu.VMEM((seq, LANES), F32)] * 4 + [pltpu.VMEM((2, nq, BLOCK + nq), F32)],
        compiler_params=_cp(("arbitrary",), VMEM_LIMIT),
    )(qr, kr, vb, d_o, o, lse)


def _mla_pre_bwd(cq, ckv, gq, gkv, wuq_e, wukv, ct, st, dq, dk, dv):
    seq = cq.shape[0]
    tr = 512

    def body(cq_ref, ckv_ref, gq_ref, gkv_ref, wuq_ref, wukv_ref, ct_ref, st_ref, dq_ref, dk_ref, dv_ref,
             dcq_ref, dckv_ref, dkr_ref, dwuq_ref, dwukv_ref, dgq_ref, dgkv_ref):
        @pl.when(pl.program_id(0) == 0)
        def _():
            dwuq_ref[...] = jnp.zeros_like(dwuq_ref)
            dwukv_ref[...] = jnp.zeros_like(dwukv_ref)
            dgq_ref[...] = jnp.zeros_like(dgq_ref)
            dgkv_ref[...] = jnp.zeros_like(dgkv_ref)

        lane = lax.broadcasted_iota(jnp.int32, (tr, LANES), 1)
        rope_lanes = jnp.logical_and(lane >= 64, lane < 96)
        ct_, st_ = ct_ref[...], st_ref[...]

        def rope_t(g):
            return ct_ * g + jnp.where(rope_lanes, _mla_rot(st_ * g, lane), 0.0)

        def norm_bwd(c, g, dn, dg_ref):
            r, _ = _rms(c, g)
            u = dn * g
            dg_ref[...] += jnp.sum(dn * c * r, axis=0, keepdims=True)
            return r * u - c * (r * r * r) * jnp.mean(u * c, axis=-1, keepdims=True)

        c, g = cq_ref[...], gq_ref[...]
        _, qn = _rms(c, g)
        dq_all = jnp.concatenate([rope_t(dq_ref[h] * MLA_SCALE) for h in range(MLA_HEADS)], axis=1).astype(BF16)
        dwuq_ref[...] += _dot(qn.astype(BF16), dq_all, TN)
        dcq_ref[...] = norm_bwd(c, g, _dot(dq_all, wuq_ref[...], NT), dgq_ref).astype(BF16)

        c, g = ckv_ref[...], gkv_ref[...]
        _, kvn = _rms(c, g)
        dkpe = jnp.zeros((tr, LANES), F32)
        parts = []
        for h in range(MLA_HEADS):
            dk_h, dv_h = dk_ref[h], dv_ref[h]
            if h % 2 == 0:
                dv_h = pltpu.roll(dv_h, 64, 1)
            parts.append(jnp.where(lane < 64, dk_h, dv_h))
            dkpe = dkpe + jnp.where(rope_lanes, dk_h, 0.0)
        dkv_all = jnp.concatenate(parts, axis=1).astype(BF16)
        dwukv_ref[...] += _dot(kvn.astype(BF16), dkv_all, TN)
        dckv_ref[...] = norm_bwd(c, g, _dot(dkv_all, wukv_ref[...], NT), dgkv_ref).astype(BF16)
        dkr_ref[...] = rope_t(dkpe).astype(BF16)

    row = lambda w: pl.BlockSpec((tr, w), lambda i: (i, 0))
    full = lambda a: pl.BlockSpec(a.shape, lambda i: (0,) * a.ndim)
    head = pl.BlockSpec((MLA_HEADS, tr, LANES), lambda i: (0, i, 0))
    return pl.pallas_call(
        body, name="mla_pre_bwd", grid=(seq // tr,),
        in_specs=[row(Q_LORA), row(KV_LORA), full(gq), full(gkv), full(wuq_e), full(wukv), row(LANES), row(LANES),
                  head, head, head],
        out_specs=[row(Q_LORA), row(KV_LORA), row(LANES), full(wuq_e), full(wukv), full(gq), full(gkv)],
        out_shape=[jax.ShapeDtypeStruct((seq, Q_LORA), BF16), jax.ShapeDtypeStruct((seq, KV_LORA), BF16),
                   jax.ShapeDtypeStruct((seq, LANES), BF16), jax.ShapeDtypeStruct(wuq_e.shape, F32),
                   jax.ShapeDtypeStruct(wukv.shape, F32), jax.ShapeDtypeStruct(gq.shape, F32),
                   jax.ShapeDtypeStruct(gkv.shape, F32)],
        compiler_params=_cp(("arbitrary",), VMEM_LIMIT),
    )(cq, ckv, gq, gkv, wuq_e, wukv, ct, st, dq, dk, dv)


def _in_bwd(dz, dcq, dckv, dgates, dqr, dkr, dvb, dkrope, cd, sd, w_in_p):
    seq = dz.shape[0]
    tr = 512

    def body(dz_ref, dcq_ref, dckv_ref, dg_ref, dqr_ref, dkr_ref, dvb_ref, dkp_ref, cd_ref, sd_ref, w_ref, gx_ref, dh_ref):
        lane = lax.broadcasted_iota(jnp.int32, (tr, LANES), 1)
        rot_lanes = lane % 64 < DIL_ROT
        cd_, sd_ = cd_ref[...], sd_ref[...]

        def rope_t(g):
            return cd_ * g + jnp.where(rot_lanes, _dil_rot(sd_ * g, lane), 0.0)

        dq = [rope_t(dqr_ref[:, LANES * p:LANES * (p + 1)].astype(F32) * DIL_SCALE).astype(BF16) for p in range(4)]
        dk = [rope_t(dkr_ref[:, LANES * p:LANES * (p + 1)].astype(F32)).astype(BF16) for p in range(4)]
        dh = jnp.concatenate([dcq_ref[...], dckv_ref[...], dg_ref[...]] + dq + dk + [dvb_ref[...], dkp_ref[...]], axis=1)
        dh_ref[...] = dh
        gx_ref[...] = ALPHA * dz_ref[...] + _dot(dh, w_ref[...], NT)

    row = lambda w: pl.BlockSpec((tr, w), lambda i: (i, 0))
    return pl.pallas_call(
        body, name="in_bwd", grid=(seq // tr,),
        in_specs=[row(D_MODEL), row(Q_LORA), row(KV_LORA), row(D_MODEL), row(512), row(512), row(512), row(LANES),
                  row(LANES), row(LANES), pl.BlockSpec((D_MODEL, IN_WIDTH_PAD), lambda i: (0, 0))],
        out_specs=[row(D_MODEL), row(IN_WIDTH_PAD)],
        out_shape=[jax.ShapeDtypeStruct((seq, D_MODEL), F32), jax.ShapeDtypeStruct((seq, IN_WIDTH_PAD), BF16)],
        compiler_params=_cp(("arbitrary",), VMEM_LIMIT),
    )(dz, dcq, dckv, dgates, dqr, dkr, dvb, dkrope, cd, sd, w_in_p)


def _dw_in(x, dh):
    seq = dh.shape[0]
    tk = 512
    tn = IN_WIDTH_PAD // 2

    def body(x_ref, dh_ref, o_ref):
        @pl.when(pl.program_id(1) == 0)
        def _():
            o_ref[...] = jnp.zeros_like(o_ref)

        o_ref[...] += _dot(dh_ref[...], x_ref[...].astype(BF16), TN)

    return pl.pallas_call(
        body, name="dw_in", grid=(2, seq // tk),
        in_specs=[pl.BlockSpec((tk, D_MODEL), lambda n, k: (k, 0)), pl.BlockSpec((tk, tn), lambda n, k: (k, n))],
        out_specs=pl.BlockSpec((tn, D_MODEL), lambda n, k: (n, 0)),
        out_shape=jax.ShapeDtypeStruct((IN_WIDTH_PAD, D_MODEL), F32),
        compiler_params=_cp(("arbitrary", "arbitrary"), VMEM_LIMIT),
    )(x, dh)


def _adamw(w, g, m, v, name):
    rows, cols = w.shape
    tc = 256 if cols % 256 == 0 and rows * cols > 2 ** 18 else cols

    def body(w_ref, g_ref, m_ref, v_ref, d_ref, nm_ref, nv_ref):
        g_ = g_ref[...]
        nm = ADAM_B1 * m_ref[...] + (1.0 - ADAM_B1) * g_
        nv = ADAM_B2 * v_ref[...] + (1.0 - ADAM_B2) * jnp.square(g_)
        m_hat = nm / (1.0 - ADAM_B1 ** ADAM_STEP)
        v_hat = nv / (1.0 - ADAM_B2 ** ADAM_STEP)
        d_ref[...] = -ADAM_LR * (m_hat / (jnp.sqrt(v_hat) + ADAM_EPS) + ADAM_WD * w_ref[...])
        nm_ref[...] = nm
        nv_ref[...] = nv

    spec = pl.BlockSpec((rows, tc), lambda i: (0, i))
    return pl.pallas_call(
        body, name=name, grid=(cols // tc,), in_specs=[spec] * 4, out_specs=[spec] * 3,
        out_shape=[jax.ShapeDtypeStruct(w.shape, F32)] * 3, compiler_params=_cp(("arbitrary",)),
    )(w, g, m, v)


def _pad_row(v):
    return jnp.pad(v.reshape(1, -1), ((0, 0), (0, D_MODEL - v.shape[-1])))


def _local_step(x2, target, w_in_p, w_uq_f, wukv_f, w_out_f, q_norm_g, kv_norm_g, ln_g, ln_b):
    seq = x2.shape[0]
    wuq_e = jnp.pad(w_uq_f.reshape(Q_LORA, MLA_HEADS, 96), ((0, 0), (0, 0), (0, 32))).reshape(Q_LORA, MLA_HEADS * LANES)
    ct, st, cd, sd = _rope_tables(seq)
    gq = q_norm_g.reshape(1, Q_LORA)
    gkv = kv_norm_g.reshape(1, KV_LORA)

    cq, ckv, gates, qr, krot, vb, q_e, k_e, v_e = _proj(x2, w_in_p, gq, gkv, wuq_e, wukv_f, ct, st, cd, sd)
    o_a, lse_a = _mla_fwd(q_e, k_e, v_e)
    o_b, lse_b = _dil_fwd(qr, krot, vb)

    dz, d_o, d_gates, dw_out, dln_g, dln_b, loss_part = _post(
        x2, o_a, o_b, gates, w_out_f, ln_g.reshape(1, D_MODEL), ln_b.reshape(1, D_MODEL), target)
    dq_e, dk_e, dv_e = _mla_bwd(q_e, k_e, v_e, d_o, o_a, lse_a)
    dqr, dkr, dvb = _dil_bwd(qr, krot, vb, d_o, o_b, lse_b)
    dcq, dckv, dkrope, dwuq_e, dwukv, dgq, dgkv = _mla_pre_bwd(cq, ckv, gq, gkv, wuq_e, wukv_f, ct, st, dq_e, dk_e, dv_e)
    grad_x, dh = _in_bwd(dz, dcq, dckv, d_gates, dqr, dkr, dvb, dkrope, cd, sd, w_in_p)
    dw_in = _unpermute_dw_in_t(_dw_in(x2, dh))
    dw_uq = dwuq_e.reshape(Q_LORA, MLA_HEADS, LANES)[:, :, :96].reshape(Q_LORA, MLA_HEADS * 96)
    return loss_part, grad_x, dw_in, dw_uq, dwukv, dw_out, dgq, dgkv, dln_g, dln_b


def kernel(x, w_in, q_norm_g, kv_norm_g, w_uq, w_ukv, w_out, ln_g, ln_b, loss_target, m_w_in, m_q_norm_g, m_kv_norm_g, m_w_uq, m_w_ukv, m_w_out, m_ln_g, m_ln_b, v_w_in, v_q_norm_g, v_kv_norm_g, v_w_uq, v_w_ukv, v_w_out, v_ln_g, v_ln_b):
    seq = x.shape[1]
    x2 = x.reshape(seq, D_MODEL)
    target = loss_target.reshape(seq, D_MODEL)

    g_w_in, g_w_uq, g_w_ukv, g_w_out = _all_gather_weights([w_in, w_uq, w_ukv, w_out])
    by_cols = lambda g: jnp.concatenate([g[j] for j in range(N_SHARD)], axis=1)
    loss_part, grad_x, dw_in, dw_uq, dwukv, dw_out, dgq, dgkv, dln_g, dln_b = _local_step(
        x2, target, _permute_w_in_shards(g_w_in), by_cols(g_w_uq), by_cols(g_w_ukv), g_w_out.reshape(D_MODEL, D_MODEL),
        q_norm_g, kv_norm_g, ln_g, ln_b)

    to_shards = lambda d: d.reshape(d.shape[0], N_SHARD, d.shape[1] // N_SHARD).transpose(1, 0, 2)
    grads = [dw_in.reshape(N_SHARD, 808, D_MODEL), to_shards(dw_uq), to_shards(dwukv),
             dw_out.reshape(N_SHARD, 256, D_MODEL)]
    small = jnp.concatenate([_pad_row(dgq), _pad_row(dgkv), dln_g, dln_b, _pad_row(loss_part),
                             jnp.zeros((3, D_MODEL), F32)], axis=0)
    *chip_sums, smalls = _reduce_over_sibling(grads, small)
    g_in_t, g_uq, g_ukv, g_out = _reduce_over_chips(chip_sums)
    g_in = g_in_t.T
    small_sum = _sum_smalls(smalls)
    loss = small_sum[4, 0]

    big = [[o.T for o in _adamw(w.T, g.T, m.T, v.T, name)] for w, g, m, v, name in (
        (w_in, g_in, m_w_in, v_w_in, "adamw_w_in"), (w_uq, g_uq, m_w_uq, v_w_uq, "adamw_w_uq"))]
    big += [_adamw(w, g, m, v, name) for w, g, m, v, name in (
        (w_ukv, g_ukv, m_w_ukv, v_w_ukv, "adamw_w_ukv"), (w_out, g_out, m_w_out, v_w_out, "adamw_w_out"))]
    vec = lambda a, b, c_, d: jnp.concatenate([_pad_row(a), _pad_row(b), _pad_row(c_), _pad_row(d),
                                               jnp.zeros((4, D_MODEL), F32)], axis=0)
    sw = vec(q_norm_g, kv_norm_g, ln_g, ln_b)
    sm = vec(m_q_norm_g, m_kv_norm_g, m_ln_g, m_ln_b)
    sv = vec(v_q_norm_g, v_kv_norm_g, v_ln_g, v_ln_b)
    sg = jnp.concatenate([small_sum[:4], jnp.zeros((4, D_MODEL), F32)], axis=0)
    s_delta, s_m, s_v = _adamw(sw, sg, sm, sv, "adamw_vectors")

    def vectors(a):
        return [a[0, :Q_LORA], a[1, :KV_LORA], a[2], a[3]]

    def ordered(bigs, smalls_):
        return [bigs[0], smalls_[0], smalls_[1], bigs[1], bigs[2], bigs[3], smalls_[2], smalls_[3]]

    grads_out = ordered([g_in, g_uq, g_ukv, g_out], vectors(small_sum))
    deltas = ordered([b[0] for b in big], vectors(s_delta))
    new_m = ordered([b[1] for b in big], vectors(s_m))
    new_v = ordered([b[2] for b in big], vectors(s_v))
    return (loss, grad_x.reshape(x.shape), *grads_out, *deltas, *new_m, *new_v)
```

```python
import functools

import jax
import jax.numpy as jnp
import numpy as np
from jax import lax
from jax.experimental import pallas as pl
from jax.experimental.pallas import tpu as pltpu

F32 = jnp.float32
BF16 = jnp.bfloat16

D_MODEL = 1024
ROPE_THETA = 500000.0
BLOCK = 128
NEG = -1e30
RMS_EPS = 1e-6
LN_EPS = 1e-5

MLA_HEADS = 8
MLA_NOPE = 64
MLA_ROPE = 32
Q_LORA = 384
KV_LORA = 256
DIL_HEADS = 8
DIL_HEAD_DIM = 64
DIL_ROT = 16
DIL_DILATIONS = (1, 4, 16)
IN_WIDTH = 3232
IN_WIDTH_PAD = 3328
ONES_LANE = (64, 0)
MLA_SCALE = (MLA_NOPE + MLA_ROPE) ** -0.5
DIL_SCALE = DIL_HEAD_DIM ** -0.5
ALPHA = 2.0 ** 0.25

ADAM_LR = 0.001
ADAM_B1 = 0.9
ADAM_B2 = 0.999
ADAM_EPS = 1e-08
ADAM_WD = 0.01
ADAM_STEP = 10

N_SHARD = 4
SHARD_SHAPES = ((1024, 808), (384, 192), (256, 256), (256, 1024))
GRAD_SHAPES = ((808, 1024), (384, 192), (256, 256), (256, 1024))
GRAD_SPLIT_COLS = (True, False, False, False)
ROW_CHUNK = 64
LANES = 128
VMEM_LIMIT = 56 * 1024 * 1024
MESH = pl.DeviceIdType.MESH

NT = (((1,), (1,)), ((), ()))
TN = (((0,), (0,)), ((), ()))


def _cp(sem=None, vmem=None):
    return pltpu.CompilerParams(dimension_semantics=sem, vmem_limit_bytes=vmem)


def _dot(a, b, dims=None):
    if dims is None:
        return jnp.dot(a, b, preferred_element_type=F32)
    return lax.dot_general(a, b, dims, preferred_element_type=F32)


def _rope_tables(seq):
    f32 = np.float32
    pos = np.arange(seq, dtype=f32)[:, None]
    one, zero = np.ones((seq, 64), f32), np.zeros((seq, 64), f32)

    def cos_sin(dim):
        inv = np.power(f32(ROPE_THETA), -np.arange(0, dim, 2, dtype=f32) / f32(dim)).astype(f32)
        ang = (pos * inv[None, :]).astype(f32)
        return np.cos(ang).astype(f32), np.sin(ang).astype(f32)

    cos, sin = cos_sin(MLA_ROPE)
    ct = np.concatenate([one, cos, cos, zero[:, :32]], axis=1)
    st = np.concatenate([zero, -sin, sin, zero[:, :32]], axis=1)
    cos, sin = cos_sin(DIL_ROT)
    cd = np.concatenate([cos, cos, one[:, :48]], axis=1)
    sd = np.concatenate([-sin, sin, zero[:, :48]], axis=1)
    return tuple(jnp.asarray(t) for t in (ct, st, np.tile(cd, (1, 2)), np.tile(sd, (1, 2))))


W_IN_ORDER = ((0, 640), (672, 1184), (2720, 3232), (1184, 2720), None, (640, 672))


def _permute_w_in(w):
    z = jnp.zeros((w.shape[0], 64), w.dtype)
    parts = [z if r is None else w[:, r[0]:r[1]] for r in W_IN_ORDER]
    return jnp.concatenate(parts + [z[:, :32]], axis=1)


def _permute_w_in_shards(g):
    width = g.shape[2]
    z = jnp.zeros((g.shape[1], 64), g.dtype)
    parts = []
    for r in W_IN_ORDER:
        if r is None:
            parts.append(z)
            continue
        for j in range(N_SHARD):
            lo, hi = max(r[0], width * j), min(r[1], width * (j + 1))
            if lo < hi:
                parts.append(g[j, :, lo - width * j:hi - width * j])
    return jnp.concatenate(parts + [z[:, :32]], axis=1)


def _unpermute_dw_in_t(dw_t):
    return jnp.concatenate([dw_t[0:640], dw_t[3264:3296], dw_t[640:1152], dw_t[1664:3200], dw_t[1152:1664]], axis=0)


def _position():
    return lax.axis_index("x"), lax.axis_index("y"), lax.axis_index("c")


def _halves(c, rows):
    hr = rows // 2
    return pl.ds(pl.multiple_of(c * hr, 8), hr), pl.ds(pl.multiple_of((1 - c) * hr, 8), hr)


def _for_row_chunks(rows, fn):
    def step(i, carry):
        fn(pl.multiple_of(i * ROW_CHUNK, ROW_CHUNK))
        return carry

    lax.fori_loop(0, rows // ROW_CHUNK, step, 0)


def _all_gather_weights(shards):
    n = len(shards)

    def body(*refs):
        ins, outs = refs[:n], refs[n:2 * n]
        send_sems, recv_sems = refs[2 * n:]
        x, y, c = _position()
        me = 2 * x + y
        chips = [(1 - x, y), (x, 1 - y), (1 - x, 1 - y)]
        for a in range(n):
            def cast(r, a=a):
                outs[a][me, pl.ds(r, ROW_CHUNK), :] = ins[a][pl.ds(r, ROW_CHUNK), :].astype(BF16)

            _for_row_chunks(SHARD_SHAPES[a][0], cast)

        def copy(k, a, slot, rows, to):
            ref = outs[a].at[slot, rows]
            return pltpu.make_async_remote_copy(
                src_ref=ref, dst_ref=ref, send_sem=send_sems.at[k * n + a], recv_sem=recv_sems.at[k * n + a],
                device_id=to, device_id_type=MESH)

        half = [_halves(c, SHARD_SHAPES[a][0])[0] for a in range(n)]
        other = [_halves(c, SHARD_SHAPES[a][0])[1] for a in range(n)]
        first = [copy(k, a, me, half[a], (px, py, c)) for k, (px, py) in enumerate(chips) for a in range(n)]
        for cp in first:
            cp.start()
        passed = []
        for k, (px, py) in enumerate(chips):
            for a in range(n):
                copy(k, a, 2 * px + py, half[a], (x, y, c)).wait_recv()
                cp = copy(3 + k, a, 2 * px + py, half[a], (x, y, 1 - c))
                cp.start()
                passed.append(cp)
        for k, (px, py) in enumerate(chips):
            for a in range(n):
                copy(3 + k, a, 2 * px + py, other[a], (x, y, c)).wait_recv()
        for cp in first + passed:
            cp.wait_send()

    vmem = pl.BlockSpec(memory_space=pltpu.VMEM)
    return pl.pallas_call(
        body, name="all_gather_weights",
        out_shape=[jax.ShapeDtypeStruct((N_SHARD,) + s, BF16) for s in SHARD_SHAPES],
        in_specs=[vmem] * n, out_specs=[vmem] * n,
        scratch_shapes=[pltpu.SemaphoreType.DMA((6 * n,)), pltpu.SemaphoreType.DMA((6 * n,))],
        compiler_params=_cp(None, VMEM_LIMIT),
    )(*shards)


def _grad_half_shape(a):
    rows, cols = GRAD_SHAPES[a]
    return (rows, cols // 2) if GRAD_SPLIT_COLS[a] else (rows // 2, cols)


def _grad_half(a, c):
    rows, cols = GRAD_SHAPES[a]
    if GRAD_SPLIT_COLS[a]:
        return slice(None), pl.ds(pl.multiple_of(c * (cols // 2), LANES), cols // 2)
    return pl.ds(pl.multiple_of(c * (rows // 2), ROW_CHUNK), rows // 2), slice(None)


def _grad_chunks(a, c):
    rows, cols = GRAD_SHAPES[a]
    if GRAD_SPLIT_COLS[a]:
        return [((slice(None), pl.ds(c0, LANES)),
                 (slice(None), pl.ds(pl.multiple_of(c * (cols // 2) + c0, LANES), LANES)))
                for c0 in range(0, cols // 2, LANES)]
    return [((pl.ds(r0, ROW_CHUNK), slice(None)),
             (pl.ds(pl.multiple_of(c * (rows // 2) + r0, ROW_CHUNK), ROW_CHUNK), slice(None)))
            for r0 in range(0, rows // 2, ROW_CHUNK)]


def _reduce_over_sibling(grads, small):
    n = len(grads)

    def body(*refs):
        g_hbm, sm = refs[:n], refs[n]
        sums, smalls = refs[n + 1:2 * n + 1], refs[2 * n + 1]
        stage, got = refs[2 * n + 2:3 * n + 2], refs[3 * n + 2:4 * n + 2]
        send_sems, recv_sems, local_sems = refs[4 * n + 2:]
        x, y, c = _position()
        me = 4 * x + 2 * y + c
        loads = [pltpu.make_async_copy(g_hbm[a], stage[a], local_sems.at[a]) for a in range(n)]
        for ld in loads:
            ld.start()
        smalls[me] = sm[...]
        sends = []
        for rel in range(1, 8):
            px = 1 - x if rel // 4 else x
            py = 1 - y if (rel // 2) % 2 else y
            pc = 1 - c if rel % 2 else c
            cp = pltpu.make_async_remote_copy(
                src_ref=sm, dst_ref=smalls.at[me], send_sem=send_sems.at[n + rel], recv_sem=recv_sems.at[n + rel],
                device_id=(px, py, pc), device_id_type=MESH)
            cp.start()
            sends.append((cp, 4 * px + 2 * py + pc))
        swaps = []
        for a in range(n):
            loads[a].wait()
            cp = pltpu.make_async_remote_copy(
                src_ref=stage[a].at[(slice(None),) + _grad_half(a, 1 - c)], dst_ref=got[a], send_sem=send_sems.at[a], recv_sem=recv_sems.at[a],
                device_id=(x, y, 1 - c), device_id_type=MESH)
            cp.start()
            swaps.append(cp)
        for a in range(n):
            swaps[a].wait_recv()
            for k in range(N_SHARD):
                for in_half, in_whole in _grad_chunks(a, c):
                    pair = stage[a][(k,) + in_whole] + got[a][(k,) + in_half]
                    sums[a][(k,) + in_half] = pair.astype(BF16)
        for rel, (cp, peer) in enumerate(sends, start=1):
            pltpu.make_async_remote_copy(
                src_ref=sm, dst_ref=smalls.at[peer], send_sem=send_sems.at[n + rel], recv_sem=recv_sems.at[n + rel],
                device_id=(x, y, c), device_id_type=MESH).wait_recv()
        for cp in swaps:
            cp.wait_send()
        for cp, _ in sends:
            cp.wait_send()

    vmem = pl.BlockSpec(memory_space=pltpu.VMEM)
    half = [(N_SHARD,) + _grad_half_shape(a) for a in range(n)]
    return pl.pallas_call(
        body, name="reduce_over_sibling",
        out_shape=[jax.ShapeDtypeStruct(s, BF16) for s in half] + [jax.ShapeDtypeStruct((8,) + small.shape, F32)],
        in_specs=[pl.BlockSpec(memory_space=pl.ANY)] * n + [vmem], out_specs=[vmem] * (n + 1),
        scratch_shapes=[pltpu.VMEM((N_SHARD,) + s, F32) for s in GRAD_SHAPES] + [pltpu.VMEM(s, F32) for s in half]
        + [pltpu.SemaphoreType.DMA((n + 8,)), pltpu.SemaphoreType.DMA((n + 8,)), pltpu.SemaphoreType.DMA((n,))],
        compiler_params=_cp(None, VMEM_LIMIT),
    )(*grads, small)


def _reduce_over_chips(sums):
    n = len(sums)

    def body(*refs):
        h, outs, got = refs[:n], refs[n:2 * n], refs[2 * n:3 * n]
        send_sems, recv_sems = refs[3 * n:]
        x, y, c = _position()
        me = 2 * x + y
        chips = [(1 - x, y), (x, 1 - y), (1 - x, 1 - y)]
        sends = []
        for k, (px, py) in enumerate(chips):
            for a in range(n):
                cp = pltpu.make_async_remote_copy(
                    src_ref=h[a].at[2 * px + py], dst_ref=got[a].at[k], send_sem=send_sems.at[k * n + a],
                    recv_sem=recv_sems.at[k * n + a], device_id=(px, py, c), device_id_type=MESH)
                cp.start()
                sends.append(cp)
        for cp in sends:
            cp.wait_recv()
        joins = []
        for a in range(n):
            for in_half, in_whole in _grad_chunks(a, c):
                total = h[a][(me,) + in_half].astype(F32)
                for k in range(3):
                    total = total + got[a][(k,) + in_half].astype(F32)
                outs[a][in_whole] = total
            half = outs[a].at[_grad_half(a, c)]
            cp = pltpu.make_async_remote_copy(
                src_ref=half, dst_ref=half, send_sem=send_sems.at[3 * n + a],
                recv_sem=recv_sems.at[3 * n + a], device_id=(x, y, 1 - c), device_id_type=MESH)
            cp.start()
            joins.append(cp)
        for a in range(n):
            other = outs[a].at[_grad_half(a, 1 - c)]
            pltpu.make_async_remote_copy(
                src_ref=other, dst_ref=other, send_sem=send_sems.at[3 * n + a],
                recv_sem=recv_sems.at[3 * n + a], device_id=(x, y, c), device_id_type=MESH).wait_recv()
        for cp in sends + joins:
            cp.wait_send()

    vmem = pl.BlockSpec(memory_space=pltpu.VMEM)
    return pl.pallas_call(
        body, name="reduce_over_chips",
        out_shape=[jax.ShapeDtypeStruct(s, F32) for s in GRAD_SHAPES],
        in_specs=[vmem] * n, out_specs=[vmem] * n,
        scratch_shapes=[pltpu.VMEM((3,) + _grad_half_shape(a), BF16) for a in range(n)]
        + [pltpu.SemaphoreType.DMA((4 * n,)), pltpu.SemaphoreType.DMA((4 * n,))],
        compiler_params=_cp(None, VMEM_LIMIT),
    )(*sums)


def _sum_smalls(smalls):
    def body(s, o):
        acc = s[0]
        for d in range(1, 8):
            acc = acc + s[d]
        o[...] = acc

    return pl.pallas_call(body, name="sum_smalls", out_shape=jax.ShapeDtypeStruct(smalls.shape[1:], F32))(smalls)


def _proj(x, w_in_p, gq, gkv, wuq_e, wukv, ct, st, cd, sd):
    seq = x.shape[0]
    tr = 512

    def body(x_ref, w_ref, gq_ref, gkv_ref, wuq_ref, wukv_ref, ct_ref, st_ref, cd_ref, sd_ref,
             cq_ref, ckv_ref, g_ref, qr_ref, kr_ref, vb_ref, q_out, k_out, v_out):
        lane = lax.broadcasted_iota(jnp.int32, (tr, LANES), 1)
        xb = x_ref[...].astype(BF16)
        cq = _dot(xb, w_ref[:, 0:384])
        ckv = _dot(xb, w_ref[:, 384:640])
        cq_ref[...] = cq
        ckv_ref[...] = ckv
        g_ref[...] = _dot(xb, w_ref[:, 640:1664])

        cd_, sd_ = cd_ref[...], sd_ref[...]
        qb = _dot(xb, w_ref[:, 1664:2176])
        kb = _dot(xb, w_ref[:, 2176:2688])
        for p in range(4):
            cols = slice(LANES * p, LANES * (p + 1))
            t = qb[:, cols]
            qr_ref[:, cols] = (t * cd_ + _dil_rot(t, lane) * sd_) * DIL_SCALE
            t = kb[:, cols]
            kr_ref[:, cols] = t * cd_ + _dil_rot(t, lane) * sd_
        vb_ref[...] = _dot(xb, w_ref[:, 2688:3200])

        ct_, st_ = ct_ref[...], st_ref[...]

        def rope(t):
            return t * ct_ + _mla_rot(t, lane) * st_

        _, qn = _rms(cq, gq_ref[...])
        q_all = _dot(qn.astype(BF16), wuq_ref[...])
        for h in range(MLA_HEADS):
            q_out[h] = (rope(q_all[:, LANES * h:LANES * (h + 1)]) * MLA_SCALE).astype(BF16)
        _, kvn = _rms(ckv, gkv_ref[...])
        kv_all = _dot(kvn.astype(BF16), wukv_ref[...])
        kpe = rope(_dot(xb, w_ref[:, 3200:3328]))
        for h in range(MLA_HEADS):
            kv_h = kv_all[:, LANES * h:LANES * (h + 1)]
            k_out[h] = jnp.where(lane < 64, kv_h, kpe).astype(BF16)
            if h % 2:
                v = jnp.where(lane >= 64, kv_h, 0.0)
            else:
                v = jnp.where(lane < 64, pltpu.roll(kv_h, 64, 1), 0.0)
            v_out[h] = jnp.where(lane == ONES_LANE[h % 2], 1.0, v).astype(BF16)

    row = lambda w: pl.BlockSpec((tr, w), lambda i: (i, 0))
    full = lambda a: pl.BlockSpec(a.shape, lambda i: (0,) * a.ndim)
    head = pl.BlockSpec((MLA_HEADS, tr, LANES), lambda i: (0, i, 0))
    widths = (Q_LORA, KV_LORA, D_MODEL, 512, 512, 512)
    return pl.pallas_call(
        body, name="proj", grid=(seq // tr,),
        in_specs=[row(D_MODEL), full(w_in_p), full(gq), full(gkv), full(wuq_e), full(wukv)] + [row(LANES)] * 4,
        out_specs=[row(w) for w in widths] + [head] * 3,
        out_shape=[jax.ShapeDtypeStruct((seq, w), F32) for w in widths]
        + [jax.ShapeDtypeStruct((MLA_HEADS, seq, LANES), BF16)] * 3,
        compiler_params=_cp(("arbitrary",), VMEM_LIMIT),
    )(x, w_in_p, gq, gkv, wuq_e, wukv, ct, st, cd, sd)


def _mla_rot(t, lane):
    return jnp.where(lane < 80, pltpu.roll(t, 112, 1), pltpu.roll(t, 16, 1))


def _dil_rot(t, lane):
    return jnp.where(lane % 64 < 8, pltpu.roll(t, 120, 1), pltpu.roll(t, 8, 1))


def _rms(c, g):
    r = lax.rsqrt(jnp.mean(c * c, axis=-1, keepdims=True) + RMS_EPS)
    return r, c * r * g


def _mla_fwd(q, k, v):
    seq = q.shape[1]
    tq = 512
    nq = seq // tq

    def body(q_ref, k_ref, v_ref, o_ref, lse_ref, m_s, acc_s, s_buf):
        i = pl.program_id(1)
        row = lax.broadcasted_iota(jnp.int32, (tq, tq), 0)
        col = lax.broadcasted_iota(jnp.int32, (tq, tq), 1)
        lane = lax.broadcasted_iota(jnp.int32, (tq, LANES), 1)
        m_s[...] = jnp.full((2, tq, LANES), NEG, F32)
        acc_s[...] = jnp.zeros((2, tq, LANES), F32)

        def block(j):
            return pl.ds(pl.multiple_of(j * tq, tq), tq)

        def scores(hh, j):
            return _dot(q_ref[hh], k_ref[hh, block(j), :], NT)

        def consume(hh, j, s):
            m_prev = m_s[hh]
            m_new = jnp.maximum(m_prev, jnp.max(s, axis=1, keepdims=True))
            p = jnp.exp(s - m_new[:, :1])
            acc_s[hh] = jnp.exp(m_prev - m_new) * acc_s[hh] + _dot(p.astype(BF16), v_ref[hh, block(j), :])
            m_s[hh] = m_new

        for hh in range(2):
            s_buf[0, hh] = scores(hh, 0)

        def full_step(j, carry):
            slot = j & 1
            for hh in range(2):
                s = s_buf[slot, hh]
                s_buf[1 - slot, hh] = scores(hh, j + 1)
                consume(hh, j, s)
            return carry

        lax.fori_loop(0, i, full_step, 0)
        total = jnp.zeros((tq, LANES), F32)
        for hh in range(2):
            consume(hh, i, jnp.where(col <= row, s_buf[i & 1, hh], NEG))
            acc = acc_s[hh]
            l = acc[:, ONES_LANE[hh]:ONES_LANE[hh] + 1]
            mine = (lane >= 64) if hh else (lane < 64)
            total = total + jnp.where(mine, acc / l, 0.0)
            lse_ref[hh] = m_s[hh] + jnp.log(l)
        o_ref[...] = total

    kv_spec = pl.BlockSpec((2, seq, LANES), lambda p, i: (p, 0, 0))
    return pl.pallas_call(
        body, name="mla_fwd", grid=(MLA_HEADS // 2, nq),
        in_specs=[pl.BlockSpec((2, tq, LANES), lambda p, i: (p, i, 0)), kv_spec, kv_spec],
        out_specs=[pl.BlockSpec((tq, LANES), lambda p, i: (i, p)), pl.BlockSpec((2, tq, LANES), lambda p, i: (p, i, 0))],
        out_shape=[jax.ShapeDtypeStruct((seq, 4 * LANES), F32), jax.ShapeDtypeStruct((MLA_HEADS, seq, LANES), F32)],
        scratch_shapes=[pltpu.VMEM((2, tq, LANES), F32), pltpu.VMEM((2, tq, LANES), F32),
                        pltpu.VMEM((2, 2, tq, tq), F32)],
        compiler_params=_cp(("arbitrary", "arbitrary"), VMEM_LIMIT),
    )(q, k, v)


DIL_Q_FWD = 2 * BLOCK
DIL_Q_BWD = BLOCK


def _dil_tile_index(t, d, seq, nq):
    per_class = seq // (nq * d)
    shift = per_class.bit_length() - 1
    r = t >> shift
    n = t & (per_class - 1)
    start = r + (nq * d) * n
    prev = jnp.maximum(start - BLOCK * d, r)
    if d == 1:
        start = pl.multiple_of(start, nq)
        prev = pl.multiple_of(prev, BLOCK)
    return (n == 0).astype(jnp.int32), start, prev


def _dil_rows(start, d, size):
    return pl.ds(start, size) if d == 1 else pl.ds(start, size, stride=d)


def _dil_bias(nq):
    i = lax.broadcasted_iota(jnp.int32, (2 * nq, BLOCK + nq), 0) % nq
    j = lax.broadcasted_iota(jnp.int32, (2 * nq, BLOCK + nq), 1)
    band = (j >= i) & (j <= i + BLOCK)
    return jnp.where(band, 0.0, NEG), jnp.where(band & (j >= BLOCK), 0.0, NEG)


def _stack_heads(t, lane):
    return jnp.concatenate([jnp.where(lane < 64, t, 0.0), jnp.where(lane >= 64, t, 0.0)], axis=0)


def _unstack_heads(t, lane):
    nq = t.shape[0] // 2
    return jnp.where(lane < 64, t[:nq], t[nq:])


def _dil_fwd(qr, kr, vb):
    seq = qr.shape[0]
    nq = DIL_Q_FWD
    n_tiles = seq // nq
    assert seq % (nq * max(DIL_DILATIONS)) == 0

    def body(q_ref, k_ref, v_ref, o_ref, lse_ref, m_s, l_s, n_s, bias_s):
        lane = lax.broadcasted_iota(jnp.int32, (nq, LANES), 1)
        bias_s[0], bias_s[1] = _dil_bias(nq)
        for bi, d in enumerate(DIL_DILATIONS):

            def tile(t, carry, d=d, bi=bi):
                first, start, prev = _dil_tile_index(t, d, seq, nq)
                rows, prows = _dil_rows(start, d, nq), _dil_rows(prev, d, BLOCK)
                qst = _stack_heads(q_ref[rows, :], lane).astype(BF16)
                if seq == nq * d:
                    kcat, vcat = k_ref[rows, :].astype(BF16), v_ref[rows, :].astype(BF16)
                    s = _dot(qst, kcat, NT) + bias_s[1, :, BLOCK:]
                else:
                    kcat = jnp.concatenate([k_ref[prows, :], k_ref[rows, :]], axis=0).astype(BF16)
                    vcat = jnp.concatenate([v_ref[prows, :], v_ref[rows, :]], axis=0).astype(BF16)
                    s = _dot(qst, kcat, NT) + bias_s[first]
                m = jnp.max(s, axis=1, keepdims=True)
                p = jnp.exp(s - m)
                l2 = _unstack_heads(jnp.sum(p, axis=1, keepdims=True) + jnp.zeros((2 * nq, LANES), F32), lane)
                m2 = _unstack_heads(m + jnp.zeros((2 * nq, LANES), F32), lane)
                num2 = _unstack_heads(_dot(p.astype(BF16), vcat), lane)
                if bi == 0:
                    m_s[rows, :] = m2
                    l_s[rows, :] = l2
                    n_s[rows, :] = num2
                else:
                    m_old = m_s[rows, :]
                    m_new = jnp.maximum(m_old, m2)
                    a = jnp.exp(m_old - m_new)
                    b = jnp.exp(m2 - m_new)
                    m_s[rows, :] = m_new
                    l_s[rows, :] = a * l_s[rows, :] + b * l2
                    n_s[rows, :] = a * n_s[rows, :] + b * num2
                return carry

            lax.fori_loop(0, n_tiles, tile, 0, unroll=2)
        o_ref[...] = n_s[...] / l_s[...]
        lse_ref[...] = m_s[...] + jnp.log(l_s[...])

    col = lambda off: pl.BlockSpec((seq, LANES), lambda p: (0, p + off))
    return pl.pallas_call(
        body, name="dil_fwd", grid=(4,),
        in_specs=[col(0), col(0), col(0)],
        out_specs=[col(0), pl.BlockSpec((None, seq, LANES), lambda p: (p, 0, 0))],
        out_shape=[jax.ShapeDtypeStruct((seq, 4 * LANES), F32), jax.ShapeDtypeStruct((4, seq, LANES), F32)],
        scratch_shapes=[pltpu.VMEM((seq, LANES), F32)] * 3 + [pltpu.VMEM((2, 2 * nq, BLOCK + nq), F32)],
        compiler_params=_cp(("arbitrary",), VMEM_LIMIT),
    )(qr, kr, vb)


def _post(x, o_a, o_b, gates, w_out, ln_g, ln_b, target):
    seq = x.shape[0]
    tr = 512

    def body(x_ref, oa_ref, ob_ref, g_ref, w_ref, lg_ref, lb_ref, t_ref,
             dz_ref, do_ref, dg_ref, dw_ref, dlg_ref, dlb_ref, loss_ref):
        @pl.when(pl.program_id(0) == 0)
        def _():
            dw_ref[...] = jnp.zeros_like(dw_ref)
            dlg_ref[...] = jnp.zeros_like(dlg_ref)
            dlb_ref[...] = jnp.zeros_like(dlb_ref)
            loss_ref[...] = jnp.zeros_like(loss_ref)

        g = g_ref[...]
        sg = 1.0 / (1.0 + jnp.exp(-g))
        silu = g * sg
        o = jnp.concatenate([oa_ref[...], ob_ref[...]], axis=1)
        mixb = (o * silu).astype(BF16)
        w = w_ref[...]
        z = ALPHA * x_ref[...] + _dot(mixb, w)
        mu = jnp.mean(z, axis=-1, keepdims=True)
        zc = z - mu
        rstd = lax.rsqrt(jnp.mean(zc * zc, axis=-1, keepdims=True) + LN_EPS)
        xhat = zc * rstd
        lg = lg_ref[...]
        err = xhat * lg + lb_ref[...] - t_ref[...]
        loss_ref[...] += jnp.sum(err * err) * (0.5 / D_MODEL)
        dy = err * (1.0 / D_MODEL)
        dlg_ref[...] += jnp.sum(dy * xhat, axis=0, keepdims=True)
        dlb_ref[...] += jnp.sum(dy, axis=0, keepdims=True)
        dxh = dy * lg
        dz = rstd * (dxh - jnp.mean(dxh, axis=-1, keepdims=True) - xhat * jnp.mean(dxh * xhat, axis=-1, keepdims=True))
        dz_ref[...] = dz
        dzb = dz.astype(BF16)
        dmix = _dot(dzb, w, NT)
        do_ref[...] = dmix * silu
        dg_ref[...] = (dmix * o * (sg * (1.0 + g * (1.0 - sg)))).astype(BF16)
        dw_ref[...] += _dot(mixb, dzb, TN)

    row = lambda w: pl.BlockSpec((tr, w), lambda i: (i, 0))
    full = lambda s: pl.BlockSpec(s, lambda i: (0, 0))
    return pl.pallas_call(
        body, name="post", grid=(seq // tr,),
        in_specs=[row(D_MODEL), row(512), row(512), row(D_MODEL), full((D_MODEL, D_MODEL)), full((1, D_MODEL)),
                  full((1, D_MODEL)), row(D_MODEL)],
        out_specs=[row(D_MODEL), row(D_MODEL), row(D_MODEL), full((D_MODEL, D_MODEL)), full((1, D_MODEL)),
                   full((1, D_MODEL)), full((1, LANES))],
        out_shape=[jax.ShapeDtypeStruct((seq, D_MODEL), F32), jax.ShapeDtypeStruct((seq, D_MODEL), F32),
                   jax.ShapeDtypeStruct((seq, D_MODEL), BF16), jax.ShapeDtypeStruct((D_MODEL, D_MODEL), F32),
                   jax.ShapeDtypeStruct((1, D_MODEL), F32), jax.ShapeDtypeStruct((1, D_MODEL), F32),
                   jax.ShapeDtypeStruct((1, LANES), F32)],
        compiler_params=_cp(("arbitrary",), VMEM_LIMIT),
    )(x, o_a, o_b, gates, w_out, ln_g, ln_b, target)


def _mla_bwd(q, k, v, d_o, o, lse):
    seq = q.shape[1]
    tq = 512
    nq = seq // tq

    def body(q_ref, k_ref, v_ref, do_ref, o_ref, lse_ref, dq_ref, dk_ref, dv_ref, d_s, lse_s, dk_s, dv_s, v_s, kt_s, dqt_s):
        j = pl.program_id(1)
        lane = lax.broadcasted_iota(jnp.int32, (tq, LANES), 1)
        row = lax.broadcasted_iota(jnp.int32, (tq, tq), 0)
        col = lax.broadcasted_iota(jnp.int32, (tq, tq), 1)

        @pl.when(j == 0)
        def _():
            dqt_s[...] = jnp.zeros_like(dqt_s)

            def rowsum(i, carry):
                rows = pl.ds(pl.multiple_of(i * tq, tq), tq)
                prod = do_ref[rows, :] * o_ref[rows, :]
                for hh in range(2):
                    mine = (lane >= 64) if hh else (lane < 64)
                    total = jnp.sum(jnp.where(mine, prod, 0.0), axis=1, keepdims=True)
                    d_s[hh, i] = jnp.transpose(total + jnp.zeros((tq, LANES), F32))[:8]
                    lse_s[hh, i] = jnp.transpose(lse_ref[hh, rows, :])[:8]
                return carry

            lax.fori_loop(0, nq, rowsum, 0)

        dk_s[...] = jnp.zeros_like(dk_s)
        dv_s[...] = jnp.zeros_like(dv_s)
        for hh in range(2):
            v_s[hh] = jnp.where(lane == ONES_LANE[hh], 0.0, v_ref[hh].astype(F32)).astype(BF16)
            kt_s[hh] = jnp.transpose(k_ref[hh].astype(F32)).astype(BF16)

        def step(i, masked):
            rows = pl.ds(pl.multiple_of(i * tq, tq), tq)
            dob = do_ref[rows, :].astype(BF16)
            for hh in range(2):
                qb, kb, vb = q_ref[hh, rows, :], k_ref[hh], v_s[hh]
                p = jnp.exp(_dot(kb, qb, NT) - lse_s[hh, i][:1])
                if masked:
                    p = jnp.where(row <= col, p, 0.0)
                dv_s[hh] += _dot(p.astype(BF16), dob)
                ds = (p * (_dot(vb, dob, NT) - d_s[hh, i][:1])).astype(BF16)
                dk_s[hh] += _dot(ds, qb)
                dqt_s[hh, i] += _dot(kt_s[hh], ds)

        def full_step(i, carry):
            step(i, False)
            return carry

        step(j, True)
        lax.fori_loop(j + 1, nq, full_step, 0)
        dk_ref[...] = dk_s[...]
        dv_ref[...] = dv_s[...]

        @pl.when(j == nq - 1)
        def _():
            def untranspose(i, carry):
                rows = pl.ds(pl.multiple_of(i * tq, tq), tq)
                for hh in range(2):
                    dq_ref[hh, rows, :] = jnp.transpose(dqt_s[hh, i])
                return carry

            lax.fori_loop(0, nq, untranspose, 0)

    whole = pl.BlockSpec((2, seq, LANES), lambda p, j: (p, 0, 0))
    blk = pl.BlockSpec((2, tq, LANES), lambda p, j: (p, j, 0))
    pair = pl.BlockSpec((seq, LANES), lambda p, j: (0, p))
    shape = jax.ShapeDtypeStruct((MLA_HEADS, seq, LANES), F32)
    return pl.pallas_call(
        body, name="mla_bwd", grid=(MLA_HEADS // 2, nq),
        in_specs=[whole, blk, blk, pair, pair, whole],
        out_specs=[whole, blk, blk], out_shape=[shape] * 3,
        scratch_shapes=[pltpu.VMEM((2, nq, 8, tq), F32), pltpu.VMEM((2, nq, 8, tq), F32),
                        pltpu.VMEM((2, tq, LANES), F32), pltpu.VMEM((2, tq, LANES), F32),
                        pltpu.VMEM((2, tq, LANES), BF16), pltpu.VMEM((2, LANES, tq), BF16),
                        pltpu.VMEM((2, nq, LANES, tq), F32)],
        compiler_params=_cp(("arbitrary", "arbitrary"), VMEM_LIMIT),
    )(q, k, v, d_o, o, lse)


def _dil_bwd(qr, kr, vb, d_o, o, lse):
    seq = qr.shape[0]
    nq = DIL_Q_BWD
    n_tiles = seq // nq
    chunk = 512

    def body(q_ref, k_ref, v_ref, do_ref, o_ref, lse_ref, dq_ref, dk_ref, dv_ref, d_s, dq_s, dk_s, dv_s, bias_s):
        lane = lax.broadcasted_iota(jnp.int32, (nq, LANES), 1)
        lanec = lax.broadcasted_iota(jnp.int32, (chunk, LANES), 1)
        bias_s[0], bias_s[1] = [b[:nq] for b in _dil_bias(nq)]

        def rowsum(i, carry):
            rows = pl.ds(pl.multiple_of(i * chunk, chunk), chunk)
            prod = do_ref[rows, :] * o_ref[rows, :]
            lo = jnp.sum(jnp.where(lanec < 64, prod, 0.0), axis=1, keepdims=True)
            hi = jnp.sum(jnp.where(lanec >= 64, prod, 0.0), axis=1, keepdims=True)
            d_s[rows, :] = jnp.where(lanec < 64, lo, hi)
            return carry

        lax.fori_loop(0, seq // chunk, rowsum, 0)
        dq_s[...] = jnp.zeros_like(dq_s)
        dk_s[...] = jnp.zeros_like(dk_s)
        dv_s[...] = jnp.zeros_like(dv_s)
        for d in DIL_DILATIONS:

            def tile(t, carry, d=d):
                first, start, prev = _dil_tile_index(t, d, seq, nq)
                rows, prows = _dil_rows(start, d, nq), _dil_rows(prev, d, BLOCK)
                q_t, do_t = q_ref[rows, :], do_ref[rows, :]
                lse_t, d_t = lse_ref[rows, :], d_s[rows, :]
                kcat = jnp.concatenate([k_ref[prows, :], k_ref[rows, :]], axis=0).astype(BF16)
                vcat = jnp.concatenate([v_ref[prows, :], v_ref[rows, :]], axis=0).astype(BF16)
                bias = bias_s[first]
                dq_t = jnp.zeros((nq, LANES), F32)
                dkcat = jnp.zeros((BLOCK + nq, LANES), F32)
                dvcat = jnp.zeros((BLOCK + nq, LANES), F32)
                for hh in range(2):
                    mine = (lane >= 64) if hh else (lane < 64)
                    c0 = 64 * hh
                    qh = jnp.where(mine, q_t, 0.0).astype(BF16)
                    doh = jnp.where(mine, do_t, 0.0).astype(BF16)
                    p = jnp.exp(_dot(qh, kcat, NT) + bias - lse_t[:, c0:c0 + 1])
                    dvcat = dvcat + _dot(p.astype(BF16), doh, TN)
                    dp = _dot(doh, vcat, NT)
                    ds = (p * (dp - d_t[:, c0:c0 + 1])).astype(BF16)
                    dq_t = dq_t + jnp.where(mine, _dot(ds, kcat), 0.0)
                    dkcat = dkcat + _dot(ds, qh, TN)
                dq_s[rows, :] += dq_t
                dk_s[prows, :] += dkcat[:BLOCK]
                dk_s[rows, :] += dkcat[BLOCK:]
                dv_s[prows, :] += dvcat[:BLOCK]
                dv_s[rows, :] += dvcat[BLOCK:]
                return carry

            lax.fori_loop(0, n_tiles, tile, 0, unroll=4)
        dq_ref[...] = dq_s[...].astype(BF16)
        dk_ref[...] = dk_s[...].astype(BF16)
        dv_ref[...] = dv_s[...].astype(BF16)

    col = lambda off: pl.BlockSpec((seq, LANES), lambda p: (0, p + off))
    shape = jax.ShapeDtypeStruct((seq, 4 * LANES), BF16)
    return pl.pallas_call(
        body, name="dil_bwd", grid=(4,),
        in_specs=[col(0), col(0), col(0), col(4), col(0), pl.BlockSpec((None, seq, LANES), lambda p: (p, 0, 0))],
        out_specs=[col(0)] * 3, out_shape=[shape] * 3,
        scratch_shapes=[pltpu.VMEM((seq, LANES), F32)] * 4 + [pltpu.VMEM((2, nq, BLOCK + nq), F32)],
        compiler_params=_cp(("arbitrary",), VMEM_LIMIT),
    )(qr, kr, vb, d_o, o, lse)


def _mla_pre_bwd(cq, ckv, gq, gkv, wuq_e, wukv, ct, st, dq, dk, dv):
    seq = cq.shape[0]
    tr = 512

    def body(cq_ref, ckv_ref, gq_ref, gkv_ref, wuq_ref, wukv_ref, ct_ref, st_ref, dq_ref, dk_ref, dv_ref,
             dcq_ref, dckv_ref, dkr_ref, dwuq_ref, dwukv_ref, dgq_ref, dgkv_ref):
        @pl.when(pl.program_id(0) == 0)
        def _():
            dwuq_ref[...] = jnp.zeros_like(dwuq_ref)
            dwukv_ref[...] = jnp.zeros_like(dwukv_ref)
            dgq_ref[...] = jnp.zeros_like(dgq_ref)
            dgkv_ref[...] = jnp.zeros_like(dgkv_ref)

        lane = lax.broadcasted_iota(jnp.int32, (tr, LANES), 1)
        rope_lanes = jnp.logical_and(lane >= 64, lane < 96)
        ct_, st_ = ct_ref[...], st_ref[...]

        def rope_t(g):
            return ct_ * g + jnp.where(rope_lanes, _mla_rot(st_ * g, lane), 0.0)

        def norm_bwd(c, g, dn, dg_ref):
            r, _ = _rms(c, g)
            u = dn * g
            dg_ref[...] += jnp.sum(dn * c * r, axis=0, keepdims=True)
            return r * u - c * (r * r * r) * jnp.mean(u * c, axis=-1, keepdims=True)

        c, g = cq_ref[...], gq_ref[...]
        _, qn = _rms(c, g)
        dq_all = jnp.concatenate([rope_t(dq_ref[h] * MLA_SCALE) for h in range(MLA_HEADS)], axis=1).astype(BF16)
        dwuq_ref[...] += _dot(qn.astype(BF16), dq_all, TN)
        dcq_ref[...] = norm_bwd(c, g, _dot(dq_all, wuq_ref[...], NT), dgq_ref).astype(BF16)

        c, g = ckv_ref[...], gkv_ref[...]
        _, kvn = _rms(c, g)
        dkpe = jnp.zeros((tr, LANES), F32)
        parts = []
        for h in range(MLA_HEADS):
            dk_h, dv_h = dk_ref[h], dv_ref[h]
            if h % 2 == 0:
                dv_h = pltpu.roll(dv_h, 64, 1)
            parts.append(jnp.where(lane < 64, dk_h, dv_h))
            dkpe = dkpe + jnp.where(rope_lanes, dk_h, 0.0)
        dkv_all = jnp.concatenate(parts, axis=1).astype(BF16)
        dwukv_ref[...] += _dot(kvn.astype(BF16), dkv_all, TN)
        dckv_ref[...] = norm_bwd(c, g, _dot(dkv_all, wukv_ref[...], NT), dgkv_ref).astype(BF16)
        dkr_ref[...] = rope_t(dkpe).astype(BF16)

    row = lambda w: pl.BlockSpec((tr, w), lambda i: (i, 0))
    full = lambda a: pl.BlockSpec(a.shape, lambda i: (0,) * a.ndim)
    head = pl.BlockSpec((MLA_HEADS, tr, LANES), lambda i: (0, i, 0))
    return pl.pallas_call(
        body, name="mla_pre_bwd", grid=(seq // tr,),
        in_specs=[row(Q_LORA), row(KV_LORA), full(gq), full(gkv), full(wuq_e), full(wukv), row(LANES), row(LANES),
                  head, head, head],
        out_specs=[row(Q_LORA), row(KV_LORA), row(LANES), full(wuq_e), full(wukv), full(gq), full(gkv)],
        out_shape=[jax.ShapeDtypeStruct((seq, Q_LORA), BF16), jax.ShapeDtypeStruct((seq, KV_LORA), BF16),
                   jax.ShapeDtypeStruct((seq, LANES), BF16), jax.ShapeDtypeStruct(wuq_e.shape, F32),
                   jax.ShapeDtypeStruct(wukv.shape, F32), jax.ShapeDtypeStruct(gq.shape, F32),
                   jax.ShapeDtypeStruct(gkv.shape, F32)],
        compiler_params=_cp(("arbitrary",), VMEM_LIMIT),
    )(cq, ckv, gq, gkv, wuq_e, wukv, ct, st, dq, dk, dv)


def _in_bwd(dz, dcq, dckv, dgates, dqr, dkr, dvb, dkrope, cd, sd, w_in_p):
    seq = dz.shape[0]
    tr = 512

    def body(dz_ref, dcq_ref, dckv_ref, dg_ref, dqr_ref, dkr_ref, dvb_ref, dkp_ref, cd_ref, sd_ref, w_ref, gx_ref, dh_ref):
        lane = lax.broadcasted_iota(jnp.int32, (tr, LANES), 1)
        rot_lanes = lane % 64 < DIL_ROT
        cd_, sd_ = cd_ref[...], sd_ref[...]

        def rope_t(g):
            return cd_ * g + jnp.where(rot_lanes, _dil_rot(sd_ * g, lane), 0.0)

        dq = [rope_t(dqr_ref[:, LANES * p:LANES * (p + 1)].astype(F32) * DIL_SCALE).astype(BF16) for p in range(4)]
        dk = [rope_t(dkr_ref[:, LANES * p:LANES * (p + 1)].astype(F32)).astype(BF16) for p in range(4)]
        dh = jnp.concatenate([dcq_ref[...], dckv_ref[...], dg_ref[...]] + dq + dk + [dvb_ref[...], dkp_ref[...]], axis=1)
        dh_ref[...] = dh
        gx_ref[...] = ALPHA * dz_ref[...] + _dot(dh, w_ref[...], NT)

    row = lambda w: pl.BlockSpec((tr, w), lambda i: (i, 0))
    return pl.pallas_call(
        body, name="in_bwd", grid=(seq // tr,),
        in_specs=[row(D_MODEL), row(Q_LORA), row(KV_LORA), row(D_MODEL), row(512), row(512), row(512), row(LANES),
                  row(LANES), row(LANES), pl.BlockSpec((D_MODEL, IN_WIDTH_PAD), lambda i: (0, 0))],
        out_specs=[row(D_MODEL), row(IN_WIDTH_PAD)],
        out_shape=[jax.ShapeDtypeStruct((seq, D_MODEL), F32), jax.ShapeDtypeStruct((seq, IN_WIDTH_PAD), BF16)],
        compiler_params=_cp(("arbitrary",), VMEM_LIMIT),
    )(dz, dcq, dckv, dgates, dqr, dkr, dvb, dkrope, cd, sd, w_in_p)


def _dw_in(x, dh):
    seq = dh.shape[0]
    tk = 512
    tn = IN_WIDTH_PAD // 2

    def body(x_ref, dh_ref, o_ref):
        @pl.when(pl.program_id(1) == 0)
        def _():
            o_ref[...] = jnp.zeros_like(o_ref)

        o_ref[...] += _dot(dh_ref[...], x_ref[...].astype(BF16), TN)

    return pl.pallas_call(
        body, name="dw_in", grid=(2, seq // tk),
        in_specs=[pl.BlockSpec((tk, D_MODEL), lambda n, k: (k, 0)), pl.BlockSpec((tk, tn), lambda n, k: (k, n))],
        out_specs=pl.BlockSpec((tn, D_MODEL), lambda n, k: (n, 0)),
        out_shape=jax.ShapeDtypeStruct((IN_WIDTH_PAD, D_MODEL), F32),
        compiler_params=_cp(("arbitrary", "arbitrary"), VMEM_LIMIT),
    )(x, dh)


def _adamw(w, g, m, v, name):
    rows, cols = w.shape
    tc = 256 if cols % 256 == 0 and rows * cols > 2 ** 18 else cols

    def body(w_ref, g_ref, m_ref, v_ref, d_ref, nm_ref, nv_ref):
        g_ = g_ref[...]
        nm = ADAM_B1 * m_ref[...] + (1.0 - ADAM_B1) * g_
        nv = ADAM_B2 * v_ref[...] + (1.0 - ADAM_B2) * jnp.square(g_)
        m_hat = nm / (1.0 - ADAM_B1 ** ADAM_STEP)
        v_hat = nv / (1.0 - ADAM_B2 ** ADAM_STEP)
        d_ref[...] = -ADAM_LR * (m_hat / (jnp.sqrt(v_hat) + ADAM_EPS) + ADAM_WD * w_ref[...])
        nm_ref[...] = nm
        nv_ref[...] = nv

    spec = pl.BlockSpec((rows, tc), lambda i: (0, i))
    return pl.pallas_call(
        body, name=name, grid=(cols // tc,), in_specs=[spec] * 4, out_specs=[spec] * 3,
        out_shape=[jax.ShapeDtypeStruct(w.shape, F32)] * 3, compiler_params=_cp(("arbitrary",)),
    )(w, g, m, v)


def _pad_row(v):
    return jnp.pad(v.reshape(1, -1), ((0, 0), (0, D_MODEL - v.shape[-1])))


def _local_step(x2, target, w_in_p, w_uq_f, wukv_f, w_out_f, q_norm_g, kv_norm_g, ln_g, ln_b):
    seq = x2.shape[0]
    wuq_e = jnp.pad(w_uq_f.reshape(Q_LORA, MLA_HEADS, 96), ((0, 0), (0, 0), (0, 32))).reshape(Q_LORA, MLA_HEADS * LANES)
    ct, st, cd, sd = _rope_tables(seq)
    gq = q_norm_g.reshape(1, Q_LORA)
    gkv = kv_norm_g.reshape(1, KV_LORA)

    cq, ckv, gates, qr, krot, vb, q_e, k_e, v_e = _proj(x2, w_in_p, gq, gkv, wuq_e, wukv_f, ct, st, cd, sd)
    o_a, lse_a = _mla_fwd(q_e, k_e, v_e)
    o_b, lse_b = _dil_fwd(qr, krot, vb)

    dz, d_o, d_gates, dw_out, dln_g, dln_b, loss_part = _post(
        x2, o_a, o_b, gates, w_out_f, ln_g.reshape(1, D_MODEL), ln_b.reshape(1, D_MODEL), target)
    dq_e, dk_e, dv_e = _mla_bwd(q_e, k_e, v_e, d_o, o_a, lse_a)
    dqr, dkr, dvb = _dil_bwd(qr, krot, vb, d_o, o_b, lse_b)
    dcq, dckv, dkrope, dwuq_e, dwukv, dgq, dgkv = _mla_pre_bwd(cq, ckv, gq, gkv, wuq_e, wukv_f, ct, st, dq_e, dk_e, dv_e)
    grad_x, dh = _in_bwd(dz, dcq, dckv, d_gates, dqr, dkr, dvb, dkrope, cd, sd, w_in_p)
    dw_in = _unpermute_dw_in_t(_dw_in(x2, dh))
    dw_uq = dwuq_e.reshape(Q_LORA, MLA_HEADS, LANES)[:, :, :96].reshape(Q_LORA, MLA_HEADS * 96)
    return loss_part, grad_x, dw_in, dw_uq, dwukv, dw_out, dgq, dgkv, dln_g, dln_b


def kernel(x, w_in, q_norm_g, kv_norm_g, w_uq, w_ukv, w_out, ln_g, ln_b, loss_target, m_w_in, m_q_norm_g, m_kv_norm_g, m_w_uq, m_w_ukv, m_w_out, m_ln_g, m_ln_b, v_w_in, v_q_norm_g, v_kv_norm_g, v_w_uq, v_w_ukv, v_w_out, v_ln_g, v_ln_b):
    seq = x.shape[1]
    x2 = x.reshape(seq, D_MODEL)
    target = loss_target.reshape(seq, D_MODEL)

    g_w_in, g_w_uq, g_w_ukv, g_w_out = _all_gather_weights([w_in, w_uq, w_ukv, w_out])
    by_cols = lambda g: jnp.concatenate([g[j] for j in range(N_SHARD)], axis=1)
    loss_part, grad_x, dw_in, dw_uq, dwukv, dw_out, dgq, dgkv, dln_g, dln_b = _local_step(
        x2, target, _permute_w_in_shards(g_w_in), by_cols(g_w_uq), by_cols(g_w_ukv), g_w_out.reshape(D_MODEL, D_MODEL),
        q_norm_g, kv_norm_g, ln_g, ln_b)

    to_shards = lambda d: d.reshape(d.shape[0], N_SHARD, d.shape[1] // N_SHARD).transpose(1, 0, 2)
    grads = [dw_in.reshape(N_SHARD, 808, D_MODEL), to_shards(dw_uq), to_shards(dwukv),
             dw_out.reshape(N_SHARD, 256, D_MODEL)]
    small = jnp.concatenate([_pad_row(dgq), _pad_row(dgkv), dln_g, dln_b, _pad_row(loss_part),
                             jnp.zeros((3, D_MODEL), F32)], axis=0)
    *chip_sums, smalls = _reduce_over_sibling(grads, small)
    g_in_t, g_uq, g_ukv, g_out = _reduce_over_chips(chip_sums)
    g_in = g_in_t.T
    small_sum = _sum_smalls(smalls)
    loss = small_sum[4, 0]

    big = [[o.T for o in _adamw(w.T, g.T, m.T, v.T, name)] for w, g, m, v, name in (
        (w_in, g_in, m_w_in, v_w_in, "adamw_w_in"), (w_uq, g_uq, m_w_uq, v_w_uq, "adamw_w_uq"))]
    big += [_adamw(w, g, m, v, name) for w, g, m, v, name in (
        (w_ukv, g_ukv, m_w_ukv, v_w_ukv, "adamw_w_ukv"), (w_out, g_out, m_w_out, v_w_out, "adamw_w_out"))]
    vec = lambda a, b, c_, d: jnp.concatenate([_pad_row(a), _pad_row(b), _pad_row(c_), _pad_row(d),
                                               jnp.zeros((4, D_MODEL), F32)], axis=0)
    sw = vec(q_norm_g, kv_norm_g, ln_g, ln_b)
    sm = vec(m_q_norm_g, m_kv_norm_g, m_ln_g, m_ln_b)
    sv = vec(v_q_norm_g, v_kv_norm_g, v_ln_g, v_ln_b)
    sg = jnp.concatenate([small_sum[:4], jnp.zeros((4, D_MODEL), F32)], axis=0)
    s_delta, s_m, s_v = _adamw(sw, sg, sm, sv, "adamw_vectors")

    def vectors(a):
        return [a[0, :Q_LORA], a[1, :KV_LORA], a[2], a[3]]

    def ordered(bigs, smalls_):
        return [bigs[0], smalls_[0], smalls_[1], bigs[1], bigs[2], bigs[3], smalls_[2], smalls_[3]]

    grads_out = ordered([g_in, g_uq, g_ukv, g_out], vectors(small_sum))
    deltas = ordered([b[0] for b in big], vectors(s_delta))
    new_m = ordered([b[1] for b in big], vectors(s_m))
    new_v = ordered([b[2] for b in big], vectors(s_v))
    return (loss, grad_x.reshape(x.shape), *grads_out, *deltas, *new_m, *new_v)
```

```python
import functools

import jax
import jax.numpy as jnp
import numpy as np
from jax import lax
from jax.experimental import pallas as pl
from jax.experimental.pallas import tpu as pltpu

F32 = jnp.float32
BF16 = jnp.bfloat16

D_MODEL = 1024
ROPE_THETA = 500000.0
BLOCK = 128
NEG = -1e30
RMS_EPS = 1e-6
LN_EPS = 1e-5

MLA_HEADS = 8
MLA_NOPE = 64
MLA_ROPE = 32
Q_LORA = 384
KV_LORA = 256
DIL_HEADS = 8
DIL_HEAD_DIM = 64
DIL_ROT = 16
DIL_DILATIONS = (1, 4, 16)
IN_WIDTH = 3232
IN_WIDTH_PAD = 3328
ONES_LANE = (64, 0)
MLA_SCALE = (MLA_NOPE + MLA_ROPE) ** -0.5
DIL_SCALE = DIL_HEAD_DIM ** -0.5
ALPHA = 2.0 ** 0.25

ADAM_LR = 0.001
ADAM_B1 = 0.9
ADAM_B2 = 0.999
ADAM_EPS = 1e-08
ADAM_WD = 0.01
ADAM_STEP = 10

N_SHARD = 4
SHARD_SHAPES = ((1024, 808), (384, 192), (256, 256), (256, 1024))
GRAD_SHAPES = ((808, 1024), (384, 192), (256, 256), (256, 1024))
GRAD_SPLIT_COLS = (True, False, False, False)
ROW_CHUNK = 64
LANES = 128
VMEM_LIMIT = 56 * 1024 * 1024
MESH = pl.DeviceIdType.MESH

NT = (((1,), (1,)), ((), ()))
TN = (((0,), (0,)), ((), ()))


def _cp(sem=None, vmem=None):
    return pltpu.CompilerParams(dimension_semantics=sem, vmem_limit_bytes=vmem)


def _dot(a, b, dims=None):
    if dims is None:
        return jnp.dot(a, b, preferred_element_type=F32)
    return lax.dot_general(a, b, dims, preferred_element_type=F32)


def _rope_tables(seq):
    f32 = np.float32
    pos = np.arange(seq, dtype=f32)[:, None]
    one, zero = np.ones((seq, 64), f32), np.zeros((seq, 64), f32)

    def cos_sin(dim):
        inv = np.power(f32(ROPE_THETA), -np.arange(0, dim, 2, dtype=f32) / f32(dim)).astype(f32)
        ang = (pos * inv[None, :]).astype(f32)
        return np.cos(ang).astype(f32), np.sin(ang).astype(f32)

    cos, sin = cos_sin(MLA_ROPE)
    ct = np.concatenate([one, cos, cos, zero[:, :32]], axis=1)
    st = np.concatenate([zero, -sin, sin, zero[:, :32]], axis=1)
    cos, sin = cos_sin(DIL_ROT)
    cd = np.concatenate([cos, cos, one[:, :48]], axis=1)
    sd = np.concatenate([-sin, sin, zero[:, :48]], axis=1)
    return tuple(jnp.asarray(t) for t in (ct, st, np.tile(cd, (1, 2)), np.tile(sd, (1, 2))))


W_IN_ORDER = ((0, 640), (672, 1184), (2720, 3232), (1184, 2720), None, (640, 672))


def _permute_w_in(w):
    z = jnp.zeros((w.shape[0], 64), w.dtype)
    parts = [z if r is None else w[:, r[0]:r[1]] for r in W_IN_ORDER]
    return jnp.concatenate(parts + [z[:, :32]], axis=1)


def _permute_w_in_shards(g):
    width = g.shape[2]
    z = jnp.zeros((g.shape[1], 64), g.dtype)
    parts = []
    for r in W_IN_ORDER:
        if r is None:
            parts.append(z)
            continue
        for j in range(N_SHARD):
            lo, hi = max(r[0], width * j), min(r[1], width * (j + 1))
            if lo < hi:
                parts.append(g[j, :, lo - width * j:hi - width * j])
    return jnp.concatenate(parts + [z[:, :32]], axis=1)


def _unpermute_dw_in_t(dw_t):
    return jnp.concatenate([dw_t[0:640], dw_t[3264:3296], dw_t[640:1152], dw_t[1664:3200], dw_t[1152:1664]], axis=0)


def _position():
    return lax.axis_index("x"), lax.axis_index("y"), lax.axis_index("c")


def _halves(c, rows):
    hr = rows // 2
    return pl.ds(pl.multiple_of(c * hr, 8), hr), pl.ds(pl.multiple_of((1 - c) * hr, 8), hr)


def _for_row_chunks(rows, fn):
    def step(i, carry):
        fn(pl.multiple_of(i * ROW_CHUNK, ROW_CHUNK))
        return carry

    lax.fori_loop(0, rows // ROW_CHUNK, step, 0)


def _all_gather_weights(shards):
    n = len(shards)

    def body(*refs):
        ins, outs = refs[:n], refs[n:2 * n]
        send_sems, recv_sems = refs[2 * n:]
        x, y, c = _position()
        me = 2 * x + y
        chips = [(1 - x, y), (x, 1 - y), (1 - x, 1 - y)]
        for a in range(n):
            def cast(r, a=a):
                outs[a][me, pl.ds(r, ROW_CHUNK), :] = ins[a][pl.ds(r, ROW_CHUNK), :].astype(BF16)

            _for_row_chunks(SHARD_SHAPES[a][0], cast)

        def copy(k, a, slot, rows, to):
            ref = outs[a].at[slot, rows]
            return pltpu.make_async_remote_copy(
                src_ref=ref, dst_ref=ref, send_sem=send_sems.at[k * n + a], recv_sem=recv_sems.at[k * n + a],
                device_id=to, device_id_type=MESH)

        half = [_halves(c, SHARD_SHAPES[a][0])[0] for a in range(n)]
        other = [_halves(c, SHARD_SHAPES[a][0])[1] for a in range(n)]
        first = [copy(k, a, me, half[a], (px, py, c)) for k, (px, py) in enumerate(chips) for a in range(n)]
        for cp in first:
            cp.start()
        passed = []
        for k, (px, py) in enumerate(chips):
            for a in range(n):
                copy(k, a, 2 * px + py, half[a], (x, y, c)).wait_recv()
                cp = copy(3 + k, a, 2 * px + py, half[a], (x, y, 1 - c))
                cp.start()
                passed.append(cp)
        for k, (px, py) in enumerate(chips):
            for a in range(n):
                copy(3 + k, a, 2 * px + py, other[a], (x, y, c)).wait_recv()
        for cp in first + passed:
            cp.wait_send()

    vmem = pl.BlockSpec(memory_space=pltpu.VMEM)
    return pl.pallas_call(
        body, name="all_gather_weights",
        out_shape=[jax.ShapeDtypeStruct((N_SHARD,) + s, BF16) for s in SHARD_SHAPES],
        in_specs=[vmem] * n, out_specs=[vmem] * n,
        scratch_shapes=[pltpu.SemaphoreType.DMA((6 * n,)), pltpu.SemaphoreType.DMA((6 * n,))],
        compiler_params=_cp(None, VMEM_LIMIT),
    )(*shards)


def _grad_half_shape(a):
    rows, cols = GRAD_SHAPES[a]
    return (rows, cols // 2) if GRAD_SPLIT_COLS[a] else (rows // 2, cols)


def _grad_half(a, c):
    rows, cols = GRAD_SHAPES[a]
    if GRAD_SPLIT_COLS[a]:
        return slice(None), pl.ds(pl.multiple_of(c * (cols // 2), LANES), cols // 2)
    return pl.ds(pl.multiple_of(c * (rows // 2), ROW_CHUNK), rows // 2), slice(None)


def _grad_chunks(a, c):
    rows, cols = GRAD_SHAPES[a]
    if GRAD_SPLIT_COLS[a]:
        return [((slice(None), pl.ds(c0, LANES)),
                 (slice(None), pl.ds(pl.multiple_of(c * (cols // 2) + c0, LANES), LANES)))
                for c0 in range(0, cols // 2, LANES)]
    return [((pl.ds(r0, ROW_CHUNK), slice(None)),
             (pl.ds(pl.multiple_of(c * (rows // 2) + r0, ROW_CHUNK), ROW_CHUNK), slice(None)))
            for r0 in range(0, rows // 2, ROW_CHUNK)]


def _reduce_over_sibling(grads, small):
    n = len(grads)

    def body(*refs):
        g_hbm, sm = refs[:n], refs[n]
        sums, smalls = refs[n + 1:2 * n + 1], refs[2 * n + 1]
        stage, got = refs[2 * n + 2:3 * n + 2], refs[3 * n + 2:4 * n + 2]
        send_sems, recv_sems, local_sems = refs[4 * n + 2:]
        x, y, c = _position()
        me = 4 * x + 2 * y + c
        loads = [pltpu.make_async_copy(g_hbm[a], stage[a], local_sems.at[a]) for a in range(n)]
        for ld in loads:
            ld.start()
        smalls[me] = sm[...]
        sends = []
        for rel in range(1, 8):
            px = 1 - x if rel // 4 else x
            py = 1 - y if (rel // 2) % 2 else y
            pc = 1 - c if rel % 2 else c
            cp = pltpu.make_async_remote_copy(
                src_ref=sm, dst_ref=smalls.at[me], send_sem=send_sems.at[n + rel], recv_sem=recv_sems.at[n + rel],
                device_id=(px, py, pc), device_id_type=MESH)
            cp.start()
            sends.append((cp, 4 * px + 2 * py + pc))
        swaps = []
        for a in range(n):
            loads[a].wait()
            cp = pltpu.make_async_remote_copy(
                src_ref=stage[a].at[(slice(None),) + _grad_half(a, 1 - c)], dst_ref=got[a], send_sem=send_sems.at[a], recv_sem=recv_sems.at[a],
                device_id=(x, y, 1 - c), device_id_type=MESH)
            cp.start()
            swaps.append(cp)
        for a in range(n):
            swaps[a].wait_recv()
            for k in range(N_SHARD):
                for in_half, in_whole in _grad_chunks(a, c):
                    pair = stage[a][(k,) + in_whole] + got[a][(k,) + in_half]
                    sums[a][(k,) + in_half] = pair.astype(BF16)
        for rel, (cp, peer) in enumerate(sends, start=1):
            pltpu.make_async_remote_copy(
                src_ref=sm, dst_ref=smalls.at[peer], send_sem=send_sems.at[n + rel], recv_sem=recv_sems.at[n + rel],
                device_id=(x, y, c), device_id_type=MESH).wait_recv()
        for cp in swaps:
            cp.wait_send()
        for cp, _ in sends:
            cp.wait_send()

    vmem = pl.BlockSpec(memory_space=pltpu.VMEM)
    half = [(N_SHARD,) + _grad_half_shape(a) for a in range(n)]
    return pl.pallas_call(
        body, name="reduce_over_sibling",
        out_shape=[jax.ShapeDtypeStruct(s, BF16) for s in half] + [jax.ShapeDtypeStruct((8,) + small.shape, F32)],
        in_specs=[pl.BlockSpec(memory_space=pl.ANY)] * n + [vmem], out_specs=[vmem] * (n + 1),
        scratch_shapes=[pltpu.VMEM((N_SHARD,) + s, F32) for s in GRAD_SHAPES] + [pltpu.VMEM(s, F32) for s in half]
        + [pltpu.SemaphoreType.DMA((n + 8,)), pltpu.SemaphoreType.DMA((n + 8,)), pltpu.SemaphoreType.DMA((n,))],
        compiler_params=_cp(None, VMEM_LIMIT),
    )(*grads, small)


def _reduce_over_chips(sums):
    n = len(sums)

    def body(*refs):
        h, outs, got = refs[:n], refs[n:2 * n], refs[2 * n:3 * n]
        send_sems, recv_sems = refs[3 * n:]
        x, y, c = _position()
        me = 2 * x + y
        chips = [(1 - x, y), (x, 1 - y), (1 - x, 1 - y)]
        sends = []
        for k, (px, py) in enumerate(chips):
            for a in range(n):
                cp = pltpu.make_async_remote_copy(
                    src_ref=h[a].at[2 * px + py], dst_ref=got[a].at[k], send_sem=send_sems.at[k * n + a],
                    recv_sem=recv_sems.at[k * n + a], device_id=(px, py, c), device_id_type=MESH)
                cp.start()
                sends.append(cp)
        for cp in sends:
            cp.wait_recv()
        joins = []
        for a in range(n):
            for in_half, in_whole in _grad_chunks(a, c):
                total = h[a][(me,) + in_half].astype(F32)
                for k in range(3):
                    total = total + got[a][(k,) + in_half].astype(F32)
                outs[a][in_whole] = total
            half = outs[a].at[_grad_half(a, c)]
            cp = pltpu.make_async_remote_copy(
                src_ref=half, dst_ref=half, send_sem=send_sems.at[3 * n + a],
                recv_sem=recv_sems.at[3 * n + a], device_id=(x, y, 1 - c), device_id_type=MESH)
            cp.start()
            joins.append(cp)
        for a in range(n):
            other = outs[a].at[_grad_half(a, 1 - c)]
            pltpu.make_async_remote_copy(
                src_ref=other, dst_ref=other, send_sem=send_sems.at[3 * n + a],
                recv_sem=recv_sems.at[3 * n + a], device_id=(x, y, c), device_id_type=MESH).wait_recv()
        for cp in sends + joins:
            cp.wait_send()

    vmem = pl.BlockSpec(memory_space=pltpu.VMEM)
    return pl.pallas_call(
        body, name="reduce_over_chips",
        out_shape=[jax.ShapeDtypeStruct(s, F32) for s in GRAD_SHAPES],
        in_specs=[vmem] * n, out_specs=[vmem] * n,
        scratch_shapes=[pltpu.VMEM((3,) + _grad_half_shape(a), BF16) for a in range(n)]
        + [pltpu.SemaphoreType.DMA((4 * n,)), pltpu.SemaphoreType.DMA((4 * n,))],
        compiler_params=_cp(None, VMEM_LIMIT),
    )(*sums)


def _sum_smalls(smalls):
    def body(s, o):
        acc = s[0]
        for d in range(1, 8):
            acc = acc + s[d]
        o[...] = acc

    return pl.pallas_call(body, name="sum_smalls", out_shape=jax.ShapeDtypeStruct(smalls.shape[1:], F32))(smalls)


def _proj(x, w_in_p, gq, gkv, wuq_e, wukv, ct, st, cd, sd):
    seq = x.shape[0]
    tr = 512

    def body(x_ref, w_ref, gq_ref, gkv_ref, wuq_ref, wukv_ref, ct_ref, st_ref, cd_ref, sd_ref,
             cq_ref, ckv_ref, g_ref, qr_ref, kr_ref, vb_ref, q_out, k_out, v_out):
        lane = lax.broadcasted_iota(jnp.int32, (tr, LANES), 1)
        xb = x_ref[...].astype(BF16)
        cq = _dot(xb, w_ref[:, 0:384])
        ckv = _dot(xb, w_ref[:, 384:640])
        cq_ref[...] = cq
        ckv_ref[...] = ckv
        g_ref[...] = _dot(xb, w_ref[:, 640:1664])

        cd_, sd_ = cd_ref[...], sd_ref[...]
        qb = _dot(xb, w_ref[:, 1664:2176])
        kb = _dot(xb, w_ref[:, 2176:2688])
        for p in range(4):
            cols = slice(LANES * p, LANES * (p + 1))
            t = qb[:, cols]
            qr_ref[:, cols] = (t * cd_ + _dil_rot(t, lane) * sd_) * DIL_SCALE
            t = kb[:, cols]
            kr_ref[:, cols] = t * cd_ + _dil_rot(t, lane) * sd_
        vb_ref[...] = _dot(xb, w_ref[:, 2688:3200])

        ct_, st_ = ct_ref[...], st_ref[...]

        def rope(t):
            return t * ct_ + _mla_rot(t, lane) * st_

        _, qn = _rms(cq, gq_ref[...])
        q_all = _dot(qn.astype(BF16), wuq_ref[...])
        for h in range(MLA_HEADS):
            q_out[h] = (rope(q_all[:, LANES * h:LANES * (h + 1)]) * MLA_SCALE).astype(BF16)
        _, kvn = _rms(ckv, gkv_ref[...])
        kv_all = _dot(kvn.astype(BF16), wukv_ref[...])
        kpe = rope(_dot(xb, w_ref[:, 3200:3328]))
        for h in range(MLA_HEADS):
            kv_h = kv_all[:, LANES * h:LANES * (h + 1)]
            k_out[h] = jnp.where(lane < 64, kv_h, kpe).astype(BF16)
            if h % 2:
                v = jnp.where(lane >= 64, kv_h, 0.0)
            else:
                v = jnp.where(lane < 64, pltpu.roll(kv_h, 64, 1), 0.0)
            v_out[h] = jnp.where(lane == ONES_LANE[h % 2], 1.0, v).astype(BF16)

    row = lambda w: pl.BlockSpec((tr, w), lambda i: (i, 0))
    full = lambda a: pl.BlockSpec(a.shape, lambda i: (0,) * a.ndim)
    head = pl.BlockSpec((MLA_HEADS, tr, LANES), lambda i: (0, i, 0))
    widths = (Q_LORA, KV_LORA, D_MODEL, 512, 512, 512)
    return pl.pallas_call(
        body, name="proj", grid=(seq // tr,),
        in_specs=[row(D_MODEL), full(w_in_p), full(gq), full(gkv), full(wuq_e), full(wukv)] + [row(LANES)] * 4,
        out_specs=[row(w) for w in widths] + [head] * 3,
        out_shape=[jax.ShapeDtypeStruct((seq, w), F32) for w in widths]
        + [jax.ShapeDtypeStruct((MLA_HEADS, seq, LANES), BF16)] * 3,
        compiler_params=_cp(("arbitrary",), VMEM_LIMIT),
    )(x, w_in_p, gq, gkv, wuq_e, wukv, ct, st, cd, sd)


def _mla_rot(t, lane):
    return jnp.where(lane < 80, pltpu.roll(t, 112, 1), pltpu.roll(t, 16, 1))


def _dil_rot(t, lane):
    return jnp.where(lane % 64 < 8, pltpu.roll(t, 120, 1), pltpu.roll(t, 8, 1))


def _rms(c, g):
    r = lax.rsqrt(jnp.mean(c * c, axis=-1, keepdims=True) + RMS_EPS)
    return r, c * r * g


def _mla_fwd(q, k, v):
    seq = q.shape[1]
    tq = 512
    nq = seq // tq

    def body(q_ref, k_ref, v_ref, o_ref, lse_ref, m_s, acc_s, s_buf):
        i = pl.program_id(1)
        row = lax.broadcasted_iota(jnp.int32, (tq, tq), 0)
        col = lax.broadcasted_iota(jnp.int32, (tq, tq), 1)
        lane = lax.broadcasted_iota(jnp.int32, (tq, LANES), 1)
        m_s[...] = jnp.full((2, tq, LANES), NEG, F32)
        acc_s[...] = jnp.zeros((2, tq, LANES), F32)

        def block(j):
            return pl.ds(pl.multiple_of(j * tq, tq), tq)

        def scores(hh, j):
            return _dot(q_ref[hh], k_ref[hh, block(j), :], NT)

        def consume(hh, j, s):
            m_prev = m_s[hh]
            m_new = jnp.maximum(m_prev, jnp.max(s, axis=1, keepdims=True))
            p = jnp.exp(s - m_new[:, :1])
            acc_s[hh] = jnp.exp(m_prev - m_new) * acc_s[hh] + _dot(p.astype(BF16), v_ref[hh, block(j), :])
            m_s[hh] = m_new

        for hh in range(2):
            s_buf[0, hh] = scores(hh, 0)

        def full_step(j, carry):
            slot = j & 1
            for hh in range(2):
                s = s_buf[slot, hh]
                s_buf[1 - slot, hh] = scores(hh, j + 1)
                consume(hh, j, s)
            return carry

        lax.fori_loop(0, i, full_step, 0)
        total = jnp.zeros((tq, LANES), F32)
        for hh in range(2):
            consume(hh, i, jnp.where(col <= row, s_buf[i & 1, hh], NEG))
            acc = acc_s[hh]
            l = acc[:, ONES_LANE[hh]:ONES_LANE[hh] + 1]
            mine = (lane >= 64) if hh else (lane < 64)
            total = total + jnp.where(mine, acc / l, 0.0)
            lse_ref[hh] = m_s[hh] + jnp.log(l)
        o_ref[...] = total

    kv_spec = pl.BlockSpec((2, seq, LANES), lambda p, i: (p, 0, 0))
    return pl.pallas_call(
        body, name="mla_fwd", grid=(MLA_HEADS // 2, nq),
        in_specs=[pl.BlockSpec((2, tq, LANES), lambda p, i: (p, i, 0)), kv_spec, kv_spec],
        out_specs=[pl.BlockSpec((tq, LANES), lambda p, i: (i, p)), pl.BlockSpec((2, tq, LANES), lambda p, i: (p, i, 0))],
        out_shape=[jax.ShapeDtypeStruct((seq, 4 * LANES), F32), jax.ShapeDtypeStruct((MLA_HEADS, seq, LANES), F32)],
        scratch_shapes=[pltpu.VMEM((2, tq, LANES), F32), pltpu.VMEM((2, tq, LANES), F32),
                        pltpu.VMEM((2, 2, tq, tq), F32)],
        compiler_params=_cp(("arbitrary", "arbitrary"), VMEM_LIMIT),
    )(q, k, v)


DIL_Q_FWD = 2 * BLOCK
DIL_Q_BWD = BLOCK


def _dil_tile_index(t, d, seq, nq):
    per_class = seq // (nq * d)
    shift = per_class.bit_length() - 1
    r = t >> shift
    n = t & (per_class - 1)
    start = r + (nq * d) * n
    prev = jnp.maximum(start - BLOCK * d, r)
    if d == 1:
        start = pl.multiple_of(start, nq)
        prev = pl.multiple_of(prev, BLOCK)
    return (n == 0).astype(jnp.int32), start, prev


def _dil_rows(start, d, size):
    return pl.ds(start, size) if d == 1 else pl.ds(start, size, stride=d)


def _dil_bias(nq):
    i = lax.broadcasted_iota(jnp.int32, (2 * nq, BLOCK + nq), 0) % nq
    j = lax.broadcasted_iota(jnp.int32, (2 * nq, BLOCK + nq), 1)
    band = (j >= i) & (j <= i + BLOCK)
    return jnp.where(band, 0.0, NEG), jnp.where(band & (j >= BLOCK), 0.0, NEG)


def _stack_heads(t, lane):
    return jnp.concatenate([jnp.where(lane < 64, t, 0.0), jnp.where(lane >= 64, t, 0.0)], axis=0)


def _unstack_heads(t, lane):
    nq = t.shape[0] // 2
    return jnp.where(lane < 64, t[:nq], t[nq:])


def _dil_fwd(qr, kr, vb):
    seq = qr.shape[0]
    nq = DIL_Q_FWD
    n_tiles = seq // nq
    assert seq % (nq * max(DIL_DILATIONS)) == 0

    def body(q_ref, k_ref, v_ref, o_ref, lse_ref, m_s, l_s, n_s, bias_s):
        lane = lax.broadcasted_iota(jnp.int32, (nq, LANES), 1)
        bias_s[0], bias_s[1] = _dil_bias(nq)
        for bi, d in enumerate(DIL_DILATIONS):

            def tile(t, carry, d=d, bi=bi):
                first, start, prev = _dil_tile_index(t, d, seq, nq)
                rows, prows = _dil_rows(start, d, nq), _dil_rows(prev, d, BLOCK)
                qst = _stack_heads(q_ref[rows, :], lane).astype(BF16)
                if seq == nq * d:
                    kcat, vcat = k_ref[rows, :].astype(BF16), v_ref[rows, :].astype(BF16)
                    s = _dot(qst, kcat, NT) + bias_s[1, :, BLOCK:]
                else:
                    kcat = jnp.concatenate([k_ref[prows, :], k_ref[rows, :]], axis=0).astype(BF16)
                    vcat = jnp.concatenate([v_ref[prows, :], v_ref[rows, :]], axis=0).astype(BF16)
                    s = _dot(qst, kcat, NT) + bias_s[first]
                m = jnp.max(s, axis=1, keepdims=True)
                p = jnp.exp(s - m)
                l2 = _unstack_heads(jnp.sum(p, axis=1, keepdims=True) + jnp.zeros((2 * nq, LANES), F32), lane)
                m2 = _unstack_heads(m + jnp.zeros((2 * nq, LANES), F32), lane)
                num2 = _unstack_heads(_dot(p.astype(BF16), vcat), lane)
                if bi == 0:
                    m_s[rows, :] = m2
                    l_s[rows, :] = l2
                    n_s[rows, :] = num2
                else:
                    m_old = m_s[rows, :]
                    m_new = jnp.maximum(m_old, m2)
                    a = jnp.exp(m_old - m_new)
                    b = jnp.exp(m2 - m_new)
                    m_s[rows, :] = m_new
                    l_s[rows, :] = a * l_s[rows, :] + b * l2
                    n_s[rows, :] = a * n_s[rows, :] + b * num2
                return carry

            lax.fori_loop(0, n_tiles, tile, 0, unroll=2)
        o_ref[...] = n_s[...] / l_s[...]
        lse_ref[...] = m_s[...] + jnp.log(l_s[...])

    col = lambda off: pl.BlockSpec((seq, LANES), lambda p: (0, p + off))
    return pl.pallas_call(
        body, name="dil_fwd", grid=(4,),
        in_specs=[col(0), col(0), col(0)],
        out_specs=[col(0), pl.BlockSpec((None, seq, LANES), lambda p: (p, 0, 0))],
        out_shape=[jax.ShapeDtypeStruct((seq, 4 * LANES), F32), jax.ShapeDtypeStruct((4, seq, LANES), F32)],
        scratch_shapes=[pltpu.VMEM((seq, LANES), F32)] * 3 + [pltpu.VMEM((2, 2 * nq, BLOCK + nq), F32)],
        compiler_params=_cp(("arbitrary",), VMEM_LIMIT),
    )(qr, kr, vb)


def _post(x, o_a, o_b, gates, w_out, ln_g, ln_b, target):
    seq = x.shape[0]
    tr = 512

    def body(x_ref, oa_ref, ob_ref, g_ref, w_ref, lg_ref, lb_ref, t_ref,
             dz_ref, do_ref, dg_ref, dw_ref, dlg_ref, dlb_ref, loss_ref):
        @pl.when(pl.program_id(0) == 0)
        def _():
            dw_ref[...] = jnp.zeros_like(dw_ref)
            dlg_ref[...] = jnp.zeros_like(dlg_ref)
            dlb_ref[...] = jnp.zeros_like(dlb_ref)
            loss_ref[...] = jnp.zeros_like(loss_ref)

        g = g_ref[...]
        sg = 1.0 / (1.0 + jnp.exp(-g))
        silu = g * sg
        o = jnp.concatenate([oa_ref[...], ob_ref[...]], axis=1)
        mixb = (o * silu).astype(BF16)
        w = w_ref[...]
        z = ALPHA * x_ref[...] + _dot(mixb, w)
        mu = jnp.mean(z, axis=-1, keepdims=True)
        zc = z - mu
        rstd = lax.rsqrt(jnp.mean(zc * zc, axis=-1, keepdims=True) + LN_EPS)
        xhat = zc * rstd
        lg = lg_ref[...]
        err = xhat * lg + lb_ref[...] - t_ref[...]
        loss_ref[...] += jnp.sum(err * err) * (0.5 / D_MODEL)
        dy = err * (1.0 / D_MODEL)
        dlg_ref[...] += jnp.sum(dy * xhat, axis=0, keepdims=True)
        dlb_ref[...] += jnp.sum(dy, axis=0, keepdims=True)
        dxh = dy * lg
        dz = rstd * (dxh - jnp.mean(dxh, axis=-1, keepdims=True) - xhat * jnp.mean(dxh * xhat, axis=-1, keepdims=True))
        dz_ref[...] = dz
        dzb = dz.astype(BF16)
        dmix = _dot(dzb, w, NT)
        do_ref[...] = dmix * silu
        dg_ref[...] = (dmix * o * (sg * (1.0 + g * (1.0 - sg)))).astype(BF16)
        dw_ref[...] += _dot(mixb, dzb, TN)

    row = lambda w: pl.BlockSpec((tr, w), lambda i: (i, 0))
    full = lambda s: pl.BlockSpec(s, lambda i: (0, 0))
    return pl.pallas_call(
        body, name="post", grid=(seq // tr,),
        in_specs=[row(D_MODEL), row(512), row(512), row(D_MODEL), full((D_MODEL, D_MODEL)), full((1, D_MODEL)),
                  full((1, D_MODEL)), row(D_MODEL)],
        out_specs=[row(D_MODEL), row(D_MODEL), row(D_MODEL), full((D_MODEL, D_MODEL)), full((1, D_MODEL)),
                   full((1, D_MODEL)), full((1, LANES))],
        out_shape=[jax.ShapeDtypeStruct((seq, D_MODEL), F32), jax.ShapeDtypeStruct((seq, D_MODEL), F32),
                   jax.ShapeDtypeStruct((seq, D_MODEL), BF16), jax.ShapeDtypeStruct((D_MODEL, D_MODEL), F32),
                   jax.ShapeDtypeStruct((1, D_MODEL), F32), jax.ShapeDtypeStruct((1, D_MODEL), F32),
                   jax.ShapeDtypeStruct((1, LANES), F32)],
        compiler_params=_cp(("arbitrary",), VMEM_LIMIT),
    )(x, o_a, o_b, gates, w_out, ln_g, ln_b, target)


def _mla_bwd(q, k, v, d_o, o, lse):
    seq = q.shape[1]
    tq = 512
    nq = seq // tq

    def body(q_ref, k_ref, v_ref, do_ref, o_ref, lse_ref, dq_ref, dk_ref, dv_ref, d_s, lse_s, dk_s, dv_s, v_s, kt_s, dqt_s):
        j = pl.program_id(1)
        lane = lax.broadcasted_iota(jnp.int32, (tq, LANES), 1)
        row = lax.broadcasted_iota(jnp.int32, (tq, tq), 0)
        col = lax.broadcasted_iota(jnp.int32, (tq, tq), 1)

        @pl.when(j == 0)
        def _():
            dqt_s[...] = jnp.zeros_like(dqt_s)

            def rowsum(i, carry):
                rows = pl.ds(pl.multiple_of(i * tq, tq), tq)
                prod = do_ref[rows, :] * o_ref[rows, :]
                for hh in range(2):
                    mine = (lane >= 64) if hh else (lane < 64)
                    total = jnp.sum(jnp.where(mine, prod, 0.0), axis=1, keepdims=True)
                    d_s[hh, i] = jnp.transpose(total + jnp.zeros((tq, LANES), F32))[:8]
                    lse_s[hh, i] = jnp.transpose(lse_ref[hh, rows, :])[:8]
                return carry

            lax.fori_loop(0, nq, rowsum, 0)

        dk_s[...] = jnp.zeros_like(dk_s)
        dv_s[...] = jnp.zeros_like(dv_s)
        for hh in range(2):
            v_s[hh] = jnp.where(lane == ONES_LANE[hh], 0.0, v_ref[hh].astype(F32)).astype(BF16)
            kt_s[hh] = jnp.transpose(k_ref[hh].astype(F32)).astype(BF16)

        def step(i, masked):
            rows = pl.ds(pl.multiple_of(i * tq, tq), tq)
            dob = do_ref[rows, :].astype(BF16)
            for hh in range(2):
                qb, kb, vb = q_ref[hh, rows, :], k_ref[hh], v_s[hh]
                p = jnp.exp(_dot(kb, qb, NT) - lse_s[hh, i][:1])
                if masked:
                    p = jnp.where(row <= col, p, 0.0)
                dv_s[hh] += _dot(p.astype(BF16), dob)
                ds = (p * (_dot(vb, dob, NT) - d_s[hh, i][:1])).astype(BF16)
                dk_s[hh] += _dot(ds, qb)
                dqt_s[hh, i] += _dot(kt_s[hh], ds)

        def full_step(i, carry):
            step(i, False)
            return carry

        step(j, True)
        lax.fori_loop(j + 1, nq, full_step, 0)
        dk_ref[...] = dk_s[...]
        dv_ref[...] = dv_s[...]

        @pl.when(j == nq - 1)
        def _():
            def untranspose(i, carry):
                rows = pl.ds(pl.multiple_of(i * tq, tq), tq)
                for hh in range(2):
                    dq_ref[hh, rows, :] = jnp.transpose(dqt_s[hh, i])
                return carry

            lax.fori_loop(0, nq, untranspose, 0)

    whole = pl.BlockSpec((2, seq, LANES), lambda p, j: (p, 0, 0))
    blk = pl.BlockSpec((2, tq, LANES), lambda p, j: (p, j, 0))
    pair = pl.BlockSpec((seq, LANES), lambda p, j: (0, p))
    shape = jax.ShapeDtypeStruct((MLA_HEADS, seq, LANES), F32)
    return pl.pallas_call(
        body, name="mla_bwd", grid=(MLA_HEADS // 2, nq),
        in_specs=[whole, blk, blk, pair, pair, whole],
        out_specs=[whole, blk, blk], out_shape=[shape] * 3,
        scratch_shapes=[pltpu.VMEM((2, nq, 8, tq), F32), pltpu.VMEM((2, nq, 8, tq), F32),
                        pltpu.VMEM((2, tq, LANES), F32), pltpu.VMEM((2, tq, LANES), F32),
                        pltpu.VMEM((2, tq, LANES), BF16), pltpu.VMEM((2, LANES, tq), BF16),
                        pltpu.VMEM((2, nq, LANES, tq), F32)],
        compiler_params=_cp(("arbitrary", "arbitrary"), VMEM_LIMIT),
    )(q, k, v, d_o, o, lse)


def _dil_bwd(qr, kr, vb, d_o, o, lse):
    seq = qr.shape[0]
    nq = DIL_Q_BWD
    n_tiles = seq // nq
    chunk = 512

    def body(q_ref, k_ref, v_ref, do_ref, o_ref, lse_ref, dq_ref, dk_ref, dv_ref, d_s, dq_s, dk_s, dv_s, bias_s):
        lane = lax.broadcasted_iota(jnp.int32, (nq, LANES), 1)
        lanec = lax.broadcasted_iota(jnp.int32, (chunk, LANES), 1)
        bias_s[0], bias_s[1] = [b[:nq] for b in _dil_bias(nq)]

        def rowsum(i, carry):
            rows = pl.ds(pl.multiple_of(i * chunk, chunk), chunk)
            prod = do_ref[rows, :] * o_ref[rows, :]
            lo = jnp.sum(jnp.where(lanec < 64, prod, 0.0), axis=1, keepdims=True)
            hi = jnp.sum(jnp.where(lanec >= 64, prod, 0.0), axis=1, keepdims=True)
            d_s[rows, :] = jnp.where(lanec < 64, lo, hi)
            return carry

        lax.fori_loop(0, seq // chunk, rowsum, 0)
        dq_s[...] = jnp.zeros_like(dq_s)
        dk_s[...] = jnp.zeros_like(dk_s)
        dv_s[...] = jnp.zeros_like(dv_s)
        for d in DIL_DILATIONS:

            def tile(t, carry, d=d):
                first, start, prev = _dil_tile_index(t, d, seq, nq)
                rows, prows = _dil_rows(start, d, nq), _dil_rows(prev, d, BLOCK)
                q_t, do_t = q_ref[rows, :], do_ref[rows, :]
                lse_t, d_t = lse_ref[rows, :], d_s[rows, :]
                kcat = jnp.concatenate([k_ref[prows, :], k_ref[rows, :]], axis=0).astype(BF16)
                vcat = jnp.concatenate([v_ref[prows, :], v_ref[rows, :]], axis=0).astype(BF16)
                bias = bias_s[first]
                dq_t = jnp.zeros((nq, LANES), F32)
                dkcat = jnp.zeros((BLOCK + nq, LANES), F32)
                dvcat = jnp.zeros((BLOCK + nq, LANES), F32)
                for hh in range(2):
                    mine = (lane >= 64) if hh else (lane < 64)
                    c0 = 64 * hh
                    qh = jnp.where(mine, q_t, 0.0).astype(BF16)
                    doh = jnp.where(mine, do_t, 0.0).astype(BF16)
                    p = jnp.exp(_dot(qh, kcat, NT) + bias - lse_t[:, c0:c0 + 1])
                    dvcat = dvcat + _dot(p.astype(BF16), doh, TN)
                    dp = _dot(doh, vcat, NT)
                    ds = (p * (dp - d_t[:, c0:c0 + 1])).astype(BF16)
                    dq_t = dq_t + jnp.where(mine, _dot(ds, kcat), 0.0)
                    dkcat = dkcat + _dot(ds, qh, TN)
                dq_s[rows, :] += dq_t
                dk_s[prows, :] += dkcat[:BLOCK]
                dk_s[rows, :] += dkcat[BLOCK:]
                dv_s[prows, :] += dvcat[:BLOCK]
                dv_s[rows, :] += dvcat[BLOCK:]
                return carry

            lax.fori_loop(0, n_tiles, tile, 0, unroll=4)
        dq_ref[...] = dq_s[...].astype(BF16)
        dk_ref[...] = dk_s[...].astype(BF16)
        dv_ref[...] = dv_s[...].astype(BF16)

    col = lambda off: pl.BlockSpec((seq, LANES), lambda p: (0, p + off))
    shape = jax.ShapeDtypeStruct((seq, 4 * LANES), BF16)
    return pl.pallas_call(
        body, name="dil_bwd", grid=(4,),
        in_specs=[col(0), col(0), col(0), col(4), col(0), pl.BlockSpec((None, seq, LANES), lambda p: (p, 0, 0))],
        out_specs=[col(0)] * 3, out_shape=[shape] * 3,
        scratch_shapes=[pltpu.VMEM((seq, LANES), F32)] * 4 + [pltpu.VMEM((2, nq, BLOCK + nq), F32)],
        compiler_params=_cp(("arbitrary",), VMEM_LIMIT),
    )(qr, kr, vb, d_o, o, lse)


def _in_bwd(dz, cq, ckv, gq, gkv, wuq_e, wukv, ct, st, dq, dk, dv, dgates, dqr, dkr, dvb, cd, sd, w_in_p):
    seq = dz.shape[0]
    tr = 512

    def body(dz_ref, cq_ref, ckv_ref, gq_ref, gkv_ref, wuq_ref, wukv_ref, ct_ref, st_ref, dq_ref, dk_ref, dv_ref,
             dg_ref, dqr_ref, dkr_ref, dvb_ref, cd_ref, sd_ref, w_ref,
             gx_ref, dh_ref, dwuq_ref, dwukv_ref, dgq_ref, dgkv_ref):
        @pl.when(pl.program_id(0) == 0)
        def _():
            dwuq_ref[...] = jnp.zeros_like(dwuq_ref)
            dwukv_ref[...] = jnp.zeros_like(dwukv_ref)
            dgq_ref[...] = jnp.zeros_like(dgq_ref)
            dgkv_ref[...] = jnp.zeros_like(dgkv_ref)

        lane = lax.broadcasted_iota(jnp.int32, (tr, LANES), 1)
        rope_lanes = jnp.logical_and(lane >= 64, lane < 96)
        ct_, st_ = ct_ref[...], st_ref[...]

        def mla_rope_t(g):
            return ct_ * g + jnp.where(rope_lanes, _mla_rot(st_ * g, lane), 0.0)

        def norm_bwd(c, g, dn, dg_ref):
            r, _ = _rms(c, g)
            u = dn * g
            dg_ref[...] += jnp.sum(dn * c * r, axis=0, keepdims=True)
            return r * u - c * (r * r * r) * jnp.mean(u * c, axis=-1, keepdims=True)

        c, g = cq_ref[...], gq_ref[...]
        _, qn = _rms(c, g)
        dq_all = jnp.concatenate([mla_rope_t(dq_ref[h] * MLA_SCALE) for h in range(MLA_HEADS)], axis=1).astype(BF16)
        dwuq_ref[...] += _dot(qn.astype(BF16), dq_all, TN)
        dcq = norm_bwd(c, g, _dot(dq_all, wuq_ref[...], NT), dgq_ref).astype(BF16)

        c, g = ckv_ref[...], gkv_ref[...]
        _, kvn = _rms(c, g)
        dkpe = jnp.zeros((tr, LANES), F32)
        parts = []
        for h in range(MLA_HEADS):
            dk_h, dv_h = dk_ref[h], dv_ref[h]
            if h % 2 == 0:
                dv_h = pltpu.roll(dv_h, 64, 1)
            parts.append(jnp.where(lane < 64, dk_h, dv_h))
            dkpe = dkpe + jnp.where(rope_lanes, dk_h, 0.0)
        dkv_all = jnp.concatenate(parts, axis=1).astype(BF16)
        dwukv_ref[...] += _dot(kvn.astype(BF16), dkv_all, TN)
        dckv = norm_bwd(c, g, _dot(dkv_all, wukv_ref[...], NT), dgkv_ref).astype(BF16)
        dkrope = mla_rope_t(dkpe).astype(BF16)

        rot_lanes = lane % 64 < DIL_ROT
        cd_, sd_ = cd_ref[...], sd_ref[...]

        def dil_rope_t(g):
            return cd_ * g + jnp.where(rot_lanes, _dil_rot(sd_ * g, lane), 0.0)

        dqb = [dil_rope_t(dqr_ref[:, LANES * p:LANES * (p + 1)].astype(F32) * DIL_SCALE).astype(BF16) for p in range(4)]
        dkb = [dil_rope_t(dkr_ref[:, LANES * p:LANES * (p + 1)].astype(F32)).astype(BF16) for p in range(4)]
        dh = jnp.concatenate([dcq, dckv, dg_ref[...]] + dqb + dkb + [dvb_ref[...], dkrope], axis=1)
        dh_ref[...] = dh
        gx_ref[...] = ALPHA * dz_ref[...] + _dot(dh, w_ref[...], NT)

    row = lambda w: pl.BlockSpec((tr, w), lambda i: (i, 0))
    full = lambda a: pl.BlockSpec(a.shape, lambda i: (0,) * a.ndim)
    head = pl.BlockSpec((MLA_HEADS, tr, LANES), lambda i: (0, i, 0))
    return pl.pallas_call(
        body, name="in_bwd", grid=(seq // tr,),
        in_specs=[row(D_MODEL), row(Q_LORA), row(KV_LORA), full(gq), full(gkv), full(wuq_e), full(wukv), row(LANES),
                  row(LANES), head, head, head, row(D_MODEL), row(512), row(512), row(512), row(LANES), row(LANES),
                  full(w_in_p)],
        out_specs=[row(D_MODEL), row(IN_WIDTH_PAD), full(wuq_e), full(wukv), full(gq), full(gkv)],
        out_shape=[jax.ShapeDtypeStruct((seq, D_MODEL), F32), jax.ShapeDtypeStruct((seq, IN_WIDTH_PAD), BF16),
                   jax.ShapeDtypeStruct(wuq_e.shape, F32), jax.ShapeDtypeStruct(wukv.shape, F32),
                   jax.ShapeDtypeStruct(gq.shape, F32), jax.ShapeDtypeStruct(gkv.shape, F32)],
        compiler_params=_cp(("arbitrary",), VMEM_LIMIT),
    )(dz, cq, ckv, gq, gkv, wuq_e, wukv, ct, st, dq, dk, dv, dgates, dqr, dkr, dvb, cd, sd, w_in_p)


def _dw_in(x, dh):
    seq = dh.shape[0]
    tk = 512
    tn = IN_WIDTH_PAD // 2

    def body(x_ref, dh_ref, o_ref):
        @pl.when(pl.program_id(1) == 0)
        def _():
            o_ref[...] = jnp.zeros_like(o_ref)

        o_ref[...] += _dot(dh_ref[...], x_ref[...].astype(BF16), TN)

    return pl.pallas_call(
        body, name="dw_in", grid=(2, seq // tk),
        in_specs=[pl.BlockSpec((tk, D_MODEL), lambda n, k: (k, 0)), pl.BlockSpec((tk, tn), lambda n, k: (k, n))],
        out_specs=pl.BlockSpec((tn, D_MODEL), lambda n, k: (n, 0)),
        out_shape=jax.ShapeDtypeStruct((IN_WIDTH_PAD, D_MODEL), F32),
        compiler_params=_cp(("arbitrary", "arbitrary"), VMEM_LIMIT),
    )(x, dh)


def _adamw(w, g, m, v, name):
    rows, cols = w.shape
    tc = 256 if cols % 256 == 0 and rows * cols > 2 ** 18 else cols

    def body(w_ref, g_ref, m_ref, v_ref, d_ref, nm_ref, nv_ref):
        g_ = g_ref[...]
        nm = ADAM_B1 * m_ref[...] + (1.0 - ADAM_B1) * g_
        nv = ADAM_B2 * v_ref[...] + (1.0 - ADAM_B2) * jnp.square(g_)
        m_hat = nm / (1.0 - ADAM_B1 ** ADAM_STEP)
        v_hat = nv / (1.0 - ADAM_B2 ** ADAM_STEP)
        d_ref[...] = -ADAM_LR * (m_hat / (jnp.sqrt(v_hat) + ADAM_EPS) + ADAM_WD * w_ref[...])
        nm_ref[...] = nm
        nv_ref[...] = nv

    spec = pl.BlockSpec((rows, tc), lambda i: (0, i))
    return pl.pallas_call(
        body, name=name, grid=(cols // tc,), in_specs=[spec] * 4, out_specs=[spec] * 3,
        out_shape=[jax.ShapeDtypeStruct(w.shape, F32)] * 3, compiler_params=_cp(("arbitrary",)),
    )(w, g, m, v)


def _pad_row(v):
    return jnp.pad(v.reshape(1, -1), ((0, 0), (0, D_MODEL - v.shape[-1])))


def _local_step(x2, target, w_in_p, w_uq_f, wukv_f, w_out_f, q_norm_g, kv_norm_g, ln_g, ln_b):
    seq = x2.shape[0]
    wuq_e = jnp.pad(w_uq_f.reshape(Q_LORA, MLA_HEADS, 96), ((0, 0), (0, 0), (0, 32))).reshape(Q_LORA, MLA_HEADS * LANES)
    ct, st, cd, sd = _rope_tables(seq)
    gq = q_norm_g.reshape(1, Q_LORA)
    gkv = kv_norm_g.reshape(1, KV_LORA)

    cq, ckv, gates, qr, krot, vb, q_e, k_e, v_e = _proj(x2, w_in_p, gq, gkv, wuq_e, wukv_f, ct, st, cd, sd)
    o_a, lse_a = _mla_fwd(q_e, k_e, v_e)
    o_b, lse_b = _dil_fwd(qr, krot, vb)

    dz, d_o, d_gates, dw_out, dln_g, dln_b, loss_part = _post(
        x2, o_a, o_b, gates, w_out_f, ln_g.reshape(1, D_MODEL), ln_b.reshape(1, D_MODEL), target)
    dq_e, dk_e, dv_e = _mla_bwd(q_e, k_e, v_e, d_o, o_a, lse_a)
    dqr, dkr, dvb = _dil_bwd(qr, krot, vb, d_o, o_b, lse_b)
    grad_x, dh, dwuq_e, dwukv, dgq, dgkv = _in_bwd(
        dz, cq, ckv, gq, gkv, wuq_e, wukv_f, ct, st, dq_e, dk_e, dv_e, d_gates, dqr, dkr, dvb, cd, sd, w_in_p)
    dw_in = _unpermute_dw_in_t(_dw_in(x2, dh))
    dw_uq = dwuq_e.reshape(Q_LORA, MLA_HEADS, LANES)[:, :, :96].reshape(Q_LORA, MLA_HEADS * 96)
    return loss_part, grad_x, dw_in, dw_uq, dwukv, dw_out, dgq, dgkv, dln_g, dln_b


def kernel(x, w_in, q_norm_g, kv_norm_g, w_uq, w_ukv, w_out, ln_g, ln_b, loss_target, m_w_in, m_q_norm_g, m_kv_norm_g, m_w_uq, m_w_ukv, m_w_out, m_ln_g, m_ln_b, v_w_in, v_q_norm_g, v_kv_norm_g, v_w_uq, v_w_ukv, v_w_out, v_ln_g, v_ln_b):
    seq = x.shape[1]
    x2 = x.reshape(seq, D_MODEL)
    target = loss_target.reshape(seq, D_MODEL)

    g_w_in, g_w_uq, g_w_ukv, g_w_out = _all_gather_weights([w_in, w_uq, w_ukv, w_out])
    by_cols = lambda g: jnp.concatenate([g[j] for j in range(N_SHARD)], axis=1)
    loss_part, grad_x, dw_in, dw_uq, dwukv, dw_out, dgq, dgkv, dln_g, dln_b = _local_step(
        x2, target, _permute_w_in_shards(g_w_in), by_cols(g_w_uq), by_cols(g_w_ukv), g_w_out.reshape(D_MODEL, D_MODEL),
        q_norm_g, kv_norm_g, ln_g, ln_b)

    to_shards = lambda d: d.reshape(d.shape[0], N_SHARD, d.shape[1] // N_SHARD).transpose(1, 0, 2)
    grads = [dw_in.reshape(N_SHARD, 808, D_MODEL), to_shards(dw_uq), to_shards(dwukv),
             dw_out.reshape(N_SHARD, 256, D_MODEL)]
    small = jnp.concatenate([_pad_row(dgq), _pad_row(dgkv), dln_g, dln_b, _pad_row(loss_part),
                             jnp.zeros((3, D_MODEL), F32)], axis=0)
    *chip_sums, smalls = _reduce_over_sibling(grads, small)
    g_in_t, g_uq, g_ukv, g_out = _reduce_over_chips(chip_sums)
    g_in = g_in_t.T
    small_sum = _sum_smalls(smalls)
    loss = small_sum[4, 0]

    big = [[o.T for o in _adamw(w.T, g.T, m.T, v.T, name)] for w, g, m, v, name in (
        (w_in, g_in, m_w_in, v_w_in, "adamw_w_in"), (w_uq, g_uq, m_w_uq, v_w_uq, "adamw_w_uq"))]
    big += [_adamw(w, g, m, v, name) for w, g, m, v, name in (
        (w_ukv, g_ukv, m_w_ukv, v_w_ukv, "adamw_w_ukv"), (w_out, g_out, m_w_out, v_w_out, "adamw_w_out"))]
    vec = lambda a, b, c_, d: jnp.concatenate([_pad_row(a), _pad_row(b), _pad_row(c_), _pad_row(d),
                                               jnp.zeros((4, D_MODEL), F32)], axis=0)
    sw = vec(q_norm_g, kv_norm_g, ln_g, ln_b)
    sm = vec(m_q_norm_g, m_kv_norm_g, m_ln_g, m_ln_b)
    sv = vec(v_q_norm_g, v_kv_norm_g, v_ln_g, v_ln_b)
    sg = jnp.concatenate([small_sum[:4], jnp.zeros((4, D_MODEL), F32)], axis=0)
    s_delta, s_m, s_v = _adamw(sw, sg, sm, sv, "adamw_vectors")

    def vectors(a):
        return [a[0, :Q_LORA], a[1, :KV_LORA], a[2], a[3]]

    def ordered(bigs, smalls_):
        return [bigs[0], smalls_[0], smalls_[1], bigs[1], bigs[2], bigs[3], smalls_[2], smalls_[3]]

    grads_out = ordered([g_in, g_uq, g_ukv, g_out], vectors(small_sum))
    deltas = ordered([b[0] for b in big], vectors(s_delta))
    new_m = ordered([b[1] for b in big], vectors(s_m))
    new_v = ordered([b[2] for b in big], vectors(s_v))
    return (loss, grad_x.reshape(x.shape), *grads_out, *deltas, *new_m, *new_v)
```

```python
import functools

import jax
import jax.numpy as jnp
import numpy as np
from jax import lax
from jax.experimental import pallas as pl
from jax.experimental.pallas import tpu as pltpu

F32 = jnp.float32
BF16 = jnp.bfloat16

D_MODEL = 1024
ROPE_THETA = 500000.0
BLOCK = 128
NEG = -1e30
RMS_EPS = 1e-6
LN_EPS = 1e-5

MLA_HEADS = 8
MLA_NOPE = 64
MLA_ROPE = 32
Q_LORA = 384
KV_LORA = 256
DIL_HEADS = 8
DIL_HEAD_DIM = 64
DIL_ROT = 16
DIL_DILATIONS = (1, 4, 16)
IN_WIDTH = 3232
IN_WIDTH_PAD = 3328
ONES_LANE = (64, 0)
MLA_SCALE = (MLA_NOPE + MLA_ROPE) ** -0.5
DIL_SCALE = DIL_HEAD_DIM ** -0.5
ALPHA = 2.0 ** 0.25

ADAM_LR = 0.001
ADAM_B1 = 0.9
ADAM_B2 = 0.999
ADAM_EPS = 1e-08
ADAM_WD = 0.01
ADAM_STEP = 10

N_SHARD = 4
SHARD_SHAPES = ((1024, 808), (384, 192), (256, 256), (256, 1024))
GRAD_SHAPES = ((808, 1024), (384, 192), (256, 256), (256, 1024))
GRAD_SPLIT_COLS = (True, False, False, False)
ROW_CHUNK = 64
LANES = 128
VMEM_LIMIT = 56 * 1024 * 1024
MESH = pl.DeviceIdType.MESH

NT = (((1,), (1,)), ((), ()))
TN = (((0,), (0,)), ((), ()))


def _cp(sem=None, vmem=None):
    return pltpu.CompilerParams(dimension_semantics=sem, vmem_limit_bytes=vmem)


def _dot(a, b, dims=None):
    if dims is None:
        return jnp.dot(a, b, preferred_element_type=F32)
    return lax.dot_general(a, b, dims, preferred_element_type=F32)


def _rope_tables(seq):
    f32 = np.float32
    pos = np.arange(seq, dtype=f32)[:, None]
    one, zero = np.ones((seq, 64), f32), np.zeros((seq, 64), f32)

    def cos_sin(dim):
        inv = np.power(f32(ROPE_THETA), -np.arange(0, dim, 2, dtype=f32) / f32(dim)).astype(f32)
        ang = (pos * inv[None, :]).astype(f32)
        return np.cos(ang).astype(f32), np.sin(ang).astype(f32)

    cos, sin = cos_sin(MLA_ROPE)
    ct = np.concatenate([one, cos, cos, zero[:, :32]], axis=1)
    st = np.concatenate([zero, -sin, sin, zero[:, :32]], axis=1)
    cos, sin = cos_sin(DIL_ROT)
    cd = np.concatenate([cos, cos, one[:, :48]], axis=1)
    sd = np.concatenate([-sin, sin, zero[:, :48]], axis=1)
    return tuple(jnp.asarray(t) for t in (ct, st, np.tile(cd, (1, 2)), np.tile(sd, (1, 2))))


W_IN_ORDER = ((0, 640), (672, 1184), (2720, 3232), (1184, 2720), None, (640, 672))


def _permute_w_in(w):
    z = jnp.zeros((w.shape[0], 64), w.dtype)
    parts = [z if r is None else w[:, r[0]:r[1]] for r in W_IN_ORDER]
    return jnp.concatenate(parts + [z[:, :32]], axis=1)


def _permute_w_in_shards(g):
    width = g.shape[2]
    z = jnp.zeros((g.shape[1], 64), g.dtype)
    parts = []
    for r in W_IN_ORDER:
        if r is None:
            parts.append(z)
            continue
        for j in range(N_SHARD):
            lo, hi = max(r[0], width * j), min(r[1], width * (j + 1))
            if lo < hi:
                parts.append(g[j, :, lo - width * j:hi - width * j])
    return jnp.concatenate(parts + [z[:, :32]], axis=1)


def _w_in_row_pieces():
    width = GRAD_SHAPES[0][0]
    pieces, at = [], 0
    for r in W_IN_ORDER:
        if r is None:
            at += 64
            continue
        for k in range(N_SHARD):
            lo, hi = max(r[0], width * k), min(r[1], width * (k + 1))
            if lo < hi:
                pieces.append((k, lo - width * k, at + lo - r[0], hi - lo))
        at += r[1] - r[0]
    return pieces


def _position():
    return lax.axis_index("x"), lax.axis_index("y"), lax.axis_index("c")


def _halves(c, rows):
    hr = rows // 2
    return pl.ds(pl.multiple_of(c * hr, 8), hr), pl.ds(pl.multiple_of((1 - c) * hr, 8), hr)


def _for_row_chunks(rows, fn):
    def step(i, carry):
        fn(pl.multiple_of(i * ROW_CHUNK, ROW_CHUNK))
        return carry

    lax.fori_loop(0, rows // ROW_CHUNK, step, 0)


def _all_gather_weights(shards):
    n = len(shards)

    def body(*refs):
        ins, outs = refs[:n], refs[n:2 * n]
        send_sems, recv_sems = refs[2 * n:]
        x, y, c = _position()
        me = 2 * x + y
        chips = [(1 - x, y), (x, 1 - y), (1 - x, 1 - y)]
        for a in range(n):
            def cast(r, a=a):
                outs[a][me, pl.ds(r, ROW_CHUNK), :] = ins[a][pl.ds(r, ROW_CHUNK), :].astype(BF16)

            _for_row_chunks(SHARD_SHAPES[a][0], cast)

        def copy(k, a, slot, rows, to):
            ref = outs[a].at[slot, rows]
            return pltpu.make_async_remote_copy(
                src_ref=ref, dst_ref=ref, send_sem=send_sems.at[k * n + a], recv_sem=recv_sems.at[k * n + a],
                device_id=to, device_id_type=MESH)

        half = [_halves(c, SHARD_SHAPES[a][0])[0] for a in range(n)]
        other = [_halves(c, SHARD_SHAPES[a][0])[1] for a in range(n)]
        first = [copy(k, a, me, half[a], (px, py, c)) for k, (px, py) in enumerate(chips) for a in range(n)]
        for cp in first:
            cp.start()
        passed = []
        for k, (px, py) in enumerate(chips):
            for a in range(n):
                copy(k, a, 2 * px + py, half[a], (x, y, c)).wait_recv()
                cp = copy(3 + k, a, 2 * px + py, half[a], (x, y, 1 - c))
                cp.start()
                passed.append(cp)
        for k, (px, py) in enumerate(chips):
            for a in range(n):
                copy(3 + k, a, 2 * px + py, other[a], (x, y, c)).wait_recv()
        for cp in first + passed:
            cp.wait_send()

    vmem = pl.BlockSpec(memory_space=pltpu.VMEM)
    return pl.pallas_call(
        body, name="all_gather_weights",
        out_shape=[jax.ShapeDtypeStruct((N_SHARD,) + s, BF16) for s in SHARD_SHAPES],
        in_specs=[vmem] * n, out_specs=[vmem] * n,
        scratch_shapes=[pltpu.SemaphoreType.DMA((6 * n,)), pltpu.SemaphoreType.DMA((6 * n,))],
        compiler_params=_cp(None, VMEM_LIMIT),
    )(*shards)


def _grad_half_shape(a):
    rows, cols = GRAD_SHAPES[a]
    return (rows, cols // 2) if GRAD_SPLIT_COLS[a] else (rows // 2, cols)


def _grad_half(a, c):
    rows, cols = GRAD_SHAPES[a]
    if GRAD_SPLIT_COLS[a]:
        return slice(None), pl.ds(pl.multiple_of(c * (cols // 2), LANES), cols // 2)
    return pl.ds(pl.multiple_of(c * (rows // 2), ROW_CHUNK), rows // 2), slice(None)


def _grad_chunks(a, c):
    rows, cols = GRAD_SHAPES[a]
    if GRAD_SPLIT_COLS[a]:
        return [((slice(None), pl.ds(c0, LANES)),
                 (slice(None), pl.ds(pl.multiple_of(c * (cols // 2) + c0, LANES), LANES)))
                for c0 in range(0, cols // 2, LANES)]
    return [((pl.ds(r0, ROW_CHUNK), slice(None)),
             (pl.ds(pl.multiple_of(c * (rows // 2) + r0, ROW_CHUNK), ROW_CHUNK), slice(None)))
            for r0 in range(0, rows // 2, ROW_CHUNK)]


def _reduce_over_sibling(grads, small):
    n = len(grads)
    pieces = _w_in_row_pieces()

    def body(*refs):
        g_hbm, sm = refs[:n], refs[n]
        sums, smalls = refs[n + 1:2 * n + 1], refs[2 * n + 1]
        stage, got = refs[2 * n + 2:3 * n + 2], refs[3 * n + 2:4 * n + 2]
        send_sems, recv_sems, local_sems = refs[4 * n + 2:]
        x, y, c = _position()
        me = 4 * x + 2 * y + c
        loads = [[pltpu.make_async_copy(g_hbm[0].at[pl.ds(src, rows)], stage[0].at[k, pl.ds(dst, rows)],
                                        local_sems.at[n + i])
                  for i, (k, dst, src, rows) in enumerate(pieces)]]
        loads += [[pltpu.make_async_copy(g_hbm[a], stage[a], local_sems.at[a])] for a in range(1, n)]
        for group in loads:
            for ld in group:
                ld.start()
        smalls[me] = sm[...]
        sends = []
        for rel in range(1, 8):
            px = 1 - x if rel // 4 else x
            py = 1 - y if (rel // 2) % 2 else y
            pc = 1 - c if rel % 2 else c
            cp = pltpu.make_async_remote_copy(
                src_ref=sm, dst_ref=smalls.at[me], send_sem=send_sems.at[n + rel], recv_sem=recv_sems.at[n + rel],
                device_id=(px, py, pc), device_id_type=MESH)
            cp.start()
            sends.append((cp, 4 * px + 2 * py + pc))
        swaps = []
        for a in range(n):
            for ld in loads[a]:
                ld.wait()
            cp = pltpu.make_async_remote_copy(
                src_ref=stage[a].at[(slice(None),) + _grad_half(a, 1 - c)], dst_ref=got[a], send_sem=send_sems.at[a], recv_sem=recv_sems.at[a],
                device_id=(x, y, 1 - c), device_id_type=MESH)
            cp.start()
            swaps.append(cp)
        for a in range(n):
            swaps[a].wait_recv()
            for k in range(N_SHARD):
                for in_half, in_whole in _grad_chunks(a, c):
                    pair = stage[a][(k,) + in_whole] + got[a][(k,) + in_half]
                    sums[a][(k,) + in_half] = pair.astype(BF16)
        for rel, (cp, peer) in enumerate(sends, start=1):
            pltpu.make_async_remote_copy(
                src_ref=sm, dst_ref=smalls.at[peer], send_sem=send_sems.at[n + rel], recv_sem=recv_sems.at[n + rel],
                device_id=(x, y, c), device_id_type=MESH).wait_recv()
        for cp in swaps:
            cp.wait_send()
        for cp, _ in sends:
            cp.wait_send()

    vmem = pl.BlockSpec(memory_space=pltpu.VMEM)
    half = [(N_SHARD,) + _grad_half_shape(a) for a in range(n)]
    return pl.pallas_call(
        body, name="reduce_over_sibling",
        out_shape=[jax.ShapeDtypeStruct(s, BF16) for s in half] + [jax.ShapeDtypeStruct((8,) + small.shape, F32)],
        in_specs=[pl.BlockSpec(memory_space=pl.ANY)] * n + [vmem], out_specs=[vmem] * (n + 1),
        scratch_shapes=[pltpu.VMEM((N_SHARD,) + s, F32) for s in GRAD_SHAPES] + [pltpu.VMEM(s, F32) for s in half]
        + [pltpu.SemaphoreType.DMA((n + 8,)), pltpu.SemaphoreType.DMA((n + 8,)),
           pltpu.SemaphoreType.DMA((n + len(pieces),))],
        compiler_params=_cp(None, VMEM_LIMIT),
    )(*grads, small)


def _reduce_over_chips(sums):
    n = len(sums)

    def body(*refs):
        h, outs, got = refs[:n], refs[n:2 * n], refs[2 * n:3 * n]
        send_sems, recv_sems = refs[3 * n:]
        x, y, c = _position()
        me = 2 * x + y
        chips = [(1 - x, y), (x, 1 - y), (1 - x, 1 - y)]
        sends = []
        for k, (px, py) in enumerate(chips):
            for a in range(n):
                cp = pltpu.make_async_remote_copy(
                    src_ref=h[a].at[2 * px + py], dst_ref=got[a].at[k], send_sem=send_sems.at[k * n + a],
                    recv_sem=recv_sems.at[k * n + a], device_id=(px, py, c), device_id_type=MESH)
                cp.start()
                sends.append(cp)
        for cp in sends:
            cp.wait_recv()
        joins = []
        for a in range(n):
            for in_half, in_whole in _grad_chunks(a, c):
                total = h[a][(me,) + in_half].astype(F32)
                for k in range(3):
                    total = total + got[a][(k,) + in_half].astype(F32)
                outs[a][in_whole] = total
            half = outs[a].at[_grad_half(a, c)]
            cp = pltpu.make_async_remote_copy(
                src_ref=half, dst_ref=half, send_sem=send_sems.at[3 * n + a],
                recv_sem=recv_sems.at[3 * n + a], device_id=(x, y, 1 - c), device_id_type=MESH)
            cp.start()
            joins.append(cp)
        for a in range(n):
            other = outs[a].at[_grad_half(a, 1 - c)]
            pltpu.make_async_remote_copy(
                src_ref=other, dst_ref=other, send_sem=send_sems.at[3 * n + a],
                recv_sem=recv_sems.at[3 * n + a], device_id=(x, y, c), device_id_type=MESH).wait_recv()
        for cp in sends + joins:
            cp.wait_send()

    vmem = pl.BlockSpec(memory_space=pltpu.VMEM)
    return pl.pallas_call(
        body, name="reduce_over_chips",
        out_shape=[jax.ShapeDtypeStruct(s, F32) for s in GRAD_SHAPES],
        in_specs=[vmem] * n, out_specs=[vmem] * n,
        scratch_shapes=[pltpu.VMEM((3,) + _grad_half_shape(a), BF16) for a in range(n)]
        + [pltpu.SemaphoreType.DMA((4 * n,)), pltpu.SemaphoreType.DMA((4 * n,))],
        compiler_params=_cp(None, VMEM_LIMIT),
    )(*sums)


def _sum_smalls(smalls):
    def body(s, o):
        acc = s[0]
        for d in range(1, 8):
            acc = acc + s[d]
        o[...] = acc

    return pl.pallas_call(body, name="sum_smalls", out_shape=jax.ShapeDtypeStruct(smalls.shape[1:], F32))(smalls)


def _proj(x, w_in_p, gq, gkv, wuq_e, wukv, ct, st, cd, sd):
    seq = x.shape[0]
    tr = 512

    def body(x_ref, w_ref, gq_ref, gkv_ref, wuq_ref, wukv_ref, ct_ref, st_ref, cd_ref, sd_ref,
             cq_ref, ckv_ref, g_ref, qr_ref, kr_ref, vb_ref, q_out, k_out, v_out):
        lane = lax.broadcasted_iota(jnp.int32, (tr, LANES), 1)
        xb = x_ref[...].astype(BF16)
        cq = _dot(xb, w_ref[:, 0:384])
        ckv = _dot(xb, w_ref[:, 384:640])
        cq_ref[...] = cq
        ckv_ref[...] = ckv
        g_ref[...] = _dot(xb, w_ref[:, 640:1664])

        cd_, sd_ = cd_ref[...], sd_ref[...]
        qb = _dot(xb, w_ref[:, 1664:2176])
        kb = _dot(xb, w_ref[:, 2176:2688])
        for p in range(4):
            cols = slice(LANES * p, LANES * (p + 1))
            t = qb[:, cols]
            qr_ref[:, cols] = (t * cd_ + _dil_rot(t, lane) * sd_) * DIL_SCALE
            t = kb[:, cols]
            kr_ref[:, cols] = t * cd_ + _dil_rot(t, lane) * sd_
        vb_ref[...] = _dot(xb, w_ref[:, 2688:3200])

        ct_, st_ = ct_ref[...], st_ref[...]

        def rope(t):
            return t * ct_ + _mla_rot(t, lane) * st_

        _, qn = _rms(cq, gq_ref[...])
        q_all = _dot(qn.astype(BF16), wuq_ref[...])
        for h in range(MLA_HEADS):
            q_out[h] = (rope(q_all[:, LANES * h:LANES * (h + 1)]) * MLA_SCALE).astype(BF16)
        _, kvn = _rms(ckv, gkv_ref[...])
        kv_all = _dot(kvn.astype(BF16), wukv_ref[...])
        kpe = rope(_dot(xb, w_ref[:, 3200:3328]))
        for h in range(MLA_HEADS):
            kv_h = kv_all[:, LANES * h:LANES * (h + 1)]
            k_out[h] = jnp.where(lane < 64, kv_h, kpe).astype(BF16)
            if h % 2:
                v = jnp.where(lane >= 64, kv_h, 0.0)
            else:
                v = jnp.where(lane < 64, pltpu.roll(kv_h, 64, 1), 0.0)
            v_out[h] = jnp.where(lane == ONES_LANE[h % 2], 1.0, v).astype(BF16)

    row = lambda w: pl.BlockSpec((tr, w), lambda i: (i, 0))
    full = lambda a: pl.BlockSpec(a.shape, lambda i: (0,) * a.ndim)
    head = pl.BlockSpec((MLA_HEADS, tr, LANES), lambda i: (0, i, 0))
    widths = (Q_LORA, KV_LORA, D_MODEL, 512, 512, 512)
    return pl.pallas_call(
        body, name="proj", grid=(seq // tr,),
        in_specs=[row(D_MODEL), full(w_in_p), full(gq), full(gkv), full(wuq_e), full(wukv)] + [row(LANES)] * 4,
        out_specs=[row(w) for w in widths] + [head] * 3,
        out_shape=[jax.ShapeDtypeStruct((seq, w), F32) for w in widths]
        + [jax.ShapeDtypeStruct((MLA_HEADS, seq, LANES), BF16)] * 3,
        compiler_params=_cp(("arbitrary",), VMEM_LIMIT),
    )(x, w_in_p, gq, gkv, wuq_e, wukv, ct, st, cd, sd)


def _mla_rot(t, lane):
    return jnp.where(lane < 80, pltpu.roll(t, 112, 1), pltpu.roll(t, 16, 1))


def _dil_rot(t, lane):
    return jnp.where(lane % 64 < 8, pltpu.roll(t, 120, 1), pltpu.roll(t, 8, 1))


def _rms(c, g):
    r = lax.rsqrt(jnp.mean(c * c, axis=-1, keepdims=True) + RMS_EPS)
    return r, c * r * g


def _mla_fwd(q, k, v):
    seq = q.shape[1]
    tq = 512
    nq = seq // tq

    def body(q_ref, k_ref, v_ref, o_ref, lse_ref, m_s, acc_s, s_buf):
        i = pl.program_id(1)
        row = lax.broadcasted_iota(jnp.int32, (tq, tq), 0)
        col = lax.broadcasted_iota(jnp.int32, (tq, tq), 1)
        lane = lax.broadcasted_iota(jnp.int32, (tq, LANES), 1)
        m_s[...] = jnp.full((2, tq, LANES), NEG, F32)
        acc_s[...] = jnp.zeros((2, tq, LANES), F32)

        def block(j):
            return pl.ds(pl.multiple_of(j * tq, tq), tq)

        def scores(hh, j):
            return _dot(q_ref[hh], k_ref[hh, block(j), :], NT)

        def consume(hh, j, s):
            m_prev = m_s[hh]
            m_new = jnp.maximum(m_prev, jnp.max(s, axis=1, keepdims=True))
            p = jnp.exp(s - m_new[:, :1])
            acc_s[hh] = jnp.exp(m_prev - m_new) * acc_s[hh] + _dot(p.astype(BF16), v_ref[hh, block(j), :])
            m_s[hh] = m_new

        for hh in range(2):
            s_buf[0, hh] = scores(hh, 0)

        def full_step(j, carry):
            slot = j & 1
            for hh in range(2):
                s = s_buf[slot, hh]
                s_buf[1 - slot, hh] = scores(hh, j + 1)
                consume(hh, j, s)
            return carry

        lax.fori_loop(0, i, full_step, 0)
        total = jnp.zeros((tq, LANES), F32)
        for hh in range(2):
            consume(hh, i, jnp.where(col <= row, s_buf[i & 1, hh], NEG))
            acc = acc_s[hh]
            l = acc[:, ONES_LANE[hh]:ONES_LANE[hh] + 1]
            mine = (lane >= 64) if hh else (lane < 64)
            total = total + jnp.where(mine, acc / l, 0.0)
            lse_ref[hh] = m_s[hh] + jnp.log(l)
        o_ref[...] = total

    kv_spec = pl.BlockSpec((2, seq, LANES), lambda p, i: (p, 0, 0))
    return pl.pallas_call(
        body, name="mla_fwd", grid=(MLA_HEADS // 2, nq),
        in_specs=[pl.BlockSpec((2, tq, LANES), lambda p, i: (p, i, 0)), kv_spec, kv_spec],
        out_specs=[pl.BlockSpec((tq, LANES), lambda p, i: (i, p)), pl.BlockSpec((2, tq, LANES), lambda p, i: (p, i, 0))],
        out_shape=[jax.ShapeDtypeStruct((seq, 4 * LANES), F32), jax.ShapeDtypeStruct((MLA_HEADS, seq, LANES), F32)],
        scratch_shapes=[pltpu.VMEM((2, tq, LANES), F32), pltpu.VMEM((2, tq, LANES), F32),
                        pltpu.VMEM((2, 2, tq, tq), F32)],
        compiler_params=_cp(("arbitrary", "arbitrary"), VMEM_LIMIT),
    )(q, k, v)


DIL_Q_FWD = 2 * BLOCK
DIL_Q_BWD = BLOCK


def _dil_tile_index(t, d, seq, nq):
    per_class = seq // (nq * d)
    shift = per_class.bit_length() - 1
    r = t >> shift
    n = t & (per_class - 1)
    start = r + (nq * d) * n
    prev = jnp.maximum(start - BLOCK * d, r)
    if d == 1:
        start = pl.multiple_of(start, nq)
        prev = pl.multiple_of(prev, BLOCK)
    return (n == 0).astype(jnp.int32), start, prev


def _dil_rows(start, d, size):
    return pl.ds(start, size) if d == 1 else pl.ds(start, size, stride=d)


def _dil_bias(nq):
    i = lax.broadcasted_iota(jnp.int32, (2 * nq, BLOCK + nq), 0) % nq
    j = lax.broadcasted_iota(jnp.int32, (2 * nq, BLOCK + nq), 1)
    band = (j >= i) & (j <= i + BLOCK)
    return jnp.where(band, 0.0, NEG), jnp.where(band & (j >= BLOCK), 0.0, NEG)


def _stack_heads(t, lane):
    return jnp.concatenate([jnp.where(lane < 64, t, 0.0), jnp.where(lane >= 64, t, 0.0)], axis=0)


def _unstack_heads(t, lane):
    nq = t.shape[0] // 2
    return jnp.where(lane < 64, t[:nq], t[nq:])


def _dil_fwd(qr, kr, vb):
    seq = qr.shape[0]
    nq = DIL_Q_FWD
    n_tiles = seq // nq
    assert seq % (nq * max(DIL_DILATIONS)) == 0

    def body(q_ref, k_ref, v_ref, o_ref, lse_ref, m_s, l_s, n_s, bias_s):
        lane = lax.broadcasted_iota(jnp.int32, (nq, LANES), 1)
        bias_s[0], bias_s[1] = _dil_bias(nq)
        for bi, d in enumerate(DIL_DILATIONS):

            def tile(t, carry, d=d, bi=bi):
                first, start, prev = _dil_tile_index(t, d, seq, nq)
                rows, prows = _dil_rows(start, d, nq), _dil_rows(prev, d, BLOCK)
                qst = _stack_heads(q_ref[rows, :], lane).astype(BF16)
                if seq == nq * d:
                    kcat, vcat = k_ref[rows, :].astype(BF16), v_ref[rows, :].astype(BF16)
                    s = _dot(qst, kcat, NT) + bias_s[1, :, BLOCK:]
                else:
                    kcat = jnp.concatenate([k_ref[prows, :], k_ref[rows, :]], axis=0).astype(BF16)
                    vcat = jnp.concatenate([v_ref[prows, :], v_ref[rows, :]], axis=0).astype(BF16)
                    s = _dot(qst, kcat, NT) + bias_s[first]
                m = jnp.max(s, axis=1, keepdims=True)
                p = jnp.exp(s - m)
                l2 = _unstack_heads(jnp.sum(p, axis=1, keepdims=True) + jnp.zeros((2 * nq, LANES), F32), lane)
                m2 = _unstack_heads(m + jnp.zeros((2 * nq, LANES), F32), lane)
                num2 = _unstack_heads(_dot(p.astype(BF16), vcat), lane)
                if bi == 0:
                    m_s[rows, :] = m2
                    l_s[rows, :] = l2
                    n_s[rows, :] = num2
                else:
                    m_old = m_s[rows, :]
                    m_new = jnp.maximum(m_old, m2)
                    a = jnp.exp(m_old - m_new)
                    b = jnp.exp(m2 - m_new)
                    m_s[rows, :] = m_new
                    l_s[rows, :] = a * l_s[rows, :] + b * l2
                    n_s[rows, :] = a * n_s[rows, :] + b * num2
                return carry

            lax.fori_loop(0, n_tiles, tile, 0, unroll=2)
        o_ref[...] = n_s[...] / l_s[...]
        lse_ref[...] = m_s[...] + jnp.log(l_s[...])

    col = lambda off: pl.BlockSpec((seq, LANES), lambda p: (0, p + off))
    return pl.pallas_call(
        body, name="dil_fwd", grid=(4,),
        in_specs=[col(0), col(0), col(0)],
        out_specs=[col(0), pl.BlockSpec((None, seq, LANES), lambda p: (p, 0, 0))],
        out_shape=[jax.ShapeDtypeStruct((seq, 4 * LANES), F32), jax.ShapeDtypeStruct((4, seq, LANES), F32)],
        scratch_shapes=[pltpu.VMEM((seq, LANES), F32)] * 3 + [pltpu.VMEM((2, 2 * nq, BLOCK + nq), F32)],
        compiler_params=_cp(("arbitrary",), VMEM_LIMIT),
    )(qr, kr, vb)


def _post(x, o_a, o_b, gates, w_out, ln_g, ln_b, target):
    seq = x.shape[0]
    tr = 512

    def body(x_ref, oa_ref, ob_ref, g_ref, w_ref, lg_ref, lb_ref, t_ref,
             dz_ref, do_ref, dg_ref, dw_ref, dlg_ref, dlb_ref, loss_ref):
        @pl.when(pl.program_id(0) == 0)
        def _():
            dw_ref[...] = jnp.zeros_like(dw_ref)
            dlg_ref[...] = jnp.zeros_like(dlg_ref)
            dlb_ref[...] = jnp.zeros_like(dlb_ref)
            loss_ref[...] = jnp.zeros_like(loss_ref)

        g = g_ref[...]
        sg = jax.nn.sigmoid(g)
        silu = g * sg
        o = jnp.concatenate([oa_ref[...], ob_ref[...]], axis=1)
        mixb = (o * silu).astype(BF16)
        w = w_ref[...]
        z = ALPHA * x_ref[...] + _dot(mixb, w)
        mu = jnp.mean(z, axis=-1, keepdims=True)
        zc = z - mu
        rstd = lax.rsqrt(jnp.mean(zc * zc, axis=-1, keepdims=True) + LN_EPS)
        xhat = zc * rstd
        lg = lg_ref[...]
        err = xhat * lg + lb_ref[...] - t_ref[...]
        loss_ref[...] += jnp.sum(err * err) * (0.5 / D_MODEL)
        dy = err * (1.0 / D_MODEL)
        dlg_ref[...] += jnp.sum(dy * xhat, axis=0, keepdims=True)
        dlb_ref[...] += jnp.sum(dy, axis=0, keepdims=True)
        dxh = dy * lg
        dz = rstd * (dxh - jnp.mean(dxh, axis=-1, keepdims=True) - xhat * jnp.mean(dxh * xhat, axis=-1, keepdims=True))
        dz_ref[...] = dz
        dzb = dz.astype(BF16)
        dmix = _dot(dzb, w, NT)
        do_ref[...] = dmix * silu
        dg_ref[...] = (dmix * o * (sg * (1.0 + g * (1.0 - sg)))).astype(BF16)
        dw_ref[...] += _dot(mixb, dzb, TN)

    row = lambda w: pl.BlockSpec((tr, w), lambda i: (i, 0))
    full = lambda s: pl.BlockSpec(s, lambda i: (0, 0))
    return pl.pallas_call(
        body, name="post", grid=(seq // tr,),
        in_specs=[row(D_MODEL), row(512), row(512), row(D_MODEL), full((D_MODEL, D_MODEL)), full((1, D_MODEL)),
                  full((1, D_MODEL)), row(D_MODEL)],
        out_specs=[row(D_MODEL), row(D_MODEL), row(D_MODEL), full((D_MODEL, D_MODEL)), full((1, D_MODEL)),
                   full((1, D_MODEL)), full((1, LANES))],
        out_shape=[jax.ShapeDtypeStruct((seq, D_MODEL), F32), jax.ShapeDtypeStruct((seq, D_MODEL), F32),
                   jax.ShapeDtypeStruct((seq, D_MODEL), BF16), jax.ShapeDtypeStruct((D_MODEL, D_MODEL), F32),
                   jax.ShapeDtypeStruct((1, D_MODEL), F32), jax.ShapeDtypeStruct((1, D_MODEL), F32),
                   jax.ShapeDtypeStruct((1, LANES), F32)],
        compiler_params=_cp(("arbitrary",), VMEM_LIMIT),
    )(x, o_a, o_b, gates, w_out, ln_g, ln_b, target)


def _mla_bwd(q, k, v, d_o, o, lse):
    seq = q.shape[1]
    tq = 512
    nq = seq // tq

    def body(q_ref, k_ref, v_ref, do_ref, o_ref, lse_ref, dq_ref, dk_ref, dv_ref, d_s, lse_s, dk_s, dv_s, v_s, kt_s, dqt_s):
        j = pl.program_id(1)
        lane = lax.broadcasted_iota(jnp.int32, (tq, LANES), 1)
        row = lax.broadcasted_iota(jnp.int32, (tq, tq), 0)
        col = lax.broadcasted_iota(jnp.int32, (tq, tq), 1)

        @pl.when(j == 0)
        def _():
            dqt_s[...] = jnp.zeros_like(dqt_s)

            def rowsum(i, carry):
                rows = pl.ds(pl.multiple_of(i * tq, tq), tq)
                prod = do_ref[rows, :] * o_ref[rows, :]
                for hh in range(2):
                    mine = (lane >= 64) if hh else (lane < 64)
                    total = jnp.sum(jnp.where(mine, prod, 0.0), axis=1, keepdims=True)
                    d_s[hh, i] = jnp.transpose(total + jnp.zeros((tq, LANES), F32))[:8]
                    lse_s[hh, i] = jnp.transpose(lse_ref[hh, rows, :])[:8]
                return carry

            lax.fori_loop(0, nq, rowsum, 0)

        dk_s[...] = jnp.zeros_like(dk_s)
        dv_s[...] = jnp.zeros_like(dv_s)
        for hh in range(2):
            v_s[hh] = jnp.where(lane == ONES_LANE[hh], 0.0, v_ref[hh].astype(F32)).astype(BF16)
            kt_s[hh] = jnp.transpose(k_ref[hh].astype(F32)).astype(BF16)

        def step(i, masked):
            rows = pl.ds(pl.multiple_of(i * tq, tq), tq)
            dob = do_ref[rows, :].astype(BF16)
            for hh in range(2):
                qb, kb, vb = q_ref[hh, rows, :], k_ref[hh], v_s[hh]
                p = jnp.exp(_dot(kb, qb, NT) - lse_s[hh, i][:1])
                if masked:
                    p = jnp.where(row <= col, p, 0.0)
                dv_s[hh] += _dot(p.astype(BF16), dob)
                ds = (p * (_dot(vb, dob, NT) - d_s[hh, i][:1])).astype(BF16)
                dk_s[hh] += _dot(ds, qb)
                dqt_s[hh, i] += _dot(kt_s[hh], ds)

        def full_step(i, carry):
            step(i, False)
            return carry

        step(j, True)
        lax.fori_loop(j + 1, nq, full_step, 0)
        dk_ref[...] = dk_s[...]
        dv_ref[...] = dv_s[...]

        @pl.when(j == nq - 1)
        def _():
            def untranspose(i, carry):
                rows = pl.ds(pl.multiple_of(i * tq, tq), tq)
                for hh in range(2):
                    dq_ref[hh, rows, :] = jnp.transpose(dqt_s[hh, i])
                return carry

            lax.fori_loop(0, nq, untranspose, 0)

    whole = pl.BlockSpec((2, seq, LANES), lambda p, j: (p, 0, 0))
    blk = pl.BlockSpec((2, tq, LANES), lambda p, j: (p, j, 0))
    pair = pl.BlockSpec((seq, LANES), lambda p, j: (0, p))
    shape = jax.ShapeDtypeStruct((MLA_HEADS, seq, LANES), F32)
    return pl.pallas_call(
        body, name="mla_bwd", grid=(MLA_HEADS // 2, nq),
        in_specs=[whole, blk, blk, pair, pair, whole],
        out_specs=[whole, blk, blk], out_shape=[shape] * 3,
        scratch_shapes=[pltpu.VMEM((2, nq, 8, tq), F32), pltpu.VMEM((2, nq, 8, tq), F32),
                        pltpu.VMEM((2, tq, LANES), F32), pltpu.VMEM((2, tq, LANES), F32),
                        pltpu.VMEM((2, tq, LANES), BF16), pltpu.VMEM((2, LANES, tq), BF16),
                        pltpu.VMEM((2, nq, LANES, tq), F32)],
        compiler_params=_cp(("arbitrary", "arbitrary"), VMEM_LIMIT),
    )(q, k, v, d_o, o, lse)


def _dil_bwd(qr, kr, vb, d_o, o, lse):
    seq = qr.shape[0]
    nq = DIL_Q_BWD
    n_tiles = seq // nq
    chunk = 512

    def body(q_ref, k_ref, v_ref, do_ref, o_ref, lse_ref, dq_ref, dk_ref, dv_ref, d_s, dq_s, dk_s, dv_s, bias_s):
        lane = lax.broadcasted_iota(jnp.int32, (nq, LANES), 1)
        lanec = lax.broadcasted_iota(jnp.int32, (chunk, LANES), 1)
        bias_s[0], bias_s[1] = [b[:nq] for b in _dil_bias(nq)]

        def rowsum(i, carry):
            rows = pl.ds(pl.multiple_of(i * chunk, chunk), chunk)
            prod = do_ref[rows, :] * o_ref[rows, :]
            lo = jnp.sum(jnp.where(lanec < 64, prod, 0.0), axis=1, keepdims=True)
            hi = jnp.sum(jnp.where(lanec >= 64, prod, 0.0), axis=1, keepdims=True)
            d_s[rows, :] = jnp.where(lanec < 64, lo, hi)
            return carry

        lax.fori_loop(0, seq // chunk, rowsum, 0)
        dq_s[...] = jnp.zeros_like(dq_s)
        dk_s[...] = jnp.zeros_like(dk_s)
        dv_s[...] = jnp.zeros_like(dv_s)
        for d in DIL_DILATIONS:

            def tile(t, carry, d=d):
                first, start, prev = _dil_tile_index(t, d, seq, nq)
                rows, prows = _dil_rows(start, d, nq), _dil_rows(prev, d, BLOCK)
                q_t, do_t = q_ref[rows, :], do_ref[rows, :]
                lse_t, d_t = lse_ref[rows, :], d_s[rows, :]
                kcat = jnp.concatenate([k_ref[prows, :], k_ref[rows, :]], axis=0).astype(BF16)
                vcat = jnp.concatenate([v_ref[prows, :], v_ref[rows, :]], axis=0).astype(BF16)
                bias = bias_s[first]
                dq_t = jnp.zeros((nq, LANES), F32)
                dkcat = jnp.zeros((BLOCK + nq, LANES), F32)
                dvcat = jnp.zeros((BLOCK + nq, LANES), F32)
                for hh in range(2):
                    mine = (lane >= 64) if hh else (lane < 64)
                    c0 = 64 * hh
                    qh = jnp.where(mine, q_t, 0.0).astype(BF16)
                    doh = jnp.where(mine, do_t, 0.0).astype(BF16)
                    p = jnp.exp(_dot(qh, kcat, NT) + bias - lse_t[:, c0:c0 + 1])
                    dvcat = dvcat + _dot(p.astype(BF16), doh, TN)
                    dp = _dot(doh, vcat, NT)
                    ds = (p * (dp - d_t[:, c0:c0 + 1])).astype(BF16)
                    dq_t = dq_t + jnp.where(mine, _dot(ds, kcat), 0.0)
                    dkcat = dkcat + _dot(ds, qh, TN)
                dq_s[rows, :] += dq_t
                dk_s[prows, :] += dkcat[:BLOCK]
                dk_s[rows, :] += dkcat[BLOCK:]
                dv_s[prows, :] += dvcat[:BLOCK]
                dv_s[rows, :] += dvcat[BLOCK:]
                return carry

            lax.fori_loop(0, n_tiles, tile, 0, unroll=4)
        dq_ref[...] = dq_s[...].astype(BF16)
        dk_ref[...] = dk_s[...].astype(BF16)
        dv_ref[...] = dv_s[...].astype(BF16)

    col = lambda off: pl.BlockSpec((seq, LANES), lambda p: (0, p + off))
    shape = jax.ShapeDtypeStruct((seq, 4 * LANES), BF16)
    return pl.pallas_call(
        body, name="dil_bwd", grid=(4,),
        in_specs=[col(0), col(0), col(0), col(4), col(0), pl.BlockSpec((None, seq, LANES), lambda p: (p, 0, 0))],
        out_specs=[col(0)] * 3, out_shape=[shape] * 3,
        scratch_shapes=[pltpu.VMEM((seq, LANES), F32)] * 4 + [pltpu.VMEM((2, nq, BLOCK + nq), F32)],
        compiler_params=_cp(("arbitrary",), VMEM_LIMIT),
    )(qr, kr, vb, d_o, o, lse)


def _in_bwd(dz, cq, ckv, gq, gkv, wuq_e, wukv, ct, st, dq, dk, dv, dgates, dqr, dkr, dvb, cd, sd, w_in_p):
    seq = dz.shape[0]
    tr = 512

    def body(dz_ref, cq_ref, ckv_ref, gq_ref, gkv_ref, wuq_ref, wukv_ref, ct_ref, st_ref, dq_ref, dk_ref, dv_ref,
             dg_ref, dqr_ref, dkr_ref, dvb_ref, cd_ref, sd_ref, w_ref,
             gx_ref, dh_ref, dwuq_ref, dwukv_ref, dgq_ref, dgkv_ref):
        @pl.when(pl.program_id(0) == 0)
        def _():
            dwuq_ref[...] = jnp.zeros_like(dwuq_ref)
            dwukv_ref[...] = jnp.zeros_like(dwukv_ref)
            dgq_ref[...] = jnp.zeros_like(dgq_ref)
            dgkv_ref[...] = jnp.zeros_like(dgkv_ref)

        lane = lax.broadcasted_iota(jnp.int32, (tr, LANES), 1)
        rope_lanes = jnp.logical_and(lane >= 64, lane < 96)
        ct_, st_ = ct_ref[...], st_ref[...]

        def mla_rope_t(g):
            return ct_ * g + jnp.where(rope_lanes, _mla_rot(st_ * g, lane), 0.0)

        def norm_bwd(c, g, dn, dg_ref):
            r, _ = _rms(c, g)
            u = dn * g
            dg_ref[...] += jnp.sum(dn * c * r, axis=0, keepdims=True)
            return r * u - c * (r * r * r) * jnp.mean(u * c, axis=-1, keepdims=True)

        c, g = cq_ref[...], gq_ref[...]
        _, qn = _rms(c, g)
        dq_all = jnp.concatenate([mla_rope_t(dq_ref[h] * MLA_SCALE) for h in range(MLA_HEADS)], axis=1).astype(BF16)
        dwuq_ref[...] += _dot(qn.astype(BF16), dq_all, TN)
        dcq = norm_bwd(c, g, _dot(dq_all, wuq_ref[...], NT), dgq_ref).astype(BF16)

        c, g = ckv_ref[...], gkv_ref[...]
        _, kvn = _rms(c, g)
        dkpe = jnp.zeros((tr, LANES), F32)
        parts = []
        for h in range(MLA_HEADS):
            dk_h, dv_h = dk_ref[h], dv_ref[h]
            if h % 2 == 0:
                dv_h = pltpu.roll(dv_h, 64, 1)
            parts.append(jnp.where(lane < 64, dk_h, dv_h))
            dkpe = dkpe + jnp.where(rope_lanes, dk_h, 0.0)
        dkv_all = jnp.concatenate(parts, axis=1).astype(BF16)
        dwukv_ref[...] += _dot(kvn.astype(BF16), dkv_all, TN)
        dckv = norm_bwd(c, g, _dot(dkv_all, wukv_ref[...], NT), dgkv_ref).astype(BF16)
        dkrope = mla_rope_t(dkpe).astype(BF16)

        rot_lanes = lane % 64 < DIL_ROT
        cd_, sd_ = cd_ref[...], sd_ref[...]

        def dil_rope_t(g):
            return cd_ * g + jnp.where(rot_lanes, _dil_rot(sd_ * g, lane), 0.0)

        dqb = [dil_rope_t(dqr_ref[:, LANES * p:LANES * (p + 1)].astype(F32) * DIL_SCALE).astype(BF16) for p in range(4)]
        dkb = [dil_rope_t(dkr_ref[:, LANES * p:LANES * (p + 1)].astype(F32)).astype(BF16) for p in range(4)]
        dh = jnp.concatenate([dcq, dckv, dg_ref[...]] + dqb + dkb + [dvb_ref[...], dkrope], axis=1)
        dh_ref[...] = dh
        gx_ref[...] = ALPHA * dz_ref[...] + _dot(dh, w_ref[...], NT)

    row = lambda w: pl.BlockSpec((tr, w), lambda i: (i, 0))
    full = lambda a: pl.BlockSpec(a.shape, lambda i: (0,) * a.ndim)
    head = pl.BlockSpec((MLA_HEADS, tr, LANES), lambda i: (0, i, 0))
    return pl.pallas_call(
        body, name="in_bwd", grid=(seq // tr,),
        in_specs=[row(D_MODEL), row(Q_LORA), row(KV_LORA), full(gq), full(gkv), full(wuq_e), full(wukv), row(LANES),
                  row(LANES), head, head, head, row(D_MODEL), row(512), row(512), row(512), row(LANES), row(LANES),
                  full(w_in_p)],
        out_specs=[row(D_MODEL), row(IN_WIDTH_PAD), full(wuq_e), full(wukv), full(gq), full(gkv)],
        out_shape=[jax.ShapeDtypeStruct((seq, D_MODEL), F32), jax.ShapeDtypeStruct((seq, IN_WIDTH_PAD), BF16),
                   jax.ShapeDtypeStruct(wuq_e.shape, F32), jax.ShapeDtypeStruct(wukv.shape, F32),
                   jax.ShapeDtypeStruct(gq.shape, F32), jax.ShapeDtypeStruct(gkv.shape, F32)],
        compiler_params=_cp(("arbitrary",), VMEM_LIMIT),
    )(dz, cq, ckv, gq, gkv, wuq_e, wukv, ct, st, dq, dk, dv, dgates, dqr, dkr, dvb, cd, sd, w_in_p)


def _dw_in(x, dh):
    seq = dh.shape[0]
    tk = 512
    tn = IN_WIDTH_PAD // 2

    def body(x_ref, dh_ref, o_ref):
        @pl.when(pl.program_id(1) == 0)
        def _():
            o_ref[...] = jnp.zeros_like(o_ref)

        o_ref[...] += _dot(dh_ref[...], x_ref[...].astype(BF16), TN)

    return pl.pallas_call(
        body, name="dw_in", grid=(2, seq // tk),
        in_specs=[pl.BlockSpec((tk, D_MODEL), lambda n, k: (k, 0)), pl.BlockSpec((tk, tn), lambda n, k: (k, n))],
        out_specs=pl.BlockSpec((tn, D_MODEL), lambda n, k: (n, 0)),
        out_shape=jax.ShapeDtypeStruct((IN_WIDTH_PAD, D_MODEL), F32),
        compiler_params=_cp(("arbitrary", "arbitrary"), VMEM_LIMIT),
    )(x, dh)


def _adamw(w, g, m, v, name):
    rows, cols = w.shape
    tc = 256 if cols % 256 == 0 and rows * cols > 2 ** 18 else cols

    def body(w_ref, g_ref, m_ref, v_ref, d_ref, nm_ref, nv_ref):
        g_ = g_ref[...]
        nm = ADAM_B1 * m_ref[...] + (1.0 - ADAM_B1) * g_
        nv = ADAM_B2 * v_ref[...] + (1.0 - ADAM_B2) * jnp.square(g_)
        m_hat = nm / (1.0 - ADAM_B1 ** ADAM_STEP)
        v_hat = nv / (1.0 - ADAM_B2 ** ADAM_STEP)
        d_ref[...] = -ADAM_LR * (m_hat / (jnp.sqrt(v_hat) + ADAM_EPS) + ADAM_WD * w_ref[...])
        nm_ref[...] = nm
        nv_ref[...] = nv

    spec = pl.BlockSpec((rows, tc), lambda i: (0, i))
    return pl.pallas_call(
        body, name=name, grid=(cols // tc,), in_specs=[spec] * 4, out_specs=[spec] * 3,
        out_shape=[jax.ShapeDtypeStruct(w.shape, F32)] * 3, compiler_params=_cp(("arbitrary",)),
    )(w, g, m, v)


def _pad_row(v):
    return jnp.pad(v.reshape(1, -1), ((0, 0), (0, D_MODEL - v.shape[-1])))


def _local_step(x2, target, w_in_p, w_uq_f, wukv_f, w_out_f, q_norm_g, kv_norm_g, ln_g, ln_b):
    seq = x2.shape[0]
    wuq_e = jnp.pad(w_uq_f.reshape(Q_LORA, MLA_HEADS, 96), ((0, 0), (0, 0), (0, 32))).reshape(Q_LORA, MLA_HEADS * LANES)
    ct, st, cd, sd = _rope_tables(seq)
    gq = q_norm_g.reshape(1, Q_LORA)
    gkv = kv_norm_g.reshape(1, KV_LORA)

    cq, ckv, gates, qr, krot, vb, q_e, k_e, v_e = _proj(x2, w_in_p, gq, gkv, wuq_e, wukv_f, ct, st, cd, sd)
    o_a, lse_a = _mla_fwd(q_e, k_e, v_e)
    o_b, lse_b = _dil_fwd(qr, krot, vb)

    dz, d_o, d_gates, dw_out, dln_g, dln_b, loss_part = _post(
        x2, o_a, o_b, gates, w_out_f, ln_g.reshape(1, D_MODEL), ln_b.reshape(1, D_MODEL), target)
    dq_e, dk_e, dv_e = _mla_bwd(q_e, k_e, v_e, d_o, o_a, lse_a)
    dqr, dkr, dvb = _dil_bwd(qr, krot, vb, d_o, o_b, lse_b)
    grad_x, dh, dwuq_e, dwukv, dgq, dgkv = _in_bwd(
        dz, cq, ckv, gq, gkv, wuq_e, wukv_f, ct, st, dq_e, dk_e, dv_e, d_gates, dqr, dkr, dvb, cd, sd, w_in_p)
    dw_in = _dw_in(x2, dh)
    dw_uq = dwuq_e.reshape(Q_LORA, MLA_HEADS, LANES)[:, :, :96].reshape(Q_LORA, MLA_HEADS * 96)
    return loss_part, grad_x, dw_in, dw_uq, dwukv, dw_out, dgq, dgkv, dln_g, dln_b


def kernel(x, w_in, q_norm_g, kv_norm_g, w_uq, w_ukv, w_out, ln_g, ln_b, loss_target, m_w_in, m_q_norm_g, m_kv_norm_g, m_w_uq, m_w_ukv, m_w_out, m_ln_g, m_ln_b, v_w_in, v_q_norm_g, v_kv_norm_g, v_w_uq, v_w_ukv, v_w_out, v_ln_g, v_ln_b):
    seq = x.shape[1]
    x2 = x.reshape(seq, D_MODEL)
    target = loss_target.reshape(seq, D_MODEL)

    g_w_in, g_w_uq, g_w_ukv, g_w_out = _all_gather_weights([w_in, w_uq, w_ukv, w_out])
    by_cols = lambda g: jnp.concatenate([g[j] for j in range(N_SHARD)], axis=1)
    loss_part, grad_x, dw_in, dw_uq, dwukv, dw_out, dgq, dgkv, dln_g, dln_b = _local_step(
        x2, target, _permute_w_in_shards(g_w_in), by_cols(g_w_uq), by_cols(g_w_ukv), g_w_out.reshape(D_MODEL, D_MODEL),
        q_norm_g, kv_norm_g, ln_g, ln_b)

    to_shards = lambda d: d.reshape(d.shape[0], N_SHARD, d.shape[1] // N_SHARD).transpose(1, 0, 2)
    grads = [dw_in, to_shards(dw_uq), to_shards(dwukv), dw_out.reshape(N_SHARD, 256, D_MODEL)]
    small = jnp.concatenate([_pad_row(dgq), _pad_row(dgkv), dln_g, dln_b, _pad_row(loss_part),
                             jnp.zeros((3, D_MODEL), F32)], axis=0)
    *chip_sums, smalls = _reduce_over_sibling(grads, small)
    g_in_t, g_uq, g_ukv, g_out = _reduce_over_chips(chip_sums)
    g_in = g_in_t.T
    small_sum = _sum_smalls(smalls)
    loss = small_sum[4, 0]

    big = [[o.T for o in _adamw(w.T, g.T, m.T, v.T, name)] for w, g, m, v, name in (
        (w_in, g_in, m_w_in, v_w_in, "adamw_w_in"), (w_uq, g_uq, m_w_uq, v_w_uq, "adamw_w_uq"))]
    big += [_adamw(w, g, m, v, name) for w, g, m, v, name in (
        (w_ukv, g_ukv, m_w_ukv, v_w_ukv, "adamw_w_ukv"), (w_out, g_out, m_w_out, v_w_out, "adamw_w_out"))]
    vec = lambda a, b, c_, d: jnp.concatenate([_pad_row(a), _pad_row(b), _pad_row(c_), _pad_row(d),
                                               jnp.zeros((4, D_MODEL), F32)], axis=0)
    sw = vec(q_norm_g, kv_norm_g, ln_g, ln_b)
    sm = vec(m_q_norm_g, m_kv_norm_g, m_ln_g, m_ln_b)
    sv = vec(v_q_norm_g, v_kv_norm_g, v_ln_g, v_ln_b)
    sg = jnp.concatenate([small_sum[:4], jnp.zeros((4, D_MODEL), F32)], axis=0)
    s_delta, s_m, s_v = _adamw(sw, sg, sm, sv, "adamw_vectors")

    def vectors(a):
        return [a[0, :Q_LORA], a[1, :KV_LORA], a[2], a[3]]

    def ordered(bigs, smalls_):
        return [bigs[0], smalls_[0], smalls_[1], bigs[1], bigs[2], bigs[3], smalls_[2], smalls_[3]]

    grads_out = ordered([g_in, g_uq, g_ukv, g_out], vectors(small_sum))
    deltas = ordered([b[0] for b in big], vectors(s_delta))
    new_m = ordered([b[1] for b in big], vectors(s_m))
    new_v = ordered([b[2] for b in big], vectors(s_v))
    return (loss, grad_x.reshape(x.shape), *grads_out, *deltas, *new_m, *new_v)
```

```python
import functools

import jax
import jax.numpy as jnp
import numpy as np
from jax import lax
from jax.experimental import pallas as pl
from jax.experimental.pallas import tpu as pltpu

F32 = jnp.float32
BF16 = jnp.bfloat16

D_MODEL = 1024
ROPE_THETA = 500000.0
BLOCK = 128
NEG = -1e30
RMS_EPS = 1e-6
LN_EPS = 1e-5

MLA_HEADS = 8
MLA_NOPE = 64
MLA_ROPE = 32
Q_LORA = 384
KV_LORA = 256
DIL_HEADS = 8
DIL_HEAD_DIM = 64
DIL_ROT = 16
DIL_DILATIONS = (1, 4, 16)
IN_WIDTH = 3232
IN_WIDTH_PAD = 3328
ONES_LANE = (64, 0)
MLA_SCALE = (MLA_NOPE + MLA_ROPE) ** -0.5
DIL_SCALE = DIL_HEAD_DIM ** -0.5
ALPHA = 2.0 ** 0.25

ADAM_LR = 0.001
ADAM_B1 = 0.9
ADAM_B2 = 0.999
ADAM_EPS = 1e-08
ADAM_WD = 0.01
ADAM_STEP = 10

N_SHARD = 4
SHARD_SHAPES = ((1024, 808), (384, 192), (256, 256), (256, 1024))
GRAD_SHAPES = ((808, 1024), (384, 192), (256, 256), (256, 1024))
GRAD_SPLIT_COLS = (True, False, False, False)
ROW_CHUNK = 64
LANES = 128
VMEM_LIMIT = 56 * 1024 * 1024
MESH = pl.DeviceIdType.MESH

NT = (((1,), (1,)), ((), ()))
TN = (((0,), (0,)), ((), ()))


def _cp(sem=None, vmem=None):
    return pltpu.CompilerParams(dimension_semantics=sem, vmem_limit_bytes=vmem)


def _dot(a, b, dims=None):
    if dims is None:
        return jnp.dot(a, b, preferred_element_type=F32)
    return lax.dot_general(a, b, dims, preferred_element_type=F32)


def _rope_tables(seq):
    f32 = np.float32
    pos = np.arange(seq, dtype=f32)[:, None]
    one, zero = np.ones((seq, 64), f32), np.zeros((seq, 64), f32)

    def cos_sin(dim):
        inv = np.power(f32(ROPE_THETA), -np.arange(0, dim, 2, dtype=f32) / f32(dim)).astype(f32)
        ang = (pos * inv[None, :]).astype(f32)
        return np.cos(ang).astype(f32), np.sin(ang).astype(f32)

    cos, sin = cos_sin(MLA_ROPE)
    ct = np.concatenate([one, cos, cos, zero[:, :32]], axis=1)
    st = np.concatenate([zero, -sin, sin, zero[:, :32]], axis=1)
    cos, sin = cos_sin(DIL_ROT)
    cd = np.concatenate([cos, cos, one[:, :48]], axis=1)
    sd = np.concatenate([-sin, sin, zero[:, :48]], axis=1)
    return tuple(jnp.asarray(t) for t in (ct, st, np.tile(cd, (1, 2)), np.tile(sd, (1, 2))))


W_IN_ORDER = ((0, 640), (672, 1184), (2720, 3232), (1184, 2720), None, (640, 672))


def _permute_w_in(w):
    z = jnp.zeros((w.shape[0], 64), w.dtype)
    parts = [z if r is None else w[:, r[0]:r[1]] for r in W_IN_ORDER]
    return jnp.concatenate(parts + [z[:, :32]], axis=1)


def _permute_w_in_shards(g):
    width = g.shape[2]
    z = jnp.zeros((g.shape[1], 64), g.dtype)
    parts = []
    for r in W_IN_ORDER:
        if r is None:
            parts.append(z)
            continue
        for j in range(N_SHARD):
            lo, hi = max(r[0], width * j), min(r[1], width * (j + 1))
            if lo < hi:
                parts.append(g[j, :, lo - width * j:hi - width * j])
    return jnp.concatenate(parts + [z[:, :32]], axis=1)


def _w_in_row_pieces():
    width = GRAD_SHAPES[0][0]
    pieces, at = [], 0
    for r in W_IN_ORDER:
        if r is None:
            at += 64
            continue
        for k in range(N_SHARD):
            lo, hi = max(r[0], width * k), min(r[1], width * (k + 1))
            if lo < hi:
                pieces.append((k, lo - width * k, at + lo - r[0], hi - lo))
        at += r[1] - r[0]
    return pieces


def _position():
    return lax.axis_index("x"), lax.axis_index("y"), lax.axis_index("c")


def _halves(c, rows):
    hr = rows // 2
    return pl.ds(pl.multiple_of(c * hr, 8), hr), pl.ds(pl.multiple_of((1 - c) * hr, 8), hr)


def _for_row_chunks(rows, fn):
    def step(i, carry):
        fn(pl.multiple_of(i * ROW_CHUNK, ROW_CHUNK))
        return carry

    lax.fori_loop(0, rows // ROW_CHUNK, step, 0)


def _all_gather_weights(shards):
    n = len(shards)

    def body(*refs):
        ins, outs = refs[:n], refs[n:2 * n]
        send_sems, recv_sems = refs[2 * n:]
        x, y, c = _position()
        me = 2 * x + y
        chips = [(1 - x, y), (x, 1 - y), (1 - x, 1 - y)]
        for a in range(n):
            def cast(r, a=a):
                outs[a][me, pl.ds(r, ROW_CHUNK), :] = ins[a][pl.ds(r, ROW_CHUNK), :].astype(BF16)

            _for_row_chunks(SHARD_SHAPES[a][0], cast)

        def copy(k, a, slot, rows, to):
            ref = outs[a].at[slot, rows]
            return pltpu.make_async_remote_copy(
                src_ref=ref, dst_ref=ref, send_sem=send_sems.at[k * n + a], recv_sem=recv_sems.at[k * n + a],
                device_id=to, device_id_type=MESH)

        half = [_halves(c, SHARD_SHAPES[a][0])[0] for a in range(n)]
        other = [_halves(c, SHARD_SHAPES[a][0])[1] for a in range(n)]
        first = [copy(k, a, me, half[a], (px, py, c)) for k, (px, py) in enumerate(chips) for a in range(n)]
        for cp in first:
            cp.start()
        passed = []
        for k, (px, py) in enumerate(chips):
            for a in range(n):
                copy(k, a, 2 * px + py, half[a], (x, y, c)).wait_recv()
                cp = copy(3 + k, a, 2 * px + py, half[a], (x, y, 1 - c))
                cp.start()
                passed.append(cp)
        for k, (px, py) in enumerate(chips):
            for a in range(n):
                copy(3 + k, a, 2 * px + py, other[a], (x, y, c)).wait_recv()
        for cp in first + passed:
            cp.wait_send()

    vmem = pl.BlockSpec(memory_space=pltpu.VMEM)
    return pl.pallas_call(
        body, name="all_gather_weights",
        out_shape=[jax.ShapeDtypeStruct((N_SHARD,) + s, BF16) for s in SHARD_SHAPES],
        in_specs=[vmem] * n, out_specs=[vmem] * n,
        scratch_shapes=[pltpu.SemaphoreType.DMA((6 * n,)), pltpu.SemaphoreType.DMA((6 * n,))],
        compiler_params=_cp(None, VMEM_LIMIT),
    )(*shards)


def _grad_half_shape(a):
    rows, cols = GRAD_SHAPES[a]
    return (rows, cols // 2) if GRAD_SPLIT_COLS[a] else (rows // 2, cols)


def _grad_half(a, c):
    rows, cols = GRAD_SHAPES[a]
    if GRAD_SPLIT_COLS[a]:
        return slice(None), pl.ds(pl.multiple_of(c * (cols // 2), LANES), cols // 2)
    return pl.ds(pl.multiple_of(c * (rows // 2), ROW_CHUNK), rows // 2), slice(None)


def _grad_chunks(a, c):
    rows, cols = GRAD_SHAPES[a]
    if GRAD_SPLIT_COLS[a]:
        return [((slice(None), pl.ds(c0, LANES)),
                 (slice(None), pl.ds(pl.multiple_of(c * (cols // 2) + c0, LANES), LANES)))
                for c0 in range(0, cols // 2, LANES)]
    return [((pl.ds(r0, ROW_CHUNK), slice(None)),
             (pl.ds(pl.multiple_of(c * (rows // 2) + r0, ROW_CHUNK), ROW_CHUNK), slice(None)))
            for r0 in range(0, rows // 2, ROW_CHUNK)]


def _reduce_over_sibling(grads, small_rows):
    n = len(grads)
    n_small = len(small_rows)
    pieces = _w_in_row_pieces()

    def body(*refs):
        g_hbm, rows_in = refs[:n], refs[n:n + n_small]
        sums, small_sum = refs[n + n_small:2 * n + n_small], refs[2 * n + n_small]
        scratch = refs[2 * n + n_small + 1:]
        stage, got = scratch[:n], scratch[n:2 * n]
        sm, smalls, send_sems, recv_sems, local_sems = scratch[2 * n:]
        x, y, c = _position()
        me = 4 * x + 2 * y + c
        sm[...] = jnp.zeros_like(sm)
        for i, row in enumerate(rows_in):
            sm[i:i + 1, 0:row.shape[1]] = row[...]
        loads = [[pltpu.make_async_copy(g_hbm[0].at[pl.ds(src, rows)], stage[0].at[k, pl.ds(dst, rows)],
                                        local_sems.at[n + i])
                  for i, (k, dst, src, rows) in enumerate(pieces)]]
        loads += [[pltpu.make_async_copy(g_hbm[a], stage[a], local_sems.at[a])] for a in range(1, n)]
        for group in loads:
            for ld in group:
                ld.start()
        smalls[me] = sm[...]
        sends = []
        for rel in range(1, 8):
            px = 1 - x if rel // 4 else x
            py = 1 - y if (rel // 2) % 2 else y
            pc = 1 - c if rel % 2 else c
            cp = pltpu.make_async_remote_copy(
                src_ref=sm, dst_ref=smalls.at[me], send_sem=send_sems.at[n + rel], recv_sem=recv_sems.at[n + rel],
                device_id=(px, py, pc), device_id_type=MESH)
            cp.start()
            sends.append((cp, 4 * px + 2 * py + pc))
        swaps = []
        for a in range(n):
            for ld in loads[a]:
                ld.wait()
            cp = pltpu.make_async_remote_copy(
                src_ref=stage[a].at[(slice(None),) + _grad_half(a, 1 - c)], dst_ref=got[a], send_sem=send_sems.at[a], recv_sem=recv_sems.at[a],
                device_id=(x, y, 1 - c), device_id_type=MESH)
            cp.start()
            swaps.append(cp)
        for a in range(n):
            swaps[a].wait_recv()
            for k in range(N_SHARD):
                for in_half, in_whole in _grad_chunks(a, c):
                    pair = stage[a][(k,) + in_whole] + got[a][(k,) + in_half]
                    sums[a][(k,) + in_half] = pair.astype(BF16)
        for rel, (cp, peer) in enumerate(sends, start=1):
            pltpu.make_async_remote_copy(
                src_ref=sm, dst_ref=smalls.at[peer], send_sem=send_sems.at[n + rel], recv_sem=recv_sems.at[n + rel],
                device_id=(x, y, c), device_id_type=MESH).wait_recv()
        total = smalls[0]
        for dev in range(1, 8):
            total = total + smalls[dev]
        small_sum[...] = total
        for cp in swaps:
            cp.wait_send()
        for cp, _ in sends:
            cp.wait_send()

    vmem = pl.BlockSpec(memory_space=pltpu.VMEM)
    half = [(N_SHARD,) + _grad_half_shape(a) for a in range(n)]
    return pl.pallas_call(
        body, name="reduce_over_sibling",
        out_shape=[jax.ShapeDtypeStruct(s, BF16) for s in half] + [jax.ShapeDtypeStruct((8, D_MODEL), F32)],
        in_specs=[pl.BlockSpec(memory_space=pl.ANY)] * n + [vmem] * n_small, out_specs=[vmem] * (n + 1),
        scratch_shapes=[pltpu.VMEM((N_SHARD,) + s, F32) for s in GRAD_SHAPES] + [pltpu.VMEM(s, F32) for s in half]
        + [pltpu.VMEM((8, D_MODEL), F32), pltpu.VMEM((8, 8, D_MODEL), F32),
           pltpu.SemaphoreType.DMA((n + 8,)), pltpu.SemaphoreType.DMA((n + 8,)),
           pltpu.SemaphoreType.DMA((n + len(pieces),))],
        compiler_params=_cp(None, VMEM_LIMIT),
    )(*grads, *small_rows)


def _reduce_over_chips(sums):
    n = len(sums)

    def body(*refs):
        h, outs, got = refs[:n], refs[n:2 * n], refs[2 * n:3 * n]
        send_sems, recv_sems = refs[3 * n:]
        x, y, c = _position()
        me = 2 * x + y
        chips = [(1 - x, y), (x, 1 - y), (1 - x, 1 - y)]
        sends = []
        for k, (px, py) in enumerate(chips):
            for a in range(n):
                cp = pltpu.make_async_remote_copy(
                    src_ref=h[a].at[2 * px + py], dst_ref=got[a].at[k], send_sem=send_sems.at[k * n + a],
                    recv_sem=recv_sems.at[k * n + a], device_id=(px, py, c), device_id_type=MESH)
                cp.start()
                sends.append(cp)
        for cp in sends:
            cp.wait_recv()
        joins = []
        for a in range(n):
            for in_half, in_whole in _grad_chunks(a, c):
                total = h[a][(me,) + in_half].astype(F32)
                for k in range(3):
                    total = total + got[a][(k,) + in_half].astype(F32)
                outs[a][in_whole] = total
            half = outs[a].at[_grad_half(a, c)]
            cp = pltpu.make_async_remote_copy(
                src_ref=half, dst_ref=half, send_sem=send_sems.at[3 * n + a],
                recv_sem=recv_sems.at[3 * n + a], device_id=(x, y, 1 - c), device_id_type=MESH)
            cp.start()
            joins.append(cp)
        for a in range(n):
            other = outs[a].at[_grad_half(a, 1 - c)]
            pltpu.make_async_remote_copy(
                src_ref=other, dst_ref=other, send_sem=send_sems.at[3 * n + a],
                recv_sem=recv_sems.at[3 * n + a], device_id=(x, y, c), device_id_type=MESH).wait_recv()
        for cp in sends + joins:
            cp.wait_send()

    vmem = pl.BlockSpec(memory_space=pltpu.VMEM)
    return pl.pallas_call(
        body, name="reduce_over_chips",
        out_shape=[jax.ShapeDtypeStruct(s, F32) for s in GRAD_SHAPES],
        in_specs=[vmem] * n, out_specs=[vmem] * n,
        scratch_shapes=[pltpu.VMEM((3,) + _grad_half_shape(a), BF16) for a in range(n)]
        + [pltpu.SemaphoreType.DMA((4 * n,)), pltpu.SemaphoreType.DMA((4 * n,))],
        compiler_params=_cp(None, VMEM_LIMIT),
    )(*sums)


def _proj(x, w_in_p, gq, gkv, wuq_e, wukv, ct, st, cd, sd):
    seq = x.shape[0]
    tr = 512

    def body(x_ref, w_ref, gq_ref, gkv_ref, wuq_ref, wukv_ref, ct_ref, st_ref, cd_ref, sd_ref,
             cq_ref, ckv_ref, g_ref, qr_ref, kr_ref, vb_ref, q_out, k_out, v_out):
        lane = lax.broadcasted_iota(jnp.int32, (tr, LANES), 1)
        xb = x_ref[...].astype(BF16)
        cq = _dot(xb, w_ref[:, 0:384])
        ckv = _dot(xb, w_ref[:, 384:640])
        cq_ref[...] = cq
        ckv_ref[...] = ckv
        g_ref[...] = _dot(xb, w_ref[:, 640:1664])

        cd_, sd_ = cd_ref[...], sd_ref[...]
        qb = _dot(xb, w_ref[:, 1664:2176])
        kb = _dot(xb, w_ref[:, 2176:2688])
        for p in range(4):
            cols = slice(LANES * p, LANES * (p + 1))
            t = qb[:, cols]
            qr_ref[:, cols] = (t * cd_ + _dil_rot(t, lane) * sd_) * DIL_SCALE
            t = kb[:, cols]
            kr_ref[:, cols] = t * cd_ + _dil_rot(t, lane) * sd_
        vb_ref[...] = _dot(xb, w_ref[:, 2688:3200])

        ct_, st_ = ct_ref[...], st_ref[...]

        def rope(t):
            return t * ct_ + _mla_rot(t, lane) * st_

        _, qn = _rms(cq, gq_ref[...])
        q_all = _dot(qn.astype(BF16), wuq_ref[...])
        for h in range(MLA_HEADS):
            q_out[h] = (rope(q_all[:, LANES * h:LANES * (h + 1)]) * MLA_SCALE).astype(BF16)
        _, kvn = _rms(ckv, gkv_ref[...])
        kv_all = _dot(kvn.astype(BF16), wukv_ref[...])
        kpe = rope(_dot(xb, w_ref[:, 3200:3328]))
        for h in range(MLA_HEADS):
            kv_h = kv_all[:, LANES * h:LANES * (h + 1)]
            k_out[h] = jnp.where(lane < 64, kv_h, kpe).astype(BF16)
            if h % 2:
                v = jnp.where(lane >= 64, kv_h, 0.0)
            else:
                v = jnp.where(lane < 64, pltpu.roll(kv_h, 64, 1), 0.0)
            v_out[h] = jnp.where(lane == ONES_LANE[h % 2], 1.0, v).astype(BF16)

    row = lambda w: pl.BlockSpec((tr, w), lambda i: (i, 0))
    full = lambda a: pl.BlockSpec(a.shape, lambda i: (0,) * a.ndim)
    head = pl.BlockSpec((MLA_HEADS, tr, LANES), lambda i: (0, i, 0))
    widths = (Q_LORA, KV_LORA, D_MODEL, 512, 512, 512)
    return pl.pallas_call(
        body, name="proj", grid=(seq // tr,),
        in_specs=[row(D_MODEL), full(w_in_p), full(gq), full(gkv), full(wuq_e), full(wukv)] + [row(LANES)] * 4,
        out_specs=[row(w) for w in widths] + [head] * 3,
        out_shape=[jax.ShapeDtypeStruct((seq, w), F32) for w in widths]
        + [jax.ShapeDtypeStruct((MLA_HEADS, seq, LANES), BF16)] * 3,
        compiler_params=_cp(("arbitrary",), VMEM_LIMIT),
    )(x, w_in_p, gq, gkv, wuq_e, wukv, ct, st, cd, sd)


def _mla_rot(t, lane):
    return jnp.where(lane < 80, pltpu.roll(t, 112, 1), pltpu.roll(t, 16, 1))


def _dil_rot(t, lane):
    return jnp.where(lane % 64 < 8, pltpu.roll(t, 120, 1), pltpu.roll(t, 8, 1))


def _rms(c, g):
    r = lax.rsqrt(jnp.mean(c * c, axis=-1, keepdims=True) + RMS_EPS)
    return r, c * r * g


def _mla_fwd(q, k, v):
    seq = q.shape[1]
    tq = 512
    nq = seq // tq

    def body(q_ref, k_ref, v_ref, o_ref, lse_ref, m_s, acc_s, s_buf):
        i = pl.program_id(1)
        row = lax.broadcasted_iota(jnp.int32, (tq, tq), 0)
        col = lax.broadcasted_iota(jnp.int32, (tq, tq), 1)
        lane = lax.broadcasted_iota(jnp.int32, (tq, LANES), 1)
        m_s[...] = jnp.full((2, tq, LANES), NEG, F32)
        acc_s[...] = jnp.zeros((2, tq, LANES), F32)

        def block(j):
            return pl.ds(pl.multiple_of(j * tq, tq), tq)

        def scores(hh, j):
            return _dot(q_ref[hh], k_ref[hh, block(j), :], NT)

        def consume(hh, j, s):
            m_prev = m_s[hh]
            m_new = jnp.maximum(m_prev, jnp.max(s, axis=1, keepdims=True))
            p = jnp.exp(s - m_new[:, :1])
            acc_s[hh] = jnp.exp(m_prev - m_new) * acc_s[hh] + _dot(p.astype(BF16), v_ref[hh, block(j), :])
            m_s[hh] = m_new

        for hh in range(2):
            s_buf[0, hh] = scores(hh, 0)

        def full_step(j, carry):
            slot = j & 1
            for hh in range(2):
                s = s_buf[slot, hh]
                s_buf[1 - slot, hh] = scores(hh, j + 1)
                consume(hh, j, s)
            return carry

        lax.fori_loop(0, i, full_step, 0)
        total = jnp.zeros((tq, LANES), F32)
        for hh in range(2):
            consume(hh, i, jnp.where(col <= row, s_buf[i & 1, hh], NEG))
            acc = acc_s[hh]
            l = acc[:, ONES_LANE[hh]:ONES_LANE[hh] + 1]
            mine = (lane >= 64) if hh else (lane < 64)
            total = total + jnp.where(mine, acc / l, 0.0)
            lse_ref[hh] = m_s[hh] + jnp.log(l)
        o_ref[...] = total

    kv_spec = pl.BlockSpec((2, seq, LANES), lambda p, i: (p, 0, 0))
    return pl.pallas_call(
        body, name="mla_fwd", grid=(MLA_HEADS // 2, nq),
        in_specs=[pl.BlockSpec((2, tq, LANES), lambda p, i: (p, i, 0)), kv_spec, kv_spec],
        out_specs=[pl.BlockSpec((tq, LANES), lambda p, i: (i, p)), pl.BlockSpec((2, tq, LANES), lambda p, i: (p, i, 0))],
        out_shape=[jax.ShapeDtypeStruct((seq, 4 * LANES), F32), jax.ShapeDtypeStruct((MLA_HEADS, seq, LANES), F32)],
        scratch_shapes=[pltpu.VMEM((2, tq, LANES), F32), pltpu.VMEM((2, tq, LANES), F32),
                        pltpu.VMEM((2, 2, tq, tq), F32)],
        compiler_params=_cp(("arbitrary", "arbitrary"), VMEM_LIMIT),
    )(q, k, v)


DIL_Q_FWD = 2 * BLOCK
DIL_Q_BWD = BLOCK


def _dil_tile_index(t, d, seq, nq):
    per_class = seq // (nq * d)
    shift = per_class.bit_length() - 1
    r = t >> shift
    n = t & (per_class - 1)
    start = r + (nq * d) * n
    prev = jnp.maximum(start - BLOCK * d, r)
    if d == 1:
        start = pl.multiple_of(start, nq)
        prev = pl.multiple_of(prev, BLOCK)
    return (n == 0).astype(jnp.int32), start, prev


def _dil_rows(start, d, size):
    return pl.ds(start, size) if d == 1 else pl.ds(start, size, stride=d)


def _dil_bias(nq):
    i = lax.broadcasted_iota(jnp.int32, (2 * nq, BLOCK + nq), 0) % nq
    j = lax.broadcasted_iota(jnp.int32, (2 * nq, BLOCK + nq), 1)
    band = (j >= i) & (j <= i + BLOCK)
    return jnp.where(band, 0.0, NEG), jnp.where(band & (j >= BLOCK), 0.0, NEG)


def _stack_heads(t, lane):
    return jnp.concatenate([jnp.where(lane < 64, t, 0.0), jnp.where(lane >= 64, t, 0.0)], axis=0)


def _unstack_heads(t, lane):
    nq = t.shape[0] // 2
    return jnp.where(lane < 64, t[:nq], t[nq:])


def _dil_fwd(qr, kr, vb):
    seq = qr.shape[0]
    nq = DIL_Q_FWD
    n_tiles = seq // nq
    assert seq % (nq * max(DIL_DILATIONS)) == 0

    def body(q_ref, k_ref, v_ref, o_ref, lse_ref, m_s, l_s, n_s, bias_s):
        lane = lax.broadcasted_iota(jnp.int32, (nq, LANES), 1)
        bias_s[0], bias_s[1] = _dil_bias(nq)
        for bi, d in enumerate(DIL_DILATIONS):

            def tile(t, carry, d=d, bi=bi):
                first, start, prev = _dil_tile_index(t, d, seq, nq)
                rows, prows = _dil_rows(start, d, nq), _dil_rows(prev, d, BLOCK)
                qst = _stack_heads(q_ref[rows, :], lane).astype(BF16)
                if seq == nq * d:
                    kcat, vcat = k_ref[rows, :].astype(BF16), v_ref[rows, :].astype(BF16)
                    s = _dot(qst, kcat, NT) + bias_s[1, :, BLOCK:]
                else:
                    kcat = jnp.concatenate([k_ref[prows, :], k_ref[rows, :]], axis=0).astype(BF16)
                    vcat = jnp.concatenate([v_ref[prows, :], v_ref[rows, :]], axis=0).astype(BF16)
                    s = _dot(qst, kcat, NT) + bias_s[first]
                m = jnp.max(s, axis=1, keepdims=True)
                p = jnp.exp(s - m)
                l2 = _unstack_heads(jnp.sum(p, axis=1, keepdims=True) + jnp.zeros((2 * nq, LANES), F32), lane)
                m2 = _unstack_heads(m + jnp.zeros((2 * nq, LANES), F32), lane)
                num2 = _unstack_heads(_dot(p.astype(BF16), vcat), lane)
                if bi == 0:
                    m_s[rows, :] = m2
                    l_s[rows, :] = l2
                    n_s[rows, :] = num2
                else:
                    m_old = m_s[rows, :]
                    m_new = jnp.maximum(m_old, m2)
                    a = jnp.exp(m_old - m_new)
                    b = jnp.exp(m2 - m_new)
                    m_s[rows, :] = m_new
                    l_s[rows, :] = a * l_s[rows, :] + b * l2
                    n_s[rows, :] = a * n_s[rows, :] + b * num2
                return carry

            lax.fori_loop(0, n_tiles, tile, 0, unroll=2)
        o_ref[...] = n_s[...] / l_s[...]
        lse_ref[...] = m_s[...] + jnp.log(l_s[...])

    col = lambda off: pl.BlockSpec((seq, LANES), lambda p: (0, p + off))
    return pl.pallas_call(
        body, name="dil_fwd", grid=(4,),
        in_specs=[col(0), col(0), col(0)],
        out_specs=[col(0), pl.BlockSpec((None, seq, LANES), lambda p: (p, 0, 0))],
        out_shape=[jax.ShapeDtypeStruct((seq, 4 * LANES), F32), jax.ShapeDtypeStruct((4, seq, LANES), F32)],
        scratch_shapes=[pltpu.VMEM((seq, LANES), F32)] * 3 + [pltpu.VMEM((2, 2 * nq, BLOCK + nq), F32)],
        compiler_params=_cp(("arbitrary",), VMEM_LIMIT),
    )(qr, kr, vb)


def _post(x, o_a, o_b, gates, w_out, ln_g, ln_b, target):
    seq = x.shape[0]
    tr = 512

    def body(x_ref, oa_ref, ob_ref, g_ref, w_ref, lg_ref, lb_ref, t_ref,
             dz_ref, do_ref, dg_ref, dw_ref, dlg_ref, dlb_ref, loss_ref):
        @pl.when(pl.program_id(0) == 0)
        def _():
            dw_ref[...] = jnp.zeros_like(dw_ref)
            dlg_ref[...] = jnp.zeros_like(dlg_ref)
            dlb_ref[...] = jnp.zeros_like(dlb_ref)
            loss_ref[...] = jnp.zeros_like(loss_ref)

        g = g_ref[...]
        sg = jax.nn.sigmoid(g)
        silu = g * sg
        o = jnp.concatenate([oa_ref[...], ob_ref[...]], axis=1)
        mixb = (o * silu).astype(BF16)
        w = w_ref[...]
        z = ALPHA * x_ref[...] + _dot(mixb, w)
        mu = jnp.mean(z, axis=-1, keepdims=True)
        zc = z - mu
        rstd = lax.rsqrt(jnp.mean(zc * zc, axis=-1, keepdims=True) + LN_EPS)
        xhat = zc * rstd
        lg = lg_ref[...]
        err = xhat * lg + lb_ref[...] - t_ref[...]
        loss_ref[...] += jnp.sum(err * err) * (0.5 / D_MODEL)
        dy = err * (1.0 / D_MODEL)
        dlg_ref[...] += jnp.sum(dy * xhat, axis=0, keepdims=True)
        dlb_ref[...] += jnp.sum(dy, axis=0, keepdims=True)
        dxh = dy * lg
        dz = rstd * (dxh - jnp.mean(dxh, axis=-1, keepdims=True) - xhat * jnp.mean(dxh * xhat, axis=-1, keepdims=True))
        dz_ref[...] = dz
        dzb = dz.astype(BF16)
        dmix = _dot(dzb, w, NT)
        do_ref[...] = dmix * silu
        dg_ref[...] = (dmix * o * (sg * (1.0 + g * (1.0 - sg)))).astype(BF16)
        dw_ref[...] += _dot(mixb, dzb, TN)

    row = lambda w: pl.BlockSpec((tr, w), lambda i: (i, 0))
    full = lambda s: pl.BlockSpec(s, lambda i: (0, 0))
    return pl.pallas_call(
        body, name="post", grid=(seq // tr,),
        in_specs=[row(D_MODEL), row(512), row(512), row(D_MODEL), full((D_MODEL, D_MODEL)), full((1, D_MODEL)),
                  full((1, D_MODEL)), row(D_MODEL)],
        out_specs=[row(D_MODEL), row(D_MODEL), row(D_MODEL), full((D_MODEL, D_MODEL)), full((1, D_MODEL)),
                   full((1, D_MODEL)), full((1, LANES))],
        out_shape=[jax.ShapeDtypeStruct((seq, D_MODEL), F32), jax.ShapeDtypeStruct((seq, D_MODEL), F32),
                   jax.ShapeDtypeStruct((seq, D_MODEL), BF16), jax.ShapeDtypeStruct((D_MODEL, D_MODEL), F32),
                   jax.ShapeDtypeStruct((1, D_MODEL), F32), jax.ShapeDtypeStruct((1, D_MODEL), F32),
                   jax.ShapeDtypeStruct((1, LANES), F32)],
        compiler_params=_cp(("arbitrary",), VMEM_LIMIT),
    )(x, o_a, o_b, gates, w_out, ln_g, ln_b, target)


def _mla_bwd(q, k, v, d_o, o, lse):
    seq = q.shape[1]
    tq = 512
    nq = seq // tq

    def body(q_ref, k_ref, v_ref, do_ref, o_ref, lse_ref, dq_ref, dk_ref, dv_ref, d_s, lse_s, dk_s, dv_s, v_s, kt_s, dqt_s):
        j = pl.program_id(1)
        lane = lax.broadcasted_iota(jnp.int32, (tq, LANES), 1)
        row = lax.broadcasted_iota(jnp.int32, (tq, tq), 0)
        col = lax.broadcasted_iota(jnp.int32, (tq, tq), 1)

        @pl.when(j == 0)
        def _():
            dqt_s[...] = jnp.zeros_like(dqt_s)

            def rowsum(i, carry):
                rows = pl.ds(pl.multiple_of(i * tq, tq), tq)
                prod = do_ref[rows, :] * o_ref[rows, :]
                for hh in range(2):
                    mine = (lane >= 64) if hh else (lane < 64)
                    total = jnp.sum(jnp.where(mine, prod, 0.0), axis=1, keepdims=True)
                    d_s[hh, i] = jnp.transpose(total + jnp.zeros((tq, LANES), F32))[:8]
                    lse_s[hh, i] = jnp.transpose(lse_ref[hh, rows, :])[:8]
                return carry

            lax.fori_loop(0, nq, rowsum, 0)

        dk_s[...] = jnp.zeros_like(dk_s)
        dv_s[...] = jnp.zeros_like(dv_s)
        for hh in range(2):
            v_s[hh] = jnp.where(lane == ONES_LANE[hh], 0.0, v_ref[hh].astype(F32)).astype(BF16)
            kt_s[hh] = jnp.transpose(k_ref[hh].astype(F32)).astype(BF16)

        def step(i, masked):
            rows = pl.ds(pl.multiple_of(i * tq, tq), tq)
            dob = do_ref[rows, :].astype(BF16)
            for hh in range(2):
                qb, kb, vb = q_ref[hh, rows, :], k_ref[hh], v_s[hh]
                p = jnp.exp(_dot(kb, qb, NT) - lse_s[hh, i][:1])
                if masked:
                    p = jnp.where(row <= col, p, 0.0)
                dv_s[hh] += _dot(p.astype(BF16), dob)
                ds = (p * (_dot(vb, dob, NT) - d_s[hh, i][:1])).astype(BF16)
                dk_s[hh] += _dot(ds, qb)
                dqt_s[hh, i] += _dot(kt_s[hh], ds)

        def full_step(i, carry):
            step(i, False)
            return carry

        step(j, True)
        lax.fori_loop(j + 1, nq, full_step, 0)
        dk_ref[...] = dk_s[...]
        dv_ref[...] = dv_s[...]

        @pl.when(j == nq - 1)
        def _():
            def untranspose(i, carry):
                rows = pl.ds(pl.multiple_of(i * tq, tq), tq)
                for hh in range(2):
                    dq_ref[hh, rows, :] = jnp.transpose(dqt_s[hh, i])
                return carry

            lax.fori_loop(0, nq, untranspose, 0)

    whole = pl.BlockSpec((2, seq, LANES), lambda p, j: (p, 0, 0))
    blk = pl.BlockSpec((2, tq, LANES), lambda p, j: (p, j, 0))
    pair = pl.BlockSpec((seq, LANES), lambda p, j: (0, p))
    shape = jax.ShapeDtypeStruct((MLA_HEADS, seq, LANES), F32)
    return pl.pallas_call(
        body, name="mla_bwd", grid=(MLA_HEADS // 2, nq),
        in_specs=[whole, blk, blk, pair, pair, whole],
        out_specs=[whole, blk, blk], out_shape=[shape] * 3,
        scratch_shapes=[pltpu.VMEM((2, nq, 8, tq), F32), pltpu.VMEM((2, nq, 8, tq), F32),
                        pltpu.VMEM((2, tq, LANES), F32), pltpu.VMEM((2, tq, LANES), F32),
                        pltpu.VMEM((2, tq, LANES), BF16), pltpu.VMEM((2, LANES, tq), BF16),
                        pltpu.VMEM((2, nq, LANES, tq), F32)],
        compiler_params=_cp(("arbitrary", "arbitrary"), VMEM_LIMIT),
    )(q, k, v, d_o, o, lse)


def _dil_bwd(qr, kr, vb, d_o, o, lse):
    seq = qr.shape[0]
    nq = DIL_Q_BWD
    n_tiles = seq // nq
    chunk = 512

    def body(q_ref, k_ref, v_ref, do_ref, o_ref, lse_ref, dq_ref, dk_ref, dv_ref, d_s, dq_s, dk_s, dv_s, bias_s):
        lane = lax.broadcasted_iota(jnp.int32, (nq, LANES), 1)
        lanec = lax.broadcasted_iota(jnp.int32, (chunk, LANES), 1)
        bias_s[0], bias_s[1] = [b[:nq] for b in _dil_bias(nq)]

        def rowsum(i, carry):
            rows = pl.ds(pl.multiple_of(i * chunk, chunk), chunk)
            prod = do_ref[rows, :] * o_ref[rows, :]
            lo = jnp.sum(jnp.where(lanec < 64, prod, 0.0), axis=1, keepdims=True)
            hi = jnp.sum(jnp.where(lanec >= 64, prod, 0.0), axis=1, keepdims=True)
            d_s[rows, :] = jnp.where(lanec < 64, lo, hi)
            return carry

        lax.fori_loop(0, seq // chunk, rowsum, 0)
        dq_s[...] = jnp.zeros_like(dq_s)
        dk_s[...] = jnp.zeros_like(dk_s)
        dv_s[...] = jnp.zeros_like(dv_s)
        for d in DIL_DILATIONS:

            def tile(t, carry, d=d):
                first, start, prev = _dil_tile_index(t, d, seq, nq)
                rows, prows = _dil_rows(start, d, nq), _dil_rows(prev, d, BLOCK)
                q_t, do_t = q_ref[rows, :], do_ref[rows, :]
                lse_t, d_t = lse_ref[rows, :], d_s[rows, :]
                kcat = jnp.concatenate([k_ref[prows, :], k_ref[rows, :]], axis=0).astype(BF16)
                vcat = jnp.concatenate([v_ref[prows, :], v_ref[rows, :]], axis=0).astype(BF16)
                bias = bias_s[first]
                dq_t = jnp.zeros((nq, LANES), F32)
                dkcat = jnp.zeros((BLOCK + nq, LANES), F32)
                dvcat = jnp.zeros((BLOCK + nq, LANES), F32)
                for hh in range(2):
                    mine = (lane >= 64) if hh else (lane < 64)
                    c0 = 64 * hh
                    qh = jnp.where(mine, q_t, 0.0).astype(BF16)
                    doh = jnp.where(mine, do_t, 0.0).astype(BF16)
                    p = jnp.exp(_dot(qh, kcat, NT) + bias - lse_t[:, c0:c0 + 1])
                    dvcat = dvcat + _dot(p.astype(BF16), doh, TN)
                    dp = _dot(doh, vcat, NT)
                    ds = (p * (dp - d_t[:, c0:c0 + 1])).astype(BF16)
                    dq_t = dq_t + jnp.where(mine, _dot(ds, kcat), 0.0)
                    dkcat = dkcat + _dot(ds, qh, TN)
                dq_s[rows, :] += dq_t
                dk_s[prows, :] += dkcat[:BLOCK]
                dk_s[rows, :] += dkcat[BLOCK:]
                dv_s[prows, :] += dvcat[:BLOCK]
                dv_s[rows, :] += dvcat[BLOCK:]
                return carry

            lax.fori_loop(0, n_tiles, tile, 0, unroll=4)
        dq_ref[...] = dq_s[...].astype(BF16)
        dk_ref[...] = dk_s[...].astype(BF16)
        dv_ref[...] = dv_s[...].astype(BF16)

    col = lambda off: pl.BlockSpec((seq, LANES), lambda p: (0, p + off))
    shape = jax.ShapeDtypeStruct((seq, 4 * LANES), BF16)
    return pl.pallas_call(
        body, name="dil_bwd", grid=(4,),
        in_specs=[col(0), col(0), col(0), col(4), col(0), pl.BlockSpec((None, seq, LANES), lambda p: (p, 0, 0))],
        out_specs=[col(0)] * 3, out_shape=[shape] * 3,
        scratch_shapes=[pltpu.VMEM((seq, LANES), F32)] * 4 + [pltpu.VMEM((2, nq, BLOCK + nq), F32)],
        compiler_params=_cp(("arbitrary",), VMEM_LIMIT),
    )(qr, kr, vb, d_o, o, lse)


def _in_bwd(dz, cq, ckv, gq, gkv, wuq_e, wukv, ct, st, dq, dk, dv, dgates, dqr, dkr, dvb, cd, sd, w_in_p):
    seq = dz.shape[0]
    tr = 512

    def body(dz_ref, cq_ref, ckv_ref, gq_ref, gkv_ref, wuq_ref, wukv_ref, ct_ref, st_ref, dq_ref, dk_ref, dv_ref,
             dg_ref, dqr_ref, dkr_ref, dvb_ref, cd_ref, sd_ref, w_ref,
             gx_ref, dh_ref, dwuq_ref, dwukv_ref, dgq_ref, dgkv_ref):
        @pl.when(pl.program_id(0) == 0)
        def _():
            dwuq_ref[...] = jnp.zeros_like(dwuq_ref)
            dwukv_ref[...] = jnp.zeros_like(dwukv_ref)
            dgq_ref[...] = jnp.zeros_like(dgq_ref)
            dgkv_ref[...] = jnp.zeros_like(dgkv_ref)

        lane = lax.broadcasted_iota(jnp.int32, (tr, LANES), 1)
        rope_lanes = jnp.logical_and(lane >= 64, lane < 96)
        ct_, st_ = ct_ref[...], st_ref[...]

        def mla_rope_t(g):
            return ct_ * g + jnp.where(rope_lanes, _mla_rot(st_ * g, lane), 0.0)

        def norm_bwd(c, g, dn, dg_ref):
            r, _ = _rms(c, g)
            u = dn * g
            dg_ref[...] += jnp.sum(dn * c * r, axis=0, keepdims=True)
            return r * u - c * (r * r * r) * jnp.mean(u * c, axis=-1, keepdims=True)

        c, g = cq_ref[...], gq_ref[...]
        _, qn = _rms(c, g)
        dq_all = jnp.concatenate([mla_rope_t(dq_ref[h] * MLA_SCALE) for h in range(MLA_HEADS)], axis=1).astype(BF16)
        dwuq_ref[...] += _dot(qn.astype(BF16), dq_all, TN)
        dcq = norm_bwd(c, g, _dot(dq_all, wuq_ref[...], NT), dgq_ref).astype(BF16)

        c, g = ckv_ref[...], gkv_ref[...]
        _, kvn = _rms(c, g)
        dkpe = jnp.zeros((tr, LANES), F32)
        parts = []
        for h in range(MLA_HEADS):
            dk_h, dv_h = dk_ref[h], dv_ref[h]
            if h % 2 == 0:
                dv_h = pltpu.roll(dv_h, 64, 1)
            parts.append(jnp.where(lane < 64, dk_h, dv_h))
            dkpe = dkpe + jnp.where(rope_lanes, dk_h, 0.0)
        dkv_all = jnp.concatenate(parts, axis=1).astype(BF16)
        dwukv_ref[...] += _dot(kvn.astype(BF16), dkv_all, TN)
        dckv = norm_bwd(c, g, _dot(dkv_all, wukv_ref[...], NT), dgkv_ref).astype(BF16)
        dkrope = mla_rope_t(dkpe).astype(BF16)

        rot_lanes = lane % 64 < DIL_ROT
        cd_, sd_ = cd_ref[...], sd_ref[...]

        def dil_rope_t(g):
            return cd_ * g + jnp.where(rot_lanes, _dil_rot(sd_ * g, lane), 0.0)

        dqb = [dil_rope_t(dqr_ref[:, LANES * p:LANES * (p + 1)].astype(F32) * DIL_SCALE).astype(BF16) for p in range(4)]
        dkb = [dil_rope_t(dkr_ref[:, LANES * p:LANES * (p + 1)].astype(F32)).astype(BF16) for p in range(4)]
        dh = jnp.concatenate([dcq, dckv, dg_ref[...]] + dqb + dkb + [dvb_ref[...], dkrope], axis=1)
        dh_ref[...] = dh
        gx_ref[...] = ALPHA * dz_ref[...] + _dot(dh, w_ref[...], NT)

    row = lambda w: pl.BlockSpec((tr, w), lambda i: (i, 0))
    full = lambda a: pl.BlockSpec(a.shape, lambda i: (0,) * a.ndim)
    head = pl.BlockSpec((MLA_HEADS, tr, LANES), lambda i: (0, i, 0))
    return pl.pallas_call(
        body, name="in_bwd", grid=(seq // tr,),
        in_specs=[row(D_MODEL), row(Q_LORA), row(KV_LORA), full(gq), full(gkv), full(wuq_e), full(wukv), row(LANES),
                  row(LANES), head, head, head, row(D_MODEL), row(512), row(512), row(512), row(LANES), row(LANES),
                  full(w_in_p)],
        out_specs=[row(D_MODEL), row(IN_WIDTH_PAD), full(wuq_e), full(wukv), full(gq), full(gkv)],
        out_shape=[jax.ShapeDtypeStruct((seq, D_MODEL), F32), jax.ShapeDtypeStruct((seq, IN_WIDTH_PAD), BF16),
                   jax.ShapeDtypeStruct(wuq_e.shape, F32), jax.ShapeDtypeStruct(wukv.shape, F32),
                   jax.ShapeDtypeStruct(gq.shape, F32), jax.ShapeDtypeStruct(gkv.shape, F32)],
        compiler_params=_cp(("arbitrary",), VMEM_LIMIT),
    )(dz, cq, ckv, gq, gkv, wuq_e, wukv, ct, st, dq, dk, dv, dgates, dqr, dkr, dvb, cd, sd, w_in_p)


def _dw_in(x, dh):
    seq = dh.shape[0]
    tk = 512
    tn = IN_WIDTH_PAD // 2

    def body(x_ref, dh_ref, o_ref):
        @pl.when(pl.program_id(1) == 0)
        def _():
            o_ref[...] = jnp.zeros_like(o_ref)

        o_ref[...] += _dot(dh_ref[...], x_ref[...].astype(BF16), TN)

    return pl.pallas_call(
        body, name="dw_in", grid=(2, seq // tk),
        in_specs=[pl.BlockSpec((tk, D_MODEL), lambda n, k: (k, 0)), pl.BlockSpec((tk, tn), lambda n, k: (k, n))],
        out_specs=pl.BlockSpec((tn, D_MODEL), lambda n, k: (n, 0)),
        out_shape=jax.ShapeDtypeStruct((IN_WIDTH_PAD, D_MODEL), F32),
        compiler_params=_cp(("arbitrary", "arbitrary"), VMEM_LIMIT),
    )(x, dh)


def _adam_update(w, g, m, v):
    nm = ADAM_B1 * m + (1.0 - ADAM_B1) * g
    nv = ADAM_B2 * v + (1.0 - ADAM_B2) * jnp.square(g)
    m_hat = nm / (1.0 - ADAM_B1 ** ADAM_STEP)
    v_hat = nv / (1.0 - ADAM_B2 ** ADAM_STEP)
    return -ADAM_LR * (m_hat / (jnp.sqrt(v_hat) + ADAM_EPS) + ADAM_WD * w), nm, nv


def _adamw(w, g, m, v, name):
    rows, cols = w.shape
    tc = 256 if cols % 256 == 0 and rows * cols > 2 ** 18 else cols

    def body(w_ref, g_ref, m_ref, v_ref, d_ref, nm_ref, nv_ref):
        d_ref[...], nm_ref[...], nv_ref[...] = _adam_update(w_ref[...], g_ref[...], m_ref[...], v_ref[...])

    spec = pl.BlockSpec((rows, tc), lambda i: (0, i))
    return pl.pallas_call(
        body, name=name, grid=(cols // tc,), in_specs=[spec] * 4, out_specs=[spec] * 3,
        out_shape=[jax.ShapeDtypeStruct(w.shape, F32)] * 3, compiler_params=_cp(("arbitrary",)),
    )(w, g, m, v)


def _adamw_vectors(small_sum, ws, ms, vs):
    k = len(ws)
    sizes = [w.shape[-1] for w in ws]

    def body(s_ref, *refs):
        ins, outs = refs[:3 * k], refs[3 * k:]
        for i, size in enumerate(sizes):
            g = s_ref[i:i + 1, 0:size]
            outs[i][...] = g
            outs[k + i][...], outs[2 * k + i][...], outs[3 * k + i][...] = _adam_update(
                ins[i][...], g, ins[k + i][...], ins[2 * k + i][...])

    out = pl.pallas_call(
        body, name="adamw_vectors", out_shape=[jax.ShapeDtypeStruct((1, size), F32) for size in sizes] * 4,
    )(small_sum, *[a.reshape(1, -1) for a in list(ws) + list(ms) + list(vs)])
    return [[a.reshape(-1) for a in out[k * j:k * (j + 1)]] for j in range(4)]


def _local_step(x2, target, w_in_p, w_uq_f, wukv_f, w_out_f, q_norm_g, kv_norm_g, ln_g, ln_b):
    seq = x2.shape[0]
    wuq_e = jnp.pad(w_uq_f.reshape(Q_LORA, MLA_HEADS, 96), ((0, 0), (0, 0), (0, 32))).reshape(Q_LORA, MLA_HEADS * LANES)
    ct, st, cd, sd = _rope_tables(seq)
    gq = q_norm_g.reshape(1, Q_LORA)
    gkv = kv_norm_g.reshape(1, KV_LORA)

    cq, ckv, gates, qr, krot, vb, q_e, k_e, v_e = _proj(x2, w_in_p, gq, gkv, wuq_e, wukv_f, ct, st, cd, sd)
    o_a, lse_a = _mla_fwd(q_e, k_e, v_e)
    o_b, lse_b = _dil_fwd(qr, krot, vb)

    dz, d_o, d_gates, dw_out, dln_g, dln_b, loss_part = _post(
        x2, o_a, o_b, gates, w_out_f, ln_g.reshape(1, D_MODEL), ln_b.reshape(1, D_MODEL), target)
    dq_e, dk_e, dv_e = _mla_bwd(q_e, k_e, v_e, d_o, o_a, lse_a)
    dqr, dkr, dvb = _dil_bwd(qr, krot, vb, d_o, o_b, lse_b)
    grad_x, dh, dwuq_e, dwukv, dgq, dgkv = _in_bwd(
        dz, cq, ckv, gq, gkv, wuq_e, wukv_f, ct, st, dq_e, dk_e, dv_e, d_gates, dqr, dkr, dvb, cd, sd, w_in_p)
    dw_in = _dw_in(x2, dh)
    dw_uq = dwuq_e.reshape(Q_LORA, MLA_HEADS, LANES)[:, :, :96].reshape(Q_LORA, MLA_HEADS * 96)
    return loss_part, grad_x, dw_in, dw_uq, dwukv, dw_out, dgq, dgkv, dln_g, dln_b


def kernel(x, w_in, q_norm_g, kv_norm_g, w_uq, w_ukv, w_out, ln_g, ln_b, loss_target, m_w_in, m_q_norm_g, m_kv_norm_g, m_w_uq, m_w_ukv, m_w_out, m_ln_g, m_ln_b, v_w_in, v_q_norm_g, v_kv_norm_g, v_w_uq, v_w_ukv, v_w_out, v_ln_g, v_ln_b):
    seq = x.shape[1]
    x2 = x.reshape(seq, D_MODEL)
    target = loss_target.reshape(seq, D_MODEL)

    g_w_in, g_w_uq, g_w_ukv, g_w_out = _all_gather_weights([w_in, w_uq, w_ukv, w_out])
    by_cols = lambda g: jnp.concatenate([g[j] for j in range(N_SHARD)], axis=1)
    loss_part, grad_x, dw_in, dw_uq, dwukv, dw_out, dgq, dgkv, dln_g, dln_b = _local_step(
        x2, target, _permute_w_in_shards(g_w_in), by_cols(g_w_uq), by_cols(g_w_ukv), g_w_out.reshape(D_MODEL, D_MODEL),
        q_norm_g, kv_norm_g, ln_g, ln_b)

    to_shards = lambda d: d.reshape(d.shape[0], N_SHARD, d.shape[1] // N_SHARD).transpose(1, 0, 2)
    grads = [dw_in, to_shards(dw_uq), to_shards(dwukv), dw_out.reshape(N_SHARD, 256, D_MODEL)]
    *chip_sums, small_sum = _reduce_over_sibling(grads, [dgq, dgkv, dln_g, dln_b, loss_part])
    g_in_t, g_uq, g_ukv, g_out = _reduce_over_chips(chip_sums)
    g_in = g_in_t.T
    loss = small_sum[4, 0]

    big = [[o.T for o in _adamw(w.T, g.T, m.T, v.T, name)] for w, g, m, v, name in (
        (w_in, g_in, m_w_in, v_w_in, "adamw_w_in"), (w_uq, g_uq, m_w_uq, v_w_uq, "adamw_w_uq"))]
    big += [_adamw(w, g, m, v, name) for w, g, m, v, name in (
        (w_ukv, g_ukv, m_w_ukv, v_w_ukv, "adamw_w_ukv"), (w_out, g_out, m_w_out, v_w_out, "adamw_w_out"))]
    vec_g, vec_delta, vec_m, vec_v = _adamw_vectors(
        small_sum, [q_norm_g, kv_norm_g, ln_g, ln_b], [m_q_norm_g, m_kv_norm_g, m_ln_g, m_ln_b],
        [v_q_norm_g, v_kv_norm_g, v_ln_g, v_ln_b])

    def ordered(bigs, vecs):
        return [bigs[0], vecs[0], vecs[1], bigs[1], bigs[2], bigs[3], vecs[2], vecs[3]]

    grads_out = ordered([g_in, g_uq, g_ukv, g_out], vec_g)
    deltas = ordered([b[0] for b in big], vec_delta)
    new_m = ordered([b[1] for b in big], vec_m)
    new_v = ordered([b[2] for b in big], vec_v)
    return (loss, grad_x.reshape(x.shape), *grads_out, *deltas, *new_m, *new_v)
```

```python
import functools

import jax
import jax.numpy as jnp
import numpy as np
from jax import lax
from jax.experimental import pallas as pl
from jax.experimental.pallas import tpu as pltpu

F32 = jnp.float32
BF16 = jnp.bfloat16

D_MODEL = 1024
ROPE_THETA = 500000.0
BLOCK = 128
NEG = -1e30
RMS_EPS = 1e-6
LN_EPS = 1e-5

MLA_HEADS = 8
MLA_NOPE = 64
MLA_ROPE = 32
Q_LORA = 384
KV_LORA = 256
DIL_HEADS = 8
DIL_HEAD_DIM = 64
DIL_ROT = 16
DIL_DILATIONS = (1, 4, 16)
IN_WIDTH = 3232
IN_WIDTH_PAD = 3328
ONES_LANE = (64, 0)
MLA_SCALE = (MLA_NOPE + MLA_ROPE) ** -0.5
DIL_SCALE = DIL_HEAD_DIM ** -0.5
ALPHA = 2.0 ** 0.25

ADAM_LR = 0.001
ADAM_B1 = 0.9
ADAM_B2 = 0.999
ADAM_EPS = 1e-08
ADAM_WD = 0.01
ADAM_STEP = 10

N_SHARD = 4
GRAD_SHAPES = ((808, 1024), (384, 192), (256, 256), (256, 1024))
GRAD_SPLIT_COLS = (True, False, False, False)
ROW_CHUNK = 64
LANES = 128
VMEM_LIMIT = 56 * 1024 * 1024
MESH = pl.DeviceIdType.MESH

NT = (((1,), (1,)), ((), ()))
TN = (((0,), (0,)), ((), ()))


def _cp(sem=None, vmem=None):
    return pltpu.CompilerParams(dimension_semantics=sem, vmem_limit_bytes=vmem)


def _dot(a, b, dims=None):
    if dims is None:
        return jnp.dot(a, b, preferred_element_type=F32)
    return lax.dot_general(a, b, dims, preferred_element_type=F32)


def _rope_tables(seq):
    f32 = np.float32
    pos = np.arange(seq, dtype=f32)[:, None]
    one, zero = np.ones((seq, 64), f32), np.zeros((seq, 64), f32)

    def cos_sin(dim):
        inv = np.power(f32(ROPE_THETA), -np.arange(0, dim, 2, dtype=f32) / f32(dim)).astype(f32)
        ang = (pos * inv[None, :]).astype(f32)
        return np.cos(ang).astype(f32), np.sin(ang).astype(f32)

    cos, sin = cos_sin(MLA_ROPE)
    ct = np.concatenate([one, cos, cos, zero[:, :32]], axis=1)
    st = np.concatenate([zero, -sin, sin, zero[:, :32]], axis=1)
    cos, sin = cos_sin(DIL_ROT)
    cd = np.concatenate([cos, cos, one[:, :48]], axis=1)
    sd = np.concatenate([-sin, sin, zero[:, :48]], axis=1)
    return tuple(jnp.asarray(t) for t in (ct, st, np.tile(cd, (1, 2)), np.tile(sd, (1, 2))))


W_IN_ORDER = ((0, 640), (672, 1184), (2720, 3232), (1184, 2720), None, (640, 672))


def _permute_w_in_t(w_t):
    z = jnp.zeros((64, w_t.shape[1]), w_t.dtype)
    parts = [z if r is None else w_t[r[0]:r[1]] for r in W_IN_ORDER]
    return jnp.concatenate(parts + [z[:32]], axis=0)


def _permute_w_in_t_shards(g):
    z = jnp.zeros((64, g.shape[2]), g.dtype)
    parts = [g[k, src:src + rows] for k, src, _, rows in _w_in_row_pieces()]
    at = [dst for _, _, dst, _ in _w_in_row_pieces()]
    assert at == sorted(at) and at[-1] == 3264
    return jnp.concatenate(parts[:-1] + [z, parts[-1], z[:32]], axis=0)


def _w_in_row_pieces():
    width = GRAD_SHAPES[0][0]
    pieces, at = [], 0
    for r in W_IN_ORDER:
        if r is None:
            at += 64
            continue
        for k in range(N_SHARD):
            lo, hi = max(r[0], width * k), min(r[1], width * (k + 1))
            if lo < hi:
                pieces.append((k, lo - width * k, at + lo - r[0], hi - lo))
        at += r[1] - r[0]
    return pieces


def _position():
    return lax.axis_index("x"), lax.axis_index("y"), lax.axis_index("c")


def _all_gather_weights(shards):
    n = len(shards)

    def body(*refs):
        ins, outs = refs[:n], refs[n:2 * n]
        send_sems, recv_sems = refs[2 * n:]
        x, y, c = _position()
        me = 2 * x + y
        chips = [(1 - x, y), (x, 1 - y), (1 - x, 1 - y)]
        for a in range(n):
            rows, cols = GRAD_SHAPES[a]
            if GRAD_SPLIT_COLS[a]:
                blocks = [(slice(None), slice(c0, c0 + LANES)) for c0 in range(0, cols, LANES)]
            else:
                blocks = [(slice(r0, r0 + ROW_CHUNK), slice(None)) for r0 in range(0, rows, ROW_CHUNK)]
            for blk in blocks:
                outs[a][(me,) + blk] = ins[a][blk].astype(BF16)

        def copy(k, a, slot, part, to):
            ref = outs[a].at[(slot,) + part]
            return pltpu.make_async_remote_copy(
                src_ref=ref, dst_ref=ref, send_sem=send_sems.at[k * n + a], recv_sem=recv_sems.at[k * n + a],
                device_id=to, device_id_type=MESH)

        half = [_grad_half(a, c) for a in range(n)]
        other = [_grad_half(a, 1 - c) for a in range(n)]
        first = [copy(k, a, me, half[a], (px, py, c)) for k, (px, py) in enumerate(chips) for a in range(n)]
        for cp in first:
            cp.start()
        passed = []
        for k, (px, py) in enumerate(chips):
            for a in range(n):
                copy(k, a, 2 * px + py, half[a], (x, y, c)).wait_recv()
                cp = copy(3 + k, a, 2 * px + py, half[a], (x, y, 1 - c))
                cp.start()
                passed.append(cp)
        for k, (px, py) in enumerate(chips):
            for a in range(n):
                copy(3 + k, a, 2 * px + py, other[a], (x, y, c)).wait_recv()
        for cp in first + passed:
            cp.wait_send()

    vmem = pl.BlockSpec(memory_space=pltpu.VMEM)
    return pl.pallas_call(
        body, name="all_gather_weights",
        out_shape=[jax.ShapeDtypeStruct((N_SHARD,) + s, BF16) for s in GRAD_SHAPES],
        in_specs=[vmem] * n, out_specs=[vmem] * n,
        scratch_shapes=[pltpu.SemaphoreType.DMA((6 * n,)), pltpu.SemaphoreType.DMA((6 * n,))],
        compiler_params=_cp(None, VMEM_LIMIT),
    )(*shards)


def _grad_half_shape(a):
    rows, cols = GRAD_SHAPES[a]
    return (rows, cols // 2) if GRAD_SPLIT_COLS[a] else (rows // 2, cols)


def _grad_half(a, c):
    rows, cols = GRAD_SHAPES[a]
    if GRAD_SPLIT_COLS[a]:
        return slice(None), pl.ds(pl.multiple_of(c * (cols // 2), LANES), cols // 2)
    return pl.ds(pl.multiple_of(c * (rows // 2), ROW_CHUNK), rows // 2), slice(None)


def _grad_chunks(a, c):
    rows, cols = GRAD_SHAPES[a]
    if GRAD_SPLIT_COLS[a]:
        return [((slice(None), pl.ds(c0, LANES)),
                 (slice(None), pl.ds(pl.multiple_of(c * (cols // 2) + c0, LANES), LANES)))
                for c0 in range(0, cols // 2, LANES)]
    return [((pl.ds(r0, ROW_CHUNK), slice(None)),
             (pl.ds(pl.multiple_of(c * (rows // 2) + r0, ROW_CHUNK), ROW_CHUNK), slice(None)))
            for r0 in range(0, rows // 2, ROW_CHUNK)]


def _reduce_over_sibling(grads, small_rows):
    n = len(grads)
    n_small = len(small_rows)
    pieces = _w_in_row_pieces()

    def body(*refs):
        g_hbm, rows_in = refs[:n], refs[n:n + n_small]
        sums, small_sum = refs[n + n_small:2 * n + n_small], refs[2 * n + n_small]
        scratch = refs[2 * n + n_small + 1:]
        stage, got = scratch[:n], scratch[n:2 * n]
        sm, smalls, send_sems, recv_sems, local_sems = scratch[2 * n:]
        x, y, c = _position()
        me = 4 * x + 2 * y + c
        sm[...] = jnp.zeros_like(sm)
        for i, row in enumerate(rows_in):
            sm[i:i + 1, 0:row.shape[1]] = row[...]
        loads = [[pltpu.make_async_copy(g_hbm[0].at[pl.ds(src, rows)], stage[0].at[k, pl.ds(dst, rows)],
                                        local_sems.at[n + i])
                  for i, (k, dst, src, rows) in enumerate(pieces)]]
        loads += [[pltpu.make_async_copy(g_hbm[a], stage[a], local_sems.at[a])] for a in range(1, n)]
        for group in loads:
            for ld in group:
                ld.start()
        smalls[me] = sm[...]
        sends = []
        for rel in range(1, 8):
            px = 1 - x if rel // 4 else x
            py = 1 - y if (rel // 2) % 2 else y
            pc = 1 - c if rel % 2 else c
            cp = pltpu.make_async_remote_copy(
                src_ref=sm, dst_ref=smalls.at[me], send_sem=send_sems.at[n + rel], recv_sem=recv_sems.at[n + rel],
                device_id=(px, py, pc), device_id_type=MESH)
            cp.start()
            sends.append((cp, 4 * px + 2 * py + pc))
        swaps = []
        for a in range(n):
            for ld in loads[a]:
                ld.wait()
            cp = pltpu.make_async_remote_copy(
                src_ref=stage[a].at[(slice(None),) + _grad_half(a, 1 - c)], dst_ref=got[a], send_sem=send_sems.at[a], recv_sem=recv_sems.at[a],
                device_id=(x, y, 1 - c), device_id_type=MESH)
            cp.start()
            swaps.append(cp)
        for a in range(n):
            swaps[a].wait_recv()
            for k in range(N_SHARD):
                for in_half, in_whole in _grad_chunks(a, c):
                    pair = stage[a][(k,) + in_whole] + got[a][(k,) + in_half]
                    sums[a][(k,) + in_half] = pair.astype(BF16)
        for rel, (cp, peer) in enumerate(sends, start=1):
            pltpu.make_async_remote_copy(
                src_ref=sm, dst_ref=smalls.at[peer], send_sem=send_sems.at[n + rel], recv_sem=recv_sems.at[n + rel],
                device_id=(x, y, c), device_id_type=MESH).wait_recv()
        total = smalls[0]
        for dev in range(1, 8):
            total = total + smalls[dev]
        small_sum[...] = total
        for cp in swaps:
            cp.wait_send()
        for cp, _ in sends:
            cp.wait_send()

    vmem = pl.BlockSpec(memory_space=pltpu.VMEM)
    half = [(N_SHARD,) + _grad_half_shape(a) for a in range(n)]
    return pl.pallas_call(
        body, name="reduce_over_sibling",
        out_shape=[jax.ShapeDtypeStruct(s, BF16) for s in half] + [jax.ShapeDtypeStruct((8, D_MODEL), F32)],
        in_specs=[pl.BlockSpec(memory_space=pl.ANY)] * n + [vmem] * n_small, out_specs=[vmem] * (n + 1),
        scratch_shapes=[pltpu.VMEM((N_SHARD,) + s, F32) for s in GRAD_SHAPES] + [pltpu.VMEM(s, F32) for s in half]
        + [pltpu.VMEM((8, D_MODEL), F32), pltpu.VMEM((8, 8, D_MODEL), F32),
           pltpu.SemaphoreType.DMA((n + 8,)), pltpu.SemaphoreType.DMA((n + 8,)),
           pltpu.SemaphoreType.DMA((n + len(pieces),))],
        compiler_params=_cp(None, VMEM_LIMIT),
    )(*grads, *small_rows)


def _reduce_over_chips(sums):
    n = len(sums)

    def body(*refs):
        h, outs, got = refs[:n], refs[n:2 * n], refs[2 * n:3 * n]
        send_sems, recv_sems = refs[3 * n:]
        x, y, c = _position()
        me = 2 * x + y
        chips = [(1 - x, y), (x, 1 - y), (1 - x, 1 - y)]
        sends = []
        for k, (px, py) in enumerate(chips):
            for a in range(n):
                cp = pltpu.make_async_remote_copy(
                    src_ref=h[a].at[2 * px + py], dst_ref=got[a].at[k], send_sem=send_sems.at[k * n + a],
                    recv_sem=recv_sems.at[k * n + a], device_id=(px, py, c), device_id_type=MESH)
                cp.start()
                sends.append(cp)
        for cp in sends:
            cp.wait_recv()
        joins = []
        for a in range(n):
            for in_half, in_whole in _grad_chunks(a, c):
                total = h[a][(me,) + in_half].astype(F32)
                for k in range(3):
                    total = total + got[a][(k,) + in_half].astype(F32)
                outs[a][in_whole] = total
            half = outs[a].at[_grad_half(a, c)]
            cp = pltpu.make_async_remote_copy(
                src_ref=half, dst_ref=half, send_sem=send_sems.at[3 * n + a],
                recv_sem=recv_sems.at[3 * n + a], device_id=(x, y, 1 - c), device_id_type=MESH)
            cp.start()
            joins.append(cp)
        for a in range(n):
            other = outs[a].at[_grad_half(a, 1 - c)]
            pltpu.make_async_remote_copy(
                src_ref=other, dst_ref=other, send_sem=send_sems.at[3 * n + a],
                recv_sem=recv_sems.at[3 * n + a], device_id=(x, y, c), device_id_type=MESH).wait_recv()
        for cp in sends + joins:
            cp.wait_send()

    vmem = pl.BlockSpec(memory_space=pltpu.VMEM)
    return pl.pallas_call(
        body, name="reduce_over_chips",
        out_shape=[jax.ShapeDtypeStruct(s, F32) for s in GRAD_SHAPES],
        in_specs=[vmem] * n, out_specs=[vmem] * n,
        scratch_shapes=[pltpu.VMEM((3,) + _grad_half_shape(a), BF16) for a in range(n)]
        + [pltpu.SemaphoreType.DMA((4 * n,)), pltpu.SemaphoreType.DMA((4 * n,))],
        compiler_params=_cp(None, VMEM_LIMIT),
    )(*sums)


def _proj(x, w_in_p, gq, gkv, wuq_e, wukv, ct, st, cd, sd):
    seq = x.shape[0]
    tr = 512

    def body(x_ref, w_ref, gq_ref, gkv_ref, wuq_ref, wukv_ref, ct_ref, st_ref, cd_ref, sd_ref,
             cq_ref, ckv_ref, g_ref, qr_ref, kr_ref, vb_ref, q_out, k_out, v_out):
        lane = lax.broadcasted_iota(jnp.int32, (tr, LANES), 1)
        xb = x_ref[...].astype(BF16)
        cq = _dot(xb, w_ref[0:384, :], NT)
        ckv = _dot(xb, w_ref[384:640, :], NT)
        cq_ref[...] = cq
        ckv_ref[...] = ckv
        g_ref[...] = _dot(xb, w_ref[640:1664, :], NT)

        cd_, sd_ = cd_ref[...], sd_ref[...]
        qb = _dot(xb, w_ref[1664:2176, :], NT)
        kb = _dot(xb, w_ref[2176:2688, :], NT)
        for p in range(4):
            cols = slice(LANES * p, LANES * (p + 1))
            t = qb[:, cols]
            qr_ref[:, cols] = (t * cd_ + _dil_rot(t, lane) * sd_) * DIL_SCALE
            t = kb[:, cols]
            kr_ref[:, cols] = t * cd_ + _dil_rot(t, lane) * sd_
        vb_ref[...] = _dot(xb, w_ref[2688:3200, :], NT)

        ct_, st_ = ct_ref[...], st_ref[...]

        def rope(t):
            return t * ct_ + _mla_rot(t, lane) * st_

        _, qn = _rms(cq, gq_ref[...])
        q_all = _dot(qn.astype(BF16), wuq_ref[...])
        for h in range(MLA_HEADS):
            q_out[h] = (rope(q_all[:, LANES * h:LANES * (h + 1)]) * MLA_SCALE).astype(BF16)
        _, kvn = _rms(ckv, gkv_ref[...])
        kv_all = _dot(kvn.astype(BF16), wukv_ref[...])
        kpe = rope(_dot(xb, w_ref[3200:3328, :], NT))
        for h in range(MLA_HEADS):
            kv_h = kv_all[:, LANES * h:LANES * (h + 1)]
            k_out[h] = jnp.where(lane < 64, kv_h, kpe).astype(BF16)
            if h % 2:
                v = jnp.where(lane >= 64, kv_h, 0.0)
            else:
                v = jnp.where(lane < 64, pltpu.roll(kv_h, 64, 1), 0.0)
            v_out[h] = jnp.where(lane == ONES_LANE[h % 2], 1.0, v).astype(BF16)

    row = lambda w: pl.BlockSpec((tr, w), lambda i: (i, 0))
    full = lambda a: pl.BlockSpec(a.shape, lambda i: (0,) * a.ndim)
    head = pl.BlockSpec((MLA_HEADS, tr, LANES), lambda i: (0, i, 0))
    widths = (Q_LORA, KV_LORA, D_MODEL, 512, 512, 512)
    return pl.pallas_call(
        body, name="proj", grid=(seq // tr,),
        in_specs=[row(D_MODEL), full(w_in_p), full(gq), full(gkv), full(wuq_e), full(wukv)] + [row(LANES)] * 4,
        out_specs=[row(w) for w in widths] + [head] * 3,
        out_shape=[jax.ShapeDtypeStruct((seq, w), F32) for w in widths]
        + [jax.ShapeDtypeStruct((MLA_HEADS, seq, LANES), BF16)] * 3,
        compiler_params=_cp(("arbitrary",), VMEM_LIMIT),
    )(x, w_in_p, gq, gkv, wuq_e, wukv, ct, st, cd, sd)


def _mla_rot(t, lane):
    return jnp.where(lane < 80, pltpu.roll(t, 112, 1), pltpu.roll(t, 16, 1))


def _dil_rot(t, lane):
    return jnp.where(lane % 64 < 8, pltpu.roll(t, 120, 1), pltpu.roll(t, 8, 1))


def _rms(c, g):
    r = lax.rsqrt(jnp.mean(c * c, axis=-1, keepdims=True) + RMS_EPS)
    return r, c * r * g


def _mla_fwd(q, k, v):
    seq = q.shape[1]
    tq = 512
    nq = seq // tq

    def body(q_ref, k_ref, v_ref, o_ref, lse_ref, m_s, acc_s, s_buf):
        i = pl.program_id(1)
        row = lax.broadcasted_iota(jnp.int32, (tq, tq), 0)
        col = lax.broadcasted_iota(jnp.int32, (tq, tq), 1)
        lane = lax.broadcasted_iota(jnp.int32, (tq, LANES), 1)
        m_s[...] = jnp.full((2, tq, LANES), NEG, F32)
        acc_s[...] = jnp.zeros((2, tq, LANES), F32)

        def block(j):
            return pl.ds(pl.multiple_of(j * tq, tq), tq)

        def scores(hh, j):
            return _dot(q_ref[hh], k_ref[hh, block(j), :], NT)

        def consume(hh, j, s):
            m_prev = m_s[hh]
            m_new = jnp.maximum(m_prev, jnp.max(s, axis=1, keepdims=True))
            p = jnp.exp(s - m_new[:, :1])
            acc_s[hh] = jnp.exp(m_prev - m_new) * acc_s[hh] + _dot(p.astype(BF16), v_ref[hh, block(j), :])
            m_s[hh] = m_new

        for hh in range(2):
            s_buf[0, hh] = scores(hh, 0)

        def full_step(j, carry):
            slot = j & 1
            for hh in range(2):
                s = s_buf[slot, hh]
                s_buf[1 - slot, hh] = scores(hh, j + 1)
                consume(hh, j, s)
            return carry

        lax.fori_loop(0, i, full_step, 0)
        total = jnp.zeros((tq, LANES), F32)
        for hh in range(2):
            consume(hh, i, jnp.where(col <= row, s_buf[i & 1, hh], NEG))
            acc = acc_s[hh]
            l = acc[:, ONES_LANE[hh]:ONES_LANE[hh] + 1]
            mine = (lane >= 64) if hh else (lane < 64)
            total = total + jnp.where(mine, acc / l, 0.0)
            lse_ref[hh] = m_s[hh] + jnp.log(l)
        o_ref[...] = total

    kv_spec = pl.BlockSpec((2, seq, LANES), lambda p, i: (p, 0, 0))
    return pl.pallas_call(
        body, name="mla_fwd", grid=(MLA_HEADS // 2, nq),
        in_specs=[pl.BlockSpec((2, tq, LANES), lambda p, i: (p, i, 0)), kv_spec, kv_spec],
        out_specs=[pl.BlockSpec((tq, LANES), lambda p, i: (i, p)), pl.BlockSpec((2, tq, LANES), lambda p, i: (p, i, 0))],
        out_shape=[jax.ShapeDtypeStruct((seq, 4 * LANES), F32), jax.ShapeDtypeStruct((MLA_HEADS, seq, LANES), F32)],
        scratch_shapes=[pltpu.VMEM((2, tq, LANES), F32), pltpu.VMEM((2, tq, LANES), F32),
                        pltpu.VMEM((2, 2, tq, tq), F32)],
        compiler_params=_cp(("arbitrary", "arbitrary"), VMEM_LIMIT),
    )(q, k, v)


DIL_Q_FWD = 2 * BLOCK
DIL_Q_BWD = BLOCK


def _dil_tile_index(t, d, seq, nq):
    per_class = seq // (nq * d)
    shift = per_class.bit_length() - 1
    r = t >> shift
    n = t & (per_class - 1)
    start = r + (nq * d) * n
    prev = jnp.maximum(start - BLOCK * d, r)
    if d == 1:
        start = pl.multiple_of(start, nq)
        prev = pl.multiple_of(prev, BLOCK)
    return (n == 0).astype(jnp.int32), start, prev


def _dil_rows(start, d, size):
    return pl.ds(start, size) if d == 1 else pl.ds(start, size, stride=d)


def _dil_bias(nq):
    i = lax.broadcasted_iota(jnp.int32, (2 * nq, BLOCK + nq), 0) % nq
    j = lax.broadcasted_iota(jnp.int32, (2 * nq, BLOCK + nq), 1)
    band = (j >= i) & (j <= i + BLOCK)
    return jnp.where(band, 0.0, NEG), jnp.where(band & (j >= BLOCK), 0.0, NEG)


def _stack_heads(t, lane):
    return jnp.concatenate([jnp.where(lane < 64, t, 0.0), jnp.where(lane >= 64, t, 0.0)], axis=0)


def _unstack_heads(t, lane):
    nq = t.shape[0] // 2
    return jnp.where(lane < 64, t[:nq], t[nq:])


def _dil_fwd(qr, kr, vb):
    seq = qr.shape[0]
    nq = DIL_Q_FWD
    n_tiles = seq // nq
    assert seq % (nq * max(DIL_DILATIONS)) == 0

    def body(q_ref, k_ref, v_ref, o_ref, lse_ref, m_s, l_s, n_s, bias_s):
        lane = lax.broadcasted_iota(jnp.int32, (nq, LANES), 1)
        bias_s[0], bias_s[1] = _dil_bias(nq)
        for bi, d in enumerate(DIL_DILATIONS):

            def tile(t, carry, d=d, bi=bi):
                first, start, prev = _dil_tile_index(t, d, seq, nq)
                rows, prows = _dil_rows(start, d, nq), _dil_rows(prev, d, BLOCK)
                qst = _stack_heads(q_ref[rows, :], lane).astype(BF16)
                if seq == nq * d:
                    kcat, vcat = k_ref[rows, :].astype(BF16), v_ref[rows, :].astype(BF16)
                    s = _dot(qst, kcat, NT) + bias_s[1, :, BLOCK:]
                else:
                    kcat = jnp.concatenate([k_ref[prows, :], k_ref[rows, :]], axis=0).astype(BF16)
                    vcat = jnp.concatenate([v_ref[prows, :], v_ref[rows, :]], axis=0).astype(BF16)
                    s = _dot(qst, kcat, NT) + bias_s[first]
                m = jnp.max(s, axis=1, keepdims=True)
                p = jnp.exp(s - m)
                l2 = _unstack_heads(jnp.sum(p, axis=1, keepdims=True) + jnp.zeros((2 * nq, LANES), F32), lane)
                m2 = _unstack_heads(m + jnp.zeros((2 * nq, LANES), F32), lane)
                num2 = _unstack_heads(_dot(p.astype(BF16), vcat), lane)
                if bi == 0:
                    m_s[rows, :] = m2
                    l_s[rows, :] = l2
                    n_s[rows, :] = num2
                else:
                    m_old = m_s[rows, :]
                    m_new = jnp.maximum(m_old, m2)
                    a = jnp.exp(m_old - m_new)
                    b = jnp.exp(m2 - m_new)
                    m_s[rows, :] = m_new
                    l_s[rows, :] = a * l_s[rows, :] + b * l2
                    n_s[rows, :] = a * n_s[rows, :] + b * num2
                return carry

            lax.fori_loop(0, n_tiles, tile, 0, unroll=2)
        o_ref[...] = n_s[...] / l_s[...]
        lse_ref[...] = m_s[...] + jnp.log(l_s[...])

    col = lambda off: pl.BlockSpec((seq, LANES), lambda p: (0, p + off))
    return pl.pallas_call(
        body, name="dil_fwd", grid=(4,),
        in_specs=[col(0), col(0), col(0)],
        out_specs=[col(0), pl.BlockSpec((None, seq, LANES), lambda p: (p, 0, 0))],
        out_shape=[jax.ShapeDtypeStruct((seq, 4 * LANES), F32), jax.ShapeDtypeStruct((4, seq, LANES), F32)],
        scratch_shapes=[pltpu.VMEM((seq, LANES), F32)] * 3 + [pltpu.VMEM((2, 2 * nq, BLOCK + nq), F32)],
        compiler_params=_cp(("arbitrary",), VMEM_LIMIT),
    )(qr, kr, vb)


def _post(x, o_a, o_b, gates, w_out, ln_g, ln_b, target):
    seq = x.shape[0]
    tr = 512

    def body(x_ref, oa_ref, ob_ref, g_ref, w_ref, lg_ref, lb_ref, t_ref,
             dz_ref, do_ref, dg_ref, dw_ref, dlg_ref, dlb_ref, loss_ref):
        @pl.when(pl.program_id(0) == 0)
        def _():
            dw_ref[...] = jnp.zeros_like(dw_ref)
            dlg_ref[...] = jnp.zeros_like(dlg_ref)
            dlb_ref[...] = jnp.zeros_like(dlb_ref)
            loss_ref[...] = jnp.zeros_like(loss_ref)

        g = g_ref[...]
        sg = jax.nn.sigmoid(g)
        silu = g * sg
        o = jnp.concatenate([oa_ref[...], ob_ref[...]], axis=1)
        mixb = (o * silu).astype(BF16)
        w = w_ref[...]
        z = ALPHA * x_ref[...] + _dot(mixb, w)
        mu = jnp.mean(z, axis=-1, keepdims=True)
        zc = z - mu
        rstd = lax.rsqrt(jnp.mean(zc * zc, axis=-1, keepdims=True) + LN_EPS)
        xhat = zc * rstd
        lg = lg_ref[...]
        err = xhat * lg + lb_ref[...] - t_ref[...]
        loss_ref[...] += jnp.sum(err * err) * (0.5 / D_MODEL)
        dy = err * (1.0 / D_MODEL)
        dlg_ref[...] += jnp.sum(dy * xhat, axis=0, keepdims=True)
        dlb_ref[...] += jnp.sum(dy, axis=0, keepdims=True)
        dxh = dy * lg
        dz = rstd * (dxh - jnp.mean(dxh, axis=-1, keepdims=True) - xhat * jnp.mean(dxh * xhat, axis=-1, keepdims=True))
        dz_ref[...] = dz
        dzb = dz.astype(BF16)
        dmix = _dot(dzb, w, NT)
        do_ref[...] = dmix * silu
        dg_ref[...] = (dmix * o * (sg * (1.0 + g * (1.0 - sg)))).astype(BF16)
        dw_ref[...] += _dot(mixb, dzb, TN)

    row = lambda w: pl.BlockSpec((tr, w), lambda i: (i, 0))
    full = lambda s: pl.BlockSpec(s, lambda i: (0, 0))
    return pl.pallas_call(
        body, name="post", grid=(seq // tr,),
        in_specs=[row(D_MODEL), row(512), row(512), row(D_MODEL), full((D_MODEL, D_MODEL)), full((1, D_MODEL)),
                  full((1, D_MODEL)), row(D_MODEL)],
        out_specs=[row(D_MODEL), row(D_MODEL), row(D_MODEL), full((D_MODEL, D_MODEL)), full((1, D_MODEL)),
                   full((1, D_MODEL)), full((1, LANES))],
        out_shape=[jax.ShapeDtypeStruct((seq, D_MODEL), F32), jax.ShapeDtypeStruct((seq, D_MODEL), F32),
                   jax.ShapeDtypeStruct((seq, D_MODEL), BF16), jax.ShapeDtypeStruct((D_MODEL, D_MODEL), F32),
                   jax.ShapeDtypeStruct((1, D_MODEL), F32), jax.ShapeDtypeStruct((1, D_MODEL), F32),
                   jax.ShapeDtypeStruct((1, LANES), F32)],
        compiler_params=_cp(("arbitrary",), VMEM_LIMIT),
    )(x, o_a, o_b, gates, w_out, ln_g, ln_b, target)


def _mla_bwd(q, k, v, d_o, o, lse):
    seq = q.shape[1]
    tq = 512
    nq = seq // tq

    def body(q_ref, k_ref, v_ref, do_ref, o_ref, lse_ref, dq_ref, dk_ref, dv_ref, d_s, lse_s, dk_s, dv_s, v_s, kt_s, dqt_s):
        j = pl.program_id(1)
        lane = lax.broadcasted_iota(jnp.int32, (tq, LANES), 1)
        row = lax.broadcasted_iota(jnp.int32, (tq, tq), 0)
        col = lax.broadcasted_iota(jnp.int32, (tq, tq), 1)

        @pl.when(j == 0)
        def _():
            dqt_s[...] = jnp.zeros_like(dqt_s)

            def rowsum(i, carry):
                rows = pl.ds(pl.multiple_of(i * tq, tq), tq)
                prod = do_ref[rows, :] * o_ref[rows, :]
                for hh in range(2):
                    mine = (lane >= 64) if hh else (lane < 64)
                    total = jnp.sum(jnp.where(mine, prod, 0.0), axis=1, keepdims=True)
                    d_s[hh, i] = jnp.transpose(total + jnp.zeros((tq, LANES), F32))[:8]
                    lse_s[hh, i] = jnp.transpose(lse_ref[hh, rows, :])[:8]
                return carry

            lax.fori_loop(0, nq, rowsum, 0)

        dk_s[...] = jnp.zeros_like(dk_s)
        dv_s[...] = jnp.zeros_like(dv_s)
        for hh in range(2):
            v_s[hh] = jnp.where(lane == ONES_LANE[hh], 0.0, v_ref[hh].astype(F32)).astype(BF16)
            kt_s[hh] = jnp.transpose(k_ref[hh].astype(F32)).astype(BF16)

        def step(i, masked):
            rows = pl.ds(pl.multiple_of(i * tq, tq), tq)
            dob = do_ref[rows, :].astype(BF16)
            for hh in range(2):
                qb, kb, vb = q_ref[hh, rows, :], k_ref[hh], v_s[hh]
                p = jnp.exp(_dot(kb, qb, NT) - lse_s[hh, i][:1])
                if masked:
                    p = jnp.where(row <= col, p, 0.0)
                dv_s[hh] += _dot(p.astype(BF16), dob)
                ds = (p * (_dot(vb, dob, NT) - d_s[hh, i][:1])).astype(BF16)
                dk_s[hh] += _dot(ds, qb)
                dqt_s[hh, i] += _dot(kt_s[hh], ds)

        def full_step(i, carry):
            step(i, False)
            return carry

        step(j, True)
        lax.fori_loop(j + 1, nq, full_step, 0)
        dk_ref[...] = dk_s[...]
        dv_ref[...] = dv_s[...]

        @pl.when(j == nq - 1)
        def _():
            def untranspose(i, carry):
                rows = pl.ds(pl.multiple_of(i * tq, tq), tq)
                for hh in range(2):
                    dq_ref[hh, rows, :] = jnp.transpose(dqt_s[hh, i])
                return carry

            lax.fori_loop(0, nq, untranspose, 0)

    whole = pl.BlockSpec((2, seq, LANES), lambda p, j: (p, 0, 0))
    blk = pl.BlockSpec((2, tq, LANES), lambda p, j: (p, j, 0))
    pair = pl.BlockSpec((seq, LANES), lambda p, j: (0, p))
    shape = jax.ShapeDtypeStruct((MLA_HEADS, seq, LANES), F32)
    return pl.pallas_call(
        body, name="mla_bwd", grid=(MLA_HEADS // 2, nq),
        in_specs=[whole, blk, blk, pair, pair, whole],
        out_specs=[whole, blk, blk], out_shape=[shape] * 3,
        scratch_shapes=[pltpu.VMEM((2, nq, 8, tq), F32), pltpu.VMEM((2, nq, 8, tq), F32),
                        pltpu.VMEM((2, tq, LANES), F32), pltpu.VMEM((2, tq, LANES), F32),
                        pltpu.VMEM((2, tq, LANES), BF16), pltpu.VMEM((2, LANES, tq), BF16),
                        pltpu.VMEM((2, nq, LANES, tq), F32)],
        compiler_params=_cp(("arbitrary", "arbitrary"), VMEM_LIMIT),
    )(q, k, v, d_o, o, lse)


def _dil_bwd(qr, kr, vb, d_o, o, lse):
    seq = qr.shape[0]
    nq = DIL_Q_BWD
    n_tiles = seq // nq
    chunk = 512

    def body(q_ref, k_ref, v_ref, do_ref, o_ref, lse_ref, dq_ref, dk_ref, dv_ref, d_s, dq_s, dk_s, dv_s, bias_s):
        lane = lax.broadcasted_iota(jnp.int32, (nq, LANES), 1)
        lanec = lax.broadcasted_iota(jnp.int32, (chunk, LANES), 1)
        bias_s[0], bias_s[1] = [b[:nq] for b in _dil_bias(nq)]

        def rowsum(i, carry):
            rows = pl.ds(pl.multiple_of(i * chunk, chunk), chunk)
            prod = do_ref[rows, :] * o_ref[rows, :]
            lo = jnp.sum(jnp.where(lanec < 64, prod, 0.0), axis=1, keepdims=True)
            hi = jnp.sum(jnp.where(lanec >= 64, prod, 0.0), axis=1, keepdims=True)
            d_s[rows, :] = jnp.where(lanec < 64, lo, hi)
            return carry

        lax.fori_loop(0, seq // chunk, rowsum, 0)
        dq_s[...] = jnp.zeros_like(dq_s)
        dk_s[...] = jnp.zeros_like(dk_s)
        dv_s[...] = jnp.zeros_like(dv_s)
        for d in DIL_DILATIONS:

            def tile(t, carry, d=d):
                first, start, prev = _dil_tile_index(t, d, seq, nq)
                rows, prows = _dil_rows(start, d, nq), _dil_rows(prev, d, BLOCK)
                q_t, do_t = q_ref[rows, :], do_ref[rows, :]
                lse_t, d_t = lse_ref[rows, :], d_s[rows, :]
                kcat = jnp.concatenate([k_ref[prows, :], k_ref[rows, :]], axis=0).astype(BF16)
                vcat = jnp.concatenate([v_ref[prows, :], v_ref[rows, :]], axis=0).astype(BF16)
                bias = bias_s[first]
                dq_t = jnp.zeros((nq, LANES), F32)
                dkcat = jnp.zeros((BLOCK + nq, LANES), F32)
                dvcat = jnp.zeros((BLOCK + nq, LANES), F32)
                for hh in range(2):
                    mine = (lane >= 64) if hh else (lane < 64)
                    c0 = 64 * hh
                    qh = jnp.where(mine, q_t, 0.0).astype(BF16)
                    doh = jnp.where(mine, do_t, 0.0).astype(BF16)
                    p = jnp.exp(_dot(qh, kcat, NT) + bias - lse_t[:, c0:c0 + 1])
                    dvcat = dvcat + _dot(p.astype(BF16), doh, TN)
                    dp = _dot(doh, vcat, NT)
                    ds = (p * (dp - d_t[:, c0:c0 + 1])).astype(BF16)
                    dq_t = dq_t + jnp.where(mine, _dot(ds, kcat), 0.0)
                    dkcat = dkcat + _dot(ds, qh, TN)
                dq_s[rows, :] += dq_t
                dk_s[prows, :] += dkcat[:BLOCK]
                dk_s[rows, :] += dkcat[BLOCK:]
                dv_s[prows, :] += dvcat[:BLOCK]
                dv_s[rows, :] += dvcat[BLOCK:]
                return carry

            lax.fori_loop(0, n_tiles, tile, 0, unroll=4)
        dq_ref[...] = dq_s[...].astype(BF16)
        dk_ref[...] = dk_s[...].astype(BF16)
        dv_ref[...] = dv_s[...].astype(BF16)

    col = lambda off: pl.BlockSpec((seq, LANES), lambda p: (0, p + off))
    shape = jax.ShapeDtypeStruct((seq, 4 * LANES), BF16)
    return pl.pallas_call(
        body, name="dil_bwd", grid=(4,),
        in_specs=[col(0), col(0), col(0), col(4), col(0), pl.BlockSpec((None, seq, LANES), lambda p: (p, 0, 0))],
        out_specs=[col(0)] * 3, out_shape=[shape] * 3,
        scratch_shapes=[pltpu.VMEM((seq, LANES), F32)] * 4 + [pltpu.VMEM((2, nq, BLOCK + nq), F32)],
        compiler_params=_cp(("arbitrary",), VMEM_LIMIT),
    )(qr, kr, vb, d_o, o, lse)


def _in_bwd(dz, cq, ckv, gq, gkv, wuq_e, wukv, ct, st, dq, dk, dv, dgates, dqr, dkr, dvb, cd, sd, w_in_p):
    seq = dz.shape[0]
    tr = 512

    def body(dz_ref, cq_ref, ckv_ref, gq_ref, gkv_ref, wuq_ref, wukv_ref, ct_ref, st_ref, dq_ref, dk_ref, dv_ref,
             dg_ref, dqr_ref, dkr_ref, dvb_ref, cd_ref, sd_ref, w_ref,
             gx_ref, dh_ref, dwuq_ref, dwukv_ref, dgq_ref, dgkv_ref):
        @pl.when(pl.program_id(0) == 0)
        def _():
            dwuq_ref[...] = jnp.zeros_like(dwuq_ref)
            dwukv_ref[...] = jnp.zeros_like(dwukv_ref)
            dgq_ref[...] = jnp.zeros_like(dgq_ref)
            dgkv_ref[...] = jnp.zeros_like(dgkv_ref)

        lane = lax.broadcasted_iota(jnp.int32, (tr, LANES), 1)
        rope_lanes = jnp.logical_and(lane >= 64, lane < 96)
        ct_, st_ = ct_ref[...], st_ref[...]

        def mla_rope_t(g):
            return ct_ * g + jnp.where(rope_lanes, _mla_rot(st_ * g, lane), 0.0)

        def norm_bwd(c, g, dn, dg_ref):
            r, _ = _rms(c, g)
            u = dn * g
            dg_ref[...] += jnp.sum(dn * c * r, axis=0, keepdims=True)
            return r * u - c * (r * r * r) * jnp.mean(u * c, axis=-1, keepdims=True)

        c, g = cq_ref[...], gq_ref[...]
        _, qn = _rms(c, g)
        dq_all = jnp.concatenate([mla_rope_t(dq_ref[h] * MLA_SCALE) for h in range(MLA_HEADS)], axis=1).astype(BF16)
        dwuq_ref[...] += _dot(qn.astype(BF16), dq_all, TN)
        dcq = norm_bwd(c, g, _dot(dq_all, wuq_ref[...], NT), dgq_ref).astype(BF16)

        c, g = ckv_ref[...], gkv_ref[...]
        _, kvn = _rms(c, g)
        dkpe = jnp.zeros((tr, LANES), F32)
        parts = []
        for h in range(MLA_HEADS):
            dk_h, dv_h = dk_ref[h], dv_ref[h]
            if h % 2 == 0:
                dv_h = pltpu.roll(dv_h, 64, 1)
            parts.append(jnp.where(lane < 64, dk_h, dv_h))
            dkpe = dkpe + jnp.where(rope_lanes, dk_h, 0.0)
        dkv_all = jnp.concatenate(parts, axis=1).astype(BF16)
        dwukv_ref[...] += _dot(kvn.astype(BF16), dkv_all, TN)
        dckv = norm_bwd(c, g, _dot(dkv_all, wukv_ref[...], NT), dgkv_ref).astype(BF16)
        dkrope = mla_rope_t(dkpe).astype(BF16)

        rot_lanes = lane % 64 < DIL_ROT
        cd_, sd_ = cd_ref[...], sd_ref[...]

        def dil_rope_t(g):
            return cd_ * g + jnp.where(rot_lanes, _dil_rot(sd_ * g, lane), 0.0)

        dqb = [dil_rope_t(dqr_ref[:, LANES * p:LANES * (p + 1)].astype(F32) * DIL_SCALE).astype(BF16) for p in range(4)]
        dkb = [dil_rope_t(dkr_ref[:, LANES * p:LANES * (p + 1)].astype(F32)).astype(BF16) for p in range(4)]
        dh = jnp.concatenate([dcq, dckv, dg_ref[...]] + dqb + dkb + [dvb_ref[...], dkrope], axis=1)
        dh_ref[...] = dh
        gx_ref[...] = ALPHA * dz_ref[...] + _dot(dh, w_ref[...])

    row = lambda w: pl.BlockSpec((tr, w), lambda i: (i, 0))
    full = lambda a: pl.BlockSpec(a.shape, lambda i: (0,) * a.ndim)
    head = pl.BlockSpec((MLA_HEADS, tr, LANES), lambda i: (0, i, 0))
    return pl.pallas_call(
        body, name="in_bwd", grid=(seq // tr,),
        in_specs=[row(D_MODEL), row(Q_LORA), row(KV_LORA), full(gq), full(gkv), full(wuq_e), full(wukv), row(LANES),
                  row(LANES), head, head, head, row(D_MODEL), row(512), row(512), row(512), row(LANES), row(LANES),
                  full(w_in_p)],
        out_specs=[row(D_MODEL), row(IN_WIDTH_PAD), full(wuq_e), full(wukv), full(gq), full(gkv)],
        out_shape=[jax.ShapeDtypeStruct((seq, D_MODEL), F32), jax.ShapeDtypeStruct((seq, IN_WIDTH_PAD), BF16),
                   jax.ShapeDtypeStruct(wuq_e.shape, F32), jax.ShapeDtypeStruct(wukv.shape, F32),
                   jax.ShapeDtypeStruct(gq.shape, F32), jax.ShapeDtypeStruct(gkv.shape, F32)],
        compiler_params=_cp(("arbitrary",), VMEM_LIMIT),
    )(dz, cq, ckv, gq, gkv, wuq_e, wukv, ct, st, dq, dk, dv, dgates, dqr, dkr, dvb, cd, sd, w_in_p)


def _dw_in(x, dh):
    seq = dh.shape[0]
    tk = 512
    tn = IN_WIDTH_PAD // 2

    def body(x_ref, dh_ref, o_ref):
        @pl.when(pl.program_id(1) == 0)
        def _():
            o_ref[...] = jnp.zeros_like(o_ref)

        o_ref[...] += _dot(dh_ref[...], x_ref[...].astype(BF16), TN)

    return pl.pallas_call(
        body, name="dw_in", grid=(2, seq // tk),
        in_specs=[pl.BlockSpec((tk, D_MODEL), lambda n, k: (k, 0)), pl.BlockSpec((tk, tn), lambda n, k: (k, n))],
        out_specs=pl.BlockSpec((tn, D_MODEL), lambda n, k: (n, 0)),
        out_shape=jax.ShapeDtypeStruct((IN_WIDTH_PAD, D_MODEL), F32),
        compiler_params=_cp(("arbitrary", "arbitrary"), VMEM_LIMIT),
    )(x, dh)


def _adam_update(w, g, m, v):
    nm = ADAM_B1 * m + (1.0 - ADAM_B1) * g
    nv = ADAM_B2 * v + (1.0 - ADAM_B2) * jnp.square(g)
    m_hat = nm / (1.0 - ADAM_B1 ** ADAM_STEP)
    v_hat = nv / (1.0 - ADAM_B2 ** ADAM_STEP)
    return -ADAM_LR * (m_hat / (jnp.sqrt(v_hat) + ADAM_EPS) + ADAM_WD * w), nm, nv


def _adamw(w, g, m, v, name):
    rows, cols = w.shape
    tc = 256 if cols % 256 == 0 and rows * cols > 2 ** 18 else cols

    def body(w_ref, g_ref, m_ref, v_ref, d_ref, nm_ref, nv_ref):
        d_ref[...], nm_ref[...], nv_ref[...] = _adam_update(w_ref[...], g_ref[...], m_ref[...], v_ref[...])

    spec = pl.BlockSpec((rows, tc), lambda i: (0, i))
    return pl.pallas_call(
        body, name=name, grid=(cols // tc,), in_specs=[spec] * 4, out_specs=[spec] * 3,
        out_shape=[jax.ShapeDtypeStruct(w.shape, F32)] * 3, compiler_params=_cp(("arbitrary",)),
    )(w, g, m, v)


def _adamw_vectors(small_sum, ws, ms, vs):
    k = len(ws)
    sizes = [w.shape[-1] for w in ws]

    def body(s_ref, *refs):
        ins, outs = refs[:3 * k], refs[3 * k:]
        for i, size in enumerate(sizes):
            g = s_ref[i:i + 1, 0:size]
            outs[i][...] = g
            outs[k + i][...], outs[2 * k + i][...], outs[3 * k + i][...] = _adam_update(
                ins[i][...], g, ins[k + i][...], ins[2 * k + i][...])

    out = pl.pallas_call(
        body, name="adamw_vectors", out_shape=[jax.ShapeDtypeStruct((1, size), F32) for size in sizes] * 4,
    )(small_sum, *[a.reshape(1, -1) for a in list(ws) + list(ms) + list(vs)])
    return [[a.reshape(-1) for a in out[k * j:k * (j + 1)]] for j in range(4)]


def _local_step(x2, target, w_in_p, w_uq_f, wukv_f, w_out_f, q_norm_g, kv_norm_g, ln_g, ln_b):
    seq = x2.shape[0]
    wuq_e = jnp.pad(w_uq_f.reshape(Q_LORA, MLA_HEADS, 96), ((0, 0), (0, 0), (0, 32))).reshape(Q_LORA, MLA_HEADS * LANES)
    ct, st, cd, sd = _rope_tables(seq)
    gq = q_norm_g.reshape(1, Q_LORA)
    gkv = kv_norm_g.reshape(1, KV_LORA)

    cq, ckv, gates, qr, krot, vb, q_e, k_e, v_e = _proj(x2, w_in_p, gq, gkv, wuq_e, wukv_f, ct, st, cd, sd)
    o_a, lse_a = _mla_fwd(q_e, k_e, v_e)
    o_b, lse_b = _dil_fwd(qr, krot, vb)

    dz, d_o, d_gates, dw_out, dln_g, dln_b, loss_part = _post(
        x2, o_a, o_b, gates, w_out_f, ln_g.reshape(1, D_MODEL), ln_b.reshape(1, D_MODEL), target)
    dq_e, dk_e, dv_e = _mla_bwd(q_e, k_e, v_e, d_o, o_a, lse_a)
    dqr, dkr, dvb = _dil_bwd(qr, krot, vb, d_o, o_b, lse_b)
    grad_x, dh, dwuq_e, dwukv, dgq, dgkv = _in_bwd(
        dz, cq, ckv, gq, gkv, wuq_e, wukv_f, ct, st, dq_e, dk_e, dv_e, d_gates, dqr, dkr, dvb, cd, sd, w_in_p)
    dw_in = _dw_in(x2, dh)
    dw_uq = dwuq_e.reshape(Q_LORA, MLA_HEADS, LANES)[:, :, :96].reshape(Q_LORA, MLA_HEADS * 96)
    return loss_part, grad_x, dw_in, dw_uq, dwukv, dw_out, dgq, dgkv, dln_g, dln_b


def kernel(x, w_in, q_norm_g, kv_norm_g, w_uq, w_ukv, w_out, ln_g, ln_b, loss_target, m_w_in, m_q_norm_g, m_kv_norm_g, m_w_uq, m_w_ukv, m_w_out, m_ln_g, m_ln_b, v_w_in, v_q_norm_g, v_kv_norm_g, v_w_uq, v_w_ukv, v_w_out, v_ln_g, v_ln_b):
    seq = x.shape[1]
    x2 = x.reshape(seq, D_MODEL)
    target = loss_target.reshape(seq, D_MODEL)

    g_w_in, g_w_uq, g_w_ukv, g_w_out = _all_gather_weights([w_in.T, w_uq, w_ukv, w_out])
    by_cols = lambda g: jnp.concatenate([g[j] for j in range(N_SHARD)], axis=1)
    loss_part, grad_x, dw_in, dw_uq, dwukv, dw_out, dgq, dgkv, dln_g, dln_b = _local_step(
        x2, target, _permute_w_in_t_shards(g_w_in), by_cols(g_w_uq), by_cols(g_w_ukv), g_w_out.reshape(D_MODEL, D_MODEL),
        q_norm_g, kv_norm_g, ln_g, ln_b)

    to_shards = lambda d: d.reshape(d.shape[0], N_SHARD, d.shape[1] // N_SHARD).transpose(1, 0, 2)
    grads = [dw_in, to_shards(dw_uq), to_shards(dwukv), dw_out.reshape(N_SHARD, 256, D_MODEL)]
    *chip_sums, small_sum = _reduce_over_sibling(grads, [dgq, dgkv, dln_g, dln_b, loss_part])
    g_in_t, g_uq, g_ukv, g_out = _reduce_over_chips(chip_sums)
    g_in = g_in_t.T
    loss = small_sum[4, 0]

    big = [[o.T for o in _adamw(w.T, g.T, m.T, v.T, name)] for w, g, m, v, name in (
        (w_in, g_in, m_w_in, v_w_in, "adamw_w_in"), (w_uq, g_uq, m_w_uq, v_w_uq, "adamw_w_uq"))]
    big += [_adamw(w, g, m, v, name) for w, g, m, v, name in (
        (w_ukv, g_ukv, m_w_ukv, v_w_ukv, "adamw_w_ukv"), (w_out, g_out, m_w_out, v_w_out, "adamw_w_out"))]
    vec_g, vec_delta, vec_m, vec_v = _adamw_vectors(
        small_sum, [q_norm_g, kv_norm_g, ln_g, ln_b], [m_q_norm_g, m_kv_norm_g, m_ln_g, m_ln_b],
        [v_q_norm_g, v_kv_norm_g, v_ln_g, v_ln_b])

    def ordered(bigs, vecs):
        return [bigs[0], vecs[0], vecs[1], bigs[1], bigs[2], bigs[3], vecs[2], vecs[3]]

    grads_out = ordered([g_in, g_uq, g_ukv, g_out], vec_g)
    deltas = ordered([b[0] for b in big], vec_delta)
    new_m = ordered([b[1] for b in big], vec_m)
    new_v = ordered([b[2] for b in big], vec_v)
    return (loss, grad_x.reshape(x.shape), *grads_out, *deltas, *new_m, *new_v)
```

```python
import functools

import jax
import jax.numpy as jnp
import numpy as np
from jax import lax
from jax.experimental import pallas as pl
from jax.experimental.pallas import tpu as pltpu

F32 = jnp.float32
BF16 = jnp.bfloat16

D_MODEL = 1024
ROPE_THETA = 500000.0
BLOCK = 128
NEG = -1e30
RMS_EPS = 1e-6
LN_EPS = 1e-5

MLA_HEADS = 8
MLA_NOPE = 64
MLA_ROPE = 32
Q_LORA = 384
KV_LORA = 256
DIL_HEADS = 8
DIL_HEAD_DIM = 64
DIL_ROT = 16
DIL_DILATIONS = (1, 4, 16)
IN_WIDTH = 3232
IN_WIDTH_PAD = 3328
ONES_LANE = (64, 0)
MLA_SCALE = (MLA_NOPE + MLA_ROPE) ** -0.5
DIL_SCALE = DIL_HEAD_DIM ** -0.5
ALPHA = 2.0 ** 0.25

ADAM_LR = 0.001
ADAM_B1 = 0.9
ADAM_B2 = 0.999
ADAM_EPS = 1e-08
ADAM_WD = 0.01
ADAM_STEP = 10

N_SHARD = 4
GRAD_SHAPES = ((808, 1024), (384, 192), (256, 256), (256, 1024))
GRAD_SPLIT_COLS = (True, False, False, False)
ROW_CHUNK = 64
LANES = 128
VMEM_LIMIT = 56 * 1024 * 1024
MESH = pl.DeviceIdType.MESH

NT = (((1,), (1,)), ((), ()))
TN = (((0,), (0,)), ((), ()))


def _cp(sem=None, vmem=None):
    return pltpu.CompilerParams(dimension_semantics=sem, vmem_limit_bytes=vmem)


def _dot(a, b, dims=None):
    if dims is None:
        return jnp.dot(a, b, preferred_element_type=F32)
    return lax.dot_general(a, b, dims, preferred_element_type=F32)


def _rope_tables(seq):
    f32 = np.float32
    pos = np.arange(seq, dtype=f32)[:, None]
    one, zero = np.ones((seq, 64), f32), np.zeros((seq, 64), f32)

    def cos_sin(dim):
        inv = np.power(f32(ROPE_THETA), -np.arange(0, dim, 2, dtype=f32) / f32(dim)).astype(f32)
        ang = (pos * inv[None, :]).astype(f32)
        return np.cos(ang).astype(f32), np.sin(ang).astype(f32)

    cos, sin = cos_sin(MLA_ROPE)
    ct = np.concatenate([one, cos, cos, zero[:, :32]], axis=1)
    st = np.concatenate([zero, -sin, sin, zero[:, :32]], axis=1)
    cos, sin = cos_sin(DIL_ROT)
    cd = np.concatenate([cos, cos, one[:, :48]], axis=1)
    sd = np.concatenate([-sin, sin, zero[:, :48]], axis=1)
    return tuple(jnp.asarray(t) for t in (ct, st, np.tile(cd, (1, 2)), np.tile(sd, (1, 2))))


W_IN_ORDER = ((0, 640), (672, 1184), (2720, 3232), (1184, 2720), None, (640, 672))


def _permute_w_in_t(w_t):
    z = jnp.zeros((64, w_t.shape[1]), w_t.dtype)
    parts = [z if r is None else w_t[r[0]:r[1]] for r in W_IN_ORDER]
    return jnp.concatenate(parts + [z[:32]], axis=0)


def _permute_w_in_t_shards(g):
    z = jnp.zeros((64, g.shape[2]), g.dtype)
    parts = [g[k, src:src + rows] for k, src, _, rows in _w_in_row_pieces()]
    at = [dst for _, _, dst, _ in _w_in_row_pieces()]
    assert at == sorted(at) and at[-1] == 3264
    return jnp.concatenate(parts[:-1] + [z, parts[-1], z[:32]], axis=0)


def _w_in_row_pieces():
    width = GRAD_SHAPES[0][0]
    pieces, at = [], 0
    for r in W_IN_ORDER:
        if r is None:
            at += 64
            continue
        for k in range(N_SHARD):
            lo, hi = max(r[0], width * k), min(r[1], width * (k + 1))
            if lo < hi:
                pieces.append((k, lo - width * k, at + lo - r[0], hi - lo))
        at += r[1] - r[0]
    return pieces


def _position():
    return lax.axis_index("x"), lax.axis_index("y"), lax.axis_index("c")


def _all_gather_weights(shards):
    n = len(shards)

    def body(*refs):
        ins, outs = refs[:n], refs[n:2 * n]
        send_sems, recv_sems = refs[2 * n:]
        x, y, c = _position()
        me = 2 * x + y
        chips = [(1 - x, y), (x, 1 - y), (1 - x, 1 - y)]
        for a in range(n):
            rows, cols = GRAD_SHAPES[a]
            if GRAD_SPLIT_COLS[a]:
                blocks = [(slice(None), slice(c0, c0 + LANES)) for c0 in range(0, cols, LANES)]
            else:
                blocks = [(slice(r0, r0 + ROW_CHUNK), slice(None)) for r0 in range(0, rows, ROW_CHUNK)]
            for blk in blocks:
                outs[a][(me,) + blk] = ins[a][blk].astype(BF16)

        def copy(k, a, slot, part, to):
            ref = outs[a].at[(slot,) + part]
            return pltpu.make_async_remote_copy(
                src_ref=ref, dst_ref=ref, send_sem=send_sems.at[k * n + a], recv_sem=recv_sems.at[k * n + a],
                device_id=to, device_id_type=MESH)

        half = [_grad_half(a, c) for a in range(n)]
        other = [_grad_half(a, 1 - c) for a in range(n)]
        first = [copy(k, a, me, half[a], (px, py, c)) for k, (px, py) in enumerate(chips) for a in range(n)]
        for cp in first:
            cp.start()
        passed = []
        for k, (px, py) in enumerate(chips):
            for a in range(n):
                copy(k, a, 2 * px + py, half[a], (x, y, c)).wait_recv()
                cp = copy(3 + k, a, 2 * px + py, half[a], (x, y, 1 - c))
                cp.start()
                passed.append(cp)
        for k, (px, py) in enumerate(chips):
            for a in range(n):
                copy(3 + k, a, 2 * px + py, other[a], (x, y, c)).wait_recv()
        for cp in first + passed:
            cp.wait_send()

    vmem = pl.BlockSpec(memory_space=pltpu.VMEM)
    return pl.pallas_call(
        body, name="all_gather_weights",
        out_shape=[jax.ShapeDtypeStruct((N_SHARD,) + s, BF16) for s in GRAD_SHAPES],
        in_specs=[vmem] * n, out_specs=[vmem] * n,
        scratch_shapes=[pltpu.SemaphoreType.DMA((6 * n,)), pltpu.SemaphoreType.DMA((6 * n,))],
        compiler_params=_cp(None, VMEM_LIMIT),
    )(*shards)


def _grad_half_shape(a):
    rows, cols = GRAD_SHAPES[a]
    return (rows, cols // 2) if GRAD_SPLIT_COLS[a] else (rows // 2, cols)


def _grad_half(a, c):
    rows, cols = GRAD_SHAPES[a]
    if GRAD_SPLIT_COLS[a]:
        return slice(None), pl.ds(pl.multiple_of(c * (cols // 2), LANES), cols // 2)
    return pl.ds(pl.multiple_of(c * (rows // 2), ROW_CHUNK), rows // 2), slice(None)


def _grad_chunks(a, c):
    rows, cols = GRAD_SHAPES[a]
    if GRAD_SPLIT_COLS[a]:
        return [((slice(None), pl.ds(c0, LANES)),
                 (slice(None), pl.ds(pl.multiple_of(c * (cols // 2) + c0, LANES), LANES)))
                for c0 in range(0, cols // 2, LANES)]
    return [((pl.ds(r0, ROW_CHUNK), slice(None)),
             (pl.ds(pl.multiple_of(c * (rows // 2) + r0, ROW_CHUNK), ROW_CHUNK), slice(None)))
            for r0 in range(0, rows // 2, ROW_CHUNK)]


def _reduce_over_sibling(grads, small_rows):
    n = len(grads)
    n_small = len(small_rows)
    pieces = _w_in_row_pieces()

    def body(*refs):
        g_hbm, rows_in = refs[:n], refs[n:n + n_small]
        sums, small_sum = refs[n + n_small:2 * n + n_small], refs[2 * n + n_small]
        scratch = refs[2 * n + n_small + 1:]
        stage, got = scratch[:n], scratch[n:2 * n]
        sm, smalls, send_sems, recv_sems, local_sems = scratch[2 * n:]
        x, y, c = _position()
        me = 4 * x + 2 * y + c
        sm[...] = jnp.zeros_like(sm)
        for i, row in enumerate(rows_in):
            sm[i:i + 1, 0:row.shape[1]] = row[...]
        loads = [[pltpu.make_async_copy(g_hbm[0].at[pl.ds(src, rows)], stage[0].at[k, pl.ds(dst, rows)],
                                        local_sems.at[n + i])
                  for i, (k, dst, src, rows) in enumerate(pieces)]]
        loads += [[pltpu.make_async_copy(g_hbm[a], stage[a], local_sems.at[a])] for a in range(1, n)]
        for group in loads:
            for ld in group:
                ld.start()
        smalls[me] = sm[...]
        sends = []
        for rel in range(1, 8):
            px = 1 - x if rel // 4 else x
            py = 1 - y if (rel // 2) % 2 else y
            pc = 1 - c if rel % 2 else c
            cp = pltpu.make_async_remote_copy(
                src_ref=sm, dst_ref=smalls.at[me], send_sem=send_sems.at[n + rel], recv_sem=recv_sems.at[n + rel],
                device_id=(px, py, pc), device_id_type=MESH)
            cp.start()
            sends.append((cp, 4 * px + 2 * py + pc))
        swaps = []
        for a in range(n):
            for ld in loads[a]:
                ld.wait()
            cp = pltpu.make_async_remote_copy(
                src_ref=stage[a].at[(slice(None),) + _grad_half(a, 1 - c)], dst_ref=got[a], send_sem=send_sems.at[a], recv_sem=recv_sems.at[a],
                device_id=(x, y, 1 - c), device_id_type=MESH)
            cp.start()
            swaps.append(cp)
        for a in range(n):
            swaps[a].wait_recv()
            for k in range(N_SHARD):
                for in_half, in_whole in _grad_chunks(a, c):
                    pair = stage[a][(k,) + in_whole] + got[a][(k,) + in_half]
                    sums[a][(k,) + in_half] = pair.astype(BF16)
        for rel, (cp, peer) in enumerate(sends, start=1):
            pltpu.make_async_remote_copy(
                src_ref=sm, dst_ref=smalls.at[peer], send_sem=send_sems.at[n + rel], recv_sem=recv_sems.at[n + rel],
                device_id=(x, y, c), device_id_type=MESH).wait_recv()
        total = smalls[0]
        for dev in range(1, 8):
            total = total + smalls[dev]
        small_sum[...] = total
        for cp in swaps:
            cp.wait_send()
        for cp, _ in sends:
            cp.wait_send()

    vmem = pl.BlockSpec(memory_space=pltpu.VMEM)
    half = [(N_SHARD,) + _grad_half_shape(a) for a in range(n)]
    return pl.pallas_call(
        body, name="reduce_over_sibling",
        out_shape=[jax.ShapeDtypeStruct(s, BF16) for s in half] + [jax.ShapeDtypeStruct((8, D_MODEL), F32)],
        in_specs=[pl.BlockSpec(memory_space=pl.ANY)] * n + [vmem] * n_small, out_specs=[vmem] * (n + 1),
        scratch_shapes=[pltpu.VMEM((N_SHARD,) + s, F32) for s in GRAD_SHAPES] + [pltpu.VMEM(s, F32) for s in half]
        + [pltpu.VMEM((8, D_MODEL), F32), pltpu.VMEM((8, 8, D_MODEL), F32),
           pltpu.SemaphoreType.DMA((n + 8,)), pltpu.SemaphoreType.DMA((n + 8,)),
           pltpu.SemaphoreType.DMA((n + len(pieces),))],
        compiler_params=_cp(None, VMEM_LIMIT),
    )(*grads, *small_rows)


def _reduce_over_chips(sums):
    n = len(sums)

    def body(*refs):
        h, outs, got = refs[:n], refs[n:2 * n], refs[2 * n:3 * n]
        send_sems, recv_sems = refs[3 * n:]
        x, y, c = _position()
        me = 2 * x + y
        chips = [(1 - x, y), (x, 1 - y), (1 - x, 1 - y)]
        sends = []
        for k, (px, py) in enumerate(chips):
            for a in range(n):
                cp = pltpu.make_async_remote_copy(
                    src_ref=h[a].at[2 * px + py], dst_ref=got[a].at[k], send_sem=send_sems.at[k * n + a],
                    recv_sem=recv_sems.at[k * n + a], device_id=(px, py, c), device_id_type=MESH)
                cp.start()
                sends.append(cp)
        for cp in sends:
            cp.wait_recv()
        joins = []
        for a in range(n):
            for in_half, in_whole in _grad_chunks(a, c):
                total = h[a][(me,) + in_half].astype(F32)
                for k in range(3):
                    total = total + got[a][(k,) + in_half].astype(F32)
                outs[a][in_whole] = total
            half = outs[a].at[_grad_half(a, c)]
            cp = pltpu.make_async_remote_copy(
                src_ref=half, dst_ref=half, send_sem=send_sems.at[3 * n + a],
                recv_sem=recv_sems.at[3 * n + a], device_id=(x, y, 1 - c), device_id_type=MESH)
            cp.start()
            joins.append(cp)
        for a in range(n):
            other = outs[a].at[_grad_half(a, 1 - c)]
            pltpu.make_async_remote_copy(
                src_ref=other, dst_ref=other, send_sem=send_sems.at[3 * n + a],
                recv_sem=recv_sems.at[3 * n + a], device_id=(x, y, c), device_id_type=MESH).wait_recv()
        for cp in sends + joins:
            cp.wait_send()

    vmem = pl.BlockSpec(memory_space=pltpu.VMEM)
    return pl.pallas_call(
        body, name="reduce_over_chips",
        out_shape=[jax.ShapeDtypeStruct(s, F32) for s in GRAD_SHAPES],
        in_specs=[vmem] * n, out_specs=[vmem] * n,
        scratch_shapes=[pltpu.VMEM((3,) + _grad_half_shape(a), BF16) for a in range(n)]
        + [pltpu.SemaphoreType.DMA((4 * n,)), pltpu.SemaphoreType.DMA((4 * n,))],
        compiler_params=_cp(None, VMEM_LIMIT),
    )(*sums)


def _proj(x, w_in_p, gq, gkv, wuq_e, wukv, ct, st, cd, sd):
    seq = x.shape[0]
    tr = 512

    def body(x_ref, w_ref, gq_ref, gkv_ref, wuq_ref, wukv_ref, ct_ref, st_ref, cd_ref, sd_ref,
             cq_ref, ckv_ref, g_ref, qr_ref, kr_ref, vb_ref, q_out, k_out, v_out):
        lane = lax.broadcasted_iota(jnp.int32, (tr, LANES), 1)
        xb = x_ref[...].astype(BF16)
        cq = _dot(xb, w_ref[0:384, :], NT)
        ckv = _dot(xb, w_ref[384:640, :], NT)
        cq_ref[...] = cq
        ckv_ref[...] = ckv
        g_ref[...] = _dot(xb, w_ref[640:1664, :], NT)

        cd_, sd_ = cd_ref[...], sd_ref[...]
        qb = _dot(xb, w_ref[1664:2176, :], NT)
        kb = _dot(xb, w_ref[2176:2688, :], NT)
        for p in range(4):
            cols = slice(LANES * p, LANES * (p + 1))
            t = qb[:, cols]
            qr_ref[:, cols] = (t * cd_ + _dil_rot(t, lane) * sd_) * DIL_SCALE
            t = kb[:, cols]
            kr_ref[:, cols] = t * cd_ + _dil_rot(t, lane) * sd_
        vb_ref[...] = _dot(xb, w_ref[2688:3200, :], NT)

        ct_, st_ = ct_ref[...], st_ref[...]

        def rope(t):
            return t * ct_ + _mla_rot(t, lane) * st_

        _, qn = _rms(cq, gq_ref[...])
        q_all = _dot(qn.astype(BF16), wuq_ref[...])
        for h in range(MLA_HEADS):
            q_out[h] = (rope(q_all[:, LANES * h:LANES * (h + 1)]) * MLA_SCALE).astype(BF16)
        _, kvn = _rms(ckv, gkv_ref[...])
        kv_all = _dot(kvn.astype(BF16), wukv_ref[...])
        kpe = rope(_dot(xb, w_ref[3200:3328, :], NT))
        for h in range(MLA_HEADS):
            kv_h = kv_all[:, LANES * h:LANES * (h + 1)]
            k_out[h] = jnp.where(lane < 64, kv_h, kpe).astype(BF16)
            if h % 2:
                v = jnp.where(lane >= 64, kv_h, 0.0)
            else:
                v = jnp.where(lane < 64, pltpu.roll(kv_h, 64, 1), 0.0)
            v_out[h] = jnp.where(lane == ONES_LANE[h % 2], 1.0, v).astype(BF16)

    row = lambda w: pl.BlockSpec((tr, w), lambda i: (i, 0))
    full = lambda a: pl.BlockSpec(a.shape, lambda i: (0,) * a.ndim)
    head = pl.BlockSpec((MLA_HEADS, tr, LANES), lambda i: (0, i, 0))
    widths = (Q_LORA, KV_LORA, D_MODEL, 512, 512, 512)
    return pl.pallas_call(
        body, name="proj", grid=(seq // tr,),
        in_specs=[row(D_MODEL), full(w_in_p), full(gq), full(gkv), full(wuq_e), full(wukv)] + [row(LANES)] * 4,
        out_specs=[row(w) for w in widths] + [head] * 3,
        out_shape=[jax.ShapeDtypeStruct((seq, w), F32) for w in widths]
        + [jax.ShapeDtypeStruct((MLA_HEADS, seq, LANES), BF16)] * 3,
        compiler_params=_cp(("arbitrary",), VMEM_LIMIT),
    )(x, w_in_p, gq, gkv, wuq_e, wukv, ct, st, cd, sd)


def _mla_rot(t, lane):
    return jnp.where(lane < 80, pltpu.roll(t, 112, 1), pltpu.roll(t, 16, 1))


def _dil_rot(t, lane):
    return jnp.where(lane % 64 < 8, pltpu.roll(t, 120, 1), pltpu.roll(t, 8, 1))


def _rms(c, g):
    r = lax.rsqrt(jnp.mean(c * c, axis=-1, keepdims=True) + RMS_EPS)
    return r, c * r * g


def _mla_fwd(q, k, v):
    seq = q.shape[1]
    tq = 512
    nq = seq // tq

    def body(q_ref, k_ref, v_ref, o_ref, lse_ref, m_s, acc_s, s_buf):
        i = pl.program_id(1)
        row = lax.broadcasted_iota(jnp.int32, (tq, tq), 0)
        col = lax.broadcasted_iota(jnp.int32, (tq, tq), 1)
        lane = lax.broadcasted_iota(jnp.int32, (tq, LANES), 1)
        m_s[...] = jnp.full((2, tq, LANES), NEG, F32)
        acc_s[...] = jnp.zeros((2, tq, LANES), F32)

        def block(j):
            return pl.ds(pl.multiple_of(j * tq, tq), tq)

        def scores(hh, j):
            return _dot(q_ref[hh], k_ref[hh, block(j), :], NT)

        def consume(hh, j, s):
            m_prev = m_s[hh]
            m_new = jnp.maximum(m_prev, jnp.max(s, axis=1, keepdims=True))
            p = jnp.exp(s - m_new[:, :1])
            acc_s[hh] = jnp.exp(m_prev - m_new) * acc_s[hh] + _dot(p.astype(BF16), v_ref[hh, block(j), :])
            m_s[hh] = m_new

        for hh in range(2):
            s_buf[0, hh] = scores(hh, 0)

        def full_step(j, carry):
            slot = j & 1
            for hh in range(2):
                s = s_buf[slot, hh]
                s_buf[1 - slot, hh] = scores(hh, j + 1)
                consume(hh, j, s)
            return carry

        lax.fori_loop(0, i, full_step, 0)
        total = jnp.zeros((tq, LANES), F32)
        for hh in range(2):
            consume(hh, i, jnp.where(col <= row, s_buf[i & 1, hh], NEG))
            acc = acc_s[hh]
            l = acc[:, ONES_LANE[hh]:ONES_LANE[hh] + 1]
            mine = (lane >= 64) if hh else (lane < 64)
            total = total + jnp.where(mine, acc / l, 0.0)
            lse_ref[hh] = m_s[hh] + jnp.log(l)
        o_ref[...] = total

    kv_spec = pl.BlockSpec((2, seq, LANES), lambda p, i: (p, 0, 0))
    return pl.pallas_call(
        body, name="mla_fwd", grid=(MLA_HEADS // 2, nq),
        in_specs=[pl.BlockSpec((2, tq, LANES), lambda p, i: (p, i, 0)), kv_spec, kv_spec],
        out_specs=[pl.BlockSpec((tq, LANES), lambda p, i: (i, p)), pl.BlockSpec((2, tq, LANES), lambda p, i: (p, i, 0))],
        out_shape=[jax.ShapeDtypeStruct((seq, 4 * LANES), F32), jax.ShapeDtypeStruct((MLA_HEADS, seq, LANES), F32)],
        scratch_shapes=[pltpu.VMEM((2, tq, LANES), F32), pltpu.VMEM((2, tq, LANES), F32),
                        pltpu.VMEM((2, 2, tq, tq), F32)],
        compiler_params=_cp(("arbitrary", "arbitrary"), VMEM_LIMIT),
    )(q, k, v)


DIL_Q_FWD = 2 * BLOCK
DIL_Q_BWD = BLOCK


def _dil_tile_index(t, d, seq, nq):
    per_class = seq // (nq * d)
    shift = per_class.bit_length() - 1
    r = t >> shift
    n = t & (per_class - 1)
    start = r + (nq * d) * n
    prev = jnp.maximum(start - BLOCK * d, r)
    if d == 1:
        start = pl.multiple_of(start, nq)
        prev = pl.multiple_of(prev, BLOCK)
    return (n == 0).astype(jnp.int32), start, prev


def _dil_rows(start, d, size):
    return pl.ds(start, size) if d == 1 else pl.ds(start, size, stride=d)


def _dil_bias(nq):
    i = lax.broadcasted_iota(jnp.int32, (2 * nq, BLOCK + nq), 0) % nq
    j = lax.broadcasted_iota(jnp.int32, (2 * nq, BLOCK + nq), 1)
    band = (j >= i) & (j <= i + BLOCK)
    return jnp.where(band, 0.0, NEG), jnp.where(band & (j >= BLOCK), 0.0, NEG)


def _stack_heads(t, lane):
    return jnp.concatenate([jnp.where(lane < 64, t, 0.0), jnp.where(lane >= 64, t, 0.0)], axis=0)


def _unstack_heads(t, lane):
    nq = t.shape[0] // 2
    return jnp.where(lane < 64, t[:nq], t[nq:])


def _dil_fwd(qr, kr, vb):
    seq = qr.shape[0]
    nq = DIL_Q_FWD
    n_tiles = seq // nq
    assert seq % (nq * max(DIL_DILATIONS)) == 0

    def body(q_ref, k_ref, v_ref, o_ref, lse_ref, m_s, l_s, n_s, bias_s):
        lane = lax.broadcasted_iota(jnp.int32, (nq, LANES), 1)
        bias_s[0], bias_s[1] = _dil_bias(nq)
        for bi, d in enumerate(DIL_DILATIONS):

            def tile(t, carry, d=d, bi=bi):
                first, start, prev = _dil_tile_index(t, d, seq, nq)
                rows, prows = _dil_rows(start, d, nq), _dil_rows(prev, d, BLOCK)
                qst = _stack_heads(q_ref[rows, :], lane).astype(BF16)
                if seq == nq * d:
                    kcat, vcat = k_ref[rows, :].astype(BF16), v_ref[rows, :].astype(BF16)
                    s = _dot(qst, kcat, NT) + bias_s[1, :, BLOCK:]
                else:
                    kcat = jnp.concatenate([k_ref[prows, :], k_ref[rows, :]], axis=0).astype(BF16)
                    vcat = jnp.concatenate([v_ref[prows, :], v_ref[rows, :]], axis=0).astype(BF16)
                    s = _dot(qst, kcat, NT) + bias_s[first]
                m = jnp.max(s, axis=1, keepdims=True)
                p = jnp.exp(s - m)
                l2 = _unstack_heads(jnp.sum(p, axis=1, keepdims=True) + jnp.zeros((2 * nq, LANES), F32), lane)
                m2 = _unstack_heads(m + jnp.zeros((2 * nq, LANES), F32), lane)
                num2 = _unstack_heads(_dot(p.astype(BF16), vcat), lane)
                if bi == 0:
                    m_s[rows, :] = m2
                    l_s[rows, :] = l2
                    n_s[rows, :] = num2
                else:
                    m_old = m_s[rows, :]
                    m_new = jnp.maximum(m_old, m2)
                    a = jnp.exp(m_old - m_new)
                    b = jnp.exp(m2 - m_new)
                    m_s[rows, :] = m_new
                    l_s[rows, :] = a * l_s[rows, :] + b * l2
                    n_s[rows, :] = a * n_s[rows, :] + b * num2
                return carry

            lax.fori_loop(0, n_tiles, tile, 0, unroll=4)
        o_ref[...] = n_s[...] / l_s[...]
        lse_ref[...] = m_s[...] + jnp.log(l_s[...])

    col = lambda off: pl.BlockSpec((seq, LANES), lambda p: (0, p + off))
    return pl.pallas_call(
        body, name="dil_fwd", grid=(4,),
        in_specs=[col(0), col(0), col(0)],
        out_specs=[col(0), pl.BlockSpec((None, seq, LANES), lambda p: (p, 0, 0))],
        out_shape=[jax.ShapeDtypeStruct((seq, 4 * LANES), F32), jax.ShapeDtypeStruct((4, seq, LANES), F32)],
        scratch_shapes=[pltpu.VMEM((seq, LANES), F32)] * 3 + [pltpu.VMEM((2, 2 * nq, BLOCK + nq), F32)],
        compiler_params=_cp(("arbitrary",), VMEM_LIMIT),
    )(qr, kr, vb)


def _post(x, o_a, o_b, gates, w_out, ln_g, ln_b, target):
    seq = x.shape[0]
    tr = 512

    def body(x_ref, oa_ref, ob_ref, g_ref, w_ref, lg_ref, lb_ref, t_ref,
             dz_ref, do_ref, dg_ref, dw_ref, dlg_ref, dlb_ref, loss_ref):
        @pl.when(pl.program_id(0) == 0)
        def _():
            dw_ref[...] = jnp.zeros_like(dw_ref)
            dlg_ref[...] = jnp.zeros_like(dlg_ref)
            dlb_ref[...] = jnp.zeros_like(dlb_ref)
            loss_ref[...] = jnp.zeros_like(loss_ref)

        g = g_ref[...]
        sg = jax.nn.sigmoid(g)
        silu = g * sg
        o = jnp.concatenate([oa_ref[...], ob_ref[...]], axis=1)
        mixb = (o * silu).astype(BF16)
        w = w_ref[...]
        z = ALPHA * x_ref[...] + _dot(mixb, w)
        mu = jnp.mean(z, axis=-1, keepdims=True)
        zc = z - mu
        rstd = lax.rsqrt(jnp.mean(zc * zc, axis=-1, keepdims=True) + LN_EPS)
        xhat = zc * rstd
        lg = lg_ref[...]
        err = xhat * lg + lb_ref[...] - t_ref[...]
        loss_ref[...] += jnp.sum(err * err) * (0.5 / D_MODEL)
        dy = err * (1.0 / D_MODEL)
        dlg_ref[...] += jnp.sum(dy * xhat, axis=0, keepdims=True)
        dlb_ref[...] += jnp.sum(dy, axis=0, keepdims=True)
        dxh = dy * lg
        dz = rstd * (dxh - jnp.mean(dxh, axis=-1, keepdims=True) - xhat * jnp.mean(dxh * xhat, axis=-1, keepdims=True))
        dz_ref[...] = dz
        dzb = dz.astype(BF16)
        dmix = _dot(dzb, w, NT)
        do_ref[...] = dmix * silu
        dg_ref[...] = (dmix * o * (sg * (1.0 + g * (1.0 - sg)))).astype(BF16)
        dw_ref[...] += _dot(mixb, dzb, TN)

    row = lambda w: pl.BlockSpec((tr, w), lambda i: (i, 0))
    full = lambda s: pl.BlockSpec(s, lambda i: (0, 0))
    return pl.pallas_call(
        body, name="post", grid=(seq // tr,),
        in_specs=[row(D_MODEL), row(512), row(512), row(D_MODEL), full((D_MODEL, D_MODEL)), full((1, D_MODEL)),
                  full((1, D_MODEL)), row(D_MODEL)],
        out_specs=[row(D_MODEL), row(D_MODEL), row(D_MODEL), full((D_MODEL, D_MODEL)), full((1, D_MODEL)),
                   full((1, D_MODEL)), full((1, LANES))],
        out_shape=[jax.ShapeDtypeStruct((seq, D_MODEL), F32), jax.ShapeDtypeStruct((seq, D_MODEL), F32),
                   jax.ShapeDtypeStruct((seq, D_MODEL), BF16), jax.ShapeDtypeStruct((D_MODEL, D_MODEL), F32),
                   jax.ShapeDtypeStruct((1, D_MODEL), F32), jax.ShapeDtypeStruct((1, D_MODEL), F32),
                   jax.ShapeDtypeStruct((1, LANES), F32)],
        compiler_params=_cp(("arbitrary",), VMEM_LIMIT),
    )(x, o_a, o_b, gates, w_out, ln_g, ln_b, target)


def _mla_bwd(q, k, v, d_o, o, lse):
    seq = q.shape[1]
    tq = 512
    nq = seq // tq

    def body(q_ref, k_ref, v_ref, do_ref, o_ref, lse_ref, dq_ref, dk_ref, dv_ref, d_s, lse_s, dk_s, dv_s, v_s, kt_s, dqt_s):
        j = pl.program_id(1)
        lane = lax.broadcasted_iota(jnp.int32, (tq, LANES), 1)
        row = lax.broadcasted_iota(jnp.int32, (tq, tq), 0)
        col = lax.broadcasted_iota(jnp.int32, (tq, tq), 1)

        @pl.when(j == 0)
        def _():
            dqt_s[...] = jnp.zeros_like(dqt_s)

            def rowsum(i, carry):
                rows = pl.ds(pl.multiple_of(i * tq, tq), tq)
                prod = do_ref[rows, :] * o_ref[rows, :]
                for hh in range(2):
                    mine = (lane >= 64) if hh else (lane < 64)
                    total = jnp.sum(jnp.where(mine, prod, 0.0), axis=1, keepdims=True)
                    d_s[hh, i] = jnp.transpose(total + jnp.zeros((tq, LANES), F32))[:8]
                    lse_s[hh, i] = jnp.transpose(lse_ref[hh, rows, :])[:8]
                return carry

            lax.fori_loop(0, nq, rowsum, 0)

        dk_s[...] = jnp.zeros_like(dk_s)
        dv_s[...] = jnp.zeros_like(dv_s)
        for hh in range(2):
            v_s[hh] = jnp.where(lane == ONES_LANE[hh], 0.0, v_ref[hh].astype(F32)).astype(BF16)
            kt_s[hh] = jnp.transpose(k_ref[hh].astype(F32)).astype(BF16)

        def step(i, masked):
            rows = pl.ds(pl.multiple_of(i * tq, tq), tq)
            dob = do_ref[rows, :].astype(BF16)
            for hh in range(2):
                qb, kb, vb = q_ref[hh, rows, :], k_ref[hh], v_s[hh]
                p = jnp.exp(_dot(kb, qb, NT) - lse_s[hh, i][:1])
                if masked:
                    p = jnp.where(row <= col, p, 0.0)
                dv_s[hh] += _dot(p.astype(BF16), dob)
                ds = (p * (_dot(vb, dob, NT) - d_s[hh, i][:1])).astype(BF16)
                dk_s[hh] += _dot(ds, qb)
                dqt_s[hh, i] += _dot(kt_s[hh], ds)

        def full_step(i, carry):
            step(i, False)
            return carry

        step(j, True)
        lax.fori_loop(j + 1, nq, full_step, 0)
        dk_ref[...] = dk_s[...]
        dv_ref[...] = dv_s[...]

        @pl.when(j == nq - 1)
        def _():
            def untranspose(i, carry):
                rows = pl.ds(pl.multiple_of(i * tq, tq), tq)
                for hh in range(2):
                    dq_ref[hh, rows, :] = jnp.transpose(dqt_s[hh, i])
                return carry

            lax.fori_loop(0, nq, untranspose, 0)

    whole = pl.BlockSpec((2, seq, LANES), lambda p, j: (p, 0, 0))
    blk = pl.BlockSpec((2, tq, LANES), lambda p, j: (p, j, 0))
    pair = pl.BlockSpec((seq, LANES), lambda p, j: (0, p))
    shape = jax.ShapeDtypeStruct((MLA_HEADS, seq, LANES), F32)
    return pl.pallas_call(
        body, name="mla_bwd", grid=(MLA_HEADS // 2, nq),
        in_specs=[whole, blk, blk, pair, pair, whole],
        out_specs=[whole, blk, blk], out_shape=[shape] * 3,
        scratch_shapes=[pltpu.VMEM((2, nq, 8, tq), F32), pltpu.VMEM((2, nq, 8, tq), F32),
                        pltpu.VMEM((2, tq, LANES), F32), pltpu.VMEM((2, tq, LANES), F32),
                        pltpu.VMEM((2, tq, LANES), BF16), pltpu.VMEM((2, LANES, tq), BF16),
                        pltpu.VMEM((2, nq, LANES, tq), F32)],
        compiler_params=_cp(("arbitrary", "arbitrary"), VMEM_LIMIT),
    )(q, k, v, d_o, o, lse)


def _dil_bwd(qr, kr, vb, d_o, o, lse):
    seq = qr.shape[0]
    nq = DIL_Q_BWD
    n_tiles = seq // nq
    chunk = 512

    def body(q_ref, k_ref, v_ref, do_ref, o_ref, lse_ref, dq_ref, dk_ref, dv_ref, d_s, dq_s, dk_s, dv_s, bias_s):
        lane = lax.broadcasted_iota(jnp.int32, (nq, LANES), 1)
        lanec = lax.broadcasted_iota(jnp.int32, (chunk, LANES), 1)
        bias_s[0], bias_s[1] = [b[:nq] for b in _dil_bias(nq)]

        def rowsum(i, carry):
            rows = pl.ds(pl.multiple_of(i * chunk, chunk), chunk)
            prod = do_ref[rows, :] * o_ref[rows, :]
            lo = jnp.sum(jnp.where(lanec < 64, prod, 0.0), axis=1, keepdims=True)
            hi = jnp.sum(jnp.where(lanec >= 64, prod, 0.0), axis=1, keepdims=True)
            d_s[rows, :] = jnp.where(lanec < 64, lo, hi)
            return carry

        lax.fori_loop(0, seq // chunk, rowsum, 0)
        dq_s[...] = jnp.zeros_like(dq_s)
        dk_s[...] = jnp.zeros_like(dk_s)
        dv_s[...] = jnp.zeros_like(dv_s)
        for d in DIL_DILATIONS:

            def tile(start, prev, first, d=d):
                rows = _dil_rows(start, d, nq)
                q_t, do_t = q_ref[rows, :], do_ref[rows, :]
                lse_t, d_t = lse_ref[rows, :], d_s[rows, :]
                if prev is None:
                    kcat, vcat = k_ref[rows, :].astype(BF16), v_ref[rows, :].astype(BF16)
                    bias = bias_s[1, :, BLOCK:]
                else:
                    prows = _dil_rows(prev, d, BLOCK)
                    kcat = jnp.concatenate([k_ref[prows, :], k_ref[rows, :]], axis=0).astype(BF16)
                    vcat = jnp.concatenate([v_ref[prows, :], v_ref[rows, :]], axis=0).astype(BF16)
                    bias = bias_s[first]
                dq_t = jnp.zeros((nq, LANES), F32)
                dkcat = jnp.zeros((kcat.shape[0], LANES), F32)
                dvcat = jnp.zeros((kcat.shape[0], LANES), F32)
                for hh in range(2):
                    mine = (lane >= 64) if hh else (lane < 64)
                    c0 = 64 * hh
                    qh = jnp.where(mine, q_t, 0.0).astype(BF16)
                    doh = jnp.where(mine, do_t, 0.0).astype(BF16)
                    p = jnp.exp(_dot(qh, kcat, NT) + bias - lse_t[:, c0:c0 + 1])
                    dvcat = dvcat + _dot(p.astype(BF16), doh, TN)
                    dp = _dot(doh, vcat, NT)
                    ds = (p * (dp - d_t[:, c0:c0 + 1])).astype(BF16)
                    dq_t = dq_t + jnp.where(mine, _dot(ds, kcat), 0.0)
                    dkcat = dkcat + _dot(ds, qh, TN)
                dq_s[rows, :] += dq_t
                if prev is not None:
                    dk_s[prows, :] += dkcat[:BLOCK]
                    dv_s[prows, :] += dvcat[:BLOCK]
                dk_s[rows, :] += dkcat[-nq:]
                dv_s[rows, :] += dvcat[-nq:]

            if seq == 2 * nq * d:

                def class_tiles(r, carry, d=d):
                    tile(r, None, 1)
                    tile(r + nq * d, r, 0)
                    return carry

                lax.fori_loop(0, d, class_tiles, 0, unroll=4)
            else:

                def any_tile(t, carry, d=d):
                    first, start, prev = _dil_tile_index(t, d, seq, nq)
                    tile(start, prev, first)
                    return carry

                lax.fori_loop(0, n_tiles, any_tile, 0, unroll=8)
        dq_ref[...] = dq_s[...].astype(BF16)
        dk_ref[...] = dk_s[...].astype(BF16)
        dv_ref[...] = dv_s[...].astype(BF16)

    col = lambda off: pl.BlockSpec((seq, LANES), lambda p: (0, p + off))
    shape = jax.ShapeDtypeStruct((seq, 4 * LANES), BF16)
    return pl.pallas_call(
        body, name="dil_bwd", grid=(4,),
        in_specs=[col(0), col(0), col(0), col(4), col(0), pl.BlockSpec((None, seq, LANES), lambda p: (p, 0, 0))],
        out_specs=[col(0)] * 3, out_shape=[shape] * 3,
        scratch_shapes=[pltpu.VMEM((seq, LANES), F32)] * 4 + [pltpu.VMEM((2, nq, BLOCK + nq), F32)],
        compiler_params=_cp(("arbitrary",), VMEM_LIMIT),
    )(qr, kr, vb, d_o, o, lse)


def _in_bwd(dz, cq, ckv, gq, gkv, wuq_e, wukv, ct, st, dq, dk, dv, dgates, dqr, dkr, dvb, cd, sd, w_in_p):
    seq = dz.shape[0]
    tr = 512

    def body(dz_ref, cq_ref, ckv_ref, gq_ref, gkv_ref, wuq_ref, wukv_ref, ct_ref, st_ref, dq_ref, dk_ref, dv_ref,
             dg_ref, dqr_ref, dkr_ref, dvb_ref, cd_ref, sd_ref, w_ref,
             gx_ref, dh_ref, dwuq_ref, dwukv_ref, dgq_ref, dgkv_ref):
        @pl.when(pl.program_id(0) == 0)
        def _():
            dwuq_ref[...] = jnp.zeros_like(dwuq_ref)
            dwukv_ref[...] = jnp.zeros_like(dwukv_ref)
            dgq_ref[...] = jnp.zeros_like(dgq_ref)
            dgkv_ref[...] = jnp.zeros_like(dgkv_ref)

        lane = lax.broadcasted_iota(jnp.int32, (tr, LANES), 1)
        rope_lanes = jnp.logical_and(lane >= 64, lane < 96)
        ct_, st_ = ct_ref[...], st_ref[...]

        def mla_rope_t(g):
            return ct_ * g + jnp.where(rope_lanes, _mla_rot(st_ * g, lane), 0.0)

        def norm_bwd(c, g, dn, dg_ref):
            r, _ = _rms(c, g)
            u = dn * g
            dg_ref[...] += jnp.sum(dn * c * r, axis=0, keepdims=True)
            return r * u - c * (r * r * r) * jnp.mean(u * c, axis=-1, keepdims=True)

        c, g = cq_ref[...], gq_ref[...]
        _, qn = _rms(c, g)
        dq_all = jnp.concatenate([mla_rope_t(dq_ref[h] * MLA_SCALE) for h in range(MLA_HEADS)], axis=1).astype(BF16)
        dwuq_ref[...] += _dot(qn.astype(BF16), dq_all, TN)
        dcq = norm_bwd(c, g, _dot(dq_all, wuq_ref[...], NT), dgq_ref).astype(BF16)

        c, g = ckv_ref[...], gkv_ref[...]
        _, kvn = _rms(c, g)
        dkpe = jnp.zeros((tr, LANES), F32)
        parts = []
        for h in range(MLA_HEADS):
            dk_h, dv_h = dk_ref[h], dv_ref[h]
            if h % 2 == 0:
                dv_h = pltpu.roll(dv_h, 64, 1)
            parts.append(jnp.where(lane < 64, dk_h, dv_h))
            dkpe = dkpe + jnp.where(rope_lanes, dk_h, 0.0)
        dkv_all = jnp.concatenate(parts, axis=1).astype(BF16)
        dwukv_ref[...] += _dot(kvn.astype(BF16), dkv_all, TN)
        dckv = norm_bwd(c, g, _dot(dkv_all, wukv_ref[...], NT), dgkv_ref).astype(BF16)
        dkrope = mla_rope_t(dkpe).astype(BF16)

        rot_lanes = lane % 64 < DIL_ROT
        cd_, sd_ = cd_ref[...], sd_ref[...]

        def dil_rope_t(g):
            return cd_ * g + jnp.where(rot_lanes, _dil_rot(sd_ * g, lane), 0.0)

        dqb = [dil_rope_t(dqr_ref[:, LANES * p:LANES * (p + 1)].astype(F32) * DIL_SCALE).astype(BF16) for p in range(4)]
        dkb = [dil_rope_t(dkr_ref[:, LANES * p:LANES * (p + 1)].astype(F32)).astype(BF16) for p in range(4)]
        dh = jnp.concatenate([dcq, dckv, dg_ref[...]] + dqb + dkb + [dvb_ref[...], dkrope], axis=1)
        dh_ref[...] = dh
        gx_ref[...] = ALPHA * dz_ref[...] + _dot(dh, w_ref[...])

    row = lambda w: pl.BlockSpec((tr, w), lambda i: (i, 0))
    full = lambda a: pl.BlockSpec(a.shape, lambda i: (0,) * a.ndim)
    head = pl.BlockSpec((MLA_HEADS, tr, LANES), lambda i: (0, i, 0))
    return pl.pallas_call(
        body, name="in_bwd", grid=(seq // tr,),
        in_specs=[row(D_MODEL), row(Q_LORA), row(KV_LORA), full(gq), full(gkv), full(wuq_e), full(wukv), row(LANES),
                  row(LANES), head, head, head, row(D_MODEL), row(512), row(512), row(512), row(LANES), row(LANES),
                  full(w_in_p)],
        out_specs=[row(D_MODEL), row(IN_WIDTH_PAD), full(wuq_e), full(wukv), full(gq), full(gkv)],
        out_shape=[jax.ShapeDtypeStruct((seq, D_MODEL), F32), jax.ShapeDtypeStruct((seq, IN_WIDTH_PAD), BF16),
                   jax.ShapeDtypeStruct(wuq_e.shape, F32), jax.ShapeDtypeStruct(wukv.shape, F32),
                   jax.ShapeDtypeStruct(gq.shape, F32), jax.ShapeDtypeStruct(gkv.shape, F32)],
        compiler_params=_cp(("arbitrary",), VMEM_LIMIT),
    )(dz, cq, ckv, gq, gkv, wuq_e, wukv, ct, st, dq, dk, dv, dgates, dqr, dkr, dvb, cd, sd, w_in_p)


def _dw_in(x, dh):
    seq = dh.shape[0]
    tk = 512
    tn = IN_WIDTH_PAD // 2

    def body(x_ref, dh_ref, o_ref):
        @pl.when(pl.program_id(1) == 0)
        def _():
            o_ref[...] = jnp.zeros_like(o_ref)

        o_ref[...] += _dot(dh_ref[...], x_ref[...].astype(BF16), TN)

    return pl.pallas_call(
        body, name="dw_in", grid=(2, seq // tk),
        in_specs=[pl.BlockSpec((tk, D_MODEL), lambda n, k: (k, 0)), pl.BlockSpec((tk, tn), lambda n, k: (k, n))],
        out_specs=pl.BlockSpec((tn, D_MODEL), lambda n, k: (n, 0)),
        out_shape=jax.ShapeDtypeStruct((IN_WIDTH_PAD, D_MODEL), F32),
        compiler_params=_cp(("arbitrary", "arbitrary"), VMEM_LIMIT),
    )(x, dh)


def _adam_update(w, g, m, v):
    nm = ADAM_B1 * m + (1.0 - ADAM_B1) * g
    nv = ADAM_B2 * v + (1.0 - ADAM_B2) * jnp.square(g)
    m_hat = nm / (1.0 - ADAM_B1 ** ADAM_STEP)
    v_hat = nv / (1.0 - ADAM_B2 ** ADAM_STEP)
    return -ADAM_LR * (m_hat / (jnp.sqrt(v_hat) + ADAM_EPS) + ADAM_WD * w), nm, nv


def _adamw(w, g, m, v, name):
    rows, cols = w.shape
    tc = 256 if cols % 256 == 0 and rows * cols > 2 ** 18 else cols

    def body(w_ref, g_ref, m_ref, v_ref, d_ref, nm_ref, nv_ref):
        d_ref[...], nm_ref[...], nv_ref[...] = _adam_update(w_ref[...], g_ref[...], m_ref[...], v_ref[...])

    spec = pl.BlockSpec((rows, tc), lambda i: (0, i))
    return pl.pallas_call(
        body, name=name, grid=(cols // tc,), in_specs=[spec] * 4, out_specs=[spec] * 3,
        out_shape=[jax.ShapeDtypeStruct(w.shape, F32)] * 3, compiler_params=_cp(("arbitrary",)),
    )(w, g, m, v)


def _adamw_vectors(small_sum, ws, ms, vs):
    k = len(ws)
    sizes = [w.shape[-1] for w in ws]

    def body(s_ref, *refs):
        ins, outs = refs[:3 * k], refs[3 * k:]
        for i, size in enumerate(sizes):
            g = s_ref[i:i + 1, 0:size]
            outs[i][...] = g
            outs[k + i][...], outs[2 * k + i][...], outs[3 * k + i][...] = _adam_update(
                ins[i][...], g, ins[k + i][...], ins[2 * k + i][...])

    out = pl.pallas_call(
        body, name="adamw_vectors", out_shape=[jax.ShapeDtypeStruct((1, size), F32) for size in sizes] * 4,
    )(small_sum, *[a.reshape(1, -1) for a in list(ws) + list(ms) + list(vs)])
    return [[a.reshape(-1) for a in out[k * j:k * (j + 1)]] for j in range(4)]


def _local_step(x2, target, w_in_p, w_uq_f, wukv_f, w_out_f, q_norm_g, kv_norm_g, ln_g, ln_b):
    seq = x2.shape[0]
    wuq_e = jnp.pad(w_uq_f.reshape(Q_LORA, MLA_HEADS, 96), ((0, 0), (0, 0), (0, 32))).reshape(Q_LORA, MLA_HEADS * LANES)
    ct, st, cd, sd = _rope_tables(seq)
    gq = q_norm_g.reshape(1, Q_LORA)
    gkv = kv_norm_g.reshape(1, KV_LORA)

    cq, ckv, gates, qr, krot, vb, q_e, k_e, v_e = _proj(x2, w_in_p, gq, gkv, wuq_e, wukv_f, ct, st, cd, sd)
    o_a, lse_a = _mla_fwd(q_e, k_e, v_e)
    o_b, lse_b = _dil_fwd(qr, krot, vb)

    dz, d_o, d_gates, dw_out, dln_g, dln_b, loss_part = _post(
        x2, o_a, o_b, gates, w_out_f, ln_g.reshape(1, D_MODEL), ln_b.reshape(1, D_MODEL), target)
    dq_e, dk_e, dv_e = _mla_bwd(q_e, k_e, v_e, d_o, o_a, lse_a)
    dqr, dkr, dvb = _dil_bwd(qr, krot, vb, d_o, o_b, lse_b)
    grad_x, dh, dwuq_e, dwukv, dgq, dgkv = _in_bwd(
        dz, cq, ckv, gq, gkv, wuq_e, wukv_f, ct, st, dq_e, dk_e, dv_e, d_gates, dqr, dkr, dvb, cd, sd, w_in_p)
    dw_in = _dw_in(x2, dh)
    dw_uq = dwuq_e.reshape(Q_LORA, MLA_HEADS, LANES)[:, :, :96].reshape(Q_LORA, MLA_HEADS * 96)
    return loss_part, grad_x, dw_in, dw_uq, dwukv, dw_out, dgq, dgkv, dln_g, dln_b


def kernel(x, w_in, q_norm_g, kv_norm_g, w_uq, w_ukv, w_out, ln_g, ln_b, loss_target, m_w_in, m_q_norm_g, m_kv_norm_g, m_w_uq, m_w_ukv, m_w_out, m_ln_g, m_ln_b, v_w_in, v_q_norm_g, v_kv_norm_g, v_w_uq, v_w_ukv, v_w_out, v_ln_g, v_ln_b):
    seq = x.shape[1]
    x2 = x.reshape(seq, D_MODEL)
    target = loss_target.reshape(seq, D_MODEL)

    g_w_in, g_w_uq, g_w_ukv, g_w_out = _all_gather_weights([w_in.T, w_uq, w_ukv, w_out])
    by_cols = lambda g: jnp.concatenate([g[j] for j in range(N_SHARD)], axis=1)
    loss_part, grad_x, dw_in, dw_uq, dwukv, dw_out, dgq, dgkv, dln_g, dln_b = _local_step(
        x2, target, _permute_w_in_t_shards(g_w_in), by_cols(g_w_uq), by_cols(g_w_ukv), g_w_out.reshape(D_MODEL, D_MODEL),
        q_norm_g, kv_norm_g, ln_g, ln_b)

    to_shards = lambda d: d.reshape(d.shape[0], N_SHARD, d.shape[1] // N_SHARD).transpose(1, 0, 2)
    grads = [dw_in, to_shards(dw_uq), to_shards(dwukv), dw_out.reshape(N_SHARD, 256, D_MODEL)]
    *chip_sums, small_sum = _reduce_over_sibling(grads, [dgq, dgkv, dln_g, dln_b, loss_part])
    g_in_t, g_uq, g_ukv, g_out = _reduce_over_chips(chip_sums)
    g_in = g_in_t.T
    loss = small_sum[4, 0]

    big = [[o.T for o in _adamw(w.T, g.T, m.T, v.T, name)] for w, g, m, v, name in (
        (w_in, g_in, m_w_in, v_w_in, "adamw_w_in"), (w_uq, g_uq, m_w_uq, v_w_uq, "adamw_w_uq"))]
    big += [_adamw(w, g, m, v, name) for w, g, m, v, name in (
        (w_ukv, g_ukv, m_w_ukv, v_w_ukv, "adamw_w_ukv"), (w_out, g_out, m_w_out, v_w_out, "adamw_w_out"))]
    vec_g, vec_delta, vec_m, vec_v = _adamw_vectors(
        small_sum, [q_norm_g, kv_norm_g, ln_g, ln_b], [m_q_norm_g, m_kv_norm_g, m_ln_g, m_ln_b],
        [v_q_norm_g, v_kv_norm_g, v_ln_g, v_ln_b])

    def ordered(bigs, vecs):
        return [bigs[0], vecs[0], vecs[1], bigs[1], bigs[2], bigs[3], vecs[2], vecs[3]]

    grads_out = ordered([g_in, g_uq, g_ukv, g_out], vec_g)
    deltas = ordered([b[0] for b in big], vec_delta)
    new_m = ordered([b[1] for b in big], vec_m)
    new_v = ordered([b[2] for b in big], vec_v)
    return (loss, grad_x.reshape(x.shape), *grads_out, *deltas, *new_m, *new_v)
```

```python
import functools

import jax
import jax.numpy as jnp
import numpy as np
from jax import lax
from jax.experimental import pallas as pl
from jax.experimental.pallas import tpu as pltpu

F32 = jnp.float32
BF16 = jnp.bfloat16

D_MODEL = 1024
ROPE_THETA = 500000.0
BLOCK = 128
NEG = -1e30
RMS_EPS = 1e-6
LN_EPS = 1e-5

MLA_HEADS = 8
MLA_NOPE = 64
MLA_ROPE = 32
Q_LORA = 384
KV_LORA = 256
DIL_HEADS = 8
DIL_HEAD_DIM = 64
DIL_ROT = 16
DIL_DILATIONS = (1, 4, 16)
IN_WIDTH = 3232
IN_WIDTH_PAD = 3328
ONES_LANE = (64, 0)
MLA_SCALE = (MLA_NOPE + MLA_ROPE) ** -0.5
DIL_SCALE = DIL_HEAD_DIM ** -0.5
ALPHA = 2.0 ** 0.25

ADAM_LR = 0.001
ADAM_B1 = 0.9
ADAM_B2 = 0.999
ADAM_EPS = 1e-08
ADAM_WD = 0.01
ADAM_STEP = 10

N_SHARD = 4
GRAD_SHAPES = ((808, 1024), (384, 192), (256, 256), (256, 1024))
GRAD_SPLIT_COLS = (True, False, False, False)
ROW_CHUNK = 64
LANES = 128
VMEM_LIMIT = 56 * 1024 * 1024
MESH = pl.DeviceIdType.MESH

NT = (((1,), (1,)), ((), ()))
TN = (((0,), (0,)), ((), ()))


def _cp(sem=None, vmem=None):
    return pltpu.CompilerParams(dimension_semantics=sem, vmem_limit_bytes=vmem)


def _dot(a, b, dims=None):
    if dims is None:
        return jnp.dot(a, b, preferred_element_type=F32)
    return lax.dot_general(a, b, dims, preferred_element_type=F32)


def _rope_tables(seq):
    f32 = np.float32
    pos = np.arange(seq, dtype=f32)[:, None]
    one, zero = np.ones((seq, 64), f32), np.zeros((seq, 64), f32)

    def cos_sin(dim):
        inv = np.power(f32(ROPE_THETA), -np.arange(0, dim, 2, dtype=f32) / f32(dim)).astype(f32)
        ang = (pos * inv[None, :]).astype(f32)
        return np.cos(ang).astype(f32), np.sin(ang).astype(f32)

    cos, sin = cos_sin(MLA_ROPE)
    ct = np.concatenate([one, cos, cos, zero[:, :32]], axis=1)
    st = np.concatenate([zero, -sin, sin, zero[:, :32]], axis=1)
    cos, sin = cos_sin(DIL_ROT)
    cd = np.concatenate([cos, cos, one[:, :48]], axis=1)
    sd = np.concatenate([-sin, sin, zero[:, :48]], axis=1)
    return tuple(jnp.asarray(t) for t in (ct, st, np.tile(cd, (1, 2)), np.tile(sd, (1, 2))))


W_IN_ORDER = ((0, 640), (672, 1184), (2720, 3232), (1184, 2720), None, (640, 672))


def _permute_w_in_t(w_t):
    z = jnp.zeros((64, w_t.shape[1]), w_t.dtype)
    parts = [z if r is None else w_t[r[0]:r[1]] for r in W_IN_ORDER]
    return jnp.concatenate(parts + [z[:32]], axis=0)


def _permute_w_in_t_shards(g):
    z = jnp.zeros((64, g.shape[2]), g.dtype)
    parts = [g[k, src:src + rows] for k, src, _, rows in _w_in_row_pieces()]
    at = [dst for _, _, dst, _ in _w_in_row_pieces()]
    assert at == sorted(at) and at[-1] == 3264
    return jnp.concatenate(parts[:-1] + [z, parts[-1], z[:32]], axis=0)


def _w_in_row_pieces():
    width = GRAD_SHAPES[0][0]
    pieces, at = [], 0
    for r in W_IN_ORDER:
        if r is None:
            at += 64
            continue
        for k in range(N_SHARD):
            lo, hi = max(r[0], width * k), min(r[1], width * (k + 1))
            if lo < hi:
                pieces.append((k, lo - width * k, at + lo - r[0], hi - lo))
        at += r[1] - r[0]
    return pieces


def _position():
    return lax.axis_index("x"), lax.axis_index("y"), lax.axis_index("c")


def _all_gather_weights(shards):
    n = len(shards)

    def body(*refs):
        ins, outs = refs[:n], refs[n:2 * n]
        send_sems, recv_sems = refs[2 * n:]
        x, y, c = _position()
        me = 2 * x + y
        chips = [(1 - x, y), (x, 1 - y), (1 - x, 1 - y)]
        for a in range(n):
            rows, cols = GRAD_SHAPES[a]
            if GRAD_SPLIT_COLS[a]:
                blocks = [(slice(None), slice(c0, c0 + LANES)) for c0 in range(0, cols, LANES)]
            else:
                blocks = [(slice(r0, r0 + ROW_CHUNK), slice(None)) for r0 in range(0, rows, ROW_CHUNK)]
            for blk in blocks:
                outs[a][(me,) + blk] = ins[a][blk].astype(BF16)

        def copy(k, a, slot, part, to):
            ref = outs[a].at[(slot,) + part]
            return pltpu.make_async_remote_copy(
                src_ref=ref, dst_ref=ref, send_sem=send_sems.at[k * n + a], recv_sem=recv_sems.at[k * n + a],
                device_id=to, device_id_type=MESH)

        half = [_grad_half(a, c) for a in range(n)]
        other = [_grad_half(a, 1 - c) for a in range(n)]
        first = [copy(k, a, me, half[a], (px, py, c)) for k, (px, py) in enumerate(chips) for a in range(n)]
        for cp in first:
            cp.start()
        passed = []
        for k, (px, py) in enumerate(chips):
            for a in range(n):
                copy(k, a, 2 * px + py, half[a], (x, y, c)).wait_recv()
                cp = copy(3 + k, a, 2 * px + py, half[a], (x, y, 1 - c))
                cp.start()
                passed.append(cp)
        for k, (px, py) in enumerate(chips):
            for a in range(n):
                copy(3 + k, a, 2 * px + py, other[a], (x, y, c)).wait_recv()
        for cp in first + passed:
            cp.wait_send()

    vmem = pl.BlockSpec(memory_space=pltpu.VMEM)
    return pl.pallas_call(
        body, name="all_gather_weights",
        out_shape=[jax.ShapeDtypeStruct((N_SHARD,) + s, BF16) for s in GRAD_SHAPES],
        in_specs=[vmem] * n, out_specs=[vmem] * n,
        scratch_shapes=[pltpu.SemaphoreType.DMA((6 * n,)), pltpu.SemaphoreType.DMA((6 * n,))],
        compiler_params=_cp(None, VMEM_LIMIT),
    )(*shards)


def _grad_half_shape(a):
    rows, cols = GRAD_SHAPES[a]
    return (rows, cols // 2) if GRAD_SPLIT_COLS[a] else (rows // 2, cols)


def _grad_half(a, c):
    rows, cols = GRAD_SHAPES[a]
    if GRAD_SPLIT_COLS[a]:
        return slice(None), pl.ds(pl.multiple_of(c * (cols // 2), LANES), cols // 2)
    return pl.ds(pl.multiple_of(c * (rows // 2), ROW_CHUNK), rows // 2), slice(None)


def _grad_chunks(a, c):
    rows, cols = GRAD_SHAPES[a]
    if GRAD_SPLIT_COLS[a]:
        return [((slice(None), pl.ds(c0, LANES)),
                 (slice(None), pl.ds(pl.multiple_of(c * (cols // 2) + c0, LANES), LANES)))
                for c0 in range(0, cols // 2, LANES)]
    return [((pl.ds(r0, ROW_CHUNK), slice(None)),
             (pl.ds(pl.multiple_of(c * (rows // 2) + r0, ROW_CHUNK), ROW_CHUNK), slice(None)))
            for r0 in range(0, rows // 2, ROW_CHUNK)]


def _reduce_over_sibling(grads, small_rows):
    n = len(grads)
    n_small = len(small_rows)
    pieces = _w_in_row_pieces()

    def body(*refs):
        g_hbm, rows_in = refs[:n], refs[n:n + n_small]
        sums, small_sum = refs[n + n_small:2 * n + n_small], refs[2 * n + n_small]
        scratch = refs[2 * n + n_small + 1:]
        stage, got = scratch[:n], scratch[n:2 * n]
        sm, smalls, send_sems, recv_sems, local_sems = scratch[2 * n:]
        x, y, c = _position()
        me = 4 * x + 2 * y + c
        sm[...] = jnp.zeros_like(sm)
        for i, row in enumerate(rows_in):
            sm[i:i + 1, 0:row.shape[1]] = row[...]
        loads = [[pltpu.make_async_copy(g_hbm[0].at[pl.ds(src, rows)], stage[0].at[k, pl.ds(dst, rows)],
                                        local_sems.at[n + i])
                  for i, (k, dst, src, rows) in enumerate(pieces)]]
        loads += [[pltpu.make_async_copy(g_hbm[a], stage[a], local_sems.at[a])] for a in range(1, n)]
        for group in loads:
            for ld in group:
                ld.start()
        smalls[me] = sm[...]
        sends = []
        for rel in range(1, 8):
            px = 1 - x if rel // 4 else x
            py = 1 - y if (rel // 2) % 2 else y
            pc = 1 - c if rel % 2 else c
            cp = pltpu.make_async_remote_copy(
                src_ref=sm, dst_ref=smalls.at[me], send_sem=send_sems.at[n + rel], recv_sem=recv_sems.at[n + rel],
                device_id=(px, py, pc), device_id_type=MESH)
            cp.start()
            sends.append((cp, 4 * px + 2 * py + pc))
        swaps = []
        for a in range(n):
            for ld in loads[a]:
                ld.wait()
            cp = pltpu.make_async_remote_copy(
                src_ref=stage[a].at[(slice(None),) + _grad_half(a, 1 - c)], dst_ref=got[a], send_sem=send_sems.at[a], recv_sem=recv_sems.at[a],
                device_id=(x, y, 1 - c), device_id_type=MESH)
            cp.start()
            swaps.append(cp)
        for a in range(n):
            swaps[a].wait_recv()
            for k in range(N_SHARD):
                for in_half, in_whole in _grad_chunks(a, c):
                    pair = stage[a][(k,) + in_whole] + got[a][(k,) + in_half]
                    sums[a][(k,) + in_half] = pair.astype(BF16)
        for rel, (cp, peer) in enumerate(sends, start=1):
            pltpu.make_async_remote_copy(
                src_ref=sm, dst_ref=smalls.at[peer], send_sem=send_sems.at[n + rel], recv_sem=recv_sems.at[n + rel],
                device_id=(x, y, c), device_id_type=MESH).wait_recv()
        total = smalls[0]
        for dev in range(1, 8):
            total = total + smalls[dev]
        small_sum[...] = total
        for cp in swaps:
            cp.wait_send()
        for cp, _ in sends:
            cp.wait_send()

    vmem = pl.BlockSpec(memory_space=pltpu.VMEM)
    half = [(N_SHARD,) + _grad_half_shape(a) for a in range(n)]
    return pl.pallas_call(
        body, name="reduce_over_sibling",
        out_shape=[jax.ShapeDtypeStruct(s, BF16) for s in half] + [jax.ShapeDtypeStruct((8, D_MODEL), F32)],
        in_specs=[pl.BlockSpec(memory_space=pl.ANY)] * n + [vmem] * n_small, out_specs=[vmem] * (n + 1),
        scratch_shapes=[pltpu.VMEM((N_SHARD,) + s, F32) for s in GRAD_SHAPES] + [pltpu.VMEM(s, F32) for s in half]
        + [pltpu.VMEM((8, D_MODEL), F32), pltpu.VMEM((8, 8, D_MODEL), F32),
           pltpu.SemaphoreType.DMA((n + 8,)), pltpu.SemaphoreType.DMA((n + 8,)),
           pltpu.SemaphoreType.DMA((n + len(pieces),))],
        compiler_params=_cp(None, VMEM_LIMIT),
    )(*grads, *small_rows)


def _reduce_over_chips(sums):
    n = len(sums)

    def body(*refs):
        h, outs, got = refs[:n], refs[n:2 * n], refs[2 * n:3 * n]
        send_sems, recv_sems = refs[3 * n:]
        x, y, c = _position()
        me = 2 * x + y
        chips = [(1 - x, y), (x, 1 - y), (1 - x, 1 - y)]
        sends = []
        for k, (px, py) in enumerate(chips):
            for a in range(n):
                cp = pltpu.make_async_remote_copy(
                    src_ref=h[a].at[2 * px + py], dst_ref=got[a].at[k], send_sem=send_sems.at[k * n + a],
                    recv_sem=recv_sems.at[k * n + a], device_id=(px, py, c), device_id_type=MESH)
                cp.start()
                sends.append(cp)
        for cp in sends:
            cp.wait_recv()
        joins = []
        for a in range(n):
            for in_half, in_whole in _grad_chunks(a, c):
                total = h[a][(me,) + in_half].astype(F32)
                for k in range(3):
                    total = total + got[a][(k,) + in_half].astype(F32)
                outs[a][in_whole] = total
            half = outs[a].at[_grad_half(a, c)]
            cp = pltpu.make_async_remote_copy(
                src_ref=half, dst_ref=half, send_sem=send_sems.at[3 * n + a],
                recv_sem=recv_sems.at[3 * n + a], device_id=(x, y, 1 - c), device_id_type=MESH)
            cp.start()
            joins.append(cp)
        for a in range(n):
            other = outs[a].at[_grad_half(a, 1 - c)]
            pltpu.make_async_remote_copy(
                src_ref=other, dst_ref=other, send_sem=send_sems.at[3 * n + a],
                recv_sem=recv_sems.at[3 * n + a], device_id=(x, y, c), device_id_type=MESH).wait_recv()
        for cp in sends + joins:
            cp.wait_send()

    vmem = pl.BlockSpec(memory_space=pltpu.VMEM)
    return pl.pallas_call(
        body, name="reduce_over_chips",
        out_shape=[jax.ShapeDtypeStruct(s, F32) for s in GRAD_SHAPES],
        in_specs=[vmem] * n, out_specs=[vmem] * n,
        scratch_shapes=[pltpu.VMEM((3,) + _grad_half_shape(a), BF16) for a in range(n)]
        + [pltpu.SemaphoreType.DMA((4 * n,)), pltpu.SemaphoreType.DMA((4 * n,))],
        compiler_params=_cp(None, VMEM_LIMIT),
    )(*sums)


def _proj(x, w_in_p, gq, gkv, wuq_e, wukv, ct, st, cd, sd):
    seq = x.shape[0]
    tr = 512

    def body(x_ref, w_ref, gq_ref, gkv_ref, wuq_ref, wukv_ref, ct_ref, st_ref, cd_ref, sd_ref,
             cq_ref, ckv_ref, g_ref, qr_ref, kr_ref, vb_ref, q_out, k_out, v_out):
        lane = lax.broadcasted_iota(jnp.int32, (tr, LANES), 1)
        xb = x_ref[...].astype(BF16)
        cq = _dot(xb, w_ref[0:384, :], NT)
        ckv = _dot(xb, w_ref[384:640, :], NT)
        cq_ref[...] = cq
        ckv_ref[...] = ckv
        g_ref[...] = _dot(xb, w_ref[640:1664, :], NT)

        cd_, sd_ = cd_ref[...], sd_ref[...]
        qb = _dot(xb, w_ref[1664:2176, :], NT)
        kb = _dot(xb, w_ref[2176:2688, :], NT)
        for p in range(4):
            cols = slice(LANES * p, LANES * (p + 1))
            t = qb[:, cols]
            qr_ref[:, cols] = (t * cd_ + _dil_rot(t, lane) * sd_) * DIL_SCALE
            t = kb[:, cols]
            kr_ref[:, cols] = t * cd_ + _dil_rot(t, lane) * sd_
        vb_ref[...] = _dot(xb, w_ref[2688:3200, :], NT)

        ct_, st_ = ct_ref[...], st_ref[...]

        def rope(t):
            return t * ct_ + _mla_rot(t, lane) * st_

        _, qn = _rms(cq, gq_ref[...])
        q_all = _dot(qn.astype(BF16), wuq_ref[...])
        for h in range(MLA_HEADS):
            q_out[h] = (rope(q_all[:, LANES * h:LANES * (h + 1)]) * MLA_SCALE).astype(BF16)
        _, kvn = _rms(ckv, gkv_ref[...])
        kv_all = _dot(kvn.astype(BF16), wukv_ref[...])
        kpe = rope(_dot(xb, w_ref[3200:3328, :], NT))
        for h in range(MLA_HEADS):
            kv_h = kv_all[:, LANES * h:LANES * (h + 1)]
            k_out[h] = jnp.where(lane < 64, kv_h, kpe).astype(BF16)
            if h % 2:
                v = jnp.where(lane >= 64, kv_h, 0.0)
            else:
                v = jnp.where(lane < 64, pltpu.roll(kv_h, 64, 1), 0.0)
            v_out[h] = jnp.where(lane == ONES_LANE[h % 2], 1.0, v).astype(BF16)

    row = lambda w: pl.BlockSpec((tr, w), lambda i: (i, 0))
    full = lambda a: pl.BlockSpec(a.shape, lambda i: (0,) * a.ndim)
    head = pl.BlockSpec((MLA_HEADS, tr, LANES), lambda i: (0, i, 0))
    widths = (Q_LORA, KV_LORA, D_MODEL, 512, 512, 512)
    return pl.pallas_call(
        body, name="proj", grid=(seq // tr,),
        in_specs=[row(D_MODEL), full(w_in_p), full(gq), full(gkv), full(wuq_e), full(wukv)] + [row(LANES)] * 4,
        out_specs=[row(w) for w in widths] + [head] * 3,
        out_shape=[jax.ShapeDtypeStruct((seq, w), F32) for w in widths]
        + [jax.ShapeDtypeStruct((MLA_HEADS, seq, LANES), BF16)] * 3,
        compiler_params=_cp(("arbitrary",), VMEM_LIMIT),
    )(x, w_in_p, gq, gkv, wuq_e, wukv, ct, st, cd, sd)


def _mla_rot(t, lane):
    return jnp.where(lane < 80, pltpu.roll(t, 112, 1), pltpu.roll(t, 16, 1))


def _dil_rot(t, lane):
    return jnp.where(lane % 64 < 8, pltpu.roll(t, 120, 1), pltpu.roll(t, 8, 1))


def _rms(c, g):
    r = lax.rsqrt(jnp.mean(c * c, axis=-1, keepdims=True) + RMS_EPS)
    return r, c * r * g


def _mla_fwd(q, k, v):
    seq = q.shape[1]
    tq = 512
    nq = seq // tq

    def body(q_ref, k_ref, v_ref, o_ref, lse_ref, m_s, acc_s, s_buf):
        i = pl.program_id(1)
        row = lax.broadcasted_iota(jnp.int32, (tq, tq), 0)
        col = lax.broadcasted_iota(jnp.int32, (tq, tq), 1)
        lane = lax.broadcasted_iota(jnp.int32, (tq, LANES), 1)
        m_s[...] = jnp.full((2, tq, LANES), NEG, F32)
        acc_s[...] = jnp.zeros((2, tq, LANES), F32)

        def block(j):
            return pl.ds(pl.multiple_of(j * tq, tq), tq)

        def scores(hh, j):
            return _dot(q_ref[hh], k_ref[hh, block(j), :], NT)

        def consume(hh, j, s):
            m_prev = m_s[hh]
            m_new = jnp.maximum(m_prev, jnp.max(s, axis=1, keepdims=True))
            p = jnp.exp(s - m_new[:, :1])
            acc_s[hh] = jnp.exp(m_prev - m_new) * acc_s[hh] + _dot(p.astype(BF16), v_ref[hh, block(j), :])
            m_s[hh] = m_new

        for hh in range(2):
            s_buf[0, hh] = scores(hh, 0)

        def full_step(j, carry):
            slot = j & 1
            for hh in range(2):
                s = s_buf[slot, hh]
                s_buf[1 - slot, hh] = scores(hh, j + 1)
                consume(hh, j, s)
            return carry

        lax.fori_loop(0, i, full_step, 0)
        total = jnp.zeros((tq, LANES), F32)
        for hh in range(2):
            consume(hh, i, jnp.where(col <= row, s_buf[i & 1, hh], NEG))
            acc = acc_s[hh]
            l = acc[:, ONES_LANE[hh]:ONES_LANE[hh] + 1]
            mine = (lane >= 64) if hh else (lane < 64)
            total = total + jnp.where(mine, acc / l, 0.0)
            lse_ref[hh] = m_s[hh] + jnp.log(l)
        o_ref[...] = total

    kv_spec = pl.BlockSpec((2, seq, LANES), lambda p, i: (p, 0, 0))
    return pl.pallas_call(
        body, name="mla_fwd", grid=(MLA_HEADS // 2, nq),
        in_specs=[pl.BlockSpec((2, tq, LANES), lambda p, i: (p, i, 0)), kv_spec, kv_spec],
        out_specs=[pl.BlockSpec((tq, LANES), lambda p, i: (i, p)), pl.BlockSpec((2, tq, LANES), lambda p, i: (p, i, 0))],
        out_shape=[jax.ShapeDtypeStruct((seq, 4 * LANES), F32), jax.ShapeDtypeStruct((MLA_HEADS, seq, LANES), F32)],
        scratch_shapes=[pltpu.VMEM((2, tq, LANES), F32), pltpu.VMEM((2, tq, LANES), F32),
                        pltpu.VMEM((2, 2, tq, tq), F32)],
        compiler_params=_cp(("arbitrary", "arbitrary"), VMEM_LIMIT),
    )(q, k, v)


DIL_Q_FWD = 2 * BLOCK
DIL_Q_BWD = BLOCK


def _dil_tile_index(t, d, seq, nq):
    per_class = seq // (nq * d)
    shift = per_class.bit_length() - 1
    r = t >> shift
    n = t & (per_class - 1)
    start = r + (nq * d) * n
    prev = jnp.maximum(start - BLOCK * d, r)
    if d == 1:
        start = pl.multiple_of(start, nq)
        prev = pl.multiple_of(prev, BLOCK)
    return (n == 0).astype(jnp.int32), start, prev


def _dil_rows(start, d, size):
    return pl.ds(start, size) if d == 1 else pl.ds(start, size, stride=d)


def _dil_bias(nq):
    i = lax.broadcasted_iota(jnp.int32, (2 * nq, BLOCK + nq), 0) % nq
    j = lax.broadcasted_iota(jnp.int32, (2 * nq, BLOCK + nq), 1)
    band = (j >= i) & (j <= i + BLOCK)
    return jnp.where(band, 0.0, NEG), jnp.where(band & (j >= BLOCK), 0.0, NEG)


def _stack_heads(t, lane):
    return jnp.concatenate([jnp.where(lane < 64, t, 0.0), jnp.where(lane >= 64, t, 0.0)], axis=0)


def _unstack_heads(t, lane):
    nq = t.shape[0] // 2
    return jnp.where(lane < 64, t[:nq], t[nq:])


def _dil_fwd(qr, kr, vb):
    seq = qr.shape[0]
    nq = DIL_Q_FWD
    n_tiles = seq // nq
    assert seq % (nq * max(DIL_DILATIONS)) == 0

    def body(q_ref, k_ref, v_ref, o_ref, lse_ref, m_s, l_s, n_s, bias_s):
        lane = lax.broadcasted_iota(jnp.int32, (nq, LANES), 1)
        bias_s[0], bias_s[1] = _dil_bias(nq)
        for bi, d in enumerate(DIL_DILATIONS):

            def tile(t, carry, d=d, bi=bi):
                first, start, prev = _dil_tile_index(t, d, seq, nq)
                rows, prows = _dil_rows(start, d, nq), _dil_rows(prev, d, BLOCK)
                qst = _stack_heads(q_ref[rows, :], lane).astype(BF16)
                if seq == nq * d:
                    kcat, vcat = k_ref[rows, :].astype(BF16), v_ref[rows, :].astype(BF16)
                    s = _dot(qst, kcat, NT) + bias_s[1, :, BLOCK:]
                else:
                    kcat = jnp.concatenate([k_ref[prows, :], k_ref[rows, :]], axis=0).astype(BF16)
                    vcat = jnp.concatenate([v_ref[prows, :], v_ref[rows, :]], axis=0).astype(BF16)
                    s = _dot(qst, kcat, NT) + bias_s[first]
                m = jnp.max(s, axis=1, keepdims=True)
                p = jnp.exp(s - m)
                l2 = _unstack_heads(jnp.sum(p, axis=1, keepdims=True) + jnp.zeros((2 * nq, LANES), F32), lane)
                m2 = _unstack_heads(m + jnp.zeros((2 * nq, LANES), F32), lane)
                num2 = _unstack_heads(_dot(p.astype(BF16), vcat), lane)
                if bi == 0:
                    m_s[rows, :] = m2
                    l_s[rows, :] = l2
                    n_s[rows, :] = num2
                else:
                    m_old = m_s[rows, :]
                    m_new = jnp.maximum(m_old, m2)
                    a = jnp.exp(m_old - m_new)
                    b = jnp.exp(m2 - m_new)
                    m_s[rows, :] = m_new
                    l_s[rows, :] = a * l_s[rows, :] + b * l2
                    n_s[rows, :] = a * n_s[rows, :] + b * num2
                return carry

            lax.fori_loop(0, n_tiles, tile, 0, unroll=8)
        o_ref[...] = n_s[...] / l_s[...]
        lse_ref[...] = m_s[...] + jnp.log(l_s[...])

    col = lambda off: pl.BlockSpec((seq, LANES), lambda p: (0, p + off))
    return pl.pallas_call(
        body, name="dil_fwd", grid=(4,),
        in_specs=[col(0), col(0), col(0)],
        out_specs=[col(0), pl.BlockSpec((None, seq, LANES), lambda p: (p, 0, 0))],
        out_shape=[jax.ShapeDtypeStruct((seq, 4 * LANES), F32), jax.ShapeDtypeStruct((4, seq, LANES), F32)],
        scratch_shapes=[pltpu.VMEM((seq, LANES), F32)] * 3 + [pltpu.VMEM((2, 2 * nq, BLOCK + nq), F32)],
        compiler_params=_cp(("arbitrary",), VMEM_LIMIT),
    )(qr, kr, vb)


def _post(x, o_a, o_b, gates, w_out, ln_g, ln_b, target):
    seq = x.shape[0]
    tr = 512

    def body(x_ref, oa_ref, ob_ref, g_ref, w_ref, lg_ref, lb_ref, t_ref,
             dz_ref, do_ref, dg_ref, dw_ref, dlg_ref, dlb_ref, loss_ref):
        @pl.when(pl.program_id(0) == 0)
        def _():
            dw_ref[...] = jnp.zeros_like(dw_ref)
            dlg_ref[...] = jnp.zeros_like(dlg_ref)
            dlb_ref[...] = jnp.zeros_like(dlb_ref)
            loss_ref[...] = jnp.zeros_like(loss_ref)

        g = g_ref[...]
        sg = jax.nn.sigmoid(g)
        silu = g * sg
        o = jnp.concatenate([oa_ref[...], ob_ref[...]], axis=1)
        mixb = (o * silu).astype(BF16)
        w = w_ref[...]
        z = ALPHA * x_ref[...] + _dot(mixb, w)
        mu = jnp.mean(z, axis=-1, keepdims=True)
        zc = z - mu
        rstd = lax.rsqrt(jnp.mean(zc * zc, axis=-1, keepdims=True) + LN_EPS)
        xhat = zc * rstd
        lg = lg_ref[...]
        err = xhat * lg + lb_ref[...] - t_ref[...]
        loss_ref[...] += jnp.sum(err * err) * (0.5 / D_MODEL)
        dy = err * (1.0 / D_MODEL)
        dlg_ref[...] += jnp.sum(dy * xhat, axis=0, keepdims=True)
        dlb_ref[...] += jnp.sum(dy, axis=0, keepdims=True)
        dxh = dy * lg
        dz = rstd * (dxh - jnp.mean(dxh, axis=-1, keepdims=True) - xhat * jnp.mean(dxh * xhat, axis=-1, keepdims=True))
        dz_ref[...] = dz
        dzb = dz.astype(BF16)
        dmix = _dot(dzb, w, NT)
        do_ref[...] = dmix * silu
        dg_ref[...] = (dmix * o * (sg * (1.0 + g * (1.0 - sg)))).astype(BF16)
        dw_ref[...] += _dot(mixb, dzb, TN)

    row = lambda w: pl.BlockSpec((tr, w), lambda i: (i, 0))
    full = lambda s: pl.BlockSpec(s, lambda i: (0, 0))
    return pl.pallas_call(
        body, name="post", grid=(seq // tr,),
        in_specs=[row(D_MODEL), row(512), row(512), row(D_MODEL), full((D_MODEL, D_MODEL)), full((1, D_MODEL)),
                  full((1, D_MODEL)), row(D_MODEL)],
        out_specs=[row(D_MODEL), row(D_MODEL), row(D_MODEL), full((D_MODEL, D_MODEL)), full((1, D_MODEL)),
                   full((1, D_MODEL)), full((1, LANES))],
        out_shape=[jax.ShapeDtypeStruct((seq, D_MODEL), F32), jax.ShapeDtypeStruct((seq, D_MODEL), F32),
                   jax.ShapeDtypeStruct((seq, D_MODEL), BF16), jax.ShapeDtypeStruct((D_MODEL, D_MODEL), F32),
                   jax.ShapeDtypeStruct((1, D_MODEL), F32), jax.ShapeDtypeStruct((1, D_MODEL), F32),
                   jax.ShapeDtypeStruct((1, LANES), F32)],
        compiler_params=_cp(("arbitrary",), VMEM_LIMIT),
    )(x, o_a, o_b, gates, w_out, ln_g, ln_b, target)


def _mla_bwd(q, k, v, d_o, o, lse):
    seq = q.shape[1]
    tq = 512
    nq = seq // tq

    def body(q_ref, k_ref, v_ref, do_ref, o_ref, lse_ref, dq_ref, dk_ref, dv_ref, d_s, lse_s, dk_s, dv_s, v_s, kt_s, dqt_s):
        j = pl.program_id(1)
        lane = lax.broadcasted_iota(jnp.int32, (tq, LANES), 1)
        row = lax.broadcasted_iota(jnp.int32, (tq, tq), 0)
        col = lax.broadcasted_iota(jnp.int32, (tq, tq), 1)

        @pl.when(j == 0)
        def _():
            dqt_s[...] = jnp.zeros_like(dqt_s)

            def rowsum(i, carry):
                rows = pl.ds(pl.multiple_of(i * tq, tq), tq)
                prod = do_ref[rows, :] * o_ref[rows, :]
                for hh in range(2):
                    mine = (lane >= 64) if hh else (lane < 64)
                    total = jnp.sum(jnp.where(mine, prod, 0.0), axis=1, keepdims=True)
                    d_s[hh, i] = jnp.transpose(total + jnp.zeros((tq, LANES), F32))[:8]
                    lse_s[hh, i] = jnp.transpose(lse_ref[hh, rows, :])[:8]
                return carry

            lax.fori_loop(0, nq, rowsum, 0)

        dk_s[...] = jnp.zeros_like(dk_s)
        dv_s[...] = jnp.zeros_like(dv_s)
        for hh in range(2):
            v_s[hh] = jnp.where(lane == ONES_LANE[hh], 0.0, v_ref[hh].astype(F32)).astype(BF16)
            kt_s[hh] = jnp.transpose(k_ref[hh].astype(F32)).astype(BF16)

        def step(i, masked):
            rows = pl.ds(pl.multiple_of(i * tq, tq), tq)
            dob = do_ref[rows, :].astype(BF16)
            for hh in range(2):
                qb, kb, vb = q_ref[hh, rows, :], k_ref[hh], v_s[hh]
                p = jnp.exp(_dot(kb, qb, NT) - lse_s[hh, i][:1])
                if masked:
                    p = jnp.where(row <= col, p, 0.0)
                dv_s[hh] += _dot(p.astype(BF16), dob)
                ds = (p * (_dot(vb, dob, NT) - d_s[hh, i][:1])).astype(BF16)
                dk_s[hh] += _dot(ds, qb)
                dqt_s[hh, i] += _dot(kt_s[hh], ds)

        def full_step(i, carry):
            step(i, False)
            return carry

        step(j, True)
        lax.fori_loop(j + 1, nq, full_step, 0)
        dk_ref[...] = dk_s[...]
        dv_ref[...] = dv_s[...]

        @pl.when(j == nq - 1)
        def _():
            def untranspose(i, carry):
                rows = pl.ds(pl.multiple_of(i * tq, tq), tq)
                for hh in range(2):
                    dq_ref[hh, rows, :] = jnp.transpose(dqt_s[hh, i])
                return carry

            lax.fori_loop(0, nq, untranspose, 0)

    whole = pl.BlockSpec((2, seq, LANES), lambda p, j: (p, 0, 0))
    blk = pl.BlockSpec((2, tq, LANES), lambda p, j: (p, j, 0))
    pair = pl.BlockSpec((seq, LANES), lambda p, j: (0, p))
    shape = jax.ShapeDtypeStruct((MLA_HEADS, seq, LANES), F32)
    return pl.pallas_call(
        body, name="mla_bwd", grid=(MLA_HEADS // 2, nq),
        in_specs=[whole, blk, blk, pair, pair, whole],
        out_specs=[whole, blk, blk], out_shape=[shape] * 3,
        scratch_shapes=[pltpu.VMEM((2, nq, 8, tq), F32), pltpu.VMEM((2, nq, 8, tq), F32),
                        pltpu.VMEM((2, tq, LANES), F32), pltpu.VMEM((2, tq, LANES), F32),
                        pltpu.VMEM((2, tq, LANES), BF16), pltpu.VMEM((2, LANES, tq), BF16),
                        pltpu.VMEM((2, nq, LANES, tq), F32)],
        compiler_params=_cp(("arbitrary", "arbitrary"), VMEM_LIMIT),
    )(q, k, v, d_o, o, lse)


def _dil_bwd(qr, kr, vb, d_o, o, lse):
    seq = qr.shape[0]
    nq = DIL_Q_BWD
    n_tiles = seq // nq
    chunk = 512

    def body(q_ref, k_ref, v_ref, do_ref, o_ref, lse_ref, dq_ref, dk_ref, dv_ref, d_s, dq_s, dk_s, dv_s, bias_s):
        lane = lax.broadcasted_iota(jnp.int32, (nq, LANES), 1)
        lanec = lax.broadcasted_iota(jnp.int32, (chunk, LANES), 1)
        bias_s[0], bias_s[1] = [b[:nq] for b in _dil_bias(nq)]

        def rowsum(i, carry):
            rows = pl.ds(pl.multiple_of(i * chunk, chunk), chunk)
            prod = do_ref[rows, :] * o_ref[rows, :]
            lo = jnp.sum(jnp.where(lanec < 64, prod, 0.0), axis=1, keepdims=True)
            hi = jnp.sum(jnp.where(lanec >= 64, prod, 0.0), axis=1, keepdims=True)
            d_s[rows, :] = jnp.where(lanec < 64, lo, hi)
            return carry

        lax.fori_loop(0, seq // chunk, rowsum, 0)
        dq_s[...] = jnp.zeros_like(dq_s)
        dk_s[...] = jnp.zeros_like(dk_s)
        dv_s[...] = jnp.zeros_like(dv_s)
        for d in DIL_DILATIONS:

            def tile(start, prev, first, d=d):
                rows = _dil_rows(start, d, nq)
                q_t, do_t = q_ref[rows, :], do_ref[rows, :]
                lse_t, d_t = lse_ref[rows, :], d_s[rows, :]
                if prev is None:
                    kcat, vcat = k_ref[rows, :].astype(BF16), v_ref[rows, :].astype(BF16)
                    bias = bias_s[1, :, BLOCK:]
                else:
                    prows = _dil_rows(prev, d, BLOCK)
                    kcat = jnp.concatenate([k_ref[prows, :], k_ref[rows, :]], axis=0).astype(BF16)
                    vcat = jnp.concatenate([v_ref[prows, :], v_ref[rows, :]], axis=0).astype(BF16)
                    bias = bias_s[first]
                dq_t = jnp.zeros((nq, LANES), F32)
                dkcat = jnp.zeros((kcat.shape[0], LANES), F32)
                dvcat = jnp.zeros((kcat.shape[0], LANES), F32)
                for hh in range(2):
                    mine = (lane >= 64) if hh else (lane < 64)
                    c0 = 64 * hh
                    qh = jnp.where(mine, q_t, 0.0).astype(BF16)
                    doh = jnp.where(mine, do_t, 0.0).astype(BF16)
                    p = jnp.exp(_dot(qh, kcat, NT) + bias - lse_t[:, c0:c0 + 1])
                    dvcat = dvcat + _dot(p.astype(BF16), doh, TN)
                    dp = _dot(doh, vcat, NT)
                    ds = (p * (dp - d_t[:, c0:c0 + 1])).astype(BF16)
                    dq_t = dq_t + jnp.where(mine, _dot(ds, kcat), 0.0)
                    dkcat = dkcat + _dot(ds, qh, TN)
                dq_s[rows, :] += dq_t
                if prev is not None:
                    dk_s[prows, :] += dkcat[:BLOCK]
                    dv_s[prows, :] += dvcat[:BLOCK]
                dk_s[rows, :] += dkcat[-nq:]
                dv_s[rows, :] += dvcat[-nq:]

            if seq == 2 * nq * d:

                def class_tiles(r, carry, d=d):
                    tile(r, None, 1)
                    tile(r + nq * d, r, 0)
                    return carry

                lax.fori_loop(0, d, class_tiles, 0, unroll=8)
            else:

                def any_tile(t, carry, d=d):
                    first, start, prev = _dil_tile_index(t, d, seq, nq)
                    tile(start, prev, first)
                    return carry

                lax.fori_loop(0, n_tiles, any_tile, 0, unroll=16)
        dq_ref[...] = dq_s[...].astype(BF16)
        dk_ref[...] = dk_s[...].astype(BF16)
        dv_ref[...] = dv_s[...].astype(BF16)

    col = lambda off: pl.BlockSpec((seq, LANES), lambda p: (0, p + off))
    shape = jax.ShapeDtypeStruct((seq, 4 * LANES), BF16)
    return pl.pallas_call(
        body, name="dil_bwd", grid=(4,),
        in_specs=[col(0), col(0), col(0), col(4), col(0), pl.BlockSpec((None, seq, LANES), lambda p: (p, 0, 0))],
        out_specs=[col(0)] * 3, out_shape=[shape] * 3,
        scratch_shapes=[pltpu.VMEM((seq, LANES), F32)] * 4 + [pltpu.VMEM((2, nq, BLOCK + nq), F32)],
        compiler_params=_cp(("arbitrary",), VMEM_LIMIT),
    )(qr, kr, vb, d_o, o, lse)


def _in_bwd(dz, cq, ckv, gq, gkv, wuq_e, wukv, ct, st, dq, dk, dv, dgates, dqr, dkr, dvb, cd, sd, w_in_p):
    seq = dz.shape[0]
    tr = 512

    def body(dz_ref, cq_ref, ckv_ref, gq_ref, gkv_ref, wuq_ref, wukv_ref, ct_ref, st_ref, dq_ref, dk_ref, dv_ref,
             dg_ref, dqr_ref, dkr_ref, dvb_ref, cd_ref, sd_ref, w_ref,
             gx_ref, dh_ref, dwuq_ref, dwukv_ref, dgq_ref, dgkv_ref):
        @pl.when(pl.program_id(0) == 0)
        def _():
            dwuq_ref[...] = jnp.zeros_like(dwuq_ref)
            dwukv_ref[...] = jnp.zeros_like(dwukv_ref)
            dgq_ref[...] = jnp.zeros_like(dgq_ref)
            dgkv_ref[...] = jnp.zeros_like(dgkv_ref)

        lane = lax.broadcasted_iota(jnp.int32, (tr, LANES), 1)
        rope_lanes = jnp.logical_and(lane >= 64, lane < 96)
        ct_, st_ = ct_ref[...], st_ref[...]

        def mla_rope_t(g):
            return ct_ * g + jnp.where(rope_lanes, _mla_rot(st_ * g, lane), 0.0)

        def norm_bwd(c, g, dn, dg_ref):
            r, _ = _rms(c, g)
            u = dn * g
            dg_ref[...] += jnp.sum(dn * c * r, axis=0, keepdims=True)
            return r * u - c * (r * r * r) * jnp.mean(u * c, axis=-1, keepdims=True)

        c, g = cq_ref[...], gq_ref[...]
        _, qn = _rms(c, g)
        dq_all = jnp.concatenate([mla_rope_t(dq_ref[h] * MLA_SCALE) for h in range(MLA_HEADS)], axis=1).astype(BF16)
        dwuq_ref[...] += _dot(qn.astype(BF16), dq_all, TN)
        dcq = norm_bwd(c, g, _dot(dq_all, wuq_ref[...], NT), dgq_ref).astype(BF16)

        c, g = ckv_ref[...], gkv_ref[...]
        _, kvn = _rms(c, g)
        dkpe = jnp.zeros((tr, LANES), F32)
        parts = []
        for h in range(MLA_HEADS):
            dk_h, dv_h = dk_ref[h], dv_ref[h]
            if h % 2 == 0:
                dv_h = pltpu.roll(dv_h, 64, 1)
            parts.append(jnp.where(lane < 64, dk_h, dv_h))
            dkpe = dkpe + jnp.where(rope_lanes, dk_h, 0.0)
        dkv_all = jnp.concatenate(parts, axis=1).astype(BF16)
        dwukv_ref[...] += _dot(kvn.astype(BF16), dkv_all, TN)
        dckv = norm_bwd(c, g, _dot(dkv_all, wukv_ref[...], NT), dgkv_ref).astype(BF16)
        dkrope = mla_rope_t(dkpe).astype(BF16)

        rot_lanes = lane % 64 < DIL_ROT
        cd_, sd_ = cd_ref[...], sd_ref[...]

        def dil_rope_t(g):
            return cd_ * g + jnp.where(rot_lanes, _dil_rot(sd_ * g, lane), 0.0)

        dqb = [dil_rope_t(dqr_ref[:, LANES * p:LANES * (p + 1)].astype(F32) * DIL_SCALE).astype(BF16) for p in range(4)]
        dkb = [dil_rope_t(dkr_ref[:, LANES * p:LANES * (p + 1)].astype(F32)).astype(BF16) for p in range(4)]
        dh = jnp.concatenate([dcq, dckv, dg_ref[...]] + dqb + dkb + [dvb_ref[...], dkrope], axis=1)
        dh_ref[...] = dh
        gx_ref[...] = ALPHA * dz_ref[...] + _dot(dh, w_ref[...])

    row = lambda w: pl.BlockSpec((tr, w), lambda i: (i, 0))
    full = lambda a: pl.BlockSpec(a.shape, lambda i: (0,) * a.ndim)
    head = pl.BlockSpec((MLA_HEADS, tr, LANES), lambda i: (0, i, 0))
    return pl.pallas_call(
        body, name="in_bwd", grid=(seq // tr,),
        in_specs=[row(D_MODEL), row(Q_LORA), row(KV_LORA), full(gq), full(gkv), full(wuq_e), full(wukv), row(LANES),
                  row(LANES), head, head, head, row(D_MODEL), row(512), row(512), row(512), row(LANES), row(LANES),
                  full(w_in_p)],
        out_specs=[row(D_MODEL), row(IN_WIDTH_PAD), full(wuq_e), full(wukv), full(gq), full(gkv)],
        out_shape=[jax.ShapeDtypeStruct((seq, D_MODEL), F32), jax.ShapeDtypeStruct((seq, IN_WIDTH_PAD), BF16),
                   jax.ShapeDtypeStruct(wuq_e.shape, F32), jax.ShapeDtypeStruct(wukv.shape, F32),
                   jax.ShapeDtypeStruct(gq.shape, F32), jax.ShapeDtypeStruct(gkv.shape, F32)],
        compiler_params=_cp(("arbitrary",), VMEM_LIMIT),
    )(dz, cq, ckv, gq, gkv, wuq_e, wukv, ct, st, dq, dk, dv, dgates, dqr, dkr, dvb, cd, sd, w_in_p)


def _dw_in(x, dh):
    seq = dh.shape[0]
    tk = 512
    tn = IN_WIDTH_PAD // 2

    def body(x_ref, dh_ref, o_ref):
        @pl.when(pl.program_id(1) == 0)
        def _():
            o_ref[...] = jnp.zeros_like(o_ref)

        o_ref[...] += _dot(dh_ref[...], x_ref[...].astype(BF16), TN)

    return pl.pallas_call(
        body, name="dw_in", grid=(2, seq // tk),
        in_specs=[pl.BlockSpec((tk, D_MODEL), lambda n, k: (k, 0)), pl.BlockSpec((tk, tn), lambda n, k: (k, n))],
        out_specs=pl.BlockSpec((tn, D_MODEL), lambda n, k: (n, 0)),
        out_shape=jax.ShapeDtypeStruct((IN_WIDTH_PAD, D_MODEL), F32),
        compiler_params=_cp(("arbitrary", "arbitrary"), VMEM_LIMIT),
    )(x, dh)


def _adam_update(w, g, m, v):
    nm = ADAM_B1 * m + (1.0 - ADAM_B1) * g
    nv = ADAM_B2 * v + (1.0 - ADAM_B2) * jnp.square(g)
    m_hat = nm / (1.0 - ADAM_B1 ** ADAM_STEP)
    v_hat = nv / (1.0 - ADAM_B2 ** ADAM_STEP)
    return -ADAM_LR * (m_hat / (jnp.sqrt(v_hat) + ADAM_EPS) + ADAM_WD * w), nm, nv


def _adamw(w, g, m, v, name):
    rows, cols = w.shape
    tc = 256 if cols % 256 == 0 and rows * cols > 2 ** 18 else cols

    def body(w_ref, g_ref, m_ref, v_ref, d_ref, nm_ref, nv_ref):
        d_ref[...], nm_ref[...], nv_ref[...] = _adam_update(w_ref[...], g_ref[...], m_ref[...], v_ref[...])

    spec = pl.BlockSpec((rows, tc), lambda i: (0, i))
    return pl.pallas_call(
        body, name=name, grid=(cols // tc,), in_specs=[spec] * 4, out_specs=[spec] * 3,
        out_shape=[jax.ShapeDtypeStruct(w.shape, F32)] * 3, compiler_params=_cp(("arbitrary",)),
    )(w, g, m, v)


def _adamw_vectors(small_sum, ws, ms, vs):
    k = len(ws)
    sizes = [w.shape[-1] for w in ws]

    def body(s_ref, *refs):
        ins, outs = refs[:3 * k], refs[3 * k:]
        for i, size in enumerate(sizes):
            g = s_ref[i:i + 1, 0:size]
            outs[i][...] = g
            outs[k + i][...], outs[2 * k + i][...], outs[3 * k + i][...] = _adam_update(
                ins[i][...], g, ins[k + i][...], ins[2 * k + i][...])

    out = pl.pallas_call(
        body, name="adamw_vectors", out_shape=[jax.ShapeDtypeStruct((1, size), F32) for size in sizes] * 4,
    )(small_sum, *[a.reshape(1, -1) for a in list(ws) + list(ms) + list(vs)])
    return [[a.reshape(-1) for a in out[k * j:k * (j + 1)]] for j in range(4)]


def _local_step(x2, target, w_in_p, w_uq_f, wukv_f, w_out_f, q_norm_g, kv_norm_g, ln_g, ln_b):
    seq = x2.shape[0]
    wuq_e = jnp.pad(w_uq_f.reshape(Q_LORA, MLA_HEADS, 96), ((0, 0), (0, 0), (0, 32))).reshape(Q_LORA, MLA_HEADS * LANES)
    ct, st, cd, sd = _rope_tables(seq)
    gq = q_norm_g.reshape(1, Q_LORA)
    gkv = kv_norm_g.reshape(1, KV_LORA)

    cq, ckv, gates, qr, krot, vb, q_e, k_e, v_e = _proj(x2, w_in_p, gq, gkv, wuq_e, wukv_f, ct, st, cd, sd)
    o_a, lse_a = _mla_fwd(q_e, k_e, v_e)
    o_b, lse_b = _dil_fwd(qr, krot, vb)

    dz, d_o, d_gates, dw_out, dln_g, dln_b, loss_part = _post(
        x2, o_a, o_b, gates, w_out_f, ln_g.reshape(1, D_MODEL), ln_b.reshape(1, D_MODEL), target)
    dq_e, dk_e, dv_e = _mla_bwd(q_e, k_e, v_e, d_o, o_a, lse_a)
    dqr, dkr, dvb = _dil_bwd(qr, krot, vb, d_o, o_b, lse_b)
    grad_x, dh, dwuq_e, dwukv, dgq, dgkv = _in_bwd(
        dz, cq, ckv, gq, gkv, wuq_e, wukv_f, ct, st, dq_e, dk_e, dv_e, d_gates, dqr, dkr, dvb, cd, sd, w_in_p)
    dw_in = _dw_in(x2, dh)
    dw_uq = dwuq_e.reshape(Q_LORA, MLA_HEADS, LANES)[:, :, :96].reshape(Q_LORA, MLA_HEADS * 96)
    return loss_part, grad_x, dw_in, dw_uq, dwukv, dw_out, dgq, dgkv, dln_g, dln_b


def kernel(x, w_in, q_norm_g, kv_norm_g, w_uq, w_ukv, w_out, ln_g, ln_b, loss_target, m_w_in, m_q_norm_g, m_kv_norm_g, m_w_uq, m_w_ukv, m_w_out, m_ln_g, m_ln_b, v_w_in, v_q_norm_g, v_kv_norm_g, v_w_uq, v_w_ukv, v_w_out, v_ln_g, v_ln_b):
    seq = x.shape[1]
    x2 = x.reshape(seq, D_MODEL)
    target = loss_target.reshape(seq, D_MODEL)

    g_w_in, g_w_uq, g_w_ukv, g_w_out = _all_gather_weights([w_in.T, w_uq, w_ukv, w_out])
    by_cols = lambda g: jnp.concatenate([g[j] for j in range(N_SHARD)], axis=1)
    loss_part, grad_x, dw_in, dw_uq, dwukv, dw_out, dgq, dgkv, dln_g, dln_b = _local_step(
        x2, target, _permute_w_in_t_shards(g_w_in), by_cols(g_w_uq), by_cols(g_w_ukv), g_w_out.reshape(D_MODEL, D_MODEL),
        q_norm_g, kv_norm_g, ln_g, ln_b)

    to_shards = lambda d: d.reshape(d.shape[0], N_SHARD, d.shape[1] // N_SHARD).transpose(1, 0, 2)
    grads = [dw_in, to_shards(dw_uq), to_shards(dwukv), dw_out.reshape(N_SHARD, 256, D_MODEL)]
    *chip_sums, small_sum = _reduce_over_sibling(grads, [dgq, dgkv, dln_g, dln_b, loss_part])
    g_in_t, g_uq, g_ukv, g_out = _reduce_over_chips(chip_sums)
    g_in = g_in_t.T
    loss = small_sum[4, 0]

    big = [[o.T for o in _adamw(w.T, g.T, m.T, v.T, name)] for w, g, m, v, name in (
        (w_in, g_in, m_w_in, v_w_in, "adamw_w_in"), (w_uq, g_uq, m_w_uq, v_w_uq, "adamw_w_uq"))]
    big += [_adamw(w, g, m, v, name) for w, g, m, v, name in (
        (w_ukv, g_ukv, m_w_ukv, v_w_ukv, "adamw_w_ukv"), (w_out, g_out, m_w_out, v_w_out, "adamw_w_out"))]
    vec_g, vec_delta, vec_m, vec_v = _adamw_vectors(
        small_sum, [q_norm_g, kv_norm_g, ln_g, ln_b], [m_q_norm_g, m_kv_norm_g, m_ln_g, m_ln_b],
        [v_q_norm_g, v_kv_norm_g, v_ln_g, v_ln_b])

    def ordered(bigs, vecs):
        return [bigs[0], vecs[0], vecs[1], bigs[1], bigs[2], bigs[3], vecs[2], vecs[3]]

    grads_out = ordered([g_in, g_uq, g_ukv, g_out], vec_g)
    deltas = ordered([b[0] for b in big], vec_delta)
    new_m = ordered([b[1] for b in big], vec_m)
    new_v = ordered([b[2] for b in big], vec_v)
    return (loss, grad_x.reshape(x.shape), *grads_out, *deltas, *new_m, *new_v)
```

```python
import jax
import jax.numpy as jnp
import numpy as np
from jax import lax
from jax.experimental import pallas as pl
from jax.experimental.pallas import tpu as pltpu

F32 = jnp.float32
BF16 = jnp.bfloat16

D_MODEL = 1024
ROPE_THETA = 500000.0
BLOCK = 128
NEG = -1e30
RMS_EPS = 1e-6
LN_EPS = 1e-5

MLA_HEADS = 8
MLA_NOPE = 64
MLA_ROPE = 32
Q_LORA = 384
KV_LORA = 256
DIL_HEAD_DIM = 64
DIL_ROT = 16
DIL_DILATIONS = (1, 4, 16)
IN_WIDTH_PAD = 3328
ONES_LANE = (64, 0)
MLA_SCALE = (MLA_NOPE + MLA_ROPE) ** -0.5
DIL_SCALE = DIL_HEAD_DIM ** -0.5
ALPHA = 2.0 ** 0.25

ADAM_LR = 0.001
ADAM_B1 = 0.9
ADAM_B2 = 0.999
ADAM_EPS = 1e-08
ADAM_WD = 0.01
ADAM_STEP = 10

N_SHARD = 4
SHARD_SHAPES = ((808, 1024), (384, 192), (256, 256), (256, 1024))
SHARD_SPLIT_COLS = (True, False, False, False)
ROW_CHUNK = 64
LANES = 128
VMEM_LIMIT = 56 * 1024 * 1024
MESH = pl.DeviceIdType.MESH

NT = (((1,), (1,)), ((), ()))
TN = (((0,), (0,)), ((), ()))


def _cp(sem=None, vmem=None):
    return pltpu.CompilerParams(dimension_semantics=sem, vmem_limit_bytes=vmem)


def _dot(a, b, dims=None):
    if dims is None:
        return jnp.dot(a, b, preferred_element_type=F32)
    return lax.dot_general(a, b, dims, preferred_element_type=F32)


def _rope_tables(seq):
    f32 = np.float32
    pos = np.arange(seq, dtype=f32)[:, None]
    one, zero = np.ones((seq, 64), f32), np.zeros((seq, 64), f32)

    def cos_sin(dim):
        inv = np.power(f32(ROPE_THETA), -np.arange(0, dim, 2, dtype=f32) / f32(dim)).astype(f32)
        ang = (pos * inv[None, :]).astype(f32)
        return np.cos(ang).astype(f32), np.sin(ang).astype(f32)

    cos, sin = cos_sin(MLA_ROPE)
    ct = np.concatenate([one, cos, cos, zero[:, :32]], axis=1)
    st = np.concatenate([zero, -sin, sin, zero[:, :32]], axis=1)
    cos, sin = cos_sin(DIL_ROT)
    cd = np.concatenate([cos, cos, one[:, :48]], axis=1)
    sd = np.concatenate([-sin, sin, zero[:, :48]], axis=1)
    return tuple(jnp.asarray(t) for t in (ct, st, np.tile(cd, (1, 2)), np.tile(sd, (1, 2))))


W_IN_ORDER = ((0, 640), (672, 1184), (2720, 3232), (1184, 2720), None, (640, 672))


def _permute_w_in_t(w_t):
    z = jnp.zeros((64, w_t.shape[1]), w_t.dtype)
    parts = [z if r is None else w_t[r[0]:r[1]] for r in W_IN_ORDER]
    return jnp.concatenate(parts + [z[:32]], axis=0)


def _w_in_row_pieces():
    width = SHARD_SHAPES[0][0]
    pieces, at = [], 0
    for r in W_IN_ORDER:
        if r is None:
            at += 64
            continue
        for k in range(N_SHARD):
            lo, hi = max(r[0], width * k), min(r[1], width * (k + 1))
            if lo < hi:
                pieces.append((k, lo - width * k, at + lo - r[0], hi - lo))
        at += r[1] - r[0]
    return pieces


def _position():
    return lax.axis_index("x"), lax.axis_index("y"), lax.axis_index("c")


def _all_gather_weights(shards):
    n = len(shards)

    def body(*refs):
        ins, outs = refs[:n], list(refs[n:2 * n])
        w_in_p, outs[0] = outs[0], refs[2 * n]
        send_sems, recv_sems = refs[2 * n + 1:]
        x, y, c = _position()
        me = 2 * x + y
        chips = [(1 - x, y), (x, 1 - y), (1 - x, 1 - y)]
        for a in range(n):
            rows, cols = SHARD_SHAPES[a]
            if SHARD_SPLIT_COLS[a]:
                blocks = [(slice(None), slice(c0, c0 + LANES)) for c0 in range(0, cols, LANES)]
            else:
                blocks = [(slice(r0, r0 + ROW_CHUNK), slice(None)) for r0 in range(0, rows, ROW_CHUNK)]
            for blk in blocks:
                outs[a][(me,) + blk] = ins[a][blk].astype(BF16)

        def copy(k, a, slot, part, to):
            ref = outs[a].at[(slot,) + part]
            return pltpu.make_async_remote_copy(
                src_ref=ref, dst_ref=ref, send_sem=send_sems.at[k * n + a], recv_sem=recv_sems.at[k * n + a],
                device_id=to, device_id_type=MESH)

        half = [_shard_half(a, c) for a in range(n)]
        other = [_shard_half(a, 1 - c) for a in range(n)]
        first = [copy(k, a, me, half[a], (px, py, c)) for k, (px, py) in enumerate(chips) for a in range(n)]
        for cp in first:
            cp.start()
        passed = []
        for k, (px, py) in enumerate(chips):
            for a in range(n):
                copy(k, a, 2 * px + py, half[a], (x, y, c)).wait_recv()
                cp = copy(3 + k, a, 2 * px + py, half[a], (x, y, 1 - c))
                cp.start()
                passed.append(cp)
        for k, (px, py) in enumerate(chips):
            for a in range(n):
                copy(3 + k, a, 2 * px + py, other[a], (x, y, c)).wait_recv()
        for cp in first + passed:
            cp.wait_send()

        written = []
        for k, r0, at, rows in _w_in_row_pieces():
            written.append((at, at + rows))
            for r in range(0, rows, 2 * LANES):
                m = min(2 * LANES, rows - r)
                for c0 in range(0, D_MODEL, LANES):
                    w_in_p[at + r:at + r + m, c0:c0 + LANES] = outs[0][k, r0 + r:r0 + r + m, c0:c0 + LANES]
        for lo, hi in zip([0] + [w[1] for w in sorted(written)], [w[0] for w in sorted(written)] + [IN_WIDTH_PAD]):
            if lo < hi:
                w_in_p[lo:hi, :] = jnp.zeros((hi - lo, D_MODEL), BF16)

    vmem = pl.BlockSpec(memory_space=pltpu.VMEM)
    return pl.pallas_call(
        body, name="all_gather_weights",
        out_shape=[jax.ShapeDtypeStruct((IN_WIDTH_PAD, D_MODEL), BF16)]
        + [jax.ShapeDtypeStruct((N_SHARD,) + s, BF16) for s in SHARD_SHAPES[1:]],
        in_specs=[vmem] * n, out_specs=[vmem] * n,
        scratch_shapes=[pltpu.VMEM((N_SHARD,) + SHARD_SHAPES[0], BF16),
                        pltpu.SemaphoreType.DMA((6 * n,)), pltpu.SemaphoreType.DMA((6 * n,))],
        compiler_params=_cp(None, VMEM_LIMIT),
    )(*shards)


def _shard_half_shape(a):
    rows, cols = SHARD_SHAPES[a]
    return (rows, cols // 2) if SHARD_SPLIT_COLS[a] else (rows // 2, cols)


def _shard_half(a, c):
    rows, cols = SHARD_SHAPES[a]
    if SHARD_SPLIT_COLS[a]:
        return slice(None), pl.ds(pl.multiple_of(c * (cols // 2), LANES), cols // 2)
    return pl.ds(pl.multiple_of(c * (rows // 2), ROW_CHUNK), rows // 2), slice(None)


def _shard_chunks(a, c):
    rows, cols = SHARD_SHAPES[a]
    if SHARD_SPLIT_COLS[a]:
        return [((slice(None), pl.ds(c0, LANES)),
                 (slice(None), pl.ds(pl.multiple_of(c * (cols // 2) + c0, LANES), LANES)))
                for c0 in range(0, cols // 2, LANES)]
    return [((pl.ds(r0, ROW_CHUNK), slice(None)),
             (pl.ds(pl.multiple_of(c * (rows // 2) + r0, ROW_CHUNK), ROW_CHUNK), slice(None)))
            for r0 in range(0, rows // 2, ROW_CHUNK)]


def _reduce_over_sibling(grads, small_rows):
    n = len(grads)
    n_small = len(small_rows)
    pieces = _w_in_row_pieces()

    def body(*refs):
        g_hbm, rows_in = refs[:n], refs[n:n + n_small]
        sums, small_sum = refs[n + n_small:2 * n + n_small], refs[2 * n + n_small]
        scratch = refs[2 * n + n_small + 1:]
        stage, got = scratch[:n], scratch[n:2 * n]
        sm, smalls, send_sems, recv_sems, local_sems = scratch[2 * n:]
        x, y, c = _position()
        me = 4 * x + 2 * y + c
        sm[...] = jnp.zeros_like(sm)
        for i, row in enumerate(rows_in):
            sm[i:i + 1, 0:row.shape[1]] = row[...]
        loads = [[pltpu.make_async_copy(g_hbm[0].at[pl.ds(src, rows)], stage[0].at[k, pl.ds(dst, rows)],
                                        local_sems.at[n + i])
                  for i, (k, dst, src, rows) in enumerate(pieces)]]
        loads += [[pltpu.make_async_copy(g_hbm[a], stage[a], local_sems.at[a])] for a in range(1, n)]
        for group in loads:
            for ld in group:
                ld.start()
        smalls[me] = sm[...]
        sends = []
        for rel in range(1, 8):
            px = 1 - x if rel // 4 else x
            py = 1 - y if (rel // 2) % 2 else y
            pc = 1 - c if rel % 2 else c
            cp = pltpu.make_async_remote_copy(
                src_ref=sm, dst_ref=smalls.at[me], send_sem=send_sems.at[n + rel], recv_sem=recv_sems.at[n + rel],
                device_id=(px, py, pc), device_id_type=MESH)
            cp.start()
            sends.append((cp, 4 * px + 2 * py + pc))
        swaps = []
        for a in range(n):
            for ld in loads[a]:
                ld.wait()
            cp = pltpu.make_async_remote_copy(
                src_ref=stage[a].at[(slice(None),) + _shard_half(a, 1 - c)], dst_ref=got[a],
                send_sem=send_sems.at[a], recv_sem=recv_sems.at[a], device_id=(x, y, 1 - c), device_id_type=MESH)
            cp.start()
            swaps.append(cp)
        for a in range(n):
            swaps[a].wait_recv()
            for k in range(N_SHARD):
                for in_half, in_whole in _shard_chunks(a, c):
                    pair = stage[a][(k,) + in_whole] + got[a][(k,) + in_half]
                    sums[a][(k,) + in_half] = pair.astype(BF16)
        for rel, (cp, peer) in enumerate(sends, start=1):
            pltpu.make_async_remote_copy(
                src_ref=sm, dst_ref=smalls.at[peer], send_sem=send_sems.at[n + rel], recv_sem=recv_sems.at[n + rel],
                device_id=(x, y, c), device_id_type=MESH).wait_recv()
        total = smalls[0]
        for dev in range(1, 8):
            total = total + smalls[dev]
        small_sum[...] = total
        for cp in swaps:
            cp.wait_send()
        for cp, _ in sends:
            cp.wait_send()

    vmem = pl.BlockSpec(memory_space=pltpu.VMEM)
    half = [(N_SHARD,) + _shard_half_shape(a) for a in range(n)]
    return pl.pallas_call(
        body, name="reduce_over_sibling",
        out_shape=[jax.ShapeDtypeStruct(s, BF16) for s in half] + [jax.ShapeDtypeStruct((8, D_MODEL), F32)],
        in_specs=[pl.BlockSpec(memory_space=pl.ANY)] * n + [vmem] * n_small, out_specs=[vmem] * (n + 1),
        scratch_shapes=[pltpu.VMEM((N_SHARD,) + s, F32) for s in SHARD_SHAPES] + [pltpu.VMEM(s, F32) for s in half]
        + [pltpu.VMEM((8, D_MODEL), F32), pltpu.VMEM((8, 8, D_MODEL), F32),
           pltpu.SemaphoreType.DMA((n + 8,)), pltpu.SemaphoreType.DMA((n + 8,)),
           pltpu.SemaphoreType.DMA((n + len(pieces),))],
        compiler_params=_cp(None, VMEM_LIMIT),
    )(*grads, *small_rows)


def _reduce_over_chips(sums):
    n = len(sums)

    def body(*refs):
        h, outs, got = refs[:n], refs[n:2 * n], refs[2 * n:3 * n]
        send_sems, recv_sems = refs[3 * n:]
        x, y, c = _position()
        me = 2 * x + y
        chips = [(1 - x, y), (x, 1 - y), (1 - x, 1 - y)]
        sends = []
        for k, (px, py) in enumerate(chips):
            for a in range(n):
                cp = pltpu.make_async_remote_copy(
                    src_ref=h[a].at[2 * px + py], dst_ref=got[a].at[k], send_sem=send_sems.at[k * n + a],
                    recv_sem=recv_sems.at[k * n + a], device_id=(px, py, c), device_id_type=MESH)
                cp.start()
                sends.append(cp)
        for cp in sends:
            cp.wait_recv()
        joins = []
        for a in range(n):
            for in_half, in_whole in _shard_chunks(a, c):
                total = h[a][(me,) + in_half].astype(F32)
                for k in range(3):
                    total = total + got[a][(k,) + in_half].astype(F32)
                outs[a][in_whole] = total
            half = outs[a].at[_shard_half(a, c)]
            cp = pltpu.make_async_remote_copy(
                src_ref=half, dst_ref=half, send_sem=send_sems.at[3 * n + a],
                recv_sem=recv_sems.at[3 * n + a], device_id=(x, y, 1 - c), device_id_type=MESH)
            cp.start()
            joins.append(cp)
        for a in range(n):
            other = outs[a].at[_shard_half(a, 1 - c)]
            pltpu.make_async_remote_copy(
                src_ref=other, dst_ref=other, send_sem=send_sems.at[3 * n + a],
                recv_sem=recv_sems.at[3 * n + a], device_id=(x, y, c), device_id_type=MESH).wait_recv()
        for cp in sends + joins:
            cp.wait_send()

    vmem = pl.BlockSpec(memory_space=pltpu.VMEM)
    return pl.pallas_call(
        body, name="reduce_over_chips",
        out_shape=[jax.ShapeDtypeStruct(s, F32) for s in SHARD_SHAPES],
        in_specs=[vmem] * n, out_specs=[vmem] * n,
        scratch_shapes=[pltpu.VMEM((3,) + _shard_half_shape(a), BF16) for a in range(n)]
        + [pltpu.SemaphoreType.DMA((4 * n,)), pltpu.SemaphoreType.DMA((4 * n,))],
        compiler_params=_cp(None, VMEM_LIMIT),
    )(*sums)


def _proj(x, w_in_p, gq, gkv, wuq_e, wukv, ct, st, cd, sd):
    seq = x.shape[0]
    tr = 512

    def body(x_ref, w_ref, gq_ref, gkv_ref, wuq_ref, wukv_ref, ct_ref, st_ref, cd_ref, sd_ref,
             cq_ref, ckv_ref, g_ref, qr_ref, kr_ref, vb_ref, q_out, k_out, v_out):
        lane = lax.broadcasted_iota(jnp.int32, (tr, LANES), 1)
        xb = x_ref[...].astype(BF16)
        cq = _dot(xb, w_ref[0:384, :], NT)
        ckv = _dot(xb, w_ref[384:640, :], NT)
        cq_ref[...] = cq
        ckv_ref[...] = ckv
        g_ref[...] = _dot(xb, w_ref[640:1664, :], NT)

        cd_, sd_ = cd_ref[...], sd_ref[...]
        qb = _dot(xb, w_ref[1664:2176, :], NT)
        kb = _dot(xb, w_ref[2176:2688, :], NT)
        for p in range(4):
            cols = slice(LANES * p, LANES * (p + 1))
            t = qb[:, cols]
            qr_ref[:, cols] = (t * cd_ + _dil_rot(t, lane) * sd_) * DIL_SCALE
            t = kb[:, cols]
            kr_ref[:, cols] = t * cd_ + _dil_rot(t, lane) * sd_
        vb_ref[...] = _dot(xb, w_ref[2688:3200, :], NT)

        ct_, st_ = ct_ref[...], st_ref[...]

        def rope(t):
            return t * ct_ + _mla_rot(t, lane) * st_

        _, qn = _rms(cq, gq_ref[...])
        q_all = _dot(qn.astype(BF16), wuq_ref[...])
        for h in range(MLA_HEADS):
            q_out[h] = (rope(q_all[:, LANES * h:LANES * (h + 1)]) * MLA_SCALE).astype(BF16)
        _, kvn = _rms(ckv, gkv_ref[...])
        kv_all = _dot(kvn.astype(BF16), wukv_ref[...])
        kpe = rope(_dot(xb, w_ref[3200:3328, :], NT))
        for h in range(MLA_HEADS):
            kv_h = kv_all[:, LANES * h:LANES * (h + 1)]
            k_out[h] = jnp.where(lane < 64, kv_h, kpe).astype(BF16)
            if h % 2:
                v = jnp.where(lane >= 64, kv_h, 0.0)
            else:
                v = jnp.where(lane < 64, pltpu.roll(kv_h, 64, 1), 0.0)
            v_out[h] = jnp.where(lane == ONES_LANE[h % 2], 1.0, v).astype(BF16)

    row = lambda w: pl.BlockSpec((tr, w), lambda i: (i, 0))
    full = lambda a: pl.BlockSpec(a.shape, lambda i: (0,) * a.ndim)
    head = pl.BlockSpec((MLA_HEADS, tr, LANES), lambda i: (0, i, 0))
    widths = (Q_LORA, KV_LORA, D_MODEL, 512, 512, 512)
    return pl.pallas_call(
        body, name="proj", grid=(seq // tr,),
        in_specs=[row(D_MODEL), full(w_in_p), full(gq), full(gkv), full(wuq_e), full(wukv)] + [row(LANES)] * 4,
        out_specs=[row(w) for w in widths] + [head] * 3,
        out_shape=[jax.ShapeDtypeStruct((seq, w), F32) for w in widths]
        + [jax.ShapeDtypeStruct((MLA_HEADS, seq, LANES), BF16)] * 3,
        compiler_params=_cp(("arbitrary",), VMEM_LIMIT),
    )(x, w_in_p, gq, gkv, wuq_e, wukv, ct, st, cd, sd)


def _mla_rot(t, lane):
    return jnp.where(lane < 80, pltpu.roll(t, 112, 1), pltpu.roll(t, 16, 1))


def _dil_rot(t, lane):
    return jnp.where(lane % 64 < 8, pltpu.roll(t, 120, 1), pltpu.roll(t, 8, 1))


def _rms(c, g):
    r = lax.rsqrt(jnp.mean(c * c, axis=-1, keepdims=True) + RMS_EPS)
    return r, c * r * g


def _mla_fwd(q, k, v):
    seq = q.shape[1]
    tq = 512
    nq = seq // tq

    def body(q_ref, k_ref, v_ref, o_ref, lse_ref, m_s, acc_s, s_buf):
        i = pl.program_id(1)
        row = lax.broadcasted_iota(jnp.int32, (tq, tq), 0)
        col = lax.broadcasted_iota(jnp.int32, (tq, tq), 1)
        lane = lax.broadcasted_iota(jnp.int32, (tq, LANES), 1)
        m_s[...] = jnp.full((2, tq, LANES), NEG, F32)
        acc_s[...] = jnp.zeros((2, tq, LANES), F32)

        def block(j):
            return pl.ds(pl.multiple_of(j * tq, tq), tq)

        def scores(hh, j):
            return _dot(q_ref[hh], k_ref[hh, block(j), :], NT)

        def consume(hh, j, s):
            m_prev = m_s[hh]
            m_new = jnp.maximum(m_prev, jnp.max(s, axis=1, keepdims=True))
            p = jnp.exp(s - m_new[:, :1])
            acc_s[hh] = jnp.exp(m_prev - m_new) * acc_s[hh] + _dot(p.astype(BF16), v_ref[hh, block(j), :])
            m_s[hh] = m_new

        for hh in range(2):
            s_buf[0, hh] = scores(hh, 0)

        def full_step(j, carry):
            slot = j & 1
            for hh in range(2):
                s = s_buf[slot, hh]
                s_buf[1 - slot, hh] = scores(hh, j + 1)
                consume(hh, j, s)
            return carry

        lax.fori_loop(0, i, full_step, 0)
        total = jnp.zeros((tq, LANES), F32)
        for hh in range(2):
            consume(hh, i, jnp.where(col <= row, s_buf[i & 1, hh], NEG))
            acc = acc_s[hh]
            l = acc[:, ONES_LANE[hh]:ONES_LANE[hh] + 1]
            mine = (lane >= 64) if hh else (lane < 64)
            total = total + jnp.where(mine, acc / l, 0.0)
            lse_ref[hh] = m_s[hh] + jnp.log(l)
        o_ref[...] = total

    kv_spec = pl.BlockSpec((2, seq, LANES), lambda p, i: (p, 0, 0))
    return pl.pallas_call(
        body, name="mla_fwd", grid=(MLA_HEADS // 2, nq),
        in_specs=[pl.BlockSpec((2, tq, LANES), lambda p, i: (p, i, 0)), kv_spec, kv_spec],
        out_specs=[pl.BlockSpec((tq, LANES), lambda p, i: (i, p)), pl.BlockSpec((2, tq, LANES), lambda p, i: (p, i, 0))],
        out_shape=[jax.ShapeDtypeStruct((seq, 4 * LANES), F32), jax.ShapeDtypeStruct((MLA_HEADS, seq, LANES), F32)],
        scratch_shapes=[pltpu.VMEM((2, tq, LANES), F32), pltpu.VMEM((2, tq, LANES), F32),
                        pltpu.VMEM((2, 2, tq, tq), F32)],
        compiler_params=_cp(("arbitrary", "arbitrary"), VMEM_LIMIT),
    )(q, k, v)


DIL_Q_FWD = 2 * BLOCK
DIL_Q_BWD = BLOCK


def _dil_tile_index(t, d, seq, nq):
    per_class = seq // (nq * d)
    shift = per_class.bit_length() - 1
    r = t >> shift
    n = t & (per_class - 1)
    start = r + (nq * d) * n
    prev = jnp.maximum(start - BLOCK * d, r)
    if d == 1:
        start = pl.multiple_of(start, nq)
        prev = pl.multiple_of(prev, BLOCK)
    return (n == 0).astype(jnp.int32), start, prev


def _dil_rows(start, d, size):
    return pl.ds(start, size) if d == 1 else pl.ds(start, size, stride=d)


def _dil_bias(nq):
    i = lax.broadcasted_iota(jnp.int32, (2 * nq, BLOCK + nq), 0) % nq
    j = lax.broadcasted_iota(jnp.int32, (2 * nq, BLOCK + nq), 1)
    band = (j >= i) & (j <= i + BLOCK)
    return jnp.where(band, 0.0, NEG), jnp.where(band & (j >= BLOCK), 0.0, NEG)


def _stack_heads(t, lane):
    return jnp.concatenate([jnp.where(lane < 64, t, 0.0), jnp.where(lane >= 64, t, 0.0)], axis=0)


def _unstack_heads(t, lane):
    nq = t.shape[0] // 2
    return jnp.where(lane < 64, t[:nq], t[nq:])


def _dil_fwd(qr, kr, vb):
    seq = qr.shape[0]
    nq = DIL_Q_FWD
    n_tiles = seq // nq
    assert seq % (nq * max(DIL_DILATIONS)) == 0

    def body(q_ref, k_ref, v_ref, o_ref, lse_ref, m_s, l_s, n_s, bias_s):
        lane = lax.broadcasted_iota(jnp.int32, (nq, LANES), 1)
        bias_s[0], bias_s[1] = _dil_bias(nq)
        for bi, d in enumerate(DIL_DILATIONS):

            def tile(t, carry, d=d, bi=bi):
                first, start, prev = _dil_tile_index(t, d, seq, nq)
                rows, prows = _dil_rows(start, d, nq), _dil_rows(prev, d, BLOCK)
                qst = _stack_heads(q_ref[rows, :], lane).astype(BF16)
                if seq == nq * d:
                    kcat, vcat = k_ref[rows, :].astype(BF16), v_ref[rows, :].astype(BF16)
                    s = _dot(qst, kcat, NT) + bias_s[1, :, BLOCK:]
                else:
                    kcat = jnp.concatenate([k_ref[prows, :], k_ref[rows, :]], axis=0).astype(BF16)
                    vcat = jnp.concatenate([v_ref[prows, :], v_ref[rows, :]], axis=0).astype(BF16)
                    s = _dot(qst, kcat, NT) + bias_s[first]
                m = jnp.max(s, axis=1, keepdims=True)
                p = jnp.exp(s - m)
                l2 = _unstack_heads(jnp.sum(p, axis=1, keepdims=True) + jnp.zeros((2 * nq, LANES), F32), lane)
                m2 = _unstack_heads(m + jnp.zeros((2 * nq, LANES), F32), lane)
                num2 = _unstack_heads(_dot(p.astype(BF16), vcat), lane)
                if bi == 0:
                    m_s[rows, :] = m2
                    l_s[rows, :] = l2
                    n_s[rows, :] = num2
                else:
                    m_old = m_s[rows, :]
                    m_new = jnp.maximum(m_old, m2)
                    a = jnp.exp(m_old - m_new)
                    b = jnp.exp(m2 - m_new)
                    m_s[rows, :] = m_new
                    l_s[rows, :] = a * l_s[rows, :] + b * l2
                    n_s[rows, :] = a * n_s[rows, :] + b * num2
                return carry

            lax.fori_loop(0, n_tiles, tile, 0, unroll=8)
        o_ref[...] = n_s[...] / l_s[...]
        lse_ref[...] = m_s[...] + jnp.log(l_s[...])

    col = lambda off: pl.BlockSpec((seq, LANES), lambda p: (0, p + off))
    return pl.pallas_call(
        body, name="dil_fwd", grid=(4,),
        in_specs=[col(0), col(0), col(0)],
        out_specs=[col(0), pl.BlockSpec((None, seq, LANES), lambda p: (p, 0, 0))],
        out_shape=[jax.ShapeDtypeStruct((seq, 4 * LANES), F32), jax.ShapeDtypeStruct((4, seq, LANES), F32)],
        scratch_shapes=[pltpu.VMEM((seq, LANES), F32)] * 3 + [pltpu.VMEM((2, 2 * nq, BLOCK + nq), F32)],
        compiler_params=_cp(("arbitrary",), VMEM_LIMIT),
    )(qr, kr, vb)


def _post(x, o_a, o_b, gates, w_out, ln_g, ln_b, target):
    seq = x.shape[0]
    tr = 512

    def body(x_ref, oa_ref, ob_ref, g_ref, w_ref, lg_ref, lb_ref, t_ref,
             dz_ref, do_ref, dg_ref, dw_ref, dlg_ref, dlb_ref, loss_ref):
        @pl.when(pl.program_id(0) == 0)
        def _():
            dw_ref[...] = jnp.zeros_like(dw_ref)
            dlg_ref[...] = jnp.zeros_like(dlg_ref)
            dlb_ref[...] = jnp.zeros_like(dlb_ref)
            loss_ref[...] = jnp.zeros_like(loss_ref)

        g = g_ref[...]
        sg = jax.nn.sigmoid(g)
        silu = g * sg
        o = jnp.concatenate([oa_ref[...], ob_ref[...]], axis=1)
        mixb = (o * silu).astype(BF16)
        w = w_ref[...]
        z = ALPHA * x_ref[...] + _dot(mixb, w)
        mu = jnp.mean(z, axis=-1, keepdims=True)
        zc = z - mu
        rstd = lax.rsqrt(jnp.mean(zc * zc, axis=-1, keepdims=True) + LN_EPS)
        xhat = zc * rstd
        lg = lg_ref[...]
        err = xhat * lg + lb_ref[...] - t_ref[...]
        loss_ref[...] += jnp.sum(err * err) * (0.5 / D_MODEL)
        dy = err * (1.0 / D_MODEL)
        dlg_ref[...] += jnp.sum(dy * xhat, axis=0, keepdims=True)
        dlb_ref[...] += jnp.sum(dy, axis=0, keepdims=True)
        dxh = dy * lg
        dz = rstd * (dxh - jnp.mean(dxh, axis=-1, keepdims=True) - xhat * jnp.mean(dxh * xhat, axis=-1, keepdims=True))
        dz_ref[...] = dz
        dzb = dz.astype(BF16)
        dmix = _dot(dzb, w, NT)
        do_ref[...] = dmix * silu
        dg_ref[...] = (dmix * o * (sg * (1.0 + g * (1.0 - sg)))).astype(BF16)
        dw_ref[...] += _dot(mixb, dzb, TN)

    row = lambda w: pl.BlockSpec((tr, w), lambda i: (i, 0))
    full = lambda s: pl.BlockSpec(s, lambda i: (0, 0))
    return pl.pallas_call(
        body, name="post", grid=(seq // tr,),
        in_specs=[row(D_MODEL), row(512), row(512), row(D_MODEL), full((D_MODEL, D_MODEL)), full((1, D_MODEL)),
                  full((1, D_MODEL)), row(D_MODEL)],
        out_specs=[row(D_MODEL), row(D_MODEL), row(D_MODEL), full((D_MODEL, D_MODEL)), full((1, D_MODEL)),
                   full((1, D_MODEL)), full((1, LANES))],
        out_shape=[jax.ShapeDtypeStruct((seq, D_MODEL), F32), jax.ShapeDtypeStruct((seq, D_MODEL), F32),
                   jax.ShapeDtypeStruct((seq, D_MODEL), BF16), jax.ShapeDtypeStruct((D_MODEL, D_MODEL), F32),
                   jax.ShapeDtypeStruct((1, D_MODEL), F32), jax.ShapeDtypeStruct((1, D_MODEL), F32),
                   jax.ShapeDtypeStruct((1, LANES), F32)],
        compiler_params=_cp(("arbitrary",), VMEM_LIMIT),
    )(x, o_a, o_b, gates, w_out, ln_g, ln_b, target)


def _mla_bwd(q, k, v, d_o, o, lse):
    seq = q.shape[1]
    tq = 512
    nq = seq // tq

    def body(q_ref, k_ref, v_ref, do_ref, o_ref, lse_ref, dq_ref, dk_ref, dv_ref, d_s, lse_s, dk_s, dv_s, v_s, kt_s, dqt_s):
        j = pl.program_id(1)
        lane = lax.broadcasted_iota(jnp.int32, (tq, LANES), 1)
        row = lax.broadcasted_iota(jnp.int32, (tq, tq), 0)
        col = lax.broadcasted_iota(jnp.int32, (tq, tq), 1)

        @pl.when(j == 0)
        def _():
            dqt_s[...] = jnp.zeros_like(dqt_s)

            def rowsum(i, carry):
                rows = pl.ds(pl.multiple_of(i * tq, tq), tq)
                prod = do_ref[rows, :] * o_ref[rows, :]
                for hh in range(2):
                    mine = (lane >= 64) if hh else (lane < 64)
                    total = jnp.sum(jnp.where(mine, prod, 0.0), axis=1, keepdims=True)
                    d_s[hh, i] = jnp.transpose(total + jnp.zeros((tq, LANES), F32))[:8]
                    lse_s[hh, i] = jnp.transpose(lse_ref[hh, rows, :])[:8]
                return carry

            lax.fori_loop(0, nq, rowsum, 0)

        dk_s[...] = jnp.zeros_like(dk_s)
        dv_s[...] = jnp.zeros_like(dv_s)
        for hh in range(2):
            v_s[hh] = jnp.where(lane == ONES_LANE[hh], 0.0, v_ref[hh].astype(F32)).astype(BF16)
            kt_s[hh] = jnp.transpose(k_ref[hh].astype(F32)).astype(BF16)

        def step(i, masked):
            rows = pl.ds(pl.multiple_of(i * tq, tq), tq)
            dob = do_ref[rows, :].astype(BF16)
            for hh in range(2):
                qb, kb, vb = q_ref[hh, rows, :], k_ref[hh], v_s[hh]
                p = jnp.exp(_dot(kb, qb, NT) - lse_s[hh, i][:1])
                if masked:
                    p = jnp.where(row <= col, p, 0.0)
                dv_s[hh] += _dot(p.astype(BF16), dob)
                ds = (p * (_dot(vb, dob, NT) - d_s[hh, i][:1])).astype(BF16)
                dk_s[hh] += _dot(ds, qb)
                dqt_s[hh, i] += _dot(kt_s[hh], ds)

        def full_step(i, carry):
            step(i, False)
            return carry

        step(j, True)
        lax.fori_loop(j + 1, nq, full_step, 0)
        dk_ref[...] = dk_s[...]
        dv_ref[...] = dv_s[...]

        @pl.when(j == nq - 1)
        def _():
            def untranspose(i, carry):
                rows = pl.ds(pl.multiple_of(i * tq, tq), tq)
                for hh in range(2):
                    dq_ref[hh, rows, :] = jnp.transpose(dqt_s[hh, i])
                return carry

            lax.fori_loop(0, nq, untranspose, 0)

    whole = pl.BlockSpec((2, seq, LANES), lambda p, j: (p, 0, 0))
    blk = pl.BlockSpec((2, tq, LANES), lambda p, j: (p, j, 0))
    pair = pl.BlockSpec((seq, LANES), lambda p, j: (0, p))
    shape = jax.ShapeDtypeStruct((MLA_HEADS, seq, LANES), F32)
    return pl.pallas_call(
        body, name="mla_bwd", grid=(MLA_HEADS // 2, nq),
        in_specs=[whole, blk, blk, pair, pair, whole],
        out_specs=[whole, blk, blk], out_shape=[shape] * 3,
        scratch_shapes=[pltpu.VMEM((2, nq, 8, tq), F32), pltpu.VMEM((2, nq, 8, tq), F32),
                        pltpu.VMEM((2, tq, LANES), F32), pltpu.VMEM((2, tq, LANES), F32),
                        pltpu.VMEM((2, tq, LANES), BF16), pltpu.VMEM((2, LANES, tq), BF16),
                        pltpu.VMEM((2, nq, LANES, tq), F32)],
        compiler_params=_cp(("arbitrary", "arbitrary"), VMEM_LIMIT),
    )(q, k, v, d_o, o, lse)


def _dil_bwd(qr, kr, vb, d_o, o, lse):
    seq = qr.shape[0]
    nq = DIL_Q_BWD
    n_tiles = seq // nq
    chunk = 512

    def body(q_ref, k_ref, v_ref, do_ref, o_ref, lse_ref, dq_ref, dk_ref, dv_ref, d_s, dq_s, dk_s, dv_s, bias_s):
        lane = lax.broadcasted_iota(jnp.int32, (nq, LANES), 1)
        lanec = lax.broadcasted_iota(jnp.int32, (chunk, LANES), 1)
        bias_s[0], bias_s[1] = [b[:nq] for b in _dil_bias(nq)]

        def rowsum(i, carry):
            rows = pl.ds(pl.multiple_of(i * chunk, chunk), chunk)
            prod = do_ref[rows, :] * o_ref[rows, :]
            lo = jnp.sum(jnp.where(lanec < 64, prod, 0.0), axis=1, keepdims=True)
            hi = jnp.sum(jnp.where(lanec >= 64, prod, 0.0), axis=1, keepdims=True)
            d_s[rows, :] = jnp.where(lanec < 64, lo, hi)
            return carry

        lax.fori_loop(0, seq // chunk, rowsum, 0)
        dq_s[...] = jnp.zeros_like(dq_s)
        dk_s[...] = jnp.zeros_like(dk_s)
        dv_s[...] = jnp.zeros_like(dv_s)
        for d in DIL_DILATIONS:

            def tile(start, prev, first, d=d):
                rows = _dil_rows(start, d, nq)
                q_t, do_t = q_ref[rows, :], do_ref[rows, :]
                lse_t, d_t = lse_ref[rows, :], d_s[rows, :]
                if prev is None:
                    kcat, vcat = k_ref[rows, :].astype(BF16), v_ref[rows, :].astype(BF16)
                    bias = bias_s[1, :, BLOCK:]
                else:
                    prows = _dil_rows(prev, d, BLOCK)
                    kcat = jnp.concatenate([k_ref[prows, :], k_ref[rows, :]], axis=0).astype(BF16)
                    vcat = jnp.concatenate([v_ref[prows, :], v_ref[rows, :]], axis=0).astype(BF16)
                    bias = bias_s[first]
                dq_t = jnp.zeros((nq, LANES), F32)
                dkcat = jnp.zeros((kcat.shape[0], LANES), F32)
                dvcat = jnp.zeros((kcat.shape[0], LANES), F32)
                for hh in range(2):
                    mine = (lane >= 64) if hh else (lane < 64)
                    c0 = 64 * hh
                    qh = jnp.where(mine, q_t, 0.0).astype(BF16)
                    doh = jnp.where(mine, do_t, 0.0).astype(BF16)
                    p = jnp.exp(_dot(qh, kcat, NT) + bias - lse_t[:, c0:c0 + 1])
                    dvcat = dvcat + _dot(p.astype(BF16), doh, TN)
                    dp = _dot(doh, vcat, NT)
                    ds = (p * (dp - d_t[:, c0:c0 + 1])).astype(BF16)
                    dq_t = dq_t + jnp.where(mine, _dot(ds, kcat), 0.0)
                    dkcat = dkcat + _dot(ds, qh, TN)
                dq_s[rows, :] += dq_t
                if prev is not None:
                    dk_s[prows, :] += dkcat[:BLOCK]
                    dv_s[prows, :] += dvcat[:BLOCK]
                dk_s[rows, :] += dkcat[-nq:]
                dv_s[rows, :] += dvcat[-nq:]

            if seq == 2 * nq * d:

                def class_tiles(r, carry, d=d):
                    tile(r, None, 1)
                    tile(r + nq * d, r, 0)
                    return carry

                lax.fori_loop(0, d, class_tiles, 0, unroll=8)
            else:

                def any_tile(t, carry, d=d):
                    first, start, prev = _dil_tile_index(t, d, seq, nq)
                    tile(start, prev, first)
                    return carry

                lax.fori_loop(0, n_tiles, any_tile, 0, unroll=16)
        dq_ref[...] = dq_s[...].astype(BF16)
        dk_ref[...] = dk_s[...].astype(BF16)
        dv_ref[...] = dv_s[...].astype(BF16)

    col = lambda off: pl.BlockSpec((seq, LANES), lambda p: (0, p + off))
    shape = jax.ShapeDtypeStruct((seq, 4 * LANES), BF16)
    return pl.pallas_call(
        body, name="dil_bwd", grid=(4,),
        in_specs=[col(0), col(0), col(0), col(4), col(0), pl.BlockSpec((None, seq, LANES), lambda p: (p, 0, 0))],
        out_specs=[col(0)] * 3, out_shape=[shape] * 3,
        scratch_shapes=[pltpu.VMEM((seq, LANES), F32)] * 4 + [pltpu.VMEM((2, nq, BLOCK + nq), F32)],
        compiler_params=_cp(("arbitrary",), VMEM_LIMIT),
    )(qr, kr, vb, d_o, o, lse)


def _in_bwd(dz, cq, ckv, gq, gkv, wuq_e, wukv, ct, st, dq, dk, dv, dgates, dqr, dkr, dvb, cd, sd, w_in_p):
    seq = dz.shape[0]
    tr = 512

    def body(dz_ref, cq_ref, ckv_ref, gq_ref, gkv_ref, wuq_ref, wukv_ref, ct_ref, st_ref, dq_ref, dk_ref, dv_ref,
             dg_ref, dqr_ref, dkr_ref, dvb_ref, cd_ref, sd_ref, w_ref,
             gx_ref, dh_ref, dwuq_ref, dwukv_ref, dgq_ref, dgkv_ref):
        @pl.when(pl.program_id(0) == 0)
        def _():
            dwuq_ref[...] = jnp.zeros_like(dwuq_ref)
            dwukv_ref[...] = jnp.zeros_like(dwukv_ref)
            dgq_ref[...] = jnp.zeros_like(dgq_ref)
            dgkv_ref[...] = jnp.zeros_like(dgkv_ref)

        lane = lax.broadcasted_iota(jnp.int32, (tr, LANES), 1)
        rope_lanes = jnp.logical_and(lane >= 64, lane < 96)
        ct_, st_ = ct_ref[...], st_ref[...]

        def mla_rope_t(g):
            return ct_ * g + jnp.where(rope_lanes, _mla_rot(st_ * g, lane), 0.0)

        def norm_bwd(c, g, dn, dg_ref):
            r, _ = _rms(c, g)
            u = dn * g
            dg_ref[...] += jnp.sum(dn * c * r, axis=0, keepdims=True)
            return r * u - c * (r * r * r) * jnp.mean(u * c, axis=-1, keepdims=True)

        c, g = cq_ref[...], gq_ref[...]
        _, qn = _rms(c, g)
        dq_all = jnp.concatenate([mla_rope_t(dq_ref[h] * MLA_SCALE) for h in range(MLA_HEADS)], axis=1).astype(BF16)
        dwuq_ref[...] += _dot(qn.astype(BF16), dq_all, TN)
        dcq = norm_bwd(c, g, _dot(dq_all, wuq_ref[...], NT), dgq_ref).astype(BF16)

        c, g = ckv_ref[...], gkv_ref[...]
        _, kvn = _rms(c, g)
        dkpe = jnp.zeros((tr, LANES), F32)
        parts = []
        for h in range(MLA_HEADS):
            dk_h, dv_h = dk_ref[h], dv_ref[h]
            if h % 2 == 0:
                dv_h = pltpu.roll(dv_h, 64, 1)
            parts.append(jnp.where(lane < 64, dk_h, dv_h))
            dkpe = dkpe + jnp.where(rope_lanes, dk_h, 0.0)
        dkv_all = jnp.concatenate(parts, axis=1).astype(BF16)
        dwukv_ref[...] += _dot(kvn.astype(BF16), dkv_all, TN)
        dckv = norm_bwd(c, g, _dot(dkv_all, wukv_ref[...], NT), dgkv_ref).astype(BF16)
        dkrope = mla_rope_t(dkpe).astype(BF16)

        rot_lanes = lane % 64 < DIL_ROT
        cd_, sd_ = cd_ref[...], sd_ref[...]

        def dil_rope_t(g):
            return cd_ * g + jnp.where(rot_lanes, _dil_rot(sd_ * g, lane), 0.0)

        dqb = [dil_rope_t(dqr_ref[:, LANES * p:LANES * (p + 1)].astype(F32) * DIL_SCALE).astype(BF16) for p in range(4)]
        dkb = [dil_rope_t(dkr_ref[:, LANES * p:LANES * (p + 1)].astype(F32)).astype(BF16) for p in range(4)]
        dh = jnp.concatenate([dcq, dckv, dg_ref[...]] + dqb + dkb + [dvb_ref[...], dkrope], axis=1)
        dh_ref[...] = dh
        gx_ref[...] = ALPHA * dz_ref[...] + _dot(dh, w_ref[...])

    row = lambda w: pl.BlockSpec((tr, w), lambda i: (i, 0))
    full = lambda a: pl.BlockSpec(a.shape, lambda i: (0,) * a.ndim)
    head = pl.BlockSpec((MLA_HEADS, tr, LANES), lambda i: (0, i, 0))
    return pl.pallas_call(
        body, name="in_bwd", grid=(seq // tr,),
        in_specs=[row(D_MODEL), row(Q_LORA), row(KV_LORA), full(gq), full(gkv), full(wuq_e), full(wukv), row(LANES),
                  row(LANES), head, head, head, row(D_MODEL), row(512), row(512), row(512), row(LANES), row(LANES),
                  full(w_in_p)],
        out_specs=[row(D_MODEL), row(IN_WIDTH_PAD), full(wuq_e), full(wukv), full(gq), full(gkv)],
        out_shape=[jax.ShapeDtypeStruct((seq, D_MODEL), F32), jax.ShapeDtypeStruct((seq, IN_WIDTH_PAD), BF16),
                   jax.ShapeDtypeStruct(wuq_e.shape, F32), jax.ShapeDtypeStruct(wukv.shape, F32),
                   jax.ShapeDtypeStruct(gq.shape, F32), jax.ShapeDtypeStruct(gkv.shape, F32)],
        compiler_params=_cp(("arbitrary",), VMEM_LIMIT),
    )(dz, cq, ckv, gq, gkv, wuq_e, wukv, ct, st, dq, dk, dv, dgates, dqr, dkr, dvb, cd, sd, w_in_p)


def _dw_in(x, dh):
    seq = dh.shape[0]
    tk = 512
    tn = IN_WIDTH_PAD // 2

    def body(x_ref, dh_ref, o_ref):
        @pl.when(pl.program_id(1) == 0)
        def _():
            o_ref[...] = jnp.zeros_like(o_ref)

        o_ref[...] += _dot(dh_ref[...], x_ref[...].astype(BF16), TN)

    return pl.pallas_call(
        body, name="dw_in", grid=(2, seq // tk),
        in_specs=[pl.BlockSpec((tk, D_MODEL), lambda n, k: (k, 0)), pl.BlockSpec((tk, tn), lambda n, k: (k, n))],
        out_specs=pl.BlockSpec((tn, D_MODEL), lambda n, k: (n, 0)),
        out_shape=jax.ShapeDtypeStruct((IN_WIDTH_PAD, D_MODEL), F32),
        compiler_params=_cp(("arbitrary", "arbitrary"), VMEM_LIMIT),
    )(x, dh)


def _adam_update(w, g, m, v):
    nm = ADAM_B1 * m + (1.0 - ADAM_B1) * g
    nv = ADAM_B2 * v + (1.0 - ADAM_B2) * jnp.square(g)
    m_hat = nm / (1.0 - ADAM_B1 ** ADAM_STEP)
    v_hat = nv / (1.0 - ADAM_B2 ** ADAM_STEP)
    return -ADAM_LR * (m_hat / (jnp.sqrt(v_hat) + ADAM_EPS) + ADAM_WD * w), nm, nv


def _adamw(w, g, m, v, name):
    rows, cols = w.shape
    tc = 256 if cols % 256 == 0 and rows * cols > 2 ** 18 else cols

    def body(w_ref, g_ref, m_ref, v_ref, d_ref, nm_ref, nv_ref):
        d_ref[...], nm_ref[...], nv_ref[...] = _adam_update(w_ref[...], g_ref[...], m_ref[...], v_ref[...])

    spec = pl.BlockSpec((rows, tc), lambda i: (0, i))
    return pl.pallas_call(
        body, name=name, grid=(cols // tc,), in_specs=[spec] * 4, out_specs=[spec] * 3,
        out_shape=[jax.ShapeDtypeStruct(w.shape, F32)] * 3, compiler_params=_cp(("arbitrary",)),
    )(w, g, m, v)


def _adamw_vectors(small_sum, ws, ms, vs):
    k = len(ws)
    sizes = [w.shape[-1] for w in ws]

    def body(s_ref, *refs):
        ins, outs = refs[:3 * k], refs[3 * k:]
        for i, size in enumerate(sizes):
            g = s_ref[i:i + 1, 0:size]
            outs[i][...] = g
            outs[k + i][...], outs[2 * k + i][...], outs[3 * k + i][...] = _adam_update(
                ins[i][...], g, ins[k + i][...], ins[2 * k + i][...])

    out = pl.pallas_call(
        body, name="adamw_vectors", out_shape=[jax.ShapeDtypeStruct((1, size), F32) for size in sizes] * 4,
    )(small_sum, *[a.reshape(1, -1) for a in list(ws) + list(ms) + list(vs)])
    return [[a.reshape(-1) for a in out[k * j:k * (j + 1)]] for j in range(4)]


def _local_step(x2, target, w_in_p, w_uq_f, wukv_f, w_out_f, q_norm_g, kv_norm_g, ln_g, ln_b):
    seq = x2.shape[0]
    wuq_e = jnp.pad(w_uq_f.reshape(Q_LORA, MLA_HEADS, 96), ((0, 0), (0, 0), (0, 32))).reshape(Q_LORA, MLA_HEADS * LANES)
    ct, st, cd, sd = _rope_tables(seq)
    gq = q_norm_g.reshape(1, Q_LORA)
    gkv = kv_norm_g.reshape(1, KV_LORA)

    cq, ckv, gates, qr, krot, vb, q_e, k_e, v_e = _proj(x2, w_in_p, gq, gkv, wuq_e, wukv_f, ct, st, cd, sd)
    o_a, lse_a = _mla_fwd(q_e, k_e, v_e)
    o_b, lse_b = _dil_fwd(qr, krot, vb)

    dz, d_o, d_gates, dw_out, dln_g, dln_b, loss_part = _post(
        x2, o_a, o_b, gates, w_out_f, ln_g.reshape(1, D_MODEL), ln_b.reshape(1, D_MODEL), target)
    dq_e, dk_e, dv_e = _mla_bwd(q_e, k_e, v_e, d_o, o_a, lse_a)
    dqr, dkr, dvb = _dil_bwd(qr, krot, vb, d_o, o_b, lse_b)
    grad_x, dh, dwuq_e, dwukv, dgq, dgkv = _in_bwd(
        dz, cq, ckv, gq, gkv, wuq_e, wukv_f, ct, st, dq_e, dk_e, dv_e, d_gates, dqr, dkr, dvb, cd, sd, w_in_p)
    dw_in = _dw_in(x2, dh)
    dw_uq = dwuq_e.reshape(Q_LORA, MLA_HEADS, LANES)[:, :, :96].reshape(Q_LORA, MLA_HEADS * 96)
    return loss_part, grad_x, dw_in, dw_uq, dwukv, dw_out, dgq, dgkv, dln_g, dln_b


def kernel(x, w_in, q_norm_g, kv_norm_g, w_uq, w_ukv, w_out, ln_g, ln_b, loss_target, m_w_in, m_q_norm_g, m_kv_norm_g, m_w_uq, m_w_ukv, m_w_out, m_ln_g, m_ln_b, v_w_in, v_q_norm_g, v_kv_norm_g, v_w_uq, v_w_ukv, v_w_out, v_ln_g, v_ln_b):
    seq = x.shape[1]
    x2 = x.reshape(seq, D_MODEL)
    target = loss_target.reshape(seq, D_MODEL)

    g_w_in, g_w_uq, g_w_ukv, g_w_out = _all_gather_weights([w_in.T, w_uq, w_ukv, w_out])
    by_cols = lambda g: jnp.concatenate([g[j] for j in range(N_SHARD)], axis=1)
    loss_part, grad_x, dw_in, dw_uq, dwukv, dw_out, dgq, dgkv, dln_g, dln_b = _local_step(
        x2, target, g_w_in, by_cols(g_w_uq), by_cols(g_w_ukv), g_w_out.reshape(D_MODEL, D_MODEL),
        q_norm_g, kv_norm_g, ln_g, ln_b)

    to_shards = lambda d: d.reshape(d.shape[0], N_SHARD, d.shape[1] // N_SHARD).transpose(1, 0, 2)
    grads = [dw_in, to_shards(dw_uq), to_shards(dwukv), dw_out.reshape(N_SHARD, 256, D_MODEL)]
    *chip_sums, small_sum = _reduce_over_sibling(grads, [dgq, dgkv, dln_g, dln_b, loss_part])
    g_in_t, g_uq, g_ukv, g_out = _reduce_over_chips(chip_sums)
    g_in = g_in_t.T
    loss = small_sum[4, 0]

    big = [[o.T for o in _adamw(w.T, g.T, m.T, v.T, name)] for w, g, m, v, name in (
        (w_in, g_in, m_w_in, v_w_in, "adamw_w_in"), (w_uq, g_uq, m_w_uq, v_w_uq, "adamw_w_uq"))]
    big += [_adamw(w, g, m, v, name) for w, g, m, v, name in (
        (w_ukv, g_ukv, m_w_ukv, v_w_ukv, "adamw_w_ukv"), (w_out, g_out, m_w_out, v_w_out, "adamw_w_out"))]
    vec_g, vec_delta, vec_m, vec_v = _adamw_vectors(
        small_sum, [q_norm_g, kv_norm_g, ln_g, ln_b], [m_q_norm_g, m_kv_norm_g, m_ln_g, m_ln_b],
        [v_q_norm_g, v_kv_norm_g, v_ln_g, v_ln_b])

    def ordered(bigs, vecs):
        return [bigs[0], vecs[0], vecs[1], bigs[1], bigs[2], bigs[3], vecs[2], vecs[3]]

    grads_out = ordered([g_in, g_uq, g_ukv, g_out], vec_g)
    deltas = ordered([b[0] for b in big], vec_delta)
    new_m = ordered([b[1] for b in big], vec_m)
    new_v = ordered([b[2] for b in big], vec_v)
    return (loss, grad_x.reshape(x.shape), *grads_out, *deltas, *new_m, *new_v)
```

```python
import jax
import jax.numpy as jnp
import numpy as np
from jax import lax
from jax.experimental import pallas as pl
from jax.experimental.pallas import tpu as pltpu

F32 = jnp.float32
BF16 = jnp.bfloat16

D_MODEL = 1024
ROPE_THETA = 500000.0
BLOCK = 128
NEG = -1e30
RMS_EPS = 1e-6
LN_EPS = 1e-5

MLA_HEADS = 8
MLA_NOPE = 64
MLA_ROPE = 32
Q_LORA = 384
KV_LORA = 256
DIL_HEAD_DIM = 64
DIL_ROT = 16
DIL_DILATIONS = (1, 4, 16)
IN_WIDTH_PAD = 3328
ONES_LANE = (64, 0)
MLA_SCALE = (MLA_NOPE + MLA_ROPE) ** -0.5
DIL_SCALE = DIL_HEAD_DIM ** -0.5
ALPHA = 2.0 ** 0.25

ADAM_LR = 0.001
ADAM_B1 = 0.9
ADAM_B2 = 0.999
ADAM_EPS = 1e-08
ADAM_WD = 0.01
ADAM_STEP = 10

N_SHARD = 4
SHARD_SHAPES = ((808, 1024), (384, 192), (256, 256), (256, 1024))
SHARD_SPLIT_COLS = (True, False, False, False)
ROW_CHUNK = 64
LANES = 128
VMEM_LIMIT = 56 * 1024 * 1024
MESH = pl.DeviceIdType.MESH

NT = (((1,), (1,)), ((), ()))
TN = (((0,), (0,)), ((), ()))


def _cp(sem=None, vmem=None):
    return pltpu.CompilerParams(dimension_semantics=sem, vmem_limit_bytes=vmem)


def _dot(a, b, dims=None):
    if dims is None:
        return jnp.dot(a, b, preferred_element_type=F32)
    return lax.dot_general(a, b, dims, preferred_element_type=F32)


def _rope_tables(seq):
    f32 = np.float32
    pos = np.arange(seq, dtype=f32)[:, None]
    one, zero = np.ones((seq, 64), f32), np.zeros((seq, 64), f32)

    def cos_sin(dim):
        inv = np.power(f32(ROPE_THETA), -np.arange(0, dim, 2, dtype=f32) / f32(dim)).astype(f32)
        ang = (pos * inv[None, :]).astype(f32)
        return np.cos(ang).astype(f32), np.sin(ang).astype(f32)

    cos, sin = cos_sin(MLA_ROPE)
    ct = np.concatenate([one, cos, cos, zero[:, :32]], axis=1)
    st = np.concatenate([zero, -sin, sin, zero[:, :32]], axis=1)
    cos, sin = cos_sin(DIL_ROT)
    cd = np.concatenate([cos, cos, one[:, :48]], axis=1)
    sd = np.concatenate([-sin, sin, zero[:, :48]], axis=1)
    return tuple(jnp.asarray(t) for t in (ct, st, np.tile(cd, (1, 2)), np.tile(sd, (1, 2))))


W_IN_ORDER = ((0, 640), (672, 1184), (2720, 3232), (1184, 2720), None, (640, 672))


def _permute_w_in_t(w_t):
    z = jnp.zeros((64, w_t.shape[1]), w_t.dtype)
    parts = [z if r is None else w_t[r[0]:r[1]] for r in W_IN_ORDER]
    return jnp.concatenate(parts + [z[:32]], axis=0)


def _w_in_row_pieces():
    width = SHARD_SHAPES[0][0]
    pieces, at = [], 0
    for r in W_IN_ORDER:
        if r is None:
            at += 64
            continue
        for k in range(N_SHARD):
            lo, hi = max(r[0], width * k), min(r[1], width * (k + 1))
            if lo < hi:
                pieces.append((k, lo - width * k, at + lo - r[0], hi - lo))
        at += r[1] - r[0]
    return pieces


def _position():
    return lax.axis_index("x"), lax.axis_index("y"), lax.axis_index("c")


def _all_gather_weights(shards):
    n = len(shards)

    def body(*refs):
        ins, outs = refs[:n], list(refs[n:2 * n])
        w_in_p, outs[0] = outs[0], refs[2 * n]
        send_sems, recv_sems = refs[2 * n + 1:]
        x, y, c = _position()
        me = 2 * x + y
        chips = [(1 - x, y), (x, 1 - y), (1 - x, 1 - y)]
        for a in range(n):
            rows, cols = SHARD_SHAPES[a]
            if SHARD_SPLIT_COLS[a]:
                blocks = [(slice(None), slice(c0, c0 + LANES)) for c0 in range(0, cols, LANES)]
            else:
                blocks = [(slice(r0, r0 + ROW_CHUNK), slice(None)) for r0 in range(0, rows, ROW_CHUNK)]
            for blk in blocks:
                outs[a][(me,) + blk] = ins[a][blk].astype(BF16)

        def copy(k, a, slot, part, to):
            ref = outs[a].at[(slot,) + part]
            return pltpu.make_async_remote_copy(
                src_ref=ref, dst_ref=ref, send_sem=send_sems.at[k * n + a], recv_sem=recv_sems.at[k * n + a],
                device_id=to, device_id_type=MESH)

        half = [_shard_half(a, c) for a in range(n)]
        other = [_shard_half(a, 1 - c) for a in range(n)]
        first = [copy(k, a, me, half[a], (px, py, c)) for k, (px, py) in enumerate(chips) for a in range(n)]
        for cp in first:
            cp.start()
        passed = []
        for k, (px, py) in enumerate(chips):
            for a in range(n):
                copy(k, a, 2 * px + py, half[a], (x, y, c)).wait_recv()
                cp = copy(3 + k, a, 2 * px + py, half[a], (x, y, 1 - c))
                cp.start()
                passed.append(cp)
        for k, (px, py) in enumerate(chips):
            for a in range(n):
                copy(3 + k, a, 2 * px + py, other[a], (x, y, c)).wait_recv()
        for cp in first + passed:
            cp.wait_send()

        written = []
        for k, r0, at, rows in _w_in_row_pieces():
            written.append((at, at + rows))
            for r in range(0, rows, 2 * LANES):
                m = min(2 * LANES, rows - r)
                for c0 in range(0, D_MODEL, LANES):
                    w_in_p[at + r:at + r + m, c0:c0 + LANES] = outs[0][k, r0 + r:r0 + r + m, c0:c0 + LANES]
        for lo, hi in zip([0] + [w[1] for w in sorted(written)], [w[0] for w in sorted(written)] + [IN_WIDTH_PAD]):
            if lo < hi:
                w_in_p[lo:hi, :] = jnp.zeros((hi - lo, D_MODEL), BF16)

    vmem = pl.BlockSpec(memory_space=pltpu.VMEM)
    return pl.pallas_call(
        body, name="all_gather_weights",
        out_shape=[jax.ShapeDtypeStruct((IN_WIDTH_PAD, D_MODEL), BF16)]
        + [jax.ShapeDtypeStruct((N_SHARD,) + s, BF16) for s in SHARD_SHAPES[1:]],
        in_specs=[vmem] * n, out_specs=[vmem] * n,
        scratch_shapes=[pltpu.VMEM((N_SHARD,) + SHARD_SHAPES[0], BF16),
                        pltpu.SemaphoreType.DMA((6 * n,)), pltpu.SemaphoreType.DMA((6 * n,))],
        compiler_params=_cp(None, VMEM_LIMIT),
    )(*shards)


def _shard_half_shape(a):
    rows, cols = SHARD_SHAPES[a]
    return (rows, cols // 2) if SHARD_SPLIT_COLS[a] else (rows // 2, cols)


def _shard_half(a, c):
    rows, cols = SHARD_SHAPES[a]
    if SHARD_SPLIT_COLS[a]:
        return slice(None), pl.ds(pl.multiple_of(c * (cols // 2), LANES), cols // 2)
    return pl.ds(pl.multiple_of(c * (rows // 2), ROW_CHUNK), rows // 2), slice(None)


def _shard_chunks(a, c):
    rows, cols = SHARD_SHAPES[a]
    if SHARD_SPLIT_COLS[a]:
        return [((slice(None), pl.ds(c0, LANES)),
                 (slice(None), pl.ds(pl.multiple_of(c * (cols // 2) + c0, LANES), LANES)))
                for c0 in range(0, cols // 2, LANES)]
    return [((pl.ds(r0, ROW_CHUNK), slice(None)),
             (pl.ds(pl.multiple_of(c * (rows // 2) + r0, ROW_CHUNK), ROW_CHUNK), slice(None)))
            for r0 in range(0, rows // 2, ROW_CHUNK)]


def _reduce_gradients(grads, small_rows):
    n = len(grads)
    n_small = len(small_rows)
    pieces = _w_in_row_pieces()
    order = sorted(range(n), key=lambda a: SHARD_SHAPES[a][0] * SHARD_SHAPES[a][1])

    def body(*refs):
        g_hbm, rows_in = refs[:n], refs[n:n + n_small]
        outs, small_sum = refs[n + n_small:2 * n + n_small], refs[2 * n + n_small]
        scratch = refs[2 * n + n_small + 1:]
        stage, got, sums, others = (scratch[i * n:(i + 1) * n] for i in range(4))
        sm, smalls, send_sems, recv_sems, local_sems = scratch[4 * n:]
        swap_sem, chip_sem, join_sem, small_sem = 0, n, 4 * n, 5 * n
        x, y, c = _position()
        me = 4 * x + 2 * y + c
        chips = [(1 - x, y), (x, 1 - y), (1 - x, 1 - y)]
        sm[...] = jnp.zeros_like(sm)
        for i, row in enumerate(rows_in):
            sm[i:i + 1, 0:row.shape[1]] = row[...]
        loads = [[pltpu.make_async_copy(g_hbm[0].at[pl.ds(src, rows)], stage[0].at[k, pl.ds(dst, rows)],
                                        local_sems.at[n + i])
                  for i, (k, dst, src, rows) in enumerate(pieces)]]
        loads += [[pltpu.make_async_copy(g_hbm[a], stage[a], local_sems.at[a])] for a in range(1, n)]
        for a in order:
            for ld in loads[a]:
                ld.start()
        smalls[me] = sm[...]
        small_sends = []
        for rel in range(1, 8):
            px = 1 - x if rel // 4 else x
            py = 1 - y if (rel // 2) % 2 else y
            pc = 1 - c if rel % 2 else c
            cp = pltpu.make_async_remote_copy(
                src_ref=sm, dst_ref=smalls.at[me], send_sem=send_sems.at[small_sem + rel],
                recv_sem=recv_sems.at[small_sem + rel], device_id=(px, py, pc), device_id_type=MESH)
            cp.start()
            small_sends.append((cp, 4 * px + 2 * py + pc))
        swaps = {}
        for a in order:
            for ld in loads[a]:
                ld.wait()
            swaps[a] = pltpu.make_async_remote_copy(
                src_ref=stage[a].at[(slice(None),) + _shard_half(a, 1 - c)], dst_ref=got[a],
                send_sem=send_sems.at[swap_sem + a], recv_sem=recv_sems.at[swap_sem + a],
                device_id=(x, y, 1 - c), device_id_type=MESH)
            swaps[a].start()
        sends = {}
        for a in order:
            swaps[a].wait_recv()
            for k in range(N_SHARD):
                for in_half, in_whole in _shard_chunks(a, c):
                    pair = stage[a][(k,) + in_whole] + got[a][(k,) + in_half]
                    sums[a][(k,) + in_half] = pair.astype(BF16)
            sends[a] = [pltpu.make_async_remote_copy(
                src_ref=sums[a].at[2 * px + py], dst_ref=others[a].at[k], send_sem=send_sems.at[chip_sem + k * n + a],
                recv_sem=recv_sems.at[chip_sem + k * n + a], device_id=(px, py, c), device_id_type=MESH)
                for k, (px, py) in enumerate(chips)]
            for cp in sends[a]:
                cp.start()
        joins = []
        for a in order:
            for cp in sends[a]:
                cp.wait_recv()
            for in_half, in_whole in _shard_chunks(a, c):
                total = sums[a][(2 * x + y,) + in_half].astype(F32)
                for k in range(3):
                    total = total + others[a][(k,) + in_half].astype(F32)
                outs[a][in_whole] = total
            half = outs[a].at[_shard_half(a, c)]
            cp = pltpu.make_async_remote_copy(
                src_ref=half, dst_ref=half, send_sem=send_sems.at[join_sem + a],
                recv_sem=recv_sems.at[join_sem + a], device_id=(x, y, 1 - c), device_id_type=MESH)
            cp.start()
            joins.append(cp)
        for rel, (cp, peer) in enumerate(small_sends, start=1):
            pltpu.make_async_remote_copy(
                src_ref=sm, dst_ref=smalls.at[peer], send_sem=send_sems.at[small_sem + rel],
                recv_sem=recv_sems.at[small_sem + rel], device_id=(x, y, c), device_id_type=MESH).wait_recv()
        total = smalls[0]
        for dev in range(1, 8):
            total = total + smalls[dev]
        small_sum[...] = total
        for a in order:
            other = outs[a].at[_shard_half(a, 1 - c)]
            pltpu.make_async_remote_copy(
                src_ref=other, dst_ref=other, send_sem=send_sems.at[join_sem + a],
                recv_sem=recv_sems.at[join_sem + a], device_id=(x, y, c), device_id_type=MESH).wait_recv()
        for cp in list(swaps.values()) + [cp for a in order for cp in sends[a]] + joins + [cp for cp, _ in small_sends]:
            cp.wait_send()

    vmem = pl.BlockSpec(memory_space=pltpu.VMEM)
    halves = [_shard_half_shape(a) for a in range(n)]
    return pl.pallas_call(
        body, name="reduce_gradients",
        out_shape=[jax.ShapeDtypeStruct(s, F32) for s in SHARD_SHAPES] + [jax.ShapeDtypeStruct((8, D_MODEL), F32)],
        in_specs=[pl.BlockSpec(memory_space=pl.ANY)] * n + [vmem] * n_small, out_specs=[vmem] * (n + 1),
        scratch_shapes=[pltpu.VMEM((N_SHARD,) + s, F32) for s in SHARD_SHAPES]
        + [pltpu.VMEM((N_SHARD,) + s, F32) for s in halves] + [pltpu.VMEM((N_SHARD,) + s, BF16) for s in halves]
        + [pltpu.VMEM((3,) + s, BF16) for s in halves]
        + [pltpu.VMEM((8, D_MODEL), F32), pltpu.VMEM((8, 8, D_MODEL), F32),
           pltpu.SemaphoreType.DMA((5 * n + 8,)), pltpu.SemaphoreType.DMA((5 * n + 8,)),
           pltpu.SemaphoreType.DMA((n + len(pieces),))],
        compiler_params=_cp(None, VMEM_LIMIT),
    )(*grads, *small_rows)


def _proj(x, w_in_p, gq, gkv, wuq_e, wukv, ct, st, cd, sd):
    seq = x.shape[0]
    tr = 512

    def body(x_ref, w_ref, gq_ref, gkv_ref, wuq_ref, wukv_ref, ct_ref, st_ref, cd_ref, sd_ref,
             cq_ref, ckv_ref, g_ref, qr_ref, kr_ref, vb_ref, q_out, k_out, v_out):
        lane = lax.broadcasted_iota(jnp.int32, (tr, LANES), 1)
        xb = x_ref[...].astype(BF16)
        cq = _dot(xb, w_ref[0:384, :], NT)
        ckv = _dot(xb, w_ref[384:640, :], NT)
        cq_ref[...] = cq
        ckv_ref[...] = ckv
        g_ref[...] = _dot(xb, w_ref[640:1664, :], NT)

        cd_, sd_ = cd_ref[...], sd_ref[...]
        qb = _dot(xb, w_ref[1664:2176, :], NT)
        kb = _dot(xb, w_ref[2176:2688, :], NT)
        for p in range(4):
            cols = slice(LANES * p, LANES * (p + 1))
            t = qb[:, cols]
            qr_ref[:, cols] = (t * cd_ + _dil_rot(t, lane) * sd_) * DIL_SCALE
            t = kb[:, cols]
            kr_ref[:, cols] = t * cd_ + _dil_rot(t, lane) * sd_
        vb_ref[...] = _dot(xb, w_ref[2688:3200, :], NT)

        ct_, st_ = ct_ref[...], st_ref[...]

        def rope(t):
            return t * ct_ + _mla_rot(t, lane) * st_

        _, qn = _rms(cq, gq_ref[...])
        q_all = _dot(qn.astype(BF16), wuq_ref[...])
        for h in range(MLA_HEADS):
            q_out[h] = (rope(q_all[:, LANES * h:LANES * (h + 1)]) * MLA_SCALE).astype(BF16)
        _, kvn = _rms(ckv, gkv_ref[...])
        kv_all = _dot(kvn.astype(BF16), wukv_ref[...])
        kpe = rope(_dot(xb, w_ref[3200:3328, :], NT))
        for h in range(MLA_HEADS):
            kv_h = kv_all[:, LANES * h:LANES * (h + 1)]
            k_out[h] = jnp.where(lane < 64, kv_h, kpe).astype(BF16)
            if h % 2:
                v = jnp.where(lane >= 64, kv_h, 0.0)
            else:
                v = jnp.where(lane < 64, pltpu.roll(kv_h, 64, 1), 0.0)
            v_out[h] = jnp.where(lane == ONES_LANE[h % 2], 1.0, v).astype(BF16)

    row = lambda w: pl.BlockSpec((tr, w), lambda i: (i, 0))
    full = lambda a: pl.BlockSpec(a.shape, lambda i: (0,) * a.ndim)
    head = pl.BlockSpec((MLA_HEADS, tr, LANES), lambda i: (0, i, 0))
    widths = (Q_LORA, KV_LORA, D_MODEL, 512, 512, 512)
    return pl.pallas_call(
        body, name="proj", grid=(seq // tr,),
        in_specs=[row(D_MODEL), full(w_in_p), full(gq), full(gkv), full(wuq_e), full(wukv)] + [row(LANES)] * 4,
        out_specs=[row(w) for w in widths] + [head] * 3,
        out_shape=[jax.ShapeDtypeStruct((seq, w), F32) for w in widths]
        + [jax.ShapeDtypeStruct((MLA_HEADS, seq, LANES), BF16)] * 3,
        compiler_params=_cp(("arbitrary",), VMEM_LIMIT),
    )(x, w_in_p, gq, gkv, wuq_e, wukv, ct, st, cd, sd)


def _mla_rot(t, lane):
    return jnp.where(lane < 80, pltpu.roll(t, 112, 1), pltpu.roll(t, 16, 1))


def _dil_rot(t, lane):
    return jnp.where(lane % 64 < 8, pltpu.roll(t, 120, 1), pltpu.roll(t, 8, 1))


def _rms(c, g):
    r = lax.rsqrt(jnp.mean(c * c, axis=-1, keepdims=True) + RMS_EPS)
    return r, c * r * g


def _mla_fwd(q, k, v):
    seq = q.shape[1]
    tq = 512
    nq = seq // tq

    def body(q_ref, k_ref, v_ref, o_ref, lse_ref, m_s, acc_s, s_buf):
        i = pl.program_id(1)
        row = lax.broadcasted_iota(jnp.int32, (tq, tq), 0)
        col = lax.broadcasted_iota(jnp.int32, (tq, tq), 1)
        lane = lax.broadcasted_iota(jnp.int32, (tq, LANES), 1)
        m_s[...] = jnp.full((2, tq, LANES), NEG, F32)
        acc_s[...] = jnp.zeros((2, tq, LANES), F32)

        def block(j):
            return pl.ds(pl.multiple_of(j * tq, tq), tq)

        def scores(hh, j):
            return _dot(q_ref[hh], k_ref[hh, block(j), :], NT)

        def consume(hh, j, s):
            m_prev = m_s[hh]
            m_new = jnp.maximum(m_prev, jnp.max(s, axis=1, keepdims=True))
            p = jnp.exp(s - m_new[:, :1])
            acc_s[hh] = jnp.exp(m_prev - m_new) * acc_s[hh] + _dot(p.astype(BF16), v_ref[hh, block(j), :])
            m_s[hh] = m_new

        for hh in range(2):
            s_buf[0, hh] = scores(hh, 0)

        def full_step(j, carry):
            slot = j & 1
            for hh in range(2):
                s = s_buf[slot, hh]
                s_buf[1 - slot, hh] = scores(hh, j + 1)
                consume(hh, j, s)
            return carry

        lax.fori_loop(0, i, full_step, 0)
        total = jnp.zeros((tq, LANES), F32)
        for hh in range(2):
            consume(hh, i, jnp.where(col <= row, s_buf[i & 1, hh], NEG))
            acc = acc_s[hh]
            l = acc[:, ONES_LANE[hh]:ONES_LANE[hh] + 1]
            mine = (lane >= 64) if hh else (lane < 64)
            total = total + jnp.where(mine, acc / l, 0.0)
            lse_ref[hh] = m_s[hh] + jnp.log(l)
        o_ref[...] = total

    kv_spec = pl.BlockSpec((2, seq, LANES), lambda p, i: (p, 0, 0))
    return pl.pallas_call(
        body, name="mla_fwd", grid=(MLA_HEADS // 2, nq),
        in_specs=[pl.BlockSpec((2, tq, LANES), lambda p, i: (p, i, 0)), kv_spec, kv_spec],
        out_specs=[pl.BlockSpec((tq, LANES), lambda p, i: (i, p)), pl.BlockSpec((2, tq, LANES), lambda p, i: (p, i, 0))],
        out_shape=[jax.ShapeDtypeStruct((seq, 4 * LANES), F32), jax.ShapeDtypeStruct((MLA_HEADS, seq, LANES), F32)],
        scratch_shapes=[pltpu.VMEM((2, tq, LANES), F32), pltpu.VMEM((2, tq, LANES), F32),
                        pltpu.VMEM((2, 2, tq, tq), F32)],
        compiler_params=_cp(("arbitrary", "arbitrary"), VMEM_LIMIT),
    )(q, k, v)


DIL_Q_FWD = 2 * BLOCK
DIL_Q_BWD = BLOCK


def _dil_tile_index(t, d, seq, nq):
    per_class = seq // (nq * d)
    shift = per_class.bit_length() - 1
    r = t >> shift
    n = t & (per_class - 1)
    start = r + (nq * d) * n
    prev = jnp.maximum(start - BLOCK * d, r)
    if d == 1:
        start = pl.multiple_of(start, nq)
        prev = pl.multiple_of(prev, BLOCK)
    return (n == 0).astype(jnp.int32), start, prev


def _dil_rows(start, d, size):
    return pl.ds(start, size) if d == 1 else pl.ds(start, size, stride=d)


def _dil_bias(nq):
    i = lax.broadcasted_iota(jnp.int32, (2 * nq, BLOCK + nq), 0) % nq
    j = lax.broadcasted_iota(jnp.int32, (2 * nq, BLOCK + nq), 1)
    band = (j >= i) & (j <= i + BLOCK)
    return jnp.where(band, 0.0, NEG), jnp.where(band & (j >= BLOCK), 0.0, NEG)


def _stack_heads(t, lane):
    return jnp.concatenate([jnp.where(lane < 64, t, 0.0), jnp.where(lane >= 64, t, 0.0)], axis=0)


def _unstack_heads(t, lane):
    nq = t.shape[0] // 2
    return jnp.where(lane < 64, t[:nq], t[nq:])


def _dil_fwd(qr, kr, vb):
    seq = qr.shape[0]
    nq = DIL_Q_FWD
    n_tiles = seq // nq
    assert seq % (nq * max(DIL_DILATIONS)) == 0

    def body(q_ref, k_ref, v_ref, o_ref, lse_ref, m_s, l_s, n_s, bias_s):
        lane = lax.broadcasted_iota(jnp.int32, (nq, LANES), 1)
        bias_s[0], bias_s[1] = _dil_bias(nq)
        for bi, d in enumerate(DIL_DILATIONS):

            def tile(t, carry, d=d, bi=bi):
                first, start, prev = _dil_tile_index(t, d, seq, nq)
                rows, prows = _dil_rows(start, d, nq), _dil_rows(prev, d, BLOCK)
                qst = _stack_heads(q_ref[rows, :], lane).astype(BF16)
                if seq == nq * d:
                    kcat, vcat = k_ref[rows, :].astype(BF16), v_ref[rows, :].astype(BF16)
                    s = _dot(qst, kcat, NT) + bias_s[1, :, BLOCK:]
                else:
                    kcat = jnp.concatenate([k_ref[prows, :], k_ref[rows, :]], axis=0).astype(BF16)
                    vcat = jnp.concatenate([v_ref[prows, :], v_ref[rows, :]], axis=0).astype(BF16)
                    s = _dot(qst, kcat, NT) + bias_s[first]
                m = jnp.max(s, axis=1, keepdims=True)
                p = jnp.exp(s - m)
                l2 = _unstack_heads(jnp.sum(p, axis=1, keepdims=True) + jnp.zeros((2 * nq, LANES), F32), lane)
                m2 = _unstack_heads(m + jnp.zeros((2 * nq, LANES), F32), lane)
                num2 = _unstack_heads(_dot(p.astype(BF16), vcat), lane)
                if bi == 0:
                    m_s[rows, :] = m2
                    l_s[rows, :] = l2
                    n_s[rows, :] = num2
                else:
                    m_old = m_s[rows, :]
                    m_new = jnp.maximum(m_old, m2)
                    a = jnp.exp(m_old - m_new)
                    b = jnp.exp(m2 - m_new)
                    m_s[rows, :] = m_new
                    l_s[rows, :] = a * l_s[rows, :] + b * l2
                    n_s[rows, :] = a * n_s[rows, :] + b * num2
                return carry

            lax.fori_loop(0, n_tiles, tile, 0, unroll=8)
        o_ref[...] = n_s[...] / l_s[...]
        lse_ref[...] = m_s[...] + jnp.log(l_s[...])

    col = lambda off: pl.BlockSpec((seq, LANES), lambda p: (0, p + off))
    return pl.pallas_call(
        body, name="dil_fwd", grid=(4,),
        in_specs=[col(0), col(0), col(0)],
        out_specs=[col(0), pl.BlockSpec((None, seq, LANES), lambda p: (p, 0, 0))],
        out_shape=[jax.ShapeDtypeStruct((seq, 4 * LANES), F32), jax.ShapeDtypeStruct((4, seq, LANES), F32)],
        scratch_shapes=[pltpu.VMEM((seq, LANES), F32)] * 3 + [pltpu.VMEM((2, 2 * nq, BLOCK + nq), F32)],
        compiler_params=_cp(("arbitrary",), VMEM_LIMIT),
    )(qr, kr, vb)


def _post(x, o_a, o_b, gates, w_out, ln_g, ln_b, target):
    seq = x.shape[0]
    tr = 512

    def body(x_ref, oa_ref, ob_ref, g_ref, w_ref, lg_ref, lb_ref, t_ref,
             dz_ref, do_ref, dg_ref, dw_ref, dlg_ref, dlb_ref, loss_ref):
        @pl.when(pl.program_id(0) == 0)
        def _():
            dw_ref[...] = jnp.zeros_like(dw_ref)
            dlg_ref[...] = jnp.zeros_like(dlg_ref)
            dlb_ref[...] = jnp.zeros_like(dlb_ref)
            loss_ref[...] = jnp.zeros_like(loss_ref)

        g = g_ref[...]
        sg = jax.nn.sigmoid(g)
        silu = g * sg
        o = jnp.concatenate([oa_ref[...], ob_ref[...]], axis=1)
        mixb = (o * silu).astype(BF16)
        w = w_ref[...]
        z = ALPHA * x_ref[...] + _dot(mixb, w)
        mu = jnp.mean(z, axis=-1, keepdims=True)
        zc = z - mu
        rstd = lax.rsqrt(jnp.mean(zc * zc, axis=-1, keepdims=True) + LN_EPS)
        xhat = zc * rstd
        lg = lg_ref[...]
        err = xhat * lg + lb_ref[...] - t_ref[...]
        loss_ref[...] += jnp.sum(err * err) * (0.5 / D_MODEL)
        dy = err * (1.0 / D_MODEL)
        dlg_ref[...] += jnp.sum(dy * xhat, axis=0, keepdims=True)
        dlb_ref[...] += jnp.sum(dy, axis=0, keepdims=True)
        dxh = dy * lg
        dz = rstd * (dxh - jnp.mean(dxh, axis=-1, keepdims=True) - xhat * jnp.mean(dxh * xhat, axis=-1, keepdims=True))
        dz_ref[...] = dz
        dzb = dz.astype(BF16)
        dmix = _dot(dzb, w, NT)
        do_ref[...] = dmix * silu
        dg_ref[...] = (dmix * o * (sg * (1.0 + g * (1.0 - sg)))).astype(BF16)
        dw_ref[...] += _dot(mixb, dzb, TN)

    row = lambda w: pl.BlockSpec((tr, w), lambda i: (i, 0))
    full = lambda s: pl.BlockSpec(s, lambda i: (0, 0))
    return pl.pallas_call(
        body, name="post", grid=(seq // tr,),
        in_specs=[row(D_MODEL), row(512), row(512), row(D_MODEL), full((D_MODEL, D_MODEL)), full((1, D_MODEL)),
                  full((1, D_MODEL)), row(D_MODEL)],
        out_specs=[row(D_MODEL), row(D_MODEL), row(D_MODEL), full((D_MODEL, D_MODEL)), full((1, D_MODEL)),
                   full((1, D_MODEL)), full((1, LANES))],
        out_shape=[jax.ShapeDtypeStruct((seq, D_MODEL), F32), jax.ShapeDtypeStruct((seq, D_MODEL), F32),
                   jax.ShapeDtypeStruct((seq, D_MODEL), BF16), jax.ShapeDtypeStruct((D_MODEL, D_MODEL), F32),
                   jax.ShapeDtypeStruct((1, D_MODEL), F32), jax.ShapeDtypeStruct((1, D_MODEL), F32),
                   jax.ShapeDtypeStruct((1, LANES), F32)],
        compiler_params=_cp(("arbitrary",), VMEM_LIMIT),
    )(x, o_a, o_b, gates, w_out, ln_g, ln_b, target)


def _mla_bwd(q, k, v, d_o, o, lse):
    seq = q.shape[1]
    tq = 512
    nq = seq // tq

    def body(q_ref, k_ref, v_ref, do_ref, o_ref, lse_ref, dq_ref, dk_ref, dv_ref, d_s, lse_s, dk_s, dv_s, v_s, kt_s, dqt_s):
        j = pl.program_id(1)
        lane = lax.broadcasted_iota(jnp.int32, (tq, LANES), 1)
        row = lax.broadcasted_iota(jnp.int32, (tq, tq), 0)
        col = lax.broadcasted_iota(jnp.int32, (tq, tq), 1)

        @pl.when(j == 0)
        def _():
            dqt_s[...] = jnp.zeros_like(dqt_s)

            def rowsum(i, carry):
                rows = pl.ds(pl.multiple_of(i * tq, tq), tq)
                prod = do_ref[rows, :] * o_ref[rows, :]
                for hh in range(2):
                    mine = (lane >= 64) if hh else (lane < 64)
                    total = jnp.sum(jnp.where(mine, prod, 0.0), axis=1, keepdims=True)
                    d_s[hh, i] = jnp.transpose(total + jnp.zeros((tq, LANES), F32))[:8]
                    lse_s[hh, i] = jnp.transpose(lse_ref[hh, rows, :])[:8]
                return carry

            lax.fori_loop(0, nq, rowsum, 0)

        dk_s[...] = jnp.zeros_like(dk_s)
        dv_s[...] = jnp.zeros_like(dv_s)
        for hh in range(2):
            v_s[hh] = jnp.where(lane == ONES_LANE[hh], 0.0, v_ref[hh].astype(F32)).astype(BF16)
            kt_s[hh] = jnp.transpose(k_ref[hh].astype(F32)).astype(BF16)

        def step(i, masked):
            rows = pl.ds(pl.multiple_of(i * tq, tq), tq)
            dob = do_ref[rows, :].astype(BF16)
            for hh in range(2):
                qb, kb, vb = q_ref[hh, rows, :], k_ref[hh], v_s[hh]
                p = jnp.exp(_dot(kb, qb, NT) - lse_s[hh, i][:1])
                if masked:
                    p = jnp.where(row <= col, p, 0.0)
                dv_s[hh] += _dot(p.astype(BF16), dob)
                ds = (p * (_dot(vb, dob, NT) - d_s[hh, i][:1])).astype(BF16)
                dk_s[hh] += _dot(ds, qb)
                dqt_s[hh, i] += _dot(kt_s[hh], ds)

        def full_step(i, carry):
            step(i, False)
            return carry

        step(j, True)
        lax.fori_loop(j + 1, nq, full_step, 0)
        dk_ref[...] = dk_s[...]
        dv_ref[...] = dv_s[...]

        @pl.when(j == nq - 1)
        def _():
            def untranspose(i, carry):
                rows = pl.ds(pl.multiple_of(i * tq, tq), tq)
                for hh in range(2):
                    dq_ref[hh, rows, :] = jnp.transpose(dqt_s[hh, i])
                return carry

            lax.fori_loop(0, nq, untranspose, 0)

    whole = pl.BlockSpec((2, seq, LANES), lambda p, j: (p, 0, 0))
    blk = pl.BlockSpec((2, tq, LANES), lambda p, j: (p, j, 0))
    pair = pl.BlockSpec((seq, LANES), lambda p, j: (0, p))
    shape = jax.ShapeDtypeStruct((MLA_HEADS, seq, LANES), F32)
    return pl.pallas_call(
        body, name="mla_bwd", grid=(MLA_HEADS // 2, nq),
        in_specs=[whole, blk, blk, pair, pair, whole],
        out_specs=[whole, blk, blk], out_shape=[shape] * 3,
        scratch_shapes=[pltpu.VMEM((2, nq, 8, tq), F32), pltpu.VMEM((2, nq, 8, tq), F32),
                        pltpu.VMEM((2, tq, LANES), F32), pltpu.VMEM((2, tq, LANES), F32),
                        pltpu.VMEM((2, tq, LANES), BF16), pltpu.VMEM((2, LANES, tq), BF16),
                        pltpu.VMEM((2, nq, LANES, tq), F32)],
        compiler_params=_cp(("arbitrary", "arbitrary"), VMEM_LIMIT),
    )(q, k, v, d_o, o, lse)


def _dil_bwd(qr, kr, vb, d_o, o, lse):
    seq = qr.shape[0]
    nq = DIL_Q_BWD
    n_tiles = seq // nq
    chunk = 512

    def body(q_ref, k_ref, v_ref, do_ref, o_ref, lse_ref, dq_ref, dk_ref, dv_ref, d_s, dq_s, dk_s, dv_s, bias_s):
        lane = lax.broadcasted_iota(jnp.int32, (nq, LANES), 1)
        lanec = lax.broadcasted_iota(jnp.int32, (chunk, LANES), 1)
        bias_s[0], bias_s[1] = [b[:nq] for b in _dil_bias(nq)]

        def rowsum(i, carry):
            rows = pl.ds(pl.multiple_of(i * chunk, chunk), chunk)
            prod = do_ref[rows, :] * o_ref[rows, :]
            lo = jnp.sum(jnp.where(lanec < 64, prod, 0.0), axis=1, keepdims=True)
            hi = jnp.sum(jnp.where(lanec >= 64, prod, 0.0), axis=1, keepdims=True)
            d_s[rows, :] = jnp.where(lanec < 64, lo, hi)
            return carry

        lax.fori_loop(0, seq // chunk, rowsum, 0)
        dq_s[...] = jnp.zeros_like(dq_s)
        dk_s[...] = jnp.zeros_like(dk_s)
        dv_s[...] = jnp.zeros_like(dv_s)
        for d in DIL_DILATIONS:

            def tile(start, prev, first, d=d):
                rows = _dil_rows(start, d, nq)
                q_t, do_t = q_ref[rows, :], do_ref[rows, :]
                lse_t, d_t = lse_ref[rows, :], d_s[rows, :]
                if prev is None:
                    kcat, vcat = k_ref[rows, :].astype(BF16), v_ref[rows, :].astype(BF16)
                    bias = bias_s[1, :, BLOCK:]
                else:
                    prows = _dil_rows(prev, d, BLOCK)
                    kcat = jnp.concatenate([k_ref[prows, :], k_ref[rows, :]], axis=0).astype(BF16)
                    vcat = jnp.concatenate([v_ref[prows, :], v_ref[rows, :]], axis=0).astype(BF16)
                    bias = bias_s[first]
                dq_t = jnp.zeros((nq, LANES), F32)
                dkcat = jnp.zeros((kcat.shape[0], LANES), F32)
                dvcat = jnp.zeros((kcat.shape[0], LANES), F32)
                for hh in range(2):
                    mine = (lane >= 64) if hh else (lane < 64)
                    c0 = 64 * hh
                    qh = jnp.where(mine, q_t, 0.0).astype(BF16)
                    doh = jnp.where(mine, do_t, 0.0).astype(BF16)
                    p = jnp.exp(_dot(qh, kcat, NT) + bias - lse_t[:, c0:c0 + 1])
                    dvcat = dvcat + _dot(p.astype(BF16), doh, TN)
                    dp = _dot(doh, vcat, NT)
                    ds = (p * (dp - d_t[:, c0:c0 + 1])).astype(BF16)
                    dq_t = dq_t + jnp.where(mine, _dot(ds, kcat), 0.0)
                    dkcat = dkcat + _dot(ds, qh, TN)
                dq_s[rows, :] += dq_t
                if prev is not None:
                    dk_s[prows, :] += dkcat[:BLOCK]
                    dv_s[prows, :] += dvcat[:BLOCK]
                dk_s[rows, :] += dkcat[-nq:]
                dv_s[rows, :] += dvcat[-nq:]

            if seq == 2 * nq * d:

                def class_tiles(r, carry, d=d):
                    tile(r, None, 1)
                    tile(r + nq * d, r, 0)
                    return carry

                lax.fori_loop(0, d, class_tiles, 0, unroll=8)
            else:

                def any_tile(t, carry, d=d):
                    first, start, prev = _dil_tile_index(t, d, seq, nq)
                    tile(start, prev, first)
                    return carry

                lax.fori_loop(0, n_tiles, any_tile, 0, unroll=16)
        dq_ref[...] = dq_s[...].astype(BF16)
        dk_ref[...] = dk_s[...].astype(BF16)
        dv_ref[...] = dv_s[...].astype(BF16)

    col = lambda off: pl.BlockSpec((seq, LANES), lambda p: (0, p + off))
    shape = jax.ShapeDtypeStruct((seq, 4 * LANES), BF16)
    return pl.pallas_call(
        body, name="dil_bwd", grid=(4,),
        in_specs=[col(0), col(0), col(0), col(4), col(0), pl.BlockSpec((None, seq, LANES), lambda p: (p, 0, 0))],
        out_specs=[col(0)] * 3, out_shape=[shape] * 3,
        scratch_shapes=[pltpu.VMEM((seq, LANES), F32)] * 4 + [pltpu.VMEM((2, nq, BLOCK + nq), F32)],
        compiler_params=_cp(("arbitrary",), VMEM_LIMIT),
    )(qr, kr, vb, d_o, o, lse)


def _in_bwd(dz, cq, ckv, gq, gkv, wuq_e, wukv, ct, st, dq, dk, dv, dgates, dqr, dkr, dvb, cd, sd, w_in_p):
    seq = dz.shape[0]
    tr = 512

    def body(dz_ref, cq_ref, ckv_ref, gq_ref, gkv_ref, wuq_ref, wukv_ref, ct_ref, st_ref, dq_ref, dk_ref, dv_ref,
             dg_ref, dqr_ref, dkr_ref, dvb_ref, cd_ref, sd_ref, w_ref,
             gx_ref, dh_ref, dwuq_ref, dwukv_ref, dgq_ref, dgkv_ref):
        @pl.when(pl.program_id(0) == 0)
        def _():
            dwuq_ref[...] = jnp.zeros_like(dwuq_ref)
            dwukv_ref[...] = jnp.zeros_like(dwukv_ref)
            dgq_ref[...] = jnp.zeros_like(dgq_ref)
            dgkv_ref[...] = jnp.zeros_like(dgkv_ref)

        lane = lax.broadcasted_iota(jnp.int32, (tr, LANES), 1)
        rope_lanes = jnp.logical_and(lane >= 64, lane < 96)
        ct_, st_ = ct_ref[...], st_ref[...]

        def mla_rope_t(g):
            return ct_ * g + jnp.where(rope_lanes, _mla_rot(st_ * g, lane), 0.0)

        def norm_bwd(c, g, dn, dg_ref):
            r, _ = _rms(c, g)
            u = dn * g
            dg_ref[...] += jnp.sum(dn * c * r, axis=0, keepdims=True)
            return r * u - c * (r * r * r) * jnp.mean(u * c, axis=-1, keepdims=True)

        c, g = cq_ref[...], gq_ref[...]
        _, qn = _rms(c, g)
        dq_all = jnp.concatenate([mla_rope_t(dq_ref[h] * MLA_SCALE) for h in range(MLA_HEADS)], axis=1).astype(BF16)
        dwuq_ref[...] += _dot(qn.astype(BF16), dq_all, TN)
        dcq = norm_bwd(c, g, _dot(dq_all, wuq_ref[...], NT), dgq_ref).astype(BF16)

        c, g = ckv_ref[...], gkv_ref[...]
        _, kvn = _rms(c, g)
        dkpe = jnp.zeros((tr, LANES), F32)
        parts = []
        for h in range(MLA_HEADS):
            dk_h, dv_h = dk_ref[h], dv_ref[h]
            if h % 2 == 0:
                dv_h = pltpu.roll(dv_h, 64, 1)
            parts.append(jnp.where(lane < 64, dk_h, dv_h))
            dkpe = dkpe + jnp.where(rope_lanes, dk_h, 0.0)
        dkv_all = jnp.concatenate(parts, axis=1).astype(BF16)
        dwukv_ref[...] += _dot(kvn.astype(BF16), dkv_all, TN)
        dckv = norm_bwd(c, g, _dot(dkv_all, wukv_ref[...], NT), dgkv_ref).astype(BF16)
        dkrope = mla_rope_t(dkpe).astype(BF16)

        rot_lanes = lane % 64 < DIL_ROT
        cd_, sd_ = cd_ref[...], sd_ref[...]

        def dil_rope_t(g):
            return cd_ * g + jnp.where(rot_lanes, _dil_rot(sd_ * g, lane), 0.0)

        dqb = [dil_rope_t(dqr_ref[:, LANES * p:LANES * (p + 1)].astype(F32) * DIL_SCALE).astype(BF16) for p in range(4)]
        dkb = [dil_rope_t(dkr_ref[:, LANES * p:LANES * (p + 1)].astype(F32)).astype(BF16) for p in range(4)]
        dh = jnp.concatenate([dcq, dckv, dg_ref[...]] + dqb + dkb + [dvb_ref[...], dkrope], axis=1)
        dh_ref[...] = dh
        gx_ref[...] = ALPHA * dz_ref[...] + _dot(dh, w_ref[...])

    row = lambda w: pl.BlockSpec((tr, w), lambda i: (i, 0))
    full = lambda a: pl.BlockSpec(a.shape, lambda i: (0,) * a.ndim)
    head = pl.BlockSpec((MLA_HEADS, tr, LANES), lambda i: (0, i, 0))
    return pl.pallas_call(
        body, name="in_bwd", grid=(seq // tr,),
        in_specs=[row(D_MODEL), row(Q_LORA), row(KV_LORA), full(gq), full(gkv), full(wuq_e), full(wukv), row(LANES),
                  row(LANES), head, head, head, row(D_MODEL), row(512), row(512), row(512), row(LANES), row(LANES),
                  full(w_in_p)],
        out_specs=[row(D_MODEL), row(IN_WIDTH_PAD), full(wuq_e), full(wukv), full(gq), full(gkv)],
        out_shape=[jax.ShapeDtypeStruct((seq, D_MODEL), F32), jax.ShapeDtypeStruct((seq, IN_WIDTH_PAD), BF16),
                   jax.ShapeDtypeStruct(wuq_e.shape, F32), jax.ShapeDtypeStruct(wukv.shape, F32),
                   jax.ShapeDtypeStruct(gq.shape, F32), jax.ShapeDtypeStruct(gkv.shape, F32)],
        compiler_params=_cp(("arbitrary",), VMEM_LIMIT),
    )(dz, cq, ckv, gq, gkv, wuq_e, wukv, ct, st, dq, dk, dv, dgates, dqr, dkr, dvb, cd, sd, w_in_p)


def _dw_in(x, dh):
    seq = dh.shape[0]
    tk = 512
    tn = IN_WIDTH_PAD // 2

    def body(x_ref, dh_ref, o_ref):
        @pl.when(pl.program_id(1) == 0)
        def _():
            o_ref[...] = jnp.zeros_like(o_ref)

        o_ref[...] += _dot(dh_ref[...], x_ref[...].astype(BF16), TN)

    return pl.pallas_call(
        body, name="dw_in", grid=(2, seq // tk),
        in_specs=[pl.BlockSpec((tk, D_MODEL), lambda n, k: (k, 0)), pl.BlockSpec((tk, tn), lambda n, k: (k, n))],
        out_specs=pl.BlockSpec((tn, D_MODEL), lambda n, k: (n, 0)),
        out_shape=jax.ShapeDtypeStruct((IN_WIDTH_PAD, D_MODEL), F32),
        compiler_params=_cp(("arbitrary", "arbitrary"), VMEM_LIMIT),
    )(x, dh)


def _adam_update(w, g, m, v):
    nm = ADAM_B1 * m + (1.0 - ADAM_B1) * g
    nv = ADAM_B2 * v + (1.0 - ADAM_B2) * jnp.square(g)
    m_hat = nm / (1.0 - ADAM_B1 ** ADAM_STEP)
    v_hat = nv / (1.0 - ADAM_B2 ** ADAM_STEP)
    return -ADAM_LR * (m_hat / (jnp.sqrt(v_hat) + ADAM_EPS) + ADAM_WD * w), nm, nv


def _adamw(w, g, m, v, name):
    rows, cols = w.shape
    tc = 256 if cols % 256 == 0 and rows * cols > 2 ** 18 else cols

    def body(w_ref, g_ref, m_ref, v_ref, d_ref, nm_ref, nv_ref):
        d_ref[...], nm_ref[...], nv_ref[...] = _adam_update(w_ref[...], g_ref[...], m_ref[...], v_ref[...])

    spec = pl.BlockSpec((rows, tc), lambda i: (0, i))
    return pl.pallas_call(
        body, name=name, grid=(cols // tc,), in_specs=[spec] * 4, out_specs=[spec] * 3,
        out_shape=[jax.ShapeDtypeStruct(w.shape, F32)] * 3, compiler_params=_cp(("arbitrary",)),
    )(w, g, m, v)


def _adamw_vectors(small_sum, ws, ms, vs):
    k = len(ws)
    sizes = [w.shape[0] for w in ws]

    def body(s_ref, *refs):
        ins, outs = refs[:3 * k], refs[3 * k:]
        for i, size in enumerate(sizes):
            g = s_ref[i, 0:size]
            outs[i][...] = g
            outs[k + i][...], outs[2 * k + i][...], outs[3 * k + i][...] = _adam_update(
                ins[i][...], g, ins[k + i][...], ins[2 * k + i][...])

    out = pl.pallas_call(
        body, name="adamw_vectors", out_shape=[jax.ShapeDtypeStruct((size,), F32) for size in sizes] * 4,
    )(small_sum, *ws, *ms, *vs)
    return [out[k * j:k * (j + 1)] for j in range(4)]


def _local_step(x2, target, w_in_p, w_uq_f, wukv_f, w_out_f, q_norm_g, kv_norm_g, ln_g, ln_b):
    seq = x2.shape[0]
    wuq_e = jnp.pad(w_uq_f.reshape(Q_LORA, MLA_HEADS, 96), ((0, 0), (0, 0), (0, 32))).reshape(Q_LORA, MLA_HEADS * LANES)
    ct, st, cd, sd = _rope_tables(seq)
    gq = q_norm_g.reshape(1, Q_LORA)
    gkv = kv_norm_g.reshape(1, KV_LORA)

    cq, ckv, gates, qr, krot, vb, q_e, k_e, v_e = _proj(x2, w_in_p, gq, gkv, wuq_e, wukv_f, ct, st, cd, sd)
    o_a, lse_a = _mla_fwd(q_e, k_e, v_e)
    o_b, lse_b = _dil_fwd(qr, krot, vb)

    dz, d_o, d_gates, dw_out, dln_g, dln_b, loss_part = _post(
        x2, o_a, o_b, gates, w_out_f, ln_g.reshape(1, D_MODEL), ln_b.reshape(1, D_MODEL), target)
    dq_e, dk_e, dv_e = _mla_bwd(q_e, k_e, v_e, d_o, o_a, lse_a)
    dqr, dkr, dvb = _dil_bwd(qr, krot, vb, d_o, o_b, lse_b)
    grad_x, dh, dwuq_e, dwukv, dgq, dgkv = _in_bwd(
        dz, cq, ckv, gq, gkv, wuq_e, wukv_f, ct, st, dq_e, dk_e, dv_e, d_gates, dqr, dkr, dvb, cd, sd, w_in_p)
    dw_in = _dw_in(x2, dh)
    dw_uq = dwuq_e.reshape(Q_LORA, MLA_HEADS, LANES)[:, :, :96].reshape(Q_LORA, MLA_HEADS * 96)
    return loss_part, grad_x, dw_in, dw_uq, dwukv, dw_out, dgq, dgkv, dln_g, dln_b


def kernel(x, w_in, q_norm_g, kv_norm_g, w_uq, w_ukv, w_out, ln_g, ln_b, loss_target, m_w_in, m_q_norm_g, m_kv_norm_g, m_w_uq, m_w_ukv, m_w_out, m_ln_g, m_ln_b, v_w_in, v_q_norm_g, v_kv_norm_g, v_w_uq, v_w_ukv, v_w_out, v_ln_g, v_ln_b):
    seq = x.shape[1]
    x2 = x.reshape(seq, D_MODEL)
    target = loss_target.reshape(seq, D_MODEL)

    g_w_in, g_w_uq, g_w_ukv, g_w_out = _all_gather_weights([w_in.T, w_uq, w_ukv, w_out])
    by_cols = lambda g: jnp.concatenate([g[j] for j in range(N_SHARD)], axis=1)
    loss_part, grad_x, dw_in, dw_uq, dwukv, dw_out, dgq, dgkv, dln_g, dln_b = _local_step(
        x2, target, g_w_in, by_cols(g_w_uq), by_cols(g_w_ukv), g_w_out.reshape(D_MODEL, D_MODEL),
        q_norm_g, kv_norm_g, ln_g, ln_b)

    to_shards = lambda d: d.reshape(d.shape[0], N_SHARD, d.shape[1] // N_SHARD).transpose(1, 0, 2)
    grads = [dw_in, to_shards(dw_uq), to_shards(dwukv), dw_out.reshape(N_SHARD, 256, D_MODEL)]
    g_in_t, g_uq, g_ukv, g_out, small_sum = _reduce_gradients(grads, [dgq, dgkv, dln_g, dln_b, loss_part])
    g_in = g_in_t.T
    loss = small_sum[4, 0]

    big = [[o.T for o in _adamw(w.T, g.T, m.T, v.T, name)] for w, g, m, v, name in (
        (w_in, g_in, m_w_in, v_w_in, "adamw_w_in"), (w_uq, g_uq, m_w_uq, v_w_uq, "adamw_w_uq"))]
    big += [_adamw(w, g, m, v, name) for w, g, m, v, name in (
        (w_ukv, g_ukv, m_w_ukv, v_w_ukv, "adamw_w_ukv"), (w_out, g_out, m_w_out, v_w_out, "adamw_w_out"))]
    vec_g, vec_delta, vec_m, vec_v = _adamw_vectors(
        small_sum, [q_norm_g, kv_norm_g, ln_g, ln_b], [m_q_norm_g, m_kv_norm_g, m_ln_g, m_ln_b],
        [v_q_norm_g, v_kv_norm_g, v_ln_g, v_ln_b])

    def ordered(bigs, vecs):
        return [bigs[0], vecs[0], vecs[1], bigs[1], bigs[2], bigs[3], vecs[2], vecs[3]]

    grads_out = ordered([g_in, g_uq, g_ukv, g_out], vec_g)
    deltas = ordered([b[0] for b in big], vec_delta)
    new_m = ordered([b[1] for b in big], vec_m)
    new_v = ordered([b[2] for b in big], vec_v)
    return (loss, grad_x.reshape(x.shape), *grads_out, *deltas, *new_m, *new_v)
```

```python
import jax
import jax.numpy as jnp
import numpy as np
from jax import lax
from jax.experimental import pallas as pl
from jax.experimental.pallas import tpu as pltpu

F32 = jnp.float32
BF16 = jnp.bfloat16

D_MODEL = 1024
ROPE_THETA = 500000.0
BLOCK = 128
NEG = -1e30
RMS_EPS = 1e-6
LN_EPS = 1e-5

MLA_HEADS = 8
MLA_NOPE = 64
MLA_ROPE = 32
Q_LORA = 384
KV_LORA = 256
DIL_HEAD_DIM = 64
DIL_ROT = 16
DIL_DILATIONS = (1, 4, 16)
IN_WIDTH_PAD = 3328
ONES_LANE = (64, 0)
MLA_SCALE = (MLA_NOPE + MLA_ROPE) ** -0.5
DIL_SCALE = DIL_HEAD_DIM ** -0.5
ALPHA = 2.0 ** 0.25

ADAM_LR = 0.001
ADAM_B1 = 0.9
ADAM_B2 = 0.999
ADAM_EPS = 1e-08
ADAM_WD = 0.01
ADAM_STEP = 10

N_SHARD = 4
SHARD_SHAPES = ((808, 1024), (384, 192), (256, 256), (256, 1024))
SHARD_SPLIT_COLS = (True, False, False, False)
ROW_CHUNK = 64
LANES = 128
VMEM_LIMIT = 56 * 1024 * 1024
MESH = pl.DeviceIdType.MESH

NT = (((1,), (1,)), ((), ()))
TN = (((0,), (0,)), ((), ()))


def _cp(sem=None, vmem=None):
    return pltpu.CompilerParams(dimension_semantics=sem, vmem_limit_bytes=vmem)


def _dot(a, b, dims=None):
    if dims is None:
        return jnp.dot(a, b, preferred_element_type=F32)
    return lax.dot_general(a, b, dims, preferred_element_type=F32)


def _rope_tables(seq):
    f32 = np.float32
    pos = np.arange(seq, dtype=f32)[:, None]
    one, zero = np.ones((seq, 64), f32), np.zeros((seq, 64), f32)

    def cos_sin(dim):
        inv = np.power(f32(ROPE_THETA), -np.arange(0, dim, 2, dtype=f32) / f32(dim)).astype(f32)
        ang = (pos * inv[None, :]).astype(f32)
        return np.cos(ang).astype(f32), np.sin(ang).astype(f32)

    cos, sin = cos_sin(MLA_ROPE)
    ct = np.concatenate([one, cos, cos, zero[:, :32]], axis=1)
    st = np.concatenate([zero, -sin, sin, zero[:, :32]], axis=1)
    cos, sin = cos_sin(DIL_ROT)
    cd = np.concatenate([cos, cos, one[:, :48]], axis=1)
    sd = np.concatenate([-sin, sin, zero[:, :48]], axis=1)
    return tuple(jnp.asarray(t) for t in (ct, st, np.tile(cd, (1, 2)), np.tile(sd, (1, 2))))


W_IN_ORDER = ((0, 640), (672, 1184), (2720, 3232), (1184, 2720), None, (640, 672))


def _permute_w_in_t(w_t):
    z = jnp.zeros((64, w_t.shape[1]), w_t.dtype)
    parts = [z if r is None else w_t[r[0]:r[1]] for r in W_IN_ORDER]
    return jnp.concatenate(parts + [z[:32]], axis=0)


def _w_in_row_pieces():
    width = SHARD_SHAPES[0][0]
    pieces, at = [], 0
    for r in W_IN_ORDER:
        if r is None:
            at += 64
            continue
        for k in range(N_SHARD):
            lo, hi = max(r[0], width * k), min(r[1], width * (k + 1))
            if lo < hi:
                pieces.append((k, lo - width * k, at + lo - r[0], hi - lo))
        at += r[1] - r[0]
    return pieces


def _position():
    return lax.axis_index("x"), lax.axis_index("y"), lax.axis_index("c")


def _all_gather_weights(shards):
    n = len(shards)

    def body(*refs):
        ins, outs = refs[:n], list(refs[n:2 * n])
        w_in_p, outs[0] = outs[0], refs[2 * n]
        send_sems, recv_sems = refs[2 * n + 1:]
        x, y, c = _position()
        me = 2 * x + y
        chips = [(1 - x, y), (x, 1 - y), (1 - x, 1 - y)]
        for a in range(n):
            for blk in _shard_blocks(a):
                outs[a][(me,) + blk] = ins[a][blk].astype(BF16)

        def copy(k, a, slot, part, to):
            ref = outs[a].at[(slot,) + part]
            return pltpu.make_async_remote_copy(
                src_ref=ref, dst_ref=ref, send_sem=send_sems.at[k * n + a], recv_sem=recv_sems.at[k * n + a],
                device_id=to, device_id_type=MESH)

        half = [_shard_half(a, c) for a in range(n)]
        other = [_shard_half(a, 1 - c) for a in range(n)]
        first = [copy(k, a, me, half[a], (px, py, c)) for k, (px, py) in enumerate(chips) for a in range(n)]
        for cp in first:
            cp.start()
        passed = []
        for k, (px, py) in enumerate(chips):
            for a in range(n):
                copy(k, a, 2 * px + py, half[a], (x, y, c)).wait_recv()
                cp = copy(3 + k, a, 2 * px + py, half[a], (x, y, 1 - c))
                cp.start()
                passed.append(cp)
        for k, (px, py) in enumerate(chips):
            for a in range(n):
                copy(3 + k, a, 2 * px + py, other[a], (x, y, c)).wait_recv()
        for cp in first + passed:
            cp.wait_send()

        written = []
        for k, r0, at, rows in _w_in_row_pieces():
            written.append((at, at + rows))
            for r in range(0, rows, 2 * LANES):
                m = min(2 * LANES, rows - r)
                for c0 in range(0, D_MODEL, LANES):
                    w_in_p[at + r:at + r + m, c0:c0 + LANES] = outs[0][k, r0 + r:r0 + r + m, c0:c0 + LANES]
        for lo, hi in zip([0] + [w[1] for w in sorted(written)], [w[0] for w in sorted(written)] + [IN_WIDTH_PAD]):
            if lo < hi:
                w_in_p[lo:hi, :] = jnp.zeros((hi - lo, D_MODEL), BF16)

    vmem = pl.BlockSpec(memory_space=pltpu.VMEM)
    return pl.pallas_call(
        body, name="all_gather_weights",
        out_shape=[jax.ShapeDtypeStruct((IN_WIDTH_PAD, D_MODEL), BF16)]
        + [jax.ShapeDtypeStruct((N_SHARD,) + s, BF16) for s in SHARD_SHAPES[1:]],
        in_specs=[vmem] * n, out_specs=[vmem] * n,
        scratch_shapes=[pltpu.VMEM((N_SHARD,) + SHARD_SHAPES[0], BF16),
                        pltpu.SemaphoreType.DMA((6 * n,)), pltpu.SemaphoreType.DMA((6 * n,))],
        compiler_params=_cp(None, VMEM_LIMIT),
    )(*shards)


def _shard_blocks(a):
    rows, cols = SHARD_SHAPES[a]
    if SHARD_SPLIT_COLS[a]:
        return [(slice(None), slice(c0, c0 + LANES)) for c0 in range(0, cols, LANES)]
    return [(slice(r0, r0 + ROW_CHUNK), slice(None)) for r0 in range(0, rows, ROW_CHUNK)]


def _shard_half_shape(a):
    rows, cols = SHARD_SHAPES[a]
    return (rows, cols // 2) if SHARD_SPLIT_COLS[a] else (rows // 2, cols)


def _shard_half(a, c):
    rows, cols = SHARD_SHAPES[a]
    if SHARD_SPLIT_COLS[a]:
        return slice(None), pl.ds(pl.multiple_of(c * (cols // 2), LANES), cols // 2)
    return pl.ds(pl.multiple_of(c * (rows // 2), ROW_CHUNK), rows // 2), slice(None)


def _shard_chunks(a, c):
    rows, cols = SHARD_SHAPES[a]
    if SHARD_SPLIT_COLS[a]:
        return [((slice(None), pl.ds(c0, LANES)),
                 (slice(None), pl.ds(pl.multiple_of(c * (cols // 2) + c0, LANES), LANES)))
                for c0 in range(0, cols // 2, LANES)]
    return [((pl.ds(r0, ROW_CHUNK), slice(None)),
             (pl.ds(pl.multiple_of(c * (rows // 2) + r0, ROW_CHUNK), ROW_CHUNK), slice(None)))
            for r0 in range(0, rows // 2, ROW_CHUNK)]


def _reduce_gradients(grads, small_rows, adam_states):
    n = len(grads)
    n_small = len(small_rows)
    pieces = _w_in_row_pieces()
    order = sorted(range(n), key=lambda a: SHARD_SHAPES[a][0] * SHARD_SHAPES[a][1])
    updated = sorted(adam_states)

    def body(*refs):
        g_hbm, rows_in = refs[:n], refs[n:n + n_small]
        state_hbm = {a: refs[n + n_small + 3 * i:n + n_small + 3 * (i + 1)] for i, a in enumerate(updated)}
        refs = refs[n + n_small + 3 * len(updated):]
        outs, small_sum = refs[:n], refs[n]
        new_hbm = {a: refs[n + 1 + 3 * i:n + 1 + 3 * (i + 1)] for i, a in enumerate(updated)}
        scratch = refs[n + 1 + 3 * len(updated):]
        stage, got, sums, others = (scratch[i * n:(i + 1) * n] for i in range(4))
        sm, smalls, send_sems, recv_sems, local_sems = scratch[4 * n:]
        swap_sem, chip_sem, join_sem, small_sem = 0, n, 4 * n, 5 * n
        state_sem = {a: n + len(pieces) + 3 * i for i, a in enumerate(updated)}
        x, y, c = _position()
        me = 4 * x + 2 * y + c
        chips = [(1 - x, y), (x, 1 - y), (1 - x, 1 - y)]
        sm[...] = jnp.zeros_like(sm)
        for i, row in enumerate(rows_in):
            sm[i:i + 1, 0:row.shape[1]] = row[...]
        loads = [[pltpu.make_async_copy(g_hbm[0].at[pl.ds(src, rows)], stage[0].at[k, pl.ds(dst, rows)],
                                        local_sems.at[n + i])
                  for i, (k, dst, src, rows) in enumerate(pieces)]]
        loads += [[pltpu.make_async_copy(g_hbm[a], stage[a], local_sems.at[a])] for a in range(1, n)]
        for a in order:
            for ld in loads[a]:
                ld.start()
        smalls[me] = sm[...]
        small_sends = []
        for rel in range(1, 8):
            px = 1 - x if rel // 4 else x
            py = 1 - y if (rel // 2) % 2 else y
            pc = 1 - c if rel % 2 else c
            cp = pltpu.make_async_remote_copy(
                src_ref=sm, dst_ref=smalls.at[me], send_sem=send_sems.at[small_sem + rel],
                recv_sem=recv_sems.at[small_sem + rel], device_id=(px, py, pc), device_id_type=MESH)
            cp.start()
            small_sends.append((cp, 4 * px + 2 * py + pc))
        swaps = {}
        for a in order:
            for ld in loads[a]:
                ld.wait()
            swaps[a] = pltpu.make_async_remote_copy(
                src_ref=stage[a].at[(slice(None),) + _shard_half(a, 1 - c)], dst_ref=got[a],
                send_sem=send_sems.at[swap_sem + a], recv_sem=recv_sems.at[swap_sem + a],
                device_id=(x, y, 1 - c), device_id_type=MESH)
            swaps[a].start()
        sends, state_loads = {}, {}
        for a in order:
            swaps[a].wait_recv()
            for k in range(N_SHARD):
                for in_half, in_whole in _shard_chunks(a, c):
                    pair = stage[a][(k,) + in_whole] + got[a][(k,) + in_half]
                    sums[a][(k,) + in_half] = pair.astype(BF16)
            sends[a] = [pltpu.make_async_remote_copy(
                src_ref=sums[a].at[2 * px + py], dst_ref=others[a].at[k], send_sem=send_sems.at[chip_sem + k * n + a],
                recv_sem=recv_sems.at[chip_sem + k * n + a], device_id=(px, py, c), device_id_type=MESH)
                for k, (px, py) in enumerate(chips)]
            for cp in sends[a]:
                cp.start()
            swaps[a].wait_send()
            if a in updated:
                state_loads[a] = [pltpu.make_async_copy(state_hbm[a][i], stage[a].at[i], local_sems.at[state_sem[a] + i])
                                  for i in range(3)]
                for ld in state_loads[a]:
                    ld.start()
        joins = []
        for a in order:
            for cp in sends[a]:
                cp.wait_recv()
            for in_half, in_whole in _shard_chunks(a, c):
                total = sums[a][(2 * x + y,) + in_half].astype(F32)
                for k in range(3):
                    total = total + others[a][(k,) + in_half].astype(F32)
                outs[a][in_whole] = total
            half = outs[a].at[_shard_half(a, c)]
            cp = pltpu.make_async_remote_copy(
                src_ref=half, dst_ref=half, send_sem=send_sems.at[join_sem + a],
                recv_sem=recv_sems.at[join_sem + a], device_id=(x, y, 1 - c), device_id_type=MESH)
            cp.start()
            joins.append(cp)
        for rel, (cp, peer) in enumerate(small_sends, start=1):
            pltpu.make_async_remote_copy(
                src_ref=sm, dst_ref=smalls.at[peer], send_sem=send_sems.at[small_sem + rel],
                recv_sem=recv_sems.at[small_sem + rel], device_id=(x, y, c), device_id_type=MESH).wait_recv()
        total = smalls[0]
        for dev in range(1, 8):
            total = total + smalls[dev]
        small_sum[...] = total
        stores = []
        for a in order:
            other = outs[a].at[_shard_half(a, 1 - c)]
            pltpu.make_async_remote_copy(
                src_ref=other, dst_ref=other, send_sem=send_sems.at[join_sem + a],
                recv_sem=recv_sems.at[join_sem + a], device_id=(x, y, c), device_id_type=MESH).wait_recv()
            if a in updated:
                for ld in state_loads[a]:
                    ld.wait()
                for blk in _shard_blocks(a):
                    new = _adam_update(stage[a][(0,) + blk], outs[a][blk], stage[a][(1,) + blk], stage[a][(2,) + blk])
                    for i in range(3):
                        stage[a][(i,) + blk] = new[i]
                stores += [pltpu.make_async_copy(stage[a].at[i], new_hbm[a][i], local_sems.at[state_sem[a] + i])
                           for i in range(3)]
                for st in stores[-3:]:
                    st.start()
        for cp in [cp for a in order for cp in sends[a]] + joins + [cp for cp, _ in small_sends]:
            cp.wait_send()
        for st in stores:
            st.wait()

    vmem, hbm = pl.BlockSpec(memory_space=pltpu.VMEM), pl.BlockSpec(memory_space=pl.ANY)
    halves = [_shard_half_shape(a) for a in range(n)]
    out = pl.pallas_call(
        body, name="reduce_gradients",
        out_shape=[jax.ShapeDtypeStruct(s, F32) for s in SHARD_SHAPES] + [jax.ShapeDtypeStruct((8, D_MODEL), F32)]
        + [jax.ShapeDtypeStruct(SHARD_SHAPES[a], F32) for a in updated for _ in range(3)],
        in_specs=[hbm] * n + [vmem] * n_small + [hbm] * (3 * len(updated)),
        out_specs=[vmem] * (n + 1) + [hbm] * (3 * len(updated)),
        scratch_shapes=[pltpu.VMEM((N_SHARD,) + s, F32) for s in SHARD_SHAPES]
        + [pltpu.VMEM((N_SHARD,) + s, F32) for s in halves] + [pltpu.VMEM((N_SHARD,) + s, BF16) for s in halves]
        + [pltpu.VMEM((3,) + s, BF16) for s in halves]
        + [pltpu.VMEM((8, D_MODEL), F32), pltpu.VMEM((8, 8, D_MODEL), F32),
           pltpu.SemaphoreType.DMA((5 * n + 8,)), pltpu.SemaphoreType.DMA((5 * n + 8,)),
           pltpu.SemaphoreType.DMA((n + len(pieces) + 3 * len(updated),))],
        compiler_params=_cp(None, VMEM_LIMIT),
    )(*grads, *small_rows, *[s for a in updated for s in adam_states[a]])
    new_states = {a: out[n + 1 + 3 * i:n + 1 + 3 * (i + 1)] for i, a in enumerate(updated)}
    return out[:n], out[n], new_states


def _proj(x, w_in_p, gq, gkv, wuq_e, wukv, ct, st, cd, sd):
    seq = x.shape[0]
    tr = 512

    def body(x_ref, w_ref, gq_ref, gkv_ref, wuq_ref, wukv_ref, ct_ref, st_ref, cd_ref, sd_ref,
             cq_ref, ckv_ref, g_ref, qr_ref, kr_ref, vb_ref, q_out, k_out, v_out):
        lane = lax.broadcasted_iota(jnp.int32, (tr, LANES), 1)
        xb = x_ref[...].astype(BF16)
        cq = _dot(xb, w_ref[0:384, :], NT)
        ckv = _dot(xb, w_ref[384:640, :], NT)
        cq_ref[...] = cq
        ckv_ref[...] = ckv
        g_ref[...] = _dot(xb, w_ref[640:1664, :], NT)

        cd_, sd_ = cd_ref[...], sd_ref[...]
        qb = _dot(xb, w_ref[1664:2176, :], NT)
        kb = _dot(xb, w_ref[2176:2688, :], NT)
        for p in range(4):
            cols = slice(LANES * p, LANES * (p + 1))
            t = qb[:, cols]
            qr_ref[:, cols] = (t * cd_ + _dil_rot(t, lane) * sd_) * DIL_SCALE
            t = kb[:, cols]
            kr_ref[:, cols] = t * cd_ + _dil_rot(t, lane) * sd_
        vb_ref[...] = _dot(xb, w_ref[2688:3200, :], NT)

        ct_, st_ = ct_ref[...], st_ref[...]

        def rope(t):
            return t * ct_ + _mla_rot(t, lane) * st_

        _, qn = _rms(cq, gq_ref[...])
        q_all = _dot(qn.astype(BF16), wuq_ref[...])
        for h in range(MLA_HEADS):
            q_out[h] = (rope(q_all[:, LANES * h:LANES * (h + 1)]) * MLA_SCALE).astype(BF16)
        _, kvn = _rms(ckv, gkv_ref[...])
        kv_all = _dot(kvn.astype(BF16), wukv_ref[...])
        kpe = rope(_dot(xb, w_ref[3200:3328, :], NT))
        for h in range(MLA_HEADS):
            kv_h = kv_all[:, LANES * h:LANES * (h + 1)]
            k_out[h] = jnp.where(lane < 64, kv_h, kpe).astype(BF16)
            if h % 2:
                v = jnp.where(lane >= 64, kv_h, 0.0)
            else:
                v = jnp.where(lane < 64, pltpu.roll(kv_h, 64, 1), 0.0)
            v_out[h] = jnp.where(lane == ONES_LANE[h % 2], 1.0, v).astype(BF16)

    row = lambda w: pl.BlockSpec((tr, w), lambda i: (i, 0))
    full = lambda a: pl.BlockSpec(a.shape, lambda i: (0,) * a.ndim)
    head = pl.BlockSpec((MLA_HEADS, tr, LANES), lambda i: (0, i, 0))
    widths = (Q_LORA, KV_LORA, D_MODEL, 512, 512, 512)
    return pl.pallas_call(
        body, name="proj", grid=(seq // tr,),
        in_specs=[row(D_MODEL), full(w_in_p), full(gq), full(gkv), full(wuq_e), full(wukv)] + [row(LANES)] * 4,
        out_specs=[row(w) for w in widths] + [head] * 3,
        out_shape=[jax.ShapeDtypeStruct((seq, w), F32) for w in widths]
        + [jax.ShapeDtypeStruct((MLA_HEADS, seq, LANES), BF16)] * 3,
        compiler_params=_cp(("arbitrary",), VMEM_LIMIT),
    )(x, w_in_p, gq, gkv, wuq_e, wukv, ct, st, cd, sd)


def _mla_rot(t, lane):
    return jnp.where(lane < 80, pltpu.roll(t, 112, 1), pltpu.roll(t, 16, 1))


def _dil_rot(t, lane):
    return jnp.where(lane % 64 < 8, pltpu.roll(t, 120, 1), pltpu.roll(t, 8, 1))


def _rms(c, g):
    r = lax.rsqrt(jnp.mean(c * c, axis=-1, keepdims=True) + RMS_EPS)
    return r, c * r * g


def _mla_fwd(q, k, v):
    seq = q.shape[1]
    tq = 512
    nq = seq // tq

    def body(q_ref, k_ref, v_ref, o_ref, lse_ref, m_s, acc_s, s_buf):
        i = pl.program_id(1)
        row = lax.broadcasted_iota(jnp.int32, (tq, tq), 0)
        col = lax.broadcasted_iota(jnp.int32, (tq, tq), 1)
        lane = lax.broadcasted_iota(jnp.int32, (tq, LANES), 1)
        m_s[...] = jnp.full((2, tq, LANES), NEG, F32)
        acc_s[...] = jnp.zeros((2, tq, LANES), F32)

        def block(j):
            return pl.ds(pl.multiple_of(j * tq, tq), tq)

        def scores(hh, j):
            return _dot(q_ref[hh], k_ref[hh, block(j), :], NT)

        def consume(hh, j, s):
            m_prev = m_s[hh]
            m_new = jnp.maximum(m_prev, jnp.max(s, axis=1, keepdims=True))
            p = jnp.exp(s - m_new[:, :1])
            acc_s[hh] = jnp.exp(m_prev - m_new) * acc_s[hh] + _dot(p.astype(BF16), v_ref[hh, block(j), :])
            m_s[hh] = m_new

        for hh in range(2):
            s_buf[0, hh] = scores(hh, 0)

        def full_step(j, carry):
            slot = j & 1
            for hh in range(2):
                s = s_buf[slot, hh]
                s_buf[1 - slot, hh] = scores(hh, j + 1)
                consume(hh, j, s)
            return carry

        lax.fori_loop(0, i, full_step, 0)
        total = jnp.zeros((tq, LANES), F32)
        for hh in range(2):
            consume(hh, i, jnp.where(col <= row, s_buf[i & 1, hh], NEG))
            acc = acc_s[hh]
            l = acc[:, ONES_LANE[hh]:ONES_LANE[hh] + 1]
            mine = (lane >= 64) if hh else (lane < 64)
            total = total + jnp.where(mine, acc / l, 0.0)
            lse_ref[hh] = m_s[hh] + jnp.log(l)
        o_ref[...] = total

    kv_spec = pl.BlockSpec((2, seq, LANES), lambda p, i: (p, 0, 0))
    return pl.pallas_call(
        body, name="mla_fwd", grid=(MLA_HEADS // 2, nq),
        in_specs=[pl.BlockSpec((2, tq, LANES), lambda p, i: (p, i, 0)), kv_spec, kv_spec],
        out_specs=[pl.BlockSpec((tq, LANES), lambda p, i: (i, p)), pl.BlockSpec((2, tq, LANES), lambda p, i: (p, i, 0))],
        out_shape=[jax.ShapeDtypeStruct((seq, 4 * LANES), F32), jax.ShapeDtypeStruct((MLA_HEADS, seq, LANES), F32)],
        scratch_shapes=[pltpu.VMEM((2, tq, LANES), F32), pltpu.VMEM((2, tq, LANES), F32),
                        pltpu.VMEM((2, 2, tq, tq), F32)],
        compiler_params=_cp(("arbitrary", "arbitrary"), VMEM_LIMIT),
    )(q, k, v)


DIL_Q_FWD = 2 * BLOCK
DIL_Q_BWD = BLOCK


def _dil_tile_index(t, d, seq, nq):
    per_class = seq // (nq * d)
    shift = per_class.bit_length() - 1
    r = t >> shift
    n = t & (per_class - 1)
    start = r + (nq * d) * n
    prev = jnp.maximum(start - BLOCK * d, r)
    if d == 1:
        start = pl.multiple_of(start, nq)
        prev = pl.multiple_of(prev, BLOCK)
    return (n == 0).astype(jnp.int32), start, prev


def _dil_rows(start, d, size):
    return pl.ds(start, size) if d == 1 else pl.ds(start, size, stride=d)


def _dil_bias(nq):
    i = lax.broadcasted_iota(jnp.int32, (2 * nq, BLOCK + nq), 0) % nq
    j = lax.broadcasted_iota(jnp.int32, (2 * nq, BLOCK + nq), 1)
    band = (j >= i) & (j <= i + BLOCK)
    return jnp.where(band, 0.0, NEG), jnp.where(band & (j >= BLOCK), 0.0, NEG)


def _stack_heads(t, lane):
    return jnp.concatenate([jnp.where(lane < 64, t, 0.0), jnp.where(lane >= 64, t, 0.0)], axis=0)


def _unstack_heads(t, lane):
    nq = t.shape[0] // 2
    return jnp.where(lane < 64, t[:nq], t[nq:])


def _dil_fwd(qr, kr, vb):
    seq = qr.shape[0]
    nq = DIL_Q_FWD
    n_tiles = seq // nq
    assert seq % (nq * max(DIL_DILATIONS)) == 0

    def body(q_ref, k_ref, v_ref, o_ref, lse_ref, m_s, l_s, n_s, bias_s):
        lane = lax.broadcasted_iota(jnp.int32, (nq, LANES), 1)
        bias_s[0], bias_s[1] = _dil_bias(nq)
        for bi, d in enumerate(DIL_DILATIONS):

            def tile(t, carry, d=d, bi=bi):
                first, start, prev = _dil_tile_index(t, d, seq, nq)
                rows, prows = _dil_rows(start, d, nq), _dil_rows(prev, d, BLOCK)
                qst = _stack_heads(q_ref[rows, :], lane).astype(BF16)
                if seq == nq * d:
                    kcat, vcat = k_ref[rows, :].astype(BF16), v_ref[rows, :].astype(BF16)
                    s = _dot(qst, kcat, NT) + bias_s[1, :, BLOCK:]
                else:
                    kcat = jnp.concatenate([k_ref[prows, :], k_ref[rows, :]], axis=0).astype(BF16)
                    vcat = jnp.concatenate([v_ref[prows, :], v_ref[rows, :]], axis=0).astype(BF16)
                    s = _dot(qst, kcat, NT) + bias_s[first]
                m = jnp.max(s, axis=1, keepdims=True)
                p = jnp.exp(s - m)
                l2 = _unstack_heads(jnp.sum(p, axis=1, keepdims=True) + jnp.zeros((2 * nq, LANES), F32), lane)
                m2 = _unstack_heads(m + jnp.zeros((2 * nq, LANES), F32), lane)
                num2 = _unstack_heads(_dot(p.astype(BF16), vcat), lane)
                if bi == 0:
                    m_s[rows, :] = m2
                    l_s[rows, :] = l2
                    n_s[rows, :] = num2
                else:
                    m_old = m_s[rows, :]
                    m_new = jnp.maximum(m_old, m2)
                    a = jnp.exp(m_old - m_new)
                    b = jnp.exp(m2 - m_new)
                    m_s[rows, :] = m_new
                    l_s[rows, :] = a * l_s[rows, :] + b * l2
                    n_s[rows, :] = a * n_s[rows, :] + b * num2
                return carry

            lax.fori_loop(0, n_tiles, tile, 0, unroll=8)
        o_ref[...] = n_s[...] / l_s[...]
        lse_ref[...] = m_s[...] + jnp.log(l_s[...])

    col = lambda off: pl.BlockSpec((seq, LANES), lambda p: (0, p + off))
    return pl.pallas_call(
        body, name="dil_fwd", grid=(4,),
        in_specs=[col(0), col(0), col(0)],
        out_specs=[col(0), pl.BlockSpec((None, seq, LANES), lambda p: (p, 0, 0))],
        out_shape=[jax.ShapeDtypeStruct((seq, 4 * LANES), F32), jax.ShapeDtypeStruct((4, seq, LANES), F32)],
        scratch_shapes=[pltpu.VMEM((seq, LANES), F32)] * 3 + [pltpu.VMEM((2, 2 * nq, BLOCK + nq), F32)],
        compiler_params=_cp(("arbitrary",), VMEM_LIMIT),
    )(qr, kr, vb)


def _post(x, o_a, o_b, gates, w_out, ln_g, ln_b, target):
    seq = x.shape[0]
    tr = 512

    def body(x_ref, oa_ref, ob_ref, g_ref, w_ref, lg_ref, lb_ref, t_ref,
             dz_ref, do_ref, dg_ref, dw_ref, dlg_ref, dlb_ref, loss_ref):
        @pl.when(pl.program_id(0) == 0)
        def _():
            dw_ref[...] = jnp.zeros_like(dw_ref)
            dlg_ref[...] = jnp.zeros_like(dlg_ref)
            dlb_ref[...] = jnp.zeros_like(dlb_ref)
            loss_ref[...] = jnp.zeros_like(loss_ref)

        g = g_ref[...]
        sg = jax.nn.sigmoid(g)
        silu = g * sg
        o = jnp.concatenate([oa_ref[...], ob_ref[...]], axis=1)
        mixb = (o * silu).astype(BF16)
        w = w_ref[...]
        z = ALPHA * x_ref[...] + _dot(mixb, w)
        mu = jnp.mean(z, axis=-1, keepdims=True)
        zc = z - mu
        rstd = lax.rsqrt(jnp.mean(zc * zc, axis=-1, keepdims=True) + LN_EPS)
        xhat = zc * rstd
        lg = lg_ref[...]
        err = xhat * lg + lb_ref[...] - t_ref[...]
        loss_ref[...] += jnp.sum(err * err) * (0.5 / D_MODEL)
        dy = err * (1.0 / D_MODEL)
        dlg_ref[...] += jnp.sum(dy * xhat, axis=0, keepdims=True)
        dlb_ref[...] += jnp.sum(dy, axis=0, keepdims=True)
        dxh = dy * lg
        dz = rstd * (dxh - jnp.mean(dxh, axis=-1, keepdims=True) - xhat * jnp.mean(dxh * xhat, axis=-1, keepdims=True))
        dz_ref[...] = dz
        dzb = dz.astype(BF16)
        dmix = _dot(dzb, w, NT)
        do_ref[...] = dmix * silu
        dg_ref[...] = (dmix * o * (sg * (1.0 + g * (1.0 - sg)))).astype(BF16)
        dw_ref[...] += _dot(mixb, dzb, TN)

    row = lambda w: pl.BlockSpec((tr, w), lambda i: (i, 0))
    full = lambda s: pl.BlockSpec(s, lambda i: (0, 0))
    return pl.pallas_call(
        body, name="post", grid=(seq // tr,),
        in_specs=[row(D_MODEL), row(512), row(512), row(D_MODEL), full((D_MODEL, D_MODEL)), full((1, D_MODEL)),
                  full((1, D_MODEL)), row(D_MODEL)],
        out_specs=[row(D_MODEL), row(D_MODEL), row(D_MODEL), full((D_MODEL, D_MODEL)), full((1, D_MODEL)),
                   full((1, D_MODEL)), full((1, LANES))],
        out_shape=[jax.ShapeDtypeStruct((seq, D_MODEL), F32), jax.ShapeDtypeStruct((seq, D_MODEL), F32),
                   jax.ShapeDtypeStruct((seq, D_MODEL), BF16), jax.ShapeDtypeStruct((D_MODEL, D_MODEL), F32),
                   jax.ShapeDtypeStruct((1, D_MODEL), F32), jax.ShapeDtypeStruct((1, D_MODEL), F32),
                   jax.ShapeDtypeStruct((1, LANES), F32)],
        compiler_params=_cp(("arbitrary",), VMEM_LIMIT),
    )(x, o_a, o_b, gates, w_out, ln_g, ln_b, target)


def _mla_bwd(q, k, v, d_o, o, lse):
    seq = q.shape[1]
    tq = 512
    nq = seq // tq

    def body(q_ref, k_ref, v_ref, do_ref, o_ref, lse_ref, dq_ref, dk_ref, dv_ref, d_s, lse_s, dk_s, dv_s, v_s, kt_s, dqt_s):
        j = pl.program_id(1)
        lane = lax.broadcasted_iota(jnp.int32, (tq, LANES), 1)
        row = lax.broadcasted_iota(jnp.int32, (tq, tq), 0)
        col = lax.broadcasted_iota(jnp.int32, (tq, tq), 1)

        @pl.when(j == 0)
        def _():
            dqt_s[...] = jnp.zeros_like(dqt_s)

            def rowsum(i, carry):
                rows = pl.ds(pl.multiple_of(i * tq, tq), tq)
                prod = do_ref[rows, :] * o_ref[rows, :]
                for hh in range(2):
                    mine = (lane >= 64) if hh else (lane < 64)
                    total = jnp.sum(jnp.where(mine, prod, 0.0), axis=1, keepdims=True)
                    d_s[hh, i] = jnp.transpose(total + jnp.zeros((tq, LANES), F32))[:8]
                    lse_s[hh, i] = jnp.transpose(lse_ref[hh, rows, :])[:8]
                return carry

            lax.fori_loop(0, nq, rowsum, 0)

        dk_s[...] = jnp.zeros_like(dk_s)
        dv_s[...] = jnp.zeros_like(dv_s)
        for hh in range(2):
            v_s[hh] = jnp.where(lane == ONES_LANE[hh], 0.0, v_ref[hh].astype(F32)).astype(BF16)
            kt_s[hh] = jnp.transpose(k_ref[hh].astype(F32)).astype(BF16)

        def step(i, masked):
            rows = pl.ds(pl.multiple_of(i * tq, tq), tq)
            dob = do_ref[rows, :].astype(BF16)
            for hh in range(2):
                qb, kb, vb = q_ref[hh, rows, :], k_ref[hh], v_s[hh]
                p = jnp.exp(_dot(kb, qb, NT) - lse_s[hh, i][:1])
                if masked:
                    p = jnp.where(row <= col, p, 0.0)
                dv_s[hh] += _dot(p.astype(BF16), dob)
                ds = (p * (_dot(vb, dob, NT) - d_s[hh, i][:1])).astype(BF16)
                dk_s[hh] += _dot(ds, qb)
                dqt_s[hh, i] += _dot(kt_s[hh], ds)

        def full_step(i, carry):
            step(i, False)
            return carry

        step(j, True)
        lax.fori_loop(j + 1, nq, full_step, 0)
        dk_ref[...] = dk_s[...]
        dv_ref[...] = dv_s[...]

        @pl.when(j == nq - 1)
        def _():
            def untranspose(i, carry):
                rows = pl.ds(pl.multiple_of(i * tq, tq), tq)
                for hh in range(2):
                    dq_ref[hh, rows, :] = jnp.transpose(dqt_s[hh, i])
                return carry

            lax.fori_loop(0, nq, untranspose, 0)

    whole = pl.BlockSpec((2, seq, LANES), lambda p, j: (p, 0, 0))
    blk = pl.BlockSpec((2, tq, LANES), lambda p, j: (p, j, 0))
    pair = pl.BlockSpec((seq, LANES), lambda p, j: (0, p))
    shape = jax.ShapeDtypeStruct((MLA_HEADS, seq, LANES), F32)
    return pl.pallas_call(
        body, name="mla_bwd", grid=(MLA_HEADS // 2, nq),
        in_specs=[whole, blk, blk, pair, pair, whole],
        out_specs=[whole, blk, blk], out_shape=[shape] * 3,
        scratch_shapes=[pltpu.VMEM((2, nq, 8, tq), F32), pltpu.VMEM((2, nq, 8, tq), F32),
                        pltpu.VMEM((2, tq, LANES), F32), pltpu.VMEM((2, tq, LANES), F32),
                        pltpu.VMEM((2, tq, LANES), BF16), pltpu.VMEM((2, LANES, tq), BF16),
                        pltpu.VMEM((2, nq, LANES, tq), F32)],
        compiler_params=_cp(("arbitrary", "arbitrary"), VMEM_LIMIT),
    )(q, k, v, d_o, o, lse)


def _dil_bwd(qr, kr, vb, d_o, o, lse):
    seq = qr.shape[0]
    nq = DIL_Q_BWD
    n_tiles = seq // nq
    chunk = 512

    def body(q_ref, k_ref, v_ref, do_ref, o_ref, lse_ref, dq_ref, dk_ref, dv_ref, d_s, dq_s, dk_s, dv_s, bias_s):
        lane = lax.broadcasted_iota(jnp.int32, (nq, LANES), 1)
        lanec = lax.broadcasted_iota(jnp.int32, (chunk, LANES), 1)
        bias_s[0], bias_s[1] = [b[:nq] for b in _dil_bias(nq)]

        def rowsum(i, carry):
            rows = pl.ds(pl.multiple_of(i * chunk, chunk), chunk)
            prod = do_ref[rows, :] * o_ref[rows, :]
            lo = jnp.sum(jnp.where(lanec < 64, prod, 0.0), axis=1, keepdims=True)
            hi = jnp.sum(jnp.where(lanec >= 64, prod, 0.0), axis=1, keepdims=True)
            d_s[rows, :] = jnp.where(lanec < 64, lo, hi)
            return carry

        lax.fori_loop(0, seq // chunk, rowsum, 0)
        dq_s[...] = jnp.zeros_like(dq_s)
        dk_s[...] = jnp.zeros_like(dk_s)
        dv_s[...] = jnp.zeros_like(dv_s)
        for d in DIL_DILATIONS:

            def tile(start, prev, first, d=d):
                rows = _dil_rows(start, d, nq)
                q_t, do_t = q_ref[rows, :], do_ref[rows, :]
                lse_t, d_t = lse_ref[rows, :], d_s[rows, :]
                if prev is None:
                    kcat, vcat = k_ref[rows, :].astype(BF16), v_ref[rows, :].astype(BF16)
                    bias = bias_s[1, :, BLOCK:]
                else:
                    prows = _dil_rows(prev, d, BLOCK)
                    kcat = jnp.concatenate([k_ref[prows, :], k_ref[rows, :]], axis=0).astype(BF16)
                    vcat = jnp.concatenate([v_ref[prows, :], v_ref[rows, :]], axis=0).astype(BF16)
                    bias = bias_s[first]
                dq_t = jnp.zeros((nq, LANES), F32)
                dkcat = jnp.zeros((kcat.shape[0], LANES), F32)
                dvcat = jnp.zeros((kcat.shape[0], LANES), F32)
                for hh in range(2):
                    mine = (lane >= 64) if hh else (lane < 64)
                    c0 = 64 * hh
                    qh = jnp.where(mine, q_t, 0.0).astype(BF16)
                    doh = jnp.where(mine, do_t, 0.0).astype(BF16)
                    p = jnp.exp(_dot(qh, kcat, NT) + bias - lse_t[:, c0:c0 + 1])
                    dvcat = dvcat + _dot(p.astype(BF16), doh, TN)
                    dp = _dot(doh, vcat, NT)
                    ds = (p * (dp - d_t[:, c0:c0 + 1])).astype(BF16)
                    dq_t = dq_t + jnp.where(mine, _dot(ds, kcat), 0.0)
                    dkcat = dkcat + _dot(ds, qh, TN)
                dq_s[rows, :] += dq_t
                if prev is not None:
                    dk_s[prows, :] += dkcat[:BLOCK]
                    dv_s[prows, :] += dvcat[:BLOCK]
                dk_s[rows, :] += dkcat[-nq:]
                dv_s[rows, :] += dvcat[-nq:]

            if seq == 2 * nq * d:

                def class_tiles(r, carry, d=d):
                    tile(r, None, 1)
                    tile(r + nq * d, r, 0)
                    return carry

                lax.fori_loop(0, d, class_tiles, 0, unroll=8)
            else:

                def any_tile(t, carry, d=d):
                    first, start, prev = _dil_tile_index(t, d, seq, nq)
                    tile(start, prev, first)
                    return carry

                lax.fori_loop(0, n_tiles, any_tile, 0, unroll=16)
        dq_ref[...] = dq_s[...].astype(BF16)
        dk_ref[...] = dk_s[...].astype(BF16)
        dv_ref[...] = dv_s[...].astype(BF16)

    col = lambda off: pl.BlockSpec((seq, LANES), lambda p: (0, p + off))
    shape = jax.ShapeDtypeStruct((seq, 4 * LANES), BF16)
    return pl.pallas_call(
        body, name="dil_bwd", grid=(4,),
        in_specs=[col(0), col(0), col(0), col(4), col(0), pl.BlockSpec((None, seq, LANES), lambda p: (p, 0, 0))],
        out_specs=[col(0)] * 3, out_shape=[shape] * 3,
        scratch_shapes=[pltpu.VMEM((seq, LANES), F32)] * 4 + [pltpu.VMEM((2, nq, BLOCK + nq), F32)],
        compiler_params=_cp(("arbitrary",), VMEM_LIMIT),
    )(qr, kr, vb, d_o, o, lse)


def _in_bwd(dz, cq, ckv, gq, gkv, wuq_e, wukv, ct, st, dq, dk, dv, dgates, dqr, dkr, dvb, cd, sd, w_in_p):
    seq = dz.shape[0]
    tr = 512

    def body(dz_ref, cq_ref, ckv_ref, gq_ref, gkv_ref, wuq_ref, wukv_ref, ct_ref, st_ref, dq_ref, dk_ref, dv_ref,
             dg_ref, dqr_ref, dkr_ref, dvb_ref, cd_ref, sd_ref, w_ref,
             gx_ref, dh_ref, dwuq_ref, dwukv_ref, dgq_ref, dgkv_ref):
        @pl.when(pl.program_id(0) == 0)
        def _():
            dwuq_ref[...] = jnp.zeros_like(dwuq_ref)
            dwukv_ref[...] = jnp.zeros_like(dwukv_ref)
            dgq_ref[...] = jnp.zeros_like(dgq_ref)
            dgkv_ref[...] = jnp.zeros_like(dgkv_ref)

        lane = lax.broadcasted_iota(jnp.int32, (tr, LANES), 1)
        rope_lanes = jnp.logical_and(lane >= 64, lane < 96)
        ct_, st_ = ct_ref[...], st_ref[...]

        def mla_rope_t(g):
            return ct_ * g + jnp.where(rope_lanes, _mla_rot(st_ * g, lane), 0.0)

        def norm_bwd(c, g, dn, dg_ref):
            r, _ = _rms(c, g)
            u = dn * g
            dg_ref[...] += jnp.sum(dn * c * r, axis=0, keepdims=True)
            return r * u - c * (r * r * r) * jnp.mean(u * c, axis=-1, keepdims=True)

        c, g = cq_ref[...], gq_ref[...]
        _, qn = _rms(c, g)
        dq_all = jnp.concatenate([mla_rope_t(dq_ref[h] * MLA_SCALE) for h in range(MLA_HEADS)], axis=1).astype(BF16)
        dwuq_ref[...] += _dot(qn.astype(BF16), dq_all, TN)
        dcq = norm_bwd(c, g, _dot(dq_all, wuq_ref[...], NT), dgq_ref).astype(BF16)

        c, g = ckv_ref[...], gkv_ref[...]
        _, kvn = _rms(c, g)
        dkpe = jnp.zeros((tr, LANES), F32)
        parts = []
        for h in range(MLA_HEADS):
            dk_h, dv_h = dk_ref[h], dv_ref[h]
            if h % 2 == 0:
                dv_h = pltpu.roll(dv_h, 64, 1)
            parts.append(jnp.where(lane < 64, dk_h, dv_h))
            dkpe = dkpe + jnp.where(rope_lanes, dk_h, 0.0)
        dkv_all = jnp.concatenate(parts, axis=1).astype(BF16)
        dwukv_ref[...] += _dot(kvn.astype(BF16), dkv_all, TN)
        dckv = norm_bwd(c, g, _dot(dkv_all, wukv_ref[...], NT), dgkv_ref).astype(BF16)
        dkrope = mla_rope_t(dkpe).astype(BF16)

        rot_lanes = lane % 64 < DIL_ROT
        cd_, sd_ = cd_ref[...], sd_ref[...]

        def dil_rope_t(g):
            return cd_ * g + jnp.where(rot_lanes, _dil_rot(sd_ * g, lane), 0.0)

        dqb = [dil_rope_t(dqr_ref[:, LANES * p:LANES * (p + 1)].astype(F32) * DIL_SCALE).astype(BF16) for p in range(4)]
        dkb = [dil_rope_t(dkr_ref[:, LANES * p:LANES * (p + 1)].astype(F32)).astype(BF16) for p in range(4)]
        dh = jnp.concatenate([dcq, dckv, dg_ref[...]] + dqb + dkb + [dvb_ref[...], dkrope], axis=1)
        dh_ref[...] = dh
        gx_ref[...] = ALPHA * dz_ref[...] + _dot(dh, w_ref[...])

    row = lambda w: pl.BlockSpec((tr, w), lambda i: (i, 0))
    full = lambda a: pl.BlockSpec(a.shape, lambda i: (0,) * a.ndim)
    head = pl.BlockSpec((MLA_HEADS, tr, LANES), lambda i: (0, i, 0))
    return pl.pallas_call(
        body, name="in_bwd", grid=(seq // tr,),
        in_specs=[row(D_MODEL), row(Q_LORA), row(KV_LORA), full(gq), full(gkv), full(wuq_e), full(wukv), row(LANES),
                  row(LANES), head, head, head, row(D_MODEL), row(512), row(512), row(512), row(LANES), row(LANES),
                  full(w_in_p)],
        out_specs=[row(D_MODEL), row(IN_WIDTH_PAD), full(wuq_e), full(wukv), full(gq), full(gkv)],
        out_shape=[jax.ShapeDtypeStruct((seq, D_MODEL), F32), jax.ShapeDtypeStruct((seq, IN_WIDTH_PAD), BF16),
                   jax.ShapeDtypeStruct(wuq_e.shape, F32), jax.ShapeDtypeStruct(wukv.shape, F32),
                   jax.ShapeDtypeStruct(gq.shape, F32), jax.ShapeDtypeStruct(gkv.shape, F32)],
        compiler_params=_cp(("arbitrary",), VMEM_LIMIT),
    )(dz, cq, ckv, gq, gkv, wuq_e, wukv, ct, st, dq, dk, dv, dgates, dqr, dkr, dvb, cd, sd, w_in_p)


def _dw_in(x, dh):
    seq = dh.shape[0]
    tk = 512
    tn = IN_WIDTH_PAD // 2

    def body(x_ref, dh_ref, o_ref):
        @pl.when(pl.program_id(1) == 0)
        def _():
            o_ref[...] = jnp.zeros_like(o_ref)

        o_ref[...] += _dot(dh_ref[...], x_ref[...].astype(BF16), TN)

    return pl.pallas_call(
        body, name="dw_in", grid=(2, seq // tk),
        in_specs=[pl.BlockSpec((tk, D_MODEL), lambda n, k: (k, 0)), pl.BlockSpec((tk, tn), lambda n, k: (k, n))],
        out_specs=pl.BlockSpec((tn, D_MODEL), lambda n, k: (n, 0)),
        out_shape=jax.ShapeDtypeStruct((IN_WIDTH_PAD, D_MODEL), F32),
        compiler_params=_cp(("arbitrary", "arbitrary"), VMEM_LIMIT),
    )(x, dh)


def _adam_update(w, g, m, v):
    nm = ADAM_B1 * m + (1.0 - ADAM_B1) * g
    nv = ADAM_B2 * v + (1.0 - ADAM_B2) * jnp.square(g)
    m_hat = nm / (1.0 - ADAM_B1 ** ADAM_STEP)
    v_hat = nv / (1.0 - ADAM_B2 ** ADAM_STEP)
    return -ADAM_LR * (m_hat / (jnp.sqrt(v_hat) + ADAM_EPS) + ADAM_WD * w), nm, nv


def _adamw(w, g, m, v, name):
    rows, cols = w.shape
    tc = 256 if cols % 256 == 0 and rows * cols > 2 ** 18 else cols

    def body(w_ref, g_ref, m_ref, v_ref, d_ref, nm_ref, nv_ref):
        d_ref[...], nm_ref[...], nv_ref[...] = _adam_update(w_ref[...], g_ref[...], m_ref[...], v_ref[...])

    spec = pl.BlockSpec((rows, tc), lambda i: (0, i))
    return pl.pallas_call(
        body, name=name, grid=(cols // tc,), in_specs=[spec] * 4, out_specs=[spec] * 3,
        out_shape=[jax.ShapeDtypeStruct(w.shape, F32)] * 3, compiler_params=_cp(("arbitrary",)),
    )(w, g, m, v)


def _adamw_vectors(small_sum, ws, ms, vs):
    k = len(ws)
    sizes = [w.shape[0] for w in ws]

    def body(s_ref, *refs):
        ins, outs = refs[:3 * k], refs[3 * k:]
        for i, size in enumerate(sizes):
            g = s_ref[i, 0:size]
            outs[i][...] = g
            outs[k + i][...], outs[2 * k + i][...], outs[3 * k + i][...] = _adam_update(
                ins[i][...], g, ins[k + i][...], ins[2 * k + i][...])

    out = pl.pallas_call(
        body, name="adamw_vectors", out_shape=[jax.ShapeDtypeStruct((size,), F32) for size in sizes] * 4,
    )(small_sum, *ws, *ms, *vs)
    return [out[k * j:k * (j + 1)] for j in range(4)]


def _local_step(x2, target, w_in_p, w_uq_f, wukv_f, w_out_f, q_norm_g, kv_norm_g, ln_g, ln_b):
    seq = x2.shape[0]
    wuq_e = jnp.pad(w_uq_f.reshape(Q_LORA, MLA_HEADS, 96), ((0, 0), (0, 0), (0, 32))).reshape(Q_LORA, MLA_HEADS * LANES)
    ct, st, cd, sd = _rope_tables(seq)
    gq = q_norm_g.reshape(1, Q_LORA)
    gkv = kv_norm_g.reshape(1, KV_LORA)

    cq, ckv, gates, qr, krot, vb, q_e, k_e, v_e = _proj(x2, w_in_p, gq, gkv, wuq_e, wukv_f, ct, st, cd, sd)
    o_a, lse_a = _mla_fwd(q_e, k_e, v_e)
    o_b, lse_b = _dil_fwd(qr, krot, vb)

    dz, d_o, d_gates, dw_out, dln_g, dln_b, loss_part = _post(
        x2, o_a, o_b, gates, w_out_f, ln_g.reshape(1, D_MODEL), ln_b.reshape(1, D_MODEL), target)
    dq_e, dk_e, dv_e = _mla_bwd(q_e, k_e, v_e, d_o, o_a, lse_a)
    dqr, dkr, dvb = _dil_bwd(qr, krot, vb, d_o, o_b, lse_b)
    grad_x, dh, dwuq_e, dwukv, dgq, dgkv = _in_bwd(
        dz, cq, ckv, gq, gkv, wuq_e, wukv_f, ct, st, dq_e, dk_e, dv_e, d_gates, dqr, dkr, dvb, cd, sd, w_in_p)
    dw_in = _dw_in(x2, dh)
    dw_uq = dwuq_e.reshape(Q_LORA, MLA_HEADS, LANES)[:, :, :96].reshape(Q_LORA, MLA_HEADS * 96)
    return loss_part, grad_x, dw_in, dw_uq, dwukv, dw_out, dgq, dgkv, dln_g, dln_b


def kernel(x, w_in, q_norm_g, kv_norm_g, w_uq, w_ukv, w_out, ln_g, ln_b, loss_target, m_w_in, m_q_norm_g, m_kv_norm_g, m_w_uq, m_w_ukv, m_w_out, m_ln_g, m_ln_b, v_w_in, v_q_norm_g, v_kv_norm_g, v_w_uq, v_w_ukv, v_w_out, v_ln_g, v_ln_b):
    seq = x.shape[1]
    x2 = x.reshape(seq, D_MODEL)
    target = loss_target.reshape(seq, D_MODEL)

    g_w_in, g_w_uq, g_w_ukv, g_w_out = _all_gather_weights([w_in.T, w_uq, w_ukv, w_out])
    by_cols = lambda g: jnp.concatenate([g[j] for j in range(N_SHARD)], axis=1)
    loss_part, grad_x, dw_in, dw_uq, dwukv, dw_out, dgq, dgkv, dln_g, dln_b = _local_step(
        x2, target, g_w_in, by_cols(g_w_uq), by_cols(g_w_ukv), g_w_out.reshape(D_MODEL, D_MODEL),
        q_norm_g, kv_norm_g, ln_g, ln_b)

    to_shards = lambda d: d.reshape(d.shape[0], N_SHARD, d.shape[1] // N_SHARD).transpose(1, 0, 2)
    grads = [dw_in, to_shards(dw_uq), to_shards(dwukv), dw_out.reshape(N_SHARD, 256, D_MODEL)]
    (g_in_t, g_uq, g_ukv, g_out), small_sum, new = _reduce_gradients(
        grads, [dgq, dgkv, dln_g, dln_b, loss_part],
        {0: (w_in.T, m_w_in.T, v_w_in.T), 2: (w_ukv, m_w_ukv, v_w_ukv), 3: (w_out, m_w_out, v_w_out)})
    g_in = g_in_t.T
    loss = small_sum[4, 0]
    big = [[o.T for o in new[0]], [o.T for o in _adamw(w_uq.T, g_uq.T, m_w_uq.T, v_w_uq.T, "adamw_w_uq")], new[2], new[3]]
    vec_g, vec_delta, vec_m, vec_v = _adamw_vectors(
        small_sum, [q_norm_g, kv_norm_g, ln_g, ln_b], [m_q_norm_g, m_kv_norm_g, m_ln_g, m_ln_b],
        [v_q_norm_g, v_kv_norm_g, v_ln_g, v_ln_b])

    def ordered(bigs, vecs):
        return [bigs[0], vecs[0], vecs[1], bigs[1], bigs[2], bigs[3], vecs[2], vecs[3]]

    grads_out = ordered([g_in, g_uq, g_ukv, g_out], vec_g)
    deltas = ordered([b[0] for b in big], vec_delta)
    new_m = ordered([b[1] for b in big], vec_m)
    new_v = ordered([b[2] for b in big], vec_v)
    return (loss, grad_x.reshape(x.shape), *grads_out, *deltas, *new_m, *new_v)
```

```python
import jax
import jax.numpy as jnp
import numpy as np
from jax import lax
from jax.experimental import pallas as pl
from jax.experimental.pallas import tpu as pltpu

F32 = jnp.float32
BF16 = jnp.bfloat16

D_MODEL = 1024
ROPE_THETA = 500000.0
BLOCK = 128
NEG = -1e30
RMS_EPS = 1e-6
LN_EPS = 1e-5

MLA_HEADS = 8
MLA_NOPE = 64
MLA_ROPE = 32
Q_LORA = 384
KV_LORA = 256
DIL_HEAD_DIM = 64
DIL_ROT = 16
DIL_DILATIONS = (1, 4, 16)
IN_WIDTH_PAD = 3328
ONES_LANE = (64, 0)
MLA_SCALE = (MLA_NOPE + MLA_ROPE) ** -0.5
DIL_SCALE = DIL_HEAD_DIM ** -0.5
ALPHA = 2.0 ** 0.25

ADAM_LR = 0.001
ADAM_B1 = 0.9
ADAM_B2 = 0.999
ADAM_EPS = 1e-08
ADAM_WD = 0.01
ADAM_STEP = 10

N_SHARD = 4
SHARD_SHAPES = ((808, 1024), (384, 192), (256, 256), (256, 1024))
SHARD_SPLIT_COLS = (True, False, False, False)
ROW_CHUNK = 64
LANES = 128
VMEM_LIMIT = 56 * 1024 * 1024
MESH = pl.DeviceIdType.MESH

NT = (((1,), (1,)), ((), ()))
TN = (((0,), (0,)), ((), ()))


def _cp(sem=None, vmem=None):
    return pltpu.CompilerParams(dimension_semantics=sem, vmem_limit_bytes=vmem)


def _dot(a, b, dims=None):
    if dims is None:
        return jnp.dot(a, b, preferred_element_type=F32)
    return lax.dot_general(a, b, dims, preferred_element_type=F32)


def _rope_tables(seq):
    f32 = np.float32
    pos = np.arange(seq, dtype=f32)[:, None]
    one, zero = np.ones((seq, 64), f32), np.zeros((seq, 64), f32)

    def cos_sin(dim):
        inv = np.power(f32(ROPE_THETA), -np.arange(0, dim, 2, dtype=f32) / f32(dim)).astype(f32)
        ang = (pos * inv[None, :]).astype(f32)
        return np.cos(ang).astype(f32), np.sin(ang).astype(f32)

    cos, sin = cos_sin(MLA_ROPE)
    ct = np.concatenate([one, cos, cos, zero[:, :32]], axis=1)
    st = np.concatenate([zero, -sin, sin, zero[:, :32]], axis=1)
    cos, sin = cos_sin(DIL_ROT)
    cd = np.concatenate([cos, cos, one[:, :48]], axis=1)
    sd = np.concatenate([-sin, sin, zero[:, :48]], axis=1)
    return tuple(jnp.asarray(t) for t in (ct, st, np.tile(cd, (1, 2)), np.tile(sd, (1, 2))))


W_IN_ORDER = ((0, 640), (672, 1184), (2720, 3232), (1184, 2720), None, (640, 672))


def _w_in_row_pieces():
    width = SHARD_SHAPES[0][0]
    pieces, at = [], 0
    for r in W_IN_ORDER:
        if r is None:
            at += 64
            continue
        for k in range(N_SHARD):
            lo, hi = max(r[0], width * k), min(r[1], width * (k + 1))
            if lo < hi:
                pieces.append((k, lo - width * k, at + lo - r[0], hi - lo))
        at += r[1] - r[0]
    return pieces


def _position():
    return lax.axis_index("x"), lax.axis_index("y"), lax.axis_index("c")


def _all_gather_weights(shards):
    n = len(shards)

    def body(*refs):
        ins, outs = refs[:n], list(refs[n:2 * n])
        w_in_p, outs[0] = outs[0], refs[2 * n]
        send_sems, recv_sems = refs[2 * n + 1:]
        x, y, c = _position()
        me = 2 * x + y
        chips = [(1 - x, y), (x, 1 - y), (1 - x, 1 - y)]
        for a in range(n):
            for blk in _shard_blocks(a):
                outs[a][(me,) + blk] = ins[a][blk].astype(BF16)

        def copy(k, a, slot, part, to):
            ref = outs[a].at[(slot,) + part]
            return pltpu.make_async_remote_copy(
                src_ref=ref, dst_ref=ref, send_sem=send_sems.at[k * n + a], recv_sem=recv_sems.at[k * n + a],
                device_id=to, device_id_type=MESH)

        half = [_shard_half(a, c) for a in range(n)]
        other = [_shard_half(a, 1 - c) for a in range(n)]
        first = [copy(k, a, me, half[a], (px, py, c)) for k, (px, py) in enumerate(chips) for a in range(n)]
        for cp in first:
            cp.start()
        passed = []
        for k, (px, py) in enumerate(chips):
            for a in range(n):
                copy(k, a, 2 * px + py, half[a], (x, y, c)).wait_recv()
                cp = copy(3 + k, a, 2 * px + py, half[a], (x, y, 1 - c))
                cp.start()
                passed.append(cp)
        for k, (px, py) in enumerate(chips):
            for a in range(n):
                copy(3 + k, a, 2 * px + py, other[a], (x, y, c)).wait_recv()
        for cp in first + passed:
            cp.wait_send()

        written = []
        for k, r0, at, rows in _w_in_row_pieces():
            written.append((at, at + rows))
            for r in range(0, rows, 2 * LANES):
                m = min(2 * LANES, rows - r)
                for c0 in range(0, D_MODEL, LANES):
                    w_in_p[at + r:at + r + m, c0:c0 + LANES] = outs[0][k, r0 + r:r0 + r + m, c0:c0 + LANES]
        for lo, hi in zip([0] + [w[1] for w in sorted(written)], [w[0] for w in sorted(written)] + [IN_WIDTH_PAD]):
            if lo < hi:
                w_in_p[lo:hi, :] = jnp.zeros((hi - lo, D_MODEL), BF16)

    vmem = pl.BlockSpec(memory_space=pltpu.VMEM)
    return pl.pallas_call(
        body, name="all_gather_weights",
        out_shape=[jax.ShapeDtypeStruct((IN_WIDTH_PAD, D_MODEL), BF16)]
        + [jax.ShapeDtypeStruct((N_SHARD,) + s, BF16) for s in SHARD_SHAPES[1:n]],
        in_specs=[vmem] * n, out_specs=[vmem] * n,
        scratch_shapes=[pltpu.VMEM((N_SHARD,) + SHARD_SHAPES[0], BF16),
                        pltpu.SemaphoreType.DMA((6 * n,)), pltpu.SemaphoreType.DMA((6 * n,))],
        compiler_params=_cp(None, VMEM_LIMIT),
    )(*shards)


def _shard_blocks(a):
    rows, cols = SHARD_SHAPES[a]
    if SHARD_SPLIT_COLS[a]:
        return [(slice(None), slice(c0, c0 + LANES)) for c0 in range(0, cols, LANES)]
    return [(slice(r0, r0 + ROW_CHUNK), slice(None)) for r0 in range(0, rows, ROW_CHUNK)]


def _shard_half_shape(a):
    rows, cols = SHARD_SHAPES[a]
    return (rows, cols // 2) if SHARD_SPLIT_COLS[a] else (rows // 2, cols)


def _shard_half(a, c):
    rows, cols = SHARD_SHAPES[a]
    if SHARD_SPLIT_COLS[a]:
        return slice(None), pl.ds(pl.multiple_of(c * (cols // 2), LANES), cols // 2)
    return pl.ds(pl.multiple_of(c * (rows // 2), ROW_CHUNK), rows // 2), slice(None)


def _shard_chunks(a, c):
    rows, cols = SHARD_SHAPES[a]
    if SHARD_SPLIT_COLS[a]:
        return [((slice(None), pl.ds(c0, LANES)),
                 (slice(None), pl.ds(pl.multiple_of(c * (cols // 2) + c0, LANES), LANES)))
                for c0 in range(0, cols // 2, LANES)]
    return [((pl.ds(r0, ROW_CHUNK), slice(None)),
             (pl.ds(pl.multiple_of(c * (rows // 2) + r0, ROW_CHUNK), ROW_CHUNK), slice(None)))
            for r0 in range(0, rows // 2, ROW_CHUNK)]


def _reduce_gradients(grads, small_rows):
    n = len(grads)
    n_small = len(small_rows)
    pieces = _w_in_row_pieces()
    order = sorted(range(n), key=lambda a: SHARD_SHAPES[a][0] * SHARD_SHAPES[a][1])

    def body(*refs):
        g_hbm, rows_in = refs[:n], refs[n:n + n_small]
        outs, small_sum = refs[n + n_small:2 * n + n_small], refs[2 * n + n_small]
        scratch = refs[2 * n + n_small + 1:]
        stage, got, sums, others = (scratch[i * n:(i + 1) * n] for i in range(4))
        sm, smalls, send_sems, recv_sems, local_sems = scratch[4 * n:]
        swap_sem, chip_sem, join_sem, small_sem = 0, n, 4 * n, 5 * n
        x, y, c = _position()
        me = 4 * x + 2 * y + c
        chips = [(1 - x, y), (x, 1 - y), (1 - x, 1 - y)]
        sm[...] = jnp.zeros_like(sm)
        for i, row in enumerate(rows_in):
            sm[i:i + 1, 0:row.shape[1]] = row[...]
        loads = [[pltpu.make_async_copy(g_hbm[0].at[pl.ds(src, rows)], stage[0].at[k, pl.ds(dst, rows)],
                                        local_sems.at[n + i])
                  for i, (k, dst, src, rows) in enumerate(pieces)]]
        loads += [[pltpu.make_async_copy(g_hbm[a], stage[a], local_sems.at[a])] for a in range(1, n)]
        for a in order:
            for ld in loads[a]:
                ld.start()
        smalls[me] = sm[...]
        small_sends = []
        for rel in range(1, 8):
            px = 1 - x if rel // 4 else x
            py = 1 - y if (rel // 2) % 2 else y
            pc = 1 - c if rel % 2 else c
            cp = pltpu.make_async_remote_copy(
                src_ref=sm, dst_ref=smalls.at[me], send_sem=send_sems.at[small_sem + rel],
                recv_sem=recv_sems.at[small_sem + rel], device_id=(px, py, pc), device_id_type=MESH)
            cp.start()
            small_sends.append((cp, 4 * px + 2 * py + pc))
        swaps = {}
        for a in order:
            for ld in loads[a]:
                ld.wait()
            swaps[a] = pltpu.make_async_remote_copy(
                src_ref=stage[a].at[(slice(None),) + _shard_half(a, 1 - c)], dst_ref=got[a],
                send_sem=send_sems.at[swap_sem + a], recv_sem=recv_sems.at[swap_sem + a],
                device_id=(x, y, 1 - c), device_id_type=MESH)
            swaps[a].start()
        sends = {}
        for a in order:
            swaps[a].wait_recv()
            for k in range(N_SHARD):
                for in_half, in_whole in _shard_chunks(a, c):
                    pair = stage[a][(k,) + in_whole] + got[a][(k,) + in_half]
                    sums[a][(k,) + in_half] = pair.astype(BF16)
            sends[a] = [pltpu.make_async_remote_copy(
                src_ref=sums[a].at[2 * px + py], dst_ref=others[a].at[k], send_sem=send_sems.at[chip_sem + k * n + a],
                recv_sem=recv_sems.at[chip_sem + k * n + a], device_id=(px, py, c), device_id_type=MESH)
                for k, (px, py) in enumerate(chips)]
            for cp in sends[a]:
                cp.start()
        joins = []
        for a in order:
            for cp in sends[a]:
                cp.wait_recv()
            for in_half, in_whole in _shard_chunks(a, c):
                total = sums[a][(2 * x + y,) + in_half].astype(F32)
                for k in range(3):
                    total = total + others[a][(k,) + in_half].astype(F32)
                outs[a][in_whole] = total
            half = outs[a].at[_shard_half(a, c)]
            cp = pltpu.make_async_remote_copy(
                src_ref=half, dst_ref=half, send_sem=send_sems.at[join_sem + a],
                recv_sem=recv_sems.at[join_sem + a], device_id=(x, y, 1 - c), device_id_type=MESH)
            cp.start()
            joins.append(cp)
        for rel, (cp, peer) in enumerate(small_sends, start=1):
            pltpu.make_async_remote_copy(
                src_ref=sm, dst_ref=smalls.at[peer], send_sem=send_sems.at[small_sem + rel],
                recv_sem=recv_sems.at[small_sem + rel], device_id=(x, y, c), device_id_type=MESH).wait_recv()
        total = smalls[0]
        for dev in range(1, 8):
            total = total + smalls[dev]
        small_sum[...] = total
        for a in order:
            other = outs[a].at[_shard_half(a, 1 - c)]
            pltpu.make_async_remote_copy(
                src_ref=other, dst_ref=other, send_sem=send_sems.at[join_sem + a],
                recv_sem=recv_sems.at[join_sem + a], device_id=(x, y, c), device_id_type=MESH).wait_recv()
        for cp in list(swaps.values()) + [cp for a in order for cp in sends[a]] + joins + [cp for cp, _ in small_sends]:
            cp.wait_send()

    vmem = pl.BlockSpec(memory_space=pltpu.VMEM)
    halves = [_shard_half_shape(a) for a in range(n)]
    return pl.pallas_call(
        body, name="reduce_gradients",
        out_shape=[jax.ShapeDtypeStruct(s, F32) for s in SHARD_SHAPES] + [jax.ShapeDtypeStruct((8, D_MODEL), F32)],
        in_specs=[pl.BlockSpec(memory_space=pl.ANY)] * n + [vmem] * n_small, out_specs=[vmem] * (n + 1),
        scratch_shapes=[pltpu.VMEM((N_SHARD,) + s, F32) for s in SHARD_SHAPES]
        + [pltpu.VMEM((N_SHARD,) + s, F32) for s in halves] + [pltpu.VMEM((N_SHARD,) + s, BF16) for s in halves]
        + [pltpu.VMEM((3,) + s, BF16) for s in halves]
        + [pltpu.VMEM((8, D_MODEL), F32), pltpu.VMEM((8, 8, D_MODEL), F32),
           pltpu.SemaphoreType.DMA((5 * n + 8,)), pltpu.SemaphoreType.DMA((5 * n + 8,)),
           pltpu.SemaphoreType.DMA((n + len(pieces),))],
        compiler_params=_cp(None, VMEM_LIMIT),
    )(*grads, *small_rows)


def _proj(x, w_in_p, gq, gkv, wuq_e, wukv, ct, st, cd, sd, w_out):
    seq = x.shape[0]
    tr = 512
    a_out = 3

    def gather_w_out(w_out_ref, all_ref, land, send_sems, recv_sems):
        x_, y_, c = _position()
        me = 2 * x_ + y_
        chips = [(1 - x_, y_), (x_, 1 - y_), (1 - x_, 1 - y_)]
        half, other = _shard_half(a_out, c), _shard_half(a_out, 1 - c)

        def copy(k, slot, part, to):
            ref = land.at[(slot,) + part]
            return pltpu.make_async_remote_copy(
                src_ref=ref, dst_ref=ref, send_sem=send_sems.at[k], recv_sem=recv_sems.at[k],
                device_id=to, device_id_type=MESH)

        first = [copy(k, me, half, (px, py, c)) for k, (px, py) in enumerate(chips)]

        @pl.when(pl.program_id(0) == 0)
        def _():
            for blk in _shard_blocks(a_out):
                land[(me,) + blk] = w_out_ref[blk].astype(BF16)
            for cp in first:
                cp.start()

        @pl.when(pl.program_id(0) == pl.num_programs(0) - 1)
        def _():
            passed = []
            for k, (px, py) in enumerate(chips):
                copy(k, 2 * px + py, half, (x_, y_, c)).wait_recv()
                passed.append(copy(3 + k, 2 * px + py, half, (x_, y_, 1 - c)))
                passed[-1].start()
            for k, (px, py) in enumerate(chips):
                copy(3 + k, 2 * px + py, other, (x_, y_, c)).wait_recv()
            for cp in first + passed:
                cp.wait_send()
            all_ref[...] = land[...]

    def body(x_ref, w_ref, gq_ref, gkv_ref, wuq_ref, wukv_ref, ct_ref, st_ref, cd_ref, sd_ref, w_out_ref,
             cq_ref, ckv_ref, g_ref, qr_ref, kr_ref, vb_ref, q_out, k_out, v_out, w_out_all, land, send_sems, recv_sems):
        gather_w_out(w_out_ref, w_out_all, land, send_sems, recv_sems)
        lane = lax.broadcasted_iota(jnp.int32, (tr, LANES), 1)
        xb = x_ref[...].astype(BF16)
        cq = _dot(xb, w_ref[0:384, :], NT)
        ckv = _dot(xb, w_ref[384:640, :], NT)
        cq_ref[...] = cq
        ckv_ref[...] = ckv
        g_ref[...] = _dot(xb, w_ref[640:1664, :], NT)

        cd_, sd_ = cd_ref[...], sd_ref[...]
        qb = _dot(xb, w_ref[1664:2176, :], NT)
        kb = _dot(xb, w_ref[2176:2688, :], NT)
        for p in range(4):
            cols = slice(LANES * p, LANES * (p + 1))
            t = qb[:, cols]
            qr_ref[:, cols] = (t * cd_ + _dil_rot(t, lane) * sd_) * DIL_SCALE
            t = kb[:, cols]
            kr_ref[:, cols] = t * cd_ + _dil_rot(t, lane) * sd_
        vb_ref[...] = _dot(xb, w_ref[2688:3200, :], NT)

        ct_, st_ = ct_ref[...], st_ref[...]

        def rope(t):
            return t * ct_ + _mla_rot(t, lane) * st_

        _, qn = _rms(cq, gq_ref[...])
        q_all = _dot(qn.astype(BF16), wuq_ref[...])
        for h in range(MLA_HEADS):
            q_out[h] = (rope(q_all[:, LANES * h:LANES * (h + 1)]) * MLA_SCALE).astype(BF16)
        _, kvn = _rms(ckv, gkv_ref[...])
        kv_all = _dot(kvn.astype(BF16), wukv_ref[...])
        kpe = rope(_dot(xb, w_ref[3200:3328, :], NT))
        for h in range(MLA_HEADS):
            kv_h = kv_all[:, LANES * h:LANES * (h + 1)]
            k_out[h] = jnp.where(lane < 64, kv_h, kpe).astype(BF16)
            if h % 2:
                v = jnp.where(lane >= 64, kv_h, 0.0)
            else:
                v = jnp.where(lane < 64, pltpu.roll(kv_h, 64, 1), 0.0)
            v_out[h] = jnp.where(lane == ONES_LANE[h % 2], 1.0, v).astype(BF16)

    row = lambda w: pl.BlockSpec((tr, w), lambda i: (i, 0))
    full = lambda a: pl.BlockSpec(a.shape, lambda i: (0,) * a.ndim)
    head = pl.BlockSpec((MLA_HEADS, tr, LANES), lambda i: (0, i, 0))
    widths = (Q_LORA, KV_LORA, D_MODEL, 512, 512, 512)
    gathered = (N_SHARD,) + SHARD_SHAPES[a_out]
    return pl.pallas_call(
        body, name="proj", grid=(seq // tr,),
        in_specs=[row(D_MODEL), full(w_in_p), full(gq), full(gkv), full(wuq_e), full(wukv)] + [row(LANES)] * 4
        + [full(w_out)],
        out_specs=[row(w) for w in widths] + [head] * 3 + [pl.BlockSpec(gathered, lambda i: (0, 0, 0))],
        out_shape=[jax.ShapeDtypeStruct((seq, w), F32) for w in widths]
        + [jax.ShapeDtypeStruct((MLA_HEADS, seq, LANES), BF16)] * 3 + [jax.ShapeDtypeStruct(gathered, BF16)],
        scratch_shapes=[pltpu.VMEM(gathered, BF16), pltpu.SemaphoreType.DMA((6,)), pltpu.SemaphoreType.DMA((6,))],
        compiler_params=_cp(("arbitrary",), VMEM_LIMIT),
    )(x, w_in_p, gq, gkv, wuq_e, wukv, ct, st, cd, sd, w_out)


def _mla_rot(t, lane):
    return jnp.where(lane < 80, pltpu.roll(t, 112, 1), pltpu.roll(t, 16, 1))


def _dil_rot(t, lane):
    return jnp.where(lane % 64 < 8, pltpu.roll(t, 120, 1), pltpu.roll(t, 8, 1))


def _rms(c, g):
    r = lax.rsqrt(jnp.mean(c * c, axis=-1, keepdims=True) + RMS_EPS)
    return r, c * r * g


def _mla_fwd(q, k, v):
    seq = q.shape[1]
    tq = 512
    nq = seq // tq

    def body(q_ref, k_ref, v_ref, o_ref, lse_ref, m_s, acc_s, s_buf):
        i = pl.program_id(1)
        row = lax.broadcasted_iota(jnp.int32, (tq, tq), 0)
        col = lax.broadcasted_iota(jnp.int32, (tq, tq), 1)
        lane = lax.broadcasted_iota(jnp.int32, (tq, LANES), 1)
        m_s[...] = jnp.full((2, tq, LANES), NEG, F32)
        acc_s[...] = jnp.zeros((2, tq, LANES), F32)

        def block(j):
            return pl.ds(pl.multiple_of(j * tq, tq), tq)

        def scores(hh, j):
            return _dot(q_ref[hh], k_ref[hh, block(j), :], NT)

        def consume(hh, j, s):
            m_prev = m_s[hh]
            m_new = jnp.maximum(m_prev, jnp.max(s, axis=1, keepdims=True))
            p = jnp.exp(s - m_new[:, :1])
            acc_s[hh] = jnp.exp(m_prev - m_new) * acc_s[hh] + _dot(p.astype(BF16), v_ref[hh, block(j), :])
            m_s[hh] = m_new

        for hh in range(2):
            s_buf[0, hh] = scores(hh, 0)

        def full_step(j, carry):
            slot = j & 1
            for hh in range(2):
                s = s_buf[slot, hh]
                s_buf[1 - slot, hh] = scores(hh, j + 1)
                consume(hh, j, s)
            return carry

        lax.fori_loop(0, i, full_step, 0)
        total = jnp.zeros((tq, LANES), F32)
        for hh in range(2):
            consume(hh, i, jnp.where(col <= row, s_buf[i & 1, hh], NEG))
            acc = acc_s[hh]
            l = acc[:, ONES_LANE[hh]:ONES_LANE[hh] + 1]
            mine = (lane >= 64) if hh else (lane < 64)
            total = total + jnp.where(mine, acc / l, 0.0)
            lse_ref[hh] = m_s[hh] + jnp.log(l)
        o_ref[...] = total

    kv_spec = pl.BlockSpec((2, seq, LANES), lambda p, i: (p, 0, 0))
    return pl.pallas_call(
        body, name="mla_fwd", grid=(MLA_HEADS // 2, nq),
        in_specs=[pl.BlockSpec((2, tq, LANES), lambda p, i: (p, i, 0)), kv_spec, kv_spec],
        out_specs=[pl.BlockSpec((tq, LANES), lambda p, i: (i, p)), pl.BlockSpec((2, tq, LANES), lambda p, i: (p, i, 0))],
        out_shape=[jax.ShapeDtypeStruct((seq, 4 * LANES), F32), jax.ShapeDtypeStruct((MLA_HEADS, seq, LANES), F32)],
        scratch_shapes=[pltpu.VMEM((2, tq, LANES), F32), pltpu.VMEM((2, tq, LANES), F32),
                        pltpu.VMEM((2, 2, tq, tq), F32)],
        compiler_params=_cp(("arbitrary", "arbitrary"), VMEM_LIMIT),
    )(q, k, v)


DIL_Q_FWD = 2 * BLOCK
DIL_Q_BWD = BLOCK


def _dil_tile_index(t, d, seq, nq):
    per_class = seq // (nq * d)
    shift = per_class.bit_length() - 1
    r = t >> shift
    n = t & (per_class - 1)
    start = r + (nq * d) * n
    prev = jnp.maximum(start - BLOCK * d, r)
    if d == 1:
        start = pl.multiple_of(start, nq)
        prev = pl.multiple_of(prev, BLOCK)
    return (n == 0).astype(jnp.int32), start, prev


def _dil_rows(start, d, size):
    return pl.ds(start, size) if d == 1 else pl.ds(start, size, stride=d)


def _dil_bias(nq):
    i = lax.broadcasted_iota(jnp.int32, (2 * nq, BLOCK + nq), 0) % nq
    j = lax.broadcasted_iota(jnp.int32, (2 * nq, BLOCK + nq), 1)
    band = (j >= i) & (j <= i + BLOCK)
    return jnp.where(band, 0.0, NEG), jnp.where(band & (j >= BLOCK), 0.0, NEG)


def _stack_heads(t, lane):
    return jnp.concatenate([jnp.where(lane < 64, t, 0.0), jnp.where(lane >= 64, t, 0.0)], axis=0)


def _unstack_heads(t, lane):
    nq = t.shape[0] // 2
    return jnp.where(lane < 64, t[:nq], t[nq:])


def _dil_fwd(qr, kr, vb):
    seq = qr.shape[0]
    nq = DIL_Q_FWD
    n_tiles = seq // nq
    assert seq % (nq * max(DIL_DILATIONS)) == 0

    def body(q_ref, k_ref, v_ref, o_ref, lse_ref, m_s, l_s, n_s, bias_s):
        lane = lax.broadcasted_iota(jnp.int32, (nq, LANES), 1)
        bias_s[0], bias_s[1] = _dil_bias(nq)
        for bi, d in enumerate(DIL_DILATIONS):

            def tile(t, carry, d=d, bi=bi):
                first, start, prev = _dil_tile_index(t, d, seq, nq)
                rows, prows = _dil_rows(start, d, nq), _dil_rows(prev, d, BLOCK)
                qst = _stack_heads(q_ref[rows, :], lane).astype(BF16)
                if seq == nq * d:
                    kcat, vcat = k_ref[rows, :].astype(BF16), v_ref[rows, :].astype(BF16)
                    s = _dot(qst, kcat, NT) + bias_s[1, :, BLOCK:]
                else:
                    kcat = jnp.concatenate([k_ref[prows, :], k_ref[rows, :]], axis=0).astype(BF16)
                    vcat = jnp.concatenate([v_ref[prows, :], v_ref[rows, :]], axis=0).astype(BF16)
                    s = _dot(qst, kcat, NT) + bias_s[first]
                m = jnp.max(s, axis=1, keepdims=True)
                p = jnp.exp(s - m)
                l2 = _unstack_heads(jnp.sum(p, axis=1, keepdims=True) + jnp.zeros((2 * nq, LANES), F32), lane)
                m2 = _unstack_heads(m + jnp.zeros((2 * nq, LANES), F32), lane)
                num2 = _unstack_heads(_dot(p.astype(BF16), vcat), lane)
                if bi == 0:
                    m_s[rows, :] = m2
                    l_s[rows, :] = l2
                    n_s[rows, :] = num2
                else:
                    m_old = m_s[rows, :]
                    m_new = jnp.maximum(m_old, m2)
                    a = jnp.exp(m_old - m_new)
                    b = jnp.exp(m2 - m_new)
                    m_s[rows, :] = m_new
                    l_s[rows, :] = a * l_s[rows, :] + b * l2
                    n_s[rows, :] = a * n_s[rows, :] + b * num2
                return carry

            lax.fori_loop(0, n_tiles, tile, 0, unroll=8)
        o_ref[...] = n_s[...] / l_s[...]
        lse_ref[...] = m_s[...] + jnp.log(l_s[...])

    col = lambda off: pl.BlockSpec((seq, LANES), lambda p: (0, p + off))
    return pl.pallas_call(
        body, name="dil_fwd", grid=(4,),
        in_specs=[col(0), col(0), col(0)],
        out_specs=[col(0), pl.BlockSpec((None, seq, LANES), lambda p: (p, 0, 0))],
        out_shape=[jax.ShapeDtypeStruct((seq, 4 * LANES), F32), jax.ShapeDtypeStruct((4, seq, LANES), F32)],
        scratch_shapes=[pltpu.VMEM((seq, LANES), F32)] * 3 + [pltpu.VMEM((2, 2 * nq, BLOCK + nq), F32)],
        compiler_params=_cp(("arbitrary",), VMEM_LIMIT),
    )(qr, kr, vb)


def _post(x, o_a, o_b, gates, w_out, ln_g, ln_b, target):
    seq = x.shape[0]
    tr = 512

    def body(x_ref, oa_ref, ob_ref, g_ref, w_ref, lg_ref, lb_ref, t_ref,
             dz_ref, do_ref, dg_ref, dw_ref, dlg_ref, dlb_ref, loss_ref):
        @pl.when(pl.program_id(0) == 0)
        def _():
            dw_ref[...] = jnp.zeros_like(dw_ref)
            dlg_ref[...] = jnp.zeros_like(dlg_ref)
            dlb_ref[...] = jnp.zeros_like(dlb_ref)
            loss_ref[...] = jnp.zeros_like(loss_ref)

        g = g_ref[...]
        sg = jax.nn.sigmoid(g)
        silu = g * sg
        o = jnp.concatenate([oa_ref[...], ob_ref[...]], axis=1)
        mixb = (o * silu).astype(BF16)
        w = w_ref[...]
        z = ALPHA * x_ref[...] + _dot(mixb, w)
        mu = jnp.mean(z, axis=-1, keepdims=True)
        zc = z - mu
        rstd = lax.rsqrt(jnp.mean(zc * zc, axis=-1, keepdims=True) + LN_EPS)
        xhat = zc * rstd
        lg = lg_ref[...]
        err = xhat * lg + lb_ref[...] - t_ref[...]
        loss_ref[...] += jnp.sum(err * err) * (0.5 / D_MODEL)
        dy = err * (1.0 / D_MODEL)
        dlg_ref[...] += jnp.sum(dy * xhat, axis=0, keepdims=True)
        dlb_ref[...] += jnp.sum(dy, axis=0, keepdims=True)
        dxh = dy * lg
        dz = rstd * (dxh - jnp.mean(dxh, axis=-1, keepdims=True) - xhat * jnp.mean(dxh * xhat, axis=-1, keepdims=True))
        dz_ref[...] = dz
        dzb = dz.astype(BF16)
        dmix = _dot(dzb, w, NT)
        do_ref[...] = dmix * silu
        dg_ref[...] = (dmix * o * (sg * (1.0 + g * (1.0 - sg)))).astype(BF16)
        dw_ref[...] += _dot(mixb, dzb, TN)

    row = lambda w: pl.BlockSpec((tr, w), lambda i: (i, 0))
    full = lambda s: pl.BlockSpec(s, lambda i: (0, 0))
    return pl.pallas_call(
        body, name="post", grid=(seq // tr,),
        in_specs=[row(D_MODEL), row(512), row(512), row(D_MODEL), full((D_MODEL, D_MODEL)), full((1, D_MODEL)),
                  full((1, D_MODEL)), row(D_MODEL)],
        out_specs=[row(D_MODEL), row(D_MODEL), row(D_MODEL), full((D_MODEL, D_MODEL)), full((1, D_MODEL)),
                   full((1, D_MODEL)), full((1, LANES))],
        out_shape=[jax.ShapeDtypeStruct((seq, D_MODEL), F32), jax.ShapeDtypeStruct((seq, D_MODEL), F32),
                   jax.ShapeDtypeStruct((seq, D_MODEL), BF16), jax.ShapeDtypeStruct((D_MODEL, D_MODEL), F32),
                   jax.ShapeDtypeStruct((1, D_MODEL), F32), jax.ShapeDtypeStruct((1, D_MODEL), F32),
                   jax.ShapeDtypeStruct((1, LANES), F32)],
        compiler_params=_cp(("arbitrary",), VMEM_LIMIT),
    )(x, o_a, o_b, gates, w_out, ln_g, ln_b, target)


def _mla_bwd(q, k, v, d_o, o, lse):
    seq = q.shape[1]
    tq = 512
    nq = seq // tq

    def body(q_ref, k_ref, v_ref, do_ref, o_ref, lse_ref, dq_ref, dk_ref, dv_ref, d_s, lse_s, dk_s, dv_s, v_s, kt_s, dqt_s):
        j = pl.program_id(1)
        lane = lax.broadcasted_iota(jnp.int32, (tq, LANES), 1)
        row = lax.broadcasted_iota(jnp.int32, (tq, tq), 0)
        col = lax.broadcasted_iota(jnp.int32, (tq, tq), 1)

        @pl.when(j == 0)
        def _():
            dqt_s[...] = jnp.zeros_like(dqt_s)

            def rowsum(i, carry):
                rows = pl.ds(pl.multiple_of(i * tq, tq), tq)
                prod = do_ref[rows, :] * o_ref[rows, :]
                for hh in range(2):
                    mine = (lane >= 64) if hh else (lane < 64)
                    total = jnp.sum(jnp.where(mine, prod, 0.0), axis=1, keepdims=True)
                    d_s[hh, i] = jnp.transpose(total + jnp.zeros((tq, LANES), F32))[:8]
                    lse_s[hh, i] = jnp.transpose(lse_ref[hh, rows, :])[:8]
                return carry

            lax.fori_loop(0, nq, rowsum, 0)

        dk_s[...] = jnp.zeros_like(dk_s)
        dv_s[...] = jnp.zeros_like(dv_s)
        for hh in range(2):
            v_s[hh] = jnp.where(lane == ONES_LANE[hh], 0.0, v_ref[hh].astype(F32)).astype(BF16)
            kt_s[hh] = jnp.transpose(k_ref[hh].astype(F32)).astype(BF16)

        def step(i, masked):
            rows = pl.ds(pl.multiple_of(i * tq, tq), tq)
            dob = do_ref[rows, :].astype(BF16)
            for hh in range(2):
                qb, kb, vb = q_ref[hh, rows, :], k_ref[hh], v_s[hh]
                p = jnp.exp(_dot(kb, qb, NT) - lse_s[hh, i][:1])
                if masked:
                    p = jnp.where(row <= col, p, 0.0)
                dv_s[hh] += _dot(p.astype(BF16), dob)
                ds = (p * (_dot(vb, dob, NT) - d_s[hh, i][:1])).astype(BF16)
                dk_s[hh] += _dot(ds, qb)
                dqt_s[hh, i] += _dot(kt_s[hh], ds)

        def full_step(i, carry):
            step(i, False)
            return carry

        step(j, True)
        lax.fori_loop(j + 1, nq, full_step, 0)
        dk_ref[...] = dk_s[...]
        dv_ref[...] = dv_s[...]

        @pl.when(j == nq - 1)
        def _():
            def untranspose(i, carry):
                rows = pl.ds(pl.multiple_of(i * tq, tq), tq)
                for hh in range(2):
                    dq_ref[hh, rows, :] = jnp.transpose(dqt_s[hh, i])
                return carry

            lax.fori_loop(0, nq, untranspose, 0)

    whole = pl.BlockSpec((2, seq, LANES), lambda p, j: (p, 0, 0))
    blk = pl.BlockSpec((2, tq, LANES), lambda p, j: (p, j, 0))
    pair = pl.BlockSpec((seq, LANES), lambda p, j: (0, p))
    shape = jax.ShapeDtypeStruct((MLA_HEADS, seq, LANES), F32)
    return pl.pallas_call(
        body, name="mla_bwd", grid=(MLA_HEADS // 2, nq),
        in_specs=[whole, blk, blk, pair, pair, whole],
        out_specs=[whole, blk, blk], out_shape=[shape] * 3,
        scratch_shapes=[pltpu.VMEM((2, nq, 8, tq), F32), pltpu.VMEM((2, nq, 8, tq), F32),
                        pltpu.VMEM((2, tq, LANES), F32), pltpu.VMEM((2, tq, LANES), F32),
                        pltpu.VMEM((2, tq, LANES), BF16), pltpu.VMEM((2, LANES, tq), BF16),
                        pltpu.VMEM((2, nq, LANES, tq), F32)],
        compiler_params=_cp(("arbitrary", "arbitrary"), VMEM_LIMIT),
    )(q, k, v, d_o, o, lse)


def _dil_bwd(qr, kr, vb, d_o, o, lse):
    seq = qr.shape[0]
    nq = DIL_Q_BWD
    n_tiles = seq // nq
    chunk = 512

    def body(q_ref, k_ref, v_ref, do_ref, o_ref, lse_ref, dq_ref, dk_ref, dv_ref, d_s, dq_s, dk_s, dv_s, bias_s):
        lane = lax.broadcasted_iota(jnp.int32, (nq, LANES), 1)
        lanec = lax.broadcasted_iota(jnp.int32, (chunk, LANES), 1)
        bias_s[0], bias_s[1] = [b[:nq] for b in _dil_bias(nq)]

        def rowsum(i, carry):
            rows = pl.ds(pl.multiple_of(i * chunk, chunk), chunk)
            prod = do_ref[rows, :] * o_ref[rows, :]
            lo = jnp.sum(jnp.where(lanec < 64, prod, 0.0), axis=1, keepdims=True)
            hi = jnp.sum(jnp.where(lanec >= 64, prod, 0.0), axis=1, keepdims=True)
            d_s[rows, :] = jnp.where(lanec < 64, lo, hi)
            return carry

        lax.fori_loop(0, seq // chunk, rowsum, 0)
        dq_s[...] = jnp.zeros_like(dq_s)
        dk_s[...] = jnp.zeros_like(dk_s)
        dv_s[...] = jnp.zeros_like(dv_s)
        for d in DIL_DILATIONS:

            def tile(start, prev, first, d=d):
                rows = _dil_rows(start, d, nq)
                q_t, do_t = q_ref[rows, :], do_ref[rows, :]
                lse_t, d_t = lse_ref[rows, :], d_s[rows, :]
                if prev is None:
                    kcat, vcat = k_ref[rows, :].astype(BF16), v_ref[rows, :].astype(BF16)
                    bias = bias_s[1, :, BLOCK:]
                else:
                    prows = _dil_rows(prev, d, BLOCK)
                    kcat = jnp.concatenate([k_ref[prows, :], k_ref[rows, :]], axis=0).astype(BF16)
                    vcat = jnp.concatenate([v_ref[prows, :], v_ref[rows, :]], axis=0).astype(BF16)
                    bias = bias_s[first]
                dq_t = jnp.zeros((nq, LANES), F32)
                dkcat = jnp.zeros((kcat.shape[0], LANES), F32)
                dvcat = jnp.zeros((kcat.shape[0], LANES), F32)
                for hh in range(2):
                    mine = (lane >= 64) if hh else (lane < 64)
                    c0 = 64 * hh
                    qh = jnp.where(mine, q_t, 0.0).astype(BF16)
                    doh = jnp.where(mine, do_t, 0.0).astype(BF16)
                    p = jnp.exp(_dot(qh, kcat, NT) + bias - lse_t[:, c0:c0 + 1])
                    dvcat = dvcat + _dot(p.astype(BF16), doh, TN)
                    dp = _dot(doh, vcat, NT)
                    ds = (p * (dp - d_t[:, c0:c0 + 1])).astype(BF16)
                    dq_t = dq_t + jnp.where(mine, _dot(ds, kcat), 0.0)
                    dkcat = dkcat + _dot(ds, qh, TN)
                dq_s[rows, :] += dq_t
                if prev is not None:
                    dk_s[prows, :] += dkcat[:BLOCK]
                    dv_s[prows, :] += dvcat[:BLOCK]
                dk_s[rows, :] += dkcat[-nq:]
                dv_s[rows, :] += dvcat[-nq:]

            if seq == 2 * nq * d:

                def class_tiles(r, carry, d=d):
                    tile(r, None, 1)
                    tile(r + nq * d, r, 0)
                    return carry

                lax.fori_loop(0, d, class_tiles, 0, unroll=8)
            else:

                def any_tile(t, carry, d=d):
                    first, start, prev = _dil_tile_index(t, d, seq, nq)
                    tile(start, prev, first)
                    return carry

                lax.fori_loop(0, n_tiles, any_tile, 0, unroll=16)
        dq_ref[...] = dq_s[...].astype(BF16)
        dk_ref[...] = dk_s[...].astype(BF16)
        dv_ref[...] = dv_s[...].astype(BF16)

    col = lambda off: pl.BlockSpec((seq, LANES), lambda p: (0, p + off))
    shape = jax.ShapeDtypeStruct((seq, 4 * LANES), BF16)
    return pl.pallas_call(
        body, name="dil_bwd", grid=(4,),
        in_specs=[col(0), col(0), col(0), col(4), col(0), pl.BlockSpec((None, seq, LANES), lambda p: (p, 0, 0))],
        out_specs=[col(0)] * 3, out_shape=[shape] * 3,
        scratch_shapes=[pltpu.VMEM((seq, LANES), F32)] * 4 + [pltpu.VMEM((2, nq, BLOCK + nq), F32)],
        compiler_params=_cp(("arbitrary",), VMEM_LIMIT),
    )(qr, kr, vb, d_o, o, lse)


def _in_bwd(dz, cq, ckv, gq, gkv, wuq_e, wukv, ct, st, dq, dk, dv, dgates, dqr, dkr, dvb, cd, sd, w_in_p):
    seq = dz.shape[0]
    tr = 512

    def body(dz_ref, cq_ref, ckv_ref, gq_ref, gkv_ref, wuq_ref, wukv_ref, ct_ref, st_ref, dq_ref, dk_ref, dv_ref,
             dg_ref, dqr_ref, dkr_ref, dvb_ref, cd_ref, sd_ref, w_ref,
             gx_ref, dh_ref, dwuq_ref, dwukv_ref, dgq_ref, dgkv_ref):
        @pl.when(pl.program_id(0) == 0)
        def _():
            dwuq_ref[...] = jnp.zeros_like(dwuq_ref)
            dwukv_ref[...] = jnp.zeros_like(dwukv_ref)
            dgq_ref[...] = jnp.zeros_like(dgq_ref)
            dgkv_ref[...] = jnp.zeros_like(dgkv_ref)

        lane = lax.broadcasted_iota(jnp.int32, (tr, LANES), 1)
        rope_lanes = jnp.logical_and(lane >= 64, lane < 96)
        ct_, st_ = ct_ref[...], st_ref[...]

        def mla_rope_t(g):
            return ct_ * g + jnp.where(rope_lanes, _mla_rot(st_ * g, lane), 0.0)

        def norm_bwd(c, g, dn, dg_ref):
            r, _ = _rms(c, g)
            u = dn * g
            dg_ref[...] += jnp.sum(dn * c * r, axis=0, keepdims=True)
            return r * u - c * (r * r * r) * jnp.mean(u * c, axis=-1, keepdims=True)

        c, g = cq_ref[...], gq_ref[...]
        _, qn = _rms(c, g)
        dq_all = jnp.concatenate([mla_rope_t(dq_ref[h] * MLA_SCALE) for h in range(MLA_HEADS)], axis=1).astype(BF16)
        dwuq_ref[...] += _dot(qn.astype(BF16), dq_all, TN)
        dcq = norm_bwd(c, g, _dot(dq_all, wuq_ref[...], NT), dgq_ref).astype(BF16)

        c, g = ckv_ref[...], gkv_ref[...]
        _, kvn = _rms(c, g)
        dkpe = jnp.zeros((tr, LANES), F32)
        parts = []
        for h in range(MLA_HEADS):
            dk_h, dv_h = dk_ref[h], dv_ref[h]
            if h % 2 == 0:
                dv_h = pltpu.roll(dv_h, 64, 1)
            parts.append(jnp.where(lane < 64, dk_h, dv_h))
            dkpe = dkpe + jnp.where(rope_lanes, dk_h, 0.0)
        dkv_all = jnp.concatenate(parts, axis=1).astype(BF16)
        dwukv_ref[...] += _dot(kvn.astype(BF16), dkv_all, TN)
        dckv = norm_bwd(c, g, _dot(dkv_all, wukv_ref[...], NT), dgkv_ref).astype(BF16)
        dkrope = mla_rope_t(dkpe).astype(BF16)

        rot_lanes = lane % 64 < DIL_ROT
        cd_, sd_ = cd_ref[...], sd_ref[...]

        def dil_rope_t(g):
            return cd_ * g + jnp.where(rot_lanes, _dil_rot(sd_ * g, lane), 0.0)

        dqb = [dil_rope_t(dqr_ref[:, LANES * p:LANES * (p + 1)].astype(F32) * DIL_SCALE).astype(BF16) for p in range(4)]
        dkb = [dil_rope_t(dkr_ref[:, LANES * p:LANES * (p + 1)].astype(F32)).astype(BF16) for p in range(4)]
        dh = jnp.concatenate([dcq, dckv, dg_ref[...]] + dqb + dkb + [dvb_ref[...], dkrope], axis=1)
        dh_ref[...] = dh
        gx_ref[...] = ALPHA * dz_ref[...] + _dot(dh, w_ref[...])

    row = lambda w: pl.BlockSpec((tr, w), lambda i: (i, 0))
    full = lambda a: pl.BlockSpec(a.shape, lambda i: (0,) * a.ndim)
    head = pl.BlockSpec((MLA_HEADS, tr, LANES), lambda i: (0, i, 0))
    return pl.pallas_call(
        body, name="in_bwd", grid=(seq // tr,),
        in_specs=[row(D_MODEL), row(Q_LORA), row(KV_LORA), full(gq), full(gkv), full(wuq_e), full(wukv), row(LANES),
                  row(LANES), head, head, head, row(D_MODEL), row(512), row(512), row(512), row(LANES), row(LANES),
                  full(w_in_p)],
        out_specs=[row(D_MODEL), row(IN_WIDTH_PAD), full(wuq_e), full(wukv), full(gq), full(gkv)],
        out_shape=[jax.ShapeDtypeStruct((seq, D_MODEL), F32), jax.ShapeDtypeStruct((seq, IN_WIDTH_PAD), BF16),
                   jax.ShapeDtypeStruct(wuq_e.shape, F32), jax.ShapeDtypeStruct(wukv.shape, F32),
                   jax.ShapeDtypeStruct(gq.shape, F32), jax.ShapeDtypeStruct(gkv.shape, F32)],
        compiler_params=_cp(("arbitrary",), VMEM_LIMIT),
    )(dz, cq, ckv, gq, gkv, wuq_e, wukv, ct, st, dq, dk, dv, dgates, dqr, dkr, dvb, cd, sd, w_in_p)


def _dw_in(x, dh):
    seq = dh.shape[0]
    tk = 512
    tn = IN_WIDTH_PAD // 2

    def body(x_ref, dh_ref, o_ref):
        @pl.when(pl.program_id(1) == 0)
        def _():
            o_ref[...] = jnp.zeros_like(o_ref)

        o_ref[...] += _dot(dh_ref[...], x_ref[...].astype(BF16), TN)

    return pl.pallas_call(
        body, name="dw_in", grid=(2, seq // tk),
        in_specs=[pl.BlockSpec((tk, D_MODEL), lambda n, k: (k, 0)), pl.BlockSpec((tk, tn), lambda n, k: (k, n))],
        out_specs=pl.BlockSpec((tn, D_MODEL), lambda n, k: (n, 0)),
        out_shape=jax.ShapeDtypeStruct((IN_WIDTH_PAD, D_MODEL), F32),
        compiler_params=_cp(("arbitrary", "arbitrary"), VMEM_LIMIT),
    )(x, dh)


def _adam_update(w, g, m, v):
    nm = ADAM_B1 * m + (1.0 - ADAM_B1) * g
    nv = ADAM_B2 * v + (1.0 - ADAM_B2) * jnp.square(g)
    m_hat = nm / (1.0 - ADAM_B1 ** ADAM_STEP)
    v_hat = nv / (1.0 - ADAM_B2 ** ADAM_STEP)
    return -ADAM_LR * (m_hat / (jnp.sqrt(v_hat) + ADAM_EPS) + ADAM_WD * w), nm, nv


def _adamw(w, g, m, v, name):
    rows, cols = w.shape
    tc = 256 if cols % 256 == 0 and rows * cols > 2 ** 18 else cols

    def body(w_ref, g_ref, m_ref, v_ref, d_ref, nm_ref, nv_ref):
        d_ref[...], nm_ref[...], nv_ref[...] = _adam_update(w_ref[...], g_ref[...], m_ref[...], v_ref[...])

    spec = pl.BlockSpec((rows, tc), lambda i: (0, i))
    return pl.pallas_call(
        body, name=name, grid=(cols // tc,), in_specs=[spec] * 4, out_specs=[spec] * 3,
        out_shape=[jax.ShapeDtypeStruct(w.shape, F32)] * 3, compiler_params=_cp(("arbitrary",)),
    )(w, g, m, v)


def _adamw_vectors(small_sum, ws, ms, vs):
    k = len(ws)
    sizes = [w.shape[0] for w in ws]

    def body(s_ref, *refs):
        ins, outs = refs[:3 * k], refs[3 * k:]
        for i, size in enumerate(sizes):
            g = s_ref[i, 0:size]
            outs[i][...] = g
            outs[k + i][...], outs[2 * k + i][...], outs[3 * k + i][...] = _adam_update(
                ins[i][...], g, ins[k + i][...], ins[2 * k + i][...])

    out = pl.pallas_call(
        body, name="adamw_vectors", out_shape=[jax.ShapeDtypeStruct((size,), F32) for size in sizes] * 4,
    )(small_sum, *ws, *ms, *vs)
    return [out[k * j:k * (j + 1)] for j in range(4)]


def _local_step(x2, target, w_in_p, w_uq_f, wukv_f, w_out, q_norm_g, kv_norm_g, ln_g, ln_b):
    seq = x2.shape[0]
    wuq_e = jnp.pad(w_uq_f.reshape(Q_LORA, MLA_HEADS, 96), ((0, 0), (0, 0), (0, 32))).reshape(Q_LORA, MLA_HEADS * LANES)
    ct, st, cd, sd = _rope_tables(seq)
    gq = q_norm_g.reshape(1, Q_LORA)
    gkv = kv_norm_g.reshape(1, KV_LORA)

    cq, ckv, gates, qr, krot, vb, q_e, k_e, v_e, g_w_out = _proj(
        x2, w_in_p, gq, gkv, wuq_e, wukv_f, ct, st, cd, sd, w_out)
    w_out_f = g_w_out.reshape(D_MODEL, D_MODEL)
    o_a, lse_a = _mla_fwd(q_e, k_e, v_e)
    o_b, lse_b = _dil_fwd(qr, krot, vb)

    dz, d_o, d_gates, dw_out, dln_g, dln_b, loss_part = _post(
        x2, o_a, o_b, gates, w_out_f, ln_g.reshape(1, D_MODEL), ln_b.reshape(1, D_MODEL), target)
    dq_e, dk_e, dv_e = _mla_bwd(q_e, k_e, v_e, d_o, o_a, lse_a)
    dqr, dkr, dvb = _dil_bwd(qr, krot, vb, d_o, o_b, lse_b)
    grad_x, dh, dwuq_e, dwukv, dgq, dgkv = _in_bwd(
        dz, cq, ckv, gq, gkv, wuq_e, wukv_f, ct, st, dq_e, dk_e, dv_e, d_gates, dqr, dkr, dvb, cd, sd, w_in_p)
    dw_in = _dw_in(x2, dh)
    dw_uq = dwuq_e.reshape(Q_LORA, MLA_HEADS, LANES)[:, :, :96].reshape(Q_LORA, MLA_HEADS * 96)
    return loss_part, grad_x, dw_in, dw_uq, dwukv, dw_out, dgq, dgkv, dln_g, dln_b


def kernel(x, w_in, q_norm_g, kv_norm_g, w_uq, w_ukv, w_out, ln_g, ln_b, loss_target, m_w_in, m_q_norm_g, m_kv_norm_g, m_w_uq, m_w_ukv, m_w_out, m_ln_g, m_ln_b, v_w_in, v_q_norm_g, v_kv_norm_g, v_w_uq, v_w_ukv, v_w_out, v_ln_g, v_ln_b):
    seq = x.shape[1]
    x2 = x.reshape(seq, D_MODEL)
    target = loss_target.reshape(seq, D_MODEL)

    g_w_in, g_w_uq, g_w_ukv = _all_gather_weights([w_in.T, w_uq, w_ukv])
    by_cols = lambda g: jnp.concatenate([g[j] for j in range(N_SHARD)], axis=1)
    loss_part, grad_x, dw_in, dw_uq, dwukv, dw_out, dgq, dgkv, dln_g, dln_b = _local_step(
        x2, target, g_w_in, by_cols(g_w_uq), by_cols(g_w_ukv), w_out, q_norm_g, kv_norm_g, ln_g, ln_b)

    to_shards = lambda d: d.reshape(d.shape[0], N_SHARD, d.shape[1] // N_SHARD).transpose(1, 0, 2)
    grads = [dw_in, to_shards(dw_uq), to_shards(dwukv), dw_out.reshape(N_SHARD, 256, D_MODEL)]
    g_in_t, g_uq, g_ukv, g_out, small_sum = _reduce_gradients(grads, [dgq, dgkv, dln_g, dln_b, loss_part])
    g_in = g_in_t.T
    loss = small_sum[4, 0]

    big = [[o.T for o in _adamw(w.T, g.T, m.T, v.T, name)] for w, g, m, v, name in (
        (w_in, g_in, m_w_in, v_w_in, "adamw_w_in"), (w_uq, g_uq, m_w_uq, v_w_uq, "adamw_w_uq"))]
    big += [_adamw(w, g, m, v, name) for w, g, m, v, name in (
        (w_ukv, g_ukv, m_w_ukv, v_w_ukv, "adamw_w_ukv"), (w_out, g_out, m_w_out, v_w_out, "adamw_w_out"))]
    vec_g, vec_delta, vec_m, vec_v = _adamw_vectors(
        small_sum, [q_norm_g, kv_norm_g, ln_g, ln_b], [m_q_norm_g, m_kv_norm_g, m_ln_g, m_ln_b],
        [v_q_norm_g, v_kv_norm_g, v_ln_g, v_ln_b])

    def ordered(bigs, vecs):
        return [bigs[0], vecs[0], vecs[1], bigs[1], bigs[2], bigs[3], vecs[2], vecs[3]]

    grads_out = ordered([g_in, g_uq, g_ukv, g_out], vec_g)
    deltas = ordered([b[0] for b in big], vec_delta)
    new_m = ordered([b[1] for b in big], vec_m)
    new_v = ordered([b[2] for b in big], vec_v)
    return (loss, grad_x.reshape(x.shape), *grads_out, *deltas, *new_m, *new_v)
```

```python
import jax
import jax.numpy as jnp
import numpy as np
from jax import lax
from jax.experimental import pallas as pl
from jax.experimental.pallas import tpu as pltpu

F32 = jnp.float32
BF16 = jnp.bfloat16

D_MODEL = 1024
ROPE_THETA = 500000.0
BLOCK = 128
NEG = -1e30
RMS_EPS = 1e-6
LN_EPS = 1e-5

MLA_HEADS = 8
MLA_NOPE = 64
MLA_ROPE = 32
Q_LORA = 384
KV_LORA = 256
DIL_HEAD_DIM = 64
DIL_ROT = 16
DIL_DILATIONS = (1, 4, 16)
IN_WIDTH_PAD = 3328
ONES_LANE = (64, 0)
MLA_SCALE = (MLA_NOPE + MLA_ROPE) ** -0.5
DIL_SCALE = DIL_HEAD_DIM ** -0.5
ALPHA = 2.0 ** 0.25

ADAM_LR = 0.001
ADAM_B1 = 0.9
ADAM_B2 = 0.999
ADAM_EPS = 1e-08
ADAM_WD = 0.01
ADAM_STEP = 10

N_SHARD = 4
SHARD_SHAPES = ((808, 1024), (384, 192), (256, 256), (256, 1024))
SHARD_SPLIT_COLS = (True, False, False, False)
ROW_CHUNK = 64
LANES = 128
VMEM_LIMIT = 56 * 1024 * 1024
MESH = pl.DeviceIdType.MESH

NT = (((1,), (1,)), ((), ()))
TN = (((0,), (0,)), ((), ()))


def _cp(sem=None, vmem=None):
    return pltpu.CompilerParams(dimension_semantics=sem, vmem_limit_bytes=vmem)


def _dot(a, b, dims=None):
    if dims is None:
        return jnp.dot(a, b, preferred_element_type=F32)
    return lax.dot_general(a, b, dims, preferred_element_type=F32)


def _rope_tables(seq):
    f32 = np.float32
    pos = np.arange(seq, dtype=f32)[:, None]
    one, zero = np.ones((seq, 64), f32), np.zeros((seq, 64), f32)

    def cos_sin(dim):
        inv = np.power(f32(ROPE_THETA), -np.arange(0, dim, 2, dtype=f32) / f32(dim)).astype(f32)
        ang = (pos * inv[None, :]).astype(f32)
        return np.cos(ang).astype(f32), np.sin(ang).astype(f32)

    cos, sin = cos_sin(MLA_ROPE)
    ct = np.concatenate([one, cos, cos, zero[:, :32]], axis=1)
    st = np.concatenate([zero, -sin, sin, zero[:, :32]], axis=1)
    cos, sin = cos_sin(DIL_ROT)
    cd = np.concatenate([cos, cos, one[:, :48]], axis=1)
    sd = np.concatenate([-sin, sin, zero[:, :48]], axis=1)
    return tuple(jnp.asarray(t) for t in (ct, st, np.tile(cd, (1, 2)), np.tile(sd, (1, 2))))


W_IN_ORDER = ((0, 640), (672, 1184), (2720, 3232), (1184, 2720), None, (640, 672))


def _w_in_row_pieces():
    width = SHARD_SHAPES[0][0]
    pieces, at = [], 0
    for r in W_IN_ORDER:
        if r is None:
            at += 64
            continue
        for k in range(N_SHARD):
            lo, hi = max(r[0], width * k), min(r[1], width * (k + 1))
            if lo < hi:
                pieces.append((k, lo - width * k, at + lo - r[0], hi - lo))
        at += r[1] - r[0]
    return pieces


def _position():
    return lax.axis_index("x"), lax.axis_index("y"), lax.axis_index("c")


def _all_gather_weights(shards):
    n = len(shards)

    def body(*refs):
        ins, outs = refs[:n], list(refs[n:2 * n])
        w_in_p, outs[0] = outs[0], refs[2 * n]
        send_sems, recv_sems = refs[2 * n + 1:]
        x, y, c = _position()
        me = 2 * x + y
        chips = [(1 - x, y), (x, 1 - y), (1 - x, 1 - y)]
        for a in range(n):
            for blk in _shard_blocks(a):
                outs[a][(me,) + blk] = ins[a][blk].astype(BF16)

        def copy(k, a, slot, part, to):
            ref = outs[a].at[(slot,) + part]
            return pltpu.make_async_remote_copy(
                src_ref=ref, dst_ref=ref, send_sem=send_sems.at[k * n + a], recv_sem=recv_sems.at[k * n + a],
                device_id=to, device_id_type=MESH)

        half = [_shard_half(a, c) for a in range(n)]
        other = [_shard_half(a, 1 - c) for a in range(n)]
        first = [copy(k, a, me, half[a], (px, py, c)) for k, (px, py) in enumerate(chips) for a in range(n)]
        for cp in first:
            cp.start()
        passed = []
        for k, (px, py) in enumerate(chips):
            for a in range(n):
                copy(k, a, 2 * px + py, half[a], (x, y, c)).wait_recv()
                cp = copy(3 + k, a, 2 * px + py, half[a], (x, y, 1 - c))
                cp.start()
                passed.append(cp)
        for k, (px, py) in enumerate(chips):
            for a in range(n):
                copy(3 + k, a, 2 * px + py, other[a], (x, y, c)).wait_recv()
        for cp in first + passed:
            cp.wait_send()

        written = []
        for k, r0, at, rows in _w_in_row_pieces():
            written.append((at, at + rows))
            for r in range(0, rows, 2 * LANES):
                m = min(2 * LANES, rows - r)
                for c0 in range(0, D_MODEL, LANES):
                    w_in_p[at + r:at + r + m, c0:c0 + LANES] = outs[0][k, r0 + r:r0 + r + m, c0:c0 + LANES]
        for lo, hi in zip([0] + [w[1] for w in sorted(written)], [w[0] for w in sorted(written)] + [IN_WIDTH_PAD]):
            if lo < hi:
                w_in_p[lo:hi, :] = jnp.zeros((hi - lo, D_MODEL), BF16)

    vmem = pl.BlockSpec(memory_space=pltpu.VMEM)
    return pl.pallas_call(
        body, name="all_gather_weights",
        out_shape=[jax.ShapeDtypeStruct((IN_WIDTH_PAD, D_MODEL), BF16)]
        + [jax.ShapeDtypeStruct((N_SHARD,) + s, BF16) for s in SHARD_SHAPES[1:n]],
        in_specs=[vmem] * n, out_specs=[vmem] * n,
        scratch_shapes=[pltpu.VMEM((N_SHARD,) + SHARD_SHAPES[0], BF16),
                        pltpu.SemaphoreType.DMA((6 * n,)), pltpu.SemaphoreType.DMA((6 * n,))],
        compiler_params=_cp(None, VMEM_LIMIT),
    )(*shards)


def _shard_blocks(a):
    rows, cols = SHARD_SHAPES[a]
    if SHARD_SPLIT_COLS[a]:
        return [(slice(None), slice(c0, c0 + LANES)) for c0 in range(0, cols, LANES)]
    return [(slice(r0, r0 + ROW_CHUNK), slice(None)) for r0 in range(0, rows, ROW_CHUNK)]


def _shard_half_shape(a):
    rows, cols = SHARD_SHAPES[a]
    return (rows, cols // 2) if SHARD_SPLIT_COLS[a] else (rows // 2, cols)


def _shard_half(a, c):
    rows, cols = SHARD_SHAPES[a]
    if SHARD_SPLIT_COLS[a]:
        return slice(None), pl.ds(pl.multiple_of(c * (cols // 2), LANES), cols // 2)
    return pl.ds(pl.multiple_of(c * (rows // 2), ROW_CHUNK), rows // 2), slice(None)


def _shard_chunks(a, c):
    rows, cols = SHARD_SHAPES[a]
    if SHARD_SPLIT_COLS[a]:
        return [((slice(None), pl.ds(c0, LANES)),
                 (slice(None), pl.ds(pl.multiple_of(c * (cols // 2) + c0, LANES), LANES)))
                for c0 in range(0, cols // 2, LANES)]
    return [((pl.ds(r0, ROW_CHUNK), slice(None)),
             (pl.ds(pl.multiple_of(c * (rows // 2) + r0, ROW_CHUNK), ROW_CHUNK), slice(None)))
            for r0 in range(0, rows // 2, ROW_CHUNK)]


def _reduce_gradients(grads, small_rows):
    n = len(grads)
    n_small = len(small_rows)
    pieces = _w_in_row_pieces()
    order = sorted(range(n), key=lambda a: SHARD_SHAPES[a][0] * SHARD_SHAPES[a][1])

    def body(*refs):
        g_hbm, rows_in = refs[:n], refs[n:n + n_small]
        outs, small_sum = refs[n + n_small:2 * n + n_small], refs[2 * n + n_small]
        scratch = refs[2 * n + n_small + 1:]
        stage, got, sums, others = (scratch[i * n:(i + 1) * n] for i in range(4))
        sm, smalls, send_sems, recv_sems, local_sems = scratch[4 * n:]
        swap_sem, chip_sem, join_sem, small_sem = 0, n, 4 * n, 5 * n
        x, y, c = _position()
        me = 4 * x + 2 * y + c
        chips = [(1 - x, y), (x, 1 - y), (1 - x, 1 - y)]
        sm[...] = jnp.zeros_like(sm)
        for i, row in enumerate(rows_in):
            sm[i:i + 1, 0:row.shape[1]] = row[...]
        loads = [[pltpu.make_async_copy(g_hbm[0].at[pl.ds(src, rows)], stage[0].at[k, pl.ds(dst, rows)],
                                        local_sems.at[n + i])
                  for i, (k, dst, src, rows) in enumerate(pieces)]]
        loads += [[pltpu.make_async_copy(g_hbm[a], stage[a], local_sems.at[a])] for a in range(1, n)]
        for a in order:
            for ld in loads[a]:
                ld.start()
        smalls[me] = sm[...]
        small_sends = []
        for rel in range(1, 8):
            px = 1 - x if rel // 4 else x
            py = 1 - y if (rel // 2) % 2 else y
            pc = 1 - c if rel % 2 else c
            cp = pltpu.make_async_remote_copy(
                src_ref=sm, dst_ref=smalls.at[me], send_sem=send_sems.at[small_sem + rel],
                recv_sem=recv_sems.at[small_sem + rel], device_id=(px, py, pc), device_id_type=MESH)
            cp.start()
            small_sends.append((cp, 4 * px + 2 * py + pc))
        swaps = {}
        for a in order:
            for ld in loads[a]:
                ld.wait()
            swaps[a] = pltpu.make_async_remote_copy(
                src_ref=stage[a].at[(slice(None),) + _shard_half(a, 1 - c)], dst_ref=got[a],
                send_sem=send_sems.at[swap_sem + a], recv_sem=recv_sems.at[swap_sem + a],
                device_id=(x, y, 1 - c), device_id_type=MESH)
            swaps[a].start()
        sends = {}
        for a in order:
            swaps[a].wait_recv()
            for k in range(N_SHARD):
                for in_half, in_whole in _shard_chunks(a, c):
                    pair = stage[a][(k,) + in_whole] + got[a][(k,) + in_half]
                    sums[a][(k,) + in_half] = pair.astype(BF16)
            sends[a] = [pltpu.make_async_remote_copy(
                src_ref=sums[a].at[2 * px + py], dst_ref=others[a].at[k], send_sem=send_sems.at[chip_sem + k * n + a],
                recv_sem=recv_sems.at[chip_sem + k * n + a], device_id=(px, py, c), device_id_type=MESH)
                for k, (px, py) in enumerate(chips)]
            for cp in sends[a]:
                cp.start()
        joins = []
        for a in order:
            for cp in sends[a]:
                cp.wait_recv()
            for in_half, in_whole in _shard_chunks(a, c):
                total = sums[a][(2 * x + y,) + in_half].astype(F32)
                for k in range(3):
                    total = total + others[a][(k,) + in_half].astype(F32)
                outs[a][in_whole] = total
            half = outs[a].at[_shard_half(a, c)]
            cp = pltpu.make_async_remote_copy(
                src_ref=half, dst_ref=half, send_sem=send_sems.at[join_sem + a],
                recv_sem=recv_sems.at[join_sem + a], device_id=(x, y, 1 - c), device_id_type=MESH)
            cp.start()
            joins.append(cp)
        for rel, (cp, peer) in enumerate(small_sends, start=1):
            pltpu.make_async_remote_copy(
                src_ref=sm, dst_ref=smalls.at[peer], send_sem=send_sems.at[small_sem + rel],
                recv_sem=recv_sems.at[small_sem + rel], device_id=(x, y, c), device_id_type=MESH).wait_recv()
        total = smalls[0]
        for dev in range(1, 8):
            total = total + smalls[dev]
        small_sum[...] = total
        for a in order:
            other = outs[a].at[_shard_half(a, 1 - c)]
            pltpu.make_async_remote_copy(
                src_ref=other, dst_ref=other, send_sem=send_sems.at[join_sem + a],
                recv_sem=recv_sems.at[join_sem + a], device_id=(x, y, c), device_id_type=MESH).wait_recv()
        for cp in list(swaps.values()) + [cp for a in order for cp in sends[a]] + joins + [cp for cp, _ in small_sends]:
            cp.wait_send()

    vmem = pl.BlockSpec(memory_space=pltpu.VMEM)
    halves = [_shard_half_shape(a) for a in range(n)]
    return pl.pallas_call(
        body, name="reduce_gradients",
        out_shape=[jax.ShapeDtypeStruct(s, F32) for s in SHARD_SHAPES[:n]] + [jax.ShapeDtypeStruct((8, D_MODEL), F32)],
        in_specs=[pl.BlockSpec(memory_space=pl.ANY)] * n + [vmem] * n_small, out_specs=[vmem] * (n + 1),
        scratch_shapes=[pltpu.VMEM((N_SHARD,) + s, F32) for s in SHARD_SHAPES[:n]]
        + [pltpu.VMEM((N_SHARD,) + s, F32) for s in halves] + [pltpu.VMEM((N_SHARD,) + s, BF16) for s in halves]
        + [pltpu.VMEM((3,) + s, BF16) for s in halves]
        + [pltpu.VMEM((8, D_MODEL), F32), pltpu.VMEM((8, 8, D_MODEL), F32),
           pltpu.SemaphoreType.DMA((5 * n + 8,)), pltpu.SemaphoreType.DMA((5 * n + 8,)),
           pltpu.SemaphoreType.DMA((n + len(pieces),))],
        compiler_params=_cp(None, VMEM_LIMIT),
    )(*grads, *small_rows)


def _proj(x, w_in_p, gq, gkv, wuq_e, wukv, ct, st, cd, sd, w_out):
    seq = x.shape[0]
    tr = 512
    a_out = 3

    def gather_w_out(w_out_ref, all_ref, land, send_sems, recv_sems):
        x_, y_, c = _position()
        me = 2 * x_ + y_
        chips = [(1 - x_, y_), (x_, 1 - y_), (1 - x_, 1 - y_)]
        half, other = _shard_half(a_out, c), _shard_half(a_out, 1 - c)

        def copy(k, slot, part, to):
            ref = land.at[(slot,) + part]
            return pltpu.make_async_remote_copy(
                src_ref=ref, dst_ref=ref, send_sem=send_sems.at[k], recv_sem=recv_sems.at[k],
                device_id=to, device_id_type=MESH)

        first = [copy(k, me, half, (px, py, c)) for k, (px, py) in enumerate(chips)]

        @pl.when(pl.program_id(0) == 0)
        def _():
            for blk in _shard_blocks(a_out):
                land[(me,) + blk] = w_out_ref[blk].astype(BF16)
            for cp in first:
                cp.start()

        @pl.when(pl.program_id(0) == pl.num_programs(0) - 1)
        def _():
            passed = []
            for k, (px, py) in enumerate(chips):
                copy(k, 2 * px + py, half, (x_, y_, c)).wait_recv()
                passed.append(copy(3 + k, 2 * px + py, half, (x_, y_, 1 - c)))
                passed[-1].start()
            for k, (px, py) in enumerate(chips):
                copy(3 + k, 2 * px + py, other, (x_, y_, c)).wait_recv()
            for cp in first + passed:
                cp.wait_send()
            all_ref[...] = land[...]

    def body(x_ref, w_ref, gq_ref, gkv_ref, wuq_ref, wukv_ref, ct_ref, st_ref, cd_ref, sd_ref, w_out_ref,
             cq_ref, ckv_ref, g_ref, qr_ref, kr_ref, vb_ref, q_out, k_out, v_out, w_out_all, land, send_sems, recv_sems):
        gather_w_out(w_out_ref, w_out_all, land, send_sems, recv_sems)
        lane = lax.broadcasted_iota(jnp.int32, (tr, LANES), 1)
        xb = x_ref[...].astype(BF16)
        cq = _dot(xb, w_ref[0:384, :], NT)
        ckv = _dot(xb, w_ref[384:640, :], NT)
        cq_ref[...] = cq
        ckv_ref[...] = ckv
        g_ref[...] = _dot(xb, w_ref[640:1664, :], NT)

        cd_, sd_ = cd_ref[...], sd_ref[...]
        qb = _dot(xb, w_ref[1664:2176, :], NT)
        kb = _dot(xb, w_ref[2176:2688, :], NT)
        for p in range(4):
            cols = slice(LANES * p, LANES * (p + 1))
            t = qb[:, cols]
            qr_ref[:, cols] = (t * cd_ + _dil_rot(t, lane) * sd_) * DIL_SCALE
            t = kb[:, cols]
            kr_ref[:, cols] = t * cd_ + _dil_rot(t, lane) * sd_
        vb_ref[...] = _dot(xb, w_ref[2688:3200, :], NT)

        ct_, st_ = ct_ref[...], st_ref[...]

        def rope(t):
            return t * ct_ + _mla_rot(t, lane) * st_

        _, qn = _rms(cq, gq_ref[...])
        q_all = _dot(qn.astype(BF16), wuq_ref[...])
        for h in range(MLA_HEADS):
            q_out[h] = (rope(q_all[:, LANES * h:LANES * (h + 1)]) * MLA_SCALE).astype(BF16)
        _, kvn = _rms(ckv, gkv_ref[...])
        kv_all = _dot(kvn.astype(BF16), wukv_ref[...])
        kpe = rope(_dot(xb, w_ref[3200:3328, :], NT))
        for h in range(MLA_HEADS):
            kv_h = kv_all[:, LANES * h:LANES * (h + 1)]
            k_out[h] = jnp.where(lane < 64, kv_h, kpe).astype(BF16)
            if h % 2:
                v = jnp.where(lane >= 64, kv_h, 0.0)
            else:
                v = jnp.where(lane < 64, pltpu.roll(kv_h, 64, 1), 0.0)
            v_out[h] = jnp.where(lane == ONES_LANE[h % 2], 1.0, v).astype(BF16)

    row = lambda w: pl.BlockSpec((tr, w), lambda i: (i, 0))
    full = lambda a: pl.BlockSpec(a.shape, lambda i: (0,) * a.ndim)
    head = pl.BlockSpec((MLA_HEADS, tr, LANES), lambda i: (0, i, 0))
    widths = (Q_LORA, KV_LORA, D_MODEL, 512, 512, 512)
    gathered = (N_SHARD,) + SHARD_SHAPES[a_out]
    return pl.pallas_call(
        body, name="proj", grid=(seq // tr,),
        in_specs=[row(D_MODEL), full(w_in_p), full(gq), full(gkv), full(wuq_e), full(wukv)] + [row(LANES)] * 4
        + [full(w_out)],
        out_specs=[row(w) for w in widths] + [head] * 3 + [pl.BlockSpec(gathered, lambda i: (0, 0, 0))],
        out_shape=[jax.ShapeDtypeStruct((seq, w), F32) for w in widths]
        + [jax.ShapeDtypeStruct((MLA_HEADS, seq, LANES), BF16)] * 3 + [jax.ShapeDtypeStruct(gathered, BF16)],
        scratch_shapes=[pltpu.VMEM(gathered, BF16), pltpu.SemaphoreType.DMA((6,)), pltpu.SemaphoreType.DMA((6,))],
        compiler_params=_cp(("arbitrary",), VMEM_LIMIT),
    )(x, w_in_p, gq, gkv, wuq_e, wukv, ct, st, cd, sd, w_out)


def _mla_rot(t, lane):
    return jnp.where(lane < 80, pltpu.roll(t, 112, 1), pltpu.roll(t, 16, 1))


def _dil_rot(t, lane):
    return jnp.where(lane % 64 < 8, pltpu.roll(t, 120, 1), pltpu.roll(t, 8, 1))


def _rms(c, g):
    r = lax.rsqrt(jnp.mean(c * c, axis=-1, keepdims=True) + RMS_EPS)
    return r, c * r * g


def _mla_fwd(q, k, v):
    seq = q.shape[1]
    tq = 512
    nq = seq // tq

    def body(q_ref, k_ref, v_ref, o_ref, lse_ref, m_s, acc_s, s_buf):
        i = pl.program_id(1)
        row = lax.broadcasted_iota(jnp.int32, (tq, tq), 0)
        col = lax.broadcasted_iota(jnp.int32, (tq, tq), 1)
        lane = lax.broadcasted_iota(jnp.int32, (tq, LANES), 1)
        m_s[...] = jnp.full((2, tq, LANES), NEG, F32)
        acc_s[...] = jnp.zeros((2, tq, LANES), F32)

        def block(j):
            return pl.ds(pl.multiple_of(j * tq, tq), tq)

        def scores(hh, j):
            return _dot(q_ref[hh], k_ref[hh, block(j), :], NT)

        def consume(hh, j, s):
            m_prev = m_s[hh]
            m_new = jnp.maximum(m_prev, jnp.max(s, axis=1, keepdims=True))
            p = jnp.exp(s - m_new[:, :1])
            acc_s[hh] = jnp.exp(m_prev - m_new) * acc_s[hh] + _dot(p.astype(BF16), v_ref[hh, block(j), :])
            m_s[hh] = m_new

        for hh in range(2):
            s_buf[0, hh] = scores(hh, 0)

        def full_step(j, carry):
            slot = j & 1
            for hh in range(2):
                s = s_buf[slot, hh]
                s_buf[1 - slot, hh] = scores(hh, j + 1)
                consume(hh, j, s)
            return carry

        lax.fori_loop(0, i, full_step, 0)
        total = jnp.zeros((tq, LANES), F32)
        for hh in range(2):
            consume(hh, i, jnp.where(col <= row, s_buf[i & 1, hh], NEG))
            acc = acc_s[hh]
            l = acc[:, ONES_LANE[hh]:ONES_LANE[hh] + 1]
            mine = (lane >= 64) if hh else (lane < 64)
            total = total + jnp.where(mine, acc / l, 0.0)
            lse_ref[hh] = m_s[hh] + jnp.log(l)
        o_ref[...] = total

    kv_spec = pl.BlockSpec((2, seq, LANES), lambda p, i: (p, 0, 0))
    return pl.pallas_call(
        body, name="mla_fwd", grid=(MLA_HEADS // 2, nq),
        in_specs=[pl.BlockSpec((2, tq, LANES), lambda p, i: (p, i, 0)), kv_spec, kv_spec],
        out_specs=[pl.BlockSpec((tq, LANES), lambda p, i: (i, p)), pl.BlockSpec((2, tq, LANES), lambda p, i: (p, i, 0))],
        out_shape=[jax.ShapeDtypeStruct((seq, 4 * LANES), F32), jax.ShapeDtypeStruct((MLA_HEADS, seq, LANES), F32)],
        scratch_shapes=[pltpu.VMEM((2, tq, LANES), F32), pltpu.VMEM((2, tq, LANES), F32),
                        pltpu.VMEM((2, 2, tq, tq), F32)],
        compiler_params=_cp(("arbitrary", "arbitrary"), VMEM_LIMIT),
    )(q, k, v)


DIL_Q_FWD = 2 * BLOCK
DIL_Q_BWD = BLOCK


def _dil_tile_index(t, d, seq, nq):
    per_class = seq // (nq * d)
    shift = per_class.bit_length() - 1
    r = t >> shift
    n = t & (per_class - 1)
    start = r + (nq * d) * n
    prev = jnp.maximum(start - BLOCK * d, r)
    if d == 1:
        start = pl.multiple_of(start, nq)
        prev = pl.multiple_of(prev, BLOCK)
    return (n == 0).astype(jnp.int32), start, prev


def _dil_rows(start, d, size):
    return pl.ds(start, size) if d == 1 else pl.ds(start, size, stride=d)


def _dil_bias(nq):
    i = lax.broadcasted_iota(jnp.int32, (2 * nq, BLOCK + nq), 0) % nq
    j = lax.broadcasted_iota(jnp.int32, (2 * nq, BLOCK + nq), 1)
    band = (j >= i) & (j <= i + BLOCK)
    return jnp.where(band, 0.0, NEG), jnp.where(band & (j >= BLOCK), 0.0, NEG)


def _stack_heads(t, lane):
    return jnp.concatenate([jnp.where(lane < 64, t, 0.0), jnp.where(lane >= 64, t, 0.0)], axis=0)


def _unstack_heads(t, lane):
    nq = t.shape[0] // 2
    return jnp.where(lane < 64, t[:nq], t[nq:])


def _dil_fwd(qr, kr, vb):
    seq = qr.shape[0]
    nq = DIL_Q_FWD
    n_tiles = seq // nq
    assert seq % (nq * max(DIL_DILATIONS)) == 0

    def body(q_ref, k_ref, v_ref, o_ref, lse_ref, m_s, l_s, n_s, bias_s):
        lane = lax.broadcasted_iota(jnp.int32, (nq, LANES), 1)
        bias_s[0], bias_s[1] = _dil_bias(nq)
        for bi, d in enumerate(DIL_DILATIONS):

            def tile(t, carry, d=d, bi=bi):
                first, start, prev = _dil_tile_index(t, d, seq, nq)
                rows, prows = _dil_rows(start, d, nq), _dil_rows(prev, d, BLOCK)
                qst = _stack_heads(q_ref[rows, :], lane).astype(BF16)
                if seq == nq * d:
                    kcat, vcat = k_ref[rows, :].astype(BF16), v_ref[rows, :].astype(BF16)
                    s = _dot(qst, kcat, NT) + bias_s[1, :, BLOCK:]
                else:
                    kcat = jnp.concatenate([k_ref[prows, :], k_ref[rows, :]], axis=0).astype(BF16)
                    vcat = jnp.concatenate([v_ref[prows, :], v_ref[rows, :]], axis=0).astype(BF16)
                    s = _dot(qst, kcat, NT) + bias_s[first]
                m = jnp.max(s, axis=1, keepdims=True)
                p = jnp.exp(s - m)
                l2 = _unstack_heads(jnp.sum(p, axis=1, keepdims=True) + jnp.zeros((2 * nq, LANES), F32), lane)
                m2 = _unstack_heads(m + jnp.zeros((2 * nq, LANES), F32), lane)
                num2 = _unstack_heads(_dot(p.astype(BF16), vcat), lane)
                if bi == 0:
                    m_s[rows, :] = m2
                    l_s[rows, :] = l2
                    n_s[rows, :] = num2
                else:
                    m_old = m_s[rows, :]
                    m_new = jnp.maximum(m_old, m2)
                    a = jnp.exp(m_old - m_new)
                    b = jnp.exp(m2 - m_new)
                    m_s[rows, :] = m_new
                    l_s[rows, :] = a * l_s[rows, :] + b * l2
                    n_s[rows, :] = a * n_s[rows, :] + b * num2
                return carry

            lax.fori_loop(0, n_tiles, tile, 0, unroll=8)
        o_ref[...] = n_s[...] / l_s[...]
        lse_ref[...] = m_s[...] + jnp.log(l_s[...])

    col = lambda off: pl.BlockSpec((seq, LANES), lambda p: (0, p + off))
    return pl.pallas_call(
        body, name="dil_fwd", grid=(4,),
        in_specs=[col(0), col(0), col(0)],
        out_specs=[col(0), pl.BlockSpec((None, seq, LANES), lambda p: (p, 0, 0))],
        out_shape=[jax.ShapeDtypeStruct((seq, 4 * LANES), F32), jax.ShapeDtypeStruct((4, seq, LANES), F32)],
        scratch_shapes=[pltpu.VMEM((seq, LANES), F32)] * 3 + [pltpu.VMEM((2, 2 * nq, BLOCK + nq), F32)],
        compiler_params=_cp(("arbitrary",), VMEM_LIMIT),
    )(qr, kr, vb)


def _post(x, o_a, o_b, gates, w_out, ln_g, ln_b, target):
    seq = x.shape[0]
    tr = 512

    def body(x_ref, oa_ref, ob_ref, g_ref, w_ref, lg_ref, lb_ref, t_ref,
             dz_ref, do_ref, dg_ref, dw_ref, dlg_ref, dlb_ref, loss_ref):
        @pl.when(pl.program_id(0) == 0)
        def _():
            dw_ref[...] = jnp.zeros_like(dw_ref)
            dlg_ref[...] = jnp.zeros_like(dlg_ref)
            dlb_ref[...] = jnp.zeros_like(dlb_ref)
            loss_ref[...] = jnp.zeros_like(loss_ref)

        g = g_ref[...]
        sg = jax.nn.sigmoid(g)
        silu = g * sg
        o = jnp.concatenate([oa_ref[...], ob_ref[...]], axis=1)
        mixb = (o * silu).astype(BF16)
        w = w_ref[...]
        z = ALPHA * x_ref[...] + _dot(mixb, w)
        mu = jnp.mean(z, axis=-1, keepdims=True)
        zc = z - mu
        rstd = lax.rsqrt(jnp.mean(zc * zc, axis=-1, keepdims=True) + LN_EPS)
        xhat = zc * rstd
        lg = lg_ref[...]
        err = xhat * lg + lb_ref[...] - t_ref[...]
        loss_ref[...] += jnp.sum(err * err) * (0.5 / D_MODEL)
        dy = err * (1.0 / D_MODEL)
        dlg_ref[...] += jnp.sum(dy * xhat, axis=0, keepdims=True)
        dlb_ref[...] += jnp.sum(dy, axis=0, keepdims=True)
        dxh = dy * lg
        dz = rstd * (dxh - jnp.mean(dxh, axis=-1, keepdims=True) - xhat * jnp.mean(dxh * xhat, axis=-1, keepdims=True))
        dz_ref[...] = dz
        dzb = dz.astype(BF16)
        dmix = _dot(dzb, w, NT)
        do_ref[...] = dmix * silu
        dg_ref[...] = (dmix * o * (sg * (1.0 + g * (1.0 - sg)))).astype(BF16)
        dw_ref[...] += _dot(mixb, dzb, TN)

    row = lambda w: pl.BlockSpec((tr, w), lambda i: (i, 0))
    full = lambda s: pl.BlockSpec(s, lambda i: (0, 0))
    return pl.pallas_call(
        body, name="post", grid=(seq // tr,),
        in_specs=[row(D_MODEL), row(512), row(512), row(D_MODEL), full((D_MODEL, D_MODEL)), full((1, D_MODEL)),
                  full((1, D_MODEL)), row(D_MODEL)],
        out_specs=[row(D_MODEL), row(D_MODEL), row(D_MODEL), full((D_MODEL, D_MODEL)), full((1, D_MODEL)),
                   full((1, D_MODEL)), full((1, LANES))],
        out_shape=[jax.ShapeDtypeStruct((seq, D_MODEL), F32), jax.ShapeDtypeStruct((seq, D_MODEL), F32),
                   jax.ShapeDtypeStruct((seq, D_MODEL), BF16), jax.ShapeDtypeStruct((D_MODEL, D_MODEL), F32),
                   jax.ShapeDtypeStruct((1, D_MODEL), F32), jax.ShapeDtypeStruct((1, D_MODEL), F32),
                   jax.ShapeDtypeStruct((1, LANES), F32)],
        compiler_params=_cp(("arbitrary",), VMEM_LIMIT),
    )(x, o_a, o_b, gates, w_out, ln_g, ln_b, target)


def _mla_bwd(q, k, v, d_o, o, lse):
    seq = q.shape[1]
    tq = 512
    nq = seq // tq

    def body(q_ref, k_ref, v_ref, do_ref, o_ref, lse_ref, dq_ref, dk_ref, dv_ref, d_s, lse_s, dk_s, dv_s, v_s, kt_s, dqt_s):
        j = pl.program_id(1)
        lane = lax.broadcasted_iota(jnp.int32, (tq, LANES), 1)
        row = lax.broadcasted_iota(jnp.int32, (tq, tq), 0)
        col = lax.broadcasted_iota(jnp.int32, (tq, tq), 1)

        @pl.when(j == 0)
        def _():
            dqt_s[...] = jnp.zeros_like(dqt_s)

            def rowsum(i, carry):
                rows = pl.ds(pl.multiple_of(i * tq, tq), tq)
                prod = do_ref[rows, :] * o_ref[rows, :]
                for hh in range(2):
                    mine = (lane >= 64) if hh else (lane < 64)
                    total = jnp.sum(jnp.where(mine, prod, 0.0), axis=1, keepdims=True)
                    d_s[hh, i] = jnp.transpose(total + jnp.zeros((tq, LANES), F32))[:8]
                    lse_s[hh, i] = jnp.transpose(lse_ref[hh, rows, :])[:8]
                return carry

            lax.fori_loop(0, nq, rowsum, 0)

        dk_s[...] = jnp.zeros_like(dk_s)
        dv_s[...] = jnp.zeros_like(dv_s)
        for hh in range(2):
            v_s[hh] = jnp.where(lane == ONES_LANE[hh], 0.0, v_ref[hh].astype(F32)).astype(BF16)
            kt_s[hh] = jnp.transpose(k_ref[hh].astype(F32)).astype(BF16)

        def step(i, masked):
            rows = pl.ds(pl.multiple_of(i * tq, tq), tq)
            dob = do_ref[rows, :].astype(BF16)
            for hh in range(2):
                qb, kb, vb = q_ref[hh, rows, :], k_ref[hh], v_s[hh]
                p = jnp.exp(_dot(kb, qb, NT) - lse_s[hh, i][:1])
                if masked:
                    p = jnp.where(row <= col, p, 0.0)
                dv_s[hh] += _dot(p.astype(BF16), dob)
                ds = (p * (_dot(vb, dob, NT) - d_s[hh, i][:1])).astype(BF16)
                dk_s[hh] += _dot(ds, qb)
                dqt_s[hh, i] += _dot(kt_s[hh], ds)

        def full_step(i, carry):
            step(i, False)
            return carry

        step(j, True)
        lax.fori_loop(j + 1, nq, full_step, 0)
        dk_ref[...] = dk_s[...]
        dv_ref[...] = dv_s[...]

        @pl.when(j == nq - 1)
        def _():
            def untranspose(i, carry):
                rows = pl.ds(pl.multiple_of(i * tq, tq), tq)
                for hh in range(2):
                    dq_ref[hh, rows, :] = jnp.transpose(dqt_s[hh, i])
                return carry

            lax.fori_loop(0, nq, untranspose, 0)

    whole = pl.BlockSpec((2, seq, LANES), lambda p, j: (p, 0, 0))
    blk = pl.BlockSpec((2, tq, LANES), lambda p, j: (p, j, 0))
    pair = pl.BlockSpec((seq, LANES), lambda p, j: (0, p))
    shape = jax.ShapeDtypeStruct((MLA_HEADS, seq, LANES), F32)
    return pl.pallas_call(
        body, name="mla_bwd", grid=(MLA_HEADS // 2, nq),
        in_specs=[whole, blk, blk, pair, pair, whole],
        out_specs=[whole, blk, blk], out_shape=[shape] * 3,
        scratch_shapes=[pltpu.VMEM((2, nq, 8, tq), F32), pltpu.VMEM((2, nq, 8, tq), F32),
                        pltpu.VMEM((2, tq, LANES), F32), pltpu.VMEM((2, tq, LANES), F32),
                        pltpu.VMEM((2, tq, LANES), BF16), pltpu.VMEM((2, LANES, tq), BF16),
                        pltpu.VMEM((2, nq, LANES, tq), F32)],
        compiler_params=_cp(("arbitrary", "arbitrary"), VMEM_LIMIT),
    )(q, k, v, d_o, o, lse)


def _dil_bwd(qr, kr, vb, d_o, o, lse):
    seq = qr.shape[0]
    nq = DIL_Q_BWD
    n_tiles = seq // nq
    chunk = 512

    def body(q_ref, k_ref, v_ref, do_ref, o_ref, lse_ref, dq_ref, dk_ref, dv_ref, d_s, dq_s, dk_s, dv_s, bias_s):
        lane = lax.broadcasted_iota(jnp.int32, (nq, LANES), 1)
        lanec = lax.broadcasted_iota(jnp.int32, (chunk, LANES), 1)
        bias_s[0], bias_s[1] = [b[:nq] for b in _dil_bias(nq)]

        def rowsum(i, carry):
            rows = pl.ds(pl.multiple_of(i * chunk, chunk), chunk)
            prod = do_ref[rows, :] * o_ref[rows, :]
            lo = jnp.sum(jnp.where(lanec < 64, prod, 0.0), axis=1, keepdims=True)
            hi = jnp.sum(jnp.where(lanec >= 64, prod, 0.0), axis=1, keepdims=True)
            d_s[rows, :] = jnp.where(lanec < 64, lo, hi)
            return carry

        lax.fori_loop(0, seq // chunk, rowsum, 0)
        dq_s[...] = jnp.zeros_like(dq_s)
        dk_s[...] = jnp.zeros_like(dk_s)
        dv_s[...] = jnp.zeros_like(dv_s)
        for d in DIL_DILATIONS:

            def tile(start, prev, first, d=d):
                rows = _dil_rows(start, d, nq)
                q_t, do_t = q_ref[rows, :], do_ref[rows, :]
                lse_t, d_t = lse_ref[rows, :], d_s[rows, :]
                if prev is None:
                    kcat, vcat = k_ref[rows, :].astype(BF16), v_ref[rows, :].astype(BF16)
                    bias = bias_s[1, :, BLOCK:]
                else:
                    prows = _dil_rows(prev, d, BLOCK)
                    kcat = jnp.concatenate([k_ref[prows, :], k_ref[rows, :]], axis=0).astype(BF16)
                    vcat = jnp.concatenate([v_ref[prows, :], v_ref[rows, :]], axis=0).astype(BF16)
                    bias = bias_s[first]
                dq_t = jnp.zeros((nq, LANES), F32)
                dkcat = jnp.zeros((kcat.shape[0], LANES), F32)
                dvcat = jnp.zeros((kcat.shape[0], LANES), F32)
                for hh in range(2):
                    mine = (lane >= 64) if hh else (lane < 64)
                    c0 = 64 * hh
                    qh = jnp.where(mine, q_t, 0.0).astype(BF16)
                    doh = jnp.where(mine, do_t, 0.0).astype(BF16)
                    p = jnp.exp(_dot(qh, kcat, NT) + bias - lse_t[:, c0:c0 + 1])
                    dvcat = dvcat + _dot(p.astype(BF16), doh, TN)
                    dp = _dot(doh, vcat, NT)
                    ds = (p * (dp - d_t[:, c0:c0 + 1])).astype(BF16)
                    dq_t = dq_t + jnp.where(mine, _dot(ds, kcat), 0.0)
                    dkcat = dkcat + _dot(ds, qh, TN)
                dq_s[rows, :] += dq_t
                if prev is not None:
                    dk_s[prows, :] += dkcat[:BLOCK]
                    dv_s[prows, :] += dvcat[:BLOCK]
                dk_s[rows, :] += dkcat[-nq:]
                dv_s[rows, :] += dvcat[-nq:]

            if seq == 2 * nq * d:

                def class_tiles(r, carry, d=d):
                    tile(r, None, 1)
                    tile(r + nq * d, r, 0)
                    return carry

                lax.fori_loop(0, d, class_tiles, 0, unroll=8)
            else:

                def any_tile(t, carry, d=d):
                    first, start, prev = _dil_tile_index(t, d, seq, nq)
                    tile(start, prev, first)
                    return carry

                lax.fori_loop(0, n_tiles, any_tile, 0, unroll=16)
        dq_ref[...] = dq_s[...].astype(BF16)
        dk_ref[...] = dk_s[...].astype(BF16)
        dv_ref[...] = dv_s[...].astype(BF16)

    col = lambda off: pl.BlockSpec((seq, LANES), lambda p: (0, p + off))
    shape = jax.ShapeDtypeStruct((seq, 4 * LANES), BF16)
    return pl.pallas_call(
        body, name="dil_bwd", grid=(4,),
        in_specs=[col(0), col(0), col(0), col(4), col(0), pl.BlockSpec((None, seq, LANES), lambda p: (p, 0, 0))],
        out_specs=[col(0)] * 3, out_shape=[shape] * 3,
        scratch_shapes=[pltpu.VMEM((seq, LANES), F32)] * 4 + [pltpu.VMEM((2, nq, BLOCK + nq), F32)],
        compiler_params=_cp(("arbitrary",), VMEM_LIMIT),
    )(qr, kr, vb, d_o, o, lse)


def _in_bwd(dz, cq, ckv, gq, gkv, wuq_e, wukv, ct, st, dq, dk, dv, dgates, dqr, dkr, dvb, cd, sd, w_in_p):
    seq = dz.shape[0]
    tr = 512

    def body(dz_ref, cq_ref, ckv_ref, gq_ref, gkv_ref, wuq_ref, wukv_ref, ct_ref, st_ref, dq_ref, dk_ref, dv_ref,
             dg_ref, dqr_ref, dkr_ref, dvb_ref, cd_ref, sd_ref, w_ref,
             gx_ref, dh_ref, dwuq_ref, dwukv_ref, dgq_ref, dgkv_ref):
        @pl.when(pl.program_id(0) == 0)
        def _():
            dwuq_ref[...] = jnp.zeros_like(dwuq_ref)
            dwukv_ref[...] = jnp.zeros_like(dwukv_ref)
            dgq_ref[...] = jnp.zeros_like(dgq_ref)
            dgkv_ref[...] = jnp.zeros_like(dgkv_ref)

        lane = lax.broadcasted_iota(jnp.int32, (tr, LANES), 1)
        rope_lanes = jnp.logical_and(lane >= 64, lane < 96)
        ct_, st_ = ct_ref[...], st_ref[...]

        def mla_rope_t(g):
            return ct_ * g + jnp.where(rope_lanes, _mla_rot(st_ * g, lane), 0.0)

        def norm_bwd(c, g, dn, dg_ref):
            r, _ = _rms(c, g)
            u = dn * g
            dg_ref[...] += jnp.sum(dn * c * r, axis=0, keepdims=True)
            return r * u - c * (r * r * r) * jnp.mean(u * c, axis=-1, keepdims=True)

        c, g = cq_ref[...], gq_ref[...]
        _, qn = _rms(c, g)
        dq_all = jnp.concatenate([mla_rope_t(dq_ref[h] * MLA_SCALE) for h in range(MLA_HEADS)], axis=1).astype(BF16)
        dwuq_ref[...] += _dot(qn.astype(BF16), dq_all, TN)
        dcq = norm_bwd(c, g, _dot(dq_all, wuq_ref[...], NT), dgq_ref).astype(BF16)

        c, g = ckv_ref[...], gkv_ref[...]
        _, kvn = _rms(c, g)
        dkpe = jnp.zeros((tr, LANES), F32)
        parts = []
        for h in range(MLA_HEADS):
            dk_h, dv_h = dk_ref[h], dv_ref[h]
            if h % 2 == 0:
                dv_h = pltpu.roll(dv_h, 64, 1)
            parts.append(jnp.where(lane < 64, dk_h, dv_h))
            dkpe = dkpe + jnp.where(rope_lanes, dk_h, 0.0)
        dkv_all = jnp.concatenate(parts, axis=1).astype(BF16)
        dwukv_ref[...] += _dot(kvn.astype(BF16), dkv_all, TN)
        dckv = norm_bwd(c, g, _dot(dkv_all, wukv_ref[...], NT), dgkv_ref).astype(BF16)
        dkrope = mla_rope_t(dkpe).astype(BF16)

        rot_lanes = lane % 64 < DIL_ROT
        cd_, sd_ = cd_ref[...], sd_ref[...]

        def dil_rope_t(g):
            return cd_ * g + jnp.where(rot_lanes, _dil_rot(sd_ * g, lane), 0.0)

        dqb = [dil_rope_t(dqr_ref[:, LANES * p:LANES * (p + 1)].astype(F32) * DIL_SCALE).astype(BF16) for p in range(4)]
        dkb = [dil_rope_t(dkr_ref[:, LANES * p:LANES * (p + 1)].astype(F32)).astype(BF16) for p in range(4)]
        dh = jnp.concatenate([dcq, dckv, dg_ref[...]] + dqb + dkb + [dvb_ref[...], dkrope], axis=1)
        dh_ref[...] = dh
        gx_ref[...] = ALPHA * dz_ref[...] + _dot(dh, w_ref[...])

    row = lambda w: pl.BlockSpec((tr, w), lambda i: (i, 0))
    full = lambda a: pl.BlockSpec(a.shape, lambda i: (0,) * a.ndim)
    head = pl.BlockSpec((MLA_HEADS, tr, LANES), lambda i: (0, i, 0))
    return pl.pallas_call(
        body, name="in_bwd", grid=(seq // tr,),
        in_specs=[row(D_MODEL), row(Q_LORA), row(KV_LORA), full(gq), full(gkv), full(wuq_e), full(wukv), row(LANES),
                  row(LANES), head, head, head, row(D_MODEL), row(512), row(512), row(512), row(LANES), row(LANES),
                  full(w_in_p)],
        out_specs=[row(D_MODEL), row(IN_WIDTH_PAD), full(wuq_e), full(wukv), full(gq), full(gkv)],
        out_shape=[jax.ShapeDtypeStruct((seq, D_MODEL), F32), jax.ShapeDtypeStruct((seq, IN_WIDTH_PAD), BF16),
                   jax.ShapeDtypeStruct(wuq_e.shape, F32), jax.ShapeDtypeStruct(wukv.shape, F32),
                   jax.ShapeDtypeStruct(gq.shape, F32), jax.ShapeDtypeStruct(gkv.shape, F32)],
        compiler_params=_cp(("arbitrary",), VMEM_LIMIT),
    )(dz, cq, ckv, gq, gkv, wuq_e, wukv, ct, st, dq, dk, dv, dgates, dqr, dkr, dvb, cd, sd, w_in_p)


def _dw_in(x, dh, dw_out):
    seq = dh.shape[0]
    tk = 512
    tn = IN_WIDTH_PAD // 2
    a_out = 3
    n_steps = 2 * (seq // tk)

    def reduce_w_out(g_hbm, out_ref, stage, got, sums, others, total_s, send_sems, recv_sems, load_sem):
        x_, y_, c = _position()
        chips = [(1 - x_, y_), (x_, 1 - y_), (1 - x_, 1 - y_)]
        step = pl.program_id(0) * pl.num_programs(1) + pl.program_id(1)

        def remote(src, dst, sem, to):
            return pltpu.make_async_remote_copy(src_ref=src, dst_ref=dst, send_sem=send_sems.at[sem],
                                                recv_sem=recv_sems.at[sem], device_id=to, device_id_type=MESH)

        load = pltpu.make_async_copy(g_hbm, stage, load_sem.at[0])
        swap = remote(stage.at[(slice(None),) + _shard_half(a_out, 1 - c)], got, 0, (x_, y_, 1 - c))
        sends = [remote(sums.at[2 * px + py], others.at[k], 1 + k, (px, py, c)) for k, (px, py) in enumerate(chips)]
        mine, other = total_s.at[_shard_half(a_out, c)], total_s.at[_shard_half(a_out, 1 - c)]
        join = remote(mine, mine, 4, (x_, y_, 1 - c))

        @pl.when(step == 0)
        def _():
            load.start()

        @pl.when(step == 1)
        def _():
            load.wait()
            swap.start()

        @pl.when(step == 2)
        def _():
            swap.wait_recv()
            for k in range(N_SHARD):
                for in_half, in_whole in _shard_chunks(a_out, c):
                    pair = stage[(k,) + in_whole] + got[(k,) + in_half]
                    sums[(k,) + in_half] = pair.astype(BF16)
            for cp in sends:
                cp.start()

        @pl.when(step == n_steps - 1)
        def _():
            for cp in sends:
                cp.wait_recv()
            for in_half, in_whole in _shard_chunks(a_out, c):
                total = sums[(2 * x_ + y_,) + in_half].astype(F32)
                for k in range(3):
                    total = total + others[(k,) + in_half].astype(F32)
                total_s[in_whole] = total
            join.start()
            remote(other, other, 4, (x_, y_, c)).wait_recv()
            for cp in [swap] + sends + [join]:
                cp.wait_send()
            out_ref[...] = total_s[...]

    def body(x_ref, dh_ref, g_hbm, o_ref, g_ref, *scratch):
        reduce_w_out(g_hbm, g_ref, *scratch)

        @pl.when(pl.program_id(1) == 0)
        def _():
            o_ref[...] = jnp.zeros_like(o_ref)

        o_ref[...] += _dot(dh_ref[...], x_ref[...].astype(BF16), TN)

    shard, half = SHARD_SHAPES[a_out], _shard_half_shape(a_out)
    return pl.pallas_call(
        body, name="dw_in", grid=(2, seq // tk),
        in_specs=[pl.BlockSpec((tk, D_MODEL), lambda n, k: (k, 0)), pl.BlockSpec((tk, tn), lambda n, k: (k, n)),
                  pl.BlockSpec(memory_space=pl.ANY)],
        out_specs=[pl.BlockSpec((tn, D_MODEL), lambda n, k: (n, 0)), pl.BlockSpec(shard, lambda n, k: (0, 0))],
        out_shape=[jax.ShapeDtypeStruct((IN_WIDTH_PAD, D_MODEL), F32), jax.ShapeDtypeStruct(shard, F32)],
        scratch_shapes=[pltpu.VMEM((N_SHARD,) + shard, F32), pltpu.VMEM((N_SHARD,) + half, F32),
                        pltpu.VMEM((N_SHARD,) + half, BF16), pltpu.VMEM((3,) + half, BF16), pltpu.VMEM(shard, F32),
                        pltpu.SemaphoreType.DMA((5,)), pltpu.SemaphoreType.DMA((5,)), pltpu.SemaphoreType.DMA((1,))],
        compiler_params=_cp(("arbitrary", "arbitrary"), VMEM_LIMIT),
    )(x, dh, dw_out)


def _adam_update(w, g, m, v):
    nm = ADAM_B1 * m + (1.0 - ADAM_B1) * g
    nv = ADAM_B2 * v + (1.0 - ADAM_B2) * jnp.square(g)
    m_hat = nm / (1.0 - ADAM_B1 ** ADAM_STEP)
    v_hat = nv / (1.0 - ADAM_B2 ** ADAM_STEP)
    return -ADAM_LR * (m_hat / (jnp.sqrt(v_hat) + ADAM_EPS) + ADAM_WD * w), nm, nv


def _adamw(w, g, m, v, name):
    rows, cols = w.shape
    tc = 256 if cols % 256 == 0 and rows * cols > 2 ** 18 else cols

    def body(w_ref, g_ref, m_ref, v_ref, d_ref, nm_ref, nv_ref):
        d_ref[...], nm_ref[...], nv_ref[...] = _adam_update(w_ref[...], g_ref[...], m_ref[...], v_ref[...])

    spec = pl.BlockSpec((rows, tc), lambda i: (0, i))
    return pl.pallas_call(
        body, name=name, grid=(cols // tc,), in_specs=[spec] * 4, out_specs=[spec] * 3,
        out_shape=[jax.ShapeDtypeStruct(w.shape, F32)] * 3, compiler_params=_cp(("arbitrary",)),
    )(w, g, m, v)


def _adamw_vectors(small_sum, ws, ms, vs):
    k = len(ws)
    sizes = [w.shape[0] for w in ws]

    def body(s_ref, *refs):
        ins, outs = refs[:3 * k], refs[3 * k:]
        for i, size in enumerate(sizes):
            g = s_ref[i, 0:size]
            outs[i][...] = g
            outs[k + i][...], outs[2 * k + i][...], outs[3 * k + i][...] = _adam_update(
                ins[i][...], g, ins[k + i][...], ins[2 * k + i][...])

    out = pl.pallas_call(
        body, name="adamw_vectors", out_shape=[jax.ShapeDtypeStruct((size,), F32) for size in sizes] * 4,
    )(small_sum, *ws, *ms, *vs)
    return [out[k * j:k * (j + 1)] for j in range(4)]


def _local_step(x2, target, w_in_p, w_uq_f, wukv_f, w_out, q_norm_g, kv_norm_g, ln_g, ln_b):
    seq = x2.shape[0]
    wuq_e = jnp.pad(w_uq_f.reshape(Q_LORA, MLA_HEADS, 96), ((0, 0), (0, 0), (0, 32))).reshape(Q_LORA, MLA_HEADS * LANES)
    ct, st, cd, sd = _rope_tables(seq)
    gq = q_norm_g.reshape(1, Q_LORA)
    gkv = kv_norm_g.reshape(1, KV_LORA)

    cq, ckv, gates, qr, krot, vb, q_e, k_e, v_e, g_w_out = _proj(
        x2, w_in_p, gq, gkv, wuq_e, wukv_f, ct, st, cd, sd, w_out)
    w_out_f = g_w_out.reshape(D_MODEL, D_MODEL)
    o_a, lse_a = _mla_fwd(q_e, k_e, v_e)
    o_b, lse_b = _dil_fwd(qr, krot, vb)

    dz, d_o, d_gates, dw_out, dln_g, dln_b, loss_part = _post(
        x2, o_a, o_b, gates, w_out_f, ln_g.reshape(1, D_MODEL), ln_b.reshape(1, D_MODEL), target)
    dq_e, dk_e, dv_e = _mla_bwd(q_e, k_e, v_e, d_o, o_a, lse_a)
    dqr, dkr, dvb = _dil_bwd(qr, krot, vb, d_o, o_b, lse_b)
    grad_x, dh, dwuq_e, dwukv, dgq, dgkv = _in_bwd(
        dz, cq, ckv, gq, gkv, wuq_e, wukv_f, ct, st, dq_e, dk_e, dv_e, d_gates, dqr, dkr, dvb, cd, sd, w_in_p)
    dw_in, g_out = _dw_in(x2, dh, dw_out.reshape(N_SHARD, D_MODEL // N_SHARD, D_MODEL))
    dw_uq = dwuq_e.reshape(Q_LORA, MLA_HEADS, LANES)[:, :, :96].reshape(Q_LORA, MLA_HEADS * 96)
    return loss_part, grad_x, dw_in, dw_uq, dwukv, g_out, dgq, dgkv, dln_g, dln_b


def kernel(x, w_in, q_norm_g, kv_norm_g, w_uq, w_ukv, w_out, ln_g, ln_b, loss_target, m_w_in, m_q_norm_g, m_kv_norm_g, m_w_uq, m_w_ukv, m_w_out, m_ln_g, m_ln_b, v_w_in, v_q_norm_g, v_kv_norm_g, v_w_uq, v_w_ukv, v_w_out, v_ln_g, v_ln_b):
    seq = x.shape[1]
    x2 = x.reshape(seq, D_MODEL)
    target = loss_target.reshape(seq, D_MODEL)

    g_w_in, g_w_uq, g_w_ukv = _all_gather_weights([w_in.T, w_uq, w_ukv])
    by_cols = lambda g: jnp.concatenate([g[j] for j in range(N_SHARD)], axis=1)
    loss_part, grad_x, dw_in, dw_uq, dwukv, g_out, dgq, dgkv, dln_g, dln_b = _local_step(
        x2, target, g_w_in, by_cols(g_w_uq), by_cols(g_w_ukv), w_out, q_norm_g, kv_norm_g, ln_g, ln_b)

    to_shards = lambda d: d.reshape(d.shape[0], N_SHARD, d.shape[1] // N_SHARD).transpose(1, 0, 2)
    grads = [dw_in, to_shards(dw_uq), to_shards(dwukv)]
    g_in_t, g_uq, g_ukv, small_sum = _reduce_gradients(grads, [dgq, dgkv, dln_g, dln_b, loss_part])
    g_in = g_in_t.T
    loss = small_sum[4, 0]

    big = [[o.T for o in _adamw(w.T, g.T, m.T, v.T, name)] for w, g, m, v, name in (
        (w_in, g_in, m_w_in, v_w_in, "adamw_w_in"), (w_uq, g_uq, m_w_uq, v_w_uq, "adamw_w_uq"))]
    big += [_adamw(w, g, m, v, name) for w, g, m, v, name in (
        (w_ukv, g_ukv, m_w_ukv, v_w_ukv, "adamw_w_ukv"), (w_out, g_out, m_w_out, v_w_out, "adamw_w_out"))]
    vec_g, vec_delta, vec_m, vec_v = _adamw_vectors(
        small_sum, [q_norm_g, kv_norm_g, ln_g, ln_b], [m_q_norm_g, m_kv_norm_g, m_ln_g, m_ln_b],
        [v_q_norm_g, v_kv_norm_g, v_ln_g, v_ln_b])

    def ordered(bigs, vecs):
        return [bigs[0], vecs[0], vecs[1], bigs[1], bigs[2], bigs[3], vecs[2], vecs[3]]

    grads_out = ordered([g_in, g_uq, g_ukv, g_out], vec_g)
    deltas = ordered([b[0] for b in big], vec_delta)
    new_m = ordered([b[1] for b in big], vec_m)
    new_v = ordered([b[2] for b in big], vec_v)
    return (loss, grad_x.reshape(x.shape), *grads_out, *deltas, *new_m, *new_v)
```

```python
import jax
import jax.numpy as jnp
import numpy as np
from jax import lax
from jax.experimental import pallas as pl
from jax.experimental.pallas import tpu as pltpu

F32 = jnp.float32
BF16 = jnp.bfloat16

D_MODEL = 1024
ROPE_THETA = 500000.0
BLOCK = 128
NEG = -1e30
RMS_EPS = 1e-6
LN_EPS = 1e-5

MLA_HEADS = 8
MLA_NOPE = 64
MLA_ROPE = 32
Q_LORA = 384
KV_LORA = 256
DIL_HEAD_DIM = 64
DIL_ROT = 16
DIL_DILATIONS = (1, 4, 16)
IN_WIDTH_PAD = 3328
ONES_LANE = (64, 0)
MLA_SCALE = (MLA_NOPE + MLA_ROPE) ** -0.5
DIL_SCALE = DIL_HEAD_DIM ** -0.5
ALPHA = 2.0 ** 0.25

ADAM_LR = 0.001
ADAM_B1 = 0.9
ADAM_B2 = 0.999
ADAM_EPS = 1e-08
ADAM_WD = 0.01
ADAM_STEP = 10

N_SHARD = 4
SHARD_SHAPES = ((808, 1024), (384, 192), (256, 256), (256, 1024))
SHARD_SPLIT_COLS = (True, False, False, False)
ROW_CHUNK = 64
LANES = 128
VMEM_LIMIT = 56 * 1024 * 1024
MESH = pl.DeviceIdType.MESH

NT = (((1,), (1,)), ((), ()))
TN = (((0,), (0,)), ((), ()))


def _cp(sem=None, vmem=None):
    return pltpu.CompilerParams(dimension_semantics=sem, vmem_limit_bytes=vmem)


def _dot(a, b, dims=None):
    if dims is None:
        return jnp.dot(a, b, preferred_element_type=F32)
    return lax.dot_general(a, b, dims, preferred_element_type=F32)


def _rope_tables(seq):
    f32 = np.float32
    pos = np.arange(seq, dtype=f32)[:, None]
    one, zero = np.ones((seq, 64), f32), np.zeros((seq, 64), f32)

    def cos_sin(dim):
        inv = np.power(f32(ROPE_THETA), -np.arange(0, dim, 2, dtype=f32) / f32(dim)).astype(f32)
        ang = (pos * inv[None, :]).astype(f32)
        return np.cos(ang).astype(f32), np.sin(ang).astype(f32)

    cos, sin = cos_sin(MLA_ROPE)
    ct = np.concatenate([one, cos, cos, zero[:, :32]], axis=1)
    st = np.concatenate([zero, -sin, sin, zero[:, :32]], axis=1)
    cos, sin = cos_sin(DIL_ROT)
    cd = np.concatenate([cos, cos, one[:, :48]], axis=1)
    sd = np.concatenate([-sin, sin, zero[:, :48]], axis=1)
    return tuple(jnp.asarray(t) for t in (ct, st, np.tile(cd, (1, 2)), np.tile(sd, (1, 2))))


W_IN_ORDER = ((0, 640), (672, 1184), (2720, 3232), (1184, 2720), None, (640, 672))


def _w_in_row_pieces():
    width = SHARD_SHAPES[0][0]
    pieces, at = [], 0
    for r in W_IN_ORDER:
        if r is None:
            at += 64
            continue
        for k in range(N_SHARD):
            lo, hi = max(r[0], width * k), min(r[1], width * (k + 1))
            if lo < hi:
                pieces.append((k, lo - width * k, at + lo - r[0], hi - lo))
        at += r[1] - r[0]
    return pieces


def _position():
    return lax.axis_index("x"), lax.axis_index("y"), lax.axis_index("c")


def _all_gather_weights(shards):
    n = len(shards)

    def body(*refs):
        ins, outs = refs[:n], list(refs[n:2 * n])
        w_in_p, outs[0] = outs[0], refs[2 * n]
        send_sems, recv_sems = refs[2 * n + 1:]
        x, y, c = _position()
        me = 2 * x + y
        chips = [(1 - x, y), (x, 1 - y), (1 - x, 1 - y)]
        for a in range(n):
            for blk in _shard_blocks(a):
                outs[a][(me,) + blk] = ins[a][blk].astype(BF16)

        def copy(k, a, slot, part, to):
            ref = outs[a].at[(slot,) + part]
            return pltpu.make_async_remote_copy(
                src_ref=ref, dst_ref=ref, send_sem=send_sems.at[k * n + a], recv_sem=recv_sems.at[k * n + a],
                device_id=to, device_id_type=MESH)

        half = [_shard_half(a, c) for a in range(n)]
        other = [_shard_half(a, 1 - c) for a in range(n)]
        first = [copy(k, a, me, half[a], (px, py, c)) for k, (px, py) in enumerate(chips) for a in range(n)]
        for cp in first:
            cp.start()
        passed = []
        for k, (px, py) in enumerate(chips):
            for a in range(n):
                copy(k, a, 2 * px + py, half[a], (x, y, c)).wait_recv()
                cp = copy(3 + k, a, 2 * px + py, half[a], (x, y, 1 - c))
                cp.start()
                passed.append(cp)
        for k, (px, py) in enumerate(chips):
            for a in range(n):
                copy(3 + k, a, 2 * px + py, other[a], (x, y, c)).wait_recv()
        for cp in first + passed:
            cp.wait_send()

        written = []
        for k, r0, at, rows in _w_in_row_pieces():
            written.append((at, at + rows))
            for r in range(0, rows, 2 * LANES):
                m = min(2 * LANES, rows - r)
                for c0 in range(0, D_MODEL, LANES):
                    w_in_p[at + r:at + r + m, c0:c0 + LANES] = outs[0][k, r0 + r:r0 + r + m, c0:c0 + LANES]
        for lo, hi in zip([0] + [w[1] for w in sorted(written)], [w[0] for w in sorted(written)] + [IN_WIDTH_PAD]):
            if lo < hi:
                w_in_p[lo:hi, :] = jnp.zeros((hi - lo, D_MODEL), BF16)

    vmem = pl.BlockSpec(memory_space=pltpu.VMEM)
    return pl.pallas_call(
        body, name="all_gather_weights",
        out_shape=[jax.ShapeDtypeStruct((IN_WIDTH_PAD, D_MODEL), BF16)]
        + [jax.ShapeDtypeStruct((N_SHARD,) + s, BF16) for s in SHARD_SHAPES[1:n]],
        in_specs=[vmem] * n, out_specs=[vmem] * n,
        scratch_shapes=[pltpu.VMEM((N_SHARD,) + SHARD_SHAPES[0], BF16),
                        pltpu.SemaphoreType.DMA((6 * n,)), pltpu.SemaphoreType.DMA((6 * n,))],
        compiler_params=_cp(None, VMEM_LIMIT),
    )(*shards)


def _shard_blocks(a):
    rows, cols = SHARD_SHAPES[a]
    if SHARD_SPLIT_COLS[a]:
        return [(slice(None), slice(c0, c0 + LANES)) for c0 in range(0, cols, LANES)]
    return [(slice(r0, r0 + ROW_CHUNK), slice(None)) for r0 in range(0, rows, ROW_CHUNK)]


def _shard_half_shape(a):
    rows, cols = SHARD_SHAPES[a]
    return (rows, cols // 2) if SHARD_SPLIT_COLS[a] else (rows // 2, cols)


def _shard_half(a, c):
    rows, cols = SHARD_SHAPES[a]
    if SHARD_SPLIT_COLS[a]:
        return slice(None), pl.ds(pl.multiple_of(c * (cols // 2), LANES), cols // 2)
    return pl.ds(pl.multiple_of(c * (rows // 2), ROW_CHUNK), rows // 2), slice(None)


def _shard_chunks(a, c):
    rows, cols = SHARD_SHAPES[a]
    if SHARD_SPLIT_COLS[a]:
        return [((slice(None), pl.ds(c0, LANES)),
                 (slice(None), pl.ds(pl.multiple_of(c * (cols // 2) + c0, LANES), LANES)))
                for c0 in range(0, cols // 2, LANES)]
    return [((pl.ds(r0, ROW_CHUNK), slice(None)),
             (pl.ds(pl.multiple_of(c * (rows // 2) + r0, ROW_CHUNK), ROW_CHUNK), slice(None)))
            for r0 in range(0, rows // 2, ROW_CHUNK)]


def _reduce_gradients(grads, small_rows):
    n = len(grads)
    n_small = len(small_rows)
    pieces = _w_in_row_pieces()
    order = sorted(range(n), key=lambda a: SHARD_SHAPES[a][0] * SHARD_SHAPES[a][1])

    def body(*refs):
        g_hbm, rows_in = refs[:n], refs[n:n + n_small]
        outs, small_sum = refs[n + n_small:2 * n + n_small], refs[2 * n + n_small]
        scratch = refs[2 * n + n_small + 1:]
        stage, got, sums, others = (scratch[i * n:(i + 1) * n] for i in range(4))
        sm, smalls, send_sems, recv_sems, local_sems = scratch[4 * n:]
        swap_sem, chip_sem, join_sem, small_sem = 0, n, 4 * n, 5 * n
        x, y, c = _position()
        me = 4 * x + 2 * y + c
        chips = [(1 - x, y), (x, 1 - y), (1 - x, 1 - y)]
        sm[...] = jnp.zeros_like(sm)
        for i, row in enumerate(rows_in):
            sm[i:i + 1, 0:row.shape[1]] = row[...]
        loads = [[pltpu.make_async_copy(g_hbm[0].at[pl.ds(src, rows)], stage[0].at[k, pl.ds(dst, rows)],
                                        local_sems.at[n + i])
                  for i, (k, dst, src, rows) in enumerate(pieces)]]
        loads += [[pltpu.make_async_copy(g_hbm[a], stage[a], local_sems.at[a])] for a in range(1, n)]
        for a in order:
            for ld in loads[a]:
                ld.start()
        smalls[me] = sm[...]
        small_sends = []
        for rel in range(1, 8):
            px = 1 - x if rel // 4 else x
            py = 1 - y if (rel // 2) % 2 else y
            pc = 1 - c if rel % 2 else c
            cp = pltpu.make_async_remote_copy(
                src_ref=sm, dst_ref=smalls.at[me], send_sem=send_sems.at[small_sem + rel],
                recv_sem=recv_sems.at[small_sem + rel], device_id=(px, py, pc), device_id_type=MESH)
            cp.start()
            small_sends.append((cp, 4 * px + 2 * py + pc))
        swaps = {}
        for a in order:
            for ld in loads[a]:
                ld.wait()
            swaps[a] = pltpu.make_async_remote_copy(
                src_ref=stage[a].at[(slice(None),) + _shard_half(a, 1 - c)], dst_ref=got[a],
                send_sem=send_sems.at[swap_sem + a], recv_sem=recv_sems.at[swap_sem + a],
                device_id=(x, y, 1 - c), device_id_type=MESH)
            swaps[a].start()
        sends = {}
        for a in order:
            swaps[a].wait_recv()
            for k in range(N_SHARD):
                for in_half, in_whole in _shard_chunks(a, c):
                    pair = stage[a][(k,) + in_whole] + got[a][(k,) + in_half]
                    sums[a][(k,) + in_half] = pair.astype(BF16)
            sends[a] = [pltpu.make_async_remote_copy(
                src_ref=sums[a].at[2 * px + py], dst_ref=others[a].at[k], send_sem=send_sems.at[chip_sem + k * n + a],
                recv_sem=recv_sems.at[chip_sem + k * n + a], device_id=(px, py, c), device_id_type=MESH)
                for k, (px, py) in enumerate(chips)]
            for cp in sends[a]:
                cp.start()
        joins = []
        for a in order:
            for cp in sends[a]:
                cp.wait_recv()
            for in_half, in_whole in _shard_chunks(a, c):
                total = sums[a][(2 * x + y,) + in_half].astype(F32)
                for k in range(3):
                    total = total + others[a][(k,) + in_half].astype(F32)
                outs[a][in_whole] = total
            half = outs[a].at[_shard_half(a, c)]
            cp = pltpu.make_async_remote_copy(
                src_ref=half, dst_ref=half, send_sem=send_sems.at[join_sem + a],
                recv_sem=recv_sems.at[join_sem + a], device_id=(x, y, 1 - c), device_id_type=MESH)
            cp.start()
            joins.append(cp)
        for rel, (cp, peer) in enumerate(small_sends, start=1):
            pltpu.make_async_remote_copy(
                src_ref=sm, dst_ref=smalls.at[peer], send_sem=send_sems.at[small_sem + rel],
                recv_sem=recv_sems.at[small_sem + rel], device_id=(x, y, c), device_id_type=MESH).wait_recv()
        total = smalls[0]
        for dev in range(1, 8):
            total = total + smalls[dev]
        small_sum[...] = total
        for a in order:
            other = outs[a].at[_shard_half(a, 1 - c)]
            pltpu.make_async_remote_copy(
                src_ref=other, dst_ref=other, send_sem=send_sems.at[join_sem + a],
                recv_sem=recv_sems.at[join_sem + a], device_id=(x, y, c), device_id_type=MESH).wait_recv()
        for cp in list(swaps.values()) + [cp for a in order for cp in sends[a]] + joins + [cp for cp, _ in small_sends]:
            cp.wait_send()

    vmem = pl.BlockSpec(memory_space=pltpu.VMEM)
    halves = [_shard_half_shape(a) for a in range(n)]
    return pl.pallas_call(
        body, name="reduce_gradients",
        out_shape=[jax.ShapeDtypeStruct(s, F32) for s in SHARD_SHAPES] + [jax.ShapeDtypeStruct((8, D_MODEL), F32)],
        in_specs=[pl.BlockSpec(memory_space=pl.ANY)] * n + [vmem] * n_small, out_specs=[vmem] * (n + 1),
        scratch_shapes=[pltpu.VMEM((N_SHARD,) + s, F32) for s in SHARD_SHAPES]
        + [pltpu.VMEM((N_SHARD,) + s, F32) for s in halves] + [pltpu.VMEM((N_SHARD,) + s, BF16) for s in halves]
        + [pltpu.VMEM((3,) + s, BF16) for s in halves]
        + [pltpu.VMEM((8, D_MODEL), F32), pltpu.VMEM((8, 8, D_MODEL), F32),
           pltpu.SemaphoreType.DMA((5 * n + 8,)), pltpu.SemaphoreType.DMA((5 * n + 8,)),
           pltpu.SemaphoreType.DMA((n + len(pieces),))],
        compiler_params=_cp(None, VMEM_LIMIT),
    )(*grads, *small_rows)


def _proj(x, w_in_p, gq, gkv, wuq_e, wukv, ct, st, cd, sd, w_out):
    seq = x.shape[0]
    tr = 512
    a_out = 3

    def gather_w_out(w_out_ref, all_ref, land, send_sems, recv_sems):
        x_, y_, c = _position()
        me = 2 * x_ + y_
        chips = [(1 - x_, y_), (x_, 1 - y_), (1 - x_, 1 - y_)]
        half, other = _shard_half(a_out, c), _shard_half(a_out, 1 - c)

        def copy(k, slot, part, to):
            ref = land.at[(slot,) + part]
            return pltpu.make_async_remote_copy(
                src_ref=ref, dst_ref=ref, send_sem=send_sems.at[k], recv_sem=recv_sems.at[k],
                device_id=to, device_id_type=MESH)

        first = [copy(k, me, half, (px, py, c)) for k, (px, py) in enumerate(chips)]

        @pl.when(pl.program_id(0) == 0)
        def _():
            for blk in _shard_blocks(a_out):
                land[(me,) + blk] = w_out_ref[blk].astype(BF16)
            for cp in first:
                cp.start()

        @pl.when(pl.program_id(0) == pl.num_programs(0) - 1)
        def _():
            passed = []
            for k, (px, py) in enumerate(chips):
                copy(k, 2 * px + py, half, (x_, y_, c)).wait_recv()
                passed.append(copy(3 + k, 2 * px + py, half, (x_, y_, 1 - c)))
                passed[-1].start()
            for k, (px, py) in enumerate(chips):
                copy(3 + k, 2 * px + py, other, (x_, y_, c)).wait_recv()
            for cp in first + passed:
                cp.wait_send()
            all_ref[...] = land[...]

    def body(x_ref, w_ref, gq_ref, gkv_ref, wuq_ref, wukv_ref, ct_ref, st_ref, cd_ref, sd_ref, w_out_ref,
             cq_ref, ckv_ref, g_ref, qr_ref, kr_ref, vb_ref, q_out, k_out, v_out, w_out_all, land, send_sems, recv_sems):
        gather_w_out(w_out_ref, w_out_all, land, send_sems, recv_sems)
        lane = lax.broadcasted_iota(jnp.int32, (tr, LANES), 1)
        xb = x_ref[...].astype(BF16)
        cq = _dot(xb, w_ref[0:384, :], NT)
        ckv = _dot(xb, w_ref[384:640, :], NT)
        cq_ref[...] = cq
        ckv_ref[...] = ckv
        g_ref[...] = _dot(xb, w_ref[640:1664, :], NT)

        cd_, sd_ = cd_ref[...], sd_ref[...]
        qb = _dot(xb, w_ref[1664:2176, :], NT)
        kb = _dot(xb, w_ref[2176:2688, :], NT)
        for p in range(4):
            cols = slice(LANES * p, LANES * (p + 1))
            t = qb[:, cols]
            qr_ref[:, cols] = (t * cd_ + _dil_rot(t, lane) * sd_) * DIL_SCALE
            t = kb[:, cols]
            kr_ref[:, cols] = t * cd_ + _dil_rot(t, lane) * sd_
        vb_ref[...] = _dot(xb, w_ref[2688:3200, :], NT)

        ct_, st_ = ct_ref[...], st_ref[...]

        def rope(t):
            return t * ct_ + _mla_rot(t, lane) * st_

        _, qn = _rms(cq, gq_ref[...])
        q_all = _dot(qn.astype(BF16), wuq_ref[...])
        for h in range(MLA_HEADS):
            q_out[h] = (rope(q_all[:, LANES * h:LANES * (h + 1)]) * MLA_SCALE).astype(BF16)
        _, kvn = _rms(ckv, gkv_ref[...])
        kv_all = _dot(kvn.astype(BF16), wukv_ref[...])
        kpe = rope(_dot(xb, w_ref[3200:3328, :], NT))
        for h in range(MLA_HEADS):
            kv_h = kv_all[:, LANES * h:LANES * (h + 1)]
            k_out[h] = jnp.where(lane < 64, kv_h, kpe).astype(BF16)
            if h % 2:
                v = jnp.where(lane >= 64, kv_h, 0.0)
            else:
                v = jnp.where(lane < 64, pltpu.roll(kv_h, 64, 1), 0.0)
            v_out[h] = jnp.where(lane == ONES_LANE[h % 2], 1.0, v).astype(BF16)

    row = lambda w: pl.BlockSpec((tr, w), lambda i: (i, 0))
    full = lambda a: pl.BlockSpec(a.shape, lambda i: (0,) * a.ndim)
    head = pl.BlockSpec((MLA_HEADS, tr, LANES), lambda i: (0, i, 0))
    widths = (Q_LORA, KV_LORA, D_MODEL, 512, 512, 512)
    gathered = (N_SHARD,) + SHARD_SHAPES[a_out]
    return pl.pallas_call(
        body, name="proj", grid=(seq // tr,),
        in_specs=[row(D_MODEL), full(w_in_p), full(gq), full(gkv), full(wuq_e), full(wukv)] + [row(LANES)] * 4
        + [full(w_out)],
        out_specs=[row(w) for w in widths] + [head] * 3 + [pl.BlockSpec(gathered, lambda i: (0, 0, 0))],
        out_shape=[jax.ShapeDtypeStruct((seq, w), F32) for w in widths]
        + [jax.ShapeDtypeStruct((MLA_HEADS, seq, LANES), BF16)] * 3 + [jax.ShapeDtypeStruct(gathered, BF16)],
        scratch_shapes=[pltpu.VMEM(gathered, BF16), pltpu.SemaphoreType.DMA((6,)), pltpu.SemaphoreType.DMA((6,))],
        compiler_params=_cp(("arbitrary",), VMEM_LIMIT),
    )(x, w_in_p, gq, gkv, wuq_e, wukv, ct, st, cd, sd, w_out)


def _mla_rot(t, lane):
    return jnp.where(lane < 80, pltpu.roll(t, 112, 1), pltpu.roll(t, 16, 1))


def _dil_rot(t, lane):
    return jnp.where(lane % 64 < 8, pltpu.roll(t, 120, 1), pltpu.roll(t, 8, 1))


def _rms(c, g):
    r = lax.rsqrt(jnp.mean(c * c, axis=-1, keepdims=True) + RMS_EPS)
    return r, c * r * g


def _mla_fwd(q, k, v):
    seq = q.shape[1]
    tq = 512
    nq = seq // tq

    def body(q_ref, k_ref, v_ref, o_ref, lse_ref, m_s, acc_s, s_buf):
        i = pl.program_id(1)
        row = lax.broadcasted_iota(jnp.int32, (tq, tq), 0)
        col = lax.broadcasted_iota(jnp.int32, (tq, tq), 1)
        lane = lax.broadcasted_iota(jnp.int32, (tq, LANES), 1)
        m_s[...] = jnp.full((2, tq, LANES), NEG, F32)
        acc_s[...] = jnp.zeros((2, tq, LANES), F32)

        def block(j):
            return pl.ds(pl.multiple_of(j * tq, tq), tq)

        def scores(hh, j):
            return _dot(q_ref[hh], k_ref[hh, block(j), :], NT)

        def consume(hh, j, s):
            m_prev = m_s[hh]
            m_new = jnp.maximum(m_prev, jnp.max(s, axis=1, keepdims=True))
            p = jnp.exp(s - m_new[:, :1])
            acc_s[hh] = jnp.exp(m_prev - m_new) * acc_s[hh] + _dot(p.astype(BF16), v_ref[hh, block(j), :])
            m_s[hh] = m_new

        for hh in range(2):
            s_buf[0, hh] = scores(hh, 0)

        def full_step(j, carry):
            slot = j & 1
            for hh in range(2):
                s = s_buf[slot, hh]
                s_buf[1 - slot, hh] = scores(hh, j + 1)
                consume(hh, j, s)
            return carry

        lax.fori_loop(0, i, full_step, 0)
        total = jnp.zeros((tq, LANES), F32)
        for hh in range(2):
            consume(hh, i, jnp.where(col <= row, s_buf[i & 1, hh], NEG))
            acc = acc_s[hh]
            l = acc[:, ONES_LANE[hh]:ONES_LANE[hh] + 1]
            mine = (lane >= 64) if hh else (lane < 64)
            total = total + jnp.where(mine, acc / l, 0.0)
            lse_ref[hh] = m_s[hh] + jnp.log(l)
        o_ref[...] = total

    kv_spec = pl.BlockSpec((2, seq, LANES), lambda p, i: (p, 0, 0))
    return pl.pallas_call(
        body, name="mla_fwd", grid=(MLA_HEADS // 2, nq),
        in_specs=[pl.BlockSpec((2, tq, LANES), lambda p, i: (p, i, 0)), kv_spec, kv_spec],
        out_specs=[pl.BlockSpec((tq, LANES), lambda p, i: (i, p)), pl.BlockSpec((2, tq, LANES), lambda p, i: (p, i, 0))],
        out_shape=[jax.ShapeDtypeStruct((seq, 4 * LANES), F32), jax.ShapeDtypeStruct((MLA_HEADS, seq, LANES), F32)],
        scratch_shapes=[pltpu.VMEM((2, tq, LANES), F32), pltpu.VMEM((2, tq, LANES), F32),
                        pltpu.VMEM((2, 2, tq, tq), F32)],
        compiler_params=_cp(("arbitrary", "arbitrary"), VMEM_LIMIT),
    )(q, k, v)


DIL_Q_FWD = 2 * BLOCK
DIL_Q_BWD = BLOCK


def _dil_tile_index(t, d, seq, nq):
    per_class = seq // (nq * d)
    shift = per_class.bit_length() - 1
    r = t >> shift
    n = t & (per_class - 1)
    start = r + (nq * d) * n
    prev = jnp.maximum(start - BLOCK * d, r)
    if d == 1:
        start = pl.multiple_of(start, nq)
        prev = pl.multiple_of(prev, BLOCK)
    return (n == 0).astype(jnp.int32), start, prev


def _dil_rows(start, d, size):
    return pl.ds(start, size) if d == 1 else pl.ds(start, size, stride=d)


def _dil_bias(nq):
    i = lax.broadcasted_iota(jnp.int32, (2 * nq, BLOCK + nq), 0) % nq
    j = lax.broadcasted_iota(jnp.int32, (2 * nq, BLOCK + nq), 1)
    band = (j >= i) & (j <= i + BLOCK)
    return jnp.where(band, 0.0, NEG), jnp.where(band & (j >= BLOCK), 0.0, NEG)


def _stack_heads(t, lane):
    return jnp.concatenate([jnp.where(lane < 64, t, 0.0), jnp.where(lane >= 64, t, 0.0)], axis=0)


def _unstack_heads(t, lane):
    nq = t.shape[0] // 2
    return jnp.where(lane < 64, t[:nq], t[nq:])


def _dil_fwd(qr, kr, vb):
    seq = qr.shape[0]
    nq = DIL_Q_FWD
    n_tiles = seq // nq
    assert seq % (nq * max(DIL_DILATIONS)) == 0

    def body(q_ref, k_ref, v_ref, o_ref, lse_ref, m_s, l_s, n_s, bias_s):
        lane = lax.broadcasted_iota(jnp.int32, (nq, LANES), 1)
        bias_s[0], bias_s[1] = _dil_bias(nq)
        for bi, d in enumerate(DIL_DILATIONS):

            def tile(t, carry, d=d, bi=bi):
                first, start, prev = _dil_tile_index(t, d, seq, nq)
                rows, prows = _dil_rows(start, d, nq), _dil_rows(prev, d, BLOCK)
                qst = _stack_heads(q_ref[rows, :], lane).astype(BF16)
                if seq == nq * d:
                    kcat, vcat = k_ref[rows, :].astype(BF16), v_ref[rows, :].astype(BF16)
                    s = _dot(qst, kcat, NT) + bias_s[1, :, BLOCK:]
                else:
                    kcat = jnp.concatenate([k_ref[prows, :], k_ref[rows, :]], axis=0).astype(BF16)
                    vcat = jnp.concatenate([v_ref[prows, :], v_ref[rows, :]], axis=0).astype(BF16)
                    s = _dot(qst, kcat, NT) + bias_s[first]
                m = jnp.max(s, axis=1, keepdims=True)
                p = jnp.exp(s - m)
                l2 = _unstack_heads(jnp.sum(p, axis=1, keepdims=True) + jnp.zeros((2 * nq, LANES), F32), lane)
                m2 = _unstack_heads(m + jnp.zeros((2 * nq, LANES), F32), lane)
                num2 = _unstack_heads(_dot(p.astype(BF16), vcat), lane)
                if bi == 0:
                    m_s[rows, :] = m2
                    l_s[rows, :] = l2
                    n_s[rows, :] = num2
                else:
                    m_old = m_s[rows, :]
                    m_new = jnp.maximum(m_old, m2)
                    a = jnp.exp(m_old - m_new)
                    b = jnp.exp(m2 - m_new)
                    m_s[rows, :] = m_new
                    l_s[rows, :] = a * l_s[rows, :] + b * l2
                    n_s[rows, :] = a * n_s[rows, :] + b * num2
                return carry

            lax.fori_loop(0, n_tiles, tile, 0, unroll=8)
        o_ref[...] = n_s[...] / l_s[...]
        lse_ref[...] = m_s[...] + jnp.log(l_s[...])

    col = lambda off: pl.BlockSpec((seq, LANES), lambda p: (0, p + off))
    return pl.pallas_call(
        body, name="dil_fwd", grid=(4,),
        in_specs=[col(0), col(0), col(0)],
        out_specs=[col(0), pl.BlockSpec((None, seq, LANES), lambda p: (p, 0, 0))],
        out_shape=[jax.ShapeDtypeStruct((seq, 4 * LANES), F32), jax.ShapeDtypeStruct((4, seq, LANES), F32)],
        scratch_shapes=[pltpu.VMEM((seq, LANES), F32)] * 3 + [pltpu.VMEM((2, 2 * nq, BLOCK + nq), F32)],
        compiler_params=_cp(("arbitrary",), VMEM_LIMIT),
    )(qr, kr, vb)


def _post(x, o_a, o_b, gates, w_out, ln_g, ln_b, target):
    seq = x.shape[0]
    tr = 512

    def body(x_ref, oa_ref, ob_ref, g_ref, w_ref, lg_ref, lb_ref, t_ref,
             dz_ref, do_ref, dg_ref, dw_ref, dlg_ref, dlb_ref, loss_ref):
        @pl.when(pl.program_id(0) == 0)
        def _():
            dw_ref[...] = jnp.zeros_like(dw_ref)
            dlg_ref[...] = jnp.zeros_like(dlg_ref)
            dlb_ref[...] = jnp.zeros_like(dlb_ref)
            loss_ref[...] = jnp.zeros_like(loss_ref)

        g = g_ref[...]
        sg = jax.nn.sigmoid(g)
        silu = g * sg
        o = jnp.concatenate([oa_ref[...], ob_ref[...]], axis=1)
        mixb = (o * silu).astype(BF16)
        w = w_ref[...]
        z = ALPHA * x_ref[...] + _dot(mixb, w)
        mu = jnp.mean(z, axis=-1, keepdims=True)
        zc = z - mu
        rstd = lax.rsqrt(jnp.mean(zc * zc, axis=-1, keepdims=True) + LN_EPS)
        xhat = zc * rstd
        lg = lg_ref[...]
        err = xhat * lg + lb_ref[...] - t_ref[...]
        loss_ref[...] += jnp.sum(err * err) * (0.5 / D_MODEL)
        dy = err * (1.0 / D_MODEL)
        dlg_ref[...] += jnp.sum(dy * xhat, axis=0, keepdims=True)
        dlb_ref[...] += jnp.sum(dy, axis=0, keepdims=True)
        dxh = dy * lg
        dz = rstd * (dxh - jnp.mean(dxh, axis=-1, keepdims=True) - xhat * jnp.mean(dxh * xhat, axis=-1, keepdims=True))
        dz_ref[...] = dz
        dzb = dz.astype(BF16)
        dmix = _dot(dzb, w, NT)
        do_ref[...] = dmix * silu
        dg_ref[...] = (dmix * o * (sg * (1.0 + g * (1.0 - sg)))).astype(BF16)
        dw_ref[...] += _dot(mixb, dzb, TN)

    row = lambda w: pl.BlockSpec((tr, w), lambda i: (i, 0))
    full = lambda s: pl.BlockSpec(s, lambda i: (0, 0))
    return pl.pallas_call(
        body, name="post", grid=(seq // tr,),
        in_specs=[row(D_MODEL), row(512), row(512), row(D_MODEL), full((D_MODEL, D_MODEL)), full((1, D_MODEL)),
                  full((1, D_MODEL)), row(D_MODEL)],
        out_specs=[row(D_MODEL), row(D_MODEL), row(D_MODEL), full((D_MODEL, D_MODEL)), full((1, D_MODEL)),
                   full((1, D_MODEL)), full((1, LANES))],
        out_shape=[jax.ShapeDtypeStruct((seq, D_MODEL), F32), jax.ShapeDtypeStruct((seq, D_MODEL), F32),
                   jax.ShapeDtypeStruct((seq, D_MODEL), BF16), jax.ShapeDtypeStruct((D_MODEL, D_MODEL), F32),
                   jax.ShapeDtypeStruct((1, D_MODEL), F32), jax.ShapeDtypeStruct((1, D_MODEL), F32),
                   jax.ShapeDtypeStruct((1, LANES), F32)],
        compiler_params=_cp(("arbitrary",), VMEM_LIMIT),
    )(x, o_a, o_b, gates, w_out, ln_g, ln_b, target)


def _mla_bwd(q, k, v, d_o, o, lse):
    seq = q.shape[1]
    tq = 512
    nq = seq // tq

    def body(q_ref, k_ref, v_ref, do_ref, o_ref, lse_ref, dq_ref, dk_ref, dv_ref, d_s, lse_s, dk_s, dv_s, v_s, kt_s, dqt_s):
        j = nq - 1 - pl.program_id(1)
        lane = lax.broadcasted_iota(jnp.int32, (tq, LANES), 1)
        row = lax.broadcasted_iota(jnp.int32, (tq, tq), 0)
        col = lax.broadcasted_iota(jnp.int32, (tq, tq), 1)

        @pl.when(pl.program_id(1) == 0)
        def _():
            dqt_s[...] = jnp.zeros_like(dqt_s)

            def rowsum(i, carry):
                rows = pl.ds(pl.multiple_of(i * tq, tq), tq)
                prod = do_ref[rows, :] * o_ref[rows, :]
                for hh in range(2):
                    mine = (lane >= 64) if hh else (lane < 64)
                    total = jnp.sum(jnp.where(mine, prod, 0.0), axis=1, keepdims=True)
                    d_s[hh, i] = jnp.transpose(total + jnp.zeros((tq, LANES), F32))[:8]
                    lse_s[hh, i] = jnp.transpose(lse_ref[hh, rows, :])[:8]
                return carry

            lax.fori_loop(0, nq, rowsum, 0)

        dk_s[...] = jnp.zeros_like(dk_s)
        dv_s[...] = jnp.zeros_like(dv_s)
        for hh in range(2):
            v_s[hh] = jnp.where(lane == ONES_LANE[hh], 0.0, v_ref[hh].astype(F32)).astype(BF16)
            kt_s[hh] = jnp.transpose(k_ref[hh].astype(F32)).astype(BF16)

        def step(i, masked):
            rows = pl.ds(pl.multiple_of(i * tq, tq), tq)
            dob = do_ref[rows, :].astype(BF16)
            for hh in range(2):
                qb, kb, vb = q_ref[hh, rows, :], k_ref[hh], v_s[hh]
                p = jnp.exp(_dot(kb, qb, NT) - lse_s[hh, i][:1])
                if masked:
                    p = jnp.where(row <= col, p, 0.0)
                dv_s[hh] += _dot(p.astype(BF16), dob)
                ds = (p * (_dot(vb, dob, NT) - d_s[hh, i][:1])).astype(BF16)
                dk_s[hh] += _dot(ds, qb)
                dqt_s[hh, i] += _dot(kt_s[hh], ds)

        def full_step(i, carry):
            step(i, False)
            return carry

        step(j, True)
        lax.fori_loop(j + 1, nq, full_step, 0)
        dk_ref[...] = dk_s[...]
        dv_ref[...] = dv_s[...]

        @pl.when(pl.program_id(1) == nq - 1)
        def _():
            def untranspose(i, carry):
                rows = pl.ds(pl.multiple_of(i * tq, tq), tq)
                for hh in range(2):
                    dq_ref[hh, rows, :] = jnp.transpose(dqt_s[hh, i])
                return carry

            lax.fori_loop(0, nq, untranspose, 0)

    whole = pl.BlockSpec((2, seq, LANES), lambda p, j: (p, 0, 0))
    blk = pl.BlockSpec((2, tq, LANES), lambda p, j: (p, nq - 1 - j, 0))
    pair = pl.BlockSpec((seq, LANES), lambda p, j: (0, p))
    shape = jax.ShapeDtypeStruct((MLA_HEADS, seq, LANES), F32)
    return pl.pallas_call(
        body, name="mla_bwd", grid=(MLA_HEADS // 2, nq),
        in_specs=[whole, blk, blk, pair, pair, whole],
        out_specs=[whole, blk, blk], out_shape=[shape] * 3,
        scratch_shapes=[pltpu.VMEM((2, nq, 8, tq), F32), pltpu.VMEM((2, nq, 8, tq), F32),
                        pltpu.VMEM((2, tq, LANES), F32), pltpu.VMEM((2, tq, LANES), F32),
                        pltpu.VMEM((2, tq, LANES), BF16), pltpu.VMEM((2, LANES, tq), BF16),
                        pltpu.VMEM((2, nq, LANES, tq), F32)],
        compiler_params=_cp(("arbitrary", "arbitrary"), VMEM_LIMIT),
    )(q, k, v, d_o, o, lse)


def _dil_bwd(qr, kr, vb, d_o, o, lse):
    seq = qr.shape[0]
    nq = DIL_Q_BWD
    n_tiles = seq // nq
    chunk = 512

    def body(q_ref, k_ref, v_ref, do_ref, o_ref, lse_ref, dq_ref, dk_ref, dv_ref, d_s, dq_s, dk_s, dv_s, bias_s):
        lane = lax.broadcasted_iota(jnp.int32, (nq, LANES), 1)
        lanec = lax.broadcasted_iota(jnp.int32, (chunk, LANES), 1)
        bias_s[0], bias_s[1] = [b[:nq] for b in _dil_bias(nq)]

        def rowsum(i, carry):
            rows = pl.ds(pl.multiple_of(i * chunk, chunk), chunk)
            prod = do_ref[rows, :] * o_ref[rows, :]
            lo = jnp.sum(jnp.where(lanec < 64, prod, 0.0), axis=1, keepdims=True)
            hi = jnp.sum(jnp.where(lanec >= 64, prod, 0.0), axis=1, keepdims=True)
            d_s[rows, :] = jnp.where(lanec < 64, lo, hi)
            return carry

        lax.fori_loop(0, seq // chunk, rowsum, 0)
        dq_s[...] = jnp.zeros_like(dq_s)
        dk_s[...] = jnp.zeros_like(dk_s)
        dv_s[...] = jnp.zeros_like(dv_s)
        for d in DIL_DILATIONS:

            def tile(start, prev, first, d=d):
                rows = _dil_rows(start, d, nq)
                q_t, do_t = q_ref[rows, :], do_ref[rows, :]
                lse_t, d_t = lse_ref[rows, :], d_s[rows, :]
                if prev is None:
                    kcat, vcat = k_ref[rows, :].astype(BF16), v_ref[rows, :].astype(BF16)
                    bias = bias_s[1, :, BLOCK:]
                else:
                    prows = _dil_rows(prev, d, BLOCK)
                    kcat = jnp.concatenate([k_ref[prows, :], k_ref[rows, :]], axis=0).astype(BF16)
                    vcat = jnp.concatenate([v_ref[prows, :], v_ref[rows, :]], axis=0).astype(BF16)
                    bias = bias_s[first]
                dq_t = jnp.zeros((nq, LANES), F32)
                dkcat = jnp.zeros((kcat.shape[0], LANES), F32)
                dvcat = jnp.zeros((kcat.shape[0], LANES), F32)
                for hh in range(2):
                    mine = (lane >= 64) if hh else (lane < 64)
                    c0 = 64 * hh
                    qh = jnp.where(mine, q_t, 0.0).astype(BF16)
                    doh = jnp.where(mine, do_t, 0.0).astype(BF16)
                    p = jnp.exp(_dot(qh, kcat, NT) + bias - lse_t[:, c0:c0 + 1])
                    dvcat = dvcat + _dot(p.astype(BF16), doh, TN)
                    dp = _dot(doh, vcat, NT)
                    ds = (p * (dp - d_t[:, c0:c0 + 1])).astype(BF16)
                    dq_t = dq_t + jnp.where(mine, _dot(ds, kcat), 0.0)
                    dkcat = dkcat + _dot(ds, qh, TN)
                dq_s[rows, :] += dq_t
                if prev is not None:
                    dk_s[prows, :] += dkcat[:BLOCK]
                    dv_s[prows, :] += dvcat[:BLOCK]
                dk_s[rows, :] += dkcat[-nq:]
                dv_s[rows, :] += dvcat[-nq:]

            if seq == 2 * nq * d:

                def class_tiles(r, carry, d=d):
                    tile(r, None, 1)
                    tile(r + nq * d, r, 0)
                    return carry

                lax.fori_loop(0, d, class_tiles, 0, unroll=8)
            else:

                def any_tile(t, carry, d=d):
                    first, start, prev = _dil_tile_index(t, d, seq, nq)
                    tile(start, prev, first)
                    return carry

                lax.fori_loop(0, n_tiles, any_tile, 0, unroll=16)
        dq_ref[...] = dq_s[...].astype(BF16)
        dk_ref[...] = dk_s[...].astype(BF16)
        dv_ref[...] = dv_s[...].astype(BF16)

    col = lambda off: pl.BlockSpec((seq, LANES), lambda p: (0, p + off))
    shape = jax.ShapeDtypeStruct((seq, 4 * LANES), BF16)
    return pl.pallas_call(
        body, name="dil_bwd", grid=(4,),
        in_specs=[col(0), col(0), col(0), col(4), col(0), pl.BlockSpec((None, seq, LANES), lambda p: (p, 0, 0))],
        out_specs=[col(0)] * 3, out_shape=[shape] * 3,
        scratch_shapes=[pltpu.VMEM((seq, LANES), F32)] * 4 + [pltpu.VMEM((2, nq, BLOCK + nq), F32)],
        compiler_params=_cp(("arbitrary",), VMEM_LIMIT),
    )(qr, kr, vb, d_o, o, lse)


def _in_bwd(dz, cq, ckv, gq, gkv, wuq_e, wukv, ct, st, dq, dk, dv, dgates, dqr, dkr, dvb, cd, sd, w_in_p):
    seq = dz.shape[0]
    tr = 512

    def body(dz_ref, cq_ref, ckv_ref, gq_ref, gkv_ref, wuq_ref, wukv_ref, ct_ref, st_ref, dq_ref, dk_ref, dv_ref,
             dg_ref, dqr_ref, dkr_ref, dvb_ref, cd_ref, sd_ref, w_ref,
             gx_ref, dh_ref, dwuq_ref, dwukv_ref, dgq_ref, dgkv_ref):
        @pl.when(pl.program_id(0) == 0)
        def _():
            dwuq_ref[...] = jnp.zeros_like(dwuq_ref)
            dwukv_ref[...] = jnp.zeros_like(dwukv_ref)
            dgq_ref[...] = jnp.zeros_like(dgq_ref)
            dgkv_ref[...] = jnp.zeros_like(dgkv_ref)

        lane = lax.broadcasted_iota(jnp.int32, (tr, LANES), 1)
        rope_lanes = jnp.logical_and(lane >= 64, lane < 96)
        ct_, st_ = ct_ref[...], st_ref[...]

        def mla_rope_t(g):
            return ct_ * g + jnp.where(rope_lanes, _mla_rot(st_ * g, lane), 0.0)

        def norm_bwd(c, g, dn, dg_ref):
            r, _ = _rms(c, g)
            u = dn * g
            dg_ref[...] += jnp.sum(dn * c * r, axis=0, keepdims=True)
            return r * u - c * (r * r * r) * jnp.mean(u * c, axis=-1, keepdims=True)

        c, g = cq_ref[...], gq_ref[...]
        _, qn = _rms(c, g)
        dq_all = jnp.concatenate([mla_rope_t(dq_ref[h] * MLA_SCALE) for h in range(MLA_HEADS)], axis=1).astype(BF16)
        dwuq_ref[...] += _dot(qn.astype(BF16), dq_all, TN)
        dcq = norm_bwd(c, g, _dot(dq_all, wuq_ref[...], NT), dgq_ref).astype(BF16)

        c, g = ckv_ref[...], gkv_ref[...]
        _, kvn = _rms(c, g)
        dkpe = jnp.zeros((tr, LANES), F32)
        parts = []
        for h in range(MLA_HEADS):
            dk_h, dv_h = dk_ref[h], dv_ref[h]
            if h % 2 == 0:
                dv_h = pltpu.roll(dv_h, 64, 1)
            parts.append(jnp.where(lane < 64, dk_h, dv_h))
            dkpe = dkpe + jnp.where(rope_lanes, dk_h, 0.0)
        dkv_all = jnp.concatenate(parts, axis=1).astype(BF16)
        dwukv_ref[...] += _dot(kvn.astype(BF16), dkv_all, TN)
        dckv = norm_bwd(c, g, _dot(dkv_all, wukv_ref[...], NT), dgkv_ref).astype(BF16)
        dkrope = mla_rope_t(dkpe).astype(BF16)

        rot_lanes = lane % 64 < DIL_ROT
        cd_, sd_ = cd_ref[...], sd_ref[...]

        def dil_rope_t(g):
            return cd_ * g + jnp.where(rot_lanes, _dil_rot(sd_ * g, lane), 0.0)

        dqb = [dil_rope_t(dqr_ref[:, LANES * p:LANES * (p + 1)].astype(F32) * DIL_SCALE).astype(BF16) for p in range(4)]
        dkb = [dil_rope_t(dkr_ref[:, LANES * p:LANES * (p + 1)].astype(F32)).astype(BF16) for p in range(4)]
        dh = jnp.concatenate([dcq, dckv, dg_ref[...]] + dqb + dkb + [dvb_ref[...], dkrope], axis=1)
        dh_ref[...] = dh
        gx_ref[...] = ALPHA * dz_ref[...] + _dot(dh, w_ref[...])

    row = lambda w: pl.BlockSpec((tr, w), lambda i: (i, 0))
    full = lambda a: pl.BlockSpec(a.shape, lambda i: (0,) * a.ndim)
    head = pl.BlockSpec((MLA_HEADS, tr, LANES), lambda i: (0, i, 0))
    return pl.pallas_call(
        body, name="in_bwd", grid=(seq // tr,),
        in_specs=[row(D_MODEL), row(Q_LORA), row(KV_LORA), full(gq), full(gkv), full(wuq_e), full(wukv), row(LANES),
                  row(LANES), head, head, head, row(D_MODEL), row(512), row(512), row(512), row(LANES), row(LANES),
                  full(w_in_p)],
        out_specs=[row(D_MODEL), row(IN_WIDTH_PAD), full(wuq_e), full(wukv), full(gq), full(gkv)],
        out_shape=[jax.ShapeDtypeStruct((seq, D_MODEL), F32), jax.ShapeDtypeStruct((seq, IN_WIDTH_PAD), BF16),
                   jax.ShapeDtypeStruct(wuq_e.shape, F32), jax.ShapeDtypeStruct(wukv.shape, F32),
                   jax.ShapeDtypeStruct(gq.shape, F32), jax.ShapeDtypeStruct(gkv.shape, F32)],
        compiler_params=_cp(("arbitrary",), VMEM_LIMIT),
    )(dz, cq, ckv, gq, gkv, wuq_e, wukv, ct, st, dq, dk, dv, dgates, dqr, dkr, dvb, cd, sd, w_in_p)


def _dw_in(x, dh):
    seq = dh.shape[0]
    tk = 512
    tn = IN_WIDTH_PAD // 2

    def body(x_ref, dh_ref, o_ref):
        @pl.when(pl.program_id(1) == 0)
        def _():
            o_ref[...] = jnp.zeros_like(o_ref)

        o_ref[...] += _dot(dh_ref[...], x_ref[...].astype(BF16), TN)

    return pl.pallas_call(
        body, name="dw_in", grid=(2, seq // tk),
        in_specs=[pl.BlockSpec((tk, D_MODEL), lambda n, k: (k, 0)), pl.BlockSpec((tk, tn), lambda n, k: (k, n))],
        out_specs=pl.BlockSpec((tn, D_MODEL), lambda n, k: (n, 0)),
        out_shape=jax.ShapeDtypeStruct((IN_WIDTH_PAD, D_MODEL), F32),
        compiler_params=_cp(("arbitrary", "arbitrary"), VMEM_LIMIT),
    )(x, dh)


def _adam_update(w, g, m, v):
    nm = ADAM_B1 * m + (1.0 - ADAM_B1) * g
    nv = ADAM_B2 * v + (1.0 - ADAM_B2) * jnp.square(g)
    m_hat = nm / (1.0 - ADAM_B1 ** ADAM_STEP)
    v_hat = nv / (1.0 - ADAM_B2 ** ADAM_STEP)
    return -ADAM_LR * (m_hat / (jnp.sqrt(v_hat) + ADAM_EPS) + ADAM_WD * w), nm, nv


def _adamw(w, g, m, v, name):
    rows, cols = w.shape
    tc = 256 if cols % 256 == 0 and rows * cols > 2 ** 18 else cols

    def body(w_ref, g_ref, m_ref, v_ref, d_ref, nm_ref, nv_ref):
        d_ref[...], nm_ref[...], nv_ref[...] = _adam_update(w_ref[...], g_ref[...], m_ref[...], v_ref[...])

    spec = pl.BlockSpec((rows, tc), lambda i: (0, i))
    return pl.pallas_call(
        body, name=name, grid=(cols // tc,), in_specs=[spec] * 4, out_specs=[spec] * 3,
        out_shape=[jax.ShapeDtypeStruct(w.shape, F32)] * 3, compiler_params=_cp(("arbitrary",)),
    )(w, g, m, v)


def _adamw_vectors(small_sum, ws, ms, vs):
    k = len(ws)
    sizes = [w.shape[0] for w in ws]

    def body(s_ref, *refs):
        ins, outs = refs[:3 * k], refs[3 * k:]
        for i, size in enumerate(sizes):
            g = s_ref[i, 0:size]
            outs[i][...] = g
            outs[k + i][...], outs[2 * k + i][...], outs[3 * k + i][...] = _adam_update(
                ins[i][...], g, ins[k + i][...], ins[2 * k + i][...])

    out = pl.pallas_call(
        body, name="adamw_vectors", out_shape=[jax.ShapeDtypeStruct((size,), F32) for size in sizes] * 4,
    )(small_sum, *ws, *ms, *vs)
    return [out[k * j:k * (j + 1)] for j in range(4)]


def _local_step(x2, target, w_in_p, w_uq_f, wukv_f, w_out, q_norm_g, kv_norm_g, ln_g, ln_b):
    seq = x2.shape[0]
    wuq_e = jnp.pad(w_uq_f.reshape(Q_LORA, MLA_HEADS, 96), ((0, 0), (0, 0), (0, 32))).reshape(Q_LORA, MLA_HEADS * LANES)
    ct, st, cd, sd = _rope_tables(seq)
    gq = q_norm_g.reshape(1, Q_LORA)
    gkv = kv_norm_g.reshape(1, KV_LORA)

    cq, ckv, gates, qr, krot, vb, q_e, k_e, v_e, g_w_out = _proj(
        x2, w_in_p, gq, gkv, wuq_e, wukv_f, ct, st, cd, sd, w_out)
    w_out_f = g_w_out.reshape(D_MODEL, D_MODEL)
    o_a, lse_a = _mla_fwd(q_e, k_e, v_e)
    o_b, lse_b = _dil_fwd(qr, krot, vb)

    dz, d_o, d_gates, dw_out, dln_g, dln_b, loss_part = _post(
        x2, o_a, o_b, gates, w_out_f, ln_g.reshape(1, D_MODEL), ln_b.reshape(1, D_MODEL), target)
    dq_e, dk_e, dv_e = _mla_bwd(q_e, k_e, v_e, d_o, o_a, lse_a)
    dqr, dkr, dvb = _dil_bwd(qr, krot, vb, d_o, o_b, lse_b)
    grad_x, dh, dwuq_e, dwukv, dgq, dgkv = _in_bwd(
        dz, cq, ckv, gq, gkv, wuq_e, wukv_f, ct, st, dq_e, dk_e, dv_e, d_gates, dqr, dkr, dvb, cd, sd, w_in_p)
    dw_in = _dw_in(x2, dh)
    dw_uq = dwuq_e.reshape(Q_LORA, MLA_HEADS, LANES)[:, :, :96].reshape(Q_LORA, MLA_HEADS * 96)
    return loss_part, grad_x, dw_in, dw_uq, dwukv, dw_out, dgq, dgkv, dln_g, dln_b


def kernel(x, w_in, q_norm_g, kv_norm_g, w_uq, w_ukv, w_out, ln_g, ln_b, loss_target, m_w_in, m_q_norm_g, m_kv_norm_g, m_w_uq, m_w_ukv, m_w_out, m_ln_g, m_ln_b, v_w_in, v_q_norm_g, v_kv_norm_g, v_w_uq, v_w_ukv, v_w_out, v_ln_g, v_ln_b):
    seq = x.shape[1]
    x2 = x.reshape(seq, D_MODEL)
    target = loss_target.reshape(seq, D_MODEL)

    g_w_in, g_w_uq, g_w_ukv = _all_gather_weights([w_in.T, w_uq, w_ukv])
    by_cols = lambda g: jnp.concatenate([g[j] for j in range(N_SHARD)], axis=1)
    loss_part, grad_x, dw_in, dw_uq, dwukv, dw_out, dgq, dgkv, dln_g, dln_b = _local_step(
        x2, target, g_w_in, by_cols(g_w_uq), by_cols(g_w_ukv), w_out, q_norm_g, kv_norm_g, ln_g, ln_b)

    to_shards = lambda d: d.reshape(d.shape[0], N_SHARD, d.shape[1] // N_SHARD).transpose(1, 0, 2)
    grads = [dw_in, to_shards(dw_uq), to_shards(dwukv), dw_out.reshape(N_SHARD, 256, D_MODEL)]
    g_in_t, g_uq, g_ukv, g_out, small_sum = _reduce_gradients(grads, [dgq, dgkv, dln_g, dln_b, loss_part])
    g_in = g_in_t.T
    loss = small_sum[4, 0]

    big = [[o.T for o in _adamw(w.T, g.T, m.T, v.T, name)] for w, g, m, v, name in (
        (w_in, g_in, m_w_in, v_w_in, "adamw_w_in"), (w_uq, g_uq, m_w_uq, v_w_uq, "adamw_w_uq"))]
    big += [_adamw(w, g, m, v, name) for w, g, m, v, name in (
        (w_ukv, g_ukv, m_w_ukv, v_w_ukv, "adamw_w_ukv"), (w_out, g_out, m_w_out, v_w_out, "adamw_w_out"))]
    vec_g, vec_delta, vec_m, vec_v = _adamw_vectors(
        small_sum, [q_norm_g, kv_norm_g, ln_g, ln_b], [m_q_norm_g, m_kv_norm_g, m_ln_g, m_ln_b],
        [v_q_norm_g, v_kv_norm_g, v_ln_g, v_ln_b])

    def ordered(bigs, vecs):
        return [bigs[0], vecs[0], vecs[1], bigs[1], bigs[2], bigs[3], vecs[2], vecs[3]]

    grads_out = ordered([g_in, g_uq, g_ukv, g_out], vec_g)
    deltas = ordered([b[0] for b in big], vec_delta)
    new_m = ordered([b[1] for b in big], vec_m)
    new_v = ordered([b[2] for b in big], vec_v)
    return (loss, grad_x.reshape(x.shape), *grads_out, *deltas, *new_m, *new_v)
```

```python
import jax
import jax.numpy as jnp
import numpy as np
from jax import lax
from jax.experimental import pallas as pl
from jax.experimental.pallas import tpu as pltpu

F32 = jnp.float32
BF16 = jnp.bfloat16

D_MODEL = 1024
ROPE_THETA = 500000.0
BLOCK = 128
NEG = -1e30
RMS_EPS = 1e-6
LN_EPS = 1e-5

MLA_HEADS = 8
MLA_NOPE = 64
MLA_ROPE = 32
Q_LORA = 384
KV_LORA = 256
DIL_HEAD_DIM = 64
DIL_ROT = 16
DIL_DILATIONS = (1, 4, 16)
IN_WIDTH_PAD = 3328
ONES_LANE = (64, 0)
MLA_SCALE = (MLA_NOPE + MLA_ROPE) ** -0.5
DIL_SCALE = DIL_HEAD_DIM ** -0.5
ALPHA = 2.0 ** 0.25

ADAM_LR = 0.001
ADAM_B1 = 0.9
ADAM_B2 = 0.999
ADAM_EPS = 1e-08
ADAM_WD = 0.01
ADAM_STEP = 10

N_SHARD = 4
SHARD_SHAPES = ((808, 1024), (384, 192), (256, 256), (256, 1024))
SHARD_SPLIT_COLS = (True, False, False, False)
ROW_CHUNK = 64
LANES = 128
VMEM_LIMIT = 56 * 1024 * 1024
MESH = pl.DeviceIdType.MESH

NT = (((1,), (1,)), ((), ()))
TN = (((0,), (0,)), ((), ()))


def _cp(sem=None, vmem=None):
    return pltpu.CompilerParams(dimension_semantics=sem, vmem_limit_bytes=vmem)


def _dot(a, b, dims=None):
    if dims is None:
        return jnp.dot(a, b, preferred_element_type=F32)
    return lax.dot_general(a, b, dims, preferred_element_type=F32)


def _rope_tables(seq):
    f32 = np.float32
    pos = np.arange(seq, dtype=f32)[:, None]
    one, zero = np.ones((seq, 64), f32), np.zeros((seq, 64), f32)

    def cos_sin(dim):
        inv = np.power(f32(ROPE_THETA), -np.arange(0, dim, 2, dtype=f32) / f32(dim)).astype(f32)
        ang = (pos * inv[None, :]).astype(f32)
        return np.cos(ang).astype(f32), np.sin(ang).astype(f32)

    cos, sin = cos_sin(MLA_ROPE)
    ct = np.concatenate([one, cos, cos, zero[:, :32]], axis=1)
    st = np.concatenate([zero, -sin, sin, zero[:, :32]], axis=1)
    cos, sin = cos_sin(DIL_ROT)
    cd = np.concatenate([cos, cos, one[:, :48]], axis=1)
    sd = np.concatenate([-sin, sin, zero[:, :48]], axis=1)
    return tuple(jnp.asarray(t) for t in (ct, st, np.tile(cd, (1, 2)), np.tile(sd, (1, 2))))


W_IN_ORDER = ((0, 640), (672, 1184), (2720, 3232), (1184, 2720), None, (640, 672))


def _w_in_row_pieces():
    width = SHARD_SHAPES[0][0]
    pieces, at = [], 0
    for r in W_IN_ORDER:
        if r is None:
            at += 64
            continue
        for k in range(N_SHARD):
            lo, hi = max(r[0], width * k), min(r[1], width * (k + 1))
            if lo < hi:
                pieces.append((k, lo - width * k, at + lo - r[0], hi - lo))
        at += r[1] - r[0]
    return pieces


def _position():
    return lax.axis_index("x"), lax.axis_index("y"), lax.axis_index("c")


def _all_gather_weights(shards):
    n = len(shards)

    def body(*refs):
        ins, outs = refs[:n], list(refs[n:2 * n])
        w_in_p, outs[0] = outs[0], refs[2 * n]
        send_sems, recv_sems = refs[2 * n + 1:]
        x, y, c = _position()
        me = 2 * x + y
        chips = [(1 - x, y), (x, 1 - y), (1 - x, 1 - y)]
        for a in range(n):
            for blk in _shard_blocks(a):
                outs[a][(me,) + blk] = ins[a][blk].astype(BF16)

        def copy(k, a, slot, part, to):
            ref = outs[a].at[(slot,) + part]
            return pltpu.make_async_remote_copy(
                src_ref=ref, dst_ref=ref, send_sem=send_sems.at[k * n + a], recv_sem=recv_sems.at[k * n + a],
                device_id=to, device_id_type=MESH)

        half = [_shard_half(a, c) for a in range(n)]
        other = [_shard_half(a, 1 - c) for a in range(n)]
        first = [copy(k, a, me, half[a], (px, py, c)) for k, (px, py) in enumerate(chips) for a in range(n)]
        for cp in first:
            cp.start()
        passed = []
        for k, (px, py) in enumerate(chips):
            for a in range(n):
                copy(k, a, 2 * px + py, half[a], (x, y, c)).wait_recv()
                cp = copy(3 + k, a, 2 * px + py, half[a], (x, y, 1 - c))
                cp.start()
                passed.append(cp)
        for k, (px, py) in enumerate(chips):
            for a in range(n):
                copy(3 + k, a, 2 * px + py, other[a], (x, y, c)).wait_recv()
        for cp in first + passed:
            cp.wait_send()

        written = []
        for k, r0, at, rows in _w_in_row_pieces():
            written.append((at, at + rows))
            for r in range(0, rows, 2 * LANES):
                m = min(2 * LANES, rows - r)
                for c0 in range(0, D_MODEL, LANES):
                    w_in_p[at + r:at + r + m, c0:c0 + LANES] = outs[0][k, r0 + r:r0 + r + m, c0:c0 + LANES]
        for lo, hi in zip([0] + [w[1] for w in sorted(written)], [w[0] for w in sorted(written)] + [IN_WIDTH_PAD]):
            if lo < hi:
                w_in_p[lo:hi, :] = jnp.zeros((hi - lo, D_MODEL), BF16)

    vmem = pl.BlockSpec(memory_space=pltpu.VMEM)
    return pl.pallas_call(
        body, name="all_gather_weights",
        out_shape=[jax.ShapeDtypeStruct((IN_WIDTH_PAD, D_MODEL), BF16)]
        + [jax.ShapeDtypeStruct((N_SHARD,) + s, BF16) for s in SHARD_SHAPES[1:n]],
        in_specs=[vmem] * n, out_specs=[vmem] * n,
        scratch_shapes=[pltpu.VMEM((N_SHARD,) + SHARD_SHAPES[0], BF16),
                        pltpu.SemaphoreType.DMA((6 * n,)), pltpu.SemaphoreType.DMA((6 * n,))],
        compiler_params=_cp(None, VMEM_LIMIT),
    )(*shards)


def _shard_blocks(a):
    rows, cols = SHARD_SHAPES[a]
    if SHARD_SPLIT_COLS[a]:
        return [(slice(None), slice(c0, c0 + LANES)) for c0 in range(0, cols, LANES)]
    return [(slice(r0, r0 + ROW_CHUNK), slice(None)) for r0 in range(0, rows, ROW_CHUNK)]


def _shard_half_shape(a):
    rows, cols = SHARD_SHAPES[a]
    return (rows, cols // 2) if SHARD_SPLIT_COLS[a] else (rows // 2, cols)


def _shard_half(a, c):
    rows, cols = SHARD_SHAPES[a]
    if SHARD_SPLIT_COLS[a]:
        return slice(None), pl.ds(pl.multiple_of(c * (cols // 2), LANES), cols // 2)
    return pl.ds(pl.multiple_of(c * (rows // 2), ROW_CHUNK), rows // 2), slice(None)


def _shard_chunks(a, c):
    rows, cols = SHARD_SHAPES[a]
    if SHARD_SPLIT_COLS[a]:
        return [((slice(None), pl.ds(c0, LANES)),
                 (slice(None), pl.ds(pl.multiple_of(c * (cols // 2) + c0, LANES), LANES)))
                for c0 in range(0, cols // 2, LANES)]
    return [((pl.ds(r0, ROW_CHUNK), slice(None)),
             (pl.ds(pl.multiple_of(c * (rows // 2) + r0, ROW_CHUNK), ROW_CHUNK), slice(None)))
            for r0 in range(0, rows // 2, ROW_CHUNK)]


def _reduce_gradients(grads, small_rows):
    n = len(grads)
    n_small = len(small_rows)
    pieces = _w_in_row_pieces()
    order = sorted(range(n), key=lambda a: SHARD_SHAPES[a][0] * SHARD_SHAPES[a][1])

    def body(*refs):
        g_hbm, rows_in = refs[:n], refs[n:n + n_small]
        outs, small_sum = refs[n + n_small:2 * n + n_small], refs[2 * n + n_small]
        scratch = refs[2 * n + n_small + 1:]
        stage, got, sums, others = (scratch[i * n:(i + 1) * n] for i in range(4))
        sm, smalls, send_sems, recv_sems, local_sems = scratch[4 * n:]
        swap_sem, chip_sem, join_sem, small_sem = 0, n, 4 * n, 5 * n
        x, y, c = _position()
        me = 4 * x + 2 * y + c
        chips = [(1 - x, y), (x, 1 - y), (1 - x, 1 - y)]
        sm[...] = jnp.zeros_like(sm)
        for i, row in enumerate(rows_in):
            sm[i:i + 1, 0:row.shape[1]] = row[...]
        loads = [[pltpu.make_async_copy(g_hbm[0].at[pl.ds(src, rows)], stage[0].at[k, pl.ds(dst, rows)],
                                        local_sems.at[n + i])
                  for i, (k, dst, src, rows) in enumerate(pieces)]]
        loads += [[pltpu.make_async_copy(g_hbm[a], stage[a], local_sems.at[a])] for a in range(1, n)]
        for a in order:
            for ld in loads[a]:
                ld.start()
        smalls[me] = sm[...]
        small_sends = []
        for rel in range(1, 8):
            px = 1 - x if rel // 4 else x
            py = 1 - y if (rel // 2) % 2 else y
            pc = 1 - c if rel % 2 else c
            cp = pltpu.make_async_remote_copy(
                src_ref=sm, dst_ref=smalls.at[me], send_sem=send_sems.at[small_sem + rel],
                recv_sem=recv_sems.at[small_sem + rel], device_id=(px, py, pc), device_id_type=MESH)
            cp.start()
            small_sends.append((cp, 4 * px + 2 * py + pc))
        swaps = {}
        for a in order:
            for ld in loads[a]:
                ld.wait()
            swaps[a] = pltpu.make_async_remote_copy(
                src_ref=stage[a].at[(slice(None),) + _shard_half(a, 1 - c)], dst_ref=got[a],
                send_sem=send_sems.at[swap_sem + a], recv_sem=recv_sems.at[swap_sem + a],
                device_id=(x, y, 1 - c), device_id_type=MESH)
            swaps[a].start()
        sends = {}
        for a in order:
            swaps[a].wait_recv()
            for k in range(N_SHARD):
                for in_half, in_whole in _shard_chunks(a, c):
                    pair = stage[a][(k,) + in_whole] + got[a][(k,) + in_half]
                    sums[a][(k,) + in_half] = pair.astype(BF16)
            sends[a] = [pltpu.make_async_remote_copy(
                src_ref=sums[a].at[2 * px + py], dst_ref=others[a].at[k], send_sem=send_sems.at[chip_sem + k * n + a],
                recv_sem=recv_sems.at[chip_sem + k * n + a], device_id=(px, py, c), device_id_type=MESH)
                for k, (px, py) in enumerate(chips)]
            for cp in sends[a]:
                cp.start()
        joins = []
        for a in order:
            for cp in sends[a]:
                cp.wait_recv()
            for in_half, in_whole in _shard_chunks(a, c):
                total = sums[a][(2 * x + y,) + in_half].astype(F32)
                for k in range(3):
                    total = total + others[a][(k,) + in_half].astype(F32)
                outs[a][in_whole] = total
            half = outs[a].at[_shard_half(a, c)]
            cp = pltpu.make_async_remote_copy(
                src_ref=half, dst_ref=half, send_sem=send_sems.at[join_sem + a],
                recv_sem=recv_sems.at[join_sem + a], device_id=(x, y, 1 - c), device_id_type=MESH)
            cp.start()
            joins.append(cp)
        for rel, (cp, peer) in enumerate(small_sends, start=1):
            pltpu.make_async_remote_copy(
                src_ref=sm, dst_ref=smalls.at[peer], send_sem=send_sems.at[small_sem + rel],
                recv_sem=recv_sems.at[small_sem + rel], device_id=(x, y, c), device_id_type=MESH).wait_recv()
        total = smalls[0]
        for dev in range(1, 8):
            total = total + smalls[dev]
        small_sum[...] = total
        for a in order:
            other = outs[a].at[_shard_half(a, 1 - c)]
            pltpu.make_async_remote_copy(
                src_ref=other, dst_ref=other, send_sem=send_sems.at[join_sem + a],
                recv_sem=recv_sems.at[join_sem + a], device_id=(x, y, c), device_id_type=MESH).wait_recv()
        for cp in list(swaps.values()) + [cp for a in order for cp in sends[a]] + joins + [cp for cp, _ in small_sends]:
            cp.wait_send()

    vmem = pl.BlockSpec(memory_space=pltpu.VMEM)
    halves = [_shard_half_shape(a) for a in range(n)]
    return pl.pallas_call(
        body, name="reduce_gradients",
        out_shape=[jax.ShapeDtypeStruct(s, F32) for s in SHARD_SHAPES] + [jax.ShapeDtypeStruct((8, D_MODEL), F32)],
        in_specs=[pl.BlockSpec(memory_space=pl.ANY)] * n + [vmem] * n_small, out_specs=[vmem] * (n + 1),
        scratch_shapes=[pltpu.VMEM((N_SHARD,) + s, F32) for s in SHARD_SHAPES]
        + [pltpu.VMEM((N_SHARD,) + s, F32) for s in halves] + [pltpu.VMEM((N_SHARD,) + s, BF16) for s in halves]
        + [pltpu.VMEM((3,) + s, BF16) for s in halves]
        + [pltpu.VMEM((8, D_MODEL), F32), pltpu.VMEM((8, 8, D_MODEL), F32),
           pltpu.SemaphoreType.DMA((5 * n + 8,)), pltpu.SemaphoreType.DMA((5 * n + 8,)),
           pltpu.SemaphoreType.DMA((n + len(pieces),))],
        compiler_params=_cp(None, VMEM_LIMIT),
    )(*grads, *small_rows)


def _proj(x, w_in_p, gq, gkv, wuq_e, wukv, ct, st, cd, sd, w_out):
    seq = x.shape[0]
    tr = 512
    a_out = 3

    def gather_w_out(w_out_ref, all_ref, land, send_sems, recv_sems):
        x_, y_, c = _position()
        me = 2 * x_ + y_
        chips = [(1 - x_, y_), (x_, 1 - y_), (1 - x_, 1 - y_)]
        half, other = _shard_half(a_out, c), _shard_half(a_out, 1 - c)

        def copy(k, slot, part, to):
            ref = land.at[(slot,) + part]
            return pltpu.make_async_remote_copy(
                src_ref=ref, dst_ref=ref, send_sem=send_sems.at[k], recv_sem=recv_sems.at[k],
                device_id=to, device_id_type=MESH)

        first = [copy(k, me, half, (px, py, c)) for k, (px, py) in enumerate(chips)]

        @pl.when(pl.program_id(0) == 0)
        def _():
            for blk in _shard_blocks(a_out):
                land[(me,) + blk] = w_out_ref[blk].astype(BF16)
            for cp in first:
                cp.start()

        @pl.when(pl.program_id(0) == pl.num_programs(0) - 1)
        def _():
            passed = []
            for k, (px, py) in enumerate(chips):
                copy(k, 2 * px + py, half, (x_, y_, c)).wait_recv()
                passed.append(copy(3 + k, 2 * px + py, half, (x_, y_, 1 - c)))
                passed[-1].start()
            for k, (px, py) in enumerate(chips):
                copy(3 + k, 2 * px + py, other, (x_, y_, c)).wait_recv()
            for cp in first + passed:
                cp.wait_send()
            all_ref[...] = land[...]

    def body(x_ref, w_ref, gq_ref, gkv_ref, wuq_ref, wukv_ref, ct_ref, st_ref, cd_ref, sd_ref, w_out_ref,
             cq_ref, ckv_ref, g_ref, qr_ref, kr_ref, vb_ref, q_out, k_out, v_out, w_out_all, land, send_sems, recv_sems):
        gather_w_out(w_out_ref, w_out_all, land, send_sems, recv_sems)
        lane = lax.broadcasted_iota(jnp.int32, (tr, LANES), 1)
        xb = x_ref[...].astype(BF16)
        cq = _dot(xb, w_ref[0:384, :], NT)
        ckv = _dot(xb, w_ref[384:640, :], NT)
        cq_ref[...] = cq
        ckv_ref[...] = ckv
        g_ref[...] = _dot(xb, w_ref[640:1664, :], NT)

        cd_, sd_ = cd_ref[...], sd_ref[...]
        qb = _dot(xb, w_ref[1664:2176, :], NT)
        kb = _dot(xb, w_ref[2176:2688, :], NT)
        for p in range(4):
            cols = slice(LANES * p, LANES * (p + 1))
            t = qb[:, cols]
            qr_ref[:, cols] = (t * cd_ + _dil_rot(t, lane) * sd_) * DIL_SCALE
            t = kb[:, cols]
            kr_ref[:, cols] = t * cd_ + _dil_rot(t, lane) * sd_
        vb_ref[...] = _dot(xb, w_ref[2688:3200, :], NT)

        ct_, st_ = ct_ref[...], st_ref[...]

        def rope(t):
            return t * ct_ + _mla_rot(t, lane) * st_

        _, qn = _rms(cq, gq_ref[...])
        q_all = _dot(qn.astype(BF16), wuq_ref[...])
        for h in range(MLA_HEADS):
            q_out[h] = (rope(q_all[:, LANES * h:LANES * (h + 1)]) * MLA_SCALE).astype(BF16)
        _, kvn = _rms(ckv, gkv_ref[...])
        kv_all = _dot(kvn.astype(BF16), wukv_ref[...])
        kpe = rope(_dot(xb, w_ref[3200:3328, :], NT))
        for h in range(MLA_HEADS):
            kv_h = kv_all[:, LANES * h:LANES * (h + 1)]
            k_out[h] = jnp.where(lane < 64, kv_h, kpe).astype(BF16)
            if h % 2:
                v = jnp.where(lane >= 64, kv_h, 0.0)
            else:
                v = jnp.where(lane < 64, pltpu.roll(kv_h, 64, 1), 0.0)
            v_out[h] = jnp.where(lane == ONES_LANE[h % 2], 1.0, v).astype(BF16)

    row = lambda w: pl.BlockSpec((tr, w), lambda i: (i, 0))
    full = lambda a: pl.BlockSpec(a.shape, lambda i: (0,) * a.ndim)
    head = pl.BlockSpec((MLA_HEADS, tr, LANES), lambda i: (0, i, 0))
    widths = (Q_LORA, KV_LORA, D_MODEL, 512, 512, 512)
    gathered = (N_SHARD,) + SHARD_SHAPES[a_out]
    return pl.pallas_call(
        body, name="proj", grid=(seq // tr,),
        in_specs=[row(D_MODEL), full(w_in_p), full(gq), full(gkv), full(wuq_e), full(wukv)] + [row(LANES)] * 4
        + [full(w_out)],
        out_specs=[row(w) for w in widths] + [head] * 3 + [pl.BlockSpec(gathered, lambda i: (0, 0, 0))],
        out_shape=[jax.ShapeDtypeStruct((seq, w), F32) for w in widths]
        + [jax.ShapeDtypeStruct((MLA_HEADS, seq, LANES), BF16)] * 3 + [jax.ShapeDtypeStruct(gathered, BF16)],
        scratch_shapes=[pltpu.VMEM(gathered, BF16), pltpu.SemaphoreType.DMA((6,)), pltpu.SemaphoreType.DMA((6,))],
        compiler_params=_cp(("arbitrary",), VMEM_LIMIT),
    )(x, w_in_p, gq, gkv, wuq_e, wukv, ct, st, cd, sd, w_out)


def _mla_rot(t, lane):
    return jnp.where(lane < 80, pltpu.roll(t, 112, 1), pltpu.roll(t, 16, 1))


def _dil_rot(t, lane):
    return jnp.where(lane % 64 < 8, pltpu.roll(t, 120, 1), pltpu.roll(t, 8, 1))


def _rms(c, g):
    r = lax.rsqrt(jnp.mean(c * c, axis=-1, keepdims=True) + RMS_EPS)
    return r, c * r * g


def _mla_fwd(q, k, v):
    seq = q.shape[1]
    tq = 512
    nq = seq // tq

    def body(q_ref, k_ref, v_ref, o_ref, lse_ref, m_s, acc_s, s_buf):
        i = pl.program_id(1)
        row = lax.broadcasted_iota(jnp.int32, (tq, tq), 0)
        col = lax.broadcasted_iota(jnp.int32, (tq, tq), 1)
        lane = lax.broadcasted_iota(jnp.int32, (tq, LANES), 1)
        m_s[...] = jnp.full((2, tq, LANES), NEG, F32)
        acc_s[...] = jnp.zeros((2, tq, LANES), F32)

        def block(j):
            return pl.ds(pl.multiple_of(j * tq, tq), tq)

        def scores(hh, j):
            return _dot(q_ref[hh], k_ref[hh, block(j), :], NT)

        def consume(hh, j, s):
            m_prev = m_s[hh]
            m_new = jnp.maximum(m_prev, jnp.max(s, axis=1, keepdims=True))
            p = jnp.exp(s - m_new[:, :1])
            acc_s[hh] = jnp.exp(m_prev - m_new) * acc_s[hh] + _dot(p.astype(BF16), v_ref[hh, block(j), :])
            m_s[hh] = m_new

        for hh in range(2):
            s_buf[0, hh] = scores(hh, 0)

        def full_step(j, carry):
            slot = j & 1
            for hh in range(2):
                s = s_buf[slot, hh]
                s_buf[1 - slot, hh] = scores(hh, j + 1)
                consume(hh, j, s)
            return carry

        lax.fori_loop(0, i, full_step, 0)
        total = jnp.zeros((tq, LANES), F32)
        for hh in range(2):
            consume(hh, i, jnp.where(col <= row, s_buf[i & 1, hh], NEG))
            acc = acc_s[hh]
            l = acc[:, ONES_LANE[hh]:ONES_LANE[hh] + 1]
            mine = (lane >= 64) if hh else (lane < 64)
            total = total + jnp.where(mine, acc / l, 0.0)
            lse_ref[hh] = m_s[hh] + jnp.log(l)
        o_ref[...] = total

    kv_spec = pl.BlockSpec((2, seq, LANES), lambda p, i: (p, 0, 0))
    return pl.pallas_call(
        body, name="mla_fwd", grid=(MLA_HEADS // 2, nq),
        in_specs=[pl.BlockSpec((2, tq, LANES), lambda p, i: (p, i, 0)), kv_spec, kv_spec],
        out_specs=[pl.BlockSpec((tq, LANES), lambda p, i: (i, p)), pl.BlockSpec((2, tq, LANES), lambda p, i: (p, i, 0))],
        out_shape=[jax.ShapeDtypeStruct((seq, 4 * LANES), F32), jax.ShapeDtypeStruct((MLA_HEADS, seq, LANES), F32)],
        scratch_shapes=[pltpu.VMEM((2, tq, LANES), F32), pltpu.VMEM((2, tq, LANES), F32),
                        pltpu.VMEM((2, 2, tq, tq), F32)],
        compiler_params=_cp(("arbitrary", "arbitrary"), VMEM_LIMIT),
    )(q, k, v)


DIL_Q_FWD = 2 * BLOCK
DIL_Q_BWD = BLOCK


def _dil_tile_index(t, d, seq, nq):
    per_class = seq // (nq * d)
    shift = per_class.bit_length() - 1
    r = t >> shift
    n = t & (per_class - 1)
    start = r + (nq * d) * n
    prev = jnp.maximum(start - BLOCK * d, r)
    if d == 1:
        start = pl.multiple_of(start, nq)
        prev = pl.multiple_of(prev, BLOCK)
    return (n == 0).astype(jnp.int32), start, prev


def _dil_rows(start, d, size):
    return pl.ds(start, size) if d == 1 else pl.ds(start, size, stride=d)


def _dil_bias(nq):
    i = lax.broadcasted_iota(jnp.int32, (2 * nq, BLOCK + nq), 0) % nq
    j = lax.broadcasted_iota(jnp.int32, (2 * nq, BLOCK + nq), 1)
    band = (j >= i) & (j <= i + BLOCK)
    return jnp.where(band, 0.0, NEG), jnp.where(band & (j >= BLOCK), 0.0, NEG)


def _stack_heads(t, lane):
    return jnp.concatenate([jnp.where(lane < 64, t, 0.0), jnp.where(lane >= 64, t, 0.0)], axis=0)


def _unstack_heads(t, lane):
    nq = t.shape[0] // 2
    return jnp.where(lane < 64, t[:nq], t[nq:])


def _dil_fwd(qr, kr, vb):
    seq = qr.shape[0]
    nq = DIL_Q_FWD
    n_tiles = seq // nq
    assert seq % (nq * max(DIL_DILATIONS)) == 0

    def body(q_ref, k_ref, v_ref, o_ref, lse_ref, m_s, l_s, n_s, bias_s):
        lane = lax.broadcasted_iota(jnp.int32, (nq, LANES), 1)
        bias_s[0], bias_s[1] = _dil_bias(nq)
        for bi, d in enumerate(DIL_DILATIONS):

            def tile(t, carry, d=d, bi=bi):
                first, start, prev = _dil_tile_index(t, d, seq, nq)
                rows, prows = _dil_rows(start, d, nq), _dil_rows(prev, d, BLOCK)
                qst = _stack_heads(q_ref[rows, :], lane).astype(BF16)
                if seq == nq * d:
                    kcat, vcat = k_ref[rows, :].astype(BF16), v_ref[rows, :].astype(BF16)
                    s = _dot(qst, kcat, NT) + bias_s[1, :, BLOCK:]
                else:
                    kcat = jnp.concatenate([k_ref[prows, :], k_ref[rows, :]], axis=0).astype(BF16)
                    vcat = jnp.concatenate([v_ref[prows, :], v_ref[rows, :]], axis=0).astype(BF16)
                    s = _dot(qst, kcat, NT) + bias_s[first]
                m = jnp.max(s, axis=1, keepdims=True)
                p = jnp.exp(s - m)
                l2 = _unstack_heads(jnp.sum(p, axis=1, keepdims=True) + jnp.zeros((2 * nq, LANES), F32), lane)
                m2 = _unstack_heads(m + jnp.zeros((2 * nq, LANES), F32), lane)
                num2 = _unstack_heads(_dot(p.astype(BF16), vcat), lane)
                if bi == 0:
                    m_s[rows, :] = m2
                    l_s[rows, :] = l2
                    n_s[rows, :] = num2
                else:
                    m_old = m_s[rows, :]
                    m_new = jnp.maximum(m_old, m2)
                    a = jnp.exp(m_old - m_new)
                    b = jnp.exp(m2 - m_new)
                    m_s[rows, :] = m_new
                    l_s[rows, :] = a * l_s[rows, :] + b * l2
                    n_s[rows, :] = a * n_s[rows, :] + b * num2
                return carry

            lax.fori_loop(0, n_tiles, tile, 0, unroll=8)
        o_ref[...] = n_s[...] / l_s[...]
        lse_ref[...] = m_s[...] + jnp.log(l_s[...])

    col = lambda off: pl.BlockSpec((seq, LANES), lambda p: (0, p + off))
    return pl.pallas_call(
        body, name="dil_fwd", grid=(4,),
        in_specs=[col(0), col(0), col(0)],
        out_specs=[col(0), pl.BlockSpec((None, seq, LANES), lambda p: (p, 0, 0))],
        out_shape=[jax.ShapeDtypeStruct((seq, 4 * LANES), F32), jax.ShapeDtypeStruct((4, seq, LANES), F32)],
        scratch_shapes=[pltpu.VMEM((seq, LANES), F32)] * 3 + [pltpu.VMEM((2, 2 * nq, BLOCK + nq), F32)],
        compiler_params=_cp(("arbitrary",), VMEM_LIMIT),
    )(qr, kr, vb)


def _post(x, o_a, o_b, gates, w_out, ln_g, ln_b, target):
    seq = x.shape[0]
    tr = 512

    def body(x_ref, oa_ref, ob_ref, g_ref, w_ref, lg_ref, lb_ref, t_ref,
             dz_ref, do_ref, dg_ref, dw_ref, dlg_ref, dlb_ref, loss_ref):
        @pl.when(pl.program_id(0) == 0)
        def _():
            dw_ref[...] = jnp.zeros_like(dw_ref)
            dlg_ref[...] = jnp.zeros_like(dlg_ref)
            dlb_ref[...] = jnp.zeros_like(dlb_ref)
            loss_ref[...] = jnp.zeros_like(loss_ref)

        g = g_ref[...]
        sg = jax.nn.sigmoid(g)
        silu = g * sg
        o = jnp.concatenate([oa_ref[...], ob_ref[...]], axis=1)
        mixb = (o * silu).astype(BF16)
        w = w_ref[...]
        z = ALPHA * x_ref[...] + _dot(mixb, w)
        mu = jnp.mean(z, axis=-1, keepdims=True)
        zc = z - mu
        rstd = lax.rsqrt(jnp.mean(zc * zc, axis=-1, keepdims=True) + LN_EPS)
        xhat = zc * rstd
        lg = lg_ref[...]
        err = xhat * lg + lb_ref[...] - t_ref[...]
        loss_ref[...] += jnp.sum(err * err) * (0.5 / D_MODEL)
        dy = err * (1.0 / D_MODEL)
        dlg_ref[...] += jnp.sum(dy * xhat, axis=0, keepdims=True)
        dlb_ref[...] += jnp.sum(dy, axis=0, keepdims=True)
        dxh = dy * lg
        dz = rstd * (dxh - jnp.mean(dxh, axis=-1, keepdims=True) - xhat * jnp.mean(dxh * xhat, axis=-1, keepdims=True))
        dz_ref[...] = dz
        dzb = dz.astype(BF16)
        dmix = _dot(dzb, w, NT)
        do_ref[...] = dmix * silu
        dg_ref[...] = (dmix * o * (sg * (1.0 + g * (1.0 - sg)))).astype(BF16)
        dw_ref[...] += _dot(mixb, dzb, TN)

    row = lambda w: pl.BlockSpec((tr, w), lambda i: (i, 0))
    full = lambda s: pl.BlockSpec(s, lambda i: (0, 0))
    return pl.pallas_call(
        body, name="post", grid=(seq // tr,),
        in_specs=[row(D_MODEL), row(512), row(512), row(D_MODEL), full((D_MODEL, D_MODEL)), full((1, D_MODEL)),
                  full((1, D_MODEL)), row(D_MODEL)],
        out_specs=[row(D_MODEL), row(D_MODEL), row(D_MODEL), full((D_MODEL, D_MODEL)), full((1, D_MODEL)),
                   full((1, D_MODEL)), full((1, LANES))],
        out_shape=[jax.ShapeDtypeStruct((seq, D_MODEL), F32), jax.ShapeDtypeStruct((seq, D_MODEL), F32),
                   jax.ShapeDtypeStruct((seq, D_MODEL), BF16), jax.ShapeDtypeStruct((D_MODEL, D_MODEL), F32),
                   jax.ShapeDtypeStruct((1, D_MODEL), F32), jax.ShapeDtypeStruct((1, D_MODEL), F32),
                   jax.ShapeDtypeStruct((1, LANES), F32)],
        compiler_params=_cp(("arbitrary",), VMEM_LIMIT),
    )(x, o_a, o_b, gates, w_out, ln_g, ln_b, target)


def _mla_bwd(q, k, v, d_o, o, lse):
    seq = q.shape[1]
    tq = 512
    nq = seq // tq

    def body(q_ref, k_ref, v_ref, do_ref, o_ref, lse_ref, dq_ref, dk_ref, dv_ref, d_s, lse_s, dk_s, dv_s, v_s, kt_s, dqt_s):
        j = nq - 1 - pl.program_id(1)
        lane = lax.broadcasted_iota(jnp.int32, (tq, LANES), 1)
        row = lax.broadcasted_iota(jnp.int32, (tq, tq), 0)
        col = lax.broadcasted_iota(jnp.int32, (tq, tq), 1)

        @pl.when(pl.program_id(1) == 0)
        def _():
            dqt_s[...] = jnp.zeros_like(dqt_s)

            def rowsum(i, carry):
                rows = pl.ds(pl.multiple_of(i * tq, tq), tq)
                prod = do_ref[rows, :] * o_ref[rows, :]
                for hh in range(2):
                    mine = (lane >= 64) if hh else (lane < 64)
                    total = jnp.sum(jnp.where(mine, prod, 0.0), axis=1, keepdims=True)
                    d_s[hh, i] = jnp.transpose(total + jnp.zeros((tq, LANES), F32))[:8]
                    lse_s[hh, i] = jnp.transpose(lse_ref[hh, rows, :])[:8]
                return carry

            lax.fori_loop(0, nq, rowsum, 0)

        dk_s[...] = jnp.zeros_like(dk_s)
        dv_s[...] = jnp.zeros_like(dv_s)
        for hh in range(2):
            v_s[hh] = jnp.where(lane == ONES_LANE[hh], 0.0, v_ref[hh].astype(F32)).astype(BF16)
            kt_s[hh] = jnp.transpose(k_ref[hh].astype(F32)).astype(BF16)

        def step(i, masked):
            rows = pl.ds(pl.multiple_of(i * tq, tq), tq)
            dob = do_ref[rows, :].astype(BF16)
            for hh in range(2):
                qb, kb, vb = q_ref[hh, rows, :], k_ref[hh], v_s[hh]
                p = jnp.exp(_dot(kb, qb, NT) - lse_s[hh, i][:1])
                if masked:
                    p = jnp.where(row <= col, p, 0.0)
                dv_s[hh] += _dot(p.astype(BF16), dob)
                ds = (p * (_dot(vb, dob, NT) - d_s[hh, i][:1])).astype(BF16)
                dk_s[hh] += _dot(ds, qb)
                dqt_s[hh, i] += _dot(kt_s[hh], ds)

        def full_step(i, carry):
            step(i, False)
            return carry

        step(j, True)
        lax.fori_loop(j + 1, nq, full_step, 0)
        dk_ref[...] = dk_s[...]
        dv_ref[...] = dv_s[...]

        @pl.when(pl.program_id(1) == nq - 1)
        def _():
            def untranspose(i, carry):
                rows = pl.ds(pl.multiple_of(i * tq, tq), tq)
                for hh in range(2):
                    dq_ref[hh, rows, :] = jnp.transpose(dqt_s[hh, i])
                return carry

            lax.fori_loop(0, nq, untranspose, 0)

    whole = pl.BlockSpec((2, seq, LANES), lambda p, j: (p, 0, 0))
    blk = pl.BlockSpec((2, tq, LANES), lambda p, j: (p, nq - 1 - j, 0))
    pair = pl.BlockSpec((seq, LANES), lambda p, j: (0, p))
    shape = jax.ShapeDtypeStruct((MLA_HEADS, seq, LANES), F32)
    return pl.pallas_call(
        body, name="mla_bwd", grid=(MLA_HEADS // 2, nq),
        in_specs=[whole, blk, blk, pair, pair, whole],
        out_specs=[whole, blk, blk], out_shape=[shape] * 3,
        scratch_shapes=[pltpu.VMEM((2, nq, 8, tq), F32), pltpu.VMEM((2, nq, 8, tq), F32),
                        pltpu.VMEM((2, tq, LANES), F32), pltpu.VMEM((2, tq, LANES), F32),
                        pltpu.VMEM((2, tq, LANES), BF16), pltpu.VMEM((2, LANES, tq), BF16),
                        pltpu.VMEM((2, nq, LANES, tq), F32)],
        compiler_params=_cp(("arbitrary", "arbitrary"), VMEM_LIMIT),
    )(q, k, v, d_o, o, lse)


def _dil_bwd(qr, kr, vb, d_o, o, lse):
    seq = qr.shape[0]
    nq = DIL_Q_BWD
    n_tiles = seq // nq
    chunk = 512

    def body(q_ref, k_ref, v_ref, do_ref, o_ref, lse_ref, dq_ref, dk_ref, dv_ref, d_s, dq_s, dk_s, dv_s, bias_s):
        lane = lax.broadcasted_iota(jnp.int32, (nq, LANES), 1)
        lanec = lax.broadcasted_iota(jnp.int32, (chunk, LANES), 1)
        bias_s[0], bias_s[1] = [b[:nq] for b in _dil_bias(nq)]

        def rowsum(i, carry):
            rows = pl.ds(pl.multiple_of(i * chunk, chunk), chunk)
            prod = do_ref[rows, :] * o_ref[rows, :]
            lo = jnp.sum(jnp.where(lanec < 64, prod, 0.0), axis=1, keepdims=True)
            hi = jnp.sum(jnp.where(lanec >= 64, prod, 0.0), axis=1, keepdims=True)
            d_s[rows, :] = jnp.where(lanec < 64, lo, hi)
            return carry

        lax.fori_loop(0, seq // chunk, rowsum, 0)
        dq_s[...] = jnp.zeros_like(dq_s)
        dk_s[...] = jnp.zeros_like(dk_s)
        dv_s[...] = jnp.zeros_like(dv_s)
        for d in DIL_DILATIONS:

            def tile(start, prev, first, d=d):
                rows = _dil_rows(start, d, nq)
                q_t, do_t = q_ref[rows, :], do_ref[rows, :]
                lse_t, d_t = lse_ref[rows, :], d_s[rows, :]
                if prev is None:
                    kcat, vcat = k_ref[rows, :].astype(BF16), v_ref[rows, :].astype(BF16)
                    bias = bias_s[1, :, BLOCK:]
                else:
                    prows = _dil_rows(prev, d, BLOCK)
                    kcat = jnp.concatenate([k_ref[prows, :], k_ref[rows, :]], axis=0).astype(BF16)
                    vcat = jnp.concatenate([v_ref[prows, :], v_ref[rows, :]], axis=0).astype(BF16)
                    bias = bias_s[first]
                dq_t = jnp.zeros((nq, LANES), F32)
                dkcat = jnp.zeros((kcat.shape[0], LANES), F32)
                dvcat = jnp.zeros((kcat.shape[0], LANES), F32)
                for hh in range(2):
                    mine = (lane >= 64) if hh else (lane < 64)
                    c0 = 64 * hh
                    qh = jnp.where(mine, q_t, 0.0).astype(BF16)
                    doh = jnp.where(mine, do_t, 0.0).astype(BF16)
                    p = jnp.exp(_dot(qh, kcat, NT) + bias - lse_t[:, c0:c0 + 1])
                    dvcat = dvcat + _dot(p.astype(BF16), doh, TN)
                    dp = _dot(doh, vcat, NT)
                    ds = (p * (dp - d_t[:, c0:c0 + 1])).astype(BF16)
                    dq_t = dq_t + jnp.where(mine, _dot(ds, kcat), 0.0)
                    dkcat = dkcat + _dot(ds, qh, TN)
                dq_s[rows, :] += dq_t
                if prev is not None:
                    dk_s[prows, :] += dkcat[:BLOCK]
                    dv_s[prows, :] += dvcat[:BLOCK]
                dk_s[rows, :] += dkcat[-nq:]
                dv_s[rows, :] += dvcat[-nq:]

            if seq == 2 * nq * d:

                def class_tiles(r, carry, d=d):
                    tile(r, None, 1)
                    tile(r + nq * d, r, 0)
                    return carry

                lax.fori_loop(0, d, class_tiles, 0, unroll=8)
            else:

                def any_tile(t, carry, d=d):
                    first, start, prev = _dil_tile_index(t, d, seq, nq)
                    tile(start, prev, first)
                    return carry

                lax.fori_loop(0, n_tiles, any_tile, 0, unroll=16)
        dq_ref[...] = dq_s[...].astype(BF16)
        dk_ref[...] = dk_s[...].astype(BF16)
        dv_ref[...] = dv_s[...].astype(BF16)

    col = lambda off: pl.BlockSpec((seq, LANES), lambda p: (0, p + off))
    shape = jax.ShapeDtypeStruct((seq, 4 * LANES), BF16)
    return pl.pallas_call(
        body, name="dil_bwd", grid=(4,),
        in_specs=[col(0), col(0), col(0), col(4), col(0), pl.BlockSpec((None, seq, LANES), lambda p: (p, 0, 0))],
        out_specs=[col(0)] * 3, out_shape=[shape] * 3,
        scratch_shapes=[pltpu.VMEM((seq, LANES), F32)] * 4 + [pltpu.VMEM((2, nq, BLOCK + nq), F32)],
        compiler_params=_cp(("arbitrary",), VMEM_LIMIT),
    )(qr, kr, vb, d_o, o, lse)


def _in_bwd(dz, cq, ckv, gq, gkv, wuq_e, wukv, ct, st, dq, dk, dv, dgates, dqr, dkr, dvb, cd, sd, w_in_p):
    seq = dz.shape[0]
    tr = 512

    def body(dz_ref, cq_ref, ckv_ref, gq_ref, gkv_ref, wuq_ref, wukv_ref, ct_ref, st_ref, dq_ref, dk_ref, dv_ref,
             dg_ref, dqr_ref, dkr_ref, dvb_ref, cd_ref, sd_ref, w_ref,
             gx_ref, dh_ref, dwuq_ref, dwukv_ref, dgq_ref, dgkv_ref):
        @pl.when(pl.program_id(0) == 0)
        def _():
            dwuq_ref[...] = jnp.zeros_like(dwuq_ref)
            dwukv_ref[...] = jnp.zeros_like(dwukv_ref)
            dgq_ref[...] = jnp.zeros_like(dgq_ref)
            dgkv_ref[...] = jnp.zeros_like(dgkv_ref)

        lane = lax.broadcasted_iota(jnp.int32, (tr, LANES), 1)
        rope_lanes = jnp.logical_and(lane >= 64, lane < 96)
        ct_, st_ = ct_ref[...], st_ref[...]

        def mla_rope_t(g):
            return ct_ * g + jnp.where(rope_lanes, _mla_rot(st_ * g, lane), 0.0)

        def norm_bwd(c, g, dn, dg_ref):
            r, _ = _rms(c, g)
            u = dn * g
            dg_ref[...] += jnp.sum(dn * c * r, axis=0, keepdims=True)
            return r * u - c * (r * r * r) * jnp.mean(u * c, axis=-1, keepdims=True)

        c, g = cq_ref[...], gq_ref[...]
        _, qn = _rms(c, g)
        dq_all = jnp.concatenate([mla_rope_t(dq_ref[h] * MLA_SCALE) for h in range(MLA_HEADS)], axis=1).astype(BF16)
        dwuq_ref[...] += _dot(qn.astype(BF16), dq_all, TN)
        dcq = norm_bwd(c, g, _dot(dq_all, wuq_ref[...], NT), dgq_ref).astype(BF16)

        c, g = ckv_ref[...], gkv_ref[...]
        _, kvn = _rms(c, g)
        dkpe = jnp.zeros((tr, LANES), F32)
        parts = []
        for h in range(MLA_HEADS):
            dk_h, dv_h = dk_ref[h], dv_ref[h]
            if h % 2 == 0:
                dv_h = pltpu.roll(dv_h, 64, 1)
            parts.append(jnp.where(lane < 64, dk_h, dv_h))
            dkpe = dkpe + jnp.where(rope_lanes, dk_h, 0.0)
        dkv_all = jnp.concatenate(parts, axis=1).astype(BF16)
        dwukv_ref[...] += _dot(kvn.astype(BF16), dkv_all, TN)
        dckv = norm_bwd(c, g, _dot(dkv_all, wukv_ref[...], NT), dgkv_ref).astype(BF16)
        dkrope = mla_rope_t(dkpe).astype(BF16)

        rot_lanes = lane % 64 < DIL_ROT
        cd_, sd_ = cd_ref[...], sd_ref[...]

        def dil_rope_t(g):
            return cd_ * g + jnp.where(rot_lanes, _dil_rot(sd_ * g, lane), 0.0)

        dqb = [dil_rope_t(dqr_ref[:, LANES * p:LANES * (p + 1)].astype(F32) * DIL_SCALE).astype(BF16) for p in range(4)]
        dkb = [dil_rope_t(dkr_ref[:, LANES * p:LANES * (p + 1)].astype(F32)).astype(BF16) for p in range(4)]
        dh = jnp.concatenate([dcq, dckv, dg_ref[...]] + dqb + dkb + [dvb_ref[...], dkrope], axis=1)
        dh_ref[...] = dh
        gx_ref[...] = ALPHA * dz_ref[...] + _dot(dh, w_ref[...])

    row = lambda w: pl.BlockSpec((tr, w), lambda i: (i, 0))
    full = lambda a: pl.BlockSpec(a.shape, lambda i: (0,) * a.ndim)
    head = pl.BlockSpec((MLA_HEADS, tr, LANES), lambda i: (0, i, 0))
    return pl.pallas_call(
        body, name="in_bwd", grid=(seq // tr,),
        in_specs=[row(D_MODEL), row(Q_LORA), row(KV_LORA), full(gq), full(gkv), full(wuq_e), full(wukv), row(LANES),
                  row(LANES), head, head, head, row(D_MODEL), row(512), row(512), row(512), row(LANES), row(LANES),
                  full(w_in_p)],
        out_specs=[row(D_MODEL), row(IN_WIDTH_PAD), full(wuq_e), full(wukv), full(gq), full(gkv)],
        out_shape=[jax.ShapeDtypeStruct((seq, D_MODEL), F32), jax.ShapeDtypeStruct((seq, IN_WIDTH_PAD), BF16),
                   jax.ShapeDtypeStruct(wuq_e.shape, F32), jax.ShapeDtypeStruct(wukv.shape, F32),
                   jax.ShapeDtypeStruct(gq.shape, F32), jax.ShapeDtypeStruct(gkv.shape, F32)],
        compiler_params=_cp(("arbitrary",), VMEM_LIMIT),
    )(dz, cq, ckv, gq, gkv, wuq_e, wukv, ct, st, dq, dk, dv, dgates, dqr, dkr, dvb, cd, sd, w_in_p)


def _dw_in(x, dh):
    seq = dh.shape[0]
    tk = 512
    tn = IN_WIDTH_PAD // 2

    def body(x_ref, dh_ref, o_ref):
        part = lambda: _dot(dh_ref[...], x_ref[...].astype(BF16), TN)

        @pl.when(pl.program_id(1) == 0)
        def _():
            o_ref[...] = part()

        @pl.when(pl.program_id(1) > 0)
        def _():
            o_ref[...] += part()

    return pl.pallas_call(
        body, name="dw_in", grid=(2, seq // tk),
        in_specs=[pl.BlockSpec((tk, D_MODEL), lambda n, k: (k, 0)), pl.BlockSpec((tk, tn), lambda n, k: (k, n))],
        out_specs=pl.BlockSpec((tn, D_MODEL), lambda n, k: (n, 0)),
        out_shape=jax.ShapeDtypeStruct((IN_WIDTH_PAD, D_MODEL), F32),
        compiler_params=_cp(("arbitrary", "arbitrary"), VMEM_LIMIT),
    )(x, dh)


def _adam_update(w, g, m, v):
    nm = ADAM_B1 * m + (1.0 - ADAM_B1) * g
    nv = ADAM_B2 * v + (1.0 - ADAM_B2) * jnp.square(g)
    m_hat = nm / (1.0 - ADAM_B1 ** ADAM_STEP)
    v_hat = nv / (1.0 - ADAM_B2 ** ADAM_STEP)
    return -ADAM_LR * (m_hat / (jnp.sqrt(v_hat) + ADAM_EPS) + ADAM_WD * w), nm, nv


def _adamw(w, g, m, v, name):
    rows, cols = w.shape
    tc = 256 if cols % 256 == 0 and rows * cols > 2 ** 18 else cols

    def body(w_ref, g_ref, m_ref, v_ref, d_ref, nm_ref, nv_ref):
        d_ref[...], nm_ref[...], nv_ref[...] = _adam_update(w_ref[...], g_ref[...], m_ref[...], v_ref[...])

    spec = pl.BlockSpec((rows, tc), lambda i: (0, i))
    return pl.pallas_call(
        body, name=name, grid=(cols // tc,), in_specs=[spec] * 4, out_specs=[spec] * 3,
        out_shape=[jax.ShapeDtypeStruct(w.shape, F32)] * 3, compiler_params=_cp(("arbitrary",)),
    )(w, g, m, v)


def _adamw_vectors(small_sum, ws, ms, vs):
    k = len(ws)
    sizes = [w.shape[0] for w in ws]

    def body(s_ref, *refs):
        ins, outs = refs[:3 * k], refs[3 * k:]
        for i, size in enumerate(sizes):
            g = s_ref[i, 0:size]
            outs[i][...] = g
            outs[k + i][...], outs[2 * k + i][...], outs[3 * k + i][...] = _adam_update(
                ins[i][...], g, ins[k + i][...], ins[2 * k + i][...])

    out = pl.pallas_call(
        body, name="adamw_vectors", out_shape=[jax.ShapeDtypeStruct((size,), F32) for size in sizes] * 4,
    )(small_sum, *ws, *ms, *vs)
    return [out[k * j:k * (j + 1)] for j in range(4)]


def _local_step(x2, target, w_in_p, w_uq_f, wukv_f, w_out, q_norm_g, kv_norm_g, ln_g, ln_b):
    seq = x2.shape[0]
    wuq_e = jnp.pad(w_uq_f.reshape(Q_LORA, MLA_HEADS, 96), ((0, 0), (0, 0), (0, 32))).reshape(Q_LORA, MLA_HEADS * LANES)
    ct, st, cd, sd = _rope_tables(seq)
    gq = q_norm_g.reshape(1, Q_LORA)
    gkv = kv_norm_g.reshape(1, KV_LORA)

    cq, ckv, gates, qr, krot, vb, q_e, k_e, v_e, g_w_out = _proj(
        x2, w_in_p, gq, gkv, wuq_e, wukv_f, ct, st, cd, sd, w_out)
    w_out_f = g_w_out.reshape(D_MODEL, D_MODEL)
    o_a, lse_a = _mla_fwd(q_e, k_e, v_e)
    o_b, lse_b = _dil_fwd(qr, krot, vb)

    dz, d_o, d_gates, dw_out, dln_g, dln_b, loss_part = _post(
        x2, o_a, o_b, gates, w_out_f, ln_g.reshape(1, D_MODEL), ln_b.reshape(1, D_MODEL), target)
    dq_e, dk_e, dv_e = _mla_bwd(q_e, k_e, v_e, d_o, o_a, lse_a)
    dqr, dkr, dvb = _dil_bwd(qr, krot, vb, d_o, o_b, lse_b)
    grad_x, dh, dwuq_e, dwukv, dgq, dgkv = _in_bwd(
        dz, cq, ckv, gq, gkv, wuq_e, wukv_f, ct, st, dq_e, dk_e, dv_e, d_gates, dqr, dkr, dvb, cd, sd, w_in_p)
    dw_in = _dw_in(x2, dh)
    dw_uq = dwuq_e.reshape(Q_LORA, MLA_HEADS, LANES)[:, :, :96].reshape(Q_LORA, MLA_HEADS * 96)
    return loss_part, grad_x, dw_in, dw_uq, dwukv, dw_out, dgq, dgkv, dln_g, dln_b


def kernel(x, w_in, q_norm_g, kv_norm_g, w_uq, w_ukv, w_out, ln_g, ln_b, loss_target, m_w_in, m_q_norm_g, m_kv_norm_g, m_w_uq, m_w_ukv, m_w_out, m_ln_g, m_ln_b, v_w_in, v_q_norm_g, v_kv_norm_g, v_w_uq, v_w_ukv, v_w_out, v_ln_g, v_ln_b):
    seq = x.shape[1]
    x2 = x.reshape(seq, D_MODEL)
    target = loss_target.reshape(seq, D_MODEL)

    g_w_in, g_w_uq, g_w_ukv = _all_gather_weights([w_in.T, w_uq, w_ukv])
    by_cols = lambda g: jnp.concatenate([g[j] for j in range(N_SHARD)], axis=1)
    loss_part, grad_x, dw_in, dw_uq, dwukv, dw_out, dgq, dgkv, dln_g, dln_b = _local_step(
        x2, target, g_w_in, by_cols(g_w_uq), by_cols(g_w_ukv), w_out, q_norm_g, kv_norm_g, ln_g, ln_b)

    to_shards = lambda d: d.reshape(d.shape[0], N_SHARD, d.shape[1] // N_SHARD).transpose(1, 0, 2)
    grads = [dw_in, to_shards(dw_uq), to_shards(dwukv), dw_out.reshape(N_SHARD, 256, D_MODEL)]
    g_in_t, g_uq, g_ukv, g_out, small_sum = _reduce_gradients(grads, [dgq, dgkv, dln_g, dln_b, loss_part])
    g_in = g_in_t.T
    loss = small_sum[4, 0]

    big = [[o.T for o in _adamw(w.T, g.T, m.T, v.T, name)] for w, g, m, v, name in (
        (w_in, g_in, m_w_in, v_w_in, "adamw_w_in"), (w_uq, g_uq, m_w_uq, v_w_uq, "adamw_w_uq"))]
    big += [_adamw(w, g, m, v, name) for w, g, m, v, name in (
        (w_ukv, g_ukv, m_w_ukv, v_w_ukv, "adamw_w_ukv"), (w_out, g_out, m_w_out, v_w_out, "adamw_w_out"))]
    vec_g, vec_delta, vec_m, vec_v = _adamw_vectors(
        small_sum, [q_norm_g, kv_norm_g, ln_g, ln_b], [m_q_norm_g, m_kv_norm_g, m_ln_g, m_ln_b],
        [v_q_norm_g, v_kv_norm_g, v_ln_g, v_ln_b])

    def ordered(bigs, vecs):
        return [bigs[0], vecs[0], vecs[1], bigs[1], bigs[2], bigs[3], vecs[2], vecs[3]]

    grads_out = ordered([g_in, g_uq, g_ukv, g_out], vec_g)
    deltas = ordered([b[0] for b in big], vec_delta)
    new_m = ordered([b[1] for b in big], vec_m)
    new_v = ordered([b[2] for b in big], vec_v)
    return (loss, grad_x.reshape(x.shape), *grads_out, *deltas, *new_m, *new_v)
```
